```python
import jax, jax.numpy as jnp
from jax import lax
import numpy as np

D_MODEL = 1024
BATCH = 8
SEQ = 8192
DEPTH = 1

MLA_HEADS = 8
MLA_Q_RANK = 256
MLA_KV_RANK = 128
MLA_NOPE_DIM = 64
MLA_ROPE_DIM = 32
MLA_V_DIM = 64
MLA_QK_DIM = MLA_NOPE_DIM + MLA_ROPE_DIM
Q_BLOCK = 128
RET_HEADS = 8
RET_QK_DIM = D_MODEL // (2 * RET_HEADS)
RET_V_DIM = 2 * RET_QK_DIM
RET_CHUNK = 128
FFN_HIDDEN = -(-8 * D_MODEL // (3 * 256)) * 256
ROPE_THETA = 10000.0
EPS = 1e-6

IN_SPLITS = [
    MLA_Q_RANK,
    MLA_KV_RANK,
    MLA_ROPE_DIM,
    RET_HEADS * RET_QK_DIM,
    RET_HEADS * RET_QK_DIM,
    RET_HEADS * RET_V_DIM,
    RET_HEADS * RET_V_DIM,
    2 * D_MODEL,
]
IN_WIDTH = sum(IN_SPLITS)

kernel_name = "hybrid_mla_retention_gated_block"


def _rms(xf):
    return xf * lax.rsqrt(jnp.mean(xf * xf, axis=-1, keepdims=True) + EPS)


def rms_norm(x, g):
    y = _rms(x.astype(jnp.float32)) * g.astype(jnp.float32)
    return y.astype(x.dtype)


def rope(x, positions):
    half = x.shape[-1] // 2
    inv = ROPE_THETA ** (-jnp.arange(half, dtype=jnp.float32) / half)
    ang = positions.astype(jnp.float32)[..., None] * inv
    cos = jnp.cos(ang)[:, :, None, :]
    sin = jnp.sin(ang)[:, :, None, :]
    xf = x.astype(jnp.float32)
    x1, x2 = xf[..., :half], xf[..., half:]
    out = jnp.concatenate([x1 * cos - x2 * sin, x2 * cos + x1 * sin], axis=-1)
    return out.astype(x.dtype)


def mla_attention(c_q, c_kv, k_rope, positions, g_q_a, w_q_b, g_kv_a, w_kv_b, g_qn, g_kn):
    B, S, _ = c_q.shape
    H = MLA_HEADS
    q = (rms_norm(c_q, g_q_a) @ w_q_b).reshape(B, S, H, MLA_QK_DIM)
    kv = (rms_norm(c_kv, g_kv_a) @ w_kv_b).reshape(B, S, H, MLA_NOPE_DIM + MLA_V_DIM)
    k_nope, v = kv[..., :MLA_NOPE_DIM], kv[..., MLA_NOPE_DIM:]
    k_r = jnp.broadcast_to(k_rope[:, :, None, :], (B, S, H, MLA_ROPE_DIM))
    k = jnp.concatenate([k_nope, k_r], axis=-1)
    q = rms_norm(q, g_qn)
    k = rms_norm(k, g_kn)
    q = jnp.concatenate([q[..., :MLA_NOPE_DIM], rope(q[..., MLA_NOPE_DIM:], positions)], axis=-1)
    k = jnp.concatenate([k[..., :MLA_NOPE_DIM], rope(k[..., MLA_NOPE_DIM:], positions)], axis=-1)
    q = q.astype(jnp.float32).transpose(0, 2, 1, 3)
    k = k.astype(jnp.float32).transpose(0, 2, 1, 3)
    v = v.astype(jnp.float32).transpose(0, 2, 1, 3)
    scale = MLA_QK_DIM ** -0.5
    nb = S // Q_BLOCK
    qb = q.reshape(B, H, nb, Q_BLOCK, MLA_QK_DIM).transpose(2, 0, 1, 3, 4)

    def attend(q_blk):
        s = jnp.einsum('bhqd,bhkd->bhqk', q_blk, k) * scale
        p = jax.nn.softmax(s, axis=-1)
        return jnp.einsum('bhqk,bhkv->bhqv', p, v)

    o = lax.map(attend, qb)
    o = o.transpose(1, 0, 3, 2, 4).reshape(B, S, H * MLA_V_DIM)
    return o


def retention_dir(q, k, v, log_gamma, strict):
    B, H, S, dk = q.shape
    dv = v.shape[-1]
    C = RET_CHUNK
    n = S // C
    idx = jnp.arange(C, dtype=jnp.float32)
    diff = idx[:, None] - idx[None, :]
    mask = diff > 0 if strict else diff >= 0
    decay_in = jnp.where(mask, jnp.exp(log_gamma[:, None, None] * jnp.maximum(diff, 0.0)), 0.0)
    q_decay = jnp.exp(log_gamma[:, None] * (idx + 1.0))[..., None]
    k_decay = jnp.exp(log_gamma[:, None] * (C - 1.0 - idx))[..., None]
    chunk_decay = jnp.exp(log_gamma * C)[:, None, None]

    def to_chunks(a):
        return a.reshape(B, H, n, C, a.shape[-1]).transpose(2, 0, 1, 3, 4)

    def step(state, inp):
        qi, ki, vi = inp
        inner = jnp.einsum('bhcd,bhed->bhce', qi, ki) * decay_in
        inner = jnp.einsum('bhce,bhev->bhcv', inner, vi)
        cross = jnp.einsum('bhcd,bhdv->bhcv', qi * q_decay, state)
        new_state = state * chunk_decay + jnp.einsum('bhcd,bhcv->bhdv', ki * k_decay, vi)
        return new_state, inner + cross

    state0 = jnp.zeros((B, H, dk, dv), jnp.float32)
    _, out = lax.scan(step, state0, (to_chunks(q), to_chunks(k), to_chunks(v)))
    return out.transpose(1, 2, 0, 3, 4).reshape(B, H, S, dv)


def bidirectional_retention(q, k, v, decay_fwd, decay_bwd):
    lg_f = -jnp.exp(decay_fwd.astype(jnp.float32))
    lg_b = -jnp.exp(decay_bwd.astype(jnp.float32))
    fwd = retention_dir(q, k, v, lg_f, False)
    flip = lambda a: jnp.flip(a, axis=2)
    bwd = flip(retention_dir(flip(q), flip(k), flip(v), lg_b, True))
    return fwd + bwd


def _fwd_setup_inputs(seed: int = 0) -> dict:
    key = jax.random.key(seed)
    ks = jax.random.split(key, 20)
    f32 = jnp.float32

    def w(k, fan_in, fan_out):
        return jax.random.normal(k, (fan_in, fan_out), f32) * fan_in ** -0.5

    def gain(k, n):
        return 1.0 + 0.02 * jax.random.normal(k, (n,), f32)

    gamma0 = 1.0 - 2.0 ** (-5.0 - jnp.arange(RET_HEADS, dtype=f32))
    decay_base = jnp.log(-jnp.log(gamma0))
    x = jax.random.normal(ks[0], (BATCH, SEQ, D_MODEL), f32)
    positions = (jnp.arange(SEQ, dtype=jnp.int32)[None, :]
                 + jax.random.randint(ks[1], (BATCH, 1), 0, SEQ, dtype=jnp.int32))
    return {
        "x": x,
        "positions": positions,
        "g_mix": gain(ks[2], D_MODEL),
        "w_in": w(ks[3], D_MODEL, IN_WIDTH),
        "g_q_a": gain(ks[4], MLA_Q_RANK),
        "w_q_b": w(ks[5], MLA_Q_RANK, MLA_HEADS * MLA_QK_DIM),
        "g_kv_a": gain(ks[6], MLA_KV_RANK),
        "w_kv_b": w(ks[7], MLA_KV_RANK, MLA_HEADS * (MLA_NOPE_DIM + MLA_V_DIM)),
        "g_qn": gain(ks[8], MLA_QK_DIM),
        "g_kn": gain(ks[9], MLA_QK_DIM),
        "w_mla_out": w(ks[10], MLA_HEADS * MLA_V_DIM, D_MODEL),
        "ret_decay_fwd": decay_base + 0.05 * jax.random.normal(ks[11], (RET_HEADS,), f32),
        "ret_decay_bwd": decay_base + 0.05 * jax.random.normal(ks[12], (RET_HEADS,), f32),
        "w_ret_out": w(ks[13], RET_HEADS * RET_V_DIM, D_MODEL),
        "w_out": w(ks[14], D_MODEL, D_MODEL),
        "g_ffn": gain(ks[15], D_MODEL),
        "w_gate_up": w(ks[16], D_MODEL, 2 * FFN_HIDDEN),
        "w_down": w(ks[17], FFN_HIDDEN, D_MODEL),
    }


def _fwd_reference(x, positions, g_mix, w_in, g_q_a, w_q_b, g_kv_a, w_kv_b, g_qn, g_kn,
              w_mla_out, ret_decay_fwd, ret_decay_bwd, w_ret_out, w_out,
              g_ffn, w_gate_up, w_down):
    B, S, D = x.shape
    split_idx = np.cumsum(IN_SPLITS)[:-1].tolist()
    for _ in range(DEPTH):
        h = rms_norm(x, g_mix)
        proj = h @ w_in
        c_q, c_kv, k_rope, q_r, k_r, v_r, g_r, gate_logits = jnp.split(proj, split_idx, axis=-1)

        o_a = mla_attention(c_q, c_kv, k_rope, positions, g_q_a, w_q_b, g_kv_a, w_kv_b, g_qn, g_kn)
        y_a = o_a.astype(x.dtype) @ w_mla_out

        q_r = rope(q_r.reshape(B, S, RET_HEADS, RET_QK_DIM), positions)
        k_r = rope(k_r.reshape(B, S, RET_HEADS, RET_QK_DIM), positions)
        q_r = q_r.astype(jnp.float32).transpose(0, 2, 1, 3)
        k_r = k_r.astype(jnp.float32).transpose(0, 2, 1, 3) * (RET_QK_DIM ** -0.5)
        v_r = v_r.reshape(B, S, RET_HEADS, RET_V_DIM).astype(jnp.float32).transpose(0, 2, 1, 3)
        ret = bidirectional_retention(q_r, k_r, v_r, ret_decay_fwd, ret_decay_bwd)
        ret = _rms(ret).transpose(0, 2, 1, 3).reshape(B, S, RET_HEADS * RET_V_DIM)
        o_b = (jax.nn.silu(g_r.astype(jnp.float32)) * ret).astype(x.dtype)
        y_b = o_b @ w_ret_out

        gates = jax.nn.sigmoid(gate_logits.astype(jnp.float32))
        merged = gates[..., :D] * y_a.astype(jnp.float32) + gates[..., D:] * y_b.astype(jnp.float32)
        x = x + merged.astype(x.dtype) @ w_out

        h2 = rms_norm(x, g_ffn)
        gu = h2 @ w_gate_up
        gate, up = gu[..., :FFN_HIDDEN], gu[..., FFN_HIDDEN:]
        x = x + (jax.nn.silu(gate) * up) @ w_down
    return x


import jax as _jax
import jax.numpy as _jnp

TWIN_FORMAT = 'train_step'
FWD_PARAMS = ['x', 'positions', 'g_mix', 'w_in', 'g_q_a', 'w_q_b', 'g_kv_a', 'w_kv_b', 'g_qn', 'g_kn', 'w_mla_out', 'ret_decay_fwd', 'ret_decay_bwd', 'w_ret_out', 'w_out', 'g_ffn', 'w_gate_up', 'w_down']
TWIN_WEIGHTS = ['g_mix', 'w_in', 'g_q_a', 'w_q_b', 'g_kv_a', 'w_kv_b', 'g_qn', 'g_kn', 'w_mla_out', 'ret_decay_fwd', 'ret_decay_bwd', 'w_ret_out', 'w_out', 'g_ffn', 'w_gate_up', 'w_down']
TWIN_DIFF_INPUT = 'x'
TWIN_INPUTS = ['x', 'positions', 'g_mix', 'w_in', 'g_q_a', 'w_q_b', 'g_kv_a', 'w_kv_b', 'g_qn', 'g_kn', 'w_mla_out', 'ret_decay_fwd', 'ret_decay_bwd', 'w_ret_out', 'w_out', 'g_ffn', 'w_gate_up', 'w_down', 'loss_target', 'm_g_mix', 'm_w_in', 'm_g_q_a', 'm_w_q_b', 'm_g_kv_a', 'm_w_kv_b', 'm_g_qn', 'm_g_kn', 'm_w_mla_out', 'm_ret_decay_fwd', 'm_ret_decay_bwd', 'm_w_ret_out', 'm_w_out', 'm_g_ffn', 'm_w_gate_up', 'm_w_down', 'v_g_mix', 'v_w_in', 'v_g_q_a', 'v_w_q_b', 'v_g_kv_a', 'v_w_kv_b', 'v_g_qn', 'v_g_kn', 'v_w_mla_out', 'v_ret_decay_fwd', 'v_ret_decay_bwd', 'v_w_ret_out', 'v_w_out', 'v_g_ffn', 'v_w_gate_up', 'v_w_down']
TWIN_OUTPUTS = ['loss', 'grad_x', 'grad_g_mix', 'grad_w_in', 'grad_g_q_a', 'grad_w_q_b', 'grad_g_kv_a', 'grad_w_kv_b', 'grad_g_qn', 'grad_g_kn', 'grad_w_mla_out', 'grad_ret_decay_fwd', 'grad_ret_decay_bwd', 'grad_w_ret_out', 'grad_w_out', 'grad_g_ffn', 'grad_w_gate_up', 'grad_w_down', 'delta_g_mix', 'delta_w_in', 'delta_g_q_a', 'delta_w_q_b', 'delta_g_kv_a', 'delta_w_kv_b', 'delta_g_qn', 'delta_g_kn', 'delta_w_mla_out', 'delta_ret_decay_fwd', 'delta_ret_decay_bwd', 'delta_w_ret_out', 'delta_w_out', 'delta_g_ffn', 'delta_w_gate_up', 'delta_w_down', 'new_m_g_mix', 'new_m_w_in', 'new_m_g_q_a', 'new_m_w_q_b', 'new_m_g_kv_a', 'new_m_w_kv_b', 'new_m_g_qn', 'new_m_g_kn', 'new_m_w_mla_out', 'new_m_ret_decay_fwd', 'new_m_ret_decay_bwd', 'new_m_w_ret_out', 'new_m_w_out', 'new_m_g_ffn', 'new_m_w_gate_up', 'new_m_w_down', 'new_v_g_mix', 'new_v_w_in', 'new_v_g_q_a', 'new_v_w_q_b', 'new_v_g_kv_a', 'new_v_w_kv_b', 'new_v_g_qn', 'new_v_g_kn', 'new_v_w_mla_out', 'new_v_ret_decay_fwd', 'new_v_ret_decay_bwd', 'new_v_w_ret_out', 'new_v_w_out', 'new_v_g_ffn', 'new_v_w_gate_up', 'new_v_w_down']
TWIN_LEAF_KINDS = {'loss': 'loss', 'grad_x': 'grad_x', 'grad_g_mix': 'grad_w', 'grad_w_in': 'grad_w', 'grad_g_q_a': 'grad_w', 'grad_w_q_b': 'grad_w', 'grad_g_kv_a': 'grad_w', 'grad_w_kv_b': 'grad_w', 'grad_g_qn': 'grad_w', 'grad_g_kn': 'grad_w', 'grad_w_mla_out': 'grad_w', 'grad_ret_decay_fwd': 'grad_w', 'grad_ret_decay_bwd': 'grad_w', 'grad_w_ret_out': 'grad_w', 'grad_w_out': 'grad_w', 'grad_g_ffn': 'grad_w', 'grad_w_gate_up': 'grad_w', 'grad_w_down': 'grad_w', 'delta_g_mix': 'delta_w', 'delta_w_in': 'delta_w', 'delta_g_q_a': 'delta_w', 'delta_w_q_b': 'delta_w', 'delta_g_kv_a': 'delta_w', 'delta_w_kv_b': 'delta_w', 'delta_g_qn': 'delta_w', 'delta_g_kn': 'delta_w', 'delta_w_mla_out': 'delta_w', 'delta_ret_decay_fwd': 'delta_w', 'delta_ret_decay_bwd': 'delta_w', 'delta_w_ret_out': 'delta_w', 'delta_w_out': 'delta_w', 'delta_g_ffn': 'delta_w', 'delta_w_gate_up': 'delta_w', 'delta_w_down': 'delta_w', 'new_m_g_mix': 'new_m', 'new_m_w_in': 'new_m', 'new_m_g_q_a': 'new_m', 'new_m_w_q_b': 'new_m', 'new_m_g_kv_a': 'new_m', 'new_m_w_kv_b': 'new_m', 'new_m_g_qn': 'new_m', 'new_m_g_kn': 'new_m', 'new_m_w_mla_out': 'new_m', 'new_m_ret_decay_fwd': 'new_m', 'new_m_ret_decay_bwd': 'new_m', 'new_m_w_ret_out': 'new_m', 'new_m_w_out': 'new_m', 'new_m_g_ffn': 'new_m', 'new_m_w_gate_up': 'new_m', 'new_m_w_down': 'new_m', 'new_v_g_mix': 'new_v', 'new_v_w_in': 'new_v', 'new_v_g_q_a': 'new_v', 'new_v_w_q_b': 'new_v', 'new_v_g_kv_a': 'new_v', 'new_v_w_kv_b': 'new_v', 'new_v_g_qn': 'new_v', 'new_v_g_kn': 'new_v', 'new_v_w_mla_out': 'new_v', 'new_v_ret_decay_fwd': 'new_v', 'new_v_ret_decay_bwd': 'new_v', 'new_v_w_ret_out': 'new_v', 'new_v_w_out': 'new_v', 'new_v_g_ffn': 'new_v', 'new_v_w_gate_up': 'new_v', 'new_v_w_down': 'new_v'}


def _forward(args):
    return _fwd_reference(*[args[k] for k in FWD_PARAMS])


def _output_shape():
    def fwd():
        inp = _fwd_setup_inputs(0)
        return _fwd_reference(*[inp[k] for k in FWD_PARAMS])
    out = _jax.eval_shape(fwd)
    return out.shape, out.dtype

N_MICROBATCH = 1
ADAM_LR = 0.001
ADAM_B1 = 0.9
ADAM_B2 = 0.999
ADAM_EPS = 1e-08
ADAM_WD = 0.01
ADAM_STEP = 10
PER_EXAMPLE_BATCH_AXIS = {'x': 0, 'positions': 0, 'loss_target': 0}
SHARED_INPUTS = []
_WEIGHT_DTYPES = {'g_mix': _jnp.float32, 'w_in': _jnp.float32, 'g_q_a': _jnp.float32, 'w_q_b': _jnp.float32, 'g_kv_a': _jnp.float32, 'w_kv_b': _jnp.float32, 'g_qn': _jnp.float32, 'g_kn': _jnp.float32, 'w_mla_out': _jnp.float32, 'ret_decay_fwd': _jnp.float32, 'ret_decay_bwd': _jnp.float32, 'w_ret_out': _jnp.float32, 'w_out': _jnp.float32, 'g_ffn': _jnp.float32, 'w_gate_up': _jnp.float32, 'w_down': _jnp.float32}
MOMENT_SCALE = {'g_mix': 8.544061e+00, 'w_in': 1.847659e-01, 'g_q_a': 1.031965e-01, 'w_q_b': 4.808091e-02, 'g_kv_a': 7.837317e-01, 'w_kv_b': 6.433503e-02, 'g_qn': 9.076080e-01, 'g_kn': 9.052465e-01, 'w_mla_out': 4.783209e-02, 'ret_decay_fwd': 4.799337e-01, 'ret_decay_bwd': 1.063000e+00, 'w_ret_out': 2.496616e-01, 'w_out': 2.570739e-01, 'g_ffn': 4.947056e+01, 'w_gate_up': 2.610459e-01, 'w_down': 4.446887e-01}


def _to_microbatches(a, axis):
    t = _jnp.moveaxis(a, axis, 0)
    t = t.reshape((N_MICROBATCH, t.shape[0] // N_MICROBATCH) + t.shape[1:])
    return _jnp.moveaxis(t, 1, axis + 1)


def setup_inputs(seed: int = 0) -> dict:
    inp = _fwd_setup_inputs(seed)
    key = _jax.random.fold_in(_jax.random.key(seed), 7919)
    shape, _ = _output_shape()
    out = dict(inp)
    out["loss_target"] = _jax.random.normal(_jax.random.fold_in(key, 0), shape, _jnp.float32)
    for i, name in enumerate(TWIN_WEIGHTS):
        w = inp[name].astype(_jnp.float32)
        if MOMENT_SCALE is None:
            s = _jnp.sqrt(_jnp.mean(_jnp.square(w)) + 1e-30)
        else:
            s = MOMENT_SCALE[name]
        km, kv = _jax.random.split(_jax.random.fold_in(key, i + 1))
        out[name] = w
        out["m_" + name] = s * _jax.random.normal(km, w.shape, _jnp.float32)
        out["v_" + name] = (s * s) * _jax.random.uniform(kv, w.shape, _jnp.float32, 0.5, 1.5)
    if N_MICROBATCH > 1:
        for name, axis in PER_EXAMPLE_BATCH_AXIS.items():
            out[name] = _to_microbatches(out[name], axis)
    return {'x': out['x'], 'positions': out['positions'], 'g_mix': out['g_mix'], 'w_in': out['w_in'], 'g_q_a': out['g_q_a'], 'w_q_b': out['w_q_b'], 'g_kv_a': out['g_kv_a'], 'w_kv_b': out['w_kv_b'], 'g_qn': out['g_qn'], 'g_kn': out['g_kn'], 'w_mla_out': out['w_mla_out'], 'ret_decay_fwd': out['ret_decay_fwd'], 'ret_decay_bwd': out['ret_decay_bwd'], 'w_ret_out': out['w_ret_out'], 'w_out': out['w_out'], 'g_ffn': out['g_ffn'], 'w_gate_up': out['w_gate_up'], 'w_down': out['w_down'], 'loss_target': out['loss_target'], 'm_g_mix': out['m_g_mix'], 'm_w_in': out['m_w_in'], 'm_g_q_a': out['m_g_q_a'], 'm_w_q_b': out['m_w_q_b'], 'm_g_kv_a': out['m_g_kv_a'], 'm_w_kv_b': out['m_w_kv_b'], 'm_g_qn': out['m_g_qn'], 'm_g_kn': out['m_g_kn'], 'm_w_mla_out': out['m_w_mla_out'], 'm_ret_decay_fwd': out['m_ret_decay_fwd'], 'm_ret_decay_bwd': out['m_ret_decay_bwd'], 'm_w_ret_out': out['m_w_ret_out'], 'm_w_out': out['m_w_out'], 'm_g_ffn': out['m_g_ffn'], 'm_w_gate_up': out['m_w_gate_up'], 'm_w_down': out['m_w_down'], 'v_g_mix': out['v_g_mix'], 'v_w_in': out['v_w_in'], 'v_g_q_a': out['v_g_q_a'], 'v_w_q_b': out['v_w_q_b'], 'v_g_kv_a': out['v_g_kv_a'], 'v_w_kv_b': out['v_w_kv_b'], 'v_g_qn': out['v_g_qn'], 'v_g_kn': out['v_g_kn'], 'v_w_mla_out': out['v_w_mla_out'], 'v_ret_decay_fwd': out['v_ret_decay_fwd'], 'v_ret_decay_bwd': out['v_ret_decay_bwd'], 'v_w_ret_out': out['v_w_ret_out'], 'v_w_out': out['v_w_out'], 'v_g_ffn': out['v_g_ffn'], 'v_w_gate_up': out['v_w_gate_up'], 'v_w_down': out['v_w_down']}


def _loss(weights, diff, rest, loss_target):
    with _jax.named_scope("forward"):
        args = {**rest, TWIN_DIFF_INPUT: diff, **{k: w.astype(_WEIGHT_DTYPES[k]) for k, w in weights.items()}}
        y = _forward(args)
    with _jax.named_scope("loss_head"):
        err = _jnp.square(y.astype(_jnp.float32) - loss_target)
        return 0.5 * _jnp.sum(_jnp.mean(err, axis=-1)) if err.ndim else 0.5 * err


def _adamw(w, g, m, v):
    m = ADAM_B1 * m + (1.0 - ADAM_B1) * g
    v = ADAM_B2 * v + (1.0 - ADAM_B2) * _jnp.square(g)
    m_hat = m / (1.0 - ADAM_B1 ** ADAM_STEP)
    v_hat = v / (1.0 - ADAM_B2 ** ADAM_STEP)
    delta = -ADAM_LR * (m_hat / (_jnp.sqrt(v_hat) + ADAM_EPS) + ADAM_WD * w)
    return delta, m, v


def reference(x, positions, g_mix, w_in, g_q_a, w_q_b, g_kv_a, w_kv_b, g_qn, g_kn, w_mla_out, ret_decay_fwd, ret_decay_bwd, w_ret_out, w_out, g_ffn, w_gate_up, w_down, loss_target, m_g_mix, m_w_in, m_g_q_a, m_w_q_b, m_g_kv_a, m_w_kv_b, m_g_qn, m_g_kn, m_w_mla_out, m_ret_decay_fwd, m_ret_decay_bwd, m_w_ret_out, m_w_out, m_g_ffn, m_w_gate_up, m_w_down, v_g_mix, v_w_in, v_g_q_a, v_w_q_b, v_g_kv_a, v_w_kv_b, v_g_qn, v_g_kn, v_w_mla_out, v_ret_decay_fwd, v_ret_decay_bwd, v_w_ret_out, v_w_out, v_g_ffn, v_w_gate_up, v_w_down):
    given = dict(x=x, positions=positions, g_mix=g_mix, w_in=w_in, g_q_a=g_q_a, w_q_b=w_q_b, g_kv_a=g_kv_a, w_kv_b=w_kv_b, g_qn=g_qn, g_kn=g_kn, w_mla_out=w_mla_out, ret_decay_fwd=ret_decay_fwd, ret_decay_bwd=ret_decay_bwd, w_ret_out=w_ret_out, w_out=w_out, g_ffn=g_ffn, w_gate_up=w_gate_up, w_down=w_down, loss_target=loss_target, m_g_mix=m_g_mix, m_w_in=m_w_in, m_g_q_a=m_g_q_a, m_w_q_b=m_w_q_b, m_g_kv_a=m_g_kv_a, m_w_kv_b=m_w_kv_b, m_g_qn=m_g_qn, m_g_kn=m_g_kn, m_w_mla_out=m_w_mla_out, m_ret_decay_fwd=m_ret_decay_fwd, m_ret_decay_bwd=m_ret_decay_bwd, m_w_ret_out=m_w_ret_out, m_w_out=m_w_out, m_g_ffn=m_g_ffn, m_w_gate_up=m_w_gate_up, m_w_down=m_w_down, v_g_mix=v_g_mix, v_w_in=v_w_in, v_g_q_a=v_g_q_a, v_w_q_b=v_w_q_b, v_g_kv_a=v_g_kv_a, v_w_kv_b=v_w_kv_b, v_g_qn=v_g_qn, v_g_kn=v_g_kn, v_w_mla_out=v_w_mla_out, v_ret_decay_fwd=v_ret_decay_fwd, v_ret_decay_bwd=v_ret_decay_bwd, v_w_ret_out=v_w_ret_out, v_w_out=v_w_out, v_g_ffn=v_g_ffn, v_w_gate_up=v_w_gate_up, v_w_down=v_w_down)
    weights = {n: given[n] for n in TWIN_WEIGHTS}
    shared = {n: given[n] for n in SHARED_INPUTS}
    per_example = {n: given[n] for n in ['x', 'positions']}
    grad_fn = _jax.value_and_grad(_loss, argnums=(0, 1))

    def one_microbatch(ex, loss_target):
        ex = dict(ex)
        diff = ex.pop(TWIN_DIFF_INPUT)
        return grad_fn(weights, diff, {**shared, **ex}, loss_target)

    if N_MICROBATCH == 1:
        loss, (grad_w, grad_x) = one_microbatch(per_example, given["loss_target"])
    else:
        def body(carry, xs):
            loss_sum, grad_sum = carry
            l_k, (gw_k, gx_k) = one_microbatch(xs[0], xs[1])
            with _jax.named_scope("update"):
                return (loss_sum + l_k, _jax.tree.map(_jnp.add, grad_sum, gw_k)), gx_k

        init = (_jnp.zeros((), _jnp.float32), _jax.tree.map(_jnp.zeros_like, weights))
        (loss, grad_w), grad_x = _jax.lax.scan(body, init, (per_example, given["loss_target"]))
    with _jax.named_scope("update"):
        delta_w, new_m, new_v = {}, {}, {}
        for n in TWIN_WEIGHTS:
            delta_w[n], new_m[n], new_v[n] = _adamw(weights[n], grad_w[n], given["m_" + n], given["v_" + n])
    return (loss, grad_x, *[grad_w[n] for n in TWIN_WEIGHTS], *[delta_w[n] for n in TWIN_WEIGHTS],
            *[new_m[n] for n in TWIN_WEIGHTS], *[new_v[n] for n in TWIN_WEIGHTS])
```

```python
import functools
import math

import numpy as np
import jax
import jax.numpy as jnp
from jax import lax
from jax.experimental import pallas as pl
from jax.experimental.pallas import tpu as pltpu

F32 = jnp.float32
BF16 = jnp.bfloat16
MESH = pl.DeviceIdType.MESH

D_MODEL = 1024
HEADS = 8
LANES = 128
MLA_Q_RANK, MLA_KV_RANK = 256, 128
MLA_NOPE, MLA_ROPE, MLA_V = 64, 32, 64
MLA_QK = MLA_NOPE + MLA_ROPE
RET_QK, RET_V, RET_CHUNK = 64, 128, 128
FFN_HIDDEN = 2816
ROPE_THETA = 10000.0
EPS = 1e-6
IN_SPLITS = [256, 128, 32, 512, 512, 1024, 1024, 2048]
IN_OFFS = [0] + list(np.cumsum(IN_SPLITS))
ADAM_LR, ADAM_B1, ADAM_B2, ADAM_EPS, ADAM_WD, ADAM_STEP = 0.001, 0.9, 0.999, 1e-08, 0.01, 10

VMEM_LIMIT = 56 * 1024 * 1024
ROW_TILE = 256
HEAD_ROW_TILE = 1024
MM_TM, MM_TN, MM_TK, MM_KFULL = 512, 1024, 1024, 2816
ATT_TQ, ATT_TK = 512, 512
SLAB_W, SLAB_ROWS, SLAB_TILE = 512, 4224, 704

SHARDED = ["w_in", "w_q_b", "w_kv_b", "w_mla_out", "w_ret_out", "w_out", "w_gate_up", "w_down"]
COL_SHARDED = {"w_in", "w_q_b", "w_kv_b", "w_mla_out", "w_gate_up"}
SMALL = ["g_mix", "g_q_a", "g_kv_a", "g_qn", "g_kn", "ret_decay_fwd", "ret_decay_bwd", "g_ffn"]
WEIGHTS = ["g_mix", "w_in", "g_q_a", "w_q_b", "g_kv_a", "w_kv_b", "g_qn", "g_kn", "w_mla_out",
           "ret_decay_fwd", "ret_decay_bwd", "w_ret_out", "w_out", "g_ffn", "w_gate_up", "w_down"]
SMALL_ROWS = 24


def _params(**kw):
    return pltpu.CompilerParams(vmem_limit_bytes=VMEM_LIMIT, **kw)


def _pick(dim, target, unit=128):
    if dim <= target:
        return dim
    best = None
    for d in range(unit, target + 1, unit):
        if dim % d == 0:
            best = d
    assert best is not None, (dim, target)
    return best


_DOT = {"nn": (((1,), (0,)), ((), ())), "nt": (((1,), (1,)), ((), ())), "tn": (((0,), (0,)), ((), ()))}


def _dot(a, b, mode="nn"):
    return lax.dot_general(a, b, _DOT[mode], preferred_element_type=F32)


def _mm(name, a, b, mode, out_dtype, res=None):
    if mode == "nn":
        (M, K), (K2, N) = a.shape, b.shape
    elif mode == "nt":
        (M, K), (N, K2) = a.shape, b.shape
    else:
        (K, M), (K2, N) = a.shape, b.shape
    assert K == K2, (name, a.shape, b.shape)
    tm, tn = _pick(M, MM_TM), _pick(N, MM_TN)
    tk = K if K <= MM_KFULL else _pick(K, MM_TK)
    nk = K // tk

    def body(*refs):
        a_ref, b_ref = refs[0], refs[1]
        o_ref, acc = refs[-2], refs[-1]
        k = pl.program_id(2)

        @pl.when(k == 0)
        def _():
            acc[...] = jnp.zeros_like(acc)

        acc[...] += _dot(a_ref[...].astype(BF16), b_ref[...].astype(BF16), mode)

        @pl.when(k == nk - 1)
        def _():
            r = acc[...]
            if res is not None:
                r = r + refs[2][...].astype(F32)
            o_ref[...] = r.astype(o_ref.dtype)

    a_spec = pl.BlockSpec((tk, tm), lambda i, j, k: (k, i)) if mode == "tn" else pl.BlockSpec((tm, tk), lambda i, j, k: (i, k))
    b_spec = pl.BlockSpec((tn, tk), lambda i, j, k: (j, k)) if mode == "nt" else pl.BlockSpec((tk, tn), lambda i, j, k: (k, j))
    o_spec = pl.BlockSpec((tm, tn), lambda i, j, k: (i, j))
    ins, specs = [a, b], [a_spec, b_spec]
    if res is not None:
        ins.append(res)
        specs.append(o_spec)
    return pl.pallas_call(
        body, name=name, grid=(M // tm, N // tn, nk), in_specs=specs, out_specs=o_spec,
        out_shape=jax.ShapeDtypeStruct((M, N), out_dtype),
        scratch_shapes=[pltpu.VMEM((tm, tn), F32)], compiler_params=_params(),
    )(*ins)


def _piece_spec(tm, piece):
    _, w, c0, per_group = piece
    if per_group:
        return pl.BlockSpec((tm, w), lambda i, g: (i, c0 + g))
    return pl.BlockSpec((tm, w), lambda i, g: (i, c0))


def _const_spec(p):
    return pl.BlockSpec(p.shape, lambda i, g: (0, 0))


def _rowwise(name, fn, params, rows, auxs, outs, tm, groups=1):
    S = rows[0][0].shape[0]
    tm = min(tm, S)
    n_p, n_r, n_a = len(params), len(rows), len(auxs)

    def body(*refs):
        p = [r[...] for r in refs[:n_p]]
        r_ = [r[...] for r in refs[n_p:n_p + n_r]]
        a_ = [r[...] for r in refs[n_p + n_r:n_p + n_r + n_a]]
        for o_ref, o in zip(refs[n_p + n_r + n_a:], fn(p, r_, a_)):
            o_ref[...] = o.astype(o_ref.dtype)

    out_specs, out_shape = [], []
    for w, dt, per_group in outs:
        out_specs.append(_piece_spec(tm, (None, w, 0, per_group)))
        out_shape.append(jax.ShapeDtypeStruct((S, w * (groups if per_group else 1)), dt))
    return pl.pallas_call(
        body, name=name, grid=(S // tm, groups),
        in_specs=[_const_spec(p) for p in params] + [_piece_spec(tm, q) for q in list(rows) + list(auxs)],
        out_specs=out_specs, out_shape=out_shape, compiler_params=_params(),
    )(*params, *[q[0] for q in list(rows) + list(auxs)])


def _rowwise_vjp(name, fn, params, rows, auxs, cots, d_outs, tm, groups=1, adds=None):
    S = rows[0][0].shape[0]
    tm = min(tm, S)
    n_p, n_r, n_a = len(params), len(rows), len(auxs)
    cot_flat = [q for c in cots for q in c]
    adds = adds or [None] * len(d_outs)
    add_flat = [q for q in adds if q is not None]
    n_c, n_add = len(cot_flat), len(add_flat)
    shared = [not all(rows[k][3] for k in idx) and groups > 1 for idx, _ in d_outs]

    def body(*refs):
        pos = 0
        p = [r[...] for r in refs[pos:pos + n_p]]; pos += n_p
        r_ = [r[...] for r in refs[pos:pos + n_r]]; pos += n_r
        a_ = [r[...] for r in refs[pos:pos + n_a]]; pos += n_a
        c_refs = refs[pos:pos + n_c]; pos += n_c
        add_refs = list(refs[pos:pos + n_add]); pos += n_add
        d_refs = refs[pos:pos + len(d_outs)]; pos += len(d_outs)
        dp_refs = refs[pos:]
        i, g = pl.program_id(0), pl.program_id(1)
        outs, vjp_fn = jax.vjp(lambda pp, rr: fn(pp, rr, a_), p, r_)
        cts, ci = [], 0
        for c, o in zip(cots, outs):
            t = c_refs[ci][...].astype(F32)
            for extra in c_refs[ci + 1:ci + len(c)]:
                t = t + extra[...].astype(F32)
            ci += len(c)
            cts.append(t.astype(o.dtype))
        dp, dr = vjp_fn(cts)
        for (idx, _), d_ref, add, sh in zip(d_outs, d_refs, adds, shared):
            val = dr[idx[0]].astype(F32) if len(idx) == 1 else jnp.concatenate([dr[k].astype(F32) for k in idx], axis=1)
            if add is not None:
                val = val + add_refs.pop(0)[...].astype(F32)
            if sh:
                @pl.when(g == 0)
                def _(d_ref=d_ref):
                    d_ref[...] = jnp.zeros_like(d_ref)
                d_ref[...] += val.astype(d_ref.dtype)
            else:
                d_ref[...] = val.astype(d_ref.dtype)
        first = jnp.logical_and(i == 0, g == 0)
        for dp_ref, d in zip(dp_refs, dp):
            @pl.when(first)
            def _(dp_ref=dp_ref):
                dp_ref[...] = jnp.zeros_like(dp_ref)
            dp_ref[...] += d.astype(F32)

    out_specs, out_shape = [], []
    for (idx, dt), sh in zip(d_outs, shared):
        w = sum(rows[k][1] for k in idx)
        per_group = (not sh) and groups > 1
        out_specs.append(_piece_spec(tm, (None, w, 0, per_group)))
        out_shape.append(jax.ShapeDtypeStruct((S, w * (groups if per_group else 1)), dt))
    for p in params:
        out_specs.append(_const_spec(p))
        out_shape.append(jax.ShapeDtypeStruct(p.shape, F32))
    pieces = list(rows) + list(auxs) + cot_flat + add_flat
    res = pl.pallas_call(
        body, name=name, grid=(S // tm, groups),
        in_specs=[_const_spec(p) for p in params] + [_piece_spec(tm, q) for q in pieces],
        out_specs=out_specs, out_shape=out_shape, compiler_params=_params(),
    )(*params, *[q[0] for q in pieces])
    return list(res[:len(d_outs)]), list(res[len(d_outs):])


def _lane_roll(x, shift):
    @jax.custom_vjp
    def roll(v):
        return pltpu.roll(v, shift, 1)

    roll.defvjp(lambda v: (roll(v), None), lambda _, ct: (pltpu.roll(ct, LANES - shift, 1),))
    return roll(x)


@jax.custom_vjp
def _sigmoid(x):
    return 1.0 / (1.0 + jnp.exp(-x))


def _sigmoid_fwd(x):
    s = _sigmoid(x)
    return s, s


_sigmoid.defvjp(_sigmoid_fwd, lambda s, ct: (ct * s * (1.0 - s),))


def _rope(x, cos, sin_lo, sin_hi, half):
    return x * cos + _lane_roll(x, LANES - half) * sin_lo + _lane_roll(x, half) * sin_hi


def _f_rope_table(p, r, a):
    inv, first, second, fixed = p
    ang = a[0] * inv
    cs, sn = jnp.cos(ang), jnp.sin(ang)
    return [cs * (first + second) + fixed, -sn * first, sn * second]


def _f_rms(p, r, a):
    x = r[0].astype(F32)
    return [x * lax.rsqrt(jnp.mean(x * x, axis=-1, keepdims=True) + EPS) * p[0]]


def _f_mla_a(p, r, a):
    return _f_rms([p[0]], [r[0]], a) + _f_rms([p[1]], [r[1]], a)


def _f_mla_b(p, r, a):
    def norm_rope(v, g):
        ms = jnp.sum(v * v, axis=-1, keepdims=True) * (1.0 / MLA_QK)
        return _rope(v * lax.rsqrt(ms + EPS) * g, a[0], a[1], a[2], MLA_ROPE // 2)

    return [norm_rope(r[0].astype(F32), p[0]), norm_rope(r[1].astype(F32) + r[2].astype(F32), p[1])]


def _f_ret_rope(p, r, a):
    q = _rope(r[0].astype(F32), a[0], a[1], a[2], RET_QK // 2)
    k = _rope(r[1].astype(F32), a[0], a[1], a[2], RET_QK // 2)
    return [q, k * (RET_QK ** -0.5)]


def _f_ret_post(p, r, a):
    ret = r[0].astype(F32) + r[1].astype(F32)
    g = r[2].astype(F32)
    normed = ret * lax.rsqrt(jnp.mean(ret * ret, axis=-1, keepdims=True) + EPS)
    return [g * _sigmoid(g) * normed]


def _f_merge(p, r, a):
    return [_sigmoid(r[0].astype(F32)) * r[2].astype(F32) + _sigmoid(r[1].astype(F32)) * r[3].astype(F32)]


def _f_swiglu(p, r, a):
    g = r[0].astype(F32)
    return [g * _sigmoid(g) * r[1].astype(F32)]


def _f_delta(p, r, a):
    d = jnp.sum(r[0].astype(F32) * r[1].astype(F32), axis=-1, keepdims=True)
    return [jnp.broadcast_to(d, r[0].shape)]


def _f_add(p, r, a):
    return [r[0].astype(F32) + r[1].astype(F32)]


def _loss_kernel(y, tgt):
    S, Dm = y.shape
    tm = min(ROW_TILE, S)

    def body(y_ref, t_ref, dy_ref, loss_ref):
        @pl.when(pl.program_id(0) == 0)
        def _():
            loss_ref[...] = jnp.zeros_like(loss_ref)

        e = y_ref[...] - t_ref[...]
        dy_ref[...] = e * (1.0 / Dm)
        loss_ref[...] += 0.5 * jnp.sum(jnp.mean(e * e, axis=-1, keepdims=True), axis=0, keepdims=True)

    row = pl.BlockSpec((tm, Dm), lambda i: (i, 0))
    return pl.pallas_call(
        body, name="loss", grid=(S // tm,), in_specs=[row, row],
        out_specs=[row, pl.BlockSpec((1, LANES), lambda i: (0, 0))],
        out_shape=[jax.ShapeDtypeStruct((S, Dm), F32), jax.ShapeDtypeStruct((1, LANES), F32)],
        compiler_params=_params(),
    )(y, tgt)


def _flash_fwd(q, k, kv, scale):
    S = q.shape[0]
    tq, tk = min(ATT_TQ, S), min(ATT_TK, S)
    nk = S // tk

    def body(q_ref, k_ref, v_ref, o_ref, lse_ref, m_sc, l_sc, acc_sc):
        ki = pl.program_id(2)

        @pl.when(ki == 0)
        def _():
            m_sc[...] = jnp.full_like(m_sc, -jnp.inf)
            l_sc[...] = jnp.zeros_like(l_sc)
            acc_sc[...] = jnp.zeros_like(acc_sc)

        s = _dot(q_ref[...], k_ref[...], "nt") * scale
        m_prev = m_sc[...]
        m_new = jnp.maximum(m_prev, jnp.max(s, axis=-1, keepdims=True))
        alpha = jnp.exp(m_prev - m_new)
        p = jnp.exp(s - m_new[:, :1])
        l_sc[...] = alpha * l_sc[...] + jnp.sum(p, axis=-1, keepdims=True)
        acc_sc[...] = alpha * acc_sc[...] + _dot(p.astype(BF16), v_ref[...])
        m_sc[...] = m_new

        @pl.when(ki == nk - 1)
        def _():
            o_ref[...] = (acc_sc[...] / l_sc[...]).astype(o_ref.dtype)
            lse_ref[...] = m_sc[...] + jnp.log(l_sc[...])

    qs = pl.BlockSpec((tq, LANES), lambda h, i, j: (i, h))
    return pl.pallas_call(
        body, name="mla_fwd", grid=(HEADS, S // tq, nk),
        in_specs=[qs, pl.BlockSpec((tk, LANES), lambda h, i, j: (j, h)),
                  pl.BlockSpec((tk, LANES), lambda h, i, j: (j, HEADS + h))],
        out_specs=[qs, qs],
        out_shape=[jax.ShapeDtypeStruct((S, HEADS * LANES), BF16), jax.ShapeDtypeStruct((S, HEADS * LANES), F32)],
        scratch_shapes=[pltpu.VMEM((tq, LANES), F32)] * 3, compiler_params=_params(),
    )(q, k, kv)


def _flash_bwd(q, k, kv, do, lse, delta, scale):
    S = q.shape[0]
    tq, tk = min(ATT_TQ, S), min(ATT_TK, S)
    nq = S // tq

    def body(q_ref, k_ref, v_ref, do_ref, lse_ref, dl_ref, dq_ref, dk_ref, dv_ref, dk_sc, dv_sc):
        ki, qi = pl.program_id(1), pl.program_id(2)

        @pl.when(jnp.logical_and(ki == 0, qi == 0))
        def _():
            dq_ref[...] = jnp.zeros_like(dq_ref)

        @pl.when(qi == 0)
        def _():
            dk_sc[...] = jnp.zeros_like(dk_sc)
            dv_sc[...] = jnp.zeros_like(dv_sc)

        qv, kv_, dov = q_ref[...], k_ref[...], do_ref[...]
        p = jnp.exp(_dot(qv, kv_, "nt") * scale - lse_ref[...][:, :1])
        dp = _dot(dov, v_ref[...], "nt")
        ds = (p * (dp - dl_ref[...][:, :1]) * scale).astype(BF16)
        dv_sc[...] += _dot(p.astype(BF16), dov, "tn")
        dk_sc[...] += _dot(ds, qv, "tn")
        rows = pl.ds(pl.multiple_of(qi * tq, tq), tq)
        dq_ref[rows, :] += _dot(ds, kv_)

        @pl.when(qi == nq - 1)
        def _():
            dk_ref[...] = dk_sc[...].astype(dk_ref.dtype)
            dv_ref[...] = dv_sc[...].astype(dv_ref.dtype)

    qs = pl.BlockSpec((tq, LANES), lambda h, j, i: (i, h))
    ks = pl.BlockSpec((tk, LANES), lambda h, j, i: (j, h))
    return pl.pallas_call(
        body, name="mla_bwd", grid=(HEADS, S // tk, nq),
        in_specs=[qs, ks, pl.BlockSpec((tk, LANES), lambda h, j, i: (j, HEADS + h)), qs, qs, qs],
        out_specs=[pl.BlockSpec((S, LANES), lambda h, j, i: (0, h)), ks, ks],
        out_shape=[jax.ShapeDtypeStruct((S, HEADS * LANES), F32), jax.ShapeDtypeStruct((S, HEADS * LANES), BF16),
                   jax.ShapeDtypeStruct((S, HEADS * LANES), BF16)],
        scratch_shapes=[pltpu.VMEM((tk, LANES), F32)] * 2, compiler_params=_params(),
    )(q, k, kv, do, lse, delta)


def _ret_tables(decay_row, backward):
    C = RET_CHUNK
    lg = -jnp.exp(decay_row)
    t = lax.broadcasted_iota(jnp.int32, (C, C), 0).astype(F32)
    s = lax.broadcasted_iota(jnp.int32, (C, C), 1).astype(F32)
    ridx = lax.broadcasted_iota(jnp.int32, (C, LANES), 0).astype(F32)
    if backward:
        dist, mask, aw, bw = s - t, s > t, C - ridx, ridx
    else:
        dist, mask, aw, bw = t - s, t >= s, ridx + 1.0, C - 1.0 - ridx
    dist = jnp.maximum(dist, 0.0)
    din = jnp.where(mask, jnp.exp(lg[:, :1] * dist), 0.0)
    return dict(din=din, dist=dist, a=jnp.exp(lg * aw), b=jnp.exp(lg * bw), c=jnp.exp(lg * C), aw=aw, bw=bw)


def _ret_fwd(qr, kr, proj, v_block, dec_f, dec_b):
    S = qr.shape[0]
    C = RET_CHUNK
    n = S // C
    W = HEADS * LANES

    def body(qf, kf, vf, qb, kb, vb, df, db, of, ob, sf_out, sb_out, st):
        @pl.when(pl.program_id(0) == 0)
        def _():
            st[...] = jnp.zeros_like(st)

        for d, (q_ref, k_ref, v_ref, dec, o_ref, s_out) in enumerate(
                [(qf, kf, vf, df, of, sf_out), (qb, kb, vb, db, ob, sb_out)]):
            for h in range(HEADS):
                lanes = slice(h * LANES, (h + 1) * LANES)
                tb = _ret_tables(dec[h:h + 1, :], d == 1)
                q, k, v = q_ref[:, lanes], k_ref[:, lanes], v_ref[:, lanes]
                state = st[d, h]
                s_out[0, h] = state
                inner = _dot((_dot(q, k, "nt") * tb["din"]).astype(BF16), v)
                cross = _dot((q.astype(F32) * tb["a"]).astype(BF16), state.astype(BF16))
                o_ref[:, lanes] = inner + cross
                st[d, h] = state * tb["c"] + _dot((k.astype(F32) * tb["b"]).astype(BF16), v, "tn")

    fw = lambda c0: pl.BlockSpec((C, W), lambda j: (j, c0))
    bw = lambda c0: pl.BlockSpec((C, W), lambda j: (n - 1 - j, c0))
    dec_spec = pl.BlockSpec((HEADS, LANES), lambda j: (0, 0))
    st_shape = jax.ShapeDtypeStruct((n, HEADS, LANES, LANES), F32)
    return pl.pallas_call(
        body, name="ret_fwd", grid=(n,),
        in_specs=[fw(0), fw(0), fw(v_block), bw(0), bw(0), bw(v_block), dec_spec, dec_spec],
        out_specs=[fw(0), bw(0), pl.BlockSpec((1, HEADS, LANES, LANES), lambda j: (j, 0, 0, 0)),
                   pl.BlockSpec((1, HEADS, LANES, LANES), lambda j: (n - 1 - j, 0, 0, 0))],
        out_shape=[jax.ShapeDtypeStruct((S, W), F32)] * 2 + [st_shape] * 2,
        scratch_shapes=[pltpu.VMEM((2, HEADS, LANES, LANES), F32)], compiler_params=_params(),
    )(qr, kr, proj, qr, kr, proj, dec_f, dec_b)


def _ret_bwd(qr, kr, proj, v_block, dret, sf, sb, dec_f, dec_b):
    S = qr.shape[0]
    C = RET_CHUNK
    n = S // C
    W = HEADS * LANES

    def body(qf, kf, vf, gf, sf_ref, qb, kb, vb, gb, sb_ref, df, db,
             dqf, dkf, dvf, dqb, dkb, dvb, ddf, ddb, ds_sc):
        j = pl.program_id(0)

        @pl.when(j == 0)
        def _():
            ds_sc[...] = jnp.zeros_like(ds_sc)
            ddf[...] = jnp.zeros_like(ddf)
            ddb[...] = jnp.zeros_like(ddb)

        for d, (q_ref, k_ref, v_ref, g_ref, s_ref, dec, dq_ref, dk_ref, dv_ref, dd_ref) in enumerate(
                [(qf, kf, vf, gf, sf_ref, df, dqf, dkf, dvf, ddf), (qb, kb, vb, gb, sb_ref, db, dqb, dkb, dvb, ddb)]):
            for h in range(HEADS):
                lanes = slice(h * LANES, (h + 1) * LANES)
                tb = _ret_tables(dec[h:h + 1, :], d == 1)
                q, k, v, g = q_ref[:, lanes], k_ref[:, lanes], v_ref[:, lanes], g_ref[:, lanes]
                qf32, kf32 = q.astype(F32), k.astype(F32)
                state, dstate = s_ref[0, h], ds_sc[d, h]
                dstate_b = dstate.astype(BF16)
                dp = _dot(g, v, "nt")
                a_ = _dot(q, k, "nt")
                da = (dp * tb["din"]).astype(BF16)
                g1 = _dot(g, state.astype(BF16), "nt")
                g2 = _dot(v, dstate_b, "nt")
                dq_ref[:, lanes] = (_dot(da, k) + g1 * tb["a"]).astype(dq_ref.dtype)
                dk_ref[:, lanes] = (_dot(da, q, "tn") + g2 * tb["b"]).astype(dk_ref.dtype)
                dv_ref[:, lanes] = (_dot((a_ * tb["din"]).astype(BF16), g, "tn")
                                    + _dot((kf32 * tb["b"]).astype(BF16), dstate_b)).astype(dv_ref.dtype)
                dlg = (jnp.sum(dp * a_ * tb["din"] * tb["dist"], keepdims=True)
                       + jnp.sum(g1 * qf32 * tb["a"] * tb["aw"], keepdims=True)
                       + jnp.sum(g2 * kf32 * tb["b"] * tb["bw"], keepdims=True)
                       + C * jnp.sum(tb["c"] * dstate * state, keepdims=True))
                dd_ref[h:h + 1, :] += jnp.broadcast_to(dlg, (1, LANES))
                ds_sc[d, h] = dstate * tb["c"] + _dot((qf32 * tb["a"]).astype(BF16), g, "tn")

        @pl.when(j == n - 1)
        def _():
            ddf[...] = ddf[...] * -jnp.exp(df[...])
            ddb[...] = ddb[...] * -jnp.exp(db[...])

    fw = lambda c0: pl.BlockSpec((C, W), lambda j: (n - 1 - j, c0))
    bw = lambda c0: pl.BlockSpec((C, W), lambda j: (j, c0))
    dec_spec = pl.BlockSpec((HEADS, LANES), lambda j: (0, 0))
    act = jax.ShapeDtypeStruct((S, W), BF16)
    return pl.pallas_call(
        body, name="ret_bwd", grid=(n,),
        in_specs=[fw(0), fw(0), fw(v_block), fw(0), pl.BlockSpec((1, HEADS, LANES, LANES), lambda j: (n - 1 - j, 0, 0, 0)),
                  bw(0), bw(0), bw(v_block), bw(0), pl.BlockSpec((1, HEADS, LANES, LANES), lambda j: (j, 0, 0, 0)),
                  dec_spec, dec_spec],
        out_specs=[fw(0)] * 3 + [bw(0)] * 3 + [dec_spec] * 2,
        out_shape=[act] * 6 + [jax.ShapeDtypeStruct((HEADS, LANES), F32)] * 2,
        scratch_shapes=[pltpu.VMEM((2, HEADS, LANES, LANES), F32)], compiler_params=_params(),
    )(qr, kr, proj, dret, sf, qr, kr, proj, dret, sb, dec_f, dec_b)


def _pad_heads(w, hd):
    K = w.shape[0]
    return jnp.pad(w.reshape(K, HEADS, hd), ((0, 0), (0, 0), (0, LANES - hd))).reshape(K, HEADS * LANES)


def _unpad_heads(w, hd):
    K = w.shape[0]
    return w.reshape(K, HEADS, LANES)[:, :, :hd].reshape(K, HEADS * hd)


def _rope_consts(first_lane, half):
    lane = np.arange(LANES)
    first = ((lane >= first_lane) & (lane < first_lane + half)).astype(np.float32)
    second = ((lane >= first_lane + half) & (lane < first_lane + 2 * half)).astype(np.float32)
    fixed = (lane < first_lane).astype(np.float32)
    j = np.where(first > 0, lane - first_lane, lane - first_lane - half) * (first + second)
    inv = (ROPE_THETA ** (-j.astype(np.float64) / half)).astype(np.float32)
    return [jnp.asarray(v.reshape(1, LANES), F32) for v in (inv, first, second, fixed)]


def _local_step(x, pos, tgt, wts, small):
    w_in = wts["w_in"]
    seg = [w_in[:, IN_OFFS[i]:IN_OFFS[i + 1]] for i in range(8)]
    kr_w = jnp.pad(seg[2], ((0, 0), (MLA_NOPE, LANES - MLA_QK)))
    w_in_p = jnp.concatenate([seg[7], seg[5], seg[6], _pad_heads(seg[3], RET_QK), _pad_heads(seg[4], RET_QK),
                              seg[0], seg[1], kr_w], axis=1)
    w_qb_p = _pad_heads(wts["w_q_b"], MLA_QK)
    kvw = wts["w_kv_b"].reshape(MLA_KV_RANK, HEADS, MLA_NOPE + MLA_V)
    pad_kv = lambda t: jnp.pad(t, ((0, 0), (0, 0), (0, LANES - t.shape[2]))).reshape(MLA_KV_RANK, HEADS * LANES)
    w_kn_p, w_v_p = pad_kv(kvw[:, :, :MLA_NOPE]), pad_kv(kvw[:, :, MLA_NOPE:])
    w_kv_p = jnp.concatenate([w_kn_p, w_v_p], axis=1)
    w_mla_p = jnp.pad(wts["w_mla_out"].reshape(HEADS, MLA_V, D_MODEL), ((0, 0), (0, LANES - MLA_V), (0, 0))).reshape(HEADS * LANES, D_MODEL)
    w_ret_out, w_out, w_gu, w_down = wts["w_ret_out"], wts["w_out"], wts["w_gate_up"], wts["w_down"]
    g_qn_p = jnp.pad(small["g_qn"], ((0, 0), (0, LANES - MLA_QK)))
    g_kn_p = jnp.pad(small["g_kn"], ((0, 0), (0, LANES - MLA_QK)))
    dec_f = jnp.broadcast_to(small["ret_decay_fwd"].reshape(HEADS, 1), (HEADS, LANES))
    dec_b = jnp.broadcast_to(small["ret_decay_bwd"].reshape(HEADS, 1), (HEADS, LANES))
    scale = MLA_QK ** -0.5
    T, N, Y = True, False, None
    RT, HT = ROW_TILE, HEAD_ROW_TILE

    tab_m = _rowwise("rope_table_mla", _f_rope_table, _rope_consts(MLA_NOPE, MLA_ROPE // 2), [(pos, 1, 0, N)], [(pos, 1, 0, N)],
                     [(LANES, F32, N)] * 3, HT)
    tab_r = _rowwise("rope_table_ret", _f_rope_table, _rope_consts(0, RET_QK // 2), [(pos, 1, 0, N)], [(pos, 1, 0, N)],
                     [(LANES, F32, N)] * 3, HT)
    aux_m = [(t, LANES, 0, N) for t in tab_m]
    aux_r = [(t, LANES, 0, N) for t in tab_r]

    rows_rms1 = [(x, D_MODEL, 0, N)]
    (h,) = _rowwise("rms_mix", _f_rms, [small["g_mix"]], rows_rms1, [], [(D_MODEL, BF16, N)], RT)
    proj = _mm("proj", h, w_in_p, "nn", BF16)
    rows_a = [(proj, MLA_Q_RANK, 24, N), (proj, MLA_KV_RANK, 50, N)]
    cqn, ckvn = _rowwise("mla_lat_norm", _f_mla_a, [small["g_q_a"], small["g_kv_a"]], rows_a, [],
                         [(MLA_Q_RANK, BF16, N), (MLA_KV_RANK, BF16, N)], RT)
    qraw = _mm("mla_q_up", cqn, w_qb_p, "nn", BF16)
    kv = _mm("mla_kv_up", ckvn, w_kv_p, "nn", BF16)
    rows_b = [(qraw, LANES, 0, T), (kv, LANES, 0, T), (proj, LANES, 51, N)]
    q, k = _rowwise("mla_qk_norm_rope", _f_mla_b, [g_qn_p, g_kn_p], rows_b, aux_m, [(LANES, BF16, T)] * 2, HT, HEADS)
    o, lse = _flash_fwd(q, k, kv, scale)
    y_a = _mm("mla_out", o, w_mla_p, "nn", F32)
    rows_rr = [(proj, LANES, 32, T), (proj, LANES, 40, T)]
    qr, kr = _rowwise("ret_rope", _f_ret_rope, [], rows_rr, aux_r, [(LANES, BF16, T)] * 2, HT, HEADS)
    ret_f, ret_b, st_f, st_b = _ret_fwd(qr, kr, proj, 2, dec_f, dec_b)
    rows_rp = [(ret_f, LANES, 0, T), (ret_b, LANES, 0, T), (proj, LANES, 24, T)]
    (o_b,) = _rowwise("ret_post", _f_ret_post, [], rows_rp, [], [(LANES, BF16, T)], HT, HEADS)
    y_b = _mm("ret_out", o_b, w_ret_out, "nn", F32)
    rows_m = [(proj, D_MODEL, 0, N), (proj, D_MODEL, 1, N), (y_a, D_MODEL, 0, N), (y_b, D_MODEL, 0, N)]
    (merged,) = _rowwise("merge", _f_merge, [], rows_m, [], [(D_MODEL, BF16, N)], RT)
    x2 = _mm("mix_out", merged, w_out, "nn", F32, res=x)
    rows_rms2 = [(x2, D_MODEL, 0, N)]
    (h2,) = _rowwise("rms_ffn", _f_rms, [small["g_ffn"]], rows_rms2, [], [(D_MODEL, BF16, N)], RT)
    gu = _mm("ffn_gate_up", h2, w_gu, "nn", BF16)
    rows_sw = [(gu, FFN_HIDDEN, 0, N), (gu, FFN_HIDDEN, 1, N)]
    (act,) = _rowwise("swiglu", _f_swiglu, [], rows_sw, [], [(FFN_HIDDEN, BF16, N)], RT)
    y = _mm("ffn_down", act, w_down, "nn", F32, res=x2)
    dy, loss_row = _loss_kernel(y, tgt)

    dact = _mm("d_act", dy, w_down, "nt", BF16)
    dw_down = _mm("dw_down", act, dy, "tn", F32)
    (dgu,), _ = _rowwise_vjp("swiglu_bwd", _f_swiglu, [], rows_sw, [], [[(dact, FFN_HIDDEN, 0, N)]], [([0, 1], BF16)], RT)
    dh2 = _mm("d_h2", dgu, w_gu, "nt", BF16)
    dw_gu = _mm("dw_gate_up", h2, dgu, "tn", F32)
    (dx2,), (dg_ffn,) = _rowwise_vjp("rms_ffn_bwd", _f_rms, [small["g_ffn"]], rows_rms2, [], [[(dh2, D_MODEL, 0, N)]],
                                     [([0], F32)], RT, adds=[(dy, D_MODEL, 0, N)])
    dmerged = _mm("d_merged", dx2, w_out, "nt", BF16)
    dw_out = _mm("dw_out", merged, dx2, "tn", F32)
    (dgl, dy_a, dy_b), _ = _rowwise_vjp("merge_bwd", _f_merge, [], rows_m, [], [[(dmerged, D_MODEL, 0, N)]],
                                        [([0, 1], BF16), ([2], BF16), ([3], BF16)], RT)
    do_b = _mm("d_ret_o", dy_b, w_ret_out, "nt", BF16)
    dw_ret_out = _mm("dw_ret_out", o_b, dy_b, "tn", F32)
    (dret, dg_r), _ = _rowwise_vjp("ret_post_bwd", _f_ret_post, [], rows_rp, [], [[(do_b, LANES, 0, T)]],
                                   [([0], BF16), ([2], BF16)], HT, HEADS)
    dqf, dkf, dvf, dqb, dkb, dvb, ddec_f, ddec_b = _ret_bwd(qr, kr, proj, 2, dret, st_f, st_b, dec_f, dec_b)
    (dq_r, dk_r), _ = _rowwise_vjp("ret_rope_bwd", _f_ret_rope, [], rows_rr, aux_r,
                                   [[(dqf, LANES, 0, T), (dqb, LANES, 0, T)], [(dkf, LANES, 0, T), (dkb, LANES, 0, T)]],
                                   [([0], BF16), ([1], BF16)], HT, HEADS)
    (dv_r,) = _rowwise("ret_dv_sum", _f_add, [], [(dvf, D_MODEL, 0, N), (dvb, D_MODEL, 0, N)], [], [(D_MODEL, BF16, N)], RT)
    do = _mm("d_mla_o", dy_a, w_mla_p, "nt", BF16)
    dw_mla_p = _mm("dw_mla_out", o, dy_a, "tn", F32)
    (delta,) = _rowwise("mla_delta", _f_delta, [], [(do, LANES, 0, T), (o, LANES, 0, T)], [], [(LANES, F32, T)], HT, HEADS)
    dq, dk, dv = _flash_bwd(q, k, kv, do, lse, delta, scale)
    (dqraw, dkn, dkr), (dg_qn_p, dg_kn_p) = _rowwise_vjp(
        "mla_qk_norm_rope_bwd", _f_mla_b, [g_qn_p, g_kn_p], rows_b, aux_m, [[(dq, LANES, 0, T)], [(dk, LANES, 0, T)]],
        [([0], BF16), ([1], BF16), ([2], F32)], HT, HEADS)
    dckvn = _mm("d_ckvn_v", dv, w_v_p, "nt", BF16, res=_mm("d_ckvn_k", dkn, w_kn_p, "nt", F32))
    dw_kn_p = _mm("dw_kv_k", ckvn, dkn, "tn", F32)
    dw_v_p = _mm("dw_kv_v", ckvn, dv, "tn", F32)
    dcqn = _mm("d_cqn", dqraw, w_qb_p, "nt", BF16)
    dw_qb_p = _mm("dw_q_b", cqn, dqraw, "tn", F32)
    (dcq, dckv), (dg_q_a, dg_kv_a) = _rowwise_vjp(
        "mla_lat_norm_bwd", _f_mla_a, [small["g_q_a"], small["g_kv_a"]], rows_a, [],
        [[(dcqn, MLA_Q_RANK, 0, N)], [(dckvn, MLA_KV_RANK, 0, N)]], [([0], BF16), ([1], BF16)], RT)
    dproj = jnp.concatenate([dgl, dv_r, dg_r, dq_r, dk_r, dcq, dckv, dkr.astype(BF16)], axis=1)
    dh = _mm("d_h", dproj, w_in_p, "nt", BF16)
    dw_in_p = _mm("dw_in", h, dproj, "tn", F32)
    (dx,), (dg_mix,) = _rowwise_vjp("rms_mix_bwd", _f_rms, [small["g_mix"]], rows_rms1, [], [[(dh, D_MODEL, 0, N)]],
                                    [([0], F32)], RT, adds=[(dx2, D_MODEL, 0, N)])

    c = lambda a, b_: dw_in_p[:, a:b_]
    dw_in = jnp.concatenate([c(6144, 6400), c(6400, 6528), c(6528 + MLA_NOPE, 6528 + MLA_QK), _unpad_heads(c(4096, 5120), RET_QK),
                             _unpad_heads(c(5120, 6144), RET_QK), c(2048, 3072), c(3072, 4096), c(0, 2048)], axis=1)
    un_kv = lambda t: t.reshape(MLA_KV_RANK, HEADS, LANES)[:, :, :MLA_NOPE]
    dw_kv = jnp.concatenate([un_kv(dw_kn_p), un_kv(dw_v_p)], axis=2).reshape(MLA_KV_RANK, HEADS * (MLA_NOPE + MLA_V))
    dw_mla = dw_mla_p.reshape(HEADS, LANES, D_MODEL)[:, :MLA_V].reshape(HEADS * MLA_V, D_MODEL)
    grads = {"w_in": dw_in, "w_q_b": _unpad_heads(dw_qb_p, MLA_QK), "w_kv_b": dw_kv, "w_mla_out": dw_mla,
             "w_ret_out": dw_ret_out, "w_out": dw_out, "w_gate_up": dw_gu, "w_down": dw_down}
    sgrads = {"g_mix": dg_mix, "g_q_a": dg_q_a, "g_kv_a": dg_kv_a, "g_qn": dg_qn_p[:, :MLA_QK], "g_kn": dg_kn_p[:, :MLA_QK],
              "ret_decay_fwd": ddec_f[:, 0].reshape(1, HEADS), "ret_decay_bwd": ddec_b[:, 0].reshape(1, HEADS), "g_ffn": dg_ffn}
    return loss_row, dx, grads, sgrads


def _coords():
    return lax.axis_index("x"), lax.axis_index("y"), lax.axis_index("c")


def _other_chips(x, y):
    return [(1 - x, y), (x, 1 - y), (1 - x, 1 - y)]


ANY = pl.BlockSpec(memory_space=pl.ANY)


def _weight_gather(shards):
    n = len(shards)

    def body(*refs):
        ins, outs = refs[:n], refs[n:2 * n]
        local_sems, send_sems, recv_sems = refs[2 * n:]
        x, y, c = _coords()
        chips = _other_chips(x, y)
        mine = 2 * x + y
        started = []
        for w in range(n):
            cp = pltpu.make_async_copy(ins[w], outs[w].at[mine], local_sems.at[w])
            cp.start()
            started.append(cp)
        sends = []
        for w in range(n):
            for j, (cx, cy) in enumerate(chips):
                cp = pltpu.make_async_remote_copy(
                    src_ref=ins[w], dst_ref=outs[w].at[mine], send_sem=send_sems.at[3 * w + j],
                    recv_sem=recv_sems.at[3 * w + j], device_id=(cx, cy, c), device_id_type=MESH)
                cp.start()
                sends.append(cp)
        for w in range(n):
            for j, (cx, cy) in enumerate(chips):
                pltpu.make_async_remote_copy(
                    src_ref=ins[w], dst_ref=outs[w].at[2 * cx + cy], send_sem=send_sems.at[3 * w + j],
                    recv_sem=recv_sems.at[3 * w + j], device_id=(cx, cy, c), device_id_type=MESH).wait_recv()
        for cp in sends:
            cp.wait_send()
        for cp in started:
            cp.wait()

    return pl.pallas_call(
        body, name="weight_gather", in_specs=[ANY] * n, out_specs=[ANY] * n,
        out_shape=[jax.ShapeDtypeStruct((4,) + s.shape, s.dtype) for s in shards],
        scratch_shapes=[pltpu.SemaphoreType.DMA((n,)), pltpu.SemaphoreType.DMA((3 * n,)), pltpu.SemaphoreType.DMA((3 * n,))],
    )(*shards)


def _sibling_exchange(slab):
    _, _, R, W = slab.shape

    def body(slab_ref, got_ref, send_sem, recv_sem):
        x, y, c = _coords()
        cp = pltpu.make_async_remote_copy(
            src_ref=slab_ref.at[:, 1 - c], dst_ref=got_ref, send_sem=send_sem, recv_sem=recv_sem,
            device_id=(x, y, 1 - c), device_id_type=MESH)
        cp.start()
        cp.wait()

    return pl.pallas_call(
        body, name="grad_sibling_exchange", in_specs=[ANY], out_specs=ANY,
        out_shape=jax.ShapeDtypeStruct((4, R, W), F32),
        scratch_shapes=[pltpu.SemaphoreType.DMA, pltpu.SemaphoreType.DMA],
    )(slab)


def _pair_sum(slab, got, c_arr):
    _, _, R, W = slab.shape
    tr = SLAB_TILE

    def body(c_ref, a_ref, b_ref, o_ref):
        o_ref[...] = a_ref[0] + b_ref[...]

    return pl.pallas_call(
        body, name="grad_pair_sum",
        grid_spec=pltpu.PrefetchScalarGridSpec(
            num_scalar_prefetch=1, grid=(4, R // tr),
            in_specs=[pl.BlockSpec((1, 1, tr, W), lambda j, i, c_ref: (j, c_ref[0], i, 0)),
                      pl.BlockSpec((1, tr, W), lambda j, i, c_ref: (j, i, 0))],
            out_specs=pl.BlockSpec((1, tr, W), lambda j, i, c_ref: (j, i, 0))),
        out_shape=jax.ShapeDtypeStruct((4, R, W), F32), compiler_params=_params(),
    )(c_arr, slab, got)


def _chip_exchange(part):
    _, R, W = part.shape

    def body(part_ref, got_ref, send_sems, recv_sems):
        x, y, c = _coords()
        sends = []
        for j, (cx, cy) in enumerate(_other_chips(x, y)):
            cp = pltpu.make_async_remote_copy(
                src_ref=part_ref.at[2 * cx + cy], dst_ref=got_ref.at[j], send_sem=send_sems.at[j],
                recv_sem=recv_sems.at[j], device_id=(cx, cy, c), device_id_type=MESH)
            cp.start()
            sends.append(cp)
        for cp in sends:
            cp.wait_recv()
        for cp in sends:
            cp.wait_send()

    return pl.pallas_call(
        body, name="grad_chip_exchange", in_specs=[ANY], out_specs=ANY,
        out_shape=jax.ShapeDtypeStruct((3, R, W), F32),
        scratch_shapes=[pltpu.SemaphoreType.DMA((3,)), pltpu.SemaphoreType.DMA((3,))],
    )(part)


def _chip_sum(part, got, slot_arr):
    _, R, W = part.shape
    tr = SLAB_TILE

    def body(s_ref, a_ref, b_ref, o_ref):
        o_ref[...] = ((a_ref[0] + b_ref[0]) + b_ref[1]) + b_ref[2]

    return pl.pallas_call(
        body, name="grad_chip_sum",
        grid_spec=pltpu.PrefetchScalarGridSpec(
            num_scalar_prefetch=1, grid=(R // tr,),
            in_specs=[pl.BlockSpec((1, tr, W), lambda i, s_ref: (s_ref[0], i, 0)),
                      pl.BlockSpec((3, tr, W), lambda i, s_ref: (0, i, 0))],
            out_specs=pl.BlockSpec((tr, W), lambda i, s_ref: (i, 0))),
        out_shape=jax.ShapeDtypeStruct((R, W), F32), compiler_params=_params(),
    )(slot_arr, part, got)


def _half_exchange(half):
    R, W = half.shape

    def body(half_ref, out_ref, local_sem, send_sem, recv_sem):
        x, y, c = _coords()
        mine = pltpu.make_async_copy(half_ref, out_ref.at[c], local_sem)
        mine.start()
        cp = pltpu.make_async_remote_copy(
            src_ref=half_ref, dst_ref=out_ref.at[c], send_sem=send_sem, recv_sem=recv_sem,
            device_id=(x, y, 1 - c), device_id_type=MESH)
        cp.start()
        pltpu.make_async_remote_copy(
            src_ref=half_ref, dst_ref=out_ref.at[1 - c], send_sem=send_sem, recv_sem=recv_sem,
            device_id=(x, y, 1 - c), device_id_type=MESH).wait_recv()
        cp.wait_send()
        mine.wait()

    return pl.pallas_call(
        body, name="grad_half_exchange", in_specs=[ANY], out_specs=ANY,
        out_shape=jax.ShapeDtypeStruct((2, R, W), F32),
        scratch_shapes=[pltpu.SemaphoreType.DMA, pltpu.SemaphoreType.DMA, pltpu.SemaphoreType.DMA],
    )(half)


def _adamw_math(w, g, m, v):
    m2 = ADAM_B1 * m + (1.0 - ADAM_B1) * g
    v2 = ADAM_B2 * v + (1.0 - ADAM_B2) * (g * g)
    m_hat = m2 / (1.0 - ADAM_B1 ** ADAM_STEP)
    v_hat = v2 / (1.0 - ADAM_B2 ** ADAM_STEP)
    return -ADAM_LR * (m_hat / (jnp.sqrt(v_hat) + ADAM_EPS) + ADAM_WD * w), m2, v2


def _small_allreduce_adamw(pack_g, pack_w, pack_m, pack_v):
    def body(g_ref, w_ref, m_ref, v_ref, sum_ref, d_ref, m_out, v_out, land, send_sems, recv_sems):
        x, y, c = _coords()
        me = 4 * x + 2 * y + c
        land[me] = g_ref[...]
        sends = []
        for k in range(1, 8):
            peer = (x ^ (k >> 2), y ^ ((k >> 1) & 1), c ^ (k & 1))
            cp = pltpu.make_async_remote_copy(
                src_ref=g_ref, dst_ref=land.at[me], send_sem=send_sems.at[k - 1], recv_sem=recv_sems.at[k - 1],
                device_id=peer, device_id_type=MESH)
            cp.start()
            sends.append((cp, peer))
        for k, (cp, peer) in enumerate(sends):
            pltpu.make_async_remote_copy(
                src_ref=g_ref, dst_ref=land.at[4 * peer[0] + 2 * peer[1] + peer[2]], send_sem=send_sems.at[k],
                recv_sem=recv_sems.at[k], device_id=peer, device_id_type=MESH).wait_recv()
        for cp, _ in sends:
            cp.wait_send()
        total = land[0]
        for d in range(1, 8):
            total = total + land[d]
        sum_ref[...] = total
        d_ref[...], m_out[...], v_out[...] = _adamw_math(w_ref[...], total, m_ref[...], v_ref[...])

    vm = pl.BlockSpec(memory_space=pltpu.VMEM)
    shp = jax.ShapeDtypeStruct(pack_g.shape, F32)
    return pl.pallas_call(
        body, name="small_allreduce_adamw", in_specs=[vm] * 4, out_specs=[vm] * 4, out_shape=[shp] * 4,
        scratch_shapes=[pltpu.VMEM((8,) + pack_g.shape, F32), pltpu.SemaphoreType.DMA((7,)), pltpu.SemaphoreType.DMA((7,))],
    )(pack_g, pack_w, pack_m, pack_v)


def _adamw(name, w, g, m, v):
    R, C = w.shape
    tr = _pick(R, 256, 8)

    def body(w_ref, g_ref, m_ref, v_ref, d_out, m_out, v_out):
        d_out[...], m_out[...], v_out[...] = _adamw_math(w_ref[...], g_ref[...], m_ref[...], v_ref[...])

    spec = pl.BlockSpec((tr, C), lambda i: (i, 0))
    return pl.pallas_call(
        body, name=name, grid=(R // tr,), in_specs=[spec] * 4, out_specs=[spec] * 3,
        out_shape=[jax.ShapeDtypeStruct((R, C), F32)] * 3, compiler_params=_params(),
    )(w, g, m, v)


def _pack_small(vals, last):
    flat = jnp.concatenate([v.reshape(-1) for v in vals] + [last.reshape(-1)])
    return jnp.pad(flat, (0, SMALL_ROWS * LANES - flat.shape[0])).reshape(SMALL_ROWS, LANES)


def kernel(x, positions, g_mix, w_in, g_q_a, w_q_b, g_kv_a, w_kv_b, g_qn, g_kn, w_mla_out, ret_decay_fwd, ret_decay_bwd, w_ret_out, w_out, g_ffn, w_gate_up, w_down, loss_target, m_g_mix, m_w_in, m_g_q_a, m_w_q_b, m_g_kv_a, m_w_kv_b, m_g_qn, m_g_kn, m_w_mla_out, m_ret_decay_fwd, m_ret_decay_bwd, m_w_ret_out, m_w_out, m_g_ffn, m_w_gate_up, m_w_down, v_g_mix, v_w_in, v_g_q_a, v_w_q_b, v_g_kv_a, v_w_kv_b, v_g_qn, v_g_kn, v_w_mla_out, v_ret_decay_fwd, v_ret_decay_bwd, v_w_ret_out, v_w_out, v_g_ffn, v_w_gate_up, v_w_down):
    given = dict(locals())
    S = x.shape[1]
    xs, tgt = x.reshape(S, D_MODEL), loss_target.reshape(S, D_MODEL)
    pos = positions.reshape(S, 1).astype(F32)
    cx, cy, cc = _coords()

    gathered = _weight_gather([given[n].astype(BF16) for n in SHARDED])
    wts = {}
    for n, g in zip(SHARDED, gathered):
        if n in COL_SHARDED:
            wts[n] = jnp.transpose(g, (1, 0, 2)).reshape(g.shape[1], 4 * g.shape[2])
        else:
            wts[n] = g.reshape(4 * g.shape[1], g.shape[2])
    small = {n: given[n].reshape(1, -1) for n in SMALL}

    loss_row, dx, grads, sgrads = _local_step(xs, pos, tgt, wts, small)

    parts = []
    for n in SHARDED:
        g = grads[n]
        if n in COL_SHARDED:
            K, N4 = g.shape
            parts.append(jnp.transpose(g.reshape(K, 4, N4 // 4), (1, 0, 2)).reshape(4, -1))
        else:
            parts.append(g.reshape(4, -1))
    sizes = [p.shape[1] for p in parts]
    flat = jnp.concatenate(parts, axis=1)
    slab = jnp.pad(flat, ((0, 0), (0, 2 * SLAB_ROWS * SLAB_W - flat.shape[1]))).reshape(4, 2, SLAB_ROWS, SLAB_W)
    c_arr = cc.reshape(1).astype(jnp.int32)
    slot_arr = (2 * cx + cy).reshape(1).astype(jnp.int32)
    pair = _pair_sum(slab, _sibling_exchange(slab), c_arr)
    half = _chip_sum(pair, _chip_exchange(pair), slot_arr)
    red = _half_exchange(half).reshape(-1)

    out = {}
    off = 0
    for n, sz in zip(SHARDED, sizes):
        g = red[off:off + sz].reshape(given[n].shape)
        off += sz
        out["grad_" + n] = g
        out["delta_" + n], out["new_m_" + n], out["new_v_" + n] = _adamw("adamw_" + n, given[n], g, given["m_" + n], given["v_" + n])

    one = jnp.ones((1,), F32)
    pk = _small_allreduce_adamw(
        _pack_small([sgrads[n] for n in SMALL], loss_row[0, :1]),
        _pack_small([given[n] for n in SMALL], 0 * one),
        _pack_small([given["m_" + n] for n in SMALL], 0 * one),
        _pack_small([given["v_" + n] for n in SMALL], one))
    off = 0
    for n in SMALL:
        sz = given[n].shape[0]
        for pre, arr in zip(["grad_", "delta_", "new_m_", "new_v_"], pk):
            out[pre + n] = arr.reshape(-1)[off:off + sz]
        off += sz
    loss = pk[0].reshape(-1)[off]

    return (loss, dx.reshape(x.shape), *[out["grad_" + n] for n in WEIGHTS], *[out["delta_" + n] for n in WEIGHTS],
            *[out["new_m_" + n] for n in WEIGHTS], *[out["new_v_" + n] for n in WEIGHTS])
```

```python
import functools
import math

import numpy as np
import jax
import jax.numpy as jnp
from jax import lax
from jax.experimental import pallas as pl
from jax.experimental.pallas import tpu as pltpu

F32 = jnp.float32
BF16 = jnp.bfloat16
MESH = pl.DeviceIdType.MESH

D_MODEL = 1024
HEADS = 8
LANES = 128
MLA_Q_RANK, MLA_KV_RANK = 256, 128
MLA_NOPE, MLA_ROPE, MLA_V = 64, 32, 64
MLA_QK = MLA_NOPE + MLA_ROPE
LN2 = math.log(2.0)
MLA_Q_SCALE = MLA_QK ** -0.5 / LN2
RET_QK, RET_V, RET_CHUNK = 64, 128, 128
RET_QK_DTYPE = F32
FFN_HIDDEN = 2816
ROPE_THETA = 10000.0
EPS = 1e-6
IN_SPLITS = [256, 128, 32, 512, 512, 1024, 1024, 2048]
IN_OFFS = [0] + list(np.cumsum(IN_SPLITS))
ADAM_LR, ADAM_B1, ADAM_B2, ADAM_EPS, ADAM_WD, ADAM_STEP = 0.001, 0.9, 0.999, 1e-08, 0.01, 10

VMEM_LIMIT = 56 * 1024 * 1024
ROW_TILE = 256
HEAD_ROW_TILE = 1024
MM_TM, MM_TN, MM_TK, MM_KFULL = 1024, 2048, 2048, 2816
ATT_TQ, ATT_TK = 512, 2048
ATT_BQ, ATT_BK = 1024, 1024

SHARDED = ["w_in", "w_q_b", "w_kv_b", "w_mla_out", "w_ret_out", "w_out", "w_gate_up", "w_down"]
COL_SHARDED = {"w_in", "w_q_b", "w_kv_b", "w_mla_out", "w_gate_up"}
SMALL = ["g_mix", "g_q_a", "g_kv_a", "g_qn", "g_kn", "ret_decay_fwd", "ret_decay_bwd", "g_ffn"]
WEIGHTS = ["g_mix", "w_in", "g_q_a", "w_q_b", "g_kv_a", "w_kv_b", "g_qn", "g_kn", "w_mla_out",
           "ret_decay_fwd", "ret_decay_bwd", "w_ret_out", "w_out", "g_ffn", "w_gate_up", "w_down"]
SMALL_ROWS = 24


def _params(**kw):
    return pltpu.CompilerParams(vmem_limit_bytes=VMEM_LIMIT, **kw)


def _pick(dim, target, unit=128):
    if dim <= target:
        return dim
    best = None
    for d in range(unit, target + 1, unit):
        if dim % d == 0:
            best = d
    assert best is not None, (dim, target)
    return best


_DOT = {"nn": (((1,), (0,)), ((), ())), "nt": (((1,), (1,)), ((), ())), "tn": (((0,), (0,)), ((), ()))}


def _dot(a, b, mode="nn"):
    return lax.dot_general(a, b, _DOT[mode], preferred_element_type=F32)


def _mm(name, a, b, mode, out_dtype, res=None):
    if mode == "nn":
        (M, K), (K2, N) = a.shape, b.shape
    elif mode == "nt":
        (M, K), (N, K2) = a.shape, b.shape
    else:
        (K, M), (K2, N) = a.shape, b.shape
    assert K == K2, (name, a.shape, b.shape)
    tm, tn = _pick(M, MM_TM), _pick(N, MM_TN)
    tk = K if K <= MM_KFULL else _pick(K, MM_TK)
    nk = K // tk

    def body(*refs):
        a_ref, b_ref = refs[0], refs[1]
        o_ref, acc = refs[-2], refs[-1]
        k = pl.program_id(2)

        @pl.when(k == 0)
        def _():
            acc[...] = jnp.zeros_like(acc)

        acc[...] += _dot(a_ref[...].astype(BF16), b_ref[...].astype(BF16), mode)

        @pl.when(k == nk - 1)
        def _():
            r = acc[...]
            if res is not None:
                r = r + refs[2][...].astype(F32)
            o_ref[...] = r.astype(o_ref.dtype)

    a_spec = pl.BlockSpec((tk, tm), lambda i, j, k: (k, i)) if mode == "tn" else pl.BlockSpec((tm, tk), lambda i, j, k: (i, k))
    b_spec = pl.BlockSpec((tn, tk), lambda i, j, k: (j, k)) if mode == "nt" else pl.BlockSpec((tk, tn), lambda i, j, k: (k, j))
    o_spec = pl.BlockSpec((tm, tn), lambda i, j, k: (i, j))
    ins, specs = [a, b], [a_spec, b_spec]
    if res is not None:
        ins.append(res)
        specs.append(o_spec)
    return pl.pallas_call(
        body, name=name, grid=(M // tm, N // tn, nk), in_specs=specs, out_specs=o_spec,
        out_shape=jax.ShapeDtypeStruct((M, N), out_dtype),
        scratch_shapes=[pltpu.VMEM((tm, tn), F32)], compiler_params=_params(),
    )(*ins)


def _piece_spec(tm, piece):
    _, w, c0, per_group = piece
    if per_group:
        return pl.BlockSpec((tm, w), lambda i, g: (i, c0 + g))
    return pl.BlockSpec((tm, w), lambda i, g: (i, c0))


def _const_spec(p):
    return pl.BlockSpec(p.shape, lambda i, g: (0, 0))


def _rowwise(name, fn, params, rows, auxs, outs, tm, groups=1):
    S = rows[0][0].shape[0]
    tm = min(tm, S)
    n_p, n_r, n_a = len(params), len(rows), len(auxs)

    def body(*refs):
        p = [r[...] for r in refs[:n_p]]
        r_ = [r[...] for r in refs[n_p:n_p + n_r]]
        a_ = [r[...] for r in refs[n_p + n_r:n_p + n_r + n_a]]
        for o_ref, o in zip(refs[n_p + n_r + n_a:], fn(p, r_, a_)):
            o_ref[...] = o.astype(o_ref.dtype)

    out_specs, out_shape = [], []
    for w, dt, per_group in outs:
        out_specs.append(_piece_spec(tm, (None, w, 0, per_group)))
        out_shape.append(jax.ShapeDtypeStruct((S, w * (groups if per_group else 1)), dt))
    return pl.pallas_call(
        body, name=name, grid=(S // tm, groups),
        in_specs=[_const_spec(p) for p in params] + [_piece_spec(tm, q) for q in list(rows) + list(auxs)],
        out_specs=out_specs, out_shape=out_shape, compiler_params=_params(),
    )(*params, *[q[0] for q in list(rows) + list(auxs)])


def _rowwise_vjp(name, fn, params, rows, auxs, cots, d_outs, tm, groups=1, adds=None):
    S = rows[0][0].shape[0]
    tm = min(tm, S)
    n_p, n_r, n_a = len(params), len(rows), len(auxs)
    cot_flat = [q for c in cots for q in c]
    adds = adds or [None] * len(d_outs)
    add_flat = [q for q in adds if q is not None]
    n_c, n_add = len(cot_flat), len(add_flat)
    shared = [not all(rows[k][3] for k in idx) and groups > 1 for idx, _ in d_outs]

    def body(*refs):
        pos = 0
        p = [r[...] for r in refs[pos:pos + n_p]]; pos += n_p
        r_ = [r[...] for r in refs[pos:pos + n_r]]; pos += n_r
        a_ = [r[...] for r in refs[pos:pos + n_a]]; pos += n_a
        c_refs = refs[pos:pos + n_c]; pos += n_c
        add_refs = list(refs[pos:pos + n_add]); pos += n_add
        d_refs = refs[pos:pos + len(d_outs)]; pos += len(d_outs)
        dp_refs = refs[pos:]
        i, g = pl.program_id(0), pl.program_id(1)
        outs, vjp_fn = jax.vjp(lambda pp, rr: fn(pp, rr, a_), p, r_)
        cts, ci = [], 0
        for c, o in zip(cots, outs):
            t = c_refs[ci][...].astype(F32)
            for extra in c_refs[ci + 1:ci + len(c)]:
                t = t + extra[...].astype(F32)
            ci += len(c)
            cts.append(t.astype(o.dtype))
        dp, dr = vjp_fn(cts)
        for (idx, _), d_ref, add, sh in zip(d_outs, d_refs, adds, shared):
            val = dr[idx[0]].astype(F32) if len(idx) == 1 else jnp.concatenate([dr[k].astype(F32) for k in idx], axis=1)
            if add is not None:
                val = val + add_refs.pop(0)[...].astype(F32)
            if sh:
                @pl.when(g == 0)
                def _(d_ref=d_ref):
                    d_ref[...] = jnp.zeros_like(d_ref)
                d_ref[...] += val.astype(d_ref.dtype)
            else:
                d_ref[...] = val.astype(d_ref.dtype)
        first = jnp.logical_and(i == 0, g == 0)
        for dp_ref, d in zip(dp_refs, dp):
            @pl.when(first)
            def _(dp_ref=dp_ref):
                dp_ref[...] = jnp.zeros_like(dp_ref)
            dp_ref[...] += d.astype(F32)

    out_specs, out_shape = [], []
    for (idx, dt), sh in zip(d_outs, shared):
        w = sum(rows[k][1] for k in idx)
        per_group = (not sh) and groups > 1
        out_specs.append(_piece_spec(tm, (None, w, 0, per_group)))
        out_shape.append(jax.ShapeDtypeStruct((S, w * (groups if per_group else 1)), dt))
    for p in params:
        out_specs.append(_const_spec(p))
        out_shape.append(jax.ShapeDtypeStruct(p.shape, F32))
    pieces = list(rows) + list(auxs) + cot_flat + add_flat
    res = pl.pallas_call(
        body, name=name, grid=(S // tm, groups),
        in_specs=[_const_spec(p) for p in params] + [_piece_spec(tm, q) for q in pieces],
        out_specs=out_specs, out_shape=out_shape, compiler_params=_params(),
    )(*params, *[q[0] for q in pieces])
    return list(res[:len(d_outs)]), list(res[len(d_outs):])


def _lane_roll(x, shift):
    @jax.custom_vjp
    def roll(v):
        return pltpu.roll(v, shift, 1)

    roll.defvjp(lambda v: (roll(v), None), lambda _, ct: (pltpu.roll(ct, LANES - shift, 1),))
    return roll(x)


@jax.custom_vjp
def _sigmoid(x):
    return 1.0 / (1.0 + jnp.exp(-x))


def _sigmoid_fwd(x):
    s = _sigmoid(x)
    return s, s


_sigmoid.defvjp(_sigmoid_fwd, lambda s, ct: (ct * s * (1.0 - s),))


def _rope(x, cos, sin_lo, sin_hi, half):
    return x * cos + _lane_roll(x, LANES - half) * sin_lo + _lane_roll(x, half) * sin_hi


def _f_rope_table(p, r, a):
    inv, first, second, fixed = p
    ang = a[0] * inv
    cs, sn = jnp.cos(ang), jnp.sin(ang)
    return [cs * (first + second) + fixed, -sn * first, sn * second]


def _f_rms(p, r, a):
    x = r[0].astype(F32)
    return [x * lax.rsqrt(jnp.mean(x * x, axis=-1, keepdims=True) + EPS) * p[0]]


def _f_mla_a(p, r, a):
    return _f_rms([p[0]], [r[0]], a) + _f_rms([p[1]], [r[1]], a)


def _f_mla_b(p, r, a):
    def norm_rope(v, g):
        ms = jnp.sum(v * v, axis=-1, keepdims=True) * (1.0 / MLA_QK)
        return _rope(v * lax.rsqrt(ms + EPS) * g, a[0], a[1], a[2], MLA_ROPE // 2)

    return [norm_rope(r[0].astype(F32), p[0]) * MLA_Q_SCALE, norm_rope(r[1].astype(F32) + r[2].astype(F32), p[1])]


def _f_ret_rope(p, r, a):
    q = _rope(r[0].astype(F32), a[0], a[1], a[2], RET_QK // 2)
    k = _rope(r[1].astype(F32), a[0], a[1], a[2], RET_QK // 2)
    return [q, k * (RET_QK ** -0.5)]


def _f_ret_post(p, r, a):
    ret = r[0].astype(F32) + r[1].astype(F32)
    g = r[2].astype(F32)
    normed = ret * lax.rsqrt(jnp.mean(ret * ret, axis=-1, keepdims=True) + EPS)
    return [g * _sigmoid(g) * normed]


def _f_merge(p, r, a):
    return [_sigmoid(r[0].astype(F32)) * r[2].astype(F32) + _sigmoid(r[1].astype(F32)) * r[3].astype(F32)]


def _f_swiglu(p, r, a):
    g = r[0].astype(F32)
    return [g * _sigmoid(g) * r[1].astype(F32)]


def _f_delta(p, r, a):
    d = jnp.sum(r[0].astype(F32) * r[1].astype(F32), axis=-1, keepdims=True)
    return [jnp.broadcast_to(d, r[0].shape)]


def _f_add(p, r, a):
    return [r[0].astype(F32) + r[1].astype(F32)]


def _loss_kernel(y, tgt):
    S, Dm = y.shape
    tm = min(ROW_TILE, S)

    def body(y_ref, t_ref, dy_ref, loss_ref):
        @pl.when(pl.program_id(0) == 0)
        def _():
            loss_ref[...] = jnp.zeros_like(loss_ref)

        e = y_ref[...] - t_ref[...]
        dy_ref[...] = e * (1.0 / Dm)
        loss_ref[...] += 0.5 * jnp.sum(jnp.mean(e * e, axis=-1, keepdims=True), axis=0, keepdims=True)

    row = pl.BlockSpec((tm, Dm), lambda i: (i, 0))
    return pl.pallas_call(
        body, name="loss", grid=(S // tm,), in_specs=[row, row],
        out_specs=[row, pl.BlockSpec((1, LANES), lambda i: (0, 0))],
        out_shape=[jax.ShapeDtypeStruct((S, Dm), F32), jax.ShapeDtypeStruct((1, LANES), F32)],
        compiler_params=_params(),
    )(y, tgt)


def _flash_fwd(q, k, kv):
    S = q.shape[0]
    tq, tk = min(ATT_TQ, S), min(ATT_TK, S)
    nk = S // tk

    def body(q_ref, k_ref, v_ref, o_ref, lse_ref, m_sc, l_sc, acc_sc):
        ki = pl.program_id(2)

        @pl.when(ki == 0)
        def _():
            m_sc[...] = jnp.full_like(m_sc, -jnp.inf)
            l_sc[...] = jnp.zeros_like(l_sc)
            acc_sc[...] = jnp.zeros_like(acc_sc)

        s = _dot(q_ref[...], k_ref[...], "nt")
        m_prev = m_sc[...]
        m_new = jnp.maximum(m_prev, jnp.max(s, axis=-1, keepdims=True))
        alpha = jnp.exp2(m_prev - m_new)
        p = jnp.exp2(s - m_new[:, :1])
        l_sc[...] = alpha * l_sc[...] + jnp.sum(p, axis=-1, keepdims=True)
        acc_sc[...] = alpha * acc_sc[...] + _dot(p.astype(BF16), v_ref[...])
        m_sc[...] = m_new

        @pl.when(ki == nk - 1)
        def _():
            o_ref[...] = (acc_sc[...] / l_sc[...]).astype(o_ref.dtype)
            lse_ref[...] = m_sc[...] + jnp.log2(l_sc[...])

    qs = pl.BlockSpec((tq, LANES), lambda h, i, j: (i, h))
    return pl.pallas_call(
        body, name="mla_fwd", grid=(HEADS, S // tq, nk),
        in_specs=[qs, pl.BlockSpec((tk, LANES), lambda h, i, j: (j, h)),
                  pl.BlockSpec((tk, LANES), lambda h, i, j: (j, HEADS + h))],
        out_specs=[qs, qs],
        out_shape=[jax.ShapeDtypeStruct((S, HEADS * LANES), BF16), jax.ShapeDtypeStruct((S, HEADS * LANES), F32)],
        scratch_shapes=[pltpu.VMEM((tq, LANES), F32)] * 3, compiler_params=_params(),
    )(q, k, kv)


def _flash_bwd(q, k, kv, do, lse, delta):
    S = q.shape[0]
    tq, tk = min(ATT_BQ, S), min(ATT_BK, S)
    nq = S // tq

    def body(q_ref, k_ref, v_ref, do_ref, lse_ref, dl_ref, dq_ref, dk_ref, dv_ref, dk_sc, dv_sc):
        ki, qi = pl.program_id(1), pl.program_id(2)

        @pl.when(jnp.logical_and(ki == 0, qi == 0))
        def _():
            dq_ref[...] = jnp.zeros_like(dq_ref)

        @pl.when(qi == 0)
        def _():
            dk_sc[...] = jnp.zeros_like(dk_sc)
            dv_sc[...] = jnp.zeros_like(dv_sc)

        qv, kv_, dov = q_ref[...], k_ref[...], do_ref[...]
        p = jnp.exp2(_dot(qv, kv_, "nt") - lse_ref[...][:, :1])
        dp = _dot(dov, v_ref[...], "nt")
        ds = (p * (dp - dl_ref[...][:, :1]) * LN2).astype(BF16)
        dv_sc[...] += _dot(p.astype(BF16), dov, "tn")
        dk_sc[...] += _dot(ds, qv, "tn")
        rows = pl.ds(pl.multiple_of(qi * tq, tq), tq)
        dq_ref[rows, :] += _dot(ds, kv_)

        @pl.when(qi == nq - 1)
        def _():
            dk_ref[...] = dk_sc[...].astype(dk_ref.dtype)
            dv_ref[...] = dv_sc[...].astype(dv_ref.dtype)

    qs = pl.BlockSpec((tq, LANES), lambda h, j, i: (i, h))
    ks = pl.BlockSpec((tk, LANES), lambda h, j, i: (j, h))
    return pl.pallas_call(
        body, name="mla_bwd", grid=(HEADS, S // tk, nq),
        in_specs=[qs, ks, pl.BlockSpec((tk, LANES), lambda h, j, i: (j, HEADS + h)), qs, qs, qs],
        out_specs=[pl.BlockSpec((S, LANES), lambda h, j, i: (0, h)), ks, ks],
        out_shape=[jax.ShapeDtypeStruct((S, HEADS * LANES), F32), jax.ShapeDtypeStruct((S, HEADS * LANES), BF16),
                   jax.ShapeDtypeStruct((S, HEADS * LANES), BF16)],
        scratch_shapes=[pltpu.VMEM((tk, LANES), F32)] * 2, compiler_params=_params(),
    )(q, k, kv, do, lse, delta)


def _ret_tables(decay_row, backward):
    C = RET_CHUNK
    lg = -jnp.exp(decay_row)
    t = lax.broadcasted_iota(jnp.int32, (C, C), 0).astype(F32)
    s = lax.broadcasted_iota(jnp.int32, (C, C), 1).astype(F32)
    ridx = lax.broadcasted_iota(jnp.int32, (C, LANES), 0).astype(F32)
    if backward:
        dist, mask, aw, bw = s - t, s > t, C - ridx, ridx
    else:
        dist, mask, aw, bw = t - s, t >= s, ridx + 1.0, C - 1.0 - ridx
    dist = jnp.maximum(dist, 0.0)
    din = jnp.where(mask, jnp.exp(lg[:, :1] * dist), 0.0)
    return dict(din=din, dist=dist, a=jnp.exp(lg * aw), b=jnp.exp(lg * bw), c=jnp.exp(lg * C), aw=aw, bw=bw)


def _ret_fwd(qr, kr, proj, v_block, dec_f, dec_b):
    S = qr.shape[0]
    C = RET_CHUNK
    n = S // C
    W = HEADS * LANES

    def body(qf, kf, vf, qb, kb, vb, df, db, of, ob, sf_out, sb_out, st):
        @pl.when(pl.program_id(0) == 0)
        def _():
            st[...] = jnp.zeros_like(st)

        for d, (q_ref, k_ref, v_ref, dec, o_ref, s_out) in enumerate(
                [(qf, kf, vf, df, of, sf_out), (qb, kb, vb, db, ob, sb_out)]):
            for h in range(HEADS):
                lanes = slice(h * LANES, (h + 1) * LANES)
                tb = _ret_tables(dec[h:h + 1, :], d == 1)
                qf32, kf32, v = q_ref[:, lanes].astype(F32), k_ref[:, lanes].astype(F32), v_ref[:, lanes]
                state = st[d, h]
                s_out[0, h] = state
                inner = _dot((_dot(qf32.astype(BF16), kf32.astype(BF16), "nt") * tb["din"]).astype(BF16), v)
                cross = _dot((qf32 * tb["a"]).astype(BF16), state.astype(BF16))
                o_ref[:, lanes] = inner + cross
                st[d, h] = state * tb["c"] + _dot((kf32 * tb["b"]).astype(BF16), v, "tn")

    fw = lambda c0: pl.BlockSpec((C, W), lambda j: (j, c0))
    bw = lambda c0: pl.BlockSpec((C, W), lambda j: (n - 1 - j, c0))
    dec_spec = pl.BlockSpec((HEADS, LANES), lambda j: (0, 0))
    st_shape = jax.ShapeDtypeStruct((n, HEADS, LANES, LANES), F32)
    return pl.pallas_call(
        body, name="ret_fwd", grid=(n,),
        in_specs=[fw(0), fw(0), fw(v_block), bw(0), bw(0), bw(v_block), dec_spec, dec_spec],
        out_specs=[fw(0), bw(0), pl.BlockSpec((1, HEADS, LANES, LANES), lambda j: (j, 0, 0, 0)),
                   pl.BlockSpec((1, HEADS, LANES, LANES), lambda j: (n - 1 - j, 0, 0, 0))],
        out_shape=[jax.ShapeDtypeStruct((S, W), F32)] * 2 + [st_shape] * 2,
        scratch_shapes=[pltpu.VMEM((2, HEADS, LANES, LANES), F32)], compiler_params=_params(),
    )(qr, kr, proj, qr, kr, proj, dec_f, dec_b)


def _ret_bwd(qr, kr, proj, v_block, dret, sf, sb, dec_f, dec_b):
    S = qr.shape[0]
    C = RET_CHUNK
    n = S // C
    W = HEADS * LANES

    def body(qf, kf, vf, gf, sf_ref, qb, kb, vb, gb, sb_ref, df, db,
             dqf, dkf, dvf, dqb, dkb, dvb, ddf, ddb, ds_sc):
        j = pl.program_id(0)

        @pl.when(j == 0)
        def _():
            ds_sc[...] = jnp.zeros_like(ds_sc)
            ddf[...] = jnp.zeros_like(ddf)
            ddb[...] = jnp.zeros_like(ddb)

        for d, (q_ref, k_ref, v_ref, g_ref, s_ref, dec, dq_ref, dk_ref, dv_ref, dd_ref) in enumerate(
                [(qf, kf, vf, gf, sf_ref, df, dqf, dkf, dvf, ddf), (qb, kb, vb, gb, sb_ref, db, dqb, dkb, dvb, ddb)]):
            for h in range(HEADS):
                lanes = slice(h * LANES, (h + 1) * LANES)
                tb = _ret_tables(dec[h:h + 1, :], d == 1)
                v, g = v_ref[:, lanes], g_ref[:, lanes]
                qf32, kf32 = q_ref[:, lanes].astype(F32), k_ref[:, lanes].astype(F32)
                q, k = qf32.astype(BF16), kf32.astype(BF16)
                state, dstate = s_ref[0, h], ds_sc[d, h]
                dstate_b = dstate.astype(BF16)
                dp = _dot(g, v, "nt")
                a_ = _dot(q, k, "nt")
                da = (dp * tb["din"]).astype(BF16)
                g1 = _dot(g, state.astype(BF16), "nt")
                g2 = _dot(v, dstate_b, "nt")
                dq_ref[:, lanes] = (_dot(da, k) + g1 * tb["a"]).astype(dq_ref.dtype)
                dk_ref[:, lanes] = (_dot(da, q, "tn") + g2 * tb["b"]).astype(dk_ref.dtype)
                dv_ref[:, lanes] = (_dot((a_ * tb["din"]).astype(BF16), g, "tn")
                                    + _dot((kf32 * tb["b"]).astype(BF16), dstate_b)).astype(dv_ref.dtype)
                dlg = (jnp.sum(dp * a_ * tb["din"] * tb["dist"], keepdims=True)
                       + jnp.sum(g1 * qf32 * tb["a"] * tb["aw"], keepdims=True)
                       + jnp.sum(g2 * kf32 * tb["b"] * tb["bw"], keepdims=True)
                       + C * jnp.sum(tb["c"] * dstate * state, keepdims=True))
                dd_ref[h:h + 1, :] += jnp.broadcast_to(dlg, (1, LANES))
                ds_sc[d, h] = dstate * tb["c"] + _dot((qf32 * tb["a"]).astype(BF16), g, "tn")

        @pl.when(j == n - 1)
        def _():
            ddf[...] = ddf[...] * -jnp.exp(df[...])
            ddb[...] = ddb[...] * -jnp.exp(db[...])

    fw = lambda c0: pl.BlockSpec((C, W), lambda j: (n - 1 - j, c0))
    bw = lambda c0: pl.BlockSpec((C, W), lambda j: (j, c0))
    dec_spec = pl.BlockSpec((HEADS, LANES), lambda j: (0, 0))
    act = jax.ShapeDtypeStruct((S, W), BF16)
    return pl.pallas_call(
        body, name="ret_bwd", grid=(n,),
        in_specs=[fw(0), fw(0), fw(v_block), fw(0), pl.BlockSpec((1, HEADS, LANES, LANES), lambda j: (n - 1 - j, 0, 0, 0)),
                  bw(0), bw(0), bw(v_block), bw(0), pl.BlockSpec((1, HEADS, LANES, LANES), lambda j: (j, 0, 0, 0)),
                  dec_spec, dec_spec],
        out_specs=[fw(0)] * 3 + [bw(0)] * 3 + [dec_spec] * 2,
        out_shape=[act] * 6 + [jax.ShapeDtypeStruct((HEADS, LANES), F32)] * 2,
        scratch_shapes=[pltpu.VMEM((2, HEADS, LANES, LANES), F32)], compiler_params=_params(),
    )(qr, kr, proj, dret, sf, qr, kr, proj, dret, sb, dec_f, dec_b)


def _pad_heads(w, hd):
    K = w.shape[0]
    return jnp.pad(w.reshape(K, HEADS, hd), ((0, 0), (0, 0), (0, LANES - hd))).reshape(K, HEADS * LANES)


def _unpad_heads(w, hd):
    K = w.shape[0]
    return w.reshape(K, HEADS, LANES)[:, :, :hd].reshape(K, HEADS * hd)


def _rope_consts(first_lane, half):
    lane = np.arange(LANES)
    first = ((lane >= first_lane) & (lane < first_lane + half)).astype(np.float32)
    second = ((lane >= first_lane + half) & (lane < first_lane + 2 * half)).astype(np.float32)
    fixed = (lane < first_lane).astype(np.float32)
    j = np.where(first > 0, lane - first_lane, lane - first_lane - half) * (first + second)
    inv = (ROPE_THETA ** (-j.astype(np.float64) / half)).astype(np.float32)
    return [jnp.asarray(v.reshape(1, LANES), F32) for v in (inv, first, second, fixed)]


def _local_step(x, pos, tgt, wts, small):
    w_in = wts["w_in"]
    seg = [w_in[:, IN_OFFS[i]:IN_OFFS[i + 1]] for i in range(8)]
    kr_w = jnp.pad(seg[2], ((0, 0), (MLA_NOPE, LANES - MLA_QK)))
    w_in_p = jnp.concatenate([seg[7], seg[5], seg[6], _pad_heads(seg[3], RET_QK), _pad_heads(seg[4], RET_QK),
                              seg[0], seg[1], kr_w], axis=1)
    w_qb_p = _pad_heads(wts["w_q_b"], MLA_QK)
    kvw = wts["w_kv_b"].reshape(MLA_KV_RANK, HEADS, MLA_NOPE + MLA_V)
    pad_kv = lambda t: jnp.pad(t, ((0, 0), (0, 0), (0, LANES - t.shape[2]))).reshape(MLA_KV_RANK, HEADS * LANES)
    w_kn_p, w_v_p = pad_kv(kvw[:, :, :MLA_NOPE]), pad_kv(kvw[:, :, MLA_NOPE:])
    w_kv_p = jnp.concatenate([w_kn_p, w_v_p], axis=1)
    w_mla_p = jnp.pad(wts["w_mla_out"].reshape(HEADS, MLA_V, D_MODEL), ((0, 0), (0, LANES - MLA_V), (0, 0))).reshape(HEADS * LANES, D_MODEL)
    w_ret_out, w_out, w_gu, w_down = wts["w_ret_out"], wts["w_out"], wts["w_gate_up"], wts["w_down"]
    g_qn_p = jnp.pad(small["g_qn"], ((0, 0), (0, LANES - MLA_QK)))
    g_kn_p = jnp.pad(small["g_kn"], ((0, 0), (0, LANES - MLA_QK)))
    dec_f = jnp.broadcast_to(small["ret_decay_fwd"].reshape(HEADS, 1), (HEADS, LANES))
    dec_b = jnp.broadcast_to(small["ret_decay_bwd"].reshape(HEADS, 1), (HEADS, LANES))
    T, N = True, False
    RT, HT = ROW_TILE, HEAD_ROW_TILE

    tab_m = _rowwise("rope_table_mla", _f_rope_table, _rope_consts(MLA_NOPE, MLA_ROPE // 2), [(pos, 1, 0, N)], [(pos, 1, 0, N)],
                     [(LANES, F32, N)] * 3, HT)
    tab_r = _rowwise("rope_table_ret", _f_rope_table, _rope_consts(0, RET_QK // 2), [(pos, 1, 0, N)], [(pos, 1, 0, N)],
                     [(LANES, F32, N)] * 3, HT)
    aux_m = [(t, LANES, 0, N) for t in tab_m]
    aux_r = [(t, LANES, 0, N) for t in tab_r]

    rows_rms1 = [(x, D_MODEL, 0, N)]
    (h,) = _rowwise("rms_mix", _f_rms, [small["g_mix"]], rows_rms1, [], [(D_MODEL, BF16, N)], RT)
    proj = _mm("proj", h, w_in_p, "nn", BF16)
    rows_a = [(proj, MLA_Q_RANK, 24, N), (proj, MLA_KV_RANK, 50, N)]
    cqn, ckvn = _rowwise("mla_lat_norm", _f_mla_a, [small["g_q_a"], small["g_kv_a"]], rows_a, [],
                         [(MLA_Q_RANK, BF16, N), (MLA_KV_RANK, BF16, N)], RT)
    qraw = _mm("mla_q_up", cqn, w_qb_p, "nn", BF16)
    kv = _mm("mla_kv_up", ckvn, w_kv_p, "nn", BF16)
    rows_b = [(qraw, LANES, 0, T), (kv, LANES, 0, T), (proj, LANES, 51, N)]
    q, k = _rowwise("mla_qk_norm_rope", _f_mla_b, [g_qn_p, g_kn_p], rows_b, aux_m, [(LANES, BF16, T)] * 2, HT, HEADS)
    o, lse = _flash_fwd(q, k, kv)
    y_a = _mm("mla_out", o, w_mla_p, "nn", F32)
    rows_rr = [(proj, LANES, 32, T), (proj, LANES, 40, T)]
    qr, kr = _rowwise("ret_rope", _f_ret_rope, [], rows_rr, aux_r, [(LANES, RET_QK_DTYPE, T)] * 2, HT, HEADS)
    ret_f, ret_b, st_f, st_b = _ret_fwd(qr, kr, proj, 2, dec_f, dec_b)
    rows_rp = [(ret_f, LANES, 0, T), (ret_b, LANES, 0, T), (proj, LANES, 24, T)]
    (o_b,) = _rowwise("ret_post", _f_ret_post, [], rows_rp, [], [(LANES, BF16, T)], HT, HEADS)
    y_b = _mm("ret_out", o_b, w_ret_out, "nn", F32)
    rows_m = [(proj, D_MODEL, 0, N), (proj, D_MODEL, 1, N), (y_a, D_MODEL, 0, N), (y_b, D_MODEL, 0, N)]
    (merged,) = _rowwise("merge", _f_merge, [], rows_m, [], [(D_MODEL, BF16, N)], RT)
    x2 = _mm("mix_out", merged, w_out, "nn", F32, res=x)
    rows_rms2 = [(x2, D_MODEL, 0, N)]
    (h2,) = _rowwise("rms_ffn", _f_rms, [small["g_ffn"]], rows_rms2, [], [(D_MODEL, BF16, N)], RT)
    gu = _mm("ffn_gate_up", h2, w_gu, "nn", BF16)
    rows_sw = [(gu, FFN_HIDDEN, 0, N), (gu, FFN_HIDDEN, 1, N)]
    (act,) = _rowwise("swiglu", _f_swiglu, [], rows_sw, [], [(FFN_HIDDEN, BF16, N)], RT)
    y = _mm("ffn_down", act, w_down, "nn", F32, res=x2)
    dy, loss_row = _loss_kernel(y, tgt)

    dact = _mm("d_act", dy, w_down, "nt", BF16)
    dw_down = _mm("dw_down", act, dy, "tn", F32)
    (dgu,), _ = _rowwise_vjp("swiglu_bwd", _f_swiglu, [], rows_sw, [], [[(dact, FFN_HIDDEN, 0, N)]], [([0, 1], BF16)], RT)
    dh2 = _mm("d_h2", dgu, w_gu, "nt", BF16)
    dw_gu = _mm("dw_gate_up", h2, dgu, "tn", F32)
    (dx2,), (dg_ffn,) = _rowwise_vjp("rms_ffn_bwd", _f_rms, [small["g_ffn"]], rows_rms2, [], [[(dh2, D_MODEL, 0, N)]],
                                     [([0], F32)], RT, adds=[(dy, D_MODEL, 0, N)])
    dmerged = _mm("d_merged", dx2, w_out, "nt", BF16)
    dw_out = _mm("dw_out", merged, dx2, "tn", F32)
    (dgl, dy_a, dy_b), _ = _rowwise_vjp("merge_bwd", _f_merge, [], rows_m, [], [[(dmerged, D_MODEL, 0, N)]],
                                        [([0, 1], BF16), ([2], BF16), ([3], BF16)], RT)
    do_b = _mm("d_ret_o", dy_b, w_ret_out, "nt", BF16)
    dw_ret_out = _mm("dw_ret_out", o_b, dy_b, "tn", F32)
    (dret, dg_r), _ = _rowwise_vjp("ret_post_bwd", _f_ret_post, [], rows_rp, [], [[(do_b, LANES, 0, T)]],
                                   [([0], BF16), ([2], BF16)], HT, HEADS)
    dqf, dkf, dvf, dqb, dkb, dvb, ddec_f, ddec_b = _ret_bwd(qr, kr, proj, 2, dret, st_f, st_b, dec_f, dec_b)
    (dq_r, dk_r), _ = _rowwise_vjp("ret_rope_bwd", _f_ret_rope, [], rows_rr, aux_r,
                                   [[(dqf, LANES, 0, T), (dqb, LANES, 0, T)], [(dkf, LANES, 0, T), (dkb, LANES, 0, T)]],
                                   [([0], BF16), ([1], BF16)], HT, HEADS)
    (dv_r,) = _rowwise("ret_dv_sum", _f_add, [], [(dvf, D_MODEL, 0, N), (dvb, D_MODEL, 0, N)], [], [(D_MODEL, BF16, N)], RT)
    do = _mm("d_mla_o", dy_a, w_mla_p, "nt", BF16)
    dw_mla_p = _mm("dw_mla_out", o, dy_a, "tn", F32)
    (delta,) = _rowwise("mla_delta", _f_delta, [], [(do, LANES, 0, T), (o, LANES, 0, T)], [], [(LANES, F32, T)], HT, HEADS)
    dq, dk, dv = _flash_bwd(q, k, kv, do, lse, delta)
    (dqraw, dkn, dkr), (dg_qn_p, dg_kn_p) = _rowwise_vjp(
        "mla_qk_norm_rope_bwd", _f_mla_b, [g_qn_p, g_kn_p], rows_b, aux_m, [[(dq, LANES, 0, T)], [(dk, LANES, 0, T)]],
        [([0], BF16), ([1], BF16), ([2], F32)], HT, HEADS)
    dckvn = _mm("d_ckvn_v", dv, w_v_p, "nt", BF16, res=_mm("d_ckvn_k", dkn, w_kn_p, "nt", F32))
    dw_kn_p = _mm("dw_kv_k", ckvn, dkn, "tn", F32)
    dw_v_p = _mm("dw_kv_v", ckvn, dv, "tn", F32)
    dcqn = _mm("d_cqn", dqraw, w_qb_p, "nt", BF16)
    dw_qb_p = _mm("dw_q_b", cqn, dqraw, "tn", F32)
    (dcq, dckv), (dg_q_a, dg_kv_a) = _rowwise_vjp(
        "mla_lat_norm_bwd", _f_mla_a, [small["g_q_a"], small["g_kv_a"]], rows_a, [],
        [[(dcqn, MLA_Q_RANK, 0, N)], [(dckvn, MLA_KV_RANK, 0, N)]], [([0], BF16), ([1], BF16)], RT)
    dproj = jnp.concatenate([dgl, dv_r, dg_r, dq_r, dk_r, dcq, dckv, dkr.astype(BF16)], axis=1)
    dh = _mm("d_h", dproj, w_in_p, "nt", BF16)
    dw_in_p = _mm("dw_in", h, dproj, "tn", F32)
    (dx,), (dg_mix,) = _rowwise_vjp("rms_mix_bwd", _f_rms, [small["g_mix"]], rows_rms1, [], [[(dh, D_MODEL, 0, N)]],
                                    [([0], F32)], RT, adds=[(dx2, D_MODEL, 0, N)])

    c = lambda a, b_: dw_in_p[:, a:b_]
    dw_in = jnp.concatenate([c(6144, 6400), c(6400, 6528), c(6528 + MLA_NOPE, 6528 + MLA_QK), _unpad_heads(c(4096, 5120), RET_QK),
                             _unpad_heads(c(5120, 6144), RET_QK), c(2048, 3072), c(3072, 4096), c(0, 2048)], axis=1)
    un_kv = lambda t: t.reshape(MLA_KV_RANK, HEADS, LANES)[:, :, :MLA_NOPE]
    dw_kv = jnp.concatenate([un_kv(dw_kn_p), un_kv(dw_v_p)], axis=2).reshape(MLA_KV_RANK, HEADS * (MLA_NOPE + MLA_V))
    dw_mla = dw_mla_p.reshape(HEADS, LANES, D_MODEL)[:, :MLA_V].reshape(HEADS * MLA_V, D_MODEL)
    grads = {"w_in": dw_in, "w_q_b": _unpad_heads(dw_qb_p, MLA_QK), "w_kv_b": dw_kv, "w_mla_out": dw_mla,
             "w_ret_out": dw_ret_out, "w_out": dw_out, "w_gate_up": dw_gu, "w_down": dw_down}
    sgrads = {"g_mix": dg_mix, "g_q_a": dg_q_a, "g_kv_a": dg_kv_a, "g_qn": dg_qn_p[:, :MLA_QK], "g_kn": dg_kn_p[:, :MLA_QK],
              "ret_decay_fwd": ddec_f[:, 0].reshape(1, HEADS), "ret_decay_bwd": ddec_b[:, 0].reshape(1, HEADS), "g_ffn": dg_ffn}
    return loss_row, dx, grads, sgrads


def _coords():
    return lax.axis_index("x"), lax.axis_index("y"), lax.axis_index("c")


def _other_chips(x, y):
    return [(1 - x, y), (x, 1 - y), (1 - x, 1 - y)]


ANY = pl.BlockSpec(memory_space=pl.ANY)


def _weight_gather(shards):
    n = len(shards)

    def body(*refs):
        ins, outs = refs[:n], refs[n:2 * n]
        local_sems, send_sems, recv_sems = refs[2 * n:]
        x, y, c = _coords()
        chips = _other_chips(x, y)
        mine = 2 * x + y
        started = []
        for w in range(n):
            cp = pltpu.make_async_copy(ins[w], outs[w].at[mine], local_sems.at[w])
            cp.start()
            started.append(cp)
        sends = []
        for w in range(n):
            for j, (cx, cy) in enumerate(chips):
                cp = pltpu.make_async_remote_copy(
                    src_ref=ins[w], dst_ref=outs[w].at[mine], send_sem=send_sems.at[3 * w + j],
                    recv_sem=recv_sems.at[3 * w + j], device_id=(cx, cy, c), device_id_type=MESH)
                cp.start()
                sends.append(cp)
        for w in range(n):
            for j, (cx, cy) in enumerate(chips):
                pltpu.make_async_remote_copy(
                    src_ref=ins[w], dst_ref=outs[w].at[2 * cx + cy], send_sem=send_sems.at[3 * w + j],
                    recv_sem=recv_sems.at[3 * w + j], device_id=(cx, cy, c), device_id_type=MESH).wait_recv()
        for cp in sends:
            cp.wait_send()
        for cp in started:
            cp.wait()

    return pl.pallas_call(
        body, name="weight_gather", in_specs=[ANY] * n, out_specs=[ANY] * n,
        out_shape=[jax.ShapeDtypeStruct((4,) + s.shape, s.dtype) for s in shards],
        scratch_shapes=[pltpu.SemaphoreType.DMA((n,)), pltpu.SemaphoreType.DMA((3 * n,)), pltpu.SemaphoreType.DMA((3 * n,))],
    )(*shards)


def _sibling_exchange(gs):
    n = len(gs)

    def body(*refs):
        ins, outs, send_sems, recv_sems = refs[:n], refs[n:2 * n], refs[2 * n], refs[2 * n + 1]
        x, y, c = _coords()
        cps = []
        for w in range(n):
            cp = pltpu.make_async_remote_copy(
                src_ref=ins[w].at[:, 1 - c], dst_ref=outs[w], send_sem=send_sems.at[w], recv_sem=recv_sems.at[w],
                device_id=(x, y, 1 - c), device_id_type=MESH)
            cp.start()
            cps.append(cp)
        for cp in cps:
            cp.wait()

    return pl.pallas_call(
        body, name="grad_sibling_exchange", in_specs=[ANY] * n, out_specs=[ANY] * n,
        out_shape=[jax.ShapeDtypeStruct((4,) + g.shape[2:], F32) for g in gs],
        scratch_shapes=[pltpu.SemaphoreType.DMA((n,)), pltpu.SemaphoreType.DMA((n,))],
    )(*gs)


def _pair_sum(name, g, got, c_arr):
    _, _, R, W = g.shape
    tr = _pick(R, 256, 8)

    def body(c_ref, a_ref, b_ref, o_ref):
        o_ref[...] = a_ref[0] + b_ref[...]

    return pl.pallas_call(
        body, name=name,
        grid_spec=pltpu.PrefetchScalarGridSpec(
            num_scalar_prefetch=1, grid=(4, R // tr),
            in_specs=[pl.BlockSpec((1, 1, tr, W), lambda j, i, c_ref: (j, c_ref[0], i, 0)),
                      pl.BlockSpec((1, tr, W), lambda j, i, c_ref: (j, i, 0))],
            out_specs=pl.BlockSpec((1, tr, W), lambda j, i, c_ref: (j, i, 0))),
        out_shape=jax.ShapeDtypeStruct((4, R, W), F32), compiler_params=_params(),
    )(c_arr, g, got)


def _chip_exchange(parts):
    n = len(parts)

    def body(*refs):
        ins, outs, send_sems, recv_sems = refs[:n], refs[n:2 * n], refs[2 * n], refs[2 * n + 1]
        x, y, c = _coords()
        sends = []
        for w in range(n):
            for j, (cx, cy) in enumerate(_other_chips(x, y)):
                cp = pltpu.make_async_remote_copy(
                    src_ref=ins[w].at[2 * cx + cy], dst_ref=outs[w].at[j], send_sem=send_sems.at[3 * w + j],
                    recv_sem=recv_sems.at[3 * w + j], device_id=(cx, cy, c), device_id_type=MESH)
                cp.start()
                sends.append(cp)
        for cp in sends:
            cp.wait_recv()
        for cp in sends:
            cp.wait_send()

    return pl.pallas_call(
        body, name="grad_chip_exchange", in_specs=[ANY] * n, out_specs=[ANY] * n,
        out_shape=[jax.ShapeDtypeStruct((3,) + p.shape[1:], F32) for p in parts],
        scratch_shapes=[pltpu.SemaphoreType.DMA((3 * n,)), pltpu.SemaphoreType.DMA((3 * n,))],
    )(*parts)


def _chip_sum(name, part, got, slot_arr):
    _, R, W = part.shape
    tr = _pick(R, 256, 8)

    def body(s_ref, a_ref, b_ref, o_ref):
        o_ref[...] = ((a_ref[0] + b_ref[0]) + b_ref[1]) + b_ref[2]

    return pl.pallas_call(
        body, name=name,
        grid_spec=pltpu.PrefetchScalarGridSpec(
            num_scalar_prefetch=1, grid=(R // tr,),
            in_specs=[pl.BlockSpec((1, tr, W), lambda i, s_ref: (s_ref[0], i, 0)),
                      pl.BlockSpec((3, tr, W), lambda i, s_ref: (0, i, 0))],
            out_specs=pl.BlockSpec((tr, W), lambda i, s_ref: (i, 0))),
        out_shape=jax.ShapeDtypeStruct((R, W), F32), compiler_params=_params(),
    )(slot_arr, part, got)


def _half_exchange(halves):
    n = len(halves)

    def body(*refs):
        ins, outs = refs[:n], refs[n:2 * n]
        local_sems, send_sems, recv_sems = refs[2 * n:]
        x, y, c = _coords()
        local, sends = [], []
        for w in range(n):
            cp = pltpu.make_async_copy(ins[w], outs[w].at[c], local_sems.at[w])
            cp.start()
            local.append(cp)
            cp = pltpu.make_async_remote_copy(
                src_ref=ins[w], dst_ref=outs[w].at[c], send_sem=send_sems.at[w], recv_sem=recv_sems.at[w],
                device_id=(x, y, 1 - c), device_id_type=MESH)
            cp.start()
            sends.append(cp)
        for w in range(n):
            pltpu.make_async_remote_copy(
                src_ref=ins[w], dst_ref=outs[w].at[1 - c], send_sem=send_sems.at[w], recv_sem=recv_sems.at[w],
                device_id=(x, y, 1 - c), device_id_type=MESH).wait_recv()
        for cp in sends:
            cp.wait_send()
        for cp in local:
            cp.wait()

    return pl.pallas_call(
        body, name="grad_half_exchange", in_specs=[ANY] * n, out_specs=[ANY] * n,
        out_shape=[jax.ShapeDtypeStruct((2,) + h.shape, F32) for h in halves],
        scratch_shapes=[pltpu.SemaphoreType.DMA((n,)), pltpu.SemaphoreType.DMA((n,)), pltpu.SemaphoreType.DMA((n,))],
    )(*halves)


def _adamw_math(w, g, m, v):
    m2 = ADAM_B1 * m + (1.0 - ADAM_B1) * g
    v2 = ADAM_B2 * v + (1.0 - ADAM_B2) * (g * g)
    m_hat = m2 / (1.0 - ADAM_B1 ** ADAM_STEP)
    v_hat = v2 / (1.0 - ADAM_B2 ** ADAM_STEP)
    return -ADAM_LR * (m_hat / (jnp.sqrt(v_hat) + ADAM_EPS) + ADAM_WD * w), m2, v2


def _small_allreduce_adamw(pack_g, pack_w, pack_m, pack_v):
    def body(g_ref, w_ref, m_ref, v_ref, sum_ref, d_ref, m_out, v_out, land, send_sems, recv_sems):
        x, y, c = _coords()
        me = 4 * x + 2 * y + c
        land[me] = g_ref[...]
        sends = []
        for k in range(1, 8):
            peer = (x ^ (k >> 2), y ^ ((k >> 1) & 1), c ^ (k & 1))
            cp = pltpu.make_async_remote_copy(
                src_ref=g_ref, dst_ref=land.at[me], send_sem=send_sems.at[k - 1], recv_sem=recv_sems.at[k - 1],
                device_id=peer, device_id_type=MESH)
            cp.start()
            sends.append((cp, peer))
        for k, (cp, peer) in enumerate(sends):
            pltpu.make_async_remote_copy(
                src_ref=g_ref, dst_ref=land.at[4 * peer[0] + 2 * peer[1] + peer[2]], send_sem=send_sems.at[k],
                recv_sem=recv_sems.at[k], device_id=peer, device_id_type=MESH).wait_recv()
        for cp, _ in sends:
            cp.wait_send()
        total = land[0]
        for d in range(1, 8):
            total = total + land[d]
        sum_ref[...] = total
        d_ref[...], m_out[...], v_out[...] = _adamw_math(w_ref[...], total, m_ref[...], v_ref[...])

    vm = pl.BlockSpec(memory_space=pltpu.VMEM)
    shp = jax.ShapeDtypeStruct(pack_g.shape, F32)
    return pl.pallas_call(
        body, name="small_allreduce_adamw", in_specs=[vm] * 4, out_specs=[vm] * 4, out_shape=[shp] * 4,
        scratch_shapes=[pltpu.VMEM((8,) + pack_g.shape, F32), pltpu.SemaphoreType.DMA((7,)), pltpu.SemaphoreType.DMA((7,))],
    )(pack_g, pack_w, pack_m, pack_v)


def _adamw(name, w, g, m, v):
    R, C = w.shape
    tr = _pick(R, 256, 8)

    def body(w_ref, g_ref, m_ref, v_ref, d_out, m_out, v_out):
        d_out[...], m_out[...], v_out[...] = _adamw_math(w_ref[...], g_ref[...], m_ref[...], v_ref[...])

    spec = pl.BlockSpec((tr, C), lambda i: (i, 0))
    return pl.pallas_call(
        body, name=name, grid=(R // tr,), in_specs=[spec] * 4, out_specs=[spec] * 3,
        out_shape=[jax.ShapeDtypeStruct((R, C), F32)] * 3, compiler_params=_params(),
    )(w, g, m, v)


def _pack_small(vals, last):
    flat = jnp.concatenate([v.reshape(-1) for v in vals] + [last.reshape(-1)])
    return jnp.pad(flat, (0, SMALL_ROWS * LANES - flat.shape[0])).reshape(SMALL_ROWS, LANES)


def kernel(x, positions, g_mix, w_in, g_q_a, w_q_b, g_kv_a, w_kv_b, g_qn, g_kn, w_mla_out, ret_decay_fwd, ret_decay_bwd, w_ret_out, w_out, g_ffn, w_gate_up, w_down, loss_target, m_g_mix, m_w_in, m_g_q_a, m_w_q_b, m_g_kv_a, m_w_kv_b, m_g_qn, m_g_kn, m_w_mla_out, m_ret_decay_fwd, m_ret_decay_bwd, m_w_ret_out, m_w_out, m_g_ffn, m_w_gate_up, m_w_down, v_g_mix, v_w_in, v_g_q_a, v_w_q_b, v_g_kv_a, v_w_kv_b, v_g_qn, v_g_kn, v_w_mla_out, v_ret_decay_fwd, v_ret_decay_bwd, v_w_ret_out, v_w_out, v_g_ffn, v_w_gate_up, v_w_down):
    given = dict(locals())
    S = x.shape[1]
    xs, tgt = x.reshape(S, D_MODEL), loss_target.reshape(S, D_MODEL)
    pos = positions.reshape(S, 1).astype(F32)
    cx, cy, cc = _coords()

    gathered = _weight_gather([given[n].astype(BF16) for n in SHARDED])
    wts = {}
    for n, g in zip(SHARDED, gathered):
        if n in COL_SHARDED:
            wts[n] = jnp.transpose(g, (1, 0, 2)).reshape(g.shape[1], 4 * g.shape[2])
        else:
            wts[n] = g.reshape(4 * g.shape[1], g.shape[2])
    small = {n: given[n].reshape(1, -1) for n in SMALL}

    loss_row, dx, grads, sgrads = _local_step(xs, pos, tgt, wts, small)

    gs = []
    for n in SHARDED:
        g = grads[n]
        if n in COL_SHARDED:
            K, N4 = g.shape
            gs.append(jnp.transpose(g.reshape(2, K // 2, 4, N4 // 4), (2, 0, 1, 3)))
        else:
            gs.append(g.reshape(4, 2, g.shape[0] // 8, g.shape[1]))
    c_arr = cc.reshape(1).astype(jnp.int32)
    slot_arr = (2 * cx + cy).reshape(1).astype(jnp.int32)
    pairs = [_pair_sum("grad_pair_sum_" + n, g, got, c_arr) for n, g, got in zip(SHARDED, gs, _sibling_exchange(gs))]
    halves = [_chip_sum("grad_chip_sum_" + n, p, got, slot_arr) for n, p, got in zip(SHARDED, pairs, _chip_exchange(pairs))]
    reduced = _half_exchange(halves)

    out = {}
    for n, r in zip(SHARDED, reduced):
        g = r.reshape(given[n].shape)
        out["grad_" + n] = g
        out["delta_" + n], out["new_m_" + n], out["new_v_" + n] = _adamw("adamw_" + n, given[n], g, given["m_" + n], given["v_" + n])

    one = jnp.ones((1,), F32)
    pk = _small_allreduce_adamw(
        _pack_small([sgrads[n] for n in SMALL], loss_row[0, :1]),
        _pack_small([given[n] for n in SMALL], 0 * one),
        _pack_small([given["m_" + n] for n in SMALL], 0 * one),
        _pack_small([given["v_" + n] for n in SMALL], one))
    off = 0
    for n in SMALL:
        sz = given[n].shape[0]
        for pre, arr in zip(["grad_", "delta_", "new_m_", "new_v_"], pk):
            out[pre + n] = arr.reshape(-1)[off:off + sz]
        off += sz
    loss = pk[0].reshape(-1)[off]

    return (loss, dx.reshape(x.shape), *[out["grad_" + n] for n in WEIGHTS], *[out["delta_" + n] for n in WEIGHTS],
            *[out["new_m_" + n] for n in WEIGHTS], *[out["new_v_" + n] for n in WEIGHTS])
```

```python
import functools
import math

import numpy as np
import jax
import jax.numpy as jnp
from jax import lax
from jax.experimental import pallas as pl
from jax.experimental.pallas import tpu as pltpu

F32 = jnp.float32
BF16 = jnp.bfloat16
MESH = pl.DeviceIdType.MESH

D_MODEL = 1024
HEADS = 8
LANES = 128
MLA_Q_RANK, MLA_KV_RANK = 256, 128
MLA_NOPE, MLA_ROPE, MLA_V = 64, 32, 64
MLA_QK = MLA_NOPE + MLA_ROPE
LN2 = math.log(2.0)
MLA_Q_SCALE = MLA_QK ** -0.5 / LN2
RET_QK, RET_V, RET_CHUNK = 64, 128, 128
RET_QK_DTYPE = F32
FFN_HIDDEN = 2816
ROPE_THETA = 10000.0
EPS = 1e-6
IN_SPLITS = [256, 128, 32, 512, 512, 1024, 1024, 2048]
IN_OFFS = [0] + list(np.cumsum(IN_SPLITS))
ADAM_LR, ADAM_B1, ADAM_B2, ADAM_EPS, ADAM_WD, ADAM_STEP = 0.001, 0.9, 0.999, 1e-08, 0.01, 10

VMEM_LIMIT = 56 * 1024 * 1024
ROW_TILE = 256
HEAD_ROW_TILE = 1024
MM_TM, MM_TN, MM_TK, MM_KFULL = 1024, 2048, 2048, 2816
ATT_TQ, ATT_TK = 512, 2048
ATT_BQ, ATT_BK = 1024, 1024

SHARDED = ["w_in", "w_q_b", "w_kv_b", "w_mla_out", "w_ret_out", "w_out", "w_gate_up", "w_down"]
COL_SHARDED = {"w_in", "w_q_b", "w_kv_b", "w_mla_out", "w_gate_up"}
FIRST = ["w_in", "w_q_b", "w_kv_b"]
LATE = ["w_mla_out", "w_ret_out", "w_out", "w_gate_up", "w_down"]
SMALL = ["g_mix", "g_q_a", "g_kv_a", "g_qn", "g_kn", "ret_decay_fwd", "ret_decay_bwd", "g_ffn"]
WEIGHTS = ["g_mix", "w_in", "g_q_a", "w_q_b", "g_kv_a", "w_kv_b", "g_qn", "g_kn", "w_mla_out",
           "ret_decay_fwd", "ret_decay_bwd", "w_ret_out", "w_out", "g_ffn", "w_gate_up", "w_down"]
SMALL_ROWS = 24


def _params(**kw):
    return pltpu.CompilerParams(vmem_limit_bytes=VMEM_LIMIT, **kw)


def _pick(dim, target, unit=128):
    if dim <= target:
        return dim
    best = None
    for d in range(unit, target + 1, unit):
        if dim % d == 0:
            best = d
    assert best is not None, (dim, target)
    return best


_DOT = {"nn": (((1,), (0,)), ((), ())), "nt": (((1,), (1,)), ((), ())), "tn": (((0,), (0,)), ((), ()))}


def _dot(a, b, mode="nn"):
    return lax.dot_general(a, b, _DOT[mode], preferred_element_type=F32)


def _mm(name, a, b, mode, out_dtype, res=None):
    if mode == "nn":
        (M, K), (K2, N) = a.shape, b.shape
    elif mode == "nt":
        (M, K), (N, K2) = a.shape, b.shape
    else:
        (K, M), (K2, N) = a.shape, b.shape
    assert K == K2, (name, a.shape, b.shape)
    tm, tn = _pick(M, MM_TM), _pick(N, MM_TN)
    tk = K if K <= MM_KFULL else _pick(K, MM_TK)
    nk = K // tk

    def body(*refs):
        a_ref, b_ref = refs[0], refs[1]
        o_ref, acc = refs[-2], refs[-1]
        k = pl.program_id(2)

        @pl.when(k == 0)
        def _():
            acc[...] = jnp.zeros_like(acc)

        acc[...] += _dot(a_ref[...].astype(BF16), b_ref[...].astype(BF16), mode)

        @pl.when(k == nk - 1)
        def _():
            r = acc[...]
            if res is not None:
                r = r + refs[2][...].astype(F32)
            o_ref[...] = r.astype(o_ref.dtype)

    a_spec = pl.BlockSpec((tk, tm), lambda i, j, k: (k, i)) if mode == "tn" else pl.BlockSpec((tm, tk), lambda i, j, k: (i, k))
    b_spec = pl.BlockSpec((tn, tk), lambda i, j, k: (j, k)) if mode == "nt" else pl.BlockSpec((tk, tn), lambda i, j, k: (k, j))
    o_spec = pl.BlockSpec((tm, tn), lambda i, j, k: (i, j))
    ins, specs = [a, b], [a_spec, b_spec]
    if res is not None:
        ins.append(res)
        specs.append(o_spec)
    return pl.pallas_call(
        body, name=name, grid=(M // tm, N // tn, nk), in_specs=specs, out_specs=o_spec,
        out_shape=jax.ShapeDtypeStruct((M, N), out_dtype),
        scratch_shapes=[pltpu.VMEM((tm, tn), F32)], compiler_params=_params(),
    )(*ins)


def _piece_spec(tm, piece):
    _, w, c0, per_group = piece
    if per_group:
        return pl.BlockSpec((tm, w), lambda i, g: (i, c0 + g))
    return pl.BlockSpec((tm, w), lambda i, g: (i, c0))


def _const_spec(p):
    return pl.BlockSpec(p.shape, lambda i, g: (0, 0))


def _rowwise(name, fn, params, rows, auxs, outs, tm, groups=1):
    S = rows[0][0].shape[0]
    tm = min(tm, S)
    n_p, n_r, n_a = len(params), len(rows), len(auxs)

    def body(*refs):
        p = [r[...] for r in refs[:n_p]]
        r_ = [r[...] for r in refs[n_p:n_p + n_r]]
        a_ = [r[...] for r in refs[n_p + n_r:n_p + n_r + n_a]]
        for o_ref, o in zip(refs[n_p + n_r + n_a:], fn(p, r_, a_)):
            o_ref[...] = o.astype(o_ref.dtype)

    out_specs, out_shape = [], []
    for w, dt, per_group in outs:
        out_specs.append(_piece_spec(tm, (None, w, 0, per_group)))
        out_shape.append(jax.ShapeDtypeStruct((S, w * (groups if per_group else 1)), dt))
    return pl.pallas_call(
        body, name=name, grid=(S // tm, groups),
        in_specs=[_const_spec(p) for p in params] + [_piece_spec(tm, q) for q in list(rows) + list(auxs)],
        out_specs=out_specs, out_shape=out_shape, compiler_params=_params(),
    )(*params, *[q[0] for q in list(rows) + list(auxs)])


def _rowwise_vjp(name, fn, params, rows, auxs, cots, d_outs, tm, groups=1, adds=None):
    S = rows[0][0].shape[0]
    tm = min(tm, S)
    n_p, n_r, n_a = len(params), len(rows), len(auxs)
    cot_flat = [q for c in cots for q in c]
    adds = adds or [None] * len(d_outs)
    add_flat = [q for q in adds if q is not None]
    n_c, n_add = len(cot_flat), len(add_flat)
    shared = [not all(rows[k][3] for k in idx) and groups > 1 for idx, _ in d_outs]

    def body(*refs):
        pos = 0
        p = [r[...] for r in refs[pos:pos + n_p]]; pos += n_p
        r_ = [r[...] for r in refs[pos:pos + n_r]]; pos += n_r
        a_ = [r[...] for r in refs[pos:pos + n_a]]; pos += n_a
        c_refs = refs[pos:pos + n_c]; pos += n_c
        add_refs = list(refs[pos:pos + n_add]); pos += n_add
        d_refs = refs[pos:pos + len(d_outs)]; pos += len(d_outs)
        dp_refs = refs[pos:]
        i, g = pl.program_id(0), pl.program_id(1)
        outs, vjp_fn = jax.vjp(lambda pp, rr: fn(pp, rr, a_), p, r_)
        cts, ci = [], 0
        for c, o in zip(cots, outs):
            t = c_refs[ci][...].astype(F32)
            for extra in c_refs[ci + 1:ci + len(c)]:
                t = t + extra[...].astype(F32)
            ci += len(c)
            cts.append(t.astype(o.dtype))
        dp, dr = vjp_fn(cts)
        for (idx, _), d_ref, add, sh in zip(d_outs, d_refs, adds, shared):
            val = dr[idx[0]].astype(F32) if len(idx) == 1 else jnp.concatenate([dr[k].astype(F32) for k in idx], axis=1)
            if add is not None:
                val = val + add_refs.pop(0)[...].astype(F32)
            if sh:
                @pl.when(g == 0)
                def _(d_ref=d_ref):
                    d_ref[...] = jnp.zeros_like(d_ref)
                d_ref[...] += val.astype(d_ref.dtype)
            else:
                d_ref[...] = val.astype(d_ref.dtype)
        first = jnp.logical_and(i == 0, g == 0)
        for dp_ref, d in zip(dp_refs, dp):
            @pl.when(first)
            def _(dp_ref=dp_ref):
                dp_ref[...] = jnp.zeros_like(dp_ref)
            dp_ref[...] += d.astype(F32)

    out_specs, out_shape = [], []
    for (idx, dt), sh in zip(d_outs, shared):
        w = sum(rows[k][1] for k in idx)
        per_group = (not sh) and groups > 1
        out_specs.append(_piece_spec(tm, (None, w, 0, per_group)))
        out_shape.append(jax.ShapeDtypeStruct((S, w * (groups if per_group else 1)), dt))
    for p in params:
        out_specs.append(_const_spec(p))
        out_shape.append(jax.ShapeDtypeStruct(p.shape, F32))
    pieces = list(rows) + list(auxs) + cot_flat + add_flat
    res = pl.pallas_call(
        body, name=name, grid=(S // tm, groups),
        in_specs=[_const_spec(p) for p in params] + [_piece_spec(tm, q) for q in pieces],
        out_specs=out_specs, out_shape=out_shape, compiler_params=_params(),
    )(*params, *[q[0] for q in pieces])
    return list(res[:len(d_outs)]), list(res[len(d_outs):])


def _lane_roll(x, shift):
    @jax.custom_vjp
    def roll(v):
        return pltpu.roll(v, shift, 1)

    roll.defvjp(lambda v: (roll(v), None), lambda _, ct: (pltpu.roll(ct, LANES - shift, 1),))
    return roll(x)


@jax.custom_vjp
def _sigmoid(x):
    return 1.0 / (1.0 + jnp.exp(-x))


def _sigmoid_fwd(x):
    s = _sigmoid(x)
    return s, s


_sigmoid.defvjp(_sigmoid_fwd, lambda s, ct: (ct * s * (1.0 - s),))


def _rope(x, cos, sin_lo, sin_hi, half):
    return x * cos + _lane_roll(x, LANES - half) * sin_lo + _lane_roll(x, half) * sin_hi


def _f_rope_table(p, r, a):
    inv, first, second, fixed = p
    ang = a[0] * inv
    cs, sn = jnp.cos(ang), jnp.sin(ang)
    return [cs * (first + second) + fixed, -sn * first, sn * second]


def _f_rms(p, r, a):
    x = r[0].astype(F32)
    return [x * lax.rsqrt(jnp.mean(x * x, axis=-1, keepdims=True) + EPS) * p[0]]


def _f_mla_a(p, r, a):
    return _f_rms([p[0]], [r[0]], a) + _f_rms([p[1]], [r[1]], a)


def _f_mla_b(p, r, a):
    def norm_rope(v, g):
        ms = jnp.sum(v * v, axis=-1, keepdims=True) * (1.0 / MLA_QK)
        return _rope(v * lax.rsqrt(ms + EPS) * g, a[0], a[1], a[2], MLA_ROPE // 2)

    return [norm_rope(r[0].astype(F32), p[0]) * MLA_Q_SCALE, norm_rope(r[1].astype(F32) + r[2].astype(F32), p[1])]


def _f_ret_rope(p, r, a):
    q = _rope(r[0].astype(F32), a[0], a[1], a[2], RET_QK // 2)
    k = _rope(r[1].astype(F32), a[0], a[1], a[2], RET_QK // 2)
    return [q, k * (RET_QK ** -0.5)]


def _f_ret_post(p, r, a):
    ret = r[0].astype(F32) + r[1].astype(F32)
    g = r[2].astype(F32)
    normed = ret * lax.rsqrt(jnp.mean(ret * ret, axis=-1, keepdims=True) + EPS)
    return [g * _sigmoid(g) * normed]


def _f_merge(p, r, a):
    return [_sigmoid(r[0].astype(F32)) * r[2].astype(F32) + _sigmoid(r[1].astype(F32)) * r[3].astype(F32)]


def _f_swiglu(p, r, a):
    g = r[0].astype(F32)
    return [g * _sigmoid(g) * r[1].astype(F32)]


def _f_delta(p, r, a):
    d = jnp.sum(r[0].astype(F32) * r[1].astype(F32), axis=-1, keepdims=True)
    return [jnp.broadcast_to(d, r[0].shape)]


def _f_add(p, r, a):
    return [r[0].astype(F32) + r[1].astype(F32)]


def _loss_kernel(y, tgt):
    S, Dm = y.shape
    tm = min(ROW_TILE, S)

    def body(y_ref, t_ref, dy_ref, loss_ref):
        @pl.when(pl.program_id(0) == 0)
        def _():
            loss_ref[...] = jnp.zeros_like(loss_ref)

        e = y_ref[...] - t_ref[...]
        dy_ref[...] = e * (1.0 / Dm)
        loss_ref[...] += 0.5 * jnp.sum(jnp.mean(e * e, axis=-1, keepdims=True), axis=0, keepdims=True)

    row = pl.BlockSpec((tm, Dm), lambda i: (i, 0))
    return pl.pallas_call(
        body, name="loss", grid=(S // tm,), in_specs=[row, row],
        out_specs=[row, pl.BlockSpec((1, LANES), lambda i: (0, 0))],
        out_shape=[jax.ShapeDtypeStruct((S, Dm), F32), jax.ShapeDtypeStruct((1, LANES), F32)],
        compiler_params=_params(),
    )(y, tgt)


def _flash_fwd(q, k, kv, shards):
    S = q.shape[0]
    tq, tk = min(ATT_TQ, S), min(ATT_TK, S)
    nq, nk = S // tq, S // tk
    n = len(shards)

    def body(q_ref, k_ref, v_ref, *rest):
        shard_refs, (o_ref, lse_ref), gathered = rest[:n], rest[n:n + 2], rest[n + 2:2 * n + 2]
        m_sc, l_sc, acc_sc, local_sems, send_sems, recv_sems = rest[2 * n + 2:]
        h, qi, ki = pl.program_id(0), pl.program_id(1), pl.program_id(2)

        @pl.when(jnp.logical_and(h == 0, jnp.logical_and(qi == 0, ki == 0)))
        def _():
            _gather_start(_gather_copies(shard_refs, gathered, local_sems, send_sems, recv_sems))

        @pl.when(ki == 0)
        def _():
            m_sc[...] = jnp.full_like(m_sc, -jnp.inf)
            l_sc[...] = jnp.zeros_like(l_sc)
            acc_sc[...] = jnp.zeros_like(acc_sc)

        s = _dot(q_ref[...], k_ref[...], "nt")
        m_prev = m_sc[...]
        m_new = jnp.maximum(m_prev, jnp.max(s, axis=-1, keepdims=True))
        alpha = jnp.exp2(m_prev - m_new)
        p = jnp.exp2(s - m_new[:, :1])
        l_sc[...] = alpha * l_sc[...] + jnp.sum(p, axis=-1, keepdims=True)
        acc_sc[...] = alpha * acc_sc[...] + _dot(p.astype(BF16), v_ref[...])
        m_sc[...] = m_new

        @pl.when(ki == nk - 1)
        def _():
            o_ref[...] = (acc_sc[...] / l_sc[...]).astype(o_ref.dtype)
            lse_ref[...] = m_sc[...] + jnp.log2(l_sc[...])

        @pl.when(jnp.logical_and(h == HEADS - 1, jnp.logical_and(qi == nq - 1, ki == nk - 1)))
        def _():
            _gather_wait(_gather_copies(shard_refs, gathered, local_sems, send_sems, recv_sems))

    qs = pl.BlockSpec((tq, LANES), lambda h, i, j: (i, h))
    res = pl.pallas_call(
        body, name="mla_fwd", grid=(HEADS, nq, nk),
        in_specs=[qs, pl.BlockSpec((tk, LANES), lambda h, i, j: (j, h)),
                  pl.BlockSpec((tk, LANES), lambda h, i, j: (j, HEADS + h))] + [ANY] * n,
        out_specs=[qs, qs] + [ANY] * n,
        out_shape=[jax.ShapeDtypeStruct((S, HEADS * LANES), BF16), jax.ShapeDtypeStruct((S, HEADS * LANES), F32)]
        + [jax.ShapeDtypeStruct((4,) + s.shape, s.dtype) for s in shards],
        scratch_shapes=[pltpu.VMEM((tq, LANES), F32)] * 3
        + [pltpu.SemaphoreType.DMA((n,)), pltpu.SemaphoreType.DMA((3 * n,)), pltpu.SemaphoreType.DMA((3 * n,))],
        compiler_params=_params(),
    )(q, k, kv, *shards)
    return res[0], res[1], list(res[2:])


def _flash_bwd(q, k, kv, do, lse, delta, gs):
    S = q.shape[0]
    tq, tk = min(ATT_BQ, S), min(ATT_BK, S)
    nq, nkt = S // tq, S // tk
    n = len(gs)

    def body(q_ref, k_ref, v_ref, do_ref, lse_ref, dl_ref, *rest):
        g_refs, (dq_ref, dk_ref, dv_ref), got_refs = rest[:n], rest[n:n + 3], rest[n + 3:2 * n + 3]
        dk_sc, dv_sc, local_sems, send_sems, recv_sems = rest[2 * n + 3:]
        h, ki, qi = pl.program_id(0), pl.program_id(1), pl.program_id(2)

        @pl.when(jnp.logical_and(h == 0, jnp.logical_and(ki == 0, qi == 0)))
        def _():
            _scatter_start(_scatter_copies(g_refs, got_refs, local_sems, send_sems, recv_sems))

        @pl.when(jnp.logical_and(ki == 0, qi == 0))
        def _():
            dq_ref[...] = jnp.zeros_like(dq_ref)

        @pl.when(qi == 0)
        def _():
            dk_sc[...] = jnp.zeros_like(dk_sc)
            dv_sc[...] = jnp.zeros_like(dv_sc)

        qv, kv_, dov = q_ref[...], k_ref[...], do_ref[...]
        p = jnp.exp2(_dot(qv, kv_, "nt") - lse_ref[...][:, :1])
        dp = _dot(dov, v_ref[...], "nt")
        ds = (p * (dp - dl_ref[...][:, :1]) * LN2).astype(BF16)
        dv_sc[...] += _dot(p.astype(BF16), dov, "tn")
        dk_sc[...] += _dot(ds, qv, "tn")
        rows = pl.ds(pl.multiple_of(qi * tq, tq), tq)
        dq_ref[rows, :] += _dot(ds, kv_)

        @pl.when(qi == nq - 1)
        def _():
            dk_ref[...] = dk_sc[...].astype(dk_ref.dtype)
            dv_ref[...] = dv_sc[...].astype(dv_ref.dtype)

        @pl.when(jnp.logical_and(h == HEADS - 1, jnp.logical_and(ki == nkt - 1, qi == nq - 1)))
        def _():
            _scatter_wait(_scatter_copies(g_refs, got_refs, local_sems, send_sems, recv_sems))

    qs = pl.BlockSpec((tq, LANES), lambda h, j, i: (i, h))
    ks = pl.BlockSpec((tk, LANES), lambda h, j, i: (j, h))
    res = pl.pallas_call(
        body, name="mla_bwd", grid=(HEADS, nkt, nq),
        in_specs=[qs, ks, pl.BlockSpec((tk, LANES), lambda h, j, i: (j, HEADS + h)), qs, qs, qs] + [ANY] * n,
        out_specs=[pl.BlockSpec((S, LANES), lambda h, j, i: (0, h)), ks, ks] + [ANY] * n,
        out_shape=[jax.ShapeDtypeStruct((S, HEADS * LANES), F32), jax.ShapeDtypeStruct((S, HEADS * LANES), BF16),
                   jax.ShapeDtypeStruct((S, HEADS * LANES), BF16)]
        + [jax.ShapeDtypeStruct((8,) + g.shape[2:], g.dtype) for g in gs],
        scratch_shapes=[pltpu.VMEM((tk, LANES), F32)] * 2
        + [pltpu.SemaphoreType.DMA((n,)), pltpu.SemaphoreType.DMA((7 * n,)), pltpu.SemaphoreType.DMA((7 * n,))],
        compiler_params=_params(),
    )(q, k, kv, do, lse, delta, *gs)
    return res[0], res[1], res[2], list(res[3:])


def _ret_tables(decay_row, backward):
    C = RET_CHUNK
    lg = -jnp.exp(decay_row)
    t = lax.broadcasted_iota(jnp.int32, (C, C), 0).astype(F32)
    s = lax.broadcasted_iota(jnp.int32, (C, C), 1).astype(F32)
    ridx = lax.broadcasted_iota(jnp.int32, (C, LANES), 0).astype(F32)
    if backward:
        dist, mask, aw, bw = s - t, s > t, C - ridx, ridx
    else:
        dist, mask, aw, bw = t - s, t >= s, ridx + 1.0, C - 1.0 - ridx
    dist = jnp.maximum(dist, 0.0)
    din = jnp.where(mask, jnp.exp(lg[:, :1] * dist), 0.0)
    return dict(din=din, dist=dist, a=jnp.exp(lg * aw), b=jnp.exp(lg * bw), c=jnp.exp(lg * C), aw=aw, bw=bw)


def _ret_fwd(qr, kr, proj, v_block, dec_f, dec_b):
    S = qr.shape[0]
    C = RET_CHUNK
    n = S // C
    W = HEADS * LANES

    def body(qf, kf, vf, qb, kb, vb, df, db, of, ob, sf_out, sb_out, st):
        @pl.when(pl.program_id(0) == 0)
        def _():
            st[...] = jnp.zeros_like(st)

        for d, (q_ref, k_ref, v_ref, dec, o_ref, s_out) in enumerate(
                [(qf, kf, vf, df, of, sf_out), (qb, kb, vb, db, ob, sb_out)]):
            for h in range(HEADS):
                lanes = slice(h * LANES, (h + 1) * LANES)
                tb = _ret_tables(dec[h:h + 1, :], d == 1)
                qf32, kf32, v = q_ref[:, lanes].astype(F32), k_ref[:, lanes].astype(F32), v_ref[:, lanes]
                state = st[d, h]
                s_out[0, h] = state
                inner = _dot((_dot(qf32.astype(BF16), kf32.astype(BF16), "nt") * tb["din"]).astype(BF16), v)
                cross = _dot((qf32 * tb["a"]).astype(BF16), state.astype(BF16))
                o_ref[:, lanes] = inner + cross
                st[d, h] = state * tb["c"] + _dot((kf32 * tb["b"]).astype(BF16), v, "tn")

    fw = lambda c0: pl.BlockSpec((C, W), lambda j: (j, c0))
    bw = lambda c0: pl.BlockSpec((C, W), lambda j: (n - 1 - j, c0))
    dec_spec = pl.BlockSpec((HEADS, LANES), lambda j: (0, 0))
    st_shape = jax.ShapeDtypeStruct((n, HEADS, LANES, LANES), F32)
    return pl.pallas_call(
        body, name="ret_fwd", grid=(n,),
        in_specs=[fw(0), fw(0), fw(v_block), bw(0), bw(0), bw(v_block), dec_spec, dec_spec],
        out_specs=[fw(0), bw(0), pl.BlockSpec((1, HEADS, LANES, LANES), lambda j: (j, 0, 0, 0)),
                   pl.BlockSpec((1, HEADS, LANES, LANES), lambda j: (n - 1 - j, 0, 0, 0))],
        out_shape=[jax.ShapeDtypeStruct((S, W), F32)] * 2 + [st_shape] * 2,
        scratch_shapes=[pltpu.VMEM((2, HEADS, LANES, LANES), F32)], compiler_params=_params(),
    )(qr, kr, proj, qr, kr, proj, dec_f, dec_b)


def _ret_bwd(qr, kr, proj, v_block, dret, sf, sb, dec_f, dec_b):
    S = qr.shape[0]
    C = RET_CHUNK
    n = S // C
    W = HEADS * LANES

    def body(qf, kf, vf, gf, sf_ref, qb, kb, vb, gb, sb_ref, df, db,
             dqf, dkf, dvf, dqb, dkb, dvb, ddf, ddb, ds_sc):
        j = pl.program_id(0)

        @pl.when(j == 0)
        def _():
            ds_sc[...] = jnp.zeros_like(ds_sc)
            ddf[...] = jnp.zeros_like(ddf)
            ddb[...] = jnp.zeros_like(ddb)

        for d, (q_ref, k_ref, v_ref, g_ref, s_ref, dec, dq_ref, dk_ref, dv_ref, dd_ref) in enumerate(
                [(qf, kf, vf, gf, sf_ref, df, dqf, dkf, dvf, ddf), (qb, kb, vb, gb, sb_ref, db, dqb, dkb, dvb, ddb)]):
            for h in range(HEADS):
                lanes = slice(h * LANES, (h + 1) * LANES)
                tb = _ret_tables(dec[h:h + 1, :], d == 1)
                v, g = v_ref[:, lanes], g_ref[:, lanes]
                qf32, kf32 = q_ref[:, lanes].astype(F32), k_ref[:, lanes].astype(F32)
                q, k = qf32.astype(BF16), kf32.astype(BF16)
                state, dstate = s_ref[0, h], ds_sc[d, h]
                dstate_b = dstate.astype(BF16)
                dp = _dot(g, v, "nt")
                a_ = _dot(q, k, "nt")
                da = (dp * tb["din"]).astype(BF16)
                g1 = _dot(g, state.astype(BF16), "nt")
                g2 = _dot(v, dstate_b, "nt")
                dq_ref[:, lanes] = (_dot(da, k) + g1 * tb["a"]).astype(dq_ref.dtype)
                dk_ref[:, lanes] = (_dot(da, q, "tn") + g2 * tb["b"]).astype(dk_ref.dtype)
                dv_ref[:, lanes] = (_dot((a_ * tb["din"]).astype(BF16), g, "tn")
                                    + _dot((kf32 * tb["b"]).astype(BF16), dstate_b)).astype(dv_ref.dtype)
                dlg = (jnp.sum(dp * a_ * tb["din"] * tb["dist"], keepdims=True)
                       + jnp.sum(g1 * qf32 * tb["a"] * tb["aw"], keepdims=True)
                       + jnp.sum(g2 * kf32 * tb["b"] * tb["bw"], keepdims=True)
                       + C * jnp.sum(tb["c"] * dstate * state, keepdims=True))
                dd_ref[h:h + 1, :] += jnp.broadcast_to(dlg, (1, LANES))
                ds_sc[d, h] = dstate * tb["c"] + _dot((qf32 * tb["a"]).astype(BF16), g, "tn")

        @pl.when(j == n - 1)
        def _():
            ddf[...] = ddf[...] * -jnp.exp(df[...])
            ddb[...] = ddb[...] * -jnp.exp(db[...])

    fw = lambda c0: pl.BlockSpec((C, W), lambda j: (n - 1 - j, c0))
    bw = lambda c0: pl.BlockSpec((C, W), lambda j: (j, c0))
    dec_spec = pl.BlockSpec((HEADS, LANES), lambda j: (0, 0))
    act = jax.ShapeDtypeStruct((S, W), BF16)
    return pl.pallas_call(
        body, name="ret_bwd", grid=(n,),
        in_specs=[fw(0), fw(0), fw(v_block), fw(0), pl.BlockSpec((1, HEADS, LANES, LANES), lambda j: (n - 1 - j, 0, 0, 0)),
                  bw(0), bw(0), bw(v_block), bw(0), pl.BlockSpec((1, HEADS, LANES, LANES), lambda j: (j, 0, 0, 0)),
                  dec_spec, dec_spec],
        out_specs=[fw(0)] * 3 + [bw(0)] * 3 + [dec_spec] * 2,
        out_shape=[act] * 6 + [jax.ShapeDtypeStruct((HEADS, LANES), F32)] * 2,
        scratch_shapes=[pltpu.VMEM((2, HEADS, LANES, LANES), F32)], compiler_params=_params(),
    )(qr, kr, proj, dret, sf, qr, kr, proj, dret, sb, dec_f, dec_b)


def _pad_heads(w, hd):
    K = w.shape[0]
    return jnp.pad(w.reshape(K, HEADS, hd), ((0, 0), (0, 0), (0, LANES - hd))).reshape(K, HEADS * LANES)


def _unpad_heads(w, hd):
    K = w.shape[0]
    return w.reshape(K, HEADS, LANES)[:, :, :hd].reshape(K, HEADS * hd)


def _rope_consts(first_lane, half):
    lane = np.arange(LANES)
    first = ((lane >= first_lane) & (lane < first_lane + half)).astype(np.float32)
    second = ((lane >= first_lane + half) & (lane < first_lane + 2 * half)).astype(np.float32)
    fixed = (lane < first_lane).astype(np.float32)
    j = np.where(first > 0, lane - first_lane, lane - first_lane - half) * (first + second)
    inv = (ROPE_THETA ** (-j.astype(np.float64) / half)).astype(np.float32)
    return [jnp.asarray(v.reshape(1, LANES), F32) for v in (inv, first, second, fixed)]


def _assemble(name, gathered):
    if name in COL_SHARDED:
        return jnp.transpose(gathered, (1, 0, 2)).reshape(gathered.shape[1], 4 * gathered.shape[2])
    return gathered.reshape(4 * gathered.shape[1], gathered.shape[2])


def _split_for_reducers(name, g, dtype):
    if name in COL_SHARDED:
        K, N4 = g.shape
        return jnp.transpose(g.reshape(2, K // 2, 4, N4 // 4), (2, 0, 1, 3)).astype(dtype)
    return g.reshape(4, 2, g.shape[0] // 8, g.shape[1]).astype(dtype)


def _local_step(x, pos, tgt, wts, late_shards, small):
    w_in = wts["w_in"]
    seg = [w_in[:, IN_OFFS[i]:IN_OFFS[i + 1]] for i in range(8)]
    kr_w = jnp.pad(seg[2], ((0, 0), (MLA_NOPE, LANES - MLA_QK)))
    w_in_p = jnp.concatenate([seg[7], seg[5], seg[6], _pad_heads(seg[3], RET_QK), _pad_heads(seg[4], RET_QK),
                              seg[0], seg[1], kr_w], axis=1)
    w_qb_p = _pad_heads(wts["w_q_b"], MLA_QK)
    kvw = wts["w_kv_b"].reshape(MLA_KV_RANK, HEADS, MLA_NOPE + MLA_V)
    pad_kv = lambda t: jnp.pad(t, ((0, 0), (0, 0), (0, LANES - t.shape[2]))).reshape(MLA_KV_RANK, HEADS * LANES)
    w_kn_p, w_v_p = pad_kv(kvw[:, :, :MLA_NOPE]), pad_kv(kvw[:, :, MLA_NOPE:])
    w_kv_p = jnp.concatenate([w_kn_p, w_v_p], axis=1)
    g_qn_p = jnp.pad(small["g_qn"], ((0, 0), (0, LANES - MLA_QK)))
    g_kn_p = jnp.pad(small["g_kn"], ((0, 0), (0, LANES - MLA_QK)))
    dec_f = jnp.broadcast_to(small["ret_decay_fwd"].reshape(HEADS, 1), (HEADS, LANES))
    dec_b = jnp.broadcast_to(small["ret_decay_bwd"].reshape(HEADS, 1), (HEADS, LANES))
    T, N = True, False
    RT, HT = ROW_TILE, HEAD_ROW_TILE

    tab_m = _rowwise("rope_table_mla", _f_rope_table, _rope_consts(MLA_NOPE, MLA_ROPE // 2), [(pos, 1, 0, N)], [(pos, 1, 0, N)],
                     [(LANES, F32, N)] * 3, HT)
    tab_r = _rowwise("rope_table_ret", _f_rope_table, _rope_consts(0, RET_QK // 2), [(pos, 1, 0, N)], [(pos, 1, 0, N)],
                     [(LANES, F32, N)] * 3, HT)
    aux_m = [(t, LANES, 0, N) for t in tab_m]
    aux_r = [(t, LANES, 0, N) for t in tab_r]

    rows_rms1 = [(x, D_MODEL, 0, N)]
    (h,) = _rowwise("rms_mix", _f_rms, [small["g_mix"]], rows_rms1, [], [(D_MODEL, BF16, N)], RT)
    proj = _mm("proj", h, w_in_p, "nn", BF16)
    rows_a = [(proj, MLA_Q_RANK, 24, N), (proj, MLA_KV_RANK, 50, N)]
    cqn, ckvn = _rowwise("mla_lat_norm", _f_mla_a, [small["g_q_a"], small["g_kv_a"]], rows_a, [],
                         [(MLA_Q_RANK, BF16, N), (MLA_KV_RANK, BF16, N)], RT)
    qraw = _mm("mla_q_up", cqn, w_qb_p, "nn", BF16)
    kv = _mm("mla_kv_up", ckvn, w_kv_p, "nn", BF16)
    rows_b = [(qraw, LANES, 0, T), (kv, LANES, 0, T), (proj, LANES, 51, N)]
    q, k = _rowwise("mla_qk_norm_rope", _f_mla_b, [g_qn_p, g_kn_p], rows_b, aux_m, [(LANES, BF16, T)] * 2, HT, HEADS)
    o, lse, late = _flash_fwd(q, k, kv, [late_shards[n] for n in LATE])
    wl = {n: _assemble(n, g) for n, g in zip(LATE, late)}
    w_mla_p = jnp.pad(wl["w_mla_out"].reshape(HEADS, MLA_V, D_MODEL), ((0, 0), (0, LANES - MLA_V), (0, 0))).reshape(HEADS * LANES, D_MODEL)
    w_ret_out, w_out, w_gu, w_down = wl["w_ret_out"], wl["w_out"], wl["w_gate_up"], wl["w_down"]
    y_a = _mm("mla_out", o, w_mla_p, "nn", F32)
    rows_rr = [(proj, LANES, 32, T), (proj, LANES, 40, T)]
    qr, kr = _rowwise("ret_rope", _f_ret_rope, [], rows_rr, aux_r, [(LANES, RET_QK_DTYPE, T)] * 2, HT, HEADS)
    ret_f, ret_b, st_f, st_b = _ret_fwd(qr, kr, proj, 2, dec_f, dec_b)
    rows_rp = [(ret_f, LANES, 0, T), (ret_b, LANES, 0, T), (proj, LANES, 24, T)]
    (o_b,) = _rowwise("ret_post", _f_ret_post, [], rows_rp, [], [(LANES, BF16, T)], HT, HEADS)
    y_b = _mm("ret_out", o_b, w_ret_out, "nn", F32)
    rows_m = [(proj, D_MODEL, 0, N), (proj, D_MODEL, 1, N), (y_a, D_MODEL, 0, N), (y_b, D_MODEL, 0, N)]
    (merged,) = _rowwise("merge", _f_merge, [], rows_m, [], [(D_MODEL, BF16, N)], RT)
    x2 = _mm("mix_out", merged, w_out, "nn", F32, res=x)
    rows_rms2 = [(x2, D_MODEL, 0, N)]
    (h2,) = _rowwise("rms_ffn", _f_rms, [small["g_ffn"]], rows_rms2, [], [(D_MODEL, BF16, N)], RT)
    gu = _mm("ffn_gate_up", h2, w_gu, "nn", BF16)
    rows_sw = [(gu, FFN_HIDDEN, 0, N), (gu, FFN_HIDDEN, 1, N)]
    (act,) = _rowwise("swiglu", _f_swiglu, [], rows_sw, [], [(FFN_HIDDEN, BF16, N)], RT)
    y = _mm("ffn_down", act, w_down, "nn", F32, res=x2)
    dy, loss_row = _loss_kernel(y, tgt)

    dact = _mm("d_act", dy, w_down, "nt", BF16)
    dw_down = _mm("dw_down", act, dy, "tn", F32)
    (dgu,), _ = _rowwise_vjp("swiglu_bwd", _f_swiglu, [], rows_sw, [], [[(dact, FFN_HIDDEN, 0, N)]], [([0, 1], BF16)], RT)
    dh2 = _mm("d_h2", dgu, w_gu, "nt", BF16)
    dw_gu = _mm("dw_gate_up", h2, dgu, "tn", F32)
    (dx2,), (dg_ffn,) = _rowwise_vjp("rms_ffn_bwd", _f_rms, [small["g_ffn"]], rows_rms2, [], [[(dh2, D_MODEL, 0, N)]],
                                     [([0], F32)], RT, adds=[(dy, D_MODEL, 0, N)])
    dmerged = _mm("d_merged", dx2, w_out, "nt", BF16)
    dw_out = _mm("dw_out", merged, dx2, "tn", F32)
    (dgl, dy_a, dy_b), _ = _rowwise_vjp("merge_bwd", _f_merge, [], rows_m, [], [[(dmerged, D_MODEL, 0, N)]],
                                        [([0, 1], BF16), ([2], BF16), ([3], BF16)], RT)
    do_b = _mm("d_ret_o", dy_b, w_ret_out, "nt", BF16)
    dw_ret_out = _mm("dw_ret_out", o_b, dy_b, "tn", F32)
    (dret, dg_r), _ = _rowwise_vjp("ret_post_bwd", _f_ret_post, [], rows_rp, [], [[(do_b, LANES, 0, T)]],
                                   [([0], BF16), ([2], BF16)], HT, HEADS)
    dqf, dkf, dvf, dqb, dkb, dvb, ddec_f, ddec_b = _ret_bwd(qr, kr, proj, 2, dret, st_f, st_b, dec_f, dec_b)
    (dq_r, dk_r), _ = _rowwise_vjp("ret_rope_bwd", _f_ret_rope, [], rows_rr, aux_r,
                                   [[(dqf, LANES, 0, T), (dqb, LANES, 0, T)], [(dkf, LANES, 0, T), (dkb, LANES, 0, T)]],
                                   [([0], BF16), ([1], BF16)], HT, HEADS)
    (dv_r,) = _rowwise("ret_dv_sum", _f_add, [], [(dvf, D_MODEL, 0, N), (dvb, D_MODEL, 0, N)], [], [(D_MODEL, BF16, N)], RT)
    do = _mm("d_mla_o", dy_a, w_mla_p, "nt", BF16)
    dw_mla_p = _mm("dw_mla_out", o, dy_a, "tn", F32)
    (delta,) = _rowwise("mla_delta", _f_delta, [], [(do, LANES, 0, T), (o, LANES, 0, T)], [], [(LANES, F32, T)], HT, HEADS)
    dw_mla = dw_mla_p.reshape(HEADS, LANES, D_MODEL)[:, :MLA_V].reshape(HEADS * MLA_V, D_MODEL)
    late_grads = {"w_mla_out": dw_mla, "w_ret_out": dw_ret_out, "w_out": dw_out, "w_gate_up": dw_gu, "w_down": dw_down}
    dq, dk, dv, late_got = _flash_bwd(q, k, kv, do, lse, delta,
                                      [_split_for_reducers(n, late_grads[n], BF16) for n in LATE])
    (dqraw, dkn, dkr), (dg_qn_p, dg_kn_p) = _rowwise_vjp(
        "mla_qk_norm_rope_bwd", _f_mla_b, [g_qn_p, g_kn_p], rows_b, aux_m, [[(dq, LANES, 0, T)], [(dk, LANES, 0, T)]],
        [([0], BF16), ([1], BF16), ([2], F32)], HT, HEADS)
    dckvn = _mm("d_ckvn_v", dv, w_v_p, "nt", BF16, res=_mm("d_ckvn_k", dkn, w_kn_p, "nt", F32))
    dw_kn_p = _mm("dw_kv_k", ckvn, dkn, "tn", F32)
    dw_v_p = _mm("dw_kv_v", ckvn, dv, "tn", F32)
    dcqn = _mm("d_cqn", dqraw, w_qb_p, "nt", BF16)
    dw_qb_p = _mm("dw_q_b", cqn, dqraw, "tn", F32)
    (dcq, dckv), (dg_q_a, dg_kv_a) = _rowwise_vjp(
        "mla_lat_norm_bwd", _f_mla_a, [small["g_q_a"], small["g_kv_a"]], rows_a, [],
        [[(dcqn, MLA_Q_RANK, 0, N)], [(dckvn, MLA_KV_RANK, 0, N)]], [([0], BF16), ([1], BF16)], RT)
    dproj = jnp.concatenate([dgl, dv_r, dg_r, dq_r, dk_r, dcq, dckv, dkr.astype(BF16)], axis=1)
    dh = _mm("d_h", dproj, w_in_p, "nt", BF16)
    dw_in_p = _mm("dw_in", h, dproj, "tn", F32)
    (dx,), (dg_mix,) = _rowwise_vjp("rms_mix_bwd", _f_rms, [small["g_mix"]], rows_rms1, [], [[(dh, D_MODEL, 0, N)]],
                                    [([0], F32)], RT, adds=[(dx2, D_MODEL, 0, N)])

    c = lambda a, b_: dw_in_p[:, a:b_]
    dw_in = jnp.concatenate([c(6144, 6400), c(6400, 6528), c(6528 + MLA_NOPE, 6528 + MLA_QK), _unpad_heads(c(4096, 5120), RET_QK),
                             _unpad_heads(c(5120, 6144), RET_QK), c(2048, 3072), c(3072, 4096), c(0, 2048)], axis=1)
    un_kv = lambda t: t.reshape(MLA_KV_RANK, HEADS, LANES)[:, :, :MLA_NOPE]
    dw_kv = jnp.concatenate([un_kv(dw_kn_p), un_kv(dw_v_p)], axis=2).reshape(MLA_KV_RANK, HEADS * (MLA_NOPE + MLA_V))
    grads = {"w_in": dw_in, "w_q_b": _unpad_heads(dw_qb_p, MLA_QK), "w_kv_b": dw_kv}
    sgrads = {"g_mix": dg_mix, "g_q_a": dg_q_a, "g_kv_a": dg_kv_a, "g_qn": dg_qn_p[:, :MLA_QK], "g_kn": dg_kn_p[:, :MLA_QK],
              "ret_decay_fwd": ddec_f[:, 0].reshape(1, HEADS), "ret_decay_bwd": ddec_b[:, 0].reshape(1, HEADS), "g_ffn": dg_ffn}
    return loss_row, dx, late_got, grads, sgrads


def _coords():
    return lax.axis_index("x"), lax.axis_index("y"), lax.axis_index("c")


def _other_chips(x, y):
    return [(1 - x, y), (x, 1 - y), (1 - x, 1 - y)]


ANY = pl.BlockSpec(memory_space=pl.ANY)


def _gather_copies(ins, outs, local_sems, send_sems, recv_sems):
    x, y, c = _coords()
    mine = 2 * x + y
    local, sends, arrivals = [], [], []
    for w in range(len(ins)):
        local.append(pltpu.make_async_copy(ins[w], outs[w].at[mine], local_sems.at[w]))
        for j, (cx, cy) in enumerate(_other_chips(x, y)):
            sems = dict(send_sem=send_sems.at[3 * w + j], recv_sem=recv_sems.at[3 * w + j],
                        device_id=(cx, cy, c), device_id_type=MESH)
            sends.append(pltpu.make_async_remote_copy(src_ref=ins[w], dst_ref=outs[w].at[mine], **sems))
            arrivals.append(functools.partial(pltpu.make_async_remote_copy, src_ref=ins[w],
                                              dst_ref=outs[w].at[2 * cx + cy], **sems))
    return local, sends, arrivals


def _gather_start(copies):
    local, sends, _ = copies
    for cp in local + sends:
        cp.start()


def _gather_wait(copies):
    local, sends, arrivals = copies
    for make in arrivals:
        make().wait_recv()
    for cp in sends:
        cp.wait_send()
    for cp in local:
        cp.wait()


def _weight_gather_first(shards):
    n = len(shards)

    def body(*refs):
        ins, outs = refs[:n], refs[n:2 * n]
        local_sems, send_sems, recv_sems = refs[2 * n:]
        x, y, c = _coords()
        chips = _other_chips(x, y)
        mine = 2 * x + y

        def half(ref, slot, core):
            rows = ref.shape[1] // 2
            return ref.at[slot, pl.ds(pl.multiple_of(core * rows, 8), rows)]

        def copy(w, k, slot, core, to, src=None):
            return pltpu.make_async_remote_copy(
                src_ref=half(outs[w], slot, core) if src is None else src, dst_ref=half(outs[w], slot, core),
                send_sem=send_sems.at[6 * w + k], recv_sem=recv_sems.at[6 * w + k], device_id=to, device_id_type=MESH)

        local, first, passed = [], [], []
        for w in range(n):
            cp = pltpu.make_async_copy(ins[w], outs[w].at[mine], local_sems.at[w])
            cp.start()
            local.append(cp)
            rows = ins[w].shape[0] // 2
            my_half = ins[w].at[pl.ds(pl.multiple_of(c * rows, 8), rows)]
            for j, (cx, cy) in enumerate(chips):
                cp = copy(w, j, mine, c, (cx, cy, c), src=my_half)
                cp.start()
                first.append(cp)
        for w in range(n):
            for j, (cx, cy) in enumerate(chips):
                copy(w, j, 2 * cx + cy, c, (x, y, c)).wait_recv()
                cp = copy(w, 3 + j, 2 * cx + cy, c, (x, y, 1 - c))
                cp.start()
                passed.append(cp)
        for w in range(n):
            for j, (cx, cy) in enumerate(chips):
                copy(w, 3 + j, 2 * cx + cy, 1 - c, (x, y, c)).wait_recv()
        for cp in first + passed:
            cp.wait_send()
        for cp in local:
            cp.wait()

    return pl.pallas_call(
        body, name="weight_gather_first", in_specs=[ANY] * n, out_specs=[ANY] * n,
        out_shape=[jax.ShapeDtypeStruct((4,) + s.shape, s.dtype) for s in shards],
        scratch_shapes=[pltpu.SemaphoreType.DMA((n,)), pltpu.SemaphoreType.DMA((6 * n,)), pltpu.SemaphoreType.DMA((6 * n,))],
    )(*shards)


def _scatter_copies(ins, outs, local_sems, send_sems, recv_sems):
    x, y, c = _coords()
    me = 4 * x + 2 * y + c
    local, sends, arrivals = [], [], []
    for w in range(len(ins)):
        local.append(pltpu.make_async_copy(ins[w].at[2 * x + y, c], outs[w].at[me], local_sems.at[w]))
        for k in range(1, 8):
            px, py, pc = x ^ (k >> 2), y ^ ((k >> 1) & 1), c ^ (k & 1)
            sems = dict(send_sem=send_sems.at[7 * w + k - 1], recv_sem=recv_sems.at[7 * w + k - 1],
                        device_id=(px, py, pc), device_id_type=MESH)
            sends.append(pltpu.make_async_remote_copy(src_ref=ins[w].at[2 * px + py, pc], dst_ref=outs[w].at[me], **sems))
            arrivals.append(functools.partial(
                pltpu.make_async_remote_copy, src_ref=ins[w].at[2 * px + py, pc],
                dst_ref=outs[w].at[4 * px + 2 * py + pc], **sems))
    return local, sends, arrivals


_scatter_start, _scatter_wait = _gather_start, _gather_wait


def _grad_scatter_late(gs):
    n = len(gs)

    def body(*refs):
        copies = _scatter_copies(refs[:n], refs[n:2 * n], *refs[2 * n:])
        _scatter_start(copies)
        _scatter_wait(copies)

    return pl.pallas_call(
        body, name="grad_scatter_late", in_specs=[ANY] * n, out_specs=[ANY] * n,
        out_shape=[jax.ShapeDtypeStruct((8,) + g.shape[2:], g.dtype) for g in gs],
        scratch_shapes=[pltpu.SemaphoreType.DMA((n,)), pltpu.SemaphoreType.DMA((7 * n,)), pltpu.SemaphoreType.DMA((7 * n,))],
    )(*gs)


def _grad_sum8(name, got):
    _, R, W = got.shape
    tr = _pick(R, 256, 16)

    def body(g_ref, o_ref):
        total = g_ref[0].astype(F32)
        for d in range(1, 8):
            total = total + g_ref[d].astype(F32)
        o_ref[...] = total

    return pl.pallas_call(
        body, name=name, grid=(R // tr,), in_specs=[pl.BlockSpec((8, tr, W), lambda i: (0, i, 0))],
        out_specs=pl.BlockSpec((tr, W), lambda i: (i, 0)), out_shape=jax.ShapeDtypeStruct((R, W), F32),
        compiler_params=_params(),
    )(got)


def _sibling_exchange(gs):
    n = len(gs)

    def body(*refs):
        ins, outs, send_sems, recv_sems = refs[:n], refs[n:2 * n], refs[2 * n], refs[2 * n + 1]
        x, y, c = _coords()
        cps = []
        for w in range(n):
            cp = pltpu.make_async_remote_copy(
                src_ref=ins[w].at[:, 1 - c], dst_ref=outs[w], send_sem=send_sems.at[w], recv_sem=recv_sems.at[w],
                device_id=(x, y, 1 - c), device_id_type=MESH)
            cp.start()
            cps.append(cp)
        for cp in cps:
            cp.wait()

    return pl.pallas_call(
        body, name="grad_sibling_exchange", in_specs=[ANY] * n, out_specs=[ANY] * n,
        out_shape=[jax.ShapeDtypeStruct((4,) + g.shape[2:], F32) for g in gs],
        scratch_shapes=[pltpu.SemaphoreType.DMA((n,)), pltpu.SemaphoreType.DMA((n,))],
    )(*gs)


def _pair_sum(name, g, got, c_arr):
    _, _, R, W = g.shape
    tr = _pick(R, 256, 8)

    def body(c_ref, a_ref, b_ref, o_ref):
        o_ref[...] = a_ref[0] + b_ref[...]

    return pl.pallas_call(
        body, name=name,
        grid_spec=pltpu.PrefetchScalarGridSpec(
            num_scalar_prefetch=1, grid=(4, R // tr),
            in_specs=[pl.BlockSpec((1, 1, tr, W), lambda j, i, c_ref: (j, c_ref[0], i, 0)),
                      pl.BlockSpec((1, tr, W), lambda j, i, c_ref: (j, i, 0))],
            out_specs=pl.BlockSpec((1, tr, W), lambda j, i, c_ref: (j, i, 0))),
        out_shape=jax.ShapeDtypeStruct((4, R, W), F32), compiler_params=_params(),
    )(c_arr, g, got)


def _chip_exchange(parts):
    n = len(parts)

    def body(*refs):
        ins, outs, send_sems, recv_sems = refs[:n], refs[n:2 * n], refs[2 * n], refs[2 * n + 1]
        x, y, c = _coords()
        sends = []
        for w in range(n):
            for j, (cx, cy) in enumerate(_other_chips(x, y)):
                cp = pltpu.make_async_remote_copy(
                    src_ref=ins[w].at[2 * cx + cy], dst_ref=outs[w].at[j], send_sem=send_sems.at[3 * w + j],
                    recv_sem=recv_sems.at[3 * w + j], device_id=(cx, cy, c), device_id_type=MESH)
                cp.start()
                sends.append(cp)
        for cp in sends:
            cp.wait_recv()
        for cp in sends:
            cp.wait_send()

    return pl.pallas_call(
        body, name="grad_chip_exchange", in_specs=[ANY] * n, out_specs=[ANY] * n,
        out_shape=[jax.ShapeDtypeStruct((3,) + p.shape[1:], F32) for p in parts],
        scratch_shapes=[pltpu.SemaphoreType.DMA((3 * n,)), pltpu.SemaphoreType.DMA((3 * n,))],
    )(*parts)


def _chip_sum(name, part, got, slot_arr):
    _, R, W = part.shape
    tr = _pick(R, 256, 8)

    def body(s_ref, a_ref, b_ref, o_ref):
        o_ref[...] = ((a_ref[0] + b_ref[0]) + b_ref[1]) + b_ref[2]

    return pl.pallas_call(
        body, name=name,
        grid_spec=pltpu.PrefetchScalarGridSpec(
            num_scalar_prefetch=1, grid=(R // tr,),
            in_specs=[pl.BlockSpec((1, tr, W), lambda i, s_ref: (s_ref[0], i, 0)),
                      pl.BlockSpec((3, tr, W), lambda i, s_ref: (0, i, 0))],
            out_specs=pl.BlockSpec((tr, W), lambda i, s_ref: (i, 0))),
        out_shape=jax.ShapeDtypeStruct((R, W), F32), compiler_params=_params(),
    )(slot_arr, part, got)


def _half_exchange(halves):
    n = len(halves)

    def body(*refs):
        ins, outs = refs[:n], refs[n:2 * n]
        local_sems, send_sems, recv_sems = refs[2 * n:]
        x, y, c = _coords()
        local, sends = [], []
        for w in range(n):
            cp = pltpu.make_async_copy(ins[w], outs[w].at[c], local_sems.at[w])
            cp.start()
            local.append(cp)
            cp = pltpu.make_async_remote_copy(
                src_ref=ins[w], dst_ref=outs[w].at[c], send_sem=send_sems.at[w], recv_sem=recv_sems.at[w],
                device_id=(x, y, 1 - c), device_id_type=MESH)
            cp.start()
            sends.append(cp)
        for w in range(n):
            pltpu.make_async_remote_copy(
                src_ref=ins[w], dst_ref=outs[w].at[1 - c], send_sem=send_sems.at[w], recv_sem=recv_sems.at[w],
                device_id=(x, y, 1 - c), device_id_type=MESH).wait_recv()
        for cp in sends:
            cp.wait_send()
        for cp in local:
            cp.wait()

    return pl.pallas_call(
        body, name="grad_half_exchange", in_specs=[ANY] * n, out_specs=[ANY] * n,
        out_shape=[jax.ShapeDtypeStruct((2,) + h.shape, F32) for h in halves],
        scratch_shapes=[pltpu.SemaphoreType.DMA((n,)), pltpu.SemaphoreType.DMA((n,)), pltpu.SemaphoreType.DMA((n,))],
    )(*halves)


def _adamw_math(w, g, m, v):
    m2 = ADAM_B1 * m + (1.0 - ADAM_B1) * g
    v2 = ADAM_B2 * v + (1.0 - ADAM_B2) * (g * g)
    m_hat = m2 / (1.0 - ADAM_B1 ** ADAM_STEP)
    v_hat = v2 / (1.0 - ADAM_B2 ** ADAM_STEP)
    return -ADAM_LR * (m_hat / (jnp.sqrt(v_hat) + ADAM_EPS) + ADAM_WD * w), m2, v2


def _small_allreduce_adamw(pack_g, pack_w, pack_m, pack_v):
    def body(g_ref, w_ref, m_ref, v_ref, sum_ref, d_ref, m_out, v_out, land, send_sems, recv_sems):
        x, y, c = _coords()
        me = 4 * x + 2 * y + c
        land[me] = g_ref[...]
        sends = []
        for k in range(1, 8):
            peer = (x ^ (k >> 2), y ^ ((k >> 1) & 1), c ^ (k & 1))
            cp = pltpu.make_async_remote_copy(
                src_ref=g_ref, dst_ref=land.at[me], send_sem=send_sems.at[k - 1], recv_sem=recv_sems.at[k - 1],
                device_id=peer, device_id_type=MESH)
            cp.start()
            sends.append((cp, peer))
        for k, (cp, peer) in enumerate(sends):
            pltpu.make_async_remote_copy(
                src_ref=g_ref, dst_ref=land.at[4 * peer[0] + 2 * peer[1] + peer[2]], send_sem=send_sems.at[k],
                recv_sem=recv_sems.at[k], device_id=peer, device_id_type=MESH).wait_recv()
        for cp, _ in sends:
            cp.wait_send()
        total = land[0]
        for d in range(1, 8):
            total = total + land[d]
        sum_ref[...] = total
        d_ref[...], m_out[...], v_out[...] = _adamw_math(w_ref[...], total, m_ref[...], v_ref[...])

    vm = pl.BlockSpec(memory_space=pltpu.VMEM)
    shp = jax.ShapeDtypeStruct(pack_g.shape, F32)
    return pl.pallas_call(
        body, name="small_allreduce_adamw", in_specs=[vm] * 4, out_specs=[vm] * 4, out_shape=[shp] * 4,
        scratch_shapes=[pltpu.VMEM((8,) + pack_g.shape, F32), pltpu.SemaphoreType.DMA((7,)), pltpu.SemaphoreType.DMA((7,))],
    )(pack_g, pack_w, pack_m, pack_v)


def _adamw(name, w, g, m, v):
    R, C = w.shape
    tr = _pick(R, 256, 8)

    def body(w_ref, g_ref, m_ref, v_ref, d_out, m_out, v_out):
        d_out[...], m_out[...], v_out[...] = _adamw_math(w_ref[...], g_ref[...], m_ref[...], v_ref[...])

    spec = pl.BlockSpec((tr, C), lambda i: (i, 0))
    return pl.pallas_call(
        body, name=name, grid=(R // tr,), in_specs=[spec] * 4, out_specs=[spec] * 3,
        out_shape=[jax.ShapeDtypeStruct((R, C), F32)] * 3, compiler_params=_params(),
    )(w, g, m, v)


def _pack_small(vals, last):
    flat = jnp.concatenate([v.reshape(-1) for v in vals] + [last.reshape(-1)])
    return jnp.pad(flat, (0, SMALL_ROWS * LANES - flat.shape[0])).reshape(SMALL_ROWS, LANES)


def kernel(x, positions, g_mix, w_in, g_q_a, w_q_b, g_kv_a, w_kv_b, g_qn, g_kn, w_mla_out, ret_decay_fwd, ret_decay_bwd, w_ret_out, w_out, g_ffn, w_gate_up, w_down, loss_target, m_g_mix, m_w_in, m_g_q_a, m_w_q_b, m_g_kv_a, m_w_kv_b, m_g_qn, m_g_kn, m_w_mla_out, m_ret_decay_fwd, m_ret_decay_bwd, m_w_ret_out, m_w_out, m_g_ffn, m_w_gate_up, m_w_down, v_g_mix, v_w_in, v_g_q_a, v_w_q_b, v_g_kv_a, v_w_kv_b, v_g_qn, v_g_kn, v_w_mla_out, v_ret_decay_fwd, v_ret_decay_bwd, v_w_ret_out, v_w_out, v_g_ffn, v_w_gate_up, v_w_down):
    given = dict(locals())
    S = x.shape[1]
    xs, tgt = x.reshape(S, D_MODEL), loss_target.reshape(S, D_MODEL)
    pos = positions.reshape(S, 1).astype(F32)

    gathered = _weight_gather_first([given[n].astype(BF16) for n in FIRST])
    wts = {n: _assemble(n, g) for n, g in zip(FIRST, gathered)}
    late_shards = {n: given[n].astype(BF16) for n in LATE}
    small = {n: given[n].reshape(1, -1) for n in SMALL}

    loss_row, dx, late_got, grads, sgrads = _local_step(xs, pos, tgt, wts, late_shards, small)

    first_got = _grad_scatter_late([_split_for_reducers(n, grads[n], BF16) for n in FIRST])
    halves = [_grad_sum8("grad_sum_" + n, got) for n, got in zip(FIRST + LATE, list(first_got) + list(late_got))]
    reduced = _half_exchange(halves)

    out = {}
    for n, r in zip(FIRST + LATE, reduced):
        g = r.reshape(given[n].shape)
        out["grad_" + n] = g
        out["delta_" + n], out["new_m_" + n], out["new_v_" + n] = _adamw("adamw_" + n, given[n], g, given["m_" + n], given["v_" + n])

    one = jnp.ones((1,), F32)
    pk = _small_allreduce_adamw(
        _pack_small([sgrads[n] for n in SMALL], loss_row[0, :1]),
        _pack_small([given[n] for n in SMALL], 0 * one),
        _pack_small([given["m_" + n] for n in SMALL], 0 * one),
        _pack_small([given["v_" + n] for n in SMALL], one))
    off = 0
    for n in SMALL:
        sz = given[n].shape[0]
        for pre, arr in zip(["grad_", "delta_", "new_m_", "new_v_"], pk):
            out[pre + n] = arr.reshape(-1)[off:off + sz]
        off += sz
    loss = pk[0].reshape(-1)[off]

    return (loss, dx.reshape(x.shape), *[out["grad_" + n] for n in WEIGHTS], *[out["delta_" + n] for n in WEIGHTS],
            *[out["new_m_" + n] for n in WEIGHTS], *[out["new_v_" + n] for n in WEIGHTS])
```

```python
import functools
import math

import numpy as np
import jax
import jax.numpy as jnp
from jax import lax
from jax.experimental import pallas as pl
from jax.experimental.pallas import tpu as pltpu

F32 = jnp.float32
BF16 = jnp.bfloat16
MESH = pl.DeviceIdType.MESH

D_MODEL = 1024
HEADS = 8
LANES = 128
MLA_Q_RANK, MLA_KV_RANK = 256, 128
MLA_NOPE, MLA_ROPE, MLA_V = 64, 32, 64
MLA_QK = MLA_NOPE + MLA_ROPE
LN2 = math.log(2.0)
MLA_Q_SCALE = MLA_QK ** -0.5 / LN2
RET_QK, RET_V, RET_CHUNK = 64, 128, 128
RET_QK_DTYPE = BF16
FFN_HIDDEN = 2816
ROPE_THETA = 10000.0
EPS = 1e-6
IN_SPLITS = [256, 128, 32, 512, 512, 1024, 1024, 2048]
IN_OFFS = [0] + list(np.cumsum(IN_SPLITS))
ADAM_LR, ADAM_B1, ADAM_B2, ADAM_EPS, ADAM_WD, ADAM_STEP = 0.001, 0.9, 0.999, 1e-08, 0.01, 10

VMEM_LIMIT = 56 * 1024 * 1024
ROW_TILE = 256
HEAD_ROW_TILE = 1024
MM_TM, MM_TN, MM_TK, MM_KFULL = 1024, 2048, 2048, 2816
ATT_TQ, ATT_TK = 512, 2048
ATT_BQ, ATT_BK = 1024, 1024

SHARDED = ["w_in", "w_q_b", "w_kv_b", "w_mla_out", "w_ret_out", "w_out", "w_gate_up", "w_down"]
COL_SHARDED = {"w_in", "w_q_b", "w_kv_b", "w_mla_out", "w_gate_up"}
FIRST = ["w_in", "w_q_b", "w_kv_b"]
LATE = ["w_mla_out", "w_ret_out", "w_out", "w_gate_up", "w_down"]
SMALL = ["g_mix", "g_q_a", "g_kv_a", "g_qn", "g_kn", "ret_decay_fwd", "ret_decay_bwd", "g_ffn"]
WEIGHTS = ["g_mix", "w_in", "g_q_a", "w_q_b", "g_kv_a", "w_kv_b", "g_qn", "g_kn", "w_mla_out",
           "ret_decay_fwd", "ret_decay_bwd", "w_ret_out", "w_out", "g_ffn", "w_gate_up", "w_down"]
SMALL_ROWS = 24


def _params(**kw):
    return pltpu.CompilerParams(vmem_limit_bytes=VMEM_LIMIT, **kw)


def _pick(dim, target, unit=128):
    if dim <= target:
        return dim
    best = None
    for d in range(unit, target + 1, unit):
        if dim % d == 0:
            best = d
    assert best is not None, (dim, target)
    return best


_DOT = {"nn": (((1,), (0,)), ((), ())), "nt": (((1,), (1,)), ((), ())), "tn": (((0,), (0,)), ((), ()))}


def _dot(a, b, mode="nn"):
    return lax.dot_general(a, b, _DOT[mode], preferred_element_type=F32)


def _mm(name, a, b, mode, out_dtype, res=None):
    if mode == "nn":
        (M, K), (K2, N) = a.shape, b.shape
    elif mode == "nt":
        (M, K), (N, K2) = a.shape, b.shape
    else:
        (K, M), (K2, N) = a.shape, b.shape
    assert K == K2, (name, a.shape, b.shape)
    tm, tn = _pick(M, MM_TM), _pick(N, MM_TN)
    tk = K if K <= MM_KFULL else _pick(K, MM_TK)
    nk = K // tk

    def body(*refs):
        a_ref, b_ref = refs[0], refs[1]
        o_ref, acc = refs[-2], refs[-1]
        k = pl.program_id(2)

        @pl.when(k == 0)
        def _():
            acc[...] = jnp.zeros_like(acc)

        acc[...] += _dot(a_ref[...].astype(BF16), b_ref[...].astype(BF16), mode)

        @pl.when(k == nk - 1)
        def _():
            r = acc[...]
            if res is not None:
                r = r + refs[2][...].astype(F32)
            o_ref[...] = r.astype(o_ref.dtype)

    a_spec = pl.BlockSpec((tk, tm), lambda i, j, k: (k, i)) if mode == "tn" else pl.BlockSpec((tm, tk), lambda i, j, k: (i, k))
    b_spec = pl.BlockSpec((tn, tk), lambda i, j, k: (j, k)) if mode == "nt" else pl.BlockSpec((tk, tn), lambda i, j, k: (k, j))
    o_spec = pl.BlockSpec((tm, tn), lambda i, j, k: (i, j))
    ins, specs = [a, b], [a_spec, b_spec]
    if res is not None:
        ins.append(res)
        specs.append(o_spec)
    return pl.pallas_call(
        body, name=name, grid=(M // tm, N // tn, nk), in_specs=specs, out_specs=o_spec,
        out_shape=jax.ShapeDtypeStruct((M, N), out_dtype),
        scratch_shapes=[pltpu.VMEM((tm, tn), F32)], compiler_params=_params(),
    )(*ins)


def _piece_spec(tm, piece):
    _, w, c0, per_group = piece
    if per_group:
        return pl.BlockSpec((tm, w), lambda i, g: (i, c0 + g))
    return pl.BlockSpec((tm, w), lambda i, g: (i, c0))


def _const_spec(p):
    return pl.BlockSpec(p.shape, lambda i, g: (0, 0))


def _rowwise(name, fn, params, rows, auxs, outs, tm, groups=1):
    S = rows[0][0].shape[0]
    tm = min(tm, S)
    n_p, n_r, n_a = len(params), len(rows), len(auxs)

    def body(*refs):
        p = [r[...] for r in refs[:n_p]]
        r_ = [r[...] for r in refs[n_p:n_p + n_r]]
        a_ = [r[...] for r in refs[n_p + n_r:n_p + n_r + n_a]]
        for o_ref, o in zip(refs[n_p + n_r + n_a:], fn(p, r_, a_)):
            o_ref[...] = o.astype(o_ref.dtype)

    out_specs, out_shape = [], []
    for w, dt, per_group in outs:
        out_specs.append(_piece_spec(tm, (None, w, 0, per_group)))
        out_shape.append(jax.ShapeDtypeStruct((S, w * (groups if per_group else 1)), dt))
    return pl.pallas_call(
        body, name=name, grid=(S // tm, groups),
        in_specs=[_const_spec(p) for p in params] + [_piece_spec(tm, q) for q in list(rows) + list(auxs)],
        out_specs=out_specs, out_shape=out_shape, compiler_params=_params(),
    )(*params, *[q[0] for q in list(rows) + list(auxs)])


def _rowwise_vjp(name, fn, params, rows, auxs, cots, d_outs, tm, groups=1, adds=None):
    S = rows[0][0].shape[0]
    tm = min(tm, S)
    n_p, n_r, n_a = len(params), len(rows), len(auxs)
    cot_flat = [q for c in cots for q in c]
    adds = adds or [None] * len(d_outs)
    add_flat = [q for q in adds if q is not None]
    n_c, n_add = len(cot_flat), len(add_flat)
    shared = [not all(rows[k][3] for k in idx) and groups > 1 for idx, _ in d_outs]

    def body(*refs):
        pos = 0
        p = [r[...] for r in refs[pos:pos + n_p]]; pos += n_p
        r_ = [r[...] for r in refs[pos:pos + n_r]]; pos += n_r
        a_ = [r[...] for r in refs[pos:pos + n_a]]; pos += n_a
        c_refs = refs[pos:pos + n_c]; pos += n_c
        add_refs = list(refs[pos:pos + n_add]); pos += n_add
        d_refs = refs[pos:pos + len(d_outs)]; pos += len(d_outs)
        dp_refs = refs[pos:]
        i, g = pl.program_id(0), pl.program_id(1)
        outs, vjp_fn = jax.vjp(lambda pp, rr: fn(pp, rr, a_), p, r_)
        cts, ci = [], 0
        for c, o in zip(cots, outs):
            t = c_refs[ci][...].astype(F32)
            for extra in c_refs[ci + 1:ci + len(c)]:
                t = t + extra[...].astype(F32)
            ci += len(c)
            cts.append(t.astype(o.dtype))
        dp, dr = vjp_fn(cts)
        for (idx, _), d_ref, add, sh in zip(d_outs, d_refs, adds, shared):
            val = dr[idx[0]].astype(F32) if len(idx) == 1 else jnp.concatenate([dr[k].astype(F32) for k in idx], axis=1)
            if add is not None:
                val = val + add_refs.pop(0)[...].astype(F32)
            if sh:
                @pl.when(g == 0)
                def _(d_ref=d_ref):
                    d_ref[...] = jnp.zeros_like(d_ref)
                d_ref[...] += val.astype(d_ref.dtype)
            else:
                d_ref[...] = val.astype(d_ref.dtype)
        first = jnp.logical_and(i == 0, g == 0)
        for dp_ref, d in zip(dp_refs, dp):
            @pl.when(first)
            def _(dp_ref=dp_ref):
                dp_ref[...] = jnp.zeros_like(dp_ref)
            dp_ref[...] += d.astype(F32)

    out_specs, out_shape = [], []
    for (idx, dt), sh in zip(d_outs, shared):
        w = sum(rows[k][1] for k in idx)
        per_group = (not sh) and groups > 1
        out_specs.append(_piece_spec(tm, (None, w, 0, per_group)))
        out_shape.append(jax.ShapeDtypeStruct((S, w * (groups if per_group else 1)), dt))
    for p in params:
        out_specs.append(_const_spec(p))
        out_shape.append(jax.ShapeDtypeStruct(p.shape, F32))
    pieces = list(rows) + list(auxs) + cot_flat + add_flat
    res = pl.pallas_call(
        body, name=name, grid=(S // tm, groups),
        in_specs=[_const_spec(p) for p in params] + [_piece_spec(tm, q) for q in pieces],
        out_specs=out_specs, out_shape=out_shape, compiler_params=_params(),
    )(*params, *[q[0] for q in pieces])
    return list(res[:len(d_outs)]), list(res[len(d_outs):])


def _lane_roll(x, shift):
    @jax.custom_vjp
    def roll(v):
        return pltpu.roll(v, shift, 1)

    roll.defvjp(lambda v: (roll(v), None), lambda _, ct: (pltpu.roll(ct, LANES - shift, 1),))
    return roll(x)


@jax.custom_vjp
def _sigmoid(x):
    return 1.0 / (1.0 + jnp.exp(-x))


def _sigmoid_fwd(x):
    s = _sigmoid(x)
    return s, s


_sigmoid.defvjp(_sigmoid_fwd, lambda s, ct: (ct * s * (1.0 - s),))


def _rope(x, cos, sin_lo, sin_hi, half):
    return x * cos + _lane_roll(x, LANES - half) * sin_lo + _lane_roll(x, half) * sin_hi


def _f_rope_table(p, r, a):
    inv, first, second, fixed = p
    ang = a[0] * inv
    cs, sn = jnp.cos(ang), jnp.sin(ang)
    return [cs * (first + second) + fixed, -sn * first, sn * second]


def _f_rms(p, r, a):
    x = r[0].astype(F32)
    return [x * lax.rsqrt(jnp.mean(x * x, axis=-1, keepdims=True) + EPS) * p[0]]


def _f_mla_a(p, r, a):
    return _f_rms([p[0]], [r[0]], a) + _f_rms([p[1]], [r[1]], a)


def _f_mla_b(p, r, a):
    def norm_rope(v, g):
        ms = jnp.sum(v * v, axis=-1, keepdims=True) * (1.0 / MLA_QK)
        return _rope(v * lax.rsqrt(ms + EPS) * g, a[0], a[1], a[2], MLA_ROPE // 2)

    return [norm_rope(r[0].astype(F32), p[0]) * MLA_Q_SCALE, norm_rope(r[1].astype(F32) + r[2].astype(F32), p[1])]


def _f_ret_rope(p, r, a):
    q = _rope(r[0].astype(F32), a[0], a[1], a[2], RET_QK // 2)
    k = _rope(r[1].astype(F32), a[0], a[1], a[2], RET_QK // 2)
    return [q, k * (RET_QK ** -0.5)]


def _f_ret_post(p, r, a):
    ret = r[0].astype(F32) + r[1].astype(F32)
    g = r[2].astype(F32)
    normed = ret * lax.rsqrt(jnp.mean(ret * ret, axis=-1, keepdims=True) + EPS)
    return [g * _sigmoid(g) * normed]


def _f_merge(p, r, a):
    return [_sigmoid(r[0].astype(F32)) * r[2].astype(F32) + _sigmoid(r[1].astype(F32)) * r[3].astype(F32)]


def _f_swiglu(p, r, a):
    g = r[0].astype(F32)
    return [g * _sigmoid(g) * r[1].astype(F32)]


def _f_delta(p, r, a):
    d = jnp.sum(r[0].astype(F32) * r[1].astype(F32), axis=-1, keepdims=True)
    return [jnp.broadcast_to(d, r[0].shape)]


def _f_add(p, r, a):
    return [r[0].astype(F32) + r[1].astype(F32)]


def _loss_kernel(y, tgt):
    S, Dm = y.shape
    tm = min(ROW_TILE, S)

    def body(y_ref, t_ref, dy_ref, loss_ref):
        @pl.when(pl.program_id(0) == 0)
        def _():
            loss_ref[...] = jnp.zeros_like(loss_ref)

        e = y_ref[...] - t_ref[...]
        dy_ref[...] = e * (1.0 / Dm)
        loss_ref[...] += 0.5 * jnp.sum(jnp.mean(e * e, axis=-1, keepdims=True), axis=0, keepdims=True)

    row = pl.BlockSpec((tm, Dm), lambda i: (i, 0))
    return pl.pallas_call(
        body, name="loss", grid=(S // tm,), in_specs=[row, row],
        out_specs=[row, pl.BlockSpec((1, LANES), lambda i: (0, 0))],
        out_shape=[jax.ShapeDtypeStruct((S, Dm), F32), jax.ShapeDtypeStruct((1, LANES), F32)],
        compiler_params=_params(),
    )(y, tgt)


def _flash_fwd(q, k, kv, shards):
    S = q.shape[0]
    tq, tk = min(ATT_TQ, S), min(ATT_TK, S)
    nq, nk = S // tq, S // tk
    n = len(shards)

    def body(q_ref, k_ref, v_ref, *rest):
        shard_refs, (o_ref, lse_ref), gathered = rest[:n], rest[n:n + 2], rest[n + 2:2 * n + 2]
        m_sc, l_sc, acc_sc, send_sems, recv_sems = rest[2 * n + 2:]
        h, qi, ki = pl.program_id(0), pl.program_id(1), pl.program_id(2)

        @pl.when(jnp.logical_and(h == 0, jnp.logical_and(qi == 0, ki == 0)))
        def _():
            _gather_start(_gather_copies(shard_refs, gathered, send_sems, recv_sems))

        @pl.when(ki == 0)
        def _():
            m_sc[...] = jnp.full_like(m_sc, -jnp.inf)
            l_sc[...] = jnp.zeros_like(l_sc)
            acc_sc[...] = jnp.zeros_like(acc_sc)

        s = _dot(q_ref[...], k_ref[...], "nt")
        m_prev = m_sc[...]
        m_new = jnp.maximum(m_prev, jnp.max(s, axis=-1, keepdims=True))
        alpha = jnp.exp2(m_prev - m_new)
        p = jnp.exp2(s - m_new[:, :1])
        l_sc[...] = alpha * l_sc[...] + jnp.sum(p, axis=-1, keepdims=True)
        acc_sc[...] = alpha * acc_sc[...] + _dot(p.astype(BF16), v_ref[...])
        m_sc[...] = m_new

        @pl.when(ki == nk - 1)
        def _():
            o_ref[...] = (acc_sc[...] / l_sc[...]).astype(o_ref.dtype)
            lse_ref[...] = m_sc[...] + jnp.log2(l_sc[...])

        @pl.when(jnp.logical_and(h == HEADS - 1, jnp.logical_and(qi == nq - 1, ki == nk - 1)))
        def _():
            _gather_wait(_gather_copies(shard_refs, gathered, send_sems, recv_sems))

    qs = pl.BlockSpec((tq, LANES), lambda h, i, j: (i, h))
    res = pl.pallas_call(
        body, name="mla_fwd", grid=(HEADS, nq, nk),
        in_specs=[qs, pl.BlockSpec((tk, LANES), lambda h, i, j: (j, h)),
                  pl.BlockSpec((tk, LANES), lambda h, i, j: (j, HEADS + h))] + [ANY] * n,
        out_specs=[qs, qs] + [ANY] * n,
        out_shape=[jax.ShapeDtypeStruct((S, HEADS * LANES), BF16), jax.ShapeDtypeStruct((S, HEADS * LANES), F32)]
        + [jax.ShapeDtypeStruct((4,) + s.shape, s.dtype) for s in shards],
        scratch_shapes=[pltpu.VMEM((tq, LANES), F32)] * 3
        + [pltpu.SemaphoreType.DMA((3 * n,)), pltpu.SemaphoreType.DMA((3 * n,))],
        compiler_params=_params(),
    )(q, k, kv, *shards)
    mine = 2 * lax.axis_index("x") + lax.axis_index("y")
    return res[0], res[1], [_fill_slot(g, s, mine) for g, s in zip(res[2:], shards)]


def _flash_bwd(q, k, kv, do, lse, delta, gs):
    S = q.shape[0]
    tq, tk = min(ATT_BQ, S), min(ATT_BK, S)
    nq, nkt = S // tq, S // tk
    n = len(gs)

    def body(q_ref, k_ref, v_ref, do_ref, lse_ref, dl_ref, *rest):
        g_refs, (dq_ref, dk_ref, dv_ref), got_refs = rest[:n], rest[n:n + 3], rest[n + 3:2 * n + 3]
        dk_sc, dv_sc, send_sems, recv_sems = rest[2 * n + 3:]
        h, ki, qi = pl.program_id(0), pl.program_id(1), pl.program_id(2)

        @pl.when(jnp.logical_and(h == 0, jnp.logical_and(ki == 0, qi == 0)))
        def _():
            _scatter_start(_scatter_copies(g_refs, got_refs, send_sems, recv_sems))

        @pl.when(jnp.logical_and(ki == 0, qi == 0))
        def _():
            dq_ref[...] = jnp.zeros_like(dq_ref)

        @pl.when(qi == 0)
        def _():
            dk_sc[...] = jnp.zeros_like(dk_sc)
            dv_sc[...] = jnp.zeros_like(dv_sc)

        qv, kv_, dov = q_ref[...], k_ref[...], do_ref[...]
        p = jnp.exp2(_dot(qv, kv_, "nt") - lse_ref[...][:, :1])
        dp = _dot(dov, v_ref[...], "nt")
        ds = (p * (dp - dl_ref[...][:, :1]) * LN2).astype(BF16)
        dv_sc[...] += _dot(p.astype(BF16), dov, "tn")
        dk_sc[...] += _dot(ds, qv, "tn")
        rows = pl.ds(pl.multiple_of(qi * tq, tq), tq)
        dq_ref[rows, :] += _dot(ds, kv_)

        @pl.when(qi == nq - 1)
        def _():
            dk_ref[...] = dk_sc[...].astype(dk_ref.dtype)
            dv_ref[...] = dv_sc[...].astype(dv_ref.dtype)

        @pl.when(jnp.logical_and(h == HEADS - 1, jnp.logical_and(ki == nkt - 1, qi == nq - 1)))
        def _():
            _scatter_wait(_scatter_copies(g_refs, got_refs, send_sems, recv_sems))

    qs = pl.BlockSpec((tq, LANES), lambda h, j, i: (i, h))
    ks = pl.BlockSpec((tk, LANES), lambda h, j, i: (j, h))
    res = pl.pallas_call(
        body, name="mla_bwd", grid=(HEADS, nkt, nq),
        in_specs=[qs, ks, pl.BlockSpec((tk, LANES), lambda h, j, i: (j, HEADS + h)), qs, qs, qs] + [ANY] * n,
        out_specs=[pl.BlockSpec((S, LANES), lambda h, j, i: (0, h)), ks, ks] + [ANY] * n,
        out_shape=[jax.ShapeDtypeStruct((S, HEADS * LANES), F32), jax.ShapeDtypeStruct((S, HEADS * LANES), BF16),
                   jax.ShapeDtypeStruct((S, HEADS * LANES), BF16)]
        + [jax.ShapeDtypeStruct((8,) + g.shape[2:], g.dtype) for g in gs],
        scratch_shapes=[pltpu.VMEM((tk, LANES), F32)] * 2
        + [pltpu.SemaphoreType.DMA((7 * n,)), pltpu.SemaphoreType.DMA((7 * n,))],
        compiler_params=_params(),
    )(q, k, kv, do, lse, delta, *gs)
    return res[0], res[1], res[2], [_own_piece(got, g) for got, g in zip(res[3:], gs)]


def _ret_tables(decay_row, backward):
    C = RET_CHUNK
    lg = -jnp.exp(decay_row)
    t = lax.broadcasted_iota(jnp.int32, (C, C), 0).astype(F32)
    s = lax.broadcasted_iota(jnp.int32, (C, C), 1).astype(F32)
    ridx = lax.broadcasted_iota(jnp.int32, (C, LANES), 0).astype(F32)
    if backward:
        dist, mask, aw, bw = s - t, s > t, C - ridx, ridx
    else:
        dist, mask, aw, bw = t - s, t >= s, ridx + 1.0, C - 1.0 - ridx
    dist = jnp.maximum(dist, 0.0)
    din = jnp.where(mask, jnp.exp(lg[:, :1] * dist), 0.0)
    return dict(din=din, dist=dist, a=jnp.exp(lg * aw), b=jnp.exp(lg * bw), c=jnp.exp(lg * C), aw=aw, bw=bw)


def _ret_fwd(qr, kr, proj, v_block, dec_f, dec_b):
    S = qr.shape[0]
    C = RET_CHUNK
    n = S // C
    W = HEADS * LANES

    def body(qf, kf, vf, qb, kb, vb, df, db, of, ob, sf_out, sb_out, st):
        @pl.when(pl.program_id(0) == 0)
        def _():
            st[...] = jnp.zeros_like(st)

        for d, (q_ref, k_ref, v_ref, dec, o_ref, s_out) in enumerate(
                [(qf, kf, vf, df, of, sf_out), (qb, kb, vb, db, ob, sb_out)]):
            for h in range(HEADS):
                lanes = slice(h * LANES, (h + 1) * LANES)
                tb = _ret_tables(dec[h:h + 1, :], d == 1)
                qf32, kf32, v = q_ref[:, lanes].astype(F32), k_ref[:, lanes].astype(F32), v_ref[:, lanes]
                state = st[d, h]
                s_out[0, h] = state
                inner = _dot((_dot(qf32.astype(BF16), kf32.astype(BF16), "nt") * tb["din"]).astype(BF16), v)
                cross = _dot((qf32 * tb["a"]).astype(BF16), state.astype(BF16))
                o_ref[:, lanes] = inner + cross
                st[d, h] = state * tb["c"] + _dot((kf32 * tb["b"]).astype(BF16), v, "tn")

    fw = lambda c0: pl.BlockSpec((C, W), lambda j: (j, c0))
    bw = lambda c0: pl.BlockSpec((C, W), lambda j: (n - 1 - j, c0))
    dec_spec = pl.BlockSpec((HEADS, LANES), lambda j: (0, 0))
    st_shape = jax.ShapeDtypeStruct((n, HEADS, LANES, LANES), F32)
    return pl.pallas_call(
        body, name="ret_fwd", grid=(n,),
        in_specs=[fw(0), fw(0), fw(v_block), bw(0), bw(0), bw(v_block), dec_spec, dec_spec],
        out_specs=[fw(0), bw(0), pl.BlockSpec((1, HEADS, LANES, LANES), lambda j: (j, 0, 0, 0)),
                   pl.BlockSpec((1, HEADS, LANES, LANES), lambda j: (n - 1 - j, 0, 0, 0))],
        out_shape=[jax.ShapeDtypeStruct((S, W), F32)] * 2 + [st_shape] * 2,
        scratch_shapes=[pltpu.VMEM((2, HEADS, LANES, LANES), F32)], compiler_params=_params(),
    )(qr, kr, proj, qr, kr, proj, dec_f, dec_b)


def _ret_bwd(qr, kr, proj, v_block, dret, sf, sb, dec_f, dec_b):
    S = qr.shape[0]
    C = RET_CHUNK
    n = S // C
    W = HEADS * LANES

    def body(qf, kf, vf, gf, sf_ref, qb, kb, vb, gb, sb_ref, df, db,
             dqf, dkf, dvf, dqb, dkb, dvb, ddf, ddb, ds_sc):
        j = pl.program_id(0)

        @pl.when(j == 0)
        def _():
            ds_sc[...] = jnp.zeros_like(ds_sc)
            ddf[...] = jnp.zeros_like(ddf)
            ddb[...] = jnp.zeros_like(ddb)

        for d, (q_ref, k_ref, v_ref, g_ref, s_ref, dec, dq_ref, dk_ref, dv_ref, dd_ref) in enumerate(
                [(qf, kf, vf, gf, sf_ref, df, dqf, dkf, dvf, ddf), (qb, kb, vb, gb, sb_ref, db, dqb, dkb, dvb, ddb)]):
            for h in range(HEADS):
                lanes = slice(h * LANES, (h + 1) * LANES)
                tb = _ret_tables(dec[h:h + 1, :], d == 1)
                v, g = v_ref[:, lanes], g_ref[:, lanes]
                qf32, kf32 = q_ref[:, lanes].astype(F32), k_ref[:, lanes].astype(F32)
                q, k = qf32.astype(BF16), kf32.astype(BF16)
                state, dstate = s_ref[0, h], ds_sc[d, h]
                dstate_b = dstate.astype(BF16)
                dp = _dot(g, v, "nt")
                a_ = _dot(q, k, "nt")
                da = (dp * tb["din"]).astype(BF16)
                g1 = _dot(g, state.astype(BF16), "nt")
                g2 = _dot(v, dstate_b, "nt")
                dq_ref[:, lanes] = (_dot(da, k) + g1 * tb["a"]).astype(dq_ref.dtype)
                dk_ref[:, lanes] = (_dot(da, q, "tn") + g2 * tb["b"]).astype(dk_ref.dtype)
                dv_ref[:, lanes] = (_dot((a_ * tb["din"]).astype(BF16), g, "tn")
                                    + _dot((kf32 * tb["b"]).astype(BF16), dstate_b)).astype(dv_ref.dtype)
                dlg = (jnp.sum(dp * a_ * tb["din"] * tb["dist"], keepdims=True)
                       + jnp.sum(g1 * qf32 * tb["a"] * tb["aw"], keepdims=True)
                       + jnp.sum(g2 * kf32 * tb["b"] * tb["bw"], keepdims=True)
                       + C * jnp.sum(tb["c"] * dstate * state, keepdims=True))
                dd_ref[h:h + 1, :] += jnp.broadcast_to(dlg, (1, LANES))
                ds_sc[d, h] = dstate * tb["c"] + _dot((qf32 * tb["a"]).astype(BF16), g, "tn")

        @pl.when(j == n - 1)
        def _():
            ddf[...] = ddf[...] * -jnp.exp(df[...])
            ddb[...] = ddb[...] * -jnp.exp(db[...])

    fw = lambda c0: pl.BlockSpec((C, W), lambda j: (n - 1 - j, c0))
    bw = lambda c0: pl.BlockSpec((C, W), lambda j: (j, c0))
    dec_spec = pl.BlockSpec((HEADS, LANES), lambda j: (0, 0))
    act = jax.ShapeDtypeStruct((S, W), BF16)
    return pl.pallas_call(
        body, name="ret_bwd", grid=(n,),
        in_specs=[fw(0), fw(0), fw(v_block), fw(0), pl.BlockSpec((1, HEADS, LANES, LANES), lambda j: (n - 1 - j, 0, 0, 0)),
                  bw(0), bw(0), bw(v_block), bw(0), pl.BlockSpec((1, HEADS, LANES, LANES), lambda j: (j, 0, 0, 0)),
                  dec_spec, dec_spec],
        out_specs=[fw(0)] * 3 + [bw(0)] * 3 + [dec_spec] * 2,
        out_shape=[act] * 6 + [jax.ShapeDtypeStruct((HEADS, LANES), F32)] * 2,
        scratch_shapes=[pltpu.VMEM((2, HEADS, LANES, LANES), F32)], compiler_params=_params(),
    )(qr, kr, proj, dret, sf, qr, kr, proj, dret, sb, dec_f, dec_b)


def _pad_heads(w, hd):
    K = w.shape[0]
    return jnp.pad(w.reshape(K, HEADS, hd), ((0, 0), (0, 0), (0, LANES - hd))).reshape(K, HEADS * LANES)


def _unpad_heads(w, hd):
    K = w.shape[0]
    return w.reshape(K, HEADS, LANES)[:, :, :hd].reshape(K, HEADS * hd)


def _rope_consts(first_lane, half):
    lane = np.arange(LANES)
    first = ((lane >= first_lane) & (lane < first_lane + half)).astype(np.float32)
    second = ((lane >= first_lane + half) & (lane < first_lane + 2 * half)).astype(np.float32)
    fixed = (lane < first_lane).astype(np.float32)
    j = np.where(first > 0, lane - first_lane, lane - first_lane - half) * (first + second)
    inv = (ROPE_THETA ** (-j.astype(np.float64) / half)).astype(np.float32)
    return [jnp.asarray(v.reshape(1, LANES), F32) for v in (inv, first, second, fixed)]


def _assemble(name, gathered):
    if name in COL_SHARDED:
        return jnp.transpose(gathered, (1, 0, 2)).reshape(gathered.shape[1], 4 * gathered.shape[2])
    return gathered.reshape(4 * gathered.shape[1], gathered.shape[2])


def _split_for_reducers(name, g, dtype):
    if name in COL_SHARDED:
        K, N4 = g.shape
        return jnp.transpose(g.reshape(2, K // 2, 4, N4 // 4), (2, 0, 1, 3)).astype(dtype)
    return g.reshape(4, 2, g.shape[0] // 8, g.shape[1]).astype(dtype)


def _local_step(x, pos, tgt, wts, late_shards, small):
    w_in = wts["w_in"]
    seg = [w_in[:, IN_OFFS[i]:IN_OFFS[i + 1]] for i in range(8)]
    kr_w = jnp.pad(seg[2], ((0, 0), (MLA_NOPE, LANES - MLA_QK)))
    w_in_p = jnp.concatenate([seg[7], seg[5], seg[6], _pad_heads(seg[3], RET_QK), _pad_heads(seg[4], RET_QK),
                              seg[0], seg[1], kr_w], axis=1)
    w_qb_p = _pad_heads(wts["w_q_b"], MLA_QK)
    kvw = wts["w_kv_b"].reshape(MLA_KV_RANK, HEADS, MLA_NOPE + MLA_V)
    pad_kv = lambda t: jnp.pad(t, ((0, 0), (0, 0), (0, LANES - t.shape[2]))).reshape(MLA_KV_RANK, HEADS * LANES)
    w_kn_p, w_v_p = pad_kv(kvw[:, :, :MLA_NOPE]), pad_kv(kvw[:, :, MLA_NOPE:])
    w_kv_p = jnp.concatenate([w_kn_p, w_v_p], axis=1)
    g_qn_p = jnp.pad(small["g_qn"], ((0, 0), (0, LANES - MLA_QK)))
    g_kn_p = jnp.pad(small["g_kn"], ((0, 0), (0, LANES - MLA_QK)))
    dec_f = jnp.broadcast_to(small["ret_decay_fwd"].reshape(HEADS, 1), (HEADS, LANES))
    dec_b = jnp.broadcast_to(small["ret_decay_bwd"].reshape(HEADS, 1), (HEADS, LANES))
    T, N = True, False
    RT, HT = ROW_TILE, HEAD_ROW_TILE

    tab_m = _rowwise("rope_table_mla", _f_rope_table, _rope_consts(MLA_NOPE, MLA_ROPE // 2), [(pos, 1, 0, N)], [(pos, 1, 0, N)],
                     [(LANES, F32, N)] * 3, HT)
    tab_r = _rowwise("rope_table_ret", _f_rope_table, _rope_consts(0, RET_QK // 2), [(pos, 1, 0, N)], [(pos, 1, 0, N)],
                     [(LANES, F32, N)] * 3, HT)
    aux_m = [(t, LANES, 0, N) for t in tab_m]
    aux_r = [(t, LANES, 0, N) for t in tab_r]

    rows_rms1 = [(x, D_MODEL, 0, N)]
    (h,) = _rowwise("rms_mix", _f_rms, [small["g_mix"]], rows_rms1, [], [(D_MODEL, BF16, N)], RT)
    proj = _mm("proj", h, w_in_p, "nn", BF16)
    rows_a = [(proj, MLA_Q_RANK, 24, N), (proj, MLA_KV_RANK, 50, N)]
    cqn, ckvn = _rowwise("mla_lat_norm", _f_mla_a, [small["g_q_a"], small["g_kv_a"]], rows_a, [],
                         [(MLA_Q_RANK, BF16, N), (MLA_KV_RANK, BF16, N)], RT)
    qraw = _mm("mla_q_up", cqn, w_qb_p, "nn", BF16)
    kv = _mm("mla_kv_up", ckvn, w_kv_p, "nn", BF16)
    rows_b = [(qraw, LANES, 0, T), (kv, LANES, 0, T), (proj, LANES, 51, N)]
    q, k = _rowwise("mla_qk_norm_rope", _f_mla_b, [g_qn_p, g_kn_p], rows_b, aux_m, [(LANES, BF16, T)] * 2, HT, HEADS)
    o, lse, late = _flash_fwd(q, k, kv, [late_shards[n] for n in LATE])
    wl = {n: _assemble(n, g) for n, g in zip(LATE, late)}
    w_mla_p = jnp.pad(wl["w_mla_out"].reshape(HEADS, MLA_V, D_MODEL), ((0, 0), (0, LANES - MLA_V), (0, 0))).reshape(HEADS * LANES, D_MODEL)
    w_ret_out, w_out, w_gu, w_down = wl["w_ret_out"], wl["w_out"], wl["w_gate_up"], wl["w_down"]
    y_a = _mm("mla_out", o, w_mla_p, "nn", F32)
    rows_rr = [(proj, LANES, 32, T), (proj, LANES, 40, T)]
    qr, kr = _rowwise("ret_rope", _f_ret_rope, [], rows_rr, aux_r, [(LANES, RET_QK_DTYPE, T)] * 2, HT, HEADS)
    ret_f, ret_b, st_f, st_b = _ret_fwd(qr, kr, proj, 2, dec_f, dec_b)
    rows_rp = [(ret_f, LANES, 0, T), (ret_b, LANES, 0, T), (proj, LANES, 24, T)]
    (o_b,) = _rowwise("ret_post", _f_ret_post, [], rows_rp, [], [(LANES, BF16, T)], HT, HEADS)
    y_b = _mm("ret_out", o_b, w_ret_out, "nn", F32)
    rows_m = [(proj, D_MODEL, 0, N), (proj, D_MODEL, 1, N), (y_a, D_MODEL, 0, N), (y_b, D_MODEL, 0, N)]
    (merged,) = _rowwise("merge", _f_merge, [], rows_m, [], [(D_MODEL, BF16, N)], RT)
    x2 = _mm("mix_out", merged, w_out, "nn", F32, res=x)
    rows_rms2 = [(x2, D_MODEL, 0, N)]
    (h2,) = _rowwise("rms_ffn", _f_rms, [small["g_ffn"]], rows_rms2, [], [(D_MODEL, BF16, N)], RT)
    gu = _mm("ffn_gate_up", h2, w_gu, "nn", BF16)
    rows_sw = [(gu, FFN_HIDDEN, 0, N), (gu, FFN_HIDDEN, 1, N)]
    (act,) = _rowwise("swiglu", _f_swiglu, [], rows_sw, [], [(FFN_HIDDEN, BF16, N)], RT)
    y = _mm("ffn_down", act, w_down, "nn", F32, res=x2)
    dy, loss_row = _loss_kernel(y, tgt)

    dact = _mm("d_act", dy, w_down, "nt", BF16)
    dw_down = _mm("dw_down", act, dy, "tn", F32)
    (dgu,), _ = _rowwise_vjp("swiglu_bwd", _f_swiglu, [], rows_sw, [], [[(dact, FFN_HIDDEN, 0, N)]], [([0, 1], BF16)], RT)
    dh2 = _mm("d_h2", dgu, w_gu, "nt", BF16)
    dw_gu = _mm("dw_gate_up", h2, dgu, "tn", F32)
    (dx2,), (dg_ffn,) = _rowwise_vjp("rms_ffn_bwd", _f_rms, [small["g_ffn"]], rows_rms2, [], [[(dh2, D_MODEL, 0, N)]],
                                     [([0], F32)], RT, adds=[(dy, D_MODEL, 0, N)])
    dmerged = _mm("d_merged", dx2, w_out, "nt", BF16)
    dw_out = _mm("dw_out", merged, dx2, "tn", F32)
    (dgl, dy_a, dy_b), _ = _rowwise_vjp("merge_bwd", _f_merge, [], rows_m, [], [[(dmerged, D_MODEL, 0, N)]],
                                        [([0, 1], BF16), ([2], BF16), ([3], BF16)], RT)
    do_b = _mm("d_ret_o", dy_b, w_ret_out, "nt", BF16)
    dw_ret_out = _mm("dw_ret_out", o_b, dy_b, "tn", F32)
    (dret, dg_r), _ = _rowwise_vjp("ret_post_bwd", _f_ret_post, [], rows_rp, [], [[(do_b, LANES, 0, T)]],
                                   [([0], BF16), ([2], BF16)], HT, HEADS)
    dqf, dkf, dvf, dqb, dkb, dvb, ddec_f, ddec_b = _ret_bwd(qr, kr, proj, 2, dret, st_f, st_b, dec_f, dec_b)
    (dq_r, dk_r), _ = _rowwise_vjp("ret_rope_bwd", _f_ret_rope, [], rows_rr, aux_r,
                                   [[(dqf, LANES, 0, T), (dqb, LANES, 0, T)], [(dkf, LANES, 0, T), (dkb, LANES, 0, T)]],
                                   [([0], BF16), ([1], BF16)], HT, HEADS)
    (dv_r,) = _rowwise("ret_dv_sum", _f_add, [], [(dvf, D_MODEL, 0, N), (dvb, D_MODEL, 0, N)], [], [(D_MODEL, BF16, N)], RT)
    do = _mm("d_mla_o", dy_a, w_mla_p, "nt", BF16)
    dw_mla_p = _mm("dw_mla_out", o, dy_a, "tn", F32)
    (delta,) = _rowwise("mla_delta", _f_delta, [], [(do, LANES, 0, T), (o, LANES, 0, T)], [], [(LANES, F32, T)], HT, HEADS)
    dw_mla = dw_mla_p.reshape(HEADS, LANES, D_MODEL)[:, :MLA_V].reshape(HEADS * MLA_V, D_MODEL)
    late_grads = {"w_mla_out": dw_mla, "w_ret_out": dw_ret_out, "w_out": dw_out, "w_gate_up": dw_gu, "w_down": dw_down}
    dq, dk, dv, late_got = _flash_bwd(q, k, kv, do, lse, delta,
                                      [_split_for_reducers(n, late_grads[n], BF16) for n in LATE])
    (dqraw, dkn, dkr), (dg_qn_p, dg_kn_p) = _rowwise_vjp(
        "mla_qk_norm_rope_bwd", _f_mla_b, [g_qn_p, g_kn_p], rows_b, aux_m, [[(dq, LANES, 0, T)], [(dk, LANES, 0, T)]],
        [([0], BF16), ([1], BF16), ([2], F32)], HT, HEADS)
    dckvn = _mm("d_ckvn_v", dv, w_v_p, "nt", BF16, res=_mm("d_ckvn_k", dkn, w_kn_p, "nt", F32))
    dw_kn_p = _mm("dw_kv_k", ckvn, dkn, "tn", F32)
    dw_v_p = _mm("dw_kv_v", ckvn, dv, "tn", F32)
    dcqn = _mm("d_cqn", dqraw, w_qb_p, "nt", BF16)
    dw_qb_p = _mm("dw_q_b", cqn, dqraw, "tn", F32)
    (dcq, dckv), (dg_q_a, dg_kv_a) = _rowwise_vjp(
        "mla_lat_norm_bwd", _f_mla_a, [small["g_q_a"], small["g_kv_a"]], rows_a, [],
        [[(dcqn, MLA_Q_RANK, 0, N)], [(dckvn, MLA_KV_RANK, 0, N)]], [([0], BF16), ([1], BF16)], RT)
    dproj = jnp.concatenate([dgl, dv_r, dg_r, dq_r, dk_r, dcq, dckv, dkr.astype(BF16)], axis=1)
    dh = _mm("d_h", dproj, w_in_p, "nt", BF16)
    dw_in_p = _mm("dw_in", h, dproj, "tn", F32)
    (dx,), (dg_mix,) = _rowwise_vjp("rms_mix_bwd", _f_rms, [small["g_mix"]], rows_rms1, [], [[(dh, D_MODEL, 0, N)]],
                                    [([0], F32)], RT, adds=[(dx2, D_MODEL, 0, N)])

    c = lambda a, b_: dw_in_p[:, a:b_]
    dw_in = jnp.concatenate([c(6144, 6400), c(6400, 6528), c(6528 + MLA_NOPE, 6528 + MLA_QK), _unpad_heads(c(4096, 5120), RET_QK),
                             _unpad_heads(c(5120, 6144), RET_QK), c(2048, 3072), c(3072, 4096), c(0, 2048)], axis=1)
    un_kv = lambda t: t.reshape(MLA_KV_RANK, HEADS, LANES)[:, :, :MLA_NOPE]
    dw_kv = jnp.concatenate([un_kv(dw_kn_p), un_kv(dw_v_p)], axis=2).reshape(MLA_KV_RANK, HEADS * (MLA_NOPE + MLA_V))
    grads = {"w_in": dw_in, "w_q_b": _unpad_heads(dw_qb_p, MLA_QK), "w_kv_b": dw_kv}
    sgrads = {"g_mix": dg_mix, "g_q_a": dg_q_a, "g_kv_a": dg_kv_a, "g_qn": dg_qn_p[:, :MLA_QK], "g_kn": dg_kn_p[:, :MLA_QK],
              "ret_decay_fwd": ddec_f[:, 0].reshape(1, HEADS), "ret_decay_bwd": ddec_b[:, 0].reshape(1, HEADS), "g_ffn": dg_ffn}
    return loss_row, dx, late_got, grads, sgrads


def _coords():
    return lax.axis_index("x"), lax.axis_index("y"), lax.axis_index("c")


def _other_chips(x, y):
    return [(1 - x, y), (x, 1 - y), (1 - x, 1 - y)]


ANY = pl.BlockSpec(memory_space=pl.ANY)


def _gather_copies(ins, outs, send_sems, recv_sems):
    x, y, c = _coords()
    mine = 2 * x + y
    sends, arrivals = [], []
    for w in range(len(ins)):
        for j, (cx, cy) in enumerate(_other_chips(x, y)):
            sems = dict(send_sem=send_sems.at[3 * w + j], recv_sem=recv_sems.at[3 * w + j],
                        device_id=(cx, cy, c), device_id_type=MESH)
            sends.append(pltpu.make_async_remote_copy(src_ref=ins[w], dst_ref=outs[w].at[mine], **sems))
            arrivals.append(functools.partial(pltpu.make_async_remote_copy, src_ref=ins[w],
                                              dst_ref=outs[w].at[2 * cx + cy], **sems))
    return sends, arrivals


def _gather_start(copies):
    for cp in copies[0]:
        cp.start()


def _gather_wait(copies):
    sends, arrivals = copies
    for make in arrivals:
        make().wait_recv()
    for cp in sends:
        cp.wait_send()


def _fill_slot(buf, piece, slot):
    return lax.dynamic_update_slice(buf, piece[None], (slot,) + (0,) * piece.ndim)


def _weight_gather_first(shards):
    n = len(shards)

    def body(*refs):
        ins, outs = refs[:n], refs[n:2 * n]
        send_sems, recv_sems = refs[2 * n:]
        x, y, c = _coords()
        chips = _other_chips(x, y)
        mine = 2 * x + y

        def half(ref, slot, core):
            rows = ref.shape[1] // 2
            return ref.at[slot, pl.ds(pl.multiple_of(core * rows, 8), rows)]

        def copy(w, k, slot, core, to, src=None):
            return pltpu.make_async_remote_copy(
                src_ref=half(outs[w], slot, core) if src is None else src, dst_ref=half(outs[w], slot, core),
                send_sem=send_sems.at[6 * w + k], recv_sem=recv_sems.at[6 * w + k], device_id=to, device_id_type=MESH)

        first, passed = [], []
        for w in range(n):
            rows = ins[w].shape[0] // 2
            my_half = ins[w].at[pl.ds(pl.multiple_of(c * rows, 8), rows)]
            for j, (cx, cy) in enumerate(chips):
                cp = copy(w, j, mine, c, (cx, cy, c), src=my_half)
                cp.start()
                first.append(cp)
        for w in range(n):
            for j, (cx, cy) in enumerate(chips):
                copy(w, j, 2 * cx + cy, c, (x, y, c)).wait_recv()
                cp = copy(w, 3 + j, 2 * cx + cy, c, (x, y, 1 - c))
                cp.start()
                passed.append(cp)
        for w in range(n):
            for j, (cx, cy) in enumerate(chips):
                copy(w, 3 + j, 2 * cx + cy, 1 - c, (x, y, c)).wait_recv()
        for cp in first + passed:
            cp.wait_send()

    return pl.pallas_call(
        body, name="weight_gather_first", in_specs=[ANY] * n, out_specs=[ANY] * n,
        out_shape=[jax.ShapeDtypeStruct((4,) + s.shape, s.dtype) for s in shards],
        scratch_shapes=[pltpu.SemaphoreType.DMA((6 * n,)), pltpu.SemaphoreType.DMA((6 * n,))],
    )(*shards)


def _scatter_copies(ins, outs, send_sems, recv_sems):
    x, y, c = _coords()
    me = 4 * x + 2 * y + c
    sends, arrivals = [], []
    for w in range(len(ins)):
        for k in range(1, 8):
            px, py, pc = x ^ (k >> 2), y ^ ((k >> 1) & 1), c ^ (k & 1)
            sems = dict(send_sem=send_sems.at[7 * w + k - 1], recv_sem=recv_sems.at[7 * w + k - 1],
                        device_id=(px, py, pc), device_id_type=MESH)
            sends.append(pltpu.make_async_remote_copy(src_ref=ins[w].at[2 * px + py, pc], dst_ref=outs[w].at[me], **sems))
            arrivals.append(functools.partial(
                pltpu.make_async_remote_copy, src_ref=ins[w].at[2 * px + py, pc],
                dst_ref=outs[w].at[4 * px + 2 * py + pc], **sems))
    return sends, arrivals


_scatter_start, _scatter_wait = _gather_start, _gather_wait


def _own_piece(got, g):
    x, y, c = _coords()
    mine = lax.dynamic_slice(g, (2 * x + y, c, 0, 0), (1, 1) + g.shape[2:]).reshape(g.shape[2:])
    return _fill_slot(got, mine, 4 * x + 2 * y + c)


def _grad_scatter_late(gs):
    n = len(gs)

    def body(*refs):
        copies = _scatter_copies(refs[:n], refs[n:2 * n], *refs[2 * n:])
        _scatter_start(copies)
        _scatter_wait(copies)

    return pl.pallas_call(
        body, name="grad_scatter_late", in_specs=[ANY] * n, out_specs=[ANY] * n,
        out_shape=[jax.ShapeDtypeStruct((8,) + g.shape[2:], g.dtype) for g in gs],
        scratch_shapes=[pltpu.SemaphoreType.DMA((7 * n,)), pltpu.SemaphoreType.DMA((7 * n,))],
    )(*gs)


def _grad_sum8(name, got):
    _, R, W = got.shape
    tr = _pick(R, 256, 16)

    def body(g_ref, o_ref):
        total = g_ref[0].astype(F32)
        for d in range(1, 8):
            total = total + g_ref[d].astype(F32)
        o_ref[...] = total

    return pl.pallas_call(
        body, name=name, grid=(R // tr,), in_specs=[pl.BlockSpec((8, tr, W), lambda i: (0, i, 0))],
        out_specs=pl.BlockSpec((tr, W), lambda i: (i, 0)), out_shape=jax.ShapeDtypeStruct((R, W), F32),
        compiler_params=_params(),
    )(got)


def _sibling_exchange(gs):
    n = len(gs)

    def body(*refs):
        ins, outs, send_sems, recv_sems = refs[:n], refs[n:2 * n], refs[2 * n], refs[2 * n + 1]
        x, y, c = _coords()
        cps = []
        for w in range(n):
            cp = pltpu.make_async_remote_copy(
                src_ref=ins[w].at[:, 1 - c], dst_ref=outs[w], send_sem=send_sems.at[w], recv_sem=recv_sems.at[w],
                device_id=(x, y, 1 - c), device_id_type=MESH)
            cp.start()
            cps.append(cp)
        for cp in cps:
            cp.wait()

    return pl.pallas_call(
        body, name="grad_sibling_exchange", in_specs=[ANY] * n, out_specs=[ANY] * n,
        out_shape=[jax.ShapeDtypeStruct((4,) + g.shape[2:], F32) for g in gs],
        scratch_shapes=[pltpu.SemaphoreType.DMA((n,)), pltpu.SemaphoreType.DMA((n,))],
    )(*gs)


def _pair_sum(name, g, got, c_arr):
    _, _, R, W = g.shape
    tr = _pick(R, 256, 8)

    def body(c_ref, a_ref, b_ref, o_ref):
        o_ref[...] = a_ref[0] + b_ref[...]

    return pl.pallas_call(
        body, name=name,
        grid_spec=pltpu.PrefetchScalarGridSpec(
            num_scalar_prefetch=1, grid=(4, R // tr),
            in_specs=[pl.BlockSpec((1, 1, tr, W), lambda j, i, c_ref: (j, c_ref[0], i, 0)),
                      pl.BlockSpec((1, tr, W), lambda j, i, c_ref: (j, i, 0))],
            out_specs=pl.BlockSpec((1, tr, W), lambda j, i, c_ref: (j, i, 0))),
        out_shape=jax.ShapeDtypeStruct((4, R, W), F32), compiler_params=_params(),
    )(c_arr, g, got)


def _chip_exchange(parts):
    n = len(parts)

    def body(*refs):
        ins, outs, send_sems, recv_sems = refs[:n], refs[n:2 * n], refs[2 * n], refs[2 * n + 1]
        x, y, c = _coords()
        sends = []
        for w in range(n):
            for j, (cx, cy) in enumerate(_other_chips(x, y)):
                cp = pltpu.make_async_remote_copy(
                    src_ref=ins[w].at[2 * cx + cy], dst_ref=outs[w].at[j], send_sem=send_sems.at[3 * w + j],
                    recv_sem=recv_sems.at[3 * w + j], device_id=(cx, cy, c), device_id_type=MESH)
                cp.start()
                sends.append(cp)
        for cp in sends:
            cp.wait_recv()
        for cp in sends:
            cp.wait_send()

    return pl.pallas_call(
        body, name="grad_chip_exchange", in_specs=[ANY] * n, out_specs=[ANY] * n,
        out_shape=[jax.ShapeDtypeStruct((3,) + p.shape[1:], F32) for p in parts],
        scratch_shapes=[pltpu.SemaphoreType.DMA((3 * n,)), pltpu.SemaphoreType.DMA((3 * n,))],
    )(*parts)


def _chip_sum(name, part, got, slot_arr):
    _, R, W = part.shape
    tr = _pick(R, 256, 8)

    def body(s_ref, a_ref, b_ref, o_ref):
        o_ref[...] = ((a_ref[0] + b_ref[0]) + b_ref[1]) + b_ref[2]

    return pl.pallas_call(
        body, name=name,
        grid_spec=pltpu.PrefetchScalarGridSpec(
            num_scalar_prefetch=1, grid=(R // tr,),
            in_specs=[pl.BlockSpec((1, tr, W), lambda i, s_ref: (s_ref[0], i, 0)),
                      pl.BlockSpec((3, tr, W), lambda i, s_ref: (0, i, 0))],
            out_specs=pl.BlockSpec((tr, W), lambda i, s_ref: (i, 0))),
        out_shape=jax.ShapeDtypeStruct((R, W), F32), compiler_params=_params(),
    )(slot_arr, part, got)


def _half_exchange(halves):
    n = len(halves)

    def body(*refs):
        ins, outs, send_sems, recv_sems = refs[:n], refs[n:2 * n], refs[2 * n], refs[2 * n + 1]
        x, y, c = _coords()
        sends = []
        for w in range(n):
            cp = pltpu.make_async_remote_copy(
                src_ref=ins[w], dst_ref=outs[w], send_sem=send_sems.at[w], recv_sem=recv_sems.at[w],
                device_id=(x, y, 1 - c), device_id_type=MESH)
            cp.start()
            sends.append(cp)
        for cp in sends:
            cp.wait()

    got = pl.pallas_call(
        body, name="grad_half_exchange", in_specs=[ANY] * n, out_specs=[ANY] * n,
        out_shape=[jax.ShapeDtypeStruct(h.shape, F32) for h in halves],
        scratch_shapes=[pltpu.SemaphoreType.DMA((n,)), pltpu.SemaphoreType.DMA((n,))],
    )(*halves)
    c = lax.axis_index("c")
    return [jnp.where(c == 0, jnp.stack([mine, theirs]), jnp.stack([theirs, mine])) for mine, theirs in zip(halves, got)]


def _adamw_math(w, g, m, v):
    m2 = ADAM_B1 * m + (1.0 - ADAM_B1) * g
    v2 = ADAM_B2 * v + (1.0 - ADAM_B2) * (g * g)
    m_hat = m2 / (1.0 - ADAM_B1 ** ADAM_STEP)
    v_hat = v2 / (1.0 - ADAM_B2 ** ADAM_STEP)
    return -ADAM_LR * (m_hat / (jnp.sqrt(v_hat) + ADAM_EPS) + ADAM_WD * w), m2, v2


def _small_allreduce_adamw(pack_g, pack_w, pack_m, pack_v):
    def body(g_ref, w_ref, m_ref, v_ref, sum_ref, d_ref, m_out, v_out, land, send_sems, recv_sems):
        x, y, c = _coords()
        me = 4 * x + 2 * y + c
        land[me] = g_ref[...]
        sends = []
        for k in range(1, 8):
            peer = (x ^ (k >> 2), y ^ ((k >> 1) & 1), c ^ (k & 1))
            cp = pltpu.make_async_remote_copy(
                src_ref=g_ref, dst_ref=land.at[me], send_sem=send_sems.at[k - 1], recv_sem=recv_sems.at[k - 1],
                device_id=peer, device_id_type=MESH)
            cp.start()
            sends.append((cp, peer))
        for k, (cp, peer) in enumerate(sends):
            pltpu.make_async_remote_copy(
                src_ref=g_ref, dst_ref=land.at[4 * peer[0] + 2 * peer[1] + peer[2]], send_sem=send_sems.at[k],
                recv_sem=recv_sems.at[k], device_id=peer, device_id_type=MESH).wait_recv()
        for cp, _ in sends:
            cp.wait_send()
        total = land[0]
        for d in range(1, 8):
            total = total + land[d]
        sum_ref[...] = total
        d_ref[...], m_out[...], v_out[...] = _adamw_math(w_ref[...], total, m_ref[...], v_ref[...])

    vm = pl.BlockSpec(memory_space=pltpu.VMEM)
    shp = jax.ShapeDtypeStruct(pack_g.shape, F32)
    return pl.pallas_call(
        body, name="small_allreduce_adamw", in_specs=[vm] * 4, out_specs=[vm] * 4, out_shape=[shp] * 4,
        scratch_shapes=[pltpu.VMEM((8,) + pack_g.shape, F32), pltpu.SemaphoreType.DMA((7,)), pltpu.SemaphoreType.DMA((7,))],
    )(pack_g, pack_w, pack_m, pack_v)


def _adamw(name, w, g, m, v):
    R, C = w.shape
    tr = _pick(R, 256, 8)

    def body(w_ref, g_ref, m_ref, v_ref, d_out, m_out, v_out):
        d_out[...], m_out[...], v_out[...] = _adamw_math(w_ref[...], g_ref[...], m_ref[...], v_ref[...])

    spec = pl.BlockSpec((tr, C), lambda i: (i, 0))
    return pl.pallas_call(
        body, name=name, grid=(R // tr,), in_specs=[spec] * 4, out_specs=[spec] * 3,
        out_shape=[jax.ShapeDtypeStruct((R, C), F32)] * 3, compiler_params=_params(),
    )(w, g, m, v)


def _pack_small(vals, last):
    flat = jnp.concatenate([v.reshape(-1) for v in vals] + [last.reshape(-1)])
    return jnp.pad(flat, (0, SMALL_ROWS * LANES - flat.shape[0])).reshape(SMALL_ROWS, LANES)


def kernel(x, positions, g_mix, w_in, g_q_a, w_q_b, g_kv_a, w_kv_b, g_qn, g_kn, w_mla_out, ret_decay_fwd, ret_decay_bwd, w_ret_out, w_out, g_ffn, w_gate_up, w_down, loss_target, m_g_mix, m_w_in, m_g_q_a, m_w_q_b, m_g_kv_a, m_w_kv_b, m_g_qn, m_g_kn, m_w_mla_out, m_ret_decay_fwd, m_ret_decay_bwd, m_w_ret_out, m_w_out, m_g_ffn, m_w_gate_up, m_w_down, v_g_mix, v_w_in, v_g_q_a, v_w_q_b, v_g_kv_a, v_w_kv_b, v_g_qn, v_g_kn, v_w_mla_out, v_ret_decay_fwd, v_ret_decay_bwd, v_w_ret_out, v_w_out, v_g_ffn, v_w_gate_up, v_w_down):
    given = dict(locals())
    S = x.shape[1]
    xs, tgt = x.reshape(S, D_MODEL), loss_target.reshape(S, D_MODEL)
    pos = positions.reshape(S, 1).astype(F32)

    first_shards = [given[n].astype(BF16) for n in FIRST]
    my_chip = 2 * lax.axis_index("x") + lax.axis_index("y")
    wts = {n: _assemble(n, _fill_slot(g, s, my_chip))
           for n, g, s in zip(FIRST, _weight_gather_first(first_shards), first_shards)}
    late_shards = {n: given[n].astype(BF16) for n in LATE}
    small = {n: given[n].reshape(1, -1) for n in SMALL}

    loss_row, dx, late_got, grads, sgrads = _local_step(xs, pos, tgt, wts, late_shards, small)

    first_gs = [_split_for_reducers(n, grads[n], BF16) for n in FIRST]
    first_got = [_own_piece(got, g) for got, g in zip(_grad_scatter_late(first_gs), first_gs)]
    halves = [_grad_sum8("grad_sum_" + n, got) for n, got in zip(FIRST + LATE, first_got + list(late_got))]
    reduced = _half_exchange(halves)

    out = {}
    for n, r in zip(FIRST + LATE, reduced):
        g = r.reshape(given[n].shape)
        out["grad_" + n] = g
        out["delta_" + n], out["new_m_" + n], out["new_v_" + n] = _adamw("adamw_" + n, given[n], g, given["m_" + n], given["v_" + n])

    one = jnp.ones((1,), F32)
    pk = _small_allreduce_adamw(
        _pack_small([sgrads[n] for n in SMALL], loss_row[0, :1]),
        _pack_small([given[n] for n in SMALL], 0 * one),
        _pack_small([given["m_" + n] for n in SMALL], 0 * one),
        _pack_small([given["v_" + n] for n in SMALL], one))
    off = 0
    for n in SMALL:
        sz = given[n].shape[0]
        for pre, arr in zip(["grad_", "delta_", "new_m_", "new_v_"], pk):
            out[pre + n] = arr.reshape(-1)[off:off + sz]
        off += sz
    loss = pk[0].reshape(-1)[off]

    return (loss, dx.reshape(x.shape), *[out["grad_" + n] for n in WEIGHTS], *[out["delta_" + n] for n in WEIGHTS],
            *[out["new_m_" + n] for n in WEIGHTS], *[out["new_v_" + n] for n in WEIGHTS])
```

```python
import functools
import math

import numpy as np
import jax
import jax.numpy as jnp
from jax import lax
from jax.experimental import pallas as pl
from jax.experimental.pallas import tpu as pltpu

F32 = jnp.float32
BF16 = jnp.bfloat16
MESH = pl.DeviceIdType.MESH

D_MODEL = 1024
HEADS = 8
LANES = 128
MLA_Q_RANK, MLA_KV_RANK = 256, 128
MLA_NOPE, MLA_ROPE, MLA_V = 64, 32, 64
MLA_QK = MLA_NOPE + MLA_ROPE
LN2 = math.log(2.0)
MLA_Q_SCALE = MLA_QK ** -0.5 / LN2
RET_QK, RET_V, RET_CHUNK = 64, 128, 128
RET_QK_DTYPE = BF16
FFN_HIDDEN = 2816
ROPE_THETA = 10000.0
EPS = 1e-6
IN_SPLITS = [256, 128, 32, 512, 512, 1024, 1024, 2048]
IN_OFFS = [0] + list(np.cumsum(IN_SPLITS))
ADAM_LR, ADAM_B1, ADAM_B2, ADAM_EPS, ADAM_WD, ADAM_STEP = 0.001, 0.9, 0.999, 1e-08, 0.01, 10

VMEM_LIMIT = 56 * 1024 * 1024
ROW_TILE = 256
HEAD_ROW_TILE = 2048
MM_TM, MM_TN, MM_TK, MM_KFULL = 1408, 2048, 2048, 2816
ATT_TQ, ATT_TK = 512, 2048
ATT_BQ, ATT_BK = 1024, 1024

SHARDED = ["w_in", "w_q_b", "w_kv_b", "w_mla_out", "w_ret_out", "w_out", "w_gate_up", "w_down"]
COL_SHARDED = {"w_in", "w_q_b", "w_kv_b", "w_mla_out", "w_gate_up"}
FIRST = ["w_in", "w_q_b", "w_kv_b"]
LATE = ["w_mla_out", "w_ret_out", "w_out", "w_gate_up", "w_down"]
SMALL = ["g_mix", "g_q_a", "g_kv_a", "g_qn", "g_kn", "ret_decay_fwd", "ret_decay_bwd", "g_ffn"]
WEIGHTS = ["g_mix", "w_in", "g_q_a", "w_q_b", "g_kv_a", "w_kv_b", "g_qn", "g_kn", "w_mla_out",
           "ret_decay_fwd", "ret_decay_bwd", "w_ret_out", "w_out", "g_ffn", "w_gate_up", "w_down"]
SMALL_ROWS = 24


def _params(**kw):
    return pltpu.CompilerParams(vmem_limit_bytes=VMEM_LIMIT, **kw)


def _pick(dim, target, unit=128):
    if dim <= target:
        return dim
    best = None
    for d in range(unit, target + 1, unit):
        if dim % d == 0:
            best = d
    assert best is not None, (dim, target)
    return best


_DOT = {"nn": (((1,), (0,)), ((), ())), "nt": (((1,), (1,)), ((), ())), "tn": (((0,), (0,)), ((), ()))}


def _dot(a, b, mode="nn"):
    return lax.dot_general(a, b, _DOT[mode], preferred_element_type=F32)


def _rms_rows(x, g):
    x = x.astype(F32)
    return x * lax.rsqrt(jnp.mean(x * x, axis=-1, keepdims=True) + EPS) * g


def _epi_loss(acc, extras, params):
    e = acc + extras[0] - extras[1]
    dy = e * (1.0 / D_MODEL)
    loss = 0.5 * jnp.sum(jnp.mean(e * e, axis=-1, keepdims=True), axis=0, keepdims=True)
    return [dy, dy], [jnp.broadcast_to(loss, (1, LANES))]


def _epi_rms_bwd(n_out):
    def fn(acc, extras, params):
        _, vjp = jax.vjp(_rms_rows, extras[0], params[0])
        dx, dg = vjp(acc)
        return [dx + extras[1]] * n_out, [dg]
    return fn


def _mm(name, a, b, mode, out_dtype, res=None, a_gain=None, a_scale=None, epilogue=None, shard_out=False):
    if mode == "nn":
        (M, K), (K2, N) = a.shape, b.shape
    elif mode == "nt":
        (M, K), (N, K2) = a.shape, b.shape
    else:
        (K, M), (K2, N) = a.shape, b.shape
    assert K == K2, (name, a.shape, b.shape)
    tm, tn = _pick(M, MM_TM), _pick(N, MM_TN)
    tk = K if K <= MM_KFULL else _pick(K, MM_TK)
    if shard_out:
        tm, tn = M // 2, N // 4
    if a_scale is not None:
        assert mode == "tn", name
        tk = _pick(K, MM_TK // 2)
    if epilogue is not None:
        tm = _pick(M, MM_TM // 2)
    nk = K // tk
    cache_a = a_gain is not None
    if a_gain is not None:
        assert mode == "nn" and tk == K and epilogue is None and not shard_out, name
    n_in = 2 + (res is not None) + (a_gain is not None) + 2 * (a_scale is not None)
    extras, eparams, e_outs, e_sums = ([], [], [], [])
    if epilogue is not None:
        assert tn == N and res is None and not shard_out, name
        epi_fn, extras, eparams, e_outs, e_sums = epilogue
    n_out = len(e_outs) + len(e_sums) if epilogue is not None else 1 + cache_a

    def body(*refs):
        a_ref, b_ref = refs[0], refs[1]
        ex_refs = refs[n_in:n_in + len(extras)]
        ep_refs = refs[n_in + len(extras):n_in + len(extras) + len(eparams)]
        out_refs = refs[n_in + len(extras) + len(eparams):][:n_out]
        scratch = refs[n_in + len(extras) + len(eparams) + n_out:]
        acc = scratch[0]
        i, j, k = pl.program_id(0), pl.program_id(1), pl.program_id(2)

        @pl.when(k == 0)
        def _():
            acc[...] = jnp.zeros_like(acc)

        if cache_a:
            @pl.when(j == 0)
            def _():
                x = a_ref[...].astype(F32)
                rstd = lax.rsqrt(jnp.mean(x * x, axis=-1, keepdims=True) + EPS)
                scratch[1][...] = (x * rstd * refs[n_in - 1][...]).astype(BF16)
                out_refs[1][...] = rstd
            av = scratch[1][...]
        elif a_scale is not None:
            av = (a_ref[...].astype(F32) * refs[n_in - 2][...] * refs[n_in - 1][...]).astype(BF16)
        else:
            av = a_ref[...].astype(BF16)
        acc[...] += _dot(av, b_ref[...].astype(BF16), mode)

        @pl.when(k == nk - 1)
        def _():
            if epilogue is None:
                r = acc[...]
                if res is not None:
                    r = r + refs[2][...].astype(F32)
                out_refs[0][...] = r.astype(out_refs[0].dtype).reshape(out_refs[0].shape)
            else:
                vals, sums = epi_fn(acc[...], [r[...] for r in ex_refs], [p[...] for p in ep_refs])
                for o_ref, v in zip(out_refs, vals):
                    o_ref[...] = v.astype(o_ref.dtype)
                for s_ref, v in zip(out_refs[len(vals):], sums):
                    @pl.when(i == 0)
                    def _(s_ref=s_ref):
                        s_ref[...] = jnp.zeros_like(s_ref)
                    s_ref[...] += v

    a_spec = pl.BlockSpec((tk, tm), lambda i, j, k: (k, i)) if mode == "tn" else pl.BlockSpec((tm, tk), lambda i, j, k: (i, k))
    b_spec = pl.BlockSpec((tn, tk), lambda i, j, k: (j, k)) if mode == "nt" else pl.BlockSpec((tk, tn), lambda i, j, k: (k, j))
    o_spec = pl.BlockSpec((tm, tn), lambda i, j, k: (i, j))
    const = lambda p: pl.BlockSpec(p.shape, lambda i, j, k: (0,) * p.ndim)
    ins, specs = [a, b], [a_spec, b_spec]
    if res is not None:
        ins.append(res)
        specs.append(o_spec)
    if a_gain is not None:
        ins.append(a_gain)
        specs.append(const(a_gain))
    if a_scale is not None:
        ins += list(a_scale)
        specs += [pl.BlockSpec((tk, 1), lambda i, j, k: (k, 0)), pl.BlockSpec((1, tm), lambda i, j, k: (0, i))]
    ins += list(extras) + list(eparams)
    specs += [o_spec] * len(extras) + [const(p) for p in eparams]
    if epilogue is not None:
        out_specs = [o_spec] * len(e_outs) + [pl.BlockSpec(s, lambda i, j, k: (0, 0)) for s in e_sums]
        out_shape = [jax.ShapeDtypeStruct((M, N), dt) for dt in e_outs] + [jax.ShapeDtypeStruct(s, F32) for s in e_sums]
    elif shard_out:
        out_specs = pl.BlockSpec((1, 1, tm, tn), lambda i, j, k: (j, i, 0, 0))
        out_shape = jax.ShapeDtypeStruct((4, 2, tm, tn), out_dtype)
    elif cache_a:
        out_specs = [o_spec, pl.BlockSpec((tm, 1), lambda i, j, k: (i, 0))]
        out_shape = [jax.ShapeDtypeStruct((M, N), out_dtype), jax.ShapeDtypeStruct((M, 1), F32)]
    else:
        out_specs, out_shape = o_spec, jax.ShapeDtypeStruct((M, N), out_dtype)
    scratch_shapes = [pltpu.VMEM((tm, tn), F32)] + ([pltpu.VMEM((tm, tk), BF16)] if cache_a else [])
    return pl.pallas_call(
        body, name=name, grid=(M // tm, N // tn, nk), in_specs=specs, out_specs=out_specs, out_shape=out_shape,
        scratch_shapes=scratch_shapes, compiler_params=_params(),
    )(*ins)


def _piece_spec(tm, piece):
    _, w, c0, per_group = piece
    if per_group:
        return pl.BlockSpec((tm, w), lambda i, g: (i, c0 + g))
    return pl.BlockSpec((tm, w), lambda i, g: (i, c0))


def _const_spec(p):
    return pl.BlockSpec(p.shape, lambda i, g: (0, 0))


def _rowwise(name, fn, params, rows, auxs, outs, tm, groups=1):
    S = rows[0][0].shape[0]
    tm = min(tm, S)
    n_p, n_r, n_a = len(params), len(rows), len(auxs)

    def body(*refs):
        p = [r[...] for r in refs[:n_p]]
        r_ = [r[...] for r in refs[n_p:n_p + n_r]]
        a_ = [r[...] for r in refs[n_p + n_r:n_p + n_r + n_a]]
        for o_ref, o in zip(refs[n_p + n_r + n_a:], fn(p, r_, a_)):
            o_ref[...] = o.astype(o_ref.dtype)

    out_specs, out_shape = [], []
    for w, dt, per_group in outs:
        out_specs.append(_piece_spec(tm, (None, w, 0, per_group)))
        out_shape.append(jax.ShapeDtypeStruct((S, w * (groups if per_group else 1)), dt))
    return pl.pallas_call(
        body, name=name, grid=(S // tm, groups),
        in_specs=[_const_spec(p) for p in params] + [_piece_spec(tm, q) for q in list(rows) + list(auxs)],
        out_specs=out_specs, out_shape=out_shape, compiler_params=_params(),
    )(*params, *[q[0] for q in list(rows) + list(auxs)])


def _rowwise_vjp(name, fn, params, rows, auxs, cots, d_outs, tm, groups=1, adds=None):
    S = rows[0][0].shape[0]
    tm = min(tm, S)
    n_p, n_r, n_a = len(params), len(rows), len(auxs)
    cot_flat = [q for c in cots for q in c]
    adds = adds or [None] * len(d_outs)
    add_flat = [q for q in adds if q is not None]
    n_c, n_add = len(cot_flat), len(add_flat)
    shared = [not all(rows[k][3] for k in idx) and groups > 1 for idx, _ in d_outs]

    def body(*refs):
        pos = 0
        p = [r[...] for r in refs[pos:pos + n_p]]; pos += n_p
        r_ = [r[...] for r in refs[pos:pos + n_r]]; pos += n_r
        a_ = [r[...] for r in refs[pos:pos + n_a]]; pos += n_a
        c_refs = refs[pos:pos + n_c]; pos += n_c
        add_refs = list(refs[pos:pos + n_add]); pos += n_add
        d_refs = refs[pos:pos + len(d_outs)]; pos += len(d_outs)
        dp_refs = refs[pos:]
        i, g = pl.program_id(0), pl.program_id(1)
        outs, vjp_fn = jax.vjp(lambda pp, rr: fn(pp, rr, a_), p, r_)
        cts, ci = [], 0
        for c, o in zip(cots, outs):
            t = c_refs[ci][...].astype(F32)
            for extra in c_refs[ci + 1:ci + len(c)]:
                t = t + extra[...].astype(F32)
            ci += len(c)
            cts.append(t.astype(o.dtype))
        dp, dr = vjp_fn(cts)
        for (idx, _), d_ref, add, sh in zip(d_outs, d_refs, adds, shared):
            val = dr[idx[0]].astype(F32) if len(idx) == 1 else jnp.concatenate([dr[k].astype(F32) for k in idx], axis=1)
            if add is not None:
                val = val + add_refs.pop(0)[...].astype(F32)
            if sh:
                @pl.when(g == 0)
                def _(d_ref=d_ref):
                    d_ref[...] = jnp.zeros_like(d_ref)
                d_ref[...] += val.astype(d_ref.dtype)
            else:
                d_ref[...] = val.astype(d_ref.dtype)
        first = jnp.logical_and(i == 0, g == 0)
        for dp_ref, d in zip(dp_refs, dp):
            @pl.when(first)
            def _(dp_ref=dp_ref):
                dp_ref[...] = jnp.zeros_like(dp_ref)
            dp_ref[...] += d.astype(F32)

    out_specs, out_shape = [], []
    for (idx, dt), sh in zip(d_outs, shared):
        w = sum(rows[k][1] for k in idx)
        per_group = (not sh) and groups > 1
        out_specs.append(_piece_spec(tm, (None, w, 0, per_group)))
        out_shape.append(jax.ShapeDtypeStruct((S, w * (groups if per_group else 1)), dt))
    for p in params:
        out_specs.append(_const_spec(p))
        out_shape.append(jax.ShapeDtypeStruct(p.shape, F32))
    pieces = list(rows) + list(auxs) + cot_flat + add_flat
    res = pl.pallas_call(
        body, name=name, grid=(S // tm, groups),
        in_specs=[_const_spec(p) for p in params] + [_piece_spec(tm, q) for q in pieces],
        out_specs=out_specs, out_shape=out_shape, compiler_params=_params(),
    )(*params, *[q[0] for q in pieces])
    return list(res[:len(d_outs)]), list(res[len(d_outs):])


def _lane_roll(x, shift):
    @jax.custom_vjp
    def roll(v):
        return pltpu.roll(v, shift, 1)

    roll.defvjp(lambda v: (roll(v), None), lambda _, ct: (pltpu.roll(ct, LANES - shift, 1),))
    return roll(x)


@jax.custom_vjp
def _sigmoid(x):
    return 1.0 / (1.0 + jnp.exp(-x))


def _sigmoid_fwd(x):
    s = _sigmoid(x)
    return s, s


_sigmoid.defvjp(_sigmoid_fwd, lambda s, ct: (ct * s * (1.0 - s),))


def _rope(x, cos, sin_lo, sin_hi, half):
    return x * cos + _lane_roll(x, LANES - half) * sin_lo + _lane_roll(x, half) * sin_hi


def _f_rope_table(p, r, a):
    inv, first, second, fixed = p
    ang = a[0] * inv
    cs, sn = jnp.cos(ang), jnp.sin(ang)
    return [cs * (first + second) + fixed, -sn * first, sn * second]


def _f_rms(p, r, a):
    x = r[0].astype(F32)
    return [x * lax.rsqrt(jnp.mean(x * x, axis=-1, keepdims=True) + EPS) * p[0]]


def _f_mla_a(p, r, a):
    return _f_rms([p[0]], [r[0]], a) + _f_rms([p[1]], [r[1]], a)


def _f_mla_b(p, r, a):
    def norm_rope(v, g):
        ms = jnp.sum(v * v, axis=-1, keepdims=True) * (1.0 / MLA_QK)
        return _rope(v * lax.rsqrt(ms + EPS) * g, a[0], a[1], a[2], MLA_ROPE // 2)

    return [norm_rope(r[0].astype(F32), p[0]) * MLA_Q_SCALE, norm_rope(r[1].astype(F32) + r[2].astype(F32), p[1])]


def _f_ret_rope(p, r, a):
    q = _rope(r[0].astype(F32), a[0], a[1], a[2], RET_QK // 2)
    k = _rope(r[1].astype(F32), a[0], a[1], a[2], RET_QK // 2)
    return [q, k * (RET_QK ** -0.5)]


def _f_ret_post(p, r, a):
    ret = r[0].astype(F32) + r[1].astype(F32)
    g = r[2].astype(F32)
    normed = ret * lax.rsqrt(jnp.mean(ret * ret, axis=-1, keepdims=True) + EPS)
    return [g * _sigmoid(g) * normed]


def _f_merge(p, r, a):
    return [_sigmoid(r[0].astype(F32)) * r[2].astype(F32) + _sigmoid(r[1].astype(F32)) * r[3].astype(F32)]


def _f_swiglu(p, r, a):
    g = r[0].astype(F32)
    return [g * _sigmoid(g) * r[1].astype(F32)]


def _f_delta(p, r, a):
    d = jnp.sum(r[0].astype(F32) * r[1].astype(F32), axis=-1, keepdims=True)
    return [jnp.broadcast_to(d, r[0].shape)]


def _f_add(p, r, a):
    return [r[0].astype(F32) + r[1].astype(F32)]


def _loss_kernel(y, tgt):
    S, Dm = y.shape
    tm = min(ROW_TILE, S)

    def body(y_ref, t_ref, dy_ref, loss_ref):
        @pl.when(pl.program_id(0) == 0)
        def _():
            loss_ref[...] = jnp.zeros_like(loss_ref)

        e = y_ref[...] - t_ref[...]
        dy_ref[...] = e * (1.0 / Dm)
        loss_ref[...] += 0.5 * jnp.sum(jnp.mean(e * e, axis=-1, keepdims=True), axis=0, keepdims=True)

    row = pl.BlockSpec((tm, Dm), lambda i: (i, 0))
    return pl.pallas_call(
        body, name="loss", grid=(S // tm,), in_specs=[row, row],
        out_specs=[row, pl.BlockSpec((1, LANES), lambda i: (0, 0))],
        out_shape=[jax.ShapeDtypeStruct((S, Dm), F32), jax.ShapeDtypeStruct((1, LANES), F32)],
        compiler_params=_params(),
    )(y, tgt)


def _flash_fwd(q, k, kv, shards):
    S = q.shape[0]
    tq, tk = min(ATT_TQ, S), min(ATT_TK, S)
    nq, nk = S // tq, S // tk
    n = len(shards)

    def body(q_ref, k_ref, v_ref, *rest):
        shard_refs, (o_ref, lse_ref), gathered = rest[:n], rest[n:n + 2], rest[n + 2:2 * n + 2]
        m_sc, l_sc, acc_sc, send_sems, recv_sems = rest[2 * n + 2:]
        h, qi, ki = pl.program_id(0), pl.program_id(1), pl.program_id(2)

        @pl.when(jnp.logical_and(h == 0, jnp.logical_and(qi == 0, ki == 0)))
        def _():
            _gather_start(_gather_copies(shard_refs, gathered, send_sems, recv_sems))

        @pl.when(ki == 0)
        def _():
            m_sc[...] = jnp.full_like(m_sc, -jnp.inf)
            l_sc[...] = jnp.zeros_like(l_sc)
            acc_sc[...] = jnp.zeros_like(acc_sc)

        s = _dot(q_ref[...], k_ref[...], "nt")
        m_prev = m_sc[...]
        m_new = jnp.maximum(m_prev, jnp.max(s, axis=-1, keepdims=True))
        alpha = jnp.exp2(m_prev - m_new)
        p = jnp.exp2(s - m_new[:, :1])
        l_sc[...] = alpha * l_sc[...] + jnp.sum(p, axis=-1, keepdims=True)
        acc_sc[...] = alpha * acc_sc[...] + _dot(p.astype(BF16), v_ref[...])
        m_sc[...] = m_new

        @pl.when(ki == nk - 1)
        def _():
            o_ref[...] = (acc_sc[...] / l_sc[...]).astype(o_ref.dtype)
            lse_ref[...] = m_sc[...] + jnp.log2(l_sc[...])

        @pl.when(jnp.logical_and(h == HEADS - 1, jnp.logical_and(qi == nq - 1, ki == nk - 1)))
        def _():
            _gather_wait(_gather_copies(shard_refs, gathered, send_sems, recv_sems))

    qs = pl.BlockSpec((tq, LANES), lambda h, i, j: (i, h))
    res = pl.pallas_call(
        body, name="mla_fwd", grid=(HEADS, nq, nk),
        in_specs=[qs, pl.BlockSpec((tk, LANES), lambda h, i, j: (j, h)),
                  pl.BlockSpec((tk, LANES), lambda h, i, j: (j, HEADS + h))] + [ANY] * n,
        out_specs=[qs, qs] + [ANY] * n,
        out_shape=[jax.ShapeDtypeStruct((S, HEADS * LANES), BF16), jax.ShapeDtypeStruct((S, HEADS * LANES), F32)]
        + [jax.ShapeDtypeStruct((4,) + s.shape, s.dtype) for s in shards],
        scratch_shapes=[pltpu.VMEM((tq, LANES), F32)] * 3
        + [pltpu.SemaphoreType.DMA((3 * n,)), pltpu.SemaphoreType.DMA((3 * n,))],
        compiler_params=_params(),
    )(q, k, kv, *shards)
    mine = 2 * lax.axis_index("x") + lax.axis_index("y")
    return res[0], res[1], [_fill_slot(g, s, mine) for g, s in zip(res[2:], shards)]


def _flash_bwd(q, k, kv, do, lse, delta, gs):
    S = q.shape[0]
    tq, tk = min(ATT_BQ, S), min(ATT_BK, S)
    nq, nkt = S // tq, S // tk
    n = len(gs)

    def body(q_ref, k_ref, v_ref, do_ref, lse_ref, dl_ref, *rest):
        g_refs, (dq_ref, dk_ref, dv_ref), got_refs = rest[:n], rest[n:n + 3], rest[n + 3:2 * n + 3]
        dk_sc, dv_sc, send_sems, recv_sems = rest[2 * n + 3:]
        h, ki, qi = pl.program_id(0), pl.program_id(1), pl.program_id(2)

        @pl.when(jnp.logical_and(h == 0, jnp.logical_and(ki == 0, qi == 0)))
        def _():
            _scatter_start(_scatter_copies(g_refs, got_refs, send_sems, recv_sems))

        @pl.when(jnp.logical_and(ki == 0, qi == 0))
        def _():
            dq_ref[...] = jnp.zeros_like(dq_ref)

        @pl.when(qi == 0)
        def _():
            dk_sc[...] = jnp.zeros_like(dk_sc)
            dv_sc[...] = jnp.zeros_like(dv_sc)

        qv, kv_, dov = q_ref[...], k_ref[...], do_ref[...]
        p = jnp.exp2(_dot(qv, kv_, "nt") - lse_ref[...][:, :1])
        dp = _dot(dov, v_ref[...], "nt")
        ds = (p * (dp - dl_ref[...][:, :1]) * LN2).astype(BF16)
        dv_sc[...] += _dot(p.astype(BF16), dov, "tn")
        dk_sc[...] += _dot(ds, qv, "tn")
        rows = pl.ds(pl.multiple_of(qi * tq, tq), tq)
        dq_ref[rows, :] += _dot(ds, kv_)

        @pl.when(qi == nq - 1)
        def _():
            dk_ref[...] = dk_sc[...].astype(dk_ref.dtype)
            dv_ref[...] = dv_sc[...].astype(dv_ref.dtype)

        @pl.when(jnp.logical_and(h == HEADS - 1, jnp.logical_and(ki == nkt - 1, qi == nq - 1)))
        def _():
            _scatter_wait(_scatter_copies(g_refs, got_refs, send_sems, recv_sems))

    qs = pl.BlockSpec((tq, LANES), lambda h, j, i: (i, h))
    ks = pl.BlockSpec((tk, LANES), lambda h, j, i: (j, h))
    res = pl.pallas_call(
        body, name="mla_bwd", grid=(HEADS, nkt, nq),
        in_specs=[qs, ks, pl.BlockSpec((tk, LANES), lambda h, j, i: (j, HEADS + h)), qs, qs, qs] + [ANY] * n,
        out_specs=[pl.BlockSpec((S, LANES), lambda h, j, i: (0, h)), ks, ks] + [ANY] * n,
        out_shape=[jax.ShapeDtypeStruct((S, HEADS * LANES), F32), jax.ShapeDtypeStruct((S, HEADS * LANES), BF16),
                   jax.ShapeDtypeStruct((S, HEADS * LANES), BF16)]
        + [jax.ShapeDtypeStruct((8,) + g.shape[2:], g.dtype) for g in gs],
        scratch_shapes=[pltpu.VMEM((tk, LANES), F32)] * 2
        + [pltpu.SemaphoreType.DMA((7 * n,)), pltpu.SemaphoreType.DMA((7 * n,))],
        compiler_params=_params(),
    )(q, k, kv, do, lse, delta, *gs)
    return res[0], res[1], res[2], [_own_piece(got, g) for got, g in zip(res[3:], gs)]


def _ret_tables(decay_row, backward):
    C = RET_CHUNK
    lg = -jnp.exp(decay_row)
    t = lax.broadcasted_iota(jnp.int32, (C, C), 0).astype(F32)
    s = lax.broadcasted_iota(jnp.int32, (C, C), 1).astype(F32)
    ridx = lax.broadcasted_iota(jnp.int32, (C, LANES), 0).astype(F32)
    if backward:
        dist, mask, aw, bw = s - t, s > t, C - ridx, ridx
    else:
        dist, mask, aw, bw = t - s, t >= s, ridx + 1.0, C - 1.0 - ridx
    dist = jnp.maximum(dist, 0.0)
    din = jnp.where(mask, jnp.exp(lg[:, :1] * dist), 0.0)
    return dict(din=din, dist=dist, a=jnp.exp(lg * aw), b=jnp.exp(lg * bw), c=jnp.exp(lg * C), aw=aw, bw=bw)


def _ret_fwd(qr, kr, proj, v_block, dec_f, dec_b):
    S = qr.shape[0]
    C = RET_CHUNK
    n = S // C
    W = HEADS * LANES

    def body(qf, kf, vf, qb, kb, vb, df, db, of, ob, sf_out, sb_out, st):
        @pl.when(pl.program_id(0) == 0)
        def _():
            st[...] = jnp.zeros_like(st)

        for d, (q_ref, k_ref, v_ref, dec, o_ref, s_out) in enumerate(
                [(qf, kf, vf, df, of, sf_out), (qb, kb, vb, db, ob, sb_out)]):
            for h in range(HEADS):
                lanes = slice(h * LANES, (h + 1) * LANES)
                tb = _ret_tables(dec[h:h + 1, :], d == 1)
                qf32, kf32, v = q_ref[:, lanes].astype(F32), k_ref[:, lanes].astype(F32), v_ref[:, lanes]
                state = st[d, h]
                s_out[0, h] = state
                inner = _dot((_dot(qf32.astype(BF16), kf32.astype(BF16), "nt") * tb["din"]).astype(BF16), v)
                cross = _dot((qf32 * tb["a"]).astype(BF16), state.astype(BF16))
                o_ref[:, lanes] = inner + cross
                st[d, h] = state * tb["c"] + _dot((kf32 * tb["b"]).astype(BF16), v, "tn")

    fw = lambda c0: pl.BlockSpec((C, W), lambda j: (j, c0))
    bw = lambda c0: pl.BlockSpec((C, W), lambda j: (n - 1 - j, c0))
    dec_spec = pl.BlockSpec((HEADS, LANES), lambda j: (0, 0))
    st_shape = jax.ShapeDtypeStruct((n, HEADS, LANES, LANES), F32)
    return pl.pallas_call(
        body, name="ret_fwd", grid=(n,),
        in_specs=[fw(0), fw(0), fw(v_block), bw(0), bw(0), bw(v_block), dec_spec, dec_spec],
        out_specs=[fw(0), bw(0), pl.BlockSpec((1, HEADS, LANES, LANES), lambda j: (j, 0, 0, 0)),
                   pl.BlockSpec((1, HEADS, LANES, LANES), lambda j: (n - 1 - j, 0, 0, 0))],
        out_shape=[jax.ShapeDtypeStruct((S, W), F32)] * 2 + [st_shape] * 2,
        scratch_shapes=[pltpu.VMEM((2, HEADS, LANES, LANES), F32)], compiler_params=_params(),
    )(qr, kr, proj, qr, kr, proj, dec_f, dec_b)


def _ret_bwd(qr, kr, proj, v_block, dret, sf, sb, dec_f, dec_b):
    S = qr.shape[0]
    C = RET_CHUNK
    n = S // C
    W = HEADS * LANES

    def body(qf, kf, vf, gf, sf_ref, qb, kb, vb, gb, sb_ref, df, db,
             dqf, dkf, dvf, dqb, dkb, dvb, ddf, ddb, ds_sc):
        j = pl.program_id(0)

        @pl.when(j == 0)
        def _():
            ds_sc[...] = jnp.zeros_like(ds_sc)
            ddf[...] = jnp.zeros_like(ddf)
            ddb[...] = jnp.zeros_like(ddb)

        for d, (q_ref, k_ref, v_ref, g_ref, s_ref, dec, dq_ref, dk_ref, dv_ref, dd_ref) in enumerate(
                [(qf, kf, vf, gf, sf_ref, df, dqf, dkf, dvf, ddf), (qb, kb, vb, gb, sb_ref, db, dqb, dkb, dvb, ddb)]):
            for h in range(HEADS):
                lanes = slice(h * LANES, (h + 1) * LANES)
                tb = _ret_tables(dec[h:h + 1, :], d == 1)
                v, g = v_ref[:, lanes], g_ref[:, lanes]
                qf32, kf32 = q_ref[:, lanes].astype(F32), k_ref[:, lanes].astype(F32)
                q, k = qf32.astype(BF16), kf32.astype(BF16)
                state, dstate = s_ref[0, h], ds_sc[d, h]
                dstate_b = dstate.astype(BF16)
                dp = _dot(g, v, "nt")
                a_ = _dot(q, k, "nt")
                da = (dp * tb["din"]).astype(BF16)
                g1 = _dot(g, state.astype(BF16), "nt")
                g2 = _dot(v, dstate_b, "nt")
                dq_ref[:, lanes] = (_dot(da, k) + g1 * tb["a"]).astype(dq_ref.dtype)
                dk_ref[:, lanes] = (_dot(da, q, "tn") + g2 * tb["b"]).astype(dk_ref.dtype)
                dv_ref[:, lanes] = (_dot((a_ * tb["din"]).astype(BF16), g, "tn")
                                    + _dot((kf32 * tb["b"]).astype(BF16), dstate_b)).astype(dv_ref.dtype)
                dlg = (jnp.sum(dp * a_ * tb["din"] * tb["dist"], keepdims=True)
                       + jnp.sum(g1 * qf32 * tb["a"] * tb["aw"], keepdims=True)
                       + jnp.sum(g2 * kf32 * tb["b"] * tb["bw"], keepdims=True)
                       + C * jnp.sum(tb["c"] * dstate * state, keepdims=True))
                dd_ref[h:h + 1, :] += jnp.broadcast_to(dlg, (1, LANES))
                ds_sc[d, h] = dstate * tb["c"] + _dot((qf32 * tb["a"]).astype(BF16), g, "tn")

        @pl.when(j == n - 1)
        def _():
            ddf[...] = ddf[...] * -jnp.exp(df[...])
            ddb[...] = ddb[...] * -jnp.exp(db[...])

    fw = lambda c0: pl.BlockSpec((C, W), lambda j: (n - 1 - j, c0))
    bw = lambda c0: pl.BlockSpec((C, W), lambda j: (j, c0))
    dec_spec = pl.BlockSpec((HEADS, LANES), lambda j: (0, 0))
    act = jax.ShapeDtypeStruct((S, W), BF16)
    return pl.pallas_call(
        body, name="ret_bwd", grid=(n,),
        in_specs=[fw(0), fw(0), fw(v_block), fw(0), pl.BlockSpec((1, HEADS, LANES, LANES), lambda j: (n - 1 - j, 0, 0, 0)),
                  bw(0), bw(0), bw(v_block), bw(0), pl.BlockSpec((1, HEADS, LANES, LANES), lambda j: (j, 0, 0, 0)),
                  dec_spec, dec_spec],
        out_specs=[fw(0)] * 3 + [bw(0)] * 3 + [dec_spec] * 2,
        out_shape=[act] * 6 + [jax.ShapeDtypeStruct((HEADS, LANES), F32)] * 2,
        scratch_shapes=[pltpu.VMEM((2, HEADS, LANES, LANES), F32)], compiler_params=_params(),
    )(qr, kr, proj, dret, sf, qr, kr, proj, dret, sb, dec_f, dec_b)


def _pad_heads(w, hd):
    K = w.shape[0]
    return jnp.pad(w.reshape(K, HEADS, hd), ((0, 0), (0, 0), (0, LANES - hd))).reshape(K, HEADS * LANES)


def _unpad_heads(w, hd):
    K = w.shape[0]
    return w.reshape(K, HEADS, LANES)[:, :, :hd].reshape(K, HEADS * hd)


def _rope_consts(first_lane, half):
    lane = np.arange(LANES)
    first = ((lane >= first_lane) & (lane < first_lane + half)).astype(np.float32)
    second = ((lane >= first_lane + half) & (lane < first_lane + 2 * half)).astype(np.float32)
    fixed = (lane < first_lane).astype(np.float32)
    j = np.where(first > 0, lane - first_lane, lane - first_lane - half) * (first + second)
    inv = (ROPE_THETA ** (-j.astype(np.float64) / half)).astype(np.float32)
    return [jnp.asarray(v.reshape(1, LANES), F32) for v in (inv, first, second, fixed)]


def _assemble(name, gathered):
    if name in COL_SHARDED:
        return jnp.transpose(gathered, (1, 0, 2)).reshape(gathered.shape[1], 4 * gathered.shape[2])
    return gathered.reshape(4 * gathered.shape[1], gathered.shape[2])


def _split_for_reducers(name, g, dtype):
    if name in COL_SHARDED:
        K, N4 = g.shape
        return jnp.transpose(g.reshape(2, K // 2, 4, N4 // 4), (2, 0, 1, 3)).astype(dtype)
    return g.reshape(4, 2, g.shape[0] // 8, g.shape[1]).astype(dtype)


def _local_step(x, pos, tgt, wts, late_shards, small):
    w_in = wts["w_in"]
    seg = [w_in[:, IN_OFFS[i]:IN_OFFS[i + 1]] for i in range(8)]
    kr_w = jnp.pad(seg[2], ((0, 0), (MLA_NOPE, LANES - MLA_QK)))
    w_in_p = jnp.concatenate([seg[7], seg[5], seg[6], _pad_heads(seg[3], RET_QK), _pad_heads(seg[4], RET_QK),
                              seg[0], seg[1], kr_w], axis=1)
    w_qb_p = _pad_heads(wts["w_q_b"], MLA_QK)
    kvw = wts["w_kv_b"].reshape(MLA_KV_RANK, HEADS, MLA_NOPE + MLA_V)
    pad_kv = lambda t: jnp.pad(t, ((0, 0), (0, 0), (0, LANES - t.shape[2]))).reshape(MLA_KV_RANK, HEADS * LANES)
    w_kn_p, w_v_p = pad_kv(kvw[:, :, :MLA_NOPE]), pad_kv(kvw[:, :, MLA_NOPE:])
    w_kv_p = jnp.concatenate([w_kn_p, w_v_p], axis=1)
    g_qn_p = jnp.pad(small["g_qn"], ((0, 0), (0, LANES - MLA_QK)))
    g_kn_p = jnp.pad(small["g_kn"], ((0, 0), (0, LANES - MLA_QK)))
    dec_f = jnp.broadcast_to(small["ret_decay_fwd"].reshape(HEADS, 1), (HEADS, LANES))
    dec_b = jnp.broadcast_to(small["ret_decay_bwd"].reshape(HEADS, 1), (HEADS, LANES))
    T, N = True, False
    RT, HT = ROW_TILE, HEAD_ROW_TILE

    tab_m = _rowwise("rope_table_mla", _f_rope_table, _rope_consts(MLA_NOPE, MLA_ROPE // 2), [(pos, 1, 0, N)], [(pos, 1, 0, N)],
                     [(LANES, F32, N)] * 3, HT)
    tab_r = _rowwise("rope_table_ret", _f_rope_table, _rope_consts(0, RET_QK // 2), [(pos, 1, 0, N)], [(pos, 1, 0, N)],
                     [(LANES, F32, N)] * 3, HT)
    aux_m = [(t, LANES, 0, N) for t in tab_m]
    aux_r = [(t, LANES, 0, N) for t in tab_r]

    proj, rstd1 = _mm("proj", x, w_in_p, "nn", BF16, a_gain=small["g_mix"])
    rows_a = [(proj, MLA_Q_RANK, 24, N), (proj, MLA_KV_RANK, 50, N)]
    cqn, ckvn = _rowwise("mla_lat_norm", _f_mla_a, [small["g_q_a"], small["g_kv_a"]], rows_a, [],
                         [(MLA_Q_RANK, BF16, N), (MLA_KV_RANK, BF16, N)], RT)
    qraw = _mm("mla_q_up", cqn, w_qb_p, "nn", BF16)
    kv = _mm("mla_kv_up", ckvn, w_kv_p, "nn", BF16)
    rows_b = [(qraw, LANES, 0, T), (kv, LANES, 0, T), (proj, LANES, 51, N)]
    q, k = _rowwise("mla_qk_norm_rope", _f_mla_b, [g_qn_p, g_kn_p], rows_b, aux_m, [(LANES, BF16, T)] * 2, HT, HEADS)
    o, lse, late = _flash_fwd(q, k, kv, [late_shards[n] for n in LATE])
    wl = {n: _assemble(n, g) for n, g in zip(LATE, late)}
    w_mla_p = jnp.pad(wl["w_mla_out"].reshape(HEADS, MLA_V, D_MODEL), ((0, 0), (0, LANES - MLA_V), (0, 0))).reshape(HEADS * LANES, D_MODEL)
    w_ret_out, w_out, w_gu, w_down = wl["w_ret_out"], wl["w_out"], wl["w_gate_up"], wl["w_down"]
    y_a = _mm("mla_out", o, w_mla_p, "nn", F32)
    rows_rr = [(proj, LANES, 32, T), (proj, LANES, 40, T)]
    qr, kr = _rowwise("ret_rope", _f_ret_rope, [], rows_rr, aux_r, [(LANES, RET_QK_DTYPE, T)] * 2, HT, HEADS)
    ret_f, ret_b, st_f, st_b = _ret_fwd(qr, kr, proj, 2, dec_f, dec_b)
    rows_rp = [(ret_f, LANES, 0, T), (ret_b, LANES, 0, T), (proj, LANES, 24, T)]
    (o_b,) = _rowwise("ret_post", _f_ret_post, [], rows_rp, [], [(LANES, BF16, T)], HT, HEADS)
    y_b = _mm("ret_out", o_b, w_ret_out, "nn", F32)
    rows_m = [(proj, D_MODEL, 0, N), (proj, D_MODEL, 1, N), (y_a, D_MODEL, 0, N), (y_b, D_MODEL, 0, N)]
    (merged,) = _rowwise("merge", _f_merge, [], rows_m, [], [(D_MODEL, BF16, N)], RT)
    x2 = _mm("mix_out", merged, w_out, "nn", F32, res=x)
    gu, rstd2 = _mm("ffn_gate_up", x2, w_gu, "nn", BF16, a_gain=small["g_ffn"])
    rows_sw = [(gu, FFN_HIDDEN, 0, N), (gu, FFN_HIDDEN, 1, N)]
    (act,) = _rowwise("swiglu", _f_swiglu, [], rows_sw, [], [(FFN_HIDDEN, BF16, N)], RT)
    dy, dy_b16, loss_row = _mm("ffn_down", act, w_down, "nn", None,
                               epilogue=(_epi_loss, [x2, tgt], [], [F32, BF16], [(1, LANES)]))

    dact = _mm("d_act", dy_b16, w_down, "nt", BF16)
    dw_down = _mm("dw_down", act, dy_b16, "tn", BF16)
    (dgu,), _ = _rowwise_vjp("swiglu_bwd", _f_swiglu, [], rows_sw, [], [[(dact, FFN_HIDDEN, 0, N)]], [([0, 1], BF16)], RT)
    dx2, dx2_b16, dg_ffn = _mm("d_h2", dgu, w_gu, "nt", None,
                               epilogue=(_epi_rms_bwd(2), [x2, dy], [small["g_ffn"]], [F32, BF16], [(1, D_MODEL)]))
    dw_gu = _mm("dw_gate_up", x2, dgu, "tn", BF16, a_scale=(rstd2, small["g_ffn"]), shard_out=True)
    dmerged = _mm("d_merged", dx2_b16, w_out, "nt", BF16)
    dw_out = _mm("dw_out", merged, dx2_b16, "tn", BF16)
    (dgl, dy_a, dy_b), _ = _rowwise_vjp("merge_bwd", _f_merge, [], rows_m, [], [[(dmerged, D_MODEL, 0, N)]],
                                        [([0, 1], BF16), ([2], BF16), ([3], BF16)], RT)
    do_b = _mm("d_ret_o", dy_b, w_ret_out, "nt", BF16)
    dw_ret_out = _mm("dw_ret_out", o_b, dy_b, "tn", BF16)
    (dret, dg_r), _ = _rowwise_vjp("ret_post_bwd", _f_ret_post, [], rows_rp, [], [[(do_b, LANES, 0, T)]],
                                   [([0], BF16), ([2], BF16)], HT, HEADS)
    dqf, dkf, dvf, dqb, dkb, dvb, ddec_f, ddec_b = _ret_bwd(qr, kr, proj, 2, dret, st_f, st_b, dec_f, dec_b)
    (dq_r, dk_r), _ = _rowwise_vjp("ret_rope_bwd", _f_ret_rope, [], rows_rr, aux_r,
                                   [[(dqf, LANES, 0, T), (dqb, LANES, 0, T)], [(dkf, LANES, 0, T), (dkb, LANES, 0, T)]],
                                   [([0], BF16), ([1], BF16)], HT, HEADS)
    (dv_r,) = _rowwise("ret_dv_sum", _f_add, [], [(dvf, D_MODEL, 0, N), (dvb, D_MODEL, 0, N)], [], [(D_MODEL, BF16, N)], RT)
    do = _mm("d_mla_o", dy_a, w_mla_p, "nt", BF16)
    dw_mla_p = _mm("dw_mla_out", o, dy_a, "tn", BF16)
    (delta,) = _rowwise("mla_delta", _f_delta, [], [(do, LANES, 0, T), (o, LANES, 0, T)], [], [(LANES, F32, T)], HT, HEADS)
    dw_mla = dw_mla_p.reshape(HEADS, LANES, D_MODEL)[:, :MLA_V].reshape(HEADS * MLA_V, D_MODEL)
    late_grads = {"w_mla_out": dw_mla, "w_ret_out": dw_ret_out, "w_out": dw_out, "w_down": dw_down}
    late_gs = [dw_gu if n == "w_gate_up" else _split_for_reducers(n, late_grads[n], BF16) for n in LATE]
    dq, dk, dv, late_got = _flash_bwd(q, k, kv, do, lse, delta, late_gs)
    (dqraw, dkn, dkr), (dg_qn_p, dg_kn_p) = _rowwise_vjp(
        "mla_qk_norm_rope_bwd", _f_mla_b, [g_qn_p, g_kn_p], rows_b, aux_m, [[(dq, LANES, 0, T)], [(dk, LANES, 0, T)]],
        [([0], BF16), ([1], BF16), ([2], F32)], HT, HEADS)
    dckvn = _mm("d_ckvn_v", dv, w_v_p, "nt", BF16, res=_mm("d_ckvn_k", dkn, w_kn_p, "nt", F32))
    dw_kn_p = _mm("dw_kv_k", ckvn, dkn, "tn", BF16)
    dw_v_p = _mm("dw_kv_v", ckvn, dv, "tn", BF16)
    dcqn = _mm("d_cqn", dqraw, w_qb_p, "nt", BF16)
    dw_qb_p = _mm("dw_q_b", cqn, dqraw, "tn", BF16)
    (dcq, dckv), (dg_q_a, dg_kv_a) = _rowwise_vjp(
        "mla_lat_norm_bwd", _f_mla_a, [small["g_q_a"], small["g_kv_a"]], rows_a, [],
        [[(dcqn, MLA_Q_RANK, 0, N)], [(dckvn, MLA_KV_RANK, 0, N)]], [([0], BF16), ([1], BF16)], RT)
    dproj = jnp.concatenate([dgl, dv_r, dg_r, dq_r, dk_r, dcq, dckv, dkr.astype(BF16)], axis=1)
    dx, dg_mix = _mm("d_h", dproj, w_in_p, "nt", None,
                     epilogue=(_epi_rms_bwd(1), [x, dx2], [small["g_mix"]], [F32], [(1, D_MODEL)]))
    dw_in_p = _mm("dw_in", x, dproj, "tn", BF16, a_scale=(rstd1, small["g_mix"]))

    c = lambda a, b_: dw_in_p[:, a:b_]
    dw_in = jnp.concatenate([c(6144, 6400), c(6400, 6528), c(6528 + MLA_NOPE, 6528 + MLA_QK), _unpad_heads(c(4096, 5120), RET_QK),
                             _unpad_heads(c(5120, 6144), RET_QK), c(2048, 3072), c(3072, 4096), c(0, 2048)], axis=1)
    un_kv = lambda t: t.reshape(MLA_KV_RANK, HEADS, LANES)[:, :, :MLA_NOPE]
    dw_kv = jnp.concatenate([un_kv(dw_kn_p), un_kv(dw_v_p)], axis=2).reshape(MLA_KV_RANK, HEADS * (MLA_NOPE + MLA_V))
    grads = {"w_in": dw_in, "w_q_b": _unpad_heads(dw_qb_p, MLA_QK), "w_kv_b": dw_kv}
    sgrads = {"g_mix": dg_mix, "g_q_a": dg_q_a, "g_kv_a": dg_kv_a, "g_qn": dg_qn_p[:, :MLA_QK], "g_kn": dg_kn_p[:, :MLA_QK],
              "ret_decay_fwd": ddec_f[:, 0].reshape(1, HEADS), "ret_decay_bwd": ddec_b[:, 0].reshape(1, HEADS), "g_ffn": dg_ffn}
    return loss_row, dx, late_got, grads, sgrads


def _coords():
    return lax.axis_index("x"), lax.axis_index("y"), lax.axis_index("c")


def _other_chips(x, y):
    return [(1 - x, y), (x, 1 - y), (1 - x, 1 - y)]


ANY = pl.BlockSpec(memory_space=pl.ANY)


def _gather_copies(ins, outs, send_sems, recv_sems):
    x, y, c = _coords()
    mine = 2 * x + y
    sends, arrivals = [], []
    for w in range(len(ins)):
        for j, (cx, cy) in enumerate(_other_chips(x, y)):
            sems = dict(send_sem=send_sems.at[3 * w + j], recv_sem=recv_sems.at[3 * w + j],
                        device_id=(cx, cy, c), device_id_type=MESH)
            sends.append(pltpu.make_async_remote_copy(src_ref=ins[w], dst_ref=outs[w].at[mine], **sems))
            arrivals.append(functools.partial(pltpu.make_async_remote_copy, src_ref=ins[w],
                                              dst_ref=outs[w].at[2 * cx + cy], **sems))
    return sends, arrivals


def _gather_start(copies):
    for cp in copies[0]:
        cp.start()


def _gather_wait(copies):
    sends, arrivals = copies
    for make in arrivals:
        make().wait_recv()
    for cp in sends:
        cp.wait_send()


def _fill_slot(buf, piece, slot):
    return lax.dynamic_update_slice(buf, piece[None], (slot,) + (0,) * piece.ndim)


def _weight_gather_first(shards):
    n = len(shards)

    def body(*refs):
        ins, outs = refs[:n], refs[n:2 * n]
        send_sems, recv_sems = refs[2 * n:]
        x, y, c = _coords()
        chips = _other_chips(x, y)
        mine = 2 * x + y

        def half(ref, slot, core):
            rows = ref.shape[1] // 2
            return ref.at[slot, pl.ds(pl.multiple_of(core * rows, 8), rows)]

        def copy(w, k, slot, core, to, src=None):
            return pltpu.make_async_remote_copy(
                src_ref=half(outs[w], slot, core) if src is None else src, dst_ref=half(outs[w], slot, core),
                send_sem=send_sems.at[6 * w + k], recv_sem=recv_sems.at[6 * w + k], device_id=to, device_id_type=MESH)

        first, passed = [], []
        for w in range(n):
            rows = ins[w].shape[0] // 2
            my_half = ins[w].at[pl.ds(pl.multiple_of(c * rows, 8), rows)]
            for j, (cx, cy) in enumerate(chips):
                cp = copy(w, j, mine, c, (cx, cy, c), src=my_half)
                cp.start()
                first.append(cp)
        for w in range(n):
            for j, (cx, cy) in enumerate(chips):
                copy(w, j, 2 * cx + cy, c, (x, y, c)).wait_recv()
                cp = copy(w, 3 + j, 2 * cx + cy, c, (x, y, 1 - c))
                cp.start()
                passed.append(cp)
        for w in range(n):
            for j, (cx, cy) in enumerate(chips):
                copy(w, 3 + j, 2 * cx + cy, 1 - c, (x, y, c)).wait_recv()
        for cp in first + passed:
            cp.wait_send()

    return pl.pallas_call(
        body, name="weight_gather_first", in_specs=[ANY] * n, out_specs=[ANY] * n,
        out_shape=[jax.ShapeDtypeStruct((4,) + s.shape, s.dtype) for s in shards],
        scratch_shapes=[pltpu.SemaphoreType.DMA((6 * n,)), pltpu.SemaphoreType.DMA((6 * n,))],
    )(*shards)


def _scatter_copies(ins, outs, send_sems, recv_sems):
    x, y, c = _coords()
    me = 4 * x + 2 * y + c
    sends, arrivals = [], []
    for w in range(len(ins)):
        for k in range(1, 8):
            px, py, pc = x ^ (k >> 2), y ^ ((k >> 1) & 1), c ^ (k & 1)
            sems = dict(send_sem=send_sems.at[7 * w + k - 1], recv_sem=recv_sems.at[7 * w + k - 1],
                        device_id=(px, py, pc), device_id_type=MESH)
            sends.append(pltpu.make_async_remote_copy(src_ref=ins[w].at[2 * px + py, pc], dst_ref=outs[w].at[me], **sems))
            arrivals.append(functools.partial(
                pltpu.make_async_remote_copy, src_ref=ins[w].at[2 * px + py, pc],
                dst_ref=outs[w].at[4 * px + 2 * py + pc], **sems))
    return sends, arrivals


_scatter_start, _scatter_wait = _gather_start, _gather_wait


def _own_piece(got, g):
    x, y, c = _coords()
    mine = lax.dynamic_slice(g, (2 * x + y, c, 0, 0), (1, 1) + g.shape[2:]).reshape(g.shape[2:])
    return _fill_slot(got, mine, 4 * x + 2 * y + c)


def _grad_scatter_late(gs):
    n = len(gs)

    def body(*refs):
        copies = _scatter_copies(refs[:n], refs[n:2 * n], *refs[2 * n:])
        _scatter_start(copies)
        _scatter_wait(copies)

    return pl.pallas_call(
        body, name="grad_scatter_late", in_specs=[ANY] * n, out_specs=[ANY] * n,
        out_shape=[jax.ShapeDtypeStruct((8,) + g.shape[2:], g.dtype) for g in gs],
        scratch_shapes=[pltpu.SemaphoreType.DMA((7 * n,)), pltpu.SemaphoreType.DMA((7 * n,))],
    )(*gs)


def _grad_sum8(name, got):
    _, R, W = got.shape
    tr = _pick(R, 256, 16)

    def body(g_ref, o_ref):
        total = g_ref[0].astype(F32)
        for d in range(1, 8):
            total = total + g_ref[d].astype(F32)
        o_ref[...] = total

    return pl.pallas_call(
        body, name=name, grid=(R // tr,), in_specs=[pl.BlockSpec((8, tr, W), lambda i: (0, i, 0))],
        out_specs=pl.BlockSpec((tr, W), lambda i: (i, 0)), out_shape=jax.ShapeDtypeStruct((R, W), F32),
        compiler_params=_params(),
    )(got)


def _sibling_exchange(gs):
    n = len(gs)

    def body(*refs):
        ins, outs, send_sems, recv_sems = refs[:n], refs[n:2 * n], refs[2 * n], refs[2 * n + 1]
        x, y, c = _coords()
        cps = []
        for w in range(n):
            cp = pltpu.make_async_remote_copy(
                src_ref=ins[w].at[:, 1 - c], dst_ref=outs[w], send_sem=send_sems.at[w], recv_sem=recv_sems.at[w],
                device_id=(x, y, 1 - c), device_id_type=MESH)
            cp.start()
            cps.append(cp)
        for cp in cps:
            cp.wait()

    return pl.pallas_call(
        body, name="grad_sibling_exchange", in_specs=[ANY] * n, out_specs=[ANY] * n,
        out_shape=[jax.ShapeDtypeStruct((4,) + g.shape[2:], F32) for g in gs],
        scratch_shapes=[pltpu.SemaphoreType.DMA((n,)), pltpu.SemaphoreType.DMA((n,))],
    )(*gs)


def _pair_sum(name, g, got, c_arr):
    _, _, R, W = g.shape
    tr = _pick(R, 256, 8)

    def body(c_ref, a_ref, b_ref, o_ref):
        o_ref[...] = a_ref[0] + b_ref[...]

    return pl.pallas_call(
        body, name=name,
        grid_spec=pltpu.PrefetchScalarGridSpec(
            num_scalar_prefetch=1, grid=(4, R // tr),
            in_specs=[pl.BlockSpec((1, 1, tr, W), lambda j, i, c_ref: (j, c_ref[0], i, 0)),
                      pl.BlockSpec((1, tr, W), lambda j, i, c_ref: (j, i, 0))],
            out_specs=pl.BlockSpec((1, tr, W), lambda j, i, c_ref: (j, i, 0))),
        out_shape=jax.ShapeDtypeStruct((4, R, W), F32), compiler_params=_params(),
    )(c_arr, g, got)


def _chip_exchange(parts):
    n = len(parts)

    def body(*refs):
        ins, outs, send_sems, recv_sems = refs[:n], refs[n:2 * n], refs[2 * n], refs[2 * n + 1]
        x, y, c = _coords()
        sends = []
        for w in range(n):
            for j, (cx, cy) in enumerate(_other_chips(x, y)):
                cp = pltpu.make_async_remote_copy(
                    src_ref=ins[w].at[2 * cx + cy], dst_ref=outs[w].at[j], send_sem=send_sems.at[3 * w + j],
                    recv_sem=recv_sems.at[3 * w + j], device_id=(cx, cy, c), device_id_type=MESH)
                cp.start()
                sends.append(cp)
        for cp in sends:
            cp.wait_recv()
        for cp in sends:
            cp.wait_send()

    return pl.pallas_call(
        body, name="grad_chip_exchange", in_specs=[ANY] * n, out_specs=[ANY] * n,
        out_shape=[jax.ShapeDtypeStruct((3,) + p.shape[1:], F32) for p in parts],
        scratch_shapes=[pltpu.SemaphoreType.DMA((3 * n,)), pltpu.SemaphoreType.DMA((3 * n,))],
    )(*parts)


def _chip_sum(name, part, got, slot_arr):
    _, R, W = part.shape
    tr = _pick(R, 256, 8)

    def body(s_ref, a_ref, b_ref, o_ref):
        o_ref[...] = ((a_ref[0] + b_ref[0]) + b_ref[1]) + b_ref[2]

    return pl.pallas_call(
        body, name=name,
        grid_spec=pltpu.PrefetchScalarGridSpec(
            num_scalar_prefetch=1, grid=(R // tr,),
            in_specs=[pl.BlockSpec((1, tr, W), lambda i, s_ref: (s_ref[0], i, 0)),
                      pl.BlockSpec((3, tr, W), lambda i, s_ref: (0, i, 0))],
            out_specs=pl.BlockSpec((tr, W), lambda i, s_ref: (i, 0))),
        out_shape=jax.ShapeDtypeStruct((R, W), F32), compiler_params=_params(),
    )(slot_arr, part, got)


def _half_exchange(halves):
    n = len(halves)

    def body(*refs):
        ins, outs, send_sems, recv_sems = refs[:n], refs[n:2 * n], refs[2 * n], refs[2 * n + 1]
        x, y, c = _coords()
        sends = []
        for w in range(n):
            cp = pltpu.make_async_remote_copy(
                src_ref=ins[w], dst_ref=outs[w], send_sem=send_sems.at[w], recv_sem=recv_sems.at[w],
                device_id=(x, y, 1 - c), device_id_type=MESH)
            cp.start()
            sends.append(cp)
        for cp in sends:
            cp.wait()

    got = pl.pallas_call(
        body, name="grad_half_exchange", in_specs=[ANY] * n, out_specs=[ANY] * n,
        out_shape=[jax.ShapeDtypeStruct(h.shape, F32) for h in halves],
        scratch_shapes=[pltpu.SemaphoreType.DMA((n,)), pltpu.SemaphoreType.DMA((n,))],
    )(*halves)
    c = lax.axis_index("c")
    return [jnp.where(c == 0, jnp.stack([mine, theirs]), jnp.stack([theirs, mine])) for mine, theirs in zip(halves, got)]


def _adamw_math(w, g, m, v):
    m2 = ADAM_B1 * m + (1.0 - ADAM_B1) * g
    v2 = ADAM_B2 * v + (1.0 - ADAM_B2) * (g * g)
    m_hat = m2 / (1.0 - ADAM_B1 ** ADAM_STEP)
    v_hat = v2 / (1.0 - ADAM_B2 ** ADAM_STEP)
    return -ADAM_LR * (m_hat / (jnp.sqrt(v_hat) + ADAM_EPS) + ADAM_WD * w), m2, v2


def _small_allreduce_adamw(pack_g, pack_w, pack_m, pack_v):
    def body(g_ref, w_ref, m_ref, v_ref, sum_ref, d_ref, m_out, v_out, land, send_sems, recv_sems):
        x, y, c = _coords()
        me = 4 * x + 2 * y + c
        land[me] = g_ref[...]
        sends = []
        for k in range(1, 8):
            peer = (x ^ (k >> 2), y ^ ((k >> 1) & 1), c ^ (k & 1))
            cp = pltpu.make_async_remote_copy(
                src_ref=g_ref, dst_ref=land.at[me], send_sem=send_sems.at[k - 1], recv_sem=recv_sems.at[k - 1],
                device_id=peer, device_id_type=MESH)
            cp.start()
            sends.append((cp, peer))
        for k, (cp, peer) in enumerate(sends):
            pltpu.make_async_remote_copy(
                src_ref=g_ref, dst_ref=land.at[4 * peer[0] + 2 * peer[1] + peer[2]], send_sem=send_sems.at[k],
                recv_sem=recv_sems.at[k], device_id=peer, device_id_type=MESH).wait_recv()
        for cp, _ in sends:
            cp.wait_send()
        total = land[0]
        for d in range(1, 8):
            total = total + land[d]
        sum_ref[...] = total
        d_ref[...], m_out[...], v_out[...] = _adamw_math(w_ref[...], total, m_ref[...], v_ref[...])

    vm = pl.BlockSpec(memory_space=pltpu.VMEM)
    shp = jax.ShapeDtypeStruct(pack_g.shape, F32)
    return pl.pallas_call(
        body, name="small_allreduce_adamw", in_specs=[vm] * 4, out_specs=[vm] * 4, out_shape=[shp] * 4,
        scratch_shapes=[pltpu.VMEM((8,) + pack_g.shape, F32), pltpu.SemaphoreType.DMA((7,)), pltpu.SemaphoreType.DMA((7,))],
    )(pack_g, pack_w, pack_m, pack_v)


def _adamw(name, w, g, m, v):
    R, C = w.shape
    tr = _pick(R, 256, 8)

    def body(w_ref, g_ref, m_ref, v_ref, d_out, m_out, v_out):
        d_out[...], m_out[...], v_out[...] = _adamw_math(w_ref[...], g_ref[...], m_ref[...], v_ref[...])

    spec = pl.BlockSpec((tr, C), lambda i: (i, 0))
    return pl.pallas_call(
        body, name=name, grid=(R // tr,), in_specs=[spec] * 4, out_specs=[spec] * 3,
        out_shape=[jax.ShapeDtypeStruct((R, C), F32)] * 3, compiler_params=_params(),
    )(w, g, m, v)


def _pack_small(vals, last):
    flat = jnp.concatenate([v.reshape(-1) for v in vals] + [last.reshape(-1)])
    return jnp.pad(flat, (0, SMALL_ROWS * LANES - flat.shape[0])).reshape(SMALL_ROWS, LANES)


def kernel(x, positions, g_mix, w_in, g_q_a, w_q_b, g_kv_a, w_kv_b, g_qn, g_kn, w_mla_out, ret_decay_fwd, ret_decay_bwd, w_ret_out, w_out, g_ffn, w_gate_up, w_down, loss_target, m_g_mix, m_w_in, m_g_q_a, m_w_q_b, m_g_kv_a, m_w_kv_b, m_g_qn, m_g_kn, m_w_mla_out, m_ret_decay_fwd, m_ret_decay_bwd, m_w_ret_out, m_w_out, m_g_ffn, m_w_gate_up, m_w_down, v_g_mix, v_w_in, v_g_q_a, v_w_q_b, v_g_kv_a, v_w_kv_b, v_g_qn, v_g_kn, v_w_mla_out, v_ret_decay_fwd, v_ret_decay_bwd, v_w_ret_out, v_w_out, v_g_ffn, v_w_gate_up, v_w_down):
    given = dict(locals())
    S = x.shape[1]
    xs, tgt = x.reshape(S, D_MODEL), loss_target.reshape(S, D_MODEL)
    pos = positions.reshape(S, 1).astype(F32)

    first_shards = [given[n].astype(BF16) for n in FIRST]
    my_chip = 2 * lax.axis_index("x") + lax.axis_index("y")
    wts = {n: _assemble(n, _fill_slot(g, s, my_chip))
           for n, g, s in zip(FIRST, _weight_gather_first(first_shards), first_shards)}
    late_shards = {n: given[n].astype(BF16) for n in LATE}
    small = {n: given[n].reshape(1, -1) for n in SMALL}

    loss_row, dx, late_got, grads, sgrads = _local_step(xs, pos, tgt, wts, late_shards, small)

    first_gs = [_split_for_reducers(n, grads[n], BF16) for n in FIRST]
    first_got = [_own_piece(got, g) for got, g in zip(_grad_scatter_late(first_gs), first_gs)]
    halves = [_grad_sum8("grad_sum_" + n, got) for n, got in zip(FIRST + LATE, first_got + list(late_got))]
    reduced = _half_exchange(halves)

    out = {}
    for n, r in zip(FIRST + LATE, reduced):
        g = r.reshape(given[n].shape)
        out["grad_" + n] = g
        out["delta_" + n], out["new_m_" + n], out["new_v_" + n] = _adamw("adamw_" + n, given[n], g, given["m_" + n], given["v_" + n])

    one = jnp.ones((1,), F32)
    pk = _small_allreduce_adamw(
        _pack_small([sgrads[n] for n in SMALL], loss_row[0, :1]),
        _pack_small([given[n] for n in SMALL], 0 * one),
        _pack_small([given["m_" + n] for n in SMALL], 0 * one),
        _pack_small([given["v_" + n] for n in SMALL], one))
    off = 0
    for n in SMALL:
        sz = given[n].shape[0]
        for pre, arr in zip(["grad_", "delta_", "new_m_", "new_v_"], pk):
            out[pre + n] = arr.reshape(-1)[off:off + sz]
        off += sz
    loss = pk[0].reshape(-1)[off]

    return (loss, dx.reshape(x.shape), *[out["grad_" + n] for n in WEIGHTS], *[out["delta_" + n] for n in WEIGHTS],
            *[out["new_m_" + n] for n in WEIGHTS], *[out["new_v_" + n] for n in WEIGHTS])
```

```python
import functools
import math

import numpy as np
import jax
import jax.numpy as jnp
from jax import lax
from jax.experimental import pallas as pl
from jax.experimental.pallas import tpu as pltpu

F32 = jnp.float32
BF16 = jnp.bfloat16
MESH = pl.DeviceIdType.MESH

D_MODEL = 1024
HEADS = 8
LANES = 128
MLA_Q_RANK, MLA_KV_RANK = 256, 128
MLA_NOPE, MLA_ROPE, MLA_V = 64, 32, 64
MLA_QK = MLA_NOPE + MLA_ROPE
LN2 = math.log(2.0)
MLA_Q_SCALE = MLA_QK ** -0.5 / LN2
RET_QK, RET_V, RET_CHUNK = 64, 128, 128
RET_QK_DTYPE = BF16
FFN_HIDDEN = 2816
ROPE_THETA = 10000.0
EPS = 1e-6
IN_SPLITS = [256, 128, 32, 512, 512, 1024, 1024, 2048]
IN_OFFS = [0] + list(np.cumsum(IN_SPLITS))
ADAM_LR, ADAM_B1, ADAM_B2, ADAM_EPS, ADAM_WD, ADAM_STEP = 0.001, 0.9, 0.999, 1e-08, 0.01, 10

VMEM_LIMIT = 56 * 1024 * 1024
ROW_TILE = 256
HEAD_ROW_TILE = 2048
MM_TM, MM_TN, MM_TK, MM_KFULL = 1408, 2048, 2048, 2816
ATT_TQ, ATT_TK = 512, 2048
ATT_BQ, ATT_BK = 1024, 1024
ATT_HEADS_PER_STEP = 8
ATT_BWD_HEADS_PER_STEP = 4

SHARDED = ["w_in", "w_q_b", "w_kv_b", "w_mla_out", "w_ret_out", "w_out", "w_gate_up", "w_down"]
COL_SHARDED = {"w_in", "w_q_b", "w_kv_b", "w_mla_out", "w_gate_up"}
FIRST = ["w_in", "w_q_b", "w_kv_b"]
LATE = ["w_mla_out", "w_ret_out", "w_out", "w_gate_up", "w_down"]
SMALL = ["g_mix", "g_q_a", "g_kv_a", "g_qn", "g_kn", "ret_decay_fwd", "ret_decay_bwd", "g_ffn"]
WEIGHTS = ["g_mix", "w_in", "g_q_a", "w_q_b", "g_kv_a", "w_kv_b", "g_qn", "g_kn", "w_mla_out",
           "ret_decay_fwd", "ret_decay_bwd", "w_ret_out", "w_out", "g_ffn", "w_gate_up", "w_down"]
SMALL_ROWS = 24


def _params(**kw):
    return pltpu.CompilerParams(vmem_limit_bytes=VMEM_LIMIT, **kw)


def _pick(dim, target, unit=128):
    if dim <= target:
        return dim
    best = None
    for d in range(unit, target + 1, unit):
        if dim % d == 0:
            best = d
    assert best is not None, (dim, target)
    return best


_DOT = {"nn": (((1,), (0,)), ((), ())), "nt": (((1,), (1,)), ((), ())), "tn": (((0,), (0,)), ((), ()))}


def _dot(a, b, mode="nn"):
    return lax.dot_general(a, b, _DOT[mode], preferred_element_type=F32)


def _rms_rows(x, g):
    x = x.astype(F32)
    return x * lax.rsqrt(jnp.mean(x * x, axis=-1, keepdims=True) + EPS) * g


def _epi_loss(acc, extras, params):
    e = acc + extras[0] - extras[1]
    dy = e * (1.0 / D_MODEL)
    loss = 0.5 * jnp.sum(jnp.mean(e * e, axis=-1, keepdims=True), axis=0, keepdims=True)
    return [dy, dy], [jnp.broadcast_to(loss, (1, LANES))]


def _epi_rms_bwd(n_out):
    def fn(acc, extras, params):
        _, vjp = jax.vjp(_rms_rows, extras[0], params[0])
        dx, dg = vjp(acc)
        return [dx + extras[1]] * n_out, [dg]
    return fn


def _mm(name, a, b, mode, out_dtype, res=None, a_gain=None, a_scale=None, epilogue=None, shard_out=False,
        scatter=None):
    if mode == "nn":
        (M, K), (K2, N) = a.shape, b.shape
    elif mode == "nt":
        (M, K), (N, K2) = a.shape, b.shape
    else:
        (K, M), (K2, N) = a.shape, b.shape
    assert K == K2, (name, a.shape, b.shape)
    tm, tn = _pick(M, MM_TM), _pick(N, MM_TN)
    tk = K if K <= MM_KFULL else _pick(K, MM_TK)
    if shard_out:
        tm, tn = M // 2, N // 4
    if a_scale is not None:
        assert mode == "tn", name
        tk = _pick(K, MM_TK // 2)
    if epilogue is not None:
        tm = _pick(M, MM_TM // 2)
    nk = K // tk
    cache_a = a_gain is not None
    if a_gain is not None:
        assert mode == "nn" and tk == K and epilogue is None and not shard_out, name
    n_in = 2 + (res is not None) + (a_gain is not None) + 2 * (a_scale is not None)
    extras, eparams, e_outs, e_sums = ([], [], [], [])
    if epilogue is not None:
        assert tn == N and res is None and not shard_out, name
        epi_fn, extras, eparams, e_outs, e_sums = epilogue
    n_out = len(e_outs) + len(e_sums) if epilogue is not None else 1 + cache_a
    scatter = list(scatter or [])
    n_sc = len(scatter)
    assert not n_sc or epilogue is not None, name
    ni, nj = M // tm, N // tn

    def body(*refs):
        a_ref, b_ref = refs[0], refs[1]
        base = n_in + len(extras) + len(eparams)
        ex_refs = refs[n_in:n_in + len(extras)]
        ep_refs = refs[n_in + len(extras):base]
        sc_in, out_refs = refs[base:base + n_sc], refs[base + n_sc:base + n_sc + n_out]
        sc_out = refs[base + n_sc + n_out:base + 2 * n_sc + n_out]
        scratch = refs[base + 2 * n_sc + n_out:]
        acc = scratch[0]
        i, j, k = pl.program_id(0), pl.program_id(1), pl.program_id(2)

        if n_sc:
            @pl.when(jnp.logical_and(i == 0, jnp.logical_and(j == 0, k == 0)))
            def _():
                _scatter_start(_scatter_copies(sc_in, sc_out, scratch[-2], scratch[-1]))

        @pl.when(k == 0)
        def _():
            acc[...] = jnp.zeros_like(acc)

        if cache_a:
            @pl.when(j == 0)
            def _():
                x = a_ref[...].astype(F32)
                rstd = lax.rsqrt(jnp.mean(x * x, axis=-1, keepdims=True) + EPS)
                scratch[1][...] = (x * rstd * refs[n_in - 1][...]).astype(BF16)
                out_refs[1][...] = rstd
            av = scratch[1][...]
        elif a_scale is not None:
            av = (a_ref[...].astype(F32) * refs[n_in - 2][...] * refs[n_in - 1][...]).astype(BF16)
        else:
            av = a_ref[...].astype(BF16)
        acc[...] += _dot(av, b_ref[...].astype(BF16), mode)

        @pl.when(k == nk - 1)
        def _():
            if epilogue is None:
                r = acc[...]
                if res is not None:
                    r = r + refs[2][...].astype(F32)
                out_refs[0][...] = r.astype(out_refs[0].dtype).reshape(out_refs[0].shape)
            else:
                vals, sums = epi_fn(acc[...], [r[...] for r in ex_refs], [p[...] for p in ep_refs])
                for o_ref, v in zip(out_refs, vals):
                    o_ref[...] = v.astype(o_ref.dtype)
                for s_ref, v in zip(out_refs[len(vals):], sums):
                    @pl.when(i == 0)
                    def _(s_ref=s_ref):
                        s_ref[...] = jnp.zeros_like(s_ref)
                    s_ref[...] += v

        if n_sc:
            @pl.when(jnp.logical_and(i == ni - 1, jnp.logical_and(j == nj - 1, k == nk - 1)))
            def _():
                _scatter_wait(_scatter_copies(sc_in, sc_out, scratch[-2], scratch[-1]))

    a_spec = pl.BlockSpec((tk, tm), lambda i, j, k: (k, i)) if mode == "tn" else pl.BlockSpec((tm, tk), lambda i, j, k: (i, k))
    b_spec = pl.BlockSpec((tn, tk), lambda i, j, k: (j, k)) if mode == "nt" else pl.BlockSpec((tk, tn), lambda i, j, k: (k, j))
    o_spec = pl.BlockSpec((tm, tn), lambda i, j, k: (i, j))
    const = lambda p: pl.BlockSpec(p.shape, lambda i, j, k: (0,) * p.ndim)
    ins, specs = [a, b], [a_spec, b_spec]
    if res is not None:
        ins.append(res)
        specs.append(o_spec)
    if a_gain is not None:
        ins.append(a_gain)
        specs.append(const(a_gain))
    if a_scale is not None:
        ins += list(a_scale)
        specs += [pl.BlockSpec((tk, 1), lambda i, j, k: (k, 0)), pl.BlockSpec((1, tm), lambda i, j, k: (0, i))]
    ins += list(extras) + list(eparams)
    specs += [o_spec] * len(extras) + [const(p) for p in eparams]
    if epilogue is not None:
        out_specs = [o_spec] * len(e_outs) + [pl.BlockSpec(s, lambda i, j, k: (0, 0)) for s in e_sums]
        out_shape = [jax.ShapeDtypeStruct((M, N), dt) for dt in e_outs] + [jax.ShapeDtypeStruct(s, F32) for s in e_sums]
    elif shard_out:
        out_specs = pl.BlockSpec((1, 1, tm, tn), lambda i, j, k: (j, i, 0, 0))
        out_shape = jax.ShapeDtypeStruct((4, 2, tm, tn), out_dtype)
    elif cache_a:
        out_specs = [o_spec, pl.BlockSpec((tm, 1), lambda i, j, k: (i, 0))]
        out_shape = [jax.ShapeDtypeStruct((M, N), out_dtype), jax.ShapeDtypeStruct((M, 1), F32)]
    else:
        out_specs, out_shape = o_spec, jax.ShapeDtypeStruct((M, N), out_dtype)
    scratch_shapes = [pltpu.VMEM((tm, tn), F32)] + ([pltpu.VMEM((tm, tk), BF16)] if cache_a else [])
    if n_sc:
        ins += scatter
        specs += [ANY] * n_sc
        out_specs = list(out_specs) + [ANY] * n_sc
        out_shape = list(out_shape) + [jax.ShapeDtypeStruct((8,) + g.shape[2:], g.dtype) for g in scatter]
        scratch_shapes += [pltpu.SemaphoreType.DMA((7 * n_sc,)), pltpu.SemaphoreType.DMA((7 * n_sc,))]
    res_ = pl.pallas_call(
        body, name=name, grid=(ni, nj, nk), in_specs=specs, out_specs=out_specs, out_shape=out_shape,
        scratch_shapes=scratch_shapes, compiler_params=_params(),
    )(*ins)
    if n_sc:
        return list(res_[:n_out]) + [[_own_piece(got, g) for got, g in zip(res_[n_out:], scatter)]]
    return res_


def _piece_spec(tm, piece):
    _, w, c0, per_group = piece
    if per_group:
        return pl.BlockSpec((tm, w), lambda i, g: (i, c0 + g))
    return pl.BlockSpec((tm, w), lambda i, g: (i, c0))


def _const_spec(p):
    return pl.BlockSpec(p.shape, lambda i, g: (0, 0))


def _rowwise(name, fn, params, rows, auxs, outs, tm, groups=1):
    S = rows[0][0].shape[0]
    tm = min(tm, S)
    n_p, n_r, n_a = len(params), len(rows), len(auxs)

    def body(*refs):
        p = [r[...] for r in refs[:n_p]]
        r_ = [r[...] for r in refs[n_p:n_p + n_r]]
        a_ = [r[...] for r in refs[n_p + n_r:n_p + n_r + n_a]]
        for o_ref, o in zip(refs[n_p + n_r + n_a:], fn(p, r_, a_)):
            o_ref[...] = o.astype(o_ref.dtype)

    out_specs, out_shape = [], []
    for w, dt, per_group in outs:
        out_specs.append(_piece_spec(tm, (None, w, 0, per_group)))
        out_shape.append(jax.ShapeDtypeStruct((S, w * (groups if per_group else 1)), dt))
    return pl.pallas_call(
        body, name=name, grid=(S // tm, groups),
        in_specs=[_const_spec(p) for p in params] + [_piece_spec(tm, q) for q in list(rows) + list(auxs)],
        out_specs=out_specs, out_shape=out_shape, compiler_params=_params(),
    )(*params, *[q[0] for q in list(rows) + list(auxs)])


def _rowwise_vjp(name, fn, params, rows, auxs, cots, d_outs, tm, groups=1, adds=None):
    S = rows[0][0].shape[0]
    tm = min(tm, S)
    n_p, n_r, n_a = len(params), len(rows), len(auxs)
    cot_flat = [q for c in cots for q in c]
    adds = adds or [None] * len(d_outs)
    add_flat = [q for q in adds if q is not None]
    n_c, n_add = len(cot_flat), len(add_flat)
    shared = [not all(rows[k][3] for k in idx) and groups > 1 for idx, _ in d_outs]

    def body(*refs):
        pos = 0
        p = [r[...] for r in refs[pos:pos + n_p]]; pos += n_p
        r_ = [r[...] for r in refs[pos:pos + n_r]]; pos += n_r
        a_ = [r[...] for r in refs[pos:pos + n_a]]; pos += n_a
        c_refs = refs[pos:pos + n_c]; pos += n_c
        add_refs = list(refs[pos:pos + n_add]); pos += n_add
        d_refs = refs[pos:pos + len(d_outs)]; pos += len(d_outs)
        dp_refs = refs[pos:]
        i, g = pl.program_id(0), pl.program_id(1)
        outs, vjp_fn = jax.vjp(lambda pp, rr: fn(pp, rr, a_), p, r_)
        cts, ci = [], 0
        for c, o in zip(cots, outs):
            t = c_refs[ci][...].astype(F32)
            for extra in c_refs[ci + 1:ci + len(c)]:
                t = t + extra[...].astype(F32)
            ci += len(c)
            cts.append(t.astype(o.dtype))
        dp, dr = vjp_fn(cts)
        for (idx, _), d_ref, add, sh in zip(d_outs, d_refs, adds, shared):
            val = dr[idx[0]].astype(F32) if len(idx) == 1 else jnp.concatenate([dr[k].astype(F32) for k in idx], axis=1)
            if add is not None:
                val = val + add_refs.pop(0)[...].astype(F32)
            if sh:
                @pl.when(g == 0)
                def _(d_ref=d_ref):
                    d_ref[...] = jnp.zeros_like(d_ref)
                d_ref[...] += val.astype(d_ref.dtype)
            else:
                d_ref[...] = val.astype(d_ref.dtype)
        first = jnp.logical_and(i == 0, g == 0)
        for dp_ref, d in zip(dp_refs, dp):
            @pl.when(first)
            def _(dp_ref=dp_ref):
                dp_ref[...] = jnp.zeros_like(dp_ref)
            dp_ref[...] += d.astype(F32)

    out_specs, out_shape = [], []
    for (idx, dt), sh in zip(d_outs, shared):
        w = sum(rows[k][1] for k in idx)
        per_group = (not sh) and groups > 1
        out_specs.append(_piece_spec(tm, (None, w, 0, per_group)))
        out_shape.append(jax.ShapeDtypeStruct((S, w * (groups if per_group else 1)), dt))
    for p in params:
        out_specs.append(_const_spec(p))
        out_shape.append(jax.ShapeDtypeStruct(p.shape, F32))
    pieces = list(rows) + list(auxs) + cot_flat + add_flat
    res = pl.pallas_call(
        body, name=name, grid=(S // tm, groups),
        in_specs=[_const_spec(p) for p in params] + [_piece_spec(tm, q) for q in pieces],
        out_specs=out_specs, out_shape=out_shape, compiler_params=_params(),
    )(*params, *[q[0] for q in pieces])
    return list(res[:len(d_outs)]), list(res[len(d_outs):])


def _lane_roll(x, shift):
    @jax.custom_vjp
    def roll(v):
        return pltpu.roll(v, shift, 1)

    roll.defvjp(lambda v: (roll(v), None), lambda _, ct: (pltpu.roll(ct, LANES - shift, 1),))
    return roll(x)


@jax.custom_vjp
def _sigmoid(x):
    return 1.0 / (1.0 + jnp.exp(-x))


def _sigmoid_fwd(x):
    s = _sigmoid(x)
    return s, s


_sigmoid.defvjp(_sigmoid_fwd, lambda s, ct: (ct * s * (1.0 - s),))


def _rope(x, cos, sin_lo, sin_hi, half):
    return x * cos + _lane_roll(x, LANES - half) * sin_lo + _lane_roll(x, half) * sin_hi


def _f_rope_table(p, r, a):
    inv, first, second, fixed = p
    ang = a[0] * inv
    cs, sn = jnp.cos(ang), jnp.sin(ang)
    return [cs * (first + second) + fixed, -sn * first, sn * second]


def _f_rms(p, r, a):
    x = r[0].astype(F32)
    return [x * lax.rsqrt(jnp.mean(x * x, axis=-1, keepdims=True) + EPS) * p[0]]


def _f_mla_a(p, r, a):
    return _f_rms([p[0]], [r[0]], a) + _f_rms([p[1]], [r[1]], a)


def _f_mla_b(p, r, a):
    def norm_rope(v, g):
        ms = jnp.sum(v * v, axis=-1, keepdims=True) * (1.0 / MLA_QK)
        return _rope(v * lax.rsqrt(ms + EPS) * g, a[0], a[1], a[2], MLA_ROPE // 2)

    return [norm_rope(r[0].astype(F32), p[0]) * MLA_Q_SCALE, norm_rope(r[1].astype(F32) + r[2].astype(F32), p[1])]


def _f_ret_rope(p, r, a):
    q = _rope(r[0].astype(F32), a[0], a[1], a[2], RET_QK // 2)
    k = _rope(r[1].astype(F32), a[0], a[1], a[2], RET_QK // 2)
    return [q, k * (RET_QK ** -0.5)]


def _f_ret_post(p, r, a):
    ret = r[0].astype(F32) + r[1].astype(F32)
    g = r[2].astype(F32)
    normed = ret * lax.rsqrt(jnp.mean(ret * ret, axis=-1, keepdims=True) + EPS)
    return [g * _sigmoid(g) * normed]


def _f_merge(p, r, a):
    return [_sigmoid(r[0].astype(F32)) * r[2].astype(F32) + _sigmoid(r[1].astype(F32)) * r[3].astype(F32)]


def _f_swiglu(p, r, a):
    g = r[0].astype(F32)
    return [g * _sigmoid(g) * r[1].astype(F32)]


def _f_delta(p, r, a):
    d = jnp.sum(r[0].astype(F32) * r[1].astype(F32), axis=-1, keepdims=True)
    return [jnp.broadcast_to(d, r[0].shape)]


def _f_add(p, r, a):
    return [r[0].astype(F32) + r[1].astype(F32)]


def _loss_kernel(y, tgt):
    S, Dm = y.shape
    tm = min(ROW_TILE, S)

    def body(y_ref, t_ref, dy_ref, loss_ref):
        @pl.when(pl.program_id(0) == 0)
        def _():
            loss_ref[...] = jnp.zeros_like(loss_ref)

        e = y_ref[...] - t_ref[...]
        dy_ref[...] = e * (1.0 / Dm)
        loss_ref[...] += 0.5 * jnp.sum(jnp.mean(e * e, axis=-1, keepdims=True), axis=0, keepdims=True)

    row = pl.BlockSpec((tm, Dm), lambda i: (i, 0))
    return pl.pallas_call(
        body, name="loss", grid=(S // tm,), in_specs=[row, row],
        out_specs=[row, pl.BlockSpec((1, LANES), lambda i: (0, 0))],
        out_shape=[jax.ShapeDtypeStruct((S, Dm), F32), jax.ShapeDtypeStruct((1, LANES), F32)],
        compiler_params=_params(),
    )(y, tgt)


def _flash_fwd(q, k, kv, shards):
    S = q.shape[0]
    tq, tk = min(ATT_TQ, S), min(ATT_TK, S)
    nq, nk = S // tq, S // tk
    n = len(shards)

    def body(q_ref, k_ref, v_ref, *rest):
        shard_refs, (o_ref, lse_ref), gathered = rest[:n], rest[n:n + 2], rest[n + 2:2 * n + 2]
        m_sc, l_sc, acc_sc, send_sems, recv_sems = rest[2 * n + 2:]
        h, qi, ki = pl.program_id(0), pl.program_id(1), pl.program_id(2)

        @pl.when(jnp.logical_and(h == 0, jnp.logical_and(qi == 0, ki == 0)))
        def _():
            _gather_start(_gather_copies(shard_refs, gathered, send_sems, recv_sems))

        @pl.when(ki == 0)
        def _():
            m_sc[...] = jnp.full_like(m_sc, -jnp.inf)
            l_sc[...] = jnp.zeros_like(l_sc)
            acc_sc[...] = jnp.zeros_like(acc_sc)

        for hh in range(hps):
            lanes = slice(hh * LANES, (hh + 1) * LANES)
            s = _dot(q_ref[:, lanes], k_ref[:, lanes], "nt")
            m_prev = m_sc[:, lanes]
            m_new = jnp.maximum(m_prev, jnp.max(s, axis=-1, keepdims=True))
            alpha = jnp.exp2(m_prev - m_new)
            p = jnp.exp2(s - m_new[:, :1])
            l_sc[:, lanes] = alpha * l_sc[:, lanes] + jnp.sum(p, axis=-1, keepdims=True)
            acc_sc[:, lanes] = alpha * acc_sc[:, lanes] + _dot(p.astype(BF16), v_ref[:, lanes])
            m_sc[:, lanes] = m_new

        @pl.when(ki == nk - 1)
        def _():
            o_ref[...] = (acc_sc[...] / l_sc[...]).astype(o_ref.dtype)
            lse_ref[...] = m_sc[...] + jnp.log2(l_sc[...])

        @pl.when(jnp.logical_and(h == HEADS // hps - 1, jnp.logical_and(qi == nq - 1, ki == nk - 1)))
        def _():
            _gather_wait(_gather_copies(shard_refs, gathered, send_sems, recv_sems))

    hps = ATT_HEADS_PER_STEP
    qs = pl.BlockSpec((tq, hps * LANES), lambda h, i, j: (i, h))
    res = pl.pallas_call(
        body, name="mla_fwd", grid=(HEADS // hps, nq, nk),
        in_specs=[qs, pl.BlockSpec((tk, hps * LANES), lambda h, i, j: (j, h)),
                  pl.BlockSpec((tk, hps * LANES), lambda h, i, j: (j, HEADS // hps + h))] + [ANY] * n,
        out_specs=[qs, qs] + [ANY] * n,
        out_shape=[jax.ShapeDtypeStruct((S, HEADS * LANES), BF16), jax.ShapeDtypeStruct((S, HEADS * LANES), F32)]
        + [jax.ShapeDtypeStruct((4,) + s.shape, s.dtype) for s in shards],
        scratch_shapes=[pltpu.VMEM((tq, hps * LANES), F32)] * 3
        + [pltpu.SemaphoreType.DMA((3 * n,)), pltpu.SemaphoreType.DMA((3 * n,))],
        compiler_params=_params(),
    )(q, k, kv, *shards)
    mine = 2 * lax.axis_index("x") + lax.axis_index("y")
    return res[0], res[1], [_fill_slot(g, s, mine) for g, s in zip(res[2:], shards)]


def _flash_bwd(q, k, kv, do, lse, delta, gs):
    S = q.shape[0]
    tq, tk = min(ATT_BQ, S), min(ATT_BK, S)
    nq, nkt = S // tq, S // tk
    n = len(gs)

    def body(q_ref, k_ref, v_ref, do_ref, lse_ref, dl_ref, *rest):
        g_refs, (dq_ref, dk_ref, dv_ref), got_refs = rest[:n], rest[n:n + 3], rest[n + 3:2 * n + 3]
        dk_sc, dv_sc, send_sems, recv_sems = rest[2 * n + 3:]
        h, ki, qi = pl.program_id(0), pl.program_id(1), pl.program_id(2)

        @pl.when(jnp.logical_and(h == 0, jnp.logical_and(ki == 0, qi == 0)))
        def _():
            _scatter_start(_scatter_copies(g_refs, got_refs, send_sems, recv_sems))

        @pl.when(jnp.logical_and(ki == 0, qi == 0))
        def _():
            dq_ref[...] = jnp.zeros_like(dq_ref)

        @pl.when(qi == 0)
        def _():
            dk_sc[...] = jnp.zeros_like(dk_sc)
            dv_sc[...] = jnp.zeros_like(dv_sc)

        rows = pl.ds(pl.multiple_of(qi * tq, tq), tq)
        for hh in range(hps):
            lanes = slice(hh * LANES, (hh + 1) * LANES)
            qv, kv_, dov = q_ref[:, lanes], k_ref[:, lanes], do_ref[:, lanes]
            p = jnp.exp2(_dot(qv, kv_, "nt") - lse_ref[:, lanes][:, :1])
            dp = _dot(dov, v_ref[:, lanes], "nt")
            ds = (p * (dp - dl_ref[:, lanes][:, :1]) * LN2).astype(BF16)
            dv_sc[:, lanes] += _dot(p.astype(BF16), dov, "tn")
            dk_sc[:, lanes] += _dot(ds, qv, "tn")
            dq_ref[rows, lanes] += _dot(ds, kv_)

        @pl.when(qi == nq - 1)
        def _():
            dk_ref[...] = dk_sc[...].astype(dk_ref.dtype)
            dv_ref[...] = dv_sc[...].astype(dv_ref.dtype)

        @pl.when(jnp.logical_and(h == HEADS // hps - 1, jnp.logical_and(ki == nkt - 1, qi == nq - 1)))
        def _():
            _scatter_wait(_scatter_copies(g_refs, got_refs, send_sems, recv_sems))

    hps = ATT_BWD_HEADS_PER_STEP
    qs = pl.BlockSpec((tq, hps * LANES), lambda h, j, i: (i, h))
    ks = pl.BlockSpec((tk, hps * LANES), lambda h, j, i: (j, h))
    res = pl.pallas_call(
        body, name="mla_bwd", grid=(HEADS // hps, nkt, nq),
        in_specs=[qs, ks, pl.BlockSpec((tk, hps * LANES), lambda h, j, i: (j, HEADS // hps + h)), qs, qs, qs] + [ANY] * n,
        out_specs=[pl.BlockSpec((S, hps * LANES), lambda h, j, i: (0, h), pipeline_mode=pl.Buffered(1)), ks, ks] + [ANY] * n,
        out_shape=[jax.ShapeDtypeStruct((S, HEADS * LANES), F32), jax.ShapeDtypeStruct((S, HEADS * LANES), BF16),
                   jax.ShapeDtypeStruct((S, HEADS * LANES), BF16)]
        + [jax.ShapeDtypeStruct((8,) + g.shape[2:], g.dtype) for g in gs],
        scratch_shapes=[pltpu.VMEM((tk, hps * LANES), F32)] * 2
        + [pltpu.SemaphoreType.DMA((7 * n,)), pltpu.SemaphoreType.DMA((7 * n,))],
        compiler_params=_params(),
    )(q, k, kv, do, lse, delta, *gs)
    return res[0], res[1], res[2], [_own_piece(got, g) for got, g in zip(res[3:], gs)]


def _ret_tables(decay_row, backward):
    C = RET_CHUNK
    lg = -jnp.exp(decay_row)
    t = lax.broadcasted_iota(jnp.int32, (C, C), 0).astype(F32)
    s = lax.broadcasted_iota(jnp.int32, (C, C), 1).astype(F32)
    ridx = lax.broadcasted_iota(jnp.int32, (C, LANES), 0).astype(F32)
    if backward:
        dist, mask, aw, bw = s - t, s > t, C - ridx, ridx
    else:
        dist, mask, aw, bw = t - s, t >= s, ridx + 1.0, C - 1.0 - ridx
    dist = jnp.maximum(dist, 0.0)
    din = jnp.where(mask, jnp.exp(lg[:, :1] * dist), 0.0)
    return dict(din=din, dist=dist, a=jnp.exp(lg * aw), b=jnp.exp(lg * bw), c=jnp.exp(lg * C), aw=aw, bw=bw)


def _ret_fwd(qr, kr, proj, v_block, dec_f, dec_b):
    S = qr.shape[0]
    C = RET_CHUNK
    n = S // C
    W = HEADS * LANES

    def body(qf, kf, vf, qb, kb, vb, df, db, of, ob, sf_out, sb_out, st):
        @pl.when(pl.program_id(0) == 0)
        def _():
            st[...] = jnp.zeros_like(st)

        for d, (q_ref, k_ref, v_ref, dec, o_ref, s_out) in enumerate(
                [(qf, kf, vf, df, of, sf_out), (qb, kb, vb, db, ob, sb_out)]):
            for h in range(HEADS):
                lanes = slice(h * LANES, (h + 1) * LANES)
                tb = _ret_tables(dec[h:h + 1, :], d == 1)
                qf32, kf32, v = q_ref[:, lanes].astype(F32), k_ref[:, lanes].astype(F32), v_ref[:, lanes]
                state = st[d, h]
                s_out[0, h] = state
                inner = _dot((_dot(qf32.astype(BF16), kf32.astype(BF16), "nt") * tb["din"]).astype(BF16), v)
                cross = _dot((qf32 * tb["a"]).astype(BF16), state.astype(BF16))
                o_ref[:, lanes] = inner + cross
                st[d, h] = state * tb["c"] + _dot((kf32 * tb["b"]).astype(BF16), v, "tn")

    fw = lambda c0: pl.BlockSpec((C, W), lambda j: (j, c0))
    bw = lambda c0: pl.BlockSpec((C, W), lambda j: (n - 1 - j, c0))
    dec_spec = pl.BlockSpec((HEADS, LANES), lambda j: (0, 0))
    st_shape = jax.ShapeDtypeStruct((n, HEADS, LANES, LANES), F32)
    return pl.pallas_call(
        body, name="ret_fwd", grid=(n,),
        in_specs=[fw(0), fw(0), fw(v_block), bw(0), bw(0), bw(v_block), dec_spec, dec_spec],
        out_specs=[fw(0), bw(0), pl.BlockSpec((1, HEADS, LANES, LANES), lambda j: (j, 0, 0, 0)),
                   pl.BlockSpec((1, HEADS, LANES, LANES), lambda j: (n - 1 - j, 0, 0, 0))],
        out_shape=[jax.ShapeDtypeStruct((S, W), F32)] * 2 + [st_shape] * 2,
        scratch_shapes=[pltpu.VMEM((2, HEADS, LANES, LANES), F32)], compiler_params=_params(),
    )(qr, kr, proj, qr, kr, proj, dec_f, dec_b)


def _ret_bwd(qr, kr, proj, v_block, dret, sf, sb, dec_f, dec_b):
    S = qr.shape[0]
    C = RET_CHUNK
    n = S // C
    W = HEADS * LANES

    def body(qf, kf, vf, gf, sf_ref, qb, kb, vb, gb, sb_ref, df, db,
             dqf, dkf, dvf, dqb, dkb, dvb, ddf, ddb, ds_sc):
        j = pl.program_id(0)

        @pl.when(j == 0)
        def _():
            ds_sc[...] = jnp.zeros_like(ds_sc)
            ddf[...] = jnp.zeros_like(ddf)
            ddb[...] = jnp.zeros_like(ddb)

        for d, (q_ref, k_ref, v_ref, g_ref, s_ref, dec, dq_ref, dk_ref, dv_ref, dd_ref) in enumerate(
                [(qf, kf, vf, gf, sf_ref, df, dqf, dkf, dvf, ddf), (qb, kb, vb, gb, sb_ref, db, dqb, dkb, dvb, ddb)]):
            for h in range(HEADS):
                lanes = slice(h * LANES, (h + 1) * LANES)
                tb = _ret_tables(dec[h:h + 1, :], d == 1)
                v, g = v_ref[:, lanes], g_ref[:, lanes]
                qf32, kf32 = q_ref[:, lanes].astype(F32), k_ref[:, lanes].astype(F32)
                q, k = qf32.astype(BF16), kf32.astype(BF16)
                state, dstate = s_ref[0, h], ds_sc[d, h]
                dstate_b = dstate.astype(BF16)
                dp = _dot(g, v, "nt")
                a_ = _dot(q, k, "nt")
                da = (dp * tb["din"]).astype(BF16)
                g1 = _dot(g, state.astype(BF16), "nt")
                g2 = _dot(v, dstate_b, "nt")
                dq_ref[:, lanes] = (_dot(da, k) + g1 * tb["a"]).astype(dq_ref.dtype)
                dk_ref[:, lanes] = (_dot(da, q, "tn") + g2 * tb["b"]).astype(dk_ref.dtype)
                dv_ref[:, lanes] = (_dot((a_ * tb["din"]).astype(BF16), g, "tn")
                                    + _dot((kf32 * tb["b"]).astype(BF16), dstate_b)).astype(dv_ref.dtype)
                dlg = (jnp.sum(dp * a_ * tb["din"] * tb["dist"], keepdims=True)
                       + jnp.sum(g1 * qf32 * tb["a"] * tb["aw"], keepdims=True)
                       + jnp.sum(g2 * kf32 * tb["b"] * tb["bw"], keepdims=True)
                       + C * jnp.sum(tb["c"] * dstate * state, keepdims=True))
                dd_ref[h:h + 1, :] += jnp.broadcast_to(dlg, (1, LANES))
                ds_sc[d, h] = dstate * tb["c"] + _dot((qf32 * tb["a"]).astype(BF16), g, "tn")

        @pl.when(j == n - 1)
        def _():
            ddf[...] = ddf[...] * -jnp.exp(df[...])
            ddb[...] = ddb[...] * -jnp.exp(db[...])

    fw = lambda c0: pl.BlockSpec((C, W), lambda j: (n - 1 - j, c0))
    bw = lambda c0: pl.BlockSpec((C, W), lambda j: (j, c0))
    dec_spec = pl.BlockSpec((HEADS, LANES), lambda j: (0, 0))
    act = jax.ShapeDtypeStruct((S, W), BF16)
    return pl.pallas_call(
        body, name="ret_bwd", grid=(n,),
        in_specs=[fw(0), fw(0), fw(v_block), fw(0), pl.BlockSpec((1, HEADS, LANES, LANES), lambda j: (n - 1 - j, 0, 0, 0)),
                  bw(0), bw(0), bw(v_block), bw(0), pl.BlockSpec((1, HEADS, LANES, LANES), lambda j: (j, 0, 0, 0)),
                  dec_spec, dec_spec],
        out_specs=[fw(0)] * 3 + [bw(0)] * 3 + [dec_spec] * 2,
        out_shape=[act] * 6 + [jax.ShapeDtypeStruct((HEADS, LANES), F32)] * 2,
        scratch_shapes=[pltpu.VMEM((2, HEADS, LANES, LANES), F32)], compiler_params=_params(),
    )(qr, kr, proj, dret, sf, qr, kr, proj, dret, sb, dec_f, dec_b)


def _pad_heads(w, hd):
    K = w.shape[0]
    return jnp.pad(w.reshape(K, HEADS, hd), ((0, 0), (0, 0), (0, LANES - hd))).reshape(K, HEADS * LANES)


def _unpad_heads(w, hd):
    K = w.shape[0]
    return w.reshape(K, HEADS, LANES)[:, :, :hd].reshape(K, HEADS * hd)


def _rope_consts(first_lane, half):
    lane = np.arange(LANES)
    first = ((lane >= first_lane) & (lane < first_lane + half)).astype(np.float32)
    second = ((lane >= first_lane + half) & (lane < first_lane + 2 * half)).astype(np.float32)
    fixed = (lane < first_lane).astype(np.float32)
    j = np.where(first > 0, lane - first_lane, lane - first_lane - half) * (first + second)
    inv = (ROPE_THETA ** (-j.astype(np.float64) / half)).astype(np.float32)
    return [jnp.asarray(v.reshape(1, LANES), F32) for v in (inv, first, second, fixed)]


def _assemble(name, gathered):
    if name in COL_SHARDED:
        return jnp.transpose(gathered, (1, 0, 2)).reshape(gathered.shape[1], 4 * gathered.shape[2])
    return gathered.reshape(4 * gathered.shape[1], gathered.shape[2])


def _split_for_reducers(name, g, dtype):
    if name in COL_SHARDED:
        K, N4 = g.shape
        return jnp.transpose(g.reshape(2, K // 2, 4, N4 // 4), (2, 0, 1, 3)).astype(dtype)
    return g.reshape(4, 2, g.shape[0] // 8, g.shape[1]).astype(dtype)


def _local_step(x, pos, tgt, wts, late_shards, small):
    w_in = wts["w_in"]
    seg = [w_in[:, IN_OFFS[i]:IN_OFFS[i + 1]] for i in range(8)]
    kr_w = jnp.pad(seg[2], ((0, 0), (MLA_NOPE, LANES - MLA_QK)))
    w_in_p = jnp.concatenate([seg[7], seg[5], seg[6], _pad_heads(seg[3], RET_QK), _pad_heads(seg[4], RET_QK),
                              seg[0], seg[1], kr_w], axis=1)
    w_qb_p = _pad_heads(wts["w_q_b"], MLA_QK)
    kvw = wts["w_kv_b"].reshape(MLA_KV_RANK, HEADS, MLA_NOPE + MLA_V)
    pad_kv = lambda t: jnp.pad(t, ((0, 0), (0, 0), (0, LANES - t.shape[2]))).reshape(MLA_KV_RANK, HEADS * LANES)
    w_kn_p, w_v_p = pad_kv(kvw[:, :, :MLA_NOPE]), pad_kv(kvw[:, :, MLA_NOPE:])
    w_kv_p = jnp.concatenate([w_kn_p, w_v_p], axis=1)
    g_qn_p = jnp.pad(small["g_qn"], ((0, 0), (0, LANES - MLA_QK)))
    g_kn_p = jnp.pad(small["g_kn"], ((0, 0), (0, LANES - MLA_QK)))
    dec_f = jnp.broadcast_to(small["ret_decay_fwd"].reshape(HEADS, 1), (HEADS, LANES))
    dec_b = jnp.broadcast_to(small["ret_decay_bwd"].reshape(HEADS, 1), (HEADS, LANES))
    T, N = True, False
    RT, HT = ROW_TILE, HEAD_ROW_TILE

    tab_m = _rowwise("rope_table_mla", _f_rope_table, _rope_consts(MLA_NOPE, MLA_ROPE // 2), [(pos, 1, 0, N)], [(pos, 1, 0, N)],
                     [(LANES, F32, N)] * 3, HT)
    tab_r = _rowwise("rope_table_ret", _f_rope_table, _rope_consts(0, RET_QK // 2), [(pos, 1, 0, N)], [(pos, 1, 0, N)],
                     [(LANES, F32, N)] * 3, HT)
    aux_m = [(t, LANES, 0, N) for t in tab_m]
    aux_r = [(t, LANES, 0, N) for t in tab_r]

    proj, rstd1 = _mm("proj", x, w_in_p, "nn", BF16, a_gain=small["g_mix"])
    rows_a = [(proj, MLA_Q_RANK, 24, N), (proj, MLA_KV_RANK, 50, N)]
    cqn, ckvn = _rowwise("mla_lat_norm", _f_mla_a, [small["g_q_a"], small["g_kv_a"]], rows_a, [],
                         [(MLA_Q_RANK, BF16, N), (MLA_KV_RANK, BF16, N)], RT)
    qraw = _mm("mla_q_up", cqn, w_qb_p, "nn", BF16)
    kv = _mm("mla_kv_up", ckvn, w_kv_p, "nn", BF16)
    rows_b = [(qraw, LANES, 0, T), (kv, LANES, 0, T), (proj, LANES, 51, N)]
    q, k = _rowwise("mla_qk_norm_rope", _f_mla_b, [g_qn_p, g_kn_p], rows_b, aux_m, [(LANES, BF16, T)] * 2, HT, HEADS)
    o, lse, late = _flash_fwd(q, k, kv, [late_shards[n] for n in LATE])
    wl = {n: _assemble(n, g) for n, g in zip(LATE, late)}
    w_mla_p = jnp.pad(wl["w_mla_out"].reshape(HEADS, MLA_V, D_MODEL), ((0, 0), (0, LANES - MLA_V), (0, 0))).reshape(HEADS * LANES, D_MODEL)
    w_ret_out, w_out, w_gu, w_down = wl["w_ret_out"], wl["w_out"], wl["w_gate_up"], wl["w_down"]
    y_a = _mm("mla_out", o, w_mla_p, "nn", F32)
    rows_rr = [(proj, LANES, 32, T), (proj, LANES, 40, T)]
    qr, kr = _rowwise("ret_rope", _f_ret_rope, [], rows_rr, aux_r, [(LANES, RET_QK_DTYPE, T)] * 2, HT, HEADS)
    ret_f, ret_b, st_f, st_b = _ret_fwd(qr, kr, proj, 2, dec_f, dec_b)
    rows_rp = [(ret_f, LANES, 0, T), (ret_b, LANES, 0, T), (proj, LANES, 24, T)]
    (o_b,) = _rowwise("ret_post", _f_ret_post, [], rows_rp, [], [(LANES, BF16, T)], HT, HEADS)
    y_b = _mm("ret_out", o_b, w_ret_out, "nn", F32)
    rows_m = [(proj, D_MODEL, 0, N), (proj, D_MODEL, 1, N), (y_a, D_MODEL, 0, N), (y_b, D_MODEL, 0, N)]
    (merged,) = _rowwise("merge", _f_merge, [], rows_m, [], [(D_MODEL, BF16, N)], RT)
    x2 = _mm("mix_out", merged, w_out, "nn", F32, res=x)
    gu, rstd2 = _mm("ffn_gate_up", x2, w_gu, "nn", BF16, a_gain=small["g_ffn"])
    rows_sw = [(gu, FFN_HIDDEN, 0, N), (gu, FFN_HIDDEN, 1, N)]
    (act,) = _rowwise("swiglu", _f_swiglu, [], rows_sw, [], [(FFN_HIDDEN, BF16, N)], RT)
    dy, dy_b16, loss_row = _mm("ffn_down", act, w_down, "nn", None,
                               epilogue=(_epi_loss, [x2, tgt], [], [F32, BF16], [(1, LANES)]))

    dact = _mm("d_act", dy_b16, w_down, "nt", BF16)
    dw_down = _mm("dw_down", act, dy_b16, "tn", BF16)
    (dgu,), _ = _rowwise_vjp("swiglu_bwd", _f_swiglu, [], rows_sw, [], [[(dact, FFN_HIDDEN, 0, N)]], [([0, 1], BF16)], RT)
    dx2, dx2_b16, dg_ffn = _mm("d_h2", dgu, w_gu, "nt", None,
                               epilogue=(_epi_rms_bwd(2), [x2, dy], [small["g_ffn"]], [F32, BF16], [(1, D_MODEL)]))
    dw_gu = _mm("dw_gate_up", x2, dgu, "tn", BF16, a_scale=(rstd2, small["g_ffn"]), shard_out=True)
    dmerged = _mm("d_merged", dx2_b16, w_out, "nt", BF16)
    dw_out = _mm("dw_out", merged, dx2_b16, "tn", BF16)
    (dgl, dy_a, dy_b), _ = _rowwise_vjp("merge_bwd", _f_merge, [], rows_m, [], [[(dmerged, D_MODEL, 0, N)]],
                                        [([0, 1], BF16), ([2], BF16), ([3], BF16)], RT)
    do_b = _mm("d_ret_o", dy_b, w_ret_out, "nt", BF16)
    dw_ret_out = _mm("dw_ret_out", o_b, dy_b, "tn", BF16)
    (dret, dg_r), _ = _rowwise_vjp("ret_post_bwd", _f_ret_post, [], rows_rp, [], [[(do_b, LANES, 0, T)]],
                                   [([0], BF16), ([2], BF16)], HT, HEADS)
    dqf, dkf, dvf, dqb, dkb, dvb, ddec_f, ddec_b = _ret_bwd(qr, kr, proj, 2, dret, st_f, st_b, dec_f, dec_b)
    (dq_r, dk_r), _ = _rowwise_vjp("ret_rope_bwd", _f_ret_rope, [], rows_rr, aux_r,
                                   [[(dqf, LANES, 0, T), (dqb, LANES, 0, T)], [(dkf, LANES, 0, T), (dkb, LANES, 0, T)]],
                                   [([0], BF16), ([1], BF16)], HT, HEADS)
    (dv_r,) = _rowwise("ret_dv_sum", _f_add, [], [(dvf, D_MODEL, 0, N), (dvb, D_MODEL, 0, N)], [], [(D_MODEL, BF16, N)], RT)
    do = _mm("d_mla_o", dy_a, w_mla_p, "nt", BF16)
    dw_mla_p = _mm("dw_mla_out", o, dy_a, "tn", BF16)
    (delta,) = _rowwise("mla_delta", _f_delta, [], [(do, LANES, 0, T), (o, LANES, 0, T)], [], [(LANES, F32, T)], HT, HEADS)
    dw_mla = dw_mla_p.reshape(HEADS, LANES, D_MODEL)[:, :MLA_V].reshape(HEADS * MLA_V, D_MODEL)
    late_grads = {"w_mla_out": dw_mla, "w_ret_out": dw_ret_out, "w_out": dw_out, "w_down": dw_down}
    late_gs = [dw_gu if n == "w_gate_up" else _split_for_reducers(n, late_grads[n], BF16) for n in LATE]
    dq, dk, dv, late_got = _flash_bwd(q, k, kv, do, lse, delta, late_gs)
    (dqraw, dkn, dkr), (dg_qn_p, dg_kn_p) = _rowwise_vjp(
        "mla_qk_norm_rope_bwd", _f_mla_b, [g_qn_p, g_kn_p], rows_b, aux_m, [[(dq, LANES, 0, T)], [(dk, LANES, 0, T)]],
        [([0], BF16), ([1], BF16), ([2], F32)], HT, HEADS)
    dckvn = _mm("d_ckvn_v", dv, w_v_p, "nt", BF16, res=_mm("d_ckvn_k", dkn, w_kn_p, "nt", F32))
    dw_kn_p = _mm("dw_kv_k", ckvn, dkn, "tn", BF16)
    dw_v_p = _mm("dw_kv_v", ckvn, dv, "tn", BF16)
    dcqn = _mm("d_cqn", dqraw, w_qb_p, "nt", BF16)
    dw_qb_p = _mm("dw_q_b", cqn, dqraw, "tn", BF16)
    (dcq, dckv), (dg_q_a, dg_kv_a) = _rowwise_vjp(
        "mla_lat_norm_bwd", _f_mla_a, [small["g_q_a"], small["g_kv_a"]], rows_a, [],
        [[(dcqn, MLA_Q_RANK, 0, N)], [(dckvn, MLA_KV_RANK, 0, N)]], [([0], BF16), ([1], BF16)], RT)
    dproj = jnp.concatenate([dgl, dv_r, dg_r, dq_r, dk_r, dcq, dckv, dkr.astype(BF16)], axis=1)
    dw_in_p = _mm("dw_in", x, dproj, "tn", BF16, a_scale=(rstd1, small["g_mix"]))

    c = lambda a, b_: dw_in_p[:, a:b_]
    dw_in = jnp.concatenate([c(6144, 6400), c(6400, 6528), c(6528 + MLA_NOPE, 6528 + MLA_QK), _unpad_heads(c(4096, 5120), RET_QK),
                             _unpad_heads(c(5120, 6144), RET_QK), c(2048, 3072), c(3072, 4096), c(0, 2048)], axis=1)
    un_kv = lambda t: t.reshape(MLA_KV_RANK, HEADS, LANES)[:, :, :MLA_NOPE]
    dw_kv = jnp.concatenate([un_kv(dw_kn_p), un_kv(dw_v_p)], axis=2).reshape(MLA_KV_RANK, HEADS * (MLA_NOPE + MLA_V))
    grads = {"w_in": dw_in, "w_q_b": _unpad_heads(dw_qb_p, MLA_QK), "w_kv_b": dw_kv}
    dx, dg_mix, first_got = _mm("d_h", dproj, w_in_p, "nt", None,
                                epilogue=(_epi_rms_bwd(1), [x, dx2], [small["g_mix"]], [F32], [(1, D_MODEL)]),
                                scatter=[_split_for_reducers(n, grads[n], BF16) for n in FIRST])
    sgrads = {"g_mix": dg_mix, "g_q_a": dg_q_a, "g_kv_a": dg_kv_a, "g_qn": dg_qn_p[:, :MLA_QK], "g_kn": dg_kn_p[:, :MLA_QK],
              "ret_decay_fwd": ddec_f[:, 0].reshape(1, HEADS), "ret_decay_bwd": ddec_b[:, 0].reshape(1, HEADS), "g_ffn": dg_ffn}
    return loss_row, dx, first_got + late_got, sgrads


def _coords():
    return lax.axis_index("x"), lax.axis_index("y"), lax.axis_index("c")


def _other_chips(x, y):
    return [(1 - x, y), (x, 1 - y), (1 - x, 1 - y)]


ANY = pl.BlockSpec(memory_space=pl.ANY)


def _gather_copies(ins, outs, send_sems, recv_sems):
    x, y, c = _coords()
    mine = 2 * x + y
    sends, arrivals = [], []
    for w in range(len(ins)):
        for j, (cx, cy) in enumerate(_other_chips(x, y)):
            sems = dict(send_sem=send_sems.at[3 * w + j], recv_sem=recv_sems.at[3 * w + j],
                        device_id=(cx, cy, c), device_id_type=MESH)
            sends.append(pltpu.make_async_remote_copy(src_ref=ins[w], dst_ref=outs[w].at[mine], **sems))
            arrivals.append(functools.partial(pltpu.make_async_remote_copy, src_ref=ins[w],
                                              dst_ref=outs[w].at[2 * cx + cy], **sems))
    return sends, arrivals


def _gather_start(copies):
    for cp in copies[0]:
        cp.start()


def _gather_wait(copies):
    sends, arrivals = copies
    for make in arrivals:
        make().wait_recv()
    for cp in sends:
        cp.wait_send()


def _fill_slot(buf, piece, slot):
    return lax.dynamic_update_slice(buf, piece[None], (slot,) + (0,) * piece.ndim)


def _weight_gather_first(shards):
    n = len(shards)

    def body(*refs):
        ins, outs = refs[:n], refs[n:2 * n]
        send_sems, recv_sems = refs[2 * n:]
        x, y, c = _coords()
        chips = _other_chips(x, y)
        mine = 2 * x + y

        def half(ref, slot, core):
            rows = ref.shape[1] // 2
            return ref.at[slot, pl.ds(pl.multiple_of(core * rows, 8), rows)]

        def copy(w, k, slot, core, to, src=None):
            return pltpu.make_async_remote_copy(
                src_ref=half(outs[w], slot, core) if src is None else src, dst_ref=half(outs[w], slot, core),
                send_sem=send_sems.at[6 * w + k], recv_sem=recv_sems.at[6 * w + k], device_id=to, device_id_type=MESH)

        first, passed = [], []
        for w in range(n):
            rows = ins[w].shape[0] // 2
            my_half = ins[w].at[pl.ds(pl.multiple_of(c * rows, 8), rows)]
            for j, (cx, cy) in enumerate(chips):
                cp = copy(w, j, mine, c, (cx, cy, c), src=my_half)
                cp.start()
                first.append(cp)
        for w in range(n):
            for j, (cx, cy) in enumerate(chips):
                copy(w, j, 2 * cx + cy, c, (x, y, c)).wait_recv()
                cp = copy(w, 3 + j, 2 * cx + cy, c, (x, y, 1 - c))
                cp.start()
                passed.append(cp)
        for w in range(n):
            for j, (cx, cy) in enumerate(chips):
                copy(w, 3 + j, 2 * cx + cy, 1 - c, (x, y, c)).wait_recv()
        for cp in first + passed:
            cp.wait_send()

    return pl.pallas_call(
        body, name="weight_gather_first", in_specs=[ANY] * n, out_specs=[ANY] * n,
        out_shape=[jax.ShapeDtypeStruct((4,) + s.shape, s.dtype) for s in shards],
        scratch_shapes=[pltpu.SemaphoreType.DMA((6 * n,)), pltpu.SemaphoreType.DMA((6 * n,))],
    )(*shards)


def _scatter_copies(ins, outs, send_sems, recv_sems):
    x, y, c = _coords()
    me = 4 * x + 2 * y + c
    sends, arrivals = [], []
    for w in range(len(ins)):
        for k in range(1, 8):
            px, py, pc = x ^ (k >> 2), y ^ ((k >> 1) & 1), c ^ (k & 1)
            sems = dict(send_sem=send_sems.at[7 * w + k - 1], recv_sem=recv_sems.at[7 * w + k - 1],
                        device_id=(px, py, pc), device_id_type=MESH)
            sends.append(pltpu.make_async_remote_copy(src_ref=ins[w].at[2 * px + py, pc], dst_ref=outs[w].at[me], **sems))
            arrivals.append(functools.partial(
                pltpu.make_async_remote_copy, src_ref=ins[w].at[2 * px + py, pc],
                dst_ref=outs[w].at[4 * px + 2 * py + pc], **sems))
    return sends, arrivals


_scatter_start, _scatter_wait = _gather_start, _gather_wait


def _own_piece(got, g):
    x, y, c = _coords()
    mine = lax.dynamic_slice(g, (2 * x + y, c, 0, 0), (1, 1) + g.shape[2:]).reshape(g.shape[2:])
    return _fill_slot(got, mine, 4 * x + 2 * y + c)


def _grad_scatter_late(gs):
    n = len(gs)

    def body(*refs):
        copies = _scatter_copies(refs[:n], refs[n:2 * n], *refs[2 * n:])
        _scatter_start(copies)
        _scatter_wait(copies)

    return pl.pallas_call(
        body, name="grad_scatter_late", in_specs=[ANY] * n, out_specs=[ANY] * n,
        out_shape=[jax.ShapeDtypeStruct((8,) + g.shape[2:], g.dtype) for g in gs],
        scratch_shapes=[pltpu.SemaphoreType.DMA((7 * n,)), pltpu.SemaphoreType.DMA((7 * n,))],
    )(*gs)


def _grad_sum8(name, got):
    _, R, W = got.shape
    tr = _pick(R, 256, 16)

    def body(g_ref, o_ref):
        total = g_ref[0].astype(F32)
        for d in range(1, 8):
            total = total + g_ref[d].astype(F32)
        o_ref[...] = total

    return pl.pallas_call(
        body, name=name, grid=(R // tr,), in_specs=[pl.BlockSpec((8, tr, W), lambda i: (0, i, 0))],
        out_specs=pl.BlockSpec((tr, W), lambda i: (i, 0)), out_shape=jax.ShapeDtypeStruct((R, W), F32),
        compiler_params=_params(),
    )(got)


def _sibling_exchange(gs):
    n = len(gs)

    def body(*refs):
        ins, outs, send_sems, recv_sems = refs[:n], refs[n:2 * n], refs[2 * n], refs[2 * n + 1]
        x, y, c = _coords()
        cps = []
        for w in range(n):
            cp = pltpu.make_async_remote_copy(
                src_ref=ins[w].at[:, 1 - c], dst_ref=outs[w], send_sem=send_sems.at[w], recv_sem=recv_sems.at[w],
                device_id=(x, y, 1 - c), device_id_type=MESH)
            cp.start()
            cps.append(cp)
        for cp in cps:
            cp.wait()

    return pl.pallas_call(
        body, name="grad_sibling_exchange", in_specs=[ANY] * n, out_specs=[ANY] * n,
        out_shape=[jax.ShapeDtypeStruct((4,) + g.shape[2:], F32) for g in gs],
        scratch_shapes=[pltpu.SemaphoreType.DMA((n,)), pltpu.SemaphoreType.DMA((n,))],
    )(*gs)


def _pair_sum(name, g, got, c_arr):
    _, _, R, W = g.shape
    tr = _pick(R, 256, 8)

    def body(c_ref, a_ref, b_ref, o_ref):
        o_ref[...] = a_ref[0] + b_ref[...]

    return pl.pallas_call(
        body, name=name,
        grid_spec=pltpu.PrefetchScalarGridSpec(
            num_scalar_prefetch=1, grid=(4, R // tr),
            in_specs=[pl.BlockSpec((1, 1, tr, W), lambda j, i, c_ref: (j, c_ref[0], i, 0)),
                      pl.BlockSpec((1, tr, W), lambda j, i, c_ref: (j, i, 0))],
            out_specs=pl.BlockSpec((1, tr, W), lambda j, i, c_ref: (j, i, 0))),
        out_shape=jax.ShapeDtypeStruct((4, R, W), F32), compiler_params=_params(),
    )(c_arr, g, got)


def _chip_exchange(parts):
    n = len(parts)

    def body(*refs):
        ins, outs, send_sems, recv_sems = refs[:n], refs[n:2 * n], refs[2 * n], refs[2 * n + 1]
        x, y, c = _coords()
        sends = []
        for w in range(n):
            for j, (cx, cy) in enumerate(_other_chips(x, y)):
                cp = pltpu.make_async_remote_copy(
                    src_ref=ins[w].at[2 * cx + cy], dst_ref=outs[w].at[j], send_sem=send_sems.at[3 * w + j],
                    recv_sem=recv_sems.at[3 * w + j], device_id=(cx, cy, c), device_id_type=MESH)
                cp.start()
                sends.append(cp)
        for cp in sends:
            cp.wait_recv()
        for cp in sends:
            cp.wait_send()

    return pl.pallas_call(
        body, name="grad_chip_exchange", in_specs=[ANY] * n, out_specs=[ANY] * n,
        out_shape=[jax.ShapeDtypeStruct((3,) + p.shape[1:], F32) for p in parts],
        scratch_shapes=[pltpu.SemaphoreType.DMA((3 * n,)), pltpu.SemaphoreType.DMA((3 * n,))],
    )(*parts)


def _chip_sum(name, part, got, slot_arr):
    _, R, W = part.shape
    tr = _pick(R, 256, 8)

    def body(s_ref, a_ref, b_ref, o_ref):
        o_ref[...] = ((a_ref[0] + b_ref[0]) + b_ref[1]) + b_ref[2]

    return pl.pallas_call(
        body, name=name,
        grid_spec=pltpu.PrefetchScalarGridSpec(
            num_scalar_prefetch=1, grid=(R // tr,),
            in_specs=[pl.BlockSpec((1, tr, W), lambda i, s_ref: (s_ref[0], i, 0)),
                      pl.BlockSpec((3, tr, W), lambda i, s_ref: (0, i, 0))],
            out_specs=pl.BlockSpec((tr, W), lambda i, s_ref: (i, 0))),
        out_shape=jax.ShapeDtypeStruct((R, W), F32), compiler_params=_params(),
    )(slot_arr, part, got)


def _half_exchange(halves):
    n = len(halves)

    def body(*refs):
        ins, outs, send_sems, recv_sems = refs[:n], refs[n:2 * n], refs[2 * n], refs[2 * n + 1]
        x, y, c = _coords()
        sends = []
        for w in range(n):
            cp = pltpu.make_async_remote_copy(
                src_ref=ins[w], dst_ref=outs[w], send_sem=send_sems.at[w], recv_sem=recv_sems.at[w],
                device_id=(x, y, 1 - c), device_id_type=MESH)
            cp.start()
            sends.append(cp)
        for cp in sends:
            cp.wait()

    got = pl.pallas_call(
        body, name="grad_half_exchange", in_specs=[ANY] * n, out_specs=[ANY] * n,
        out_shape=[jax.ShapeDtypeStruct(h.shape, F32) for h in halves],
        scratch_shapes=[pltpu.SemaphoreType.DMA((n,)), pltpu.SemaphoreType.DMA((n,))],
    )(*halves)
    c = lax.axis_index("c")
    return [jnp.where(c == 0, jnp.stack([mine, theirs]), jnp.stack([theirs, mine])) for mine, theirs in zip(halves, got)]


def _adamw_math(w, g, m, v):
    m2 = ADAM_B1 * m + (1.0 - ADAM_B1) * g
    v2 = ADAM_B2 * v + (1.0 - ADAM_B2) * (g * g)
    m_hat = m2 / (1.0 - ADAM_B1 ** ADAM_STEP)
    v_hat = v2 / (1.0 - ADAM_B2 ** ADAM_STEP)
    return -ADAM_LR * (m_hat / (jnp.sqrt(v_hat) + ADAM_EPS) + ADAM_WD * w), m2, v2


def _small_allreduce_adamw(pack_g, pack_w, pack_m, pack_v):
    def body(g_ref, w_ref, m_ref, v_ref, sum_ref, d_ref, m_out, v_out, land, send_sems, recv_sems):
        x, y, c = _coords()
        me = 4 * x + 2 * y + c
        land[me] = g_ref[...]
        sends = []
        for k in range(1, 8):
            peer = (x ^ (k >> 2), y ^ ((k >> 1) & 1), c ^ (k & 1))
            cp = pltpu.make_async_remote_copy(
                src_ref=g_ref, dst_ref=land.at[me], send_sem=send_sems.at[k - 1], recv_sem=recv_sems.at[k - 1],
                device_id=peer, device_id_type=MESH)
            cp.start()
            sends.append((cp, peer))
        for k, (cp, peer) in enumerate(sends):
            pltpu.make_async_remote_copy(
                src_ref=g_ref, dst_ref=land.at[4 * peer[0] + 2 * peer[1] + peer[2]], send_sem=send_sems.at[k],
                recv_sem=recv_sems.at[k], device_id=peer, device_id_type=MESH).wait_recv()
        for cp, _ in sends:
            cp.wait_send()
        total = land[0]
        for d in range(1, 8):
            total = total + land[d]
        sum_ref[...] = total
        d_ref[...], m_out[...], v_out[...] = _adamw_math(w_ref[...], total, m_ref[...], v_ref[...])

    vm = pl.BlockSpec(memory_space=pltpu.VMEM)
    shp = jax.ShapeDtypeStruct(pack_g.shape, F32)
    return pl.pallas_call(
        body, name="small_allreduce_adamw", in_specs=[vm] * 4, out_specs=[vm] * 4, out_shape=[shp] * 4,
        scratch_shapes=[pltpu.VMEM((8,) + pack_g.shape, F32), pltpu.SemaphoreType.DMA((7,)), pltpu.SemaphoreType.DMA((7,))],
    )(pack_g, pack_w, pack_m, pack_v)


def _adamw(name, w, g, m, v):
    R, C = w.shape
    tr = _pick(R, 256, 8)

    def body(w_ref, g_ref, m_ref, v_ref, d_out, m_out, v_out):
        d_out[...], m_out[...], v_out[...] = _adamw_math(w_ref[...], g_ref[...], m_ref[...], v_ref[...])

    spec = pl.BlockSpec((tr, C), lambda i: (i, 0))
    return pl.pallas_call(
        body, name=name, grid=(R // tr,), in_specs=[spec] * 4, out_specs=[spec] * 3,
        out_shape=[jax.ShapeDtypeStruct((R, C), F32)] * 3, compiler_params=_params(),
    )(w, g, m, v)


def _pack_small(vals, last):
    flat = jnp.concatenate([v.reshape(-1) for v in vals] + [last.reshape(-1)])
    return jnp.pad(flat, (0, SMALL_ROWS * LANES - flat.shape[0])).reshape(SMALL_ROWS, LANES)


def kernel(x, positions, g_mix, w_in, g_q_a, w_q_b, g_kv_a, w_kv_b, g_qn, g_kn, w_mla_out, ret_decay_fwd, ret_decay_bwd, w_ret_out, w_out, g_ffn, w_gate_up, w_down, loss_target, m_g_mix, m_w_in, m_g_q_a, m_w_q_b, m_g_kv_a, m_w_kv_b, m_g_qn, m_g_kn, m_w_mla_out, m_ret_decay_fwd, m_ret_decay_bwd, m_w_ret_out, m_w_out, m_g_ffn, m_w_gate_up, m_w_down, v_g_mix, v_w_in, v_g_q_a, v_w_q_b, v_g_kv_a, v_w_kv_b, v_g_qn, v_g_kn, v_w_mla_out, v_ret_decay_fwd, v_ret_decay_bwd, v_w_ret_out, v_w_out, v_g_ffn, v_w_gate_up, v_w_down):
    given = dict(locals())
    S = x.shape[1]
    xs, tgt = x.reshape(S, D_MODEL), loss_target.reshape(S, D_MODEL)
    pos = positions.reshape(S, 1).astype(F32)

    first_shards = [given[n].astype(BF16) for n in FIRST]
    my_chip = 2 * lax.axis_index("x") + lax.axis_index("y")
    wts = {n: _assemble(n, _fill_slot(g, s, my_chip))
           for n, g, s in zip(FIRST, _weight_gather_first(first_shards), first_shards)}
    late_shards = {n: given[n].astype(BF16) for n in LATE}
    small = {n: given[n].reshape(1, -1) for n in SMALL}

    loss_row, dx, pieces, sgrads = _local_step(xs, pos, tgt, wts, late_shards, small)

    halves = [_grad_sum8("grad_sum_" + n, got) for n, got in zip(FIRST + LATE, pieces)]
    reduced = _half_exchange(halves)

    out = {}
    for n, r in zip(FIRST + LATE, reduced):
        g = r.reshape(given[n].shape)
        out["grad_" + n] = g
        out["delta_" + n], out["new_m_" + n], out["new_v_" + n] = _adamw("adamw_" + n, given[n], g, given["m_" + n], given["v_" + n])

    one = jnp.ones((1,), F32)
    pk = _small_allreduce_adamw(
        _pack_small([sgrads[n] for n in SMALL], loss_row[0, :1]),
        _pack_small([given[n] for n in SMALL], 0 * one),
        _pack_small([given["m_" + n] for n in SMALL], 0 * one),
        _pack_small([given["v_" + n] for n in SMALL], one))
    off = 0
    for n in SMALL:
        sz = given[n].shape[0]
        for pre, arr in zip(["grad_", "delta_", "new_m_", "new_v_"], pk):
            out[pre + n] = arr.reshape(-1)[off:off + sz]
        off += sz
    loss = pk[0].reshape(-1)[off]

    return (loss, dx.reshape(x.shape), *[out["grad_" + n] for n in WEIGHTS], *[out["delta_" + n] for n in WEIGHTS],
            *[out["new_m_" + n] for n in WEIGHTS], *[out["new_v_" + n] for n in WEIGHTS])
```

```python
import functools
import math

import numpy as np
import jax
import jax.numpy as jnp
from jax import lax
from jax.experimental import pallas as pl
from jax.experimental.pallas import tpu as pltpu

F32 = jnp.float32
BF16 = jnp.bfloat16
MESH = pl.DeviceIdType.MESH

D_MODEL = 1024
HEADS = 8
LANES = 128
MLA_Q_RANK, MLA_KV_RANK = 256, 128
MLA_NOPE, MLA_ROPE, MLA_V = 64, 32, 64
MLA_QK = MLA_NOPE + MLA_ROPE
LN2 = math.log(2.0)
MLA_Q_SCALE = MLA_QK ** -0.5 / LN2
RET_QK, RET_V, RET_CHUNK = 64, 128, 128
RET_QK_DTYPE = BF16
FFN_HIDDEN = 2816
ROPE_THETA = 10000.0
EPS = 1e-6
IN_SPLITS = [256, 128, 32, 512, 512, 1024, 1024, 2048]
IN_OFFS = [0] + list(np.cumsum(IN_SPLITS))
ADAM_LR, ADAM_B1, ADAM_B2, ADAM_EPS, ADAM_WD, ADAM_STEP = 0.001, 0.9, 0.999, 1e-08, 0.01, 10

VMEM_LIMIT = 56 * 1024 * 1024
ROW_TILE = 256
HEAD_ROW_TILE = 2048
MM_TM, MM_TN, MM_TK, MM_KFULL = 1408, 2048, 2048, 2816
ATT_TQ = 256
ATT_BQ, ATT_BK = 1024, 1024
ATT_HEADS_PER_STEP = 8
ATT_BWD_HEADS_PER_STEP = 4

SHARDED = ["w_in", "w_q_b", "w_kv_b", "w_mla_out", "w_ret_out", "w_out", "w_gate_up", "w_down"]
COL_SHARDED = {"w_in", "w_q_b", "w_kv_b", "w_mla_out", "w_gate_up"}
FIRST = ["w_in", "w_q_b", "w_kv_b"]
LATE = ["w_mla_out", "w_ret_out", "w_out", "w_gate_up", "w_down"]
SMALL = ["g_mix", "g_q_a", "g_kv_a", "g_qn", "g_kn", "ret_decay_fwd", "ret_decay_bwd", "g_ffn"]
WEIGHTS = ["g_mix", "w_in", "g_q_a", "w_q_b", "g_kv_a", "w_kv_b", "g_qn", "g_kn", "w_mla_out",
           "ret_decay_fwd", "ret_decay_bwd", "w_ret_out", "w_out", "g_ffn", "w_gate_up", "w_down"]
SMALL_ROWS = 24


def _params(**kw):
    return pltpu.CompilerParams(vmem_limit_bytes=VMEM_LIMIT, **kw)


def _pick(dim, target, unit=128):
    if dim <= target:
        return dim
    best = None
    for d in range(unit, target + 1, unit):
        if dim % d == 0:
            best = d
    assert best is not None, (dim, target)
    return best


_DOT = {"nn": (((1,), (0,)), ((), ())), "nt": (((1,), (1,)), ((), ())), "tn": (((0,), (0,)), ((), ()))}


def _dot(a, b, mode="nn"):
    return lax.dot_general(a, b, _DOT[mode], preferred_element_type=F32)


def _rms_rows(x, g):
    x = x.astype(F32)
    return x * lax.rsqrt(jnp.mean(x * x, axis=-1, keepdims=True) + EPS) * g


def _epi_loss(acc, extras, params):
    e = acc + extras[0] - extras[1]
    dy = e * (1.0 / D_MODEL)
    loss = 0.5 * jnp.sum(jnp.mean(e * e, axis=-1, keepdims=True), axis=0, keepdims=True)
    return [dy, dy], [jnp.broadcast_to(loss, (1, LANES))]


def _epi_rms_bwd(n_out):
    def fn(acc, extras, params):
        _, vjp = jax.vjp(_rms_rows, extras[0], params[0])
        dx, dg = vjp(acc)
        return [dx + extras[1]] * n_out, [dg]
    return fn


def _mm(name, a, b, mode, out_dtype, res=None, a_gain=None, a_scale=None, epilogue=None, shard_out=False,
        scatter=None):
    if mode == "nn":
        (M, K), (K2, N) = a.shape, b.shape
    elif mode == "nt":
        (M, K), (N, K2) = a.shape, b.shape
    else:
        (K, M), (K2, N) = a.shape, b.shape
    assert K == K2, (name, a.shape, b.shape)
    tm, tn = _pick(M, MM_TM), _pick(N, MM_TN)
    tk = K if K <= MM_KFULL else _pick(K, MM_TK)
    if shard_out:
        tm, tn = M // 2, N // 4
    if a_scale is not None:
        assert mode == "tn", name
        tk = _pick(K, MM_TK // 2)
    if epilogue is not None:
        tm = _pick(M, MM_TM // 2)
    nk = K // tk
    cache_a = a_gain is not None
    if a_gain is not None:
        assert mode == "nn" and tk == K and epilogue is None and not shard_out, name
    n_in = 2 + (res is not None) + (a_gain is not None) + 2 * (a_scale is not None)
    extras, eparams, e_outs, e_sums = ([], [], [], [])
    if epilogue is not None:
        assert tn == N and res is None and not shard_out, name
        epi_fn, extras, eparams, e_outs, e_sums = epilogue
    n_out = len(e_outs) + len(e_sums) if epilogue is not None else 1 + cache_a
    scatter = list(scatter or [])
    n_sc = len(scatter)
    assert not n_sc or epilogue is not None, name
    ni, nj = M // tm, N // tn

    def body(*refs):
        a_ref, b_ref = refs[0], refs[1]
        base = n_in + len(extras) + len(eparams)
        ex_refs = refs[n_in:n_in + len(extras)]
        ep_refs = refs[n_in + len(extras):base]
        sc_in, out_refs = refs[base:base + n_sc], refs[base + n_sc:base + n_sc + n_out]
        sc_out = refs[base + n_sc + n_out:base + 2 * n_sc + n_out]
        scratch = refs[base + 2 * n_sc + n_out:]
        acc = scratch[0]
        i, j, k = pl.program_id(0), pl.program_id(1), pl.program_id(2)

        if n_sc:
            @pl.when(jnp.logical_and(i == 0, jnp.logical_and(j == 0, k == 0)))
            def _():
                _scatter_start(_scatter_copies(sc_in, sc_out, scratch[-2], scratch[-1]))

        @pl.when(k == 0)
        def _():
            acc[...] = jnp.zeros_like(acc)

        if cache_a:
            @pl.when(j == 0)
            def _():
                x = a_ref[...].astype(F32)
                rstd = lax.rsqrt(jnp.mean(x * x, axis=-1, keepdims=True) + EPS)
                scratch[1][...] = (x * rstd * refs[n_in - 1][...]).astype(BF16)
                out_refs[1][...] = rstd
            av = scratch[1][...]
        elif a_scale is not None:
            av = (a_ref[...].astype(F32) * refs[n_in - 2][...] * refs[n_in - 1][...]).astype(BF16)
        else:
            av = a_ref[...].astype(BF16)
        acc[...] += _dot(av, b_ref[...].astype(BF16), mode)

        @pl.when(k == nk - 1)
        def _():
            if epilogue is None:
                r = acc[...]
                if res is not None:
                    r = r + refs[2][...].astype(F32)
                out_refs[0][...] = r.astype(out_refs[0].dtype).reshape(out_refs[0].shape)
            else:
                vals, sums = epi_fn(acc[...], [r[...] for r in ex_refs], [p[...] for p in ep_refs])
                for o_ref, v in zip(out_refs, vals):
                    o_ref[...] = v.astype(o_ref.dtype)
                for s_ref, v in zip(out_refs[len(vals):], sums):
                    @pl.when(i == 0)
                    def _(s_ref=s_ref):
                        s_ref[...] = jnp.zeros_like(s_ref)
                    s_ref[...] += v

        if n_sc:
            @pl.when(jnp.logical_and(i == ni - 1, jnp.logical_and(j == nj - 1, k == nk - 1)))
            def _():
                _scatter_wait(_scatter_copies(sc_in, sc_out, scratch[-2], scratch[-1]))

    a_spec = pl.BlockSpec((tk, tm), lambda i, j, k: (k, i)) if mode == "tn" else pl.BlockSpec((tm, tk), lambda i, j, k: (i, k))
    b_spec = pl.BlockSpec((tn, tk), lambda i, j, k: (j, k)) if mode == "nt" else pl.BlockSpec((tk, tn), lambda i, j, k: (k, j))
    o_spec = pl.BlockSpec((tm, tn), lambda i, j, k: (i, j))
    const = lambda p: pl.BlockSpec(p.shape, lambda i, j, k: (0,) * p.ndim)
    ins, specs = [a, b], [a_spec, b_spec]
    if res is not None:
        ins.append(res)
        specs.append(o_spec)
    if a_gain is not None:
        ins.append(a_gain)
        specs.append(const(a_gain))
    if a_scale is not None:
        ins += list(a_scale)
        specs += [pl.BlockSpec((tk, 1), lambda i, j, k: (k, 0)), pl.BlockSpec((1, tm), lambda i, j, k: (0, i))]
    ins += list(extras) + list(eparams)
    specs += [o_spec] * len(extras) + [const(p) for p in eparams]
    if epilogue is not None:
        out_specs = [o_spec] * len(e_outs) + [pl.BlockSpec(s, lambda i, j, k: (0, 0)) for s in e_sums]
        out_shape = [jax.ShapeDtypeStruct((M, N), dt) for dt in e_outs] + [jax.ShapeDtypeStruct(s, F32) for s in e_sums]
    elif shard_out:
        out_specs = pl.BlockSpec((1, 1, tm, tn), lambda i, j, k: (j, i, 0, 0))
        out_shape = jax.ShapeDtypeStruct((4, 2, tm, tn), out_dtype)
    elif cache_a:
        out_specs = [o_spec, pl.BlockSpec((tm, 1), lambda i, j, k: (i, 0))]
        out_shape = [jax.ShapeDtypeStruct((M, N), out_dtype), jax.ShapeDtypeStruct((M, 1), F32)]
    else:
        out_specs, out_shape = o_spec, jax.ShapeDtypeStruct((M, N), out_dtype)
    scratch_shapes = [pltpu.VMEM((tm, tn), F32)] + ([pltpu.VMEM((tm, tk), BF16)] if cache_a else [])
    if n_sc:
        ins += scatter
        specs += [ANY] * n_sc
        out_specs = list(out_specs) + [ANY] * n_sc
        out_shape = list(out_shape) + [jax.ShapeDtypeStruct((8,) + g.shape[2:], g.dtype) for g in scatter]
        scratch_shapes += [pltpu.SemaphoreType.DMA((7 * n_sc,)), pltpu.SemaphoreType.DMA((7 * n_sc,))]
    res_ = pl.pallas_call(
        body, name=name, grid=(ni, nj, nk), in_specs=specs, out_specs=out_specs, out_shape=out_shape,
        scratch_shapes=scratch_shapes, compiler_params=_params(),
    )(*ins)
    if n_sc:
        return list(res_[:n_out]) + [[_own_piece(got, g) for got, g in zip(res_[n_out:], scatter)]]
    return res_


def _piece_spec(tm, piece):
    _, w, c0, per_group = piece
    if per_group:
        return pl.BlockSpec((tm, w), lambda i, g: (i, c0 + g))
    return pl.BlockSpec((tm, w), lambda i, g: (i, c0))


def _const_spec(p):
    return pl.BlockSpec(p.shape, lambda i, g: (0, 0))


def _rowwise(name, fn, params, rows, auxs, outs, tm, groups=1):
    S = rows[0][0].shape[0]
    tm = min(tm, S)
    n_p, n_r, n_a = len(params), len(rows), len(auxs)

    def body(*refs):
        p = [r[...] for r in refs[:n_p]]
        r_ = [r[...] for r in refs[n_p:n_p + n_r]]
        a_ = [r[...] for r in refs[n_p + n_r:n_p + n_r + n_a]]
        for o_ref, o in zip(refs[n_p + n_r + n_a:], fn(p, r_, a_)):
            o_ref[...] = o.astype(o_ref.dtype)

    out_specs, out_shape = [], []
    for w, dt, per_group in outs:
        out_specs.append(_piece_spec(tm, (None, w, 0, per_group)))
        out_shape.append(jax.ShapeDtypeStruct((S, w * (groups if per_group else 1)), dt))
    return pl.pallas_call(
        body, name=name, grid=(S // tm, groups),
        in_specs=[_const_spec(p) for p in params] + [_piece_spec(tm, q) for q in list(rows) + list(auxs)],
        out_specs=out_specs, out_shape=out_shape, compiler_params=_params(),
    )(*params, *[q[0] for q in list(rows) + list(auxs)])


def _rowwise_vjp(name, fn, params, rows, auxs, cots, d_outs, tm, groups=1, adds=None):
    S = rows[0][0].shape[0]
    tm = min(tm, S)
    n_p, n_r, n_a = len(params), len(rows), len(auxs)
    cot_flat = [q for c in cots for q in c]
    adds = adds or [None] * len(d_outs)
    add_flat = [q for q in adds if q is not None]
    n_c, n_add = len(cot_flat), len(add_flat)
    shared = [not all(rows[k][3] for k in idx) and groups > 1 for idx, _ in d_outs]

    def body(*refs):
        pos = 0
        p = [r[...] for r in refs[pos:pos + n_p]]; pos += n_p
        r_ = [r[...] for r in refs[pos:pos + n_r]]; pos += n_r
        a_ = [r[...] for r in refs[pos:pos + n_a]]; pos += n_a
        c_refs = refs[pos:pos + n_c]; pos += n_c
        add_refs = list(refs[pos:pos + n_add]); pos += n_add
        d_refs = refs[pos:pos + len(d_outs)]; pos += len(d_outs)
        dp_refs = refs[pos:]
        i, g = pl.program_id(0), pl.program_id(1)
        outs, vjp_fn = jax.vjp(lambda pp, rr: fn(pp, rr, a_), p, r_)
        cts, ci = [], 0
        for c, o in zip(cots, outs):
            t = c_refs[ci][...].astype(F32)
            for extra in c_refs[ci + 1:ci + len(c)]:
                t = t + extra[...].astype(F32)
            ci += len(c)
            cts.append(t.astype(o.dtype))
        dp, dr = vjp_fn(cts)
        for (idx, _), d_ref, add, sh in zip(d_outs, d_refs, adds, shared):
            val = dr[idx[0]].astype(F32) if len(idx) == 1 else jnp.concatenate([dr[k].astype(F32) for k in idx], axis=1)
            if add is not None:
                val = val + add_refs.pop(0)[...].astype(F32)
            if sh:
                @pl.when(g == 0)
                def _(d_ref=d_ref):
                    d_ref[...] = jnp.zeros_like(d_ref)
                d_ref[...] += val.astype(d_ref.dtype)
            else:
                d_ref[...] = val.astype(d_ref.dtype)
        first = jnp.logical_and(i == 0, g == 0)
        for dp_ref, d in zip(dp_refs, dp):
            @pl.when(first)
            def _(dp_ref=dp_ref):
                dp_ref[...] = jnp.zeros_like(dp_ref)
            dp_ref[...] += d.astype(F32)

    out_specs, out_shape = [], []
    for (idx, dt), sh in zip(d_outs, shared):
        w = sum(rows[k][1] for k in idx)
        per_group = (not sh) and groups > 1
        out_specs.append(_piece_spec(tm, (None, w, 0, per_group)))
        out_shape.append(jax.ShapeDtypeStruct((S, w * (groups if per_group else 1)), dt))
    for p in params:
        out_specs.append(_const_spec(p))
        out_shape.append(jax.ShapeDtypeStruct(p.shape, F32))
    pieces = list(rows) + list(auxs) + cot_flat + add_flat
    res = pl.pallas_call(
        body, name=name, grid=(S // tm, groups),
        in_specs=[_const_spec(p) for p in params] + [_piece_spec(tm, q) for q in pieces],
        out_specs=out_specs, out_shape=out_shape, compiler_params=_params(),
    )(*params, *[q[0] for q in pieces])
    return list(res[:len(d_outs)]), list(res[len(d_outs):])


def _lane_roll(x, shift):
    @jax.custom_vjp
    def roll(v):
        return pltpu.roll(v, shift, 1)

    roll.defvjp(lambda v: (roll(v), None), lambda _, ct: (pltpu.roll(ct, LANES - shift, 1),))
    return roll(x)


@jax.custom_vjp
def _sigmoid(x):
    return 1.0 / (1.0 + jnp.exp(-x))


def _sigmoid_fwd(x):
    s = _sigmoid(x)
    return s, s


_sigmoid.defvjp(_sigmoid_fwd, lambda s, ct: (ct * s * (1.0 - s),))


def _rope(x, cos, sin_lo, sin_hi, half):
    return x * cos + _lane_roll(x, LANES - half) * sin_lo + _lane_roll(x, half) * sin_hi


def _f_rope_table(p, r, a):
    inv, first, second, fixed = p
    ang = a[0] * inv
    cs, sn = jnp.cos(ang), jnp.sin(ang)
    return [cs * (first + second) + fixed, -sn * first, sn * second]


def _f_rms(p, r, a):
    x = r[0].astype(F32)
    return [x * lax.rsqrt(jnp.mean(x * x, axis=-1, keepdims=True) + EPS) * p[0]]


def _f_mla_a(p, r, a):
    return _f_rms([p[0]], [r[0]], a) + _f_rms([p[1]], [r[1]], a)


def _f_mla_b(p, r, a):
    def norm_rope(v, g):
        ms = jnp.sum(v * v, axis=-1, keepdims=True) * (1.0 / MLA_QK)
        return _rope(v * lax.rsqrt(ms + EPS) * g, a[0], a[1], a[2], MLA_ROPE // 2)

    return [norm_rope(r[0].astype(F32), p[0]) * MLA_Q_SCALE, norm_rope(r[1].astype(F32) + r[2].astype(F32), p[1])]


def _f_ret_rope(p, r, a):
    q = _rope(r[0].astype(F32), a[0], a[1], a[2], RET_QK // 2)
    k = _rope(r[1].astype(F32), a[0], a[1], a[2], RET_QK // 2)
    return [q, k * (RET_QK ** -0.5)]


def _f_ret_post(p, r, a):
    ret = r[0].astype(F32) + r[1].astype(F32)
    g = r[2].astype(F32)
    normed = ret * lax.rsqrt(jnp.mean(ret * ret, axis=-1, keepdims=True) + EPS)
    return [g * _sigmoid(g) * normed]


def _f_merge(p, r, a):
    return [_sigmoid(r[0].astype(F32)) * r[2].astype(F32) + _sigmoid(r[1].astype(F32)) * r[3].astype(F32)]


def _f_swiglu(p, r, a):
    g = r[0].astype(F32)
    return [g * _sigmoid(g) * r[1].astype(F32)]


def _f_delta(p, r, a):
    d = jnp.sum(r[0].astype(F32) * r[1].astype(F32), axis=-1, keepdims=True)
    return [jnp.broadcast_to(d, r[0].shape)]


def _f_add(p, r, a):
    return [r[0].astype(F32) + r[1].astype(F32)]


def _loss_kernel(y, tgt):
    S, Dm = y.shape
    tm = min(ROW_TILE, S)

    def body(y_ref, t_ref, dy_ref, loss_ref):
        @pl.when(pl.program_id(0) == 0)
        def _():
            loss_ref[...] = jnp.zeros_like(loss_ref)

        e = y_ref[...] - t_ref[...]
        dy_ref[...] = e * (1.0 / Dm)
        loss_ref[...] += 0.5 * jnp.sum(jnp.mean(e * e, axis=-1, keepdims=True), axis=0, keepdims=True)

    row = pl.BlockSpec((tm, Dm), lambda i: (i, 0))
    return pl.pallas_call(
        body, name="loss", grid=(S // tm,), in_specs=[row, row],
        out_specs=[row, pl.BlockSpec((1, LANES), lambda i: (0, 0))],
        out_shape=[jax.ShapeDtypeStruct((S, Dm), F32), jax.ShapeDtypeStruct((1, LANES), F32)],
        compiler_params=_params(),
    )(y, tgt)


def _flash_fwd(q, k, kv, shards):
    S = q.shape[0]
    tq = min(ATT_TQ, S)
    nq = S // tq
    n = len(shards)

    def body(q_ref, k_ref, v_ref, *rest):
        shard_refs, (o_ref, lse_ref), gathered = rest[:n], rest[n:n + 2], rest[n + 2:2 * n + 2]
        send_sems, recv_sems = rest[2 * n + 2:]
        h, qi = pl.program_id(0), pl.program_id(1)

        @pl.when(jnp.logical_and(h == 0, qi == 0))
        def _():
            _gather_start(_gather_copies(shard_refs, gathered, send_sems, recv_sems))

        for hh in range(hps):
            lanes = slice(hh * LANES, (hh + 1) * LANES)
            s = _dot(q_ref[:, lanes], k_ref[:, lanes], "nt")
            m = jnp.max(s, axis=-1, keepdims=True)
            p = jnp.exp2(s - m)
            l = jnp.sum(p, axis=-1, keepdims=True)
            o_ref[:, lanes] = (_dot(p.astype(BF16), v_ref[:, lanes]) / l).astype(o_ref.dtype)
            lse_ref[:, lanes] = jnp.broadcast_to(m + jnp.log2(l), (tq, LANES))

        @pl.when(jnp.logical_and(h == HEADS // hps - 1, qi == nq - 1))
        def _():
            _gather_wait(_gather_copies(shard_refs, gathered, send_sems, recv_sems))

    hps = ATT_HEADS_PER_STEP
    qs = pl.BlockSpec((tq, hps * LANES), lambda h, i: (i, h))
    res = pl.pallas_call(
        body, name="mla_fwd", grid=(HEADS // hps, nq),
        in_specs=[qs, pl.BlockSpec((S, hps * LANES), lambda h, i: (0, h), pipeline_mode=pl.Buffered(1)),
                  pl.BlockSpec((S, hps * LANES), lambda h, i: (0, HEADS // hps + h), pipeline_mode=pl.Buffered(1))]
        + [ANY] * n,
        out_specs=[qs, qs] + [ANY] * n,
        out_shape=[jax.ShapeDtypeStruct((S, HEADS * LANES), BF16), jax.ShapeDtypeStruct((S, HEADS * LANES), F32)]
        + [jax.ShapeDtypeStruct((4,) + s.shape, s.dtype) for s in shards],
        scratch_shapes=[pltpu.SemaphoreType.DMA((3 * n,)), pltpu.SemaphoreType.DMA((3 * n,))],
        compiler_params=_params(),
    )(q, k, kv, *shards)
    mine = 2 * lax.axis_index("x") + lax.axis_index("y")
    return res[0], res[1], [_fill_slot(g, s, mine) for g, s in zip(res[2:], shards)]


def _flash_bwd(q, k, kv, do, lse, delta, gs):
    S = q.shape[0]
    tq, tk = min(ATT_BQ, S), min(ATT_BK, S)
    nq, nkt = S // tq, S // tk
    n = len(gs)

    def body(q_ref, k_ref, v_ref, do_ref, lse_ref, dl_ref, *rest):
        g_refs, (dq_ref, dk_ref, dv_ref), got_refs = rest[:n], rest[n:n + 3], rest[n + 3:2 * n + 3]
        dk_sc, dv_sc, send_sems, recv_sems = rest[2 * n + 3:]
        h, ki, qi = pl.program_id(0), pl.program_id(1), pl.program_id(2)

        @pl.when(jnp.logical_and(h == 0, jnp.logical_and(ki == 0, qi == 0)))
        def _():
            _scatter_start(_scatter_copies(g_refs, got_refs, send_sems, recv_sems))

        @pl.when(jnp.logical_and(ki == 0, qi == 0))
        def _():
            dq_ref[...] = jnp.zeros_like(dq_ref)

        @pl.when(qi == 0)
        def _():
            dk_sc[...] = jnp.zeros_like(dk_sc)
            dv_sc[...] = jnp.zeros_like(dv_sc)

        rows = pl.ds(pl.multiple_of(qi * tq, tq), tq)
        for hh in range(hps):
            lanes = slice(hh * LANES, (hh + 1) * LANES)
            qv, kv_, dov = q_ref[:, lanes], k_ref[:, lanes], do_ref[:, lanes]
            p = jnp.exp2(_dot(qv, kv_, "nt") - lse_ref[:, lanes][:, :1])
            dp = _dot(dov, v_ref[:, lanes], "nt")
            ds = (p * (dp - dl_ref[:, lanes][:, :1]) * LN2).astype(BF16)
            dv_sc[:, lanes] += _dot(p.astype(BF16), dov, "tn")
            dk_sc[:, lanes] += _dot(ds, qv, "tn")
            dq_ref[rows, lanes] += _dot(ds, kv_)

        @pl.when(qi == nq - 1)
        def _():
            dk_ref[...] = dk_sc[...].astype(dk_ref.dtype)
            dv_ref[...] = dv_sc[...].astype(dv_ref.dtype)

        @pl.when(jnp.logical_and(h == HEADS // hps - 1, jnp.logical_and(ki == nkt - 1, qi == nq - 1)))
        def _():
            _scatter_wait(_scatter_copies(g_refs, got_refs, send_sems, recv_sems))

    hps = ATT_BWD_HEADS_PER_STEP
    qs = pl.BlockSpec((tq, hps * LANES), lambda h, j, i: (i, h))
    ks = pl.BlockSpec((tk, hps * LANES), lambda h, j, i: (j, h))
    res = pl.pallas_call(
        body, name="mla_bwd", grid=(HEADS // hps, nkt, nq),
        in_specs=[qs, ks, pl.BlockSpec((tk, hps * LANES), lambda h, j, i: (j, HEADS // hps + h)), qs, qs, qs] + [ANY] * n,
        out_specs=[pl.BlockSpec((S, hps * LANES), lambda h, j, i: (0, h), pipeline_mode=pl.Buffered(1)), ks, ks] + [ANY] * n,
        out_shape=[jax.ShapeDtypeStruct((S, HEADS * LANES), F32), jax.ShapeDtypeStruct((S, HEADS * LANES), BF16),
                   jax.ShapeDtypeStruct((S, HEADS * LANES), BF16)]
        + [jax.ShapeDtypeStruct((8,) + g.shape[2:], g.dtype) for g in gs],
        scratch_shapes=[pltpu.VMEM((tk, hps * LANES), F32)] * 2
        + [pltpu.SemaphoreType.DMA((7 * n,)), pltpu.SemaphoreType.DMA((7 * n,))],
        compiler_params=_params(),
    )(q, k, kv, do, lse, delta, *gs)
    return res[0], res[1], res[2], [_own_piece(got, g) for got, g in zip(res[3:], gs)]


def _ret_tables(decay_row, backward):
    C = RET_CHUNK
    lg = -jnp.exp(decay_row)
    t = lax.broadcasted_iota(jnp.int32, (C, C), 0).astype(F32)
    s = lax.broadcasted_iota(jnp.int32, (C, C), 1).astype(F32)
    ridx = lax.broadcasted_iota(jnp.int32, (C, LANES), 0).astype(F32)
    if backward:
        dist, mask, aw, bw = s - t, s > t, C - ridx, ridx
    else:
        dist, mask, aw, bw = t - s, t >= s, ridx + 1.0, C - 1.0 - ridx
    dist = jnp.maximum(dist, 0.0)
    din = jnp.where(mask, jnp.exp(lg[:, :1] * dist), 0.0)
    return dict(din=din, dist=dist, a=jnp.exp(lg * aw), b=jnp.exp(lg * bw), c=jnp.exp(lg * C), aw=aw, bw=bw)


def _ret_fwd(qr, kr, proj, v_block, dec_f, dec_b):
    S = qr.shape[0]
    C = RET_CHUNK
    n = S // C
    W = HEADS * LANES

    def body(qf, kf, vf, qb, kb, vb, df, db, of, ob, sf_out, sb_out, st):
        @pl.when(pl.program_id(0) == 0)
        def _():
            st[...] = jnp.zeros_like(st)

        for d, (q_ref, k_ref, v_ref, dec, o_ref, s_out) in enumerate(
                [(qf, kf, vf, df, of, sf_out), (qb, kb, vb, db, ob, sb_out)]):
            for h in range(HEADS):
                lanes = slice(h * LANES, (h + 1) * LANES)
                tb = _ret_tables(dec[h:h + 1, :], d == 1)
                qf32, kf32, v = q_ref[:, lanes].astype(F32), k_ref[:, lanes].astype(F32), v_ref[:, lanes]
                state = st[d, h]
                s_out[0, h] = state
                inner = _dot((_dot(qf32.astype(BF16), kf32.astype(BF16), "nt") * tb["din"]).astype(BF16), v)
                cross = _dot((qf32 * tb["a"]).astype(BF16), state.astype(BF16))
                o_ref[:, lanes] = inner + cross
                st[d, h] = state * tb["c"] + _dot((kf32 * tb["b"]).astype(BF16), v, "tn")

    fw = lambda c0: pl.BlockSpec((C, W), lambda j: (j, c0))
    bw = lambda c0: pl.BlockSpec((C, W), lambda j: (n - 1 - j, c0))
    dec_spec = pl.BlockSpec((HEADS, LANES), lambda j: (0, 0))
    st_shape = jax.ShapeDtypeStruct((n, HEADS, LANES, LANES), F32)
    return pl.pallas_call(
        body, name="ret_fwd", grid=(n,),
        in_specs=[fw(0), fw(0), fw(v_block), bw(0), bw(0), bw(v_block), dec_spec, dec_spec],
        out_specs=[fw(0), bw(0), pl.BlockSpec((1, HEADS, LANES, LANES), lambda j: (j, 0, 0, 0)),
                   pl.BlockSpec((1, HEADS, LANES, LANES), lambda j: (n - 1 - j, 0, 0, 0))],
        out_shape=[jax.ShapeDtypeStruct((S, W), F32)] * 2 + [st_shape] * 2,
        scratch_shapes=[pltpu.VMEM((2, HEADS, LANES, LANES), F32)], compiler_params=_params(),
    )(qr, kr, proj, qr, kr, proj, dec_f, dec_b)


def _ret_bwd(qr, kr, proj, v_block, dret, sf, sb, dec_f, dec_b):
    S = qr.shape[0]
    C = RET_CHUNK
    n = S // C
    W = HEADS * LANES

    def body(qf, kf, vf, gf, sf_ref, qb, kb, vb, gb, sb_ref, df, db,
             dqf, dkf, dvf, dqb, dkb, dvb, ddf, ddb, ds_sc):
        j = pl.program_id(0)

        @pl.when(j == 0)
        def _():
            ds_sc[...] = jnp.zeros_like(ds_sc)
            ddf[...] = jnp.zeros_like(ddf)
            ddb[...] = jnp.zeros_like(ddb)

        for d, (q_ref, k_ref, v_ref, g_ref, s_ref, dec, dq_ref, dk_ref, dv_ref, dd_ref) in enumerate(
                [(qf, kf, vf, gf, sf_ref, df, dqf, dkf, dvf, ddf), (qb, kb, vb, gb, sb_ref, db, dqb, dkb, dvb, ddb)]):
            for h in range(HEADS):
                lanes = slice(h * LANES, (h + 1) * LANES)
                tb = _ret_tables(dec[h:h + 1, :], d == 1)
                v, g = v_ref[:, lanes], g_ref[:, lanes]
                qf32, kf32 = q_ref[:, lanes].astype(F32), k_ref[:, lanes].astype(F32)
                q, k = qf32.astype(BF16), kf32.astype(BF16)
                state, dstate = s_ref[0, h], ds_sc[d, h]
                dstate_b = dstate.astype(BF16)
                dp = _dot(g, v, "nt")
                a_ = _dot(q, k, "nt")
                da = (dp * tb["din"]).astype(BF16)
                g1 = _dot(g, state.astype(BF16), "nt")
                g2 = _dot(v, dstate_b, "nt")
                dq_ref[:, lanes] = (_dot(da, k) + g1 * tb["a"]).astype(dq_ref.dtype)
                dk_ref[:, lanes] = (_dot(da, q, "tn") + g2 * tb["b"]).astype(dk_ref.dtype)
                dv_ref[:, lanes] = (_dot((a_ * tb["din"]).astype(BF16), g, "tn")
                                    + _dot((kf32 * tb["b"]).astype(BF16), dstate_b)).astype(dv_ref.dtype)
                dlg = (jnp.sum(dp * a_ * tb["din"] * tb["dist"], keepdims=True)
                       + jnp.sum(g1 * qf32 * tb["a"] * tb["aw"], keepdims=True)
                       + jnp.sum(g2 * kf32 * tb["b"] * tb["bw"], keepdims=True)
                       + C * jnp.sum(tb["c"] * dstate * state, keepdims=True))
                dd_ref[h:h + 1, :] += jnp.broadcast_to(dlg, (1, LANES))
                ds_sc[d, h] = dstate * tb["c"] + _dot((qf32 * tb["a"]).astype(BF16), g, "tn")

        @pl.when(j == n - 1)
        def _():
            ddf[...] = ddf[...] * -jnp.exp(df[...])
            ddb[...] = ddb[...] * -jnp.exp(db[...])

    fw = lambda c0: pl.BlockSpec((C, W), lambda j: (n - 1 - j, c0))
    bw = lambda c0: pl.BlockSpec((C, W), lambda j: (j, c0))
    dec_spec = pl.BlockSpec((HEADS, LANES), lambda j: (0, 0))
    act = jax.ShapeDtypeStruct((S, W), BF16)
    return pl.pallas_call(
        body, name="ret_bwd", grid=(n,),
        in_specs=[fw(0), fw(0), fw(v_block), fw(0), pl.BlockSpec((1, HEADS, LANES, LANES), lambda j: (n - 1 - j, 0, 0, 0)),
                  bw(0), bw(0), bw(v_block), bw(0), pl.BlockSpec((1, HEADS, LANES, LANES), lambda j: (j, 0, 0, 0)),
                  dec_spec, dec_spec],
        out_specs=[fw(0)] * 3 + [bw(0)] * 3 + [dec_spec] * 2,
        out_shape=[act] * 6 + [jax.ShapeDtypeStruct((HEADS, LANES), F32)] * 2,
        scratch_shapes=[pltpu.VMEM((2, HEADS, LANES, LANES), F32)], compiler_params=_params(),
    )(qr, kr, proj, dret, sf, qr, kr, proj, dret, sb, dec_f, dec_b)


def _pad_heads(w, hd):
    K = w.shape[0]
    return jnp.pad(w.reshape(K, HEADS, hd), ((0, 0), (0, 0), (0, LANES - hd))).reshape(K, HEADS * LANES)


def _unpad_heads(w, hd):
    K = w.shape[0]
    return w.reshape(K, HEADS, LANES)[:, :, :hd].reshape(K, HEADS * hd)


def _rope_consts(first_lane, half):
    lane = np.arange(LANES)
    first = ((lane >= first_lane) & (lane < first_lane + half)).astype(np.float32)
    second = ((lane >= first_lane + half) & (lane < first_lane + 2 * half)).astype(np.float32)
    fixed = (lane < first_lane).astype(np.float32)
    j = np.where(first > 0, lane - first_lane, lane - first_lane - half) * (first + second)
    inv = (ROPE_THETA ** (-j.astype(np.float64) / half)).astype(np.float32)
    return [jnp.asarray(v.reshape(1, LANES), F32) for v in (inv, first, second, fixed)]


def _assemble(name, gathered):
    if name in COL_SHARDED:
        return jnp.transpose(gathered, (1, 0, 2)).reshape(gathered.shape[1], 4 * gathered.shape[2])
    return gathered.reshape(4 * gathered.shape[1], gathered.shape[2])


def _split_for_reducers(name, g, dtype):
    if name in COL_SHARDED:
        K, N4 = g.shape
        return jnp.transpose(g.reshape(2, K // 2, 4, N4 // 4), (2, 0, 1, 3)).astype(dtype)
    return g.reshape(4, 2, g.shape[0] // 8, g.shape[1]).astype(dtype)


def _local_step(x, pos, tgt, wts, late_shards, small):
    w_in = wts["w_in"]
    seg = [w_in[:, IN_OFFS[i]:IN_OFFS[i + 1]] for i in range(8)]
    kr_w = jnp.pad(seg[2], ((0, 0), (MLA_NOPE, LANES - MLA_QK)))
    w_in_p = jnp.concatenate([seg[7], seg[5], seg[6], _pad_heads(seg[3], RET_QK), _pad_heads(seg[4], RET_QK),
                              seg[0], seg[1], kr_w], axis=1)
    w_qb_p = _pad_heads(wts["w_q_b"], MLA_QK)
    kvw = wts["w_kv_b"].reshape(MLA_KV_RANK, HEADS, MLA_NOPE + MLA_V)
    pad_kv = lambda t: jnp.pad(t, ((0, 0), (0, 0), (0, LANES - t.shape[2]))).reshape(MLA_KV_RANK, HEADS * LANES)
    w_kn_p, w_v_p = pad_kv(kvw[:, :, :MLA_NOPE]), pad_kv(kvw[:, :, MLA_NOPE:])
    w_kv_p = jnp.concatenate([w_kn_p, w_v_p], axis=1)
    g_qn_p = jnp.pad(small["g_qn"], ((0, 0), (0, LANES - MLA_QK)))
    g_kn_p = jnp.pad(small["g_kn"], ((0, 0), (0, LANES - MLA_QK)))
    dec_f = jnp.broadcast_to(small["ret_decay_fwd"].reshape(HEADS, 1), (HEADS, LANES))
    dec_b = jnp.broadcast_to(small["ret_decay_bwd"].reshape(HEADS, 1), (HEADS, LANES))
    T, N = True, False
    RT, HT = ROW_TILE, HEAD_ROW_TILE

    tab_m = _rowwise("rope_table_mla", _f_rope_table, _rope_consts(MLA_NOPE, MLA_ROPE // 2), [(pos, 1, 0, N)], [(pos, 1, 0, N)],
                     [(LANES, F32, N)] * 3, HT)
    tab_r = _rowwise("rope_table_ret", _f_rope_table, _rope_consts(0, RET_QK // 2), [(pos, 1, 0, N)], [(pos, 1, 0, N)],
                     [(LANES, F32, N)] * 3, HT)
    aux_m = [(t, LANES, 0, N) for t in tab_m]
    aux_r = [(t, LANES, 0, N) for t in tab_r]

    proj, rstd1 = _mm("proj", x, w_in_p, "nn", BF16, a_gain=small["g_mix"])
    rows_a = [(proj, MLA_Q_RANK, 24, N), (proj, MLA_KV_RANK, 50, N)]
    cqn, ckvn = _rowwise("mla_lat_norm", _f_mla_a, [small["g_q_a"], small["g_kv_a"]], rows_a, [],
                         [(MLA_Q_RANK, BF16, N), (MLA_KV_RANK, BF16, N)], RT)
    qraw = _mm("mla_q_up", cqn, w_qb_p, "nn", BF16)
    kv = _mm("mla_kv_up", ckvn, w_kv_p, "nn", BF16)
    rows_b = [(qraw, LANES, 0, T), (kv, LANES, 0, T), (proj, LANES, 51, N)]
    q, k = _rowwise("mla_qk_norm_rope", _f_mla_b, [g_qn_p, g_kn_p], rows_b, aux_m, [(LANES, BF16, T)] * 2, HT, HEADS)
    o, lse, late = _flash_fwd(q, k, kv, [late_shards[n] for n in LATE])
    wl = {n: _assemble(n, g) for n, g in zip(LATE, late)}
    w_mla_p = jnp.pad(wl["w_mla_out"].reshape(HEADS, MLA_V, D_MODEL), ((0, 0), (0, LANES - MLA_V), (0, 0))).reshape(HEADS * LANES, D_MODEL)
    w_ret_out, w_out, w_gu, w_down = wl["w_ret_out"], wl["w_out"], wl["w_gate_up"], wl["w_down"]
    y_a = _mm("mla_out", o, w_mla_p, "nn", F32)
    rows_rr = [(proj, LANES, 32, T), (proj, LANES, 40, T)]
    qr, kr = _rowwise("ret_rope", _f_ret_rope, [], rows_rr, aux_r, [(LANES, RET_QK_DTYPE, T)] * 2, HT, HEADS)
    ret_f, ret_b, st_f, st_b = _ret_fwd(qr, kr, proj, 2, dec_f, dec_b)
    rows_rp = [(ret_f, LANES, 0, T), (ret_b, LANES, 0, T), (proj, LANES, 24, T)]
    (o_b,) = _rowwise("ret_post", _f_ret_post, [], rows_rp, [], [(LANES, BF16, T)], HT, HEADS)
    y_b = _mm("ret_out", o_b, w_ret_out, "nn", F32)
    rows_m = [(proj, D_MODEL, 0, N), (proj, D_MODEL, 1, N), (y_a, D_MODEL, 0, N), (y_b, D_MODEL, 0, N)]
    (merged,) = _rowwise("merge", _f_merge, [], rows_m, [], [(D_MODEL, BF16, N)], RT)
    x2 = _mm("mix_out", merged, w_out, "nn", F32, res=x)
    gu, rstd2 = _mm("ffn_gate_up", x2, w_gu, "nn", BF16, a_gain=small["g_ffn"])
    rows_sw = [(gu, FFN_HIDDEN, 0, N), (gu, FFN_HIDDEN, 1, N)]
    (act,) = _rowwise("swiglu", _f_swiglu, [], rows_sw, [], [(FFN_HIDDEN, BF16, N)], RT)
    dy, dy_b16, loss_row = _mm("ffn_down", act, w_down, "nn", None,
                               epilogue=(_epi_loss, [x2, tgt], [], [F32, BF16], [(1, LANES)]))

    dact = _mm("d_act", dy_b16, w_down, "nt", BF16)
    dw_down = _mm("dw_down", act, dy_b16, "tn", BF16)
    (dgu,), _ = _rowwise_vjp("swiglu_bwd", _f_swiglu, [], rows_sw, [], [[(dact, FFN_HIDDEN, 0, N)]], [([0, 1], BF16)], RT)
    dx2, dx2_b16, dg_ffn = _mm("d_h2", dgu, w_gu, "nt", None,
                               epilogue=(_epi_rms_bwd(2), [x2, dy], [small["g_ffn"]], [F32, BF16], [(1, D_MODEL)]))
    dw_gu = _mm("dw_gate_up", x2, dgu, "tn", BF16, a_scale=(rstd2, small["g_ffn"]), shard_out=True)
    dmerged = _mm("d_merged", dx2_b16, w_out, "nt", BF16)
    dw_out = _mm("dw_out", merged, dx2_b16, "tn", BF16)
    (dgl, dy_a, dy_b), _ = _rowwise_vjp("merge_bwd", _f_merge, [], rows_m, [], [[(dmerged, D_MODEL, 0, N)]],
                                        [([0, 1], BF16), ([2], BF16), ([3], BF16)], RT)
    do_b = _mm("d_ret_o", dy_b, w_ret_out, "nt", BF16)
    dw_ret_out = _mm("dw_ret_out", o_b, dy_b, "tn", BF16)
    (dret, dg_r), _ = _rowwise_vjp("ret_post_bwd", _f_ret_post, [], rows_rp, [], [[(do_b, LANES, 0, T)]],
                                   [([0], BF16), ([2], BF16)], HT, HEADS)
    dqf, dkf, dvf, dqb, dkb, dvb, ddec_f, ddec_b = _ret_bwd(qr, kr, proj, 2, dret, st_f, st_b, dec_f, dec_b)
    (dq_r, dk_r), _ = _rowwise_vjp("ret_rope_bwd", _f_ret_rope, [], rows_rr, aux_r,
                                   [[(dqf, LANES, 0, T), (dqb, LANES, 0, T)], [(dkf, LANES, 0, T), (dkb, LANES, 0, T)]],
                                   [([0], BF16), ([1], BF16)], HT, HEADS)
    (dv_r,) = _rowwise("ret_dv_sum", _f_add, [], [(dvf, D_MODEL, 0, N), (dvb, D_MODEL, 0, N)], [], [(D_MODEL, BF16, N)], RT)
    do = _mm("d_mla_o", dy_a, w_mla_p, "nt", BF16)
    dw_mla_p = _mm("dw_mla_out", o, dy_a, "tn", BF16)
    (delta,) = _rowwise("mla_delta", _f_delta, [], [(do, LANES, 0, T), (o, LANES, 0, T)], [], [(LANES, F32, T)], HT, HEADS)
    dw_mla = dw_mla_p.reshape(HEADS, LANES, D_MODEL)[:, :MLA_V].reshape(HEADS * MLA_V, D_MODEL)
    late_grads = {"w_mla_out": dw_mla, "w_ret_out": dw_ret_out, "w_out": dw_out, "w_down": dw_down}
    late_gs = [dw_gu if n == "w_gate_up" else _split_for_reducers(n, late_grads[n], BF16) for n in LATE]
    dq, dk, dv, late_got = _flash_bwd(q, k, kv, do, lse, delta, late_gs)
    (dqraw, dkn, dkr), (dg_qn_p, dg_kn_p) = _rowwise_vjp(
        "mla_qk_norm_rope_bwd", _f_mla_b, [g_qn_p, g_kn_p], rows_b, aux_m, [[(dq, LANES, 0, T)], [(dk, LANES, 0, T)]],
        [([0], BF16), ([1], BF16), ([2], F32)], HT, HEADS)
    dckvn = _mm("d_ckvn_v", dv, w_v_p, "nt", BF16, res=_mm("d_ckvn_k", dkn, w_kn_p, "nt", F32))
    dw_kn_p = _mm("dw_kv_k", ckvn, dkn, "tn", BF16)
    dw_v_p = _mm("dw_kv_v", ckvn, dv, "tn", BF16)
    dcqn = _mm("d_cqn", dqraw, w_qb_p, "nt", BF16)
    dw_qb_p = _mm("dw_q_b", cqn, dqraw, "tn", BF16)
    (dcq, dckv), (dg_q_a, dg_kv_a) = _rowwise_vjp(
        "mla_lat_norm_bwd", _f_mla_a, [small["g_q_a"], small["g_kv_a"]], rows_a, [],
        [[(dcqn, MLA_Q_RANK, 0, N)], [(dckvn, MLA_KV_RANK, 0, N)]], [([0], BF16), ([1], BF16)], RT)
    dproj = jnp.concatenate([dgl, dv_r, dg_r, dq_r, dk_r, dcq, dckv, dkr.astype(BF16)], axis=1)
    dw_in_p = _mm("dw_in", x, dproj, "tn", BF16, a_scale=(rstd1, small["g_mix"]))

    c = lambda a, b_: dw_in_p[:, a:b_]
    dw_in = jnp.concatenate([c(6144, 6400), c(6400, 6528), c(6528 + MLA_NOPE, 6528 + MLA_QK), _unpad_heads(c(4096, 5120), RET_QK),
                             _unpad_heads(c(5120, 6144), RET_QK), c(2048, 3072), c(3072, 4096), c(0, 2048)], axis=1)
    un_kv = lambda t: t.reshape(MLA_KV_RANK, HEADS, LANES)[:, :, :MLA_NOPE]
    dw_kv = jnp.concatenate([un_kv(dw_kn_p), un_kv(dw_v_p)], axis=2).reshape(MLA_KV_RANK, HEADS * (MLA_NOPE + MLA_V))
    grads = {"w_in": dw_in, "w_q_b": _unpad_heads(dw_qb_p, MLA_QK), "w_kv_b": dw_kv}
    dx, dg_mix, first_got = _mm("d_h", dproj, w_in_p, "nt", None,
                                epilogue=(_epi_rms_bwd(1), [x, dx2], [small["g_mix"]], [F32], [(1, D_MODEL)]),
                                scatter=[_split_for_reducers(n, grads[n], BF16) for n in FIRST])
    sgrads = {"g_mix": dg_mix, "g_q_a": dg_q_a, "g_kv_a": dg_kv_a, "g_qn": dg_qn_p[:, :MLA_QK], "g_kn": dg_kn_p[:, :MLA_QK],
              "ret_decay_fwd": ddec_f[:, 0].reshape(1, HEADS), "ret_decay_bwd": ddec_b[:, 0].reshape(1, HEADS), "g_ffn": dg_ffn}
    return loss_row, dx, first_got + late_got, sgrads


def _coords():
    return lax.axis_index("x"), lax.axis_index("y"), lax.axis_index("c")


def _other_chips(x, y):
    return [(1 - x, y), (x, 1 - y), (1 - x, 1 - y)]


ANY = pl.BlockSpec(memory_space=pl.ANY)


def _gather_copies(ins, outs, send_sems, recv_sems):
    x, y, c = _coords()
    mine = 2 * x + y
    sends, arrivals = [], []
    for w in range(len(ins)):
        for j, (cx, cy) in enumerate(_other_chips(x, y)):
            sems = dict(send_sem=send_sems.at[3 * w + j], recv_sem=recv_sems.at[3 * w + j],
                        device_id=(cx, cy, c), device_id_type=MESH)
            sends.append(pltpu.make_async_remote_copy(src_ref=ins[w], dst_ref=outs[w].at[mine], **sems))
            arrivals.append(functools.partial(pltpu.make_async_remote_copy, src_ref=ins[w],
                                              dst_ref=outs[w].at[2 * cx + cy], **sems))
    return sends, arrivals


def _gather_start(copies):
    for cp in copies[0]:
        cp.start()


def _gather_wait(copies):
    sends, arrivals = copies
    for make in arrivals:
        make().wait_recv()
    for cp in sends:
        cp.wait_send()


def _fill_slot(buf, piece, slot):
    idx = lax.broadcasted_iota(jnp.int32, (buf.shape[0],) + (1,) * piece.ndim, 0)
    return jnp.where(idx == slot, piece[None], buf)


def _weight_gather_first(shards):
    n = len(shards)

    def body(*refs):
        ins, outs = refs[:n], refs[n:2 * n]
        send_sems, recv_sems = refs[2 * n:]
        x, y, c = _coords()
        chips = _other_chips(x, y)
        mine = 2 * x + y

        def half(ref, slot, core):
            rows = ref.shape[1] // 2
            return ref.at[slot, pl.ds(pl.multiple_of(core * rows, 8), rows)]

        def copy(w, k, slot, core, to, src=None):
            return pltpu.make_async_remote_copy(
                src_ref=half(outs[w], slot, core) if src is None else src, dst_ref=half(outs[w], slot, core),
                send_sem=send_sems.at[6 * w + k], recv_sem=recv_sems.at[6 * w + k], device_id=to, device_id_type=MESH)

        first, passed = [], []
        for w in range(n):
            rows = ins[w].shape[0] // 2
            my_half = ins[w].at[pl.ds(pl.multiple_of(c * rows, 8), rows)]
            for j, (cx, cy) in enumerate(chips):
                cp = copy(w, j, mine, c, (cx, cy, c), src=my_half)
                cp.start()
                first.append(cp)
        for w in range(n):
            for j, (cx, cy) in enumerate(chips):
                copy(w, j, 2 * cx + cy, c, (x, y, c)).wait_recv()
                cp = copy(w, 3 + j, 2 * cx + cy, c, (x, y, 1 - c))
                cp.start()
                passed.append(cp)
        for w in range(n):
            for j, (cx, cy) in enumerate(chips):
                copy(w, 3 + j, 2 * cx + cy, 1 - c, (x, y, c)).wait_recv()
        for cp in first + passed:
            cp.wait_send()

    return pl.pallas_call(
        body, name="weight_gather_first", in_specs=[ANY] * n, out_specs=[ANY] * n,
        out_shape=[jax.ShapeDtypeStruct((4,) + s.shape, s.dtype) for s in shards],
        scratch_shapes=[pltpu.SemaphoreType.DMA((6 * n,)), pltpu.SemaphoreType.DMA((6 * n,))],
    )(*shards)


def _scatter_copies(ins, outs, send_sems, recv_sems):
    x, y, c = _coords()
    me = 4 * x + 2 * y + c
    sends, arrivals = [], []
    for w in range(len(ins)):
        for k in range(1, 8):
            px, py, pc = x ^ (k >> 2), y ^ ((k >> 1) & 1), c ^ (k & 1)
            sems = dict(send_sem=send_sems.at[7 * w + k - 1], recv_sem=recv_sems.at[7 * w + k - 1],
                        device_id=(px, py, pc), device_id_type=MESH)
            sends.append(pltpu.make_async_remote_copy(src_ref=ins[w].at[2 * px + py, pc], dst_ref=outs[w].at[me], **sems))
            arrivals.append(functools.partial(
                pltpu.make_async_remote_copy, src_ref=ins[w].at[2 * px + py, pc],
                dst_ref=outs[w].at[4 * px + 2 * py + pc], **sems))
    return sends, arrivals


_scatter_start, _scatter_wait = _gather_start, _gather_wait


def _own_piece(got, g):
    return got, g


def _grad_scatter_late(gs):
    n = len(gs)

    def body(*refs):
        copies = _scatter_copies(refs[:n], refs[n:2 * n], *refs[2 * n:])
        _scatter_start(copies)
        _scatter_wait(copies)

    return pl.pallas_call(
        body, name="grad_scatter_late", in_specs=[ANY] * n, out_specs=[ANY] * n,
        out_shape=[jax.ShapeDtypeStruct((8,) + g.shape[2:], g.dtype) for g in gs],
        scratch_shapes=[pltpu.SemaphoreType.DMA((7 * n,)), pltpu.SemaphoreType.DMA((7 * n,))],
    )(*gs)


def _grad_sum8(name, pieces):
    got, g = pieces
    _, R, W = got.shape
    tr = _pick(R, 256, 16)
    x, y, c = _coords()
    where = jnp.stack([2 * x + y, c, 4 * x + 2 * y + c]).astype(jnp.int32)

    def body(where_ref, got_ref, own_ref, o_ref):
        total = None
        for d in range(8):
            piece = jnp.where(where_ref[2] == d, own_ref[0, 0], got_ref[d]).astype(F32)
            total = piece if total is None else total + piece
        o_ref[...] = total

    return pl.pallas_call(
        body, name=name,
        grid_spec=pltpu.PrefetchScalarGridSpec(
            num_scalar_prefetch=1, grid=(R // tr,),
            in_specs=[pl.BlockSpec((8, tr, W), lambda i, w: (0, i, 0)),
                      pl.BlockSpec((1, 1, tr, W), lambda i, w: (w[0], w[1], i, 0))],
            out_specs=pl.BlockSpec((tr, W), lambda i, w: (i, 0))),
        out_shape=jax.ShapeDtypeStruct((R, W), F32), compiler_params=_params(),
    )(where, got, g)


def _sibling_exchange(gs):
    n = len(gs)

    def body(*refs):
        ins, outs, send_sems, recv_sems = refs[:n], refs[n:2 * n], refs[2 * n], refs[2 * n + 1]
        x, y, c = _coords()
        cps = []
        for w in range(n):
            cp = pltpu.make_async_remote_copy(
                src_ref=ins[w].at[:, 1 - c], dst_ref=outs[w], send_sem=send_sems.at[w], recv_sem=recv_sems.at[w],
                device_id=(x, y, 1 - c), device_id_type=MESH)
            cp.start()
            cps.append(cp)
        for cp in cps:
            cp.wait()

    return pl.pallas_call(
        body, name="grad_sibling_exchange", in_specs=[ANY] * n, out_specs=[ANY] * n,
        out_shape=[jax.ShapeDtypeStruct((4,) + g.shape[2:], F32) for g in gs],
        scratch_shapes=[pltpu.SemaphoreType.DMA((n,)), pltpu.SemaphoreType.DMA((n,))],
    )(*gs)


def _pair_sum(name, g, got, c_arr):
    _, _, R, W = g.shape
    tr = _pick(R, 256, 8)

    def body(c_ref, a_ref, b_ref, o_ref):
        o_ref[...] = a_ref[0] + b_ref[...]

    return pl.pallas_call(
        body, name=name,
        grid_spec=pltpu.PrefetchScalarGridSpec(
            num_scalar_prefetch=1, grid=(4, R // tr),
            in_specs=[pl.BlockSpec((1, 1, tr, W), lambda j, i, c_ref: (j, c_ref[0], i, 0)),
                      pl.BlockSpec((1, tr, W), lambda j, i, c_ref: (j, i, 0))],
            out_specs=pl.BlockSpec((1, tr, W), lambda j, i, c_ref: (j, i, 0))),
        out_shape=jax.ShapeDtypeStruct((4, R, W), F32), compiler_params=_params(),
    )(c_arr, g, got)


def _chip_exchange(parts):
    n = len(parts)

    def body(*refs):
        ins, outs, send_sems, recv_sems = refs[:n], refs[n:2 * n], refs[2 * n], refs[2 * n + 1]
        x, y, c = _coords()
        sends = []
        for w in range(n):
            for j, (cx, cy) in enumerate(_other_chips(x, y)):
                cp = pltpu.make_async_remote_copy(
                    src_ref=ins[w].at[2 * cx + cy], dst_ref=outs[w].at[j], send_sem=send_sems.at[3 * w + j],
                    recv_sem=recv_sems.at[3 * w + j], device_id=(cx, cy, c), device_id_type=MESH)
                cp.start()
                sends.append(cp)
        for cp in sends:
            cp.wait_recv()
        for cp in sends:
            cp.wait_send()

    return pl.pallas_call(
        body, name="grad_chip_exchange", in_specs=[ANY] * n, out_specs=[ANY] * n,
        out_shape=[jax.ShapeDtypeStruct((3,) + p.shape[1:], F32) for p in parts],
        scratch_shapes=[pltpu.SemaphoreType.DMA((3 * n,)), pltpu.SemaphoreType.DMA((3 * n,))],
    )(*parts)


def _chip_sum(name, part, got, slot_arr):
    _, R, W = part.shape
    tr = _pick(R, 256, 8)

    def body(s_ref, a_ref, b_ref, o_ref):
        o_ref[...] = ((a_ref[0] + b_ref[0]) + b_ref[1]) + b_ref[2]

    return pl.pallas_call(
        body, name=name,
        grid_spec=pltpu.PrefetchScalarGridSpec(
            num_scalar_prefetch=1, grid=(R // tr,),
            in_specs=[pl.BlockSpec((1, tr, W), lambda i, s_ref: (s_ref[0], i, 0)),
                      pl.BlockSpec((3, tr, W), lambda i, s_ref: (0, i, 0))],
            out_specs=pl.BlockSpec((tr, W), lambda i, s_ref: (i, 0))),
        out_shape=jax.ShapeDtypeStruct((R, W), F32), compiler_params=_params(),
    )(slot_arr, part, got)


def _half_exchange(halves):
    n = len(halves)

    def body(*refs):
        ins, outs, send_sems, recv_sems = refs[:n], refs[n:2 * n], refs[2 * n], refs[2 * n + 1]
        x, y, c = _coords()
        sends = []
        for w in range(n):
            cp = pltpu.make_async_remote_copy(
                src_ref=ins[w], dst_ref=outs[w], send_sem=send_sems.at[w], recv_sem=recv_sems.at[w],
                device_id=(x, y, 1 - c), device_id_type=MESH)
            cp.start()
            sends.append(cp)
        for cp in sends:
            cp.wait()

    got = pl.pallas_call(
        body, name="grad_half_exchange", in_specs=[ANY] * n, out_specs=[ANY] * n,
        out_shape=[jax.ShapeDtypeStruct(h.shape, F32) for h in halves],
        scratch_shapes=[pltpu.SemaphoreType.DMA((n,)), pltpu.SemaphoreType.DMA((n,))],
    )(*halves)
    c = lax.axis_index("c")
    return [jnp.where(c == 0, jnp.stack([mine, theirs]), jnp.stack([theirs, mine])) for mine, theirs in zip(halves, got)]


def _adamw_math(w, g, m, v):
    m2 = ADAM_B1 * m + (1.0 - ADAM_B1) * g
    v2 = ADAM_B2 * v + (1.0 - ADAM_B2) * (g * g)
    m_hat = m2 / (1.0 - ADAM_B1 ** ADAM_STEP)
    v_hat = v2 / (1.0 - ADAM_B2 ** ADAM_STEP)
    return -ADAM_LR * (m_hat / (jnp.sqrt(v_hat) + ADAM_EPS) + ADAM_WD * w), m2, v2


def _small_allreduce_adamw(pack_g, pack_w, pack_m, pack_v):
    def body(g_ref, w_ref, m_ref, v_ref, sum_ref, d_ref, m_out, v_out, land, send_sems, recv_sems):
        x, y, c = _coords()
        me = 4 * x + 2 * y + c
        land[me] = g_ref[...]
        sends = []
        for k in range(1, 8):
            peer = (x ^ (k >> 2), y ^ ((k >> 1) & 1), c ^ (k & 1))
            cp = pltpu.make_async_remote_copy(
                src_ref=g_ref, dst_ref=land.at[me], send_sem=send_sems.at[k - 1], recv_sem=recv_sems.at[k - 1],
                device_id=peer, device_id_type=MESH)
            cp.start()
            sends.append((cp, peer))
        for k, (cp, peer) in enumerate(sends):
            pltpu.make_async_remote_copy(
                src_ref=g_ref, dst_ref=land.at[4 * peer[0] + 2 * peer[1] + peer[2]], send_sem=send_sems.at[k],
                recv_sem=recv_sems.at[k], device_id=peer, device_id_type=MESH).wait_recv()
        for cp, _ in sends:
            cp.wait_send()
        total = land[0]
        for d in range(1, 8):
            total = total + land[d]
        sum_ref[...] = total
        d_ref[...], m_out[...], v_out[...] = _adamw_math(w_ref[...], total, m_ref[...], v_ref[...])

    vm = pl.BlockSpec(memory_space=pltpu.VMEM)
    shp = jax.ShapeDtypeStruct(pack_g.shape, F32)
    return pl.pallas_call(
        body, name="small_allreduce_adamw", in_specs=[vm] * 4, out_specs=[vm] * 4, out_shape=[shp] * 4,
        scratch_shapes=[pltpu.VMEM((8,) + pack_g.shape, F32), pltpu.SemaphoreType.DMA((7,)), pltpu.SemaphoreType.DMA((7,))],
    )(pack_g, pack_w, pack_m, pack_v)


def _adamw(name, w, g, m, v):
    R, C = w.shape
    tr = _pick(R, 256, 8)

    def body(w_ref, g_ref, m_ref, v_ref, d_out, m_out, v_out):
        d_out[...], m_out[...], v_out[...] = _adamw_math(w_ref[...], g_ref[...], m_ref[...], v_ref[...])

    spec = pl.BlockSpec((tr, C), lambda i: (i, 0))
    return pl.pallas_call(
        body, name=name, grid=(R // tr,), in_specs=[spec] * 4, out_specs=[spec] * 3,
        out_shape=[jax.ShapeDtypeStruct((R, C), F32)] * 3, compiler_params=_params(),
    )(w, g, m, v)


def _pack_small(vals, last):
    flat = jnp.concatenate([v.reshape(-1) for v in vals] + [last.reshape(-1)])
    return jnp.pad(flat, (0, SMALL_ROWS * LANES - flat.shape[0])).reshape(SMALL_ROWS, LANES)


def kernel(x, positions, g_mix, w_in, g_q_a, w_q_b, g_kv_a, w_kv_b, g_qn, g_kn, w_mla_out, ret_decay_fwd, ret_decay_bwd, w_ret_out, w_out, g_ffn, w_gate_up, w_down, loss_target, m_g_mix, m_w_in, m_g_q_a, m_w_q_b, m_g_kv_a, m_w_kv_b, m_g_qn, m_g_kn, m_w_mla_out, m_ret_decay_fwd, m_ret_decay_bwd, m_w_ret_out, m_w_out, m_g_ffn, m_w_gate_up, m_w_down, v_g_mix, v_w_in, v_g_q_a, v_w_q_b, v_g_kv_a, v_w_kv_b, v_g_qn, v_g_kn, v_w_mla_out, v_ret_decay_fwd, v_ret_decay_bwd, v_w_ret_out, v_w_out, v_g_ffn, v_w_gate_up, v_w_down):
    given = dict(locals())
    S = x.shape[1]
    xs, tgt = x.reshape(S, D_MODEL), loss_target.reshape(S, D_MODEL)
    pos = positions.reshape(S, 1).astype(F32)

    first_shards = [given[n].astype(BF16) for n in FIRST]
    my_chip = 2 * lax.axis_index("x") + lax.axis_index("y")
    wts = {n: _assemble(n, _fill_slot(g, s, my_chip))
           for n, g, s in zip(FIRST, _weight_gather_first(first_shards), first_shards)}
    late_shards = {n: given[n].astype(BF16) for n in LATE}
    small = {n: given[n].reshape(1, -1) for n in SMALL}

    loss_row, dx, pieces, sgrads = _local_step(xs, pos, tgt, wts, late_shards, small)

    halves = [_grad_sum8("grad_sum_" + n, got) for n, got in zip(FIRST + LATE, pieces)]
    reduced = _half_exchange(halves)

    out = {}
    for n, r in zip(FIRST + LATE, reduced):
        g = r.reshape(given[n].shape)
        out["grad_" + n] = g
        out["delta_" + n], out["new_m_" + n], out["new_v_" + n] = _adamw("adamw_" + n, given[n], g, given["m_" + n], given["v_" + n])

    one = jnp.ones((1,), F32)
    pk = _small_allreduce_adamw(
        _pack_small([sgrads[n] for n in SMALL], loss_row[0, :1]),
        _pack_small([given[n] for n in SMALL], 0 * one),
        _pack_small([given["m_" + n] for n in SMALL], 0 * one),
        _pack_small([given["v_" + n] for n in SMALL], one))
    off = 0
    for n in SMALL:
        sz = given[n].shape[0]
        for pre, arr in zip(["grad_", "delta_", "new_m_", "new_v_"], pk):
            out[pre + n] = arr.reshape(-1)[off:off + sz]
        off += sz
    loss = pk[0].reshape(-1)[off]

    return (loss, dx.reshape(x.shape), *[out["grad_" + n] for n in WEIGHTS], *[out["delta_" + n] for n in WEIGHTS],
            *[out["new_m_" + n] for n in WEIGHTS], *[out["new_v_" + n] for n in WEIGHTS])
```

```python
import functools
import math

import numpy as np
import jax
import jax.numpy as jnp
from jax import lax
from jax.experimental import pallas as pl
from jax.experimental.pallas import tpu as pltpu

F32 = jnp.float32
BF16 = jnp.bfloat16
MESH = pl.DeviceIdType.MESH

D_MODEL = 1024
HEADS = 8
LANES = 128
MLA_Q_RANK, MLA_KV_RANK = 256, 128
MLA_NOPE, MLA_ROPE, MLA_V = 64, 32, 64
MLA_QK = MLA_NOPE + MLA_ROPE
LN2 = math.log(2.0)
MLA_Q_SCALE = MLA_QK ** -0.5 / LN2
RET_QK, RET_V, RET_CHUNK = 64, 128, 128
RET_QK_DTYPE = BF16
RET_CHUNKS_PER_STEP = 2
FFN_HIDDEN = 2816
ROPE_THETA = 10000.0
EPS = 1e-6
IN_SPLITS = [256, 128, 32, 512, 512, 1024, 1024, 2048]
IN_OFFS = [0] + list(np.cumsum(IN_SPLITS))
ADAM_LR, ADAM_B1, ADAM_B2, ADAM_EPS, ADAM_WD, ADAM_STEP = 0.001, 0.9, 0.999, 1e-08, 0.01, 10

VMEM_LIMIT = 56 * 1024 * 1024
ROW_TILE = 256
HEAD_ROW_TILE = 2048
MM_TM, MM_TN, MM_TK, MM_KFULL = 1408, 2048, 2048, 2816
ATT_TQ = 256
ATT_BQ, ATT_BK = 1024, 1024
ATT_HEADS_PER_STEP = 8
ATT_BWD_HEADS_PER_STEP = 4

SHARDED = ["w_in", "w_q_b", "w_kv_b", "w_mla_out", "w_ret_out", "w_out", "w_gate_up", "w_down"]
COL_SHARDED = {"w_in", "w_q_b", "w_kv_b", "w_mla_out", "w_gate_up"}
FIRST = ["w_in", "w_q_b", "w_kv_b"]
LATE = ["w_mla_out", "w_ret_out", "w_out", "w_gate_up", "w_down"]
SMALL = ["g_mix", "g_q_a", "g_kv_a", "g_qn", "g_kn", "ret_decay_fwd", "ret_decay_bwd", "g_ffn"]
WEIGHTS = ["g_mix", "w_in", "g_q_a", "w_q_b", "g_kv_a", "w_kv_b", "g_qn", "g_kn", "w_mla_out",
           "ret_decay_fwd", "ret_decay_bwd", "w_ret_out", "w_out", "g_ffn", "w_gate_up", "w_down"]
SMALL_ROWS = 24


def _params(**kw):
    return pltpu.CompilerParams(vmem_limit_bytes=VMEM_LIMIT, **kw)


def _pick(dim, target, unit=128):
    if dim <= target:
        return dim
    best = None
    for d in range(unit, target + 1, unit):
        if dim % d == 0:
            best = d
    assert best is not None, (dim, target)
    return best


_DOT = {"nn": (((1,), (0,)), ((), ())), "nt": (((1,), (1,)), ((), ())), "tn": (((0,), (0,)), ((), ()))}


def _dot(a, b, mode="nn"):
    return lax.dot_general(a, b, _DOT[mode], preferred_element_type=F32)


def _rms_rows(x, g):
    x = x.astype(F32)
    return x * lax.rsqrt(jnp.mean(x * x, axis=-1, keepdims=True) + EPS) * g


def _epi_loss(acc, extras, params):
    e = acc + extras[0] - extras[1]
    dy = e * (1.0 / D_MODEL)
    loss = 0.5 * jnp.sum(jnp.mean(e * e, axis=-1, keepdims=True), axis=0, keepdims=True)
    return [dy, dy], [jnp.broadcast_to(loss, (1, LANES))]


def _epi_rms_bwd(n_out):
    def fn(acc, extras, params):
        _, vjp = jax.vjp(_rms_rows, extras[0], params[0])
        dx, dg = vjp(acc)
        return [dx + extras[1]] * n_out, [dg]
    return fn


def _mm(name, a, b, mode, out_dtype, res=None, a_gain=None, a_scale=None, epilogue=None, shard_out=False,
        scatter=None):
    if mode == "nn":
        (M, K), (K2, N) = a.shape, b.shape
    elif mode == "nt":
        (M, K), (N, K2) = a.shape, b.shape
    else:
        (K, M), (K2, N) = a.shape, b.shape
    assert K == K2, (name, a.shape, b.shape)
    tm, tn = _pick(M, MM_TM), _pick(N, MM_TN)
    tk = K if K <= MM_KFULL else _pick(K, MM_TK)
    if shard_out:
        tm, tn = M // 2, N // 4
    if a_scale is not None:
        assert mode == "tn", name
        tk = _pick(K, MM_TK // 2)
    if epilogue is not None:
        tm = _pick(M, MM_TM // 2)
    nk = K // tk
    cache_a = a_gain is not None
    if a_gain is not None:
        assert mode == "nn" and tk == K and epilogue is None and not shard_out, name
    n_in = 2 + (res is not None) + (a_gain is not None) + 2 * (a_scale is not None)
    extras, eparams, e_outs, e_sums = ([], [], [], [])
    if epilogue is not None:
        assert tn == N and res is None and not shard_out, name
        epi_fn, extras, eparams, e_outs, e_sums = epilogue
    n_out = len(e_outs) + len(e_sums) if epilogue is not None else 1 + cache_a
    scatter = list(scatter or [])
    n_sc = len(scatter)
    assert not n_sc or epilogue is not None, name
    ni, nj = M // tm, N // tn

    def body(*refs):
        a_ref, b_ref = refs[0], refs[1]
        base = n_in + len(extras) + len(eparams)
        ex_refs = refs[n_in:n_in + len(extras)]
        ep_refs = refs[n_in + len(extras):base]
        sc_in, out_refs = refs[base:base + n_sc], refs[base + n_sc:base + n_sc + n_out]
        sc_out = refs[base + n_sc + n_out:base + 2 * n_sc + n_out]
        scratch = refs[base + 2 * n_sc + n_out:]
        acc = scratch[0]
        i, j, k = pl.program_id(0), pl.program_id(1), pl.program_id(2)

        if n_sc:
            @pl.when(jnp.logical_and(i == 0, jnp.logical_and(j == 0, k == 0)))
            def _():
                _scatter_start(_scatter_copies(sc_in, sc_out, scratch[-2], scratch[-1]))

        @pl.when(k == 0)
        def _():
            acc[...] = jnp.zeros_like(acc)

        if cache_a:
            @pl.when(j == 0)
            def _():
                x = a_ref[...].astype(F32)
                rstd = lax.rsqrt(jnp.mean(x * x, axis=-1, keepdims=True) + EPS)
                scratch[1][...] = (x * rstd * refs[n_in - 1][...]).astype(BF16)
                out_refs[1][...] = rstd
            av = scratch[1][...]
        elif a_scale is not None:
            av = (a_ref[...].astype(F32) * refs[n_in - 2][...] * refs[n_in - 1][...]).astype(BF16)
        else:
            av = a_ref[...].astype(BF16)
        acc[...] += _dot(av, b_ref[...].astype(BF16), mode)

        @pl.when(k == nk - 1)
        def _():
            if epilogue is None:
                r = acc[...]
                if res is not None:
                    r = r + refs[2][...].astype(F32)
                out_refs[0][...] = r.astype(out_refs[0].dtype).reshape(out_refs[0].shape)
            else:
                vals, sums = epi_fn(acc[...], [r[...] for r in ex_refs], [p[...] for p in ep_refs])
                for o_ref, v in zip(out_refs, vals):
                    o_ref[...] = v.astype(o_ref.dtype)
                for s_ref, v in zip(out_refs[len(vals):], sums):
                    @pl.when(i == 0)
                    def _(s_ref=s_ref):
                        s_ref[...] = jnp.zeros_like(s_ref)
                    s_ref[...] += v

        if n_sc:
            @pl.when(jnp.logical_and(i == ni - 1, jnp.logical_and(j == nj - 1, k == nk - 1)))
            def _():
                _scatter_wait(_scatter_copies(sc_in, sc_out, scratch[-2], scratch[-1]))

    a_spec = pl.BlockSpec((tk, tm), lambda i, j, k: (k, i)) if mode == "tn" else pl.BlockSpec((tm, tk), lambda i, j, k: (i, k))
    b_spec = pl.BlockSpec((tn, tk), lambda i, j, k: (j, k)) if mode == "nt" else pl.BlockSpec((tk, tn), lambda i, j, k: (k, j))
    o_spec = pl.BlockSpec((tm, tn), lambda i, j, k: (i, j))
    const = lambda p: pl.BlockSpec(p.shape, lambda i, j, k: (0,) * p.ndim)
    ins, specs = [a, b], [a_spec, b_spec]
    if res is not None:
        ins.append(res)
        specs.append(o_spec)
    if a_gain is not None:
        ins.append(a_gain)
        specs.append(const(a_gain))
    if a_scale is not None:
        ins += list(a_scale)
        specs += [pl.BlockSpec((tk, 1), lambda i, j, k: (k, 0)), pl.BlockSpec((1, tm), lambda i, j, k: (0, i))]
    ins += list(extras) + list(eparams)
    specs += [o_spec] * len(extras) + [const(p) for p in eparams]
    if epilogue is not None:
        out_specs = [o_spec] * len(e_outs) + [pl.BlockSpec(s, lambda i, j, k: (0, 0)) for s in e_sums]
        out_shape = [jax.ShapeDtypeStruct((M, N), dt) for dt in e_outs] + [jax.ShapeDtypeStruct(s, F32) for s in e_sums]
    elif shard_out:
        out_specs = pl.BlockSpec((1, 1, tm, tn), lambda i, j, k: (j, i, 0, 0))
        out_shape = jax.ShapeDtypeStruct((4, 2, tm, tn), out_dtype)
    elif cache_a:
        out_specs = [o_spec, pl.BlockSpec((tm, 1), lambda i, j, k: (i, 0))]
        out_shape = [jax.ShapeDtypeStruct((M, N), out_dtype), jax.ShapeDtypeStruct((M, 1), F32)]
    else:
        out_specs, out_shape = o_spec, jax.ShapeDtypeStruct((M, N), out_dtype)
    scratch_shapes = [pltpu.VMEM((tm, tn), F32)] + ([pltpu.VMEM((tm, tk), BF16)] if cache_a else [])
    if n_sc:
        ins += scatter
        specs += [ANY] * n_sc
        out_specs = list(out_specs) + [ANY] * n_sc
        out_shape = list(out_shape) + [jax.ShapeDtypeStruct((8,) + g.shape[2:], g.dtype) for g in scatter]
        scratch_shapes += [pltpu.SemaphoreType.DMA((8 * n_sc,)), pltpu.SemaphoreType.DMA((7 * n_sc,))]
    res_ = pl.pallas_call(
        body, name=name, grid=(ni, nj, nk), in_specs=specs, out_specs=out_specs, out_shape=out_shape,
        scratch_shapes=scratch_shapes, compiler_params=_params(),
    )(*ins)
    if n_sc:
        return list(res_[:n_out]) + [list(res_[n_out:])]
    return res_


def _piece_spec(tm, piece):
    _, w, c0, per_group = piece
    if per_group:
        return pl.BlockSpec((tm, w), lambda i, g: (i, c0 + g))
    return pl.BlockSpec((tm, w), lambda i, g: (i, c0))


def _const_spec(p):
    return pl.BlockSpec(p.shape, lambda i, g: (0, 0))


def _rowwise(name, fn, params, rows, auxs, outs, tm, groups=1):
    S = rows[0][0].shape[0]
    tm = min(tm, S)
    n_p, n_r, n_a = len(params), len(rows), len(auxs)

    def body(*refs):
        p = [r[...] for r in refs[:n_p]]
        r_ = [r[...] for r in refs[n_p:n_p + n_r]]
        a_ = [r[...] for r in refs[n_p + n_r:n_p + n_r + n_a]]
        for o_ref, o in zip(refs[n_p + n_r + n_a:], fn(p, r_, a_)):
            o_ref[...] = o.astype(o_ref.dtype)

    out_specs, out_shape = [], []
    for w, dt, per_group in outs:
        out_specs.append(_piece_spec(tm, (None, w, 0, per_group)))
        out_shape.append(jax.ShapeDtypeStruct((S, w * (groups if per_group else 1)), dt))
    return pl.pallas_call(
        body, name=name, grid=(S // tm, groups),
        in_specs=[_const_spec(p) for p in params] + [_piece_spec(tm, q) for q in list(rows) + list(auxs)],
        out_specs=out_specs, out_shape=out_shape, compiler_params=_params(),
    )(*params, *[q[0] for q in list(rows) + list(auxs)])


def _rowwise_vjp(name, fn, params, rows, auxs, cots, d_outs, tm, groups=1, adds=None):
    S = rows[0][0].shape[0]
    tm = min(tm, S)
    n_p, n_r, n_a = len(params), len(rows), len(auxs)
    cot_flat = [q for c in cots for q in c]
    adds = adds or [None] * len(d_outs)
    add_flat = [q for q in adds if q is not None]
    n_c, n_add = len(cot_flat), len(add_flat)
    shared = [not all(rows[k][3] for k in idx) and groups > 1 for idx, _ in d_outs]

    def body(*refs):
        pos = 0
        p = [r[...] for r in refs[pos:pos + n_p]]; pos += n_p
        r_ = [r[...] for r in refs[pos:pos + n_r]]; pos += n_r
        a_ = [r[...] for r in refs[pos:pos + n_a]]; pos += n_a
        c_refs = refs[pos:pos + n_c]; pos += n_c
        add_refs = list(refs[pos:pos + n_add]); pos += n_add
        d_refs = refs[pos:pos + len(d_outs)]; pos += len(d_outs)
        dp_refs = refs[pos:]
        i, g = pl.program_id(0), pl.program_id(1)
        outs, vjp_fn = jax.vjp(lambda pp, rr: fn(pp, rr, a_), p, r_)
        cts, ci = [], 0
        for c, o in zip(cots, outs):
            t = c_refs[ci][...].astype(F32)
            for extra in c_refs[ci + 1:ci + len(c)]:
                t = t + extra[...].astype(F32)
            ci += len(c)
            cts.append(t.astype(o.dtype))
        dp, dr = vjp_fn(cts)
        for (idx, _), d_ref, add, sh in zip(d_outs, d_refs, adds, shared):
            val = dr[idx[0]].astype(F32) if len(idx) == 1 else jnp.concatenate([dr[k].astype(F32) for k in idx], axis=1)
            if add is not None:
                val = val + add_refs.pop(0)[...].astype(F32)
            if sh:
                @pl.when(g == 0)
                def _(d_ref=d_ref):
                    d_ref[...] = jnp.zeros_like(d_ref)
                d_ref[...] += val.astype(d_ref.dtype)
            else:
                d_ref[...] = val.astype(d_ref.dtype)
        first = jnp.logical_and(i == 0, g == 0)
        for dp_ref, d in zip(dp_refs, dp):
            @pl.when(first)
            def _(dp_ref=dp_ref):
                dp_ref[...] = jnp.zeros_like(dp_ref)
            dp_ref[...] += d.astype(F32)

    out_specs, out_shape = [], []
    for (idx, dt), sh in zip(d_outs, shared):
        w = sum(rows[k][1] for k in idx)
        per_group = (not sh) and groups > 1
        out_specs.append(_piece_spec(tm, (None, w, 0, per_group)))
        out_shape.append(jax.ShapeDtypeStruct((S, w * (groups if per_group else 1)), dt))
    for p in params:
        out_specs.append(_const_spec(p))
        out_shape.append(jax.ShapeDtypeStruct(p.shape, F32))
    pieces = list(rows) + list(auxs) + cot_flat + add_flat
    res = pl.pallas_call(
        body, name=name, grid=(S // tm, groups),
        in_specs=[_const_spec(p) for p in params] + [_piece_spec(tm, q) for q in pieces],
        out_specs=out_specs, out_shape=out_shape, compiler_params=_params(),
    )(*params, *[q[0] for q in pieces])
    return list(res[:len(d_outs)]), list(res[len(d_outs):])


def _lane_roll(x, shift):
    @jax.custom_vjp
    def roll(v):
        return pltpu.roll(v, shift, 1)

    roll.defvjp(lambda v: (roll(v), None), lambda _, ct: (pltpu.roll(ct, LANES - shift, 1),))
    return roll(x)


@jax.custom_vjp
def _sigmoid(x):
    return 1.0 / (1.0 + jnp.exp(-x))


def _sigmoid_fwd(x):
    s = _sigmoid(x)
    return s, s


_sigmoid.defvjp(_sigmoid_fwd, lambda s, ct: (ct * s * (1.0 - s),))


def _rope(x, cos, sin_lo, sin_hi, half):
    return x * cos + _lane_roll(x, LANES - half) * sin_lo + _lane_roll(x, half) * sin_hi


def _f_rope_table(p, r, a):
    inv, first, second, fixed = p
    ang = a[0] * inv
    cs, sn = jnp.cos(ang), jnp.sin(ang)
    return [cs * (first + second) + fixed, -sn * first, sn * second]


def _f_rms(p, r, a):
    x = r[0].astype(F32)
    return [x * lax.rsqrt(jnp.mean(x * x, axis=-1, keepdims=True) + EPS) * p[0]]


def _f_mla_a(p, r, a):
    return _f_rms([p[0]], [r[0]], a) + _f_rms([p[1]], [r[1]], a)


def _f_mla_b(p, r, a):
    def norm_rope(v, g):
        ms = jnp.sum(v * v, axis=-1, keepdims=True) * (1.0 / MLA_QK)
        return _rope(v * lax.rsqrt(ms + EPS) * g, a[0], a[1], a[2], MLA_ROPE // 2)

    return [norm_rope(r[0].astype(F32), p[0]) * MLA_Q_SCALE, norm_rope(r[1].astype(F32) + r[2].astype(F32), p[1])]


def _f_ret_rope(p, r, a):
    q = _rope(r[0].astype(F32), a[0], a[1], a[2], RET_QK // 2)
    k = _rope(r[1].astype(F32), a[0], a[1], a[2], RET_QK // 2)
    return [q, k * (RET_QK ** -0.5)]


def _f_ret_post(p, r, a):
    ret = r[0].astype(F32) + r[1].astype(F32)
    g = r[2].astype(F32)
    normed = ret * lax.rsqrt(jnp.mean(ret * ret, axis=-1, keepdims=True) + EPS)
    return [g * _sigmoid(g) * normed]


def _f_merge(p, r, a):
    return [_sigmoid(r[0].astype(F32)) * r[2].astype(F32) + _sigmoid(r[1].astype(F32)) * r[3].astype(F32)]


def _f_swiglu(p, r, a):
    g = r[0].astype(F32)
    return [g * _sigmoid(g) * r[1].astype(F32)]


def _f_delta(p, r, a):
    d = jnp.sum(r[0].astype(F32) * r[1].astype(F32), axis=-1, keepdims=True)
    return [jnp.broadcast_to(d, r[0].shape)]


def _f_add(p, r, a):
    return [r[0].astype(F32) + r[1].astype(F32)]


def _loss_kernel(y, tgt):
    S, Dm = y.shape
    tm = min(ROW_TILE, S)

    def body(y_ref, t_ref, dy_ref, loss_ref):
        @pl.when(pl.program_id(0) == 0)
        def _():
            loss_ref[...] = jnp.zeros_like(loss_ref)

        e = y_ref[...] - t_ref[...]
        dy_ref[...] = e * (1.0 / Dm)
        loss_ref[...] += 0.5 * jnp.sum(jnp.mean(e * e, axis=-1, keepdims=True), axis=0, keepdims=True)

    row = pl.BlockSpec((tm, Dm), lambda i: (i, 0))
    return pl.pallas_call(
        body, name="loss", grid=(S // tm,), in_specs=[row, row],
        out_specs=[row, pl.BlockSpec((1, LANES), lambda i: (0, 0))],
        out_shape=[jax.ShapeDtypeStruct((S, Dm), F32), jax.ShapeDtypeStruct((1, LANES), F32)],
        compiler_params=_params(),
    )(y, tgt)


def _flash_fwd(q, k, kv, shards):
    S = q.shape[0]
    tq = min(ATT_TQ, S)
    nq = S // tq
    n = len(shards)

    def body(q_ref, k_ref, v_ref, *rest):
        shard_refs, (o_ref, lse_ref), gathered = rest[:n], rest[n:n + 2], rest[n + 2:2 * n + 2]
        send_sems, recv_sems = rest[2 * n + 2:]
        h, qi = pl.program_id(0), pl.program_id(1)

        @pl.when(jnp.logical_and(h == 0, qi == 0))
        def _():
            _gather_start(_gather_copies(shard_refs, gathered, send_sems, recv_sems))

        for hh in range(hps):
            lanes = slice(hh * LANES, (hh + 1) * LANES)
            s = _dot(q_ref[:, lanes], k_ref[:, lanes], "nt")
            m = jnp.max(s, axis=-1, keepdims=True)
            p = jnp.exp2(s - m)
            l = jnp.sum(p, axis=-1, keepdims=True)
            o_ref[:, lanes] = (_dot(p.astype(BF16), v_ref[:, lanes]) / l).astype(o_ref.dtype)
            lse_ref[:, lanes] = jnp.broadcast_to(m + jnp.log2(l), (tq, LANES))

        @pl.when(jnp.logical_and(h == HEADS // hps - 1, qi == nq - 1))
        def _():
            _gather_wait(_gather_copies(shard_refs, gathered, send_sems, recv_sems))

    hps = ATT_HEADS_PER_STEP
    qs = pl.BlockSpec((tq, hps * LANES), lambda h, i: (i, h))
    res = pl.pallas_call(
        body, name="mla_fwd", grid=(HEADS // hps, nq),
        in_specs=[qs, pl.BlockSpec((S, hps * LANES), lambda h, i: (0, h), pipeline_mode=pl.Buffered(1)),
                  pl.BlockSpec((S, hps * LANES), lambda h, i: (0, HEADS // hps + h), pipeline_mode=pl.Buffered(1))]
        + [ANY] * n,
        out_specs=[qs, qs] + [ANY] * n,
        out_shape=[jax.ShapeDtypeStruct((S, HEADS * LANES), BF16), jax.ShapeDtypeStruct((S, HEADS * LANES), F32)]
        + [jax.ShapeDtypeStruct((4,) + s.shape, s.dtype) for s in shards],
        scratch_shapes=[pltpu.SemaphoreType.DMA((3 * n,)), pltpu.SemaphoreType.DMA((3 * n,))],
        compiler_params=_params(),
    )(q, k, kv, *shards)
    mine = 2 * lax.axis_index("x") + lax.axis_index("y")
    return res[0], res[1], [_fill_slot(g, s, mine) for g, s in zip(res[2:], shards)]


def _flash_bwd(q, k, kv, do, lse, o, gs):
    S = q.shape[0]
    tq, tk = min(ATT_BQ, S), min(ATT_BK, S)
    nq, nkt = S // tq, S // tk
    n = len(gs)

    def body(q_ref, k_ref, v_ref, do_ref, lse_ref, o_ref, *rest):
        g_refs, (dq_ref, dk_ref, dv_ref), got_refs = rest[:n], rest[n:n + 3], rest[n + 3:2 * n + 3]
        dk_sc, dv_sc, send_sems, recv_sems = rest[2 * n + 3:]
        h, ki, qi = pl.program_id(0), pl.program_id(1), pl.program_id(2)

        @pl.when(jnp.logical_and(h == 0, jnp.logical_and(ki == 0, qi == 0)))
        def _():
            _scatter_start(_scatter_copies(g_refs, got_refs, send_sems, recv_sems))

        @pl.when(jnp.logical_and(ki == 0, qi == 0))
        def _():
            dq_ref[...] = jnp.zeros_like(dq_ref)

        @pl.when(qi == 0)
        def _():
            dk_sc[...] = jnp.zeros_like(dk_sc)
            dv_sc[...] = jnp.zeros_like(dv_sc)

        rows = pl.ds(pl.multiple_of(qi * tq, tq), tq)
        for hh in range(hps):
            lanes = slice(hh * LANES, (hh + 1) * LANES)
            qv, kv_, dov = q_ref[:, lanes], k_ref[:, lanes], do_ref[:, lanes]
            p = jnp.exp2(_dot(qv, kv_, "nt") - lse_ref[:, lanes][:, :1])
            dp = _dot(dov, v_ref[:, lanes], "nt")
            delta = jnp.sum(dov.astype(F32) * o_ref[:, lanes].astype(F32), axis=-1, keepdims=True)
            ds = (p * (dp - delta) * LN2).astype(BF16)
            dv_sc[:, lanes] += _dot(p.astype(BF16), dov, "tn")
            dk_sc[:, lanes] += _dot(ds, qv, "tn")
            dq_ref[rows, lanes] += _dot(ds, kv_)

        @pl.when(qi == nq - 1)
        def _():
            dk_ref[...] = dk_sc[...].astype(dk_ref.dtype)
            dv_ref[...] = dv_sc[...].astype(dv_ref.dtype)

        @pl.when(jnp.logical_and(h == HEADS // hps - 1, jnp.logical_and(ki == nkt - 1, qi == nq - 1)))
        def _():
            _scatter_wait(_scatter_copies(g_refs, got_refs, send_sems, recv_sems))

    hps = ATT_BWD_HEADS_PER_STEP
    qs = pl.BlockSpec((tq, hps * LANES), lambda h, j, i: (i, h))
    ks = pl.BlockSpec((tk, hps * LANES), lambda h, j, i: (j, h))
    res = pl.pallas_call(
        body, name="mla_bwd", grid=(HEADS // hps, nkt, nq),
        in_specs=[qs, ks, pl.BlockSpec((tk, hps * LANES), lambda h, j, i: (j, HEADS // hps + h)), qs, qs, qs] + [ANY] * n,
        out_specs=[pl.BlockSpec((S, hps * LANES), lambda h, j, i: (0, h), pipeline_mode=pl.Buffered(1)), ks, ks] + [ANY] * n,
        out_shape=[jax.ShapeDtypeStruct((S, HEADS * LANES), F32), jax.ShapeDtypeStruct((S, HEADS * LANES), BF16),
                   jax.ShapeDtypeStruct((S, HEADS * LANES), BF16)]
        + [jax.ShapeDtypeStruct((8,) + g.shape[2:], g.dtype) for g in gs],
        scratch_shapes=[pltpu.VMEM((tk, hps * LANES), F32)] * 2
        + [pltpu.SemaphoreType.DMA((8 * n,)), pltpu.SemaphoreType.DMA((7 * n,))],
        compiler_params=_params(),
    )(q, k, kv, do, lse, o, *gs)
    return res[0], res[1], res[2], list(res[3:])


def _ret_tables(decay_row, backward):
    C = RET_CHUNK
    lg = -jnp.exp(decay_row)
    t = lax.broadcasted_iota(jnp.int32, (C, C), 0).astype(F32)
    s = lax.broadcasted_iota(jnp.int32, (C, C), 1).astype(F32)
    ridx = lax.broadcasted_iota(jnp.int32, (C, LANES), 0).astype(F32)
    if backward:
        dist, mask, aw, bw = s - t, s > t, C - ridx, ridx
    else:
        dist, mask, aw, bw = t - s, t >= s, ridx + 1.0, C - 1.0 - ridx
    dist = jnp.maximum(dist, 0.0)
    din = jnp.where(mask, jnp.exp(lg[:, :1] * dist), 0.0)
    return dict(din=din, dist=dist, a=jnp.exp(lg * aw), b=jnp.exp(lg * bw), c=jnp.exp(lg * C), aw=aw, bw=bw)


def _ret_fill_tables(decs, din_sc, a_sc, b_sc):
    for d, dec in enumerate(decs):
        for h in range(HEADS):
            tb = _ret_tables(dec[h:h + 1, :], d == 1)
            din_sc[d, h], a_sc[d, h], b_sc[d, h] = tb["din"], tb["a"], tb["b"]


def _ret_fwd(qr, kr, proj, v_block, dec_f, dec_b):
    S = qr.shape[0]
    C = RET_CHUNK
    n = S // C
    nc = min(RET_CHUNKS_PER_STEP, n)
    nb = n // nc
    W = HEADS * LANES

    def body(qf, kf, vf, qb, kb, vb, df, db, of, ob, sf_out, sb_out, st, din_sc, a_sc, b_sc):
        @pl.when(pl.program_id(0) == 0)
        def _():
            st[...] = jnp.zeros_like(st)
            _ret_fill_tables((df, db), din_sc, a_sc, b_sc)

        for d, (q_ref, k_ref, v_ref, dec, o_ref, s_out) in enumerate(
                [(qf, kf, vf, df, of, sf_out), (qb, kb, vb, db, ob, sb_out)]):
            for h in range(HEADS):
                lanes = slice(h * LANES, (h + 1) * LANES)
                din, a, b = din_sc[d, h], a_sc[d, h], b_sc[d, h]
                c = jnp.exp(-jnp.exp(dec[h:h + 1, :]) * C)
                for ci in (range(nc) if d == 0 else reversed(range(nc))):
                    rows = slice(ci * C, (ci + 1) * C)
                    qf32, kf32, v = q_ref[rows, lanes].astype(F32), k_ref[rows, lanes].astype(F32), v_ref[rows, lanes]
                    state = st[d, h]
                    s_out[ci, h] = state
                    inner = _dot((_dot(qf32.astype(BF16), kf32.astype(BF16), "nt") * din).astype(BF16), v)
                    cross = _dot((qf32 * a).astype(BF16), state.astype(BF16))
                    o_ref[rows, lanes] = inner + cross
                    st[d, h] = state * c + _dot((kf32 * b).astype(BF16), v, "tn")

    fw = lambda c0: pl.BlockSpec((nc * C, W), lambda j: (j, c0))
    bw = lambda c0: pl.BlockSpec((nc * C, W), lambda j: (nb - 1 - j, c0))
    dec_spec = pl.BlockSpec((HEADS, LANES), lambda j: (0, 0))
    st_shape = jax.ShapeDtypeStruct((n, HEADS, LANES, LANES), F32)
    return pl.pallas_call(
        body, name="ret_fwd", grid=(nb,),
        in_specs=[fw(0), fw(0), fw(v_block), bw(0), bw(0), bw(v_block), dec_spec, dec_spec],
        out_specs=[fw(0), bw(0), pl.BlockSpec((nc, HEADS, LANES, LANES), lambda j: (j, 0, 0, 0)),
                   pl.BlockSpec((nc, HEADS, LANES, LANES), lambda j: (nb - 1 - j, 0, 0, 0))],
        out_shape=[jax.ShapeDtypeStruct((S, W), F32)] * 2 + [st_shape] * 2,
        scratch_shapes=[pltpu.VMEM((2, HEADS, LANES, LANES), F32), pltpu.VMEM((2, HEADS, C, C), F32),
                        pltpu.VMEM((2, HEADS, C, LANES), F32), pltpu.VMEM((2, HEADS, C, LANES), F32)],
        compiler_params=_params(),
    )(qr, kr, proj, qr, kr, proj, dec_f, dec_b)


def _ret_bwd(qr, kr, proj, v_block, dret, sf, sb, dec_f, dec_b):
    S = qr.shape[0]
    C = RET_CHUNK
    n = S // C
    nc = min(RET_CHUNKS_PER_STEP, n)
    nb = n // nc
    W = HEADS * LANES

    def body(qf, kf, vf, gf, sf_ref, qb, kb, vb, gb, sb_ref, df, db,
             dqf, dkf, dvf, dqb, dkb, dvb, ddf, ddb, ds_sc, din_sc, a_sc, b_sc):
        j = pl.program_id(0)

        @pl.when(j == 0)
        def _():
            ds_sc[...] = jnp.zeros_like(ds_sc)
            ddf[...] = jnp.zeros_like(ddf)
            ddb[...] = jnp.zeros_like(ddb)
            _ret_fill_tables((df, db), din_sc, a_sc, b_sc)

        for d, (q_ref, k_ref, v_ref, g_ref, s_ref, dec, dq_ref, dk_ref, dv_ref, dd_ref) in enumerate(
                [(qf, kf, vf, gf, sf_ref, df, dqf, dkf, dvf, ddf), (qb, kb, vb, gb, sb_ref, db, dqb, dkb, dvb, ddb)]):
            static = _ret_tables(dec[0:1, :], d == 1)
            dist, aw, bw_ = static["dist"], static["aw"], static["bw"]
            for h in range(HEADS):
                lanes = slice(h * LANES, (h + 1) * LANES)
                din, a, b = din_sc[d, h], a_sc[d, h], b_sc[d, h]
                c = jnp.exp(-jnp.exp(dec[h:h + 1, :]) * C)
                dlg = jnp.zeros((1, 1), F32)
                for ci in (reversed(range(nc)) if d == 0 else range(nc)):
                    rows = slice(ci * C, (ci + 1) * C)
                    v, g = v_ref[rows, lanes], g_ref[rows, lanes]
                    qf32, kf32 = q_ref[rows, lanes].astype(F32), k_ref[rows, lanes].astype(F32)
                    q, k = qf32.astype(BF16), kf32.astype(BF16)
                    state, dstate = s_ref[ci, h], ds_sc[d, h]
                    dstate_b = dstate.astype(BF16)
                    dp = _dot(g, v, "nt")
                    a_ = _dot(q, k, "nt")
                    da = (dp * din).astype(BF16)
                    g1 = _dot(g, state.astype(BF16), "nt")
                    g2 = _dot(v, dstate_b, "nt")
                    dq_ref[rows, lanes] = (_dot(da, k) + g1 * a).astype(dq_ref.dtype)
                    dk_ref[rows, lanes] = (_dot(da, q, "tn") + g2 * b).astype(dk_ref.dtype)
                    dv_ref[rows, lanes] = (_dot((a_ * din).astype(BF16), g, "tn")
                                           + _dot((kf32 * b).astype(BF16), dstate_b)).astype(dv_ref.dtype)
                    dlg = dlg + (jnp.sum(dp * a_ * din * dist, keepdims=True)
                                 + jnp.sum(g1 * qf32 * a * aw, keepdims=True)
                                 + jnp.sum(g2 * kf32 * b * bw_, keepdims=True)
                                 + C * jnp.sum(c * dstate * state, keepdims=True))
                    ds_sc[d, h] = dstate * c + _dot((qf32 * a).astype(BF16), g, "tn")
                dd_ref[h:h + 1, :] += jnp.broadcast_to(dlg, (1, LANES))

        @pl.when(j == nb - 1)
        def _():
            ddf[...] = ddf[...] * -jnp.exp(df[...])
            ddb[...] = ddb[...] * -jnp.exp(db[...])

    fw = lambda c0: pl.BlockSpec((nc * C, W), lambda j: (nb - 1 - j, c0))
    bw = lambda c0: pl.BlockSpec((nc * C, W), lambda j: (j, c0))
    dec_spec = pl.BlockSpec((HEADS, LANES), lambda j: (0, 0))
    act = jax.ShapeDtypeStruct((S, W), BF16)
    return pl.pallas_call(
        body, name="ret_bwd", grid=(nb,),
        in_specs=[fw(0), fw(0), fw(v_block), fw(0), pl.BlockSpec((nc, HEADS, LANES, LANES), lambda j: (nb - 1 - j, 0, 0, 0)),
                  bw(0), bw(0), bw(v_block), bw(0), pl.BlockSpec((nc, HEADS, LANES, LANES), lambda j: (j, 0, 0, 0)),
                  dec_spec, dec_spec],
        out_specs=[fw(0)] * 3 + [bw(0)] * 3 + [dec_spec] * 2,
        out_shape=[act] * 6 + [jax.ShapeDtypeStruct((HEADS, LANES), F32)] * 2,
        scratch_shapes=[pltpu.VMEM((2, HEADS, LANES, LANES), F32), pltpu.VMEM((2, HEADS, C, C), F32),
                        pltpu.VMEM((2, HEADS, C, LANES), F32), pltpu.VMEM((2, HEADS, C, LANES), F32)],
        compiler_params=_params(),
    )(qr, kr, proj, dret, sf, qr, kr, proj, dret, sb, dec_f, dec_b)


def _pad_heads(w, hd):
    K = w.shape[0]
    return jnp.pad(w.reshape(K, HEADS, hd), ((0, 0), (0, 0), (0, LANES - hd))).reshape(K, HEADS * LANES)


def _unpad_heads(w, hd):
    K = w.shape[0]
    return w.reshape(K, HEADS, LANES)[:, :, :hd].reshape(K, HEADS * hd)


def _rope_consts(first_lane, half):
    lane = np.arange(LANES)
    first = ((lane >= first_lane) & (lane < first_lane + half)).astype(np.float32)
    second = ((lane >= first_lane + half) & (lane < first_lane + 2 * half)).astype(np.float32)
    fixed = (lane < first_lane).astype(np.float32)
    j = np.where(first > 0, lane - first_lane, lane - first_lane - half) * (first + second)
    inv = (ROPE_THETA ** (-j.astype(np.float64) / half)).astype(np.float32)
    return [jnp.asarray(v.reshape(1, LANES), F32) for v in (inv, first, second, fixed)]


def _assemble(name, gathered):
    if name in COL_SHARDED:
        return jnp.transpose(gathered, (1, 0, 2)).reshape(gathered.shape[1], 4 * gathered.shape[2])
    return gathered.reshape(4 * gathered.shape[1], gathered.shape[2])


def _split_for_reducers(name, g, dtype):
    if name in COL_SHARDED:
        K, N4 = g.shape
        return jnp.transpose(g.reshape(2, K // 2, 4, N4 // 4), (2, 0, 1, 3)).astype(dtype)
    return g.reshape(4, 2, g.shape[0] // 8, g.shape[1]).astype(dtype)


def _local_step(x, pos, tgt, wts, late_shards, small):
    w_in = wts["w_in"]
    seg = [w_in[:, IN_OFFS[i]:IN_OFFS[i + 1]] for i in range(8)]
    kr_w = jnp.pad(seg[2], ((0, 0), (MLA_NOPE, LANES - MLA_QK)))
    w_in_p = jnp.concatenate([seg[7], seg[5], seg[6], _pad_heads(seg[3], RET_QK), _pad_heads(seg[4], RET_QK),
                              seg[0], seg[1], kr_w], axis=1)
    w_qb_p = _pad_heads(wts["w_q_b"], MLA_QK)
    kvw = wts["w_kv_b"].reshape(MLA_KV_RANK, HEADS, MLA_NOPE + MLA_V)
    pad_kv = lambda t: jnp.pad(t, ((0, 0), (0, 0), (0, LANES - t.shape[2]))).reshape(MLA_KV_RANK, HEADS * LANES)
    w_kn_p, w_v_p = pad_kv(kvw[:, :, :MLA_NOPE]), pad_kv(kvw[:, :, MLA_NOPE:])
    w_kv_p = jnp.concatenate([w_kn_p, w_v_p], axis=1)
    g_qn_p = jnp.pad(small["g_qn"], ((0, 0), (0, LANES - MLA_QK)))
    g_kn_p = jnp.pad(small["g_kn"], ((0, 0), (0, LANES - MLA_QK)))
    dec_f = jnp.broadcast_to(small["ret_decay_fwd"].reshape(HEADS, 1), (HEADS, LANES))
    dec_b = jnp.broadcast_to(small["ret_decay_bwd"].reshape(HEADS, 1), (HEADS, LANES))
    T, N = True, False
    RT, HT = ROW_TILE, HEAD_ROW_TILE

    tab_m = _rowwise("rope_table_mla", _f_rope_table, _rope_consts(MLA_NOPE, MLA_ROPE // 2), [(pos, 1, 0, N)], [(pos, 1, 0, N)],
                     [(LANES, F32, N)] * 3, HT)
    tab_r = _rowwise("rope_table_ret", _f_rope_table, _rope_consts(0, RET_QK // 2), [(pos, 1, 0, N)], [(pos, 1, 0, N)],
                     [(LANES, F32, N)] * 3, HT)
    aux_m = [(t, LANES, 0, N) for t in tab_m]
    aux_r = [(t, LANES, 0, N) for t in tab_r]

    proj, rstd1 = _mm("proj", x, w_in_p, "nn", BF16, a_gain=small["g_mix"])
    rows_a = [(proj, MLA_Q_RANK, 24, N), (proj, MLA_KV_RANK, 50, N)]
    cqn, ckvn = _rowwise("mla_lat_norm", _f_mla_a, [small["g_q_a"], small["g_kv_a"]], rows_a, [],
                         [(MLA_Q_RANK, BF16, N), (MLA_KV_RANK, BF16, N)], RT)
    qraw = _mm("mla_q_up", cqn, w_qb_p, "nn", BF16)
    kv = _mm("mla_kv_up", ckvn, w_kv_p, "nn", BF16)
    rows_b = [(qraw, LANES, 0, T), (kv, LANES, 0, T), (proj, LANES, 51, N)]
    q, k = _rowwise("mla_qk_norm_rope", _f_mla_b, [g_qn_p, g_kn_p], rows_b, aux_m, [(LANES, BF16, T)] * 2, HT, HEADS)
    o, lse, late = _flash_fwd(q, k, kv, [late_shards[n] for n in LATE])
    wl = {n: _assemble(n, g) for n, g in zip(LATE, late)}
    w_mla_p = jnp.pad(wl["w_mla_out"].reshape(HEADS, MLA_V, D_MODEL), ((0, 0), (0, LANES - MLA_V), (0, 0))).reshape(HEADS * LANES, D_MODEL)
    w_ret_out, w_out, w_gu, w_down = wl["w_ret_out"], wl["w_out"], wl["w_gate_up"], wl["w_down"]
    y_a = _mm("mla_out", o, w_mla_p, "nn", F32)
    rows_rr = [(proj, LANES, 32, T), (proj, LANES, 40, T)]
    qr, kr = _rowwise("ret_rope", _f_ret_rope, [], rows_rr, aux_r, [(LANES, RET_QK_DTYPE, T)] * 2, HT, HEADS)
    ret_f, ret_b, st_f, st_b = _ret_fwd(qr, kr, proj, 2, dec_f, dec_b)
    rows_rp = [(ret_f, LANES, 0, T), (ret_b, LANES, 0, T), (proj, LANES, 24, T)]
    (o_b,) = _rowwise("ret_post", _f_ret_post, [], rows_rp, [], [(LANES, BF16, T)], HT, HEADS)
    y_b = _mm("ret_out", o_b, w_ret_out, "nn", F32)
    rows_m = [(proj, D_MODEL, 0, N), (proj, D_MODEL, 1, N), (y_a, D_MODEL, 0, N), (y_b, D_MODEL, 0, N)]
    (merged,) = _rowwise("merge", _f_merge, [], rows_m, [], [(D_MODEL, BF16, N)], RT)
    x2 = _mm("mix_out", merged, w_out, "nn", F32, res=x)
    gu, rstd2 = _mm("ffn_gate_up", x2, w_gu, "nn", BF16, a_gain=small["g_ffn"])
    rows_sw = [(gu, FFN_HIDDEN, 0, N), (gu, FFN_HIDDEN, 1, N)]
    (act,) = _rowwise("swiglu", _f_swiglu, [], rows_sw, [], [(FFN_HIDDEN, BF16, N)], RT)
    dy, dy_b16, loss_row = _mm("ffn_down", act, w_down, "nn", None,
                               epilogue=(_epi_loss, [x2, tgt], [], [F32, BF16], [(1, LANES)]))

    dact = _mm("d_act", dy_b16, w_down, "nt", BF16)
    dw_down = _mm("dw_down", act, dy_b16, "tn", BF16)
    (dgu,), _ = _rowwise_vjp("swiglu_bwd", _f_swiglu, [], rows_sw, [], [[(dact, FFN_HIDDEN, 0, N)]], [([0, 1], BF16)], RT)
    dx2, dx2_b16, dg_ffn = _mm("d_h2", dgu, w_gu, "nt", None,
                               epilogue=(_epi_rms_bwd(2), [x2, dy], [small["g_ffn"]], [F32, BF16], [(1, D_MODEL)]))
    dw_gu = _mm("dw_gate_up", x2, dgu, "tn", BF16, a_scale=(rstd2, small["g_ffn"]), shard_out=True)
    dmerged = _mm("d_merged", dx2_b16, w_out, "nt", BF16)
    dw_out = _mm("dw_out", merged, dx2_b16, "tn", BF16)
    (dgl, dy_a, dy_b), _ = _rowwise_vjp("merge_bwd", _f_merge, [], rows_m, [], [[(dmerged, D_MODEL, 0, N)]],
                                        [([0, 1], BF16), ([2], BF16), ([3], BF16)], RT)
    do_b = _mm("d_ret_o", dy_b, w_ret_out, "nt", BF16)
    dw_ret_out = _mm("dw_ret_out", o_b, dy_b, "tn", BF16)
    (dret, dg_r), _ = _rowwise_vjp("ret_post_bwd", _f_ret_post, [], rows_rp, [], [[(do_b, LANES, 0, T)]],
                                   [([0], BF16), ([2], BF16)], HT, HEADS)
    dqf, dkf, dvf, dqb, dkb, dvb, ddec_f, ddec_b = _ret_bwd(qr, kr, proj, 2, dret, st_f, st_b, dec_f, dec_b)
    (dq_r, dk_r), _ = _rowwise_vjp("ret_rope_bwd", _f_ret_rope, [], rows_rr, aux_r,
                                   [[(dqf, LANES, 0, T), (dqb, LANES, 0, T)], [(dkf, LANES, 0, T), (dkb, LANES, 0, T)]],
                                   [([0], BF16), ([1], BF16)], HT, HEADS)
    (dv_r,) = _rowwise("ret_dv_sum", _f_add, [], [(dvf, D_MODEL, 0, N), (dvb, D_MODEL, 0, N)], [], [(D_MODEL, BF16, N)], RT)
    do = _mm("d_mla_o", dy_a, w_mla_p, "nt", BF16)
    dw_mla_p = _mm("dw_mla_out", o, dy_a, "tn", BF16)
    dw_mla = dw_mla_p.reshape(HEADS, LANES, D_MODEL)[:, :MLA_V].reshape(HEADS * MLA_V, D_MODEL)
    late_grads = {"w_mla_out": dw_mla, "w_ret_out": dw_ret_out, "w_out": dw_out, "w_down": dw_down}
    late_gs = [dw_gu if n == "w_gate_up" else _split_for_reducers(n, late_grads[n], BF16) for n in LATE]
    dq, dk, dv, late_got = _flash_bwd(q, k, kv, do, lse, o, late_gs)
    (dqraw, dkn, dkr), (dg_qn_p, dg_kn_p) = _rowwise_vjp(
        "mla_qk_norm_rope_bwd", _f_mla_b, [g_qn_p, g_kn_p], rows_b, aux_m, [[(dq, LANES, 0, T)], [(dk, LANES, 0, T)]],
        [([0], BF16), ([1], BF16), ([2], F32)], HT, HEADS)
    dckvn = _mm("d_ckvn_v", dv, w_v_p, "nt", BF16, res=_mm("d_ckvn_k", dkn, w_kn_p, "nt", F32))
    dw_kn_p = _mm("dw_kv_k", ckvn, dkn, "tn", BF16)
    dw_v_p = _mm("dw_kv_v", ckvn, dv, "tn", BF16)
    dcqn = _mm("d_cqn", dqraw, w_qb_p, "nt", BF16)
    dw_qb_p = _mm("dw_q_b", cqn, dqraw, "tn", BF16)
    (dcq, dckv), (dg_q_a, dg_kv_a) = _rowwise_vjp(
        "mla_lat_norm_bwd", _f_mla_a, [small["g_q_a"], small["g_kv_a"]], rows_a, [],
        [[(dcqn, MLA_Q_RANK, 0, N)], [(dckvn, MLA_KV_RANK, 0, N)]], [([0], BF16), ([1], BF16)], RT)
    dproj = jnp.concatenate([dgl, dv_r, dg_r, dq_r, dk_r, dcq, dckv, dkr.astype(BF16)], axis=1)
    dw_in_p = _mm("dw_in", x, dproj, "tn", BF16, a_scale=(rstd1, small["g_mix"]))

    c = lambda a, b_: dw_in_p[:, a:b_]
    dw_in = jnp.concatenate([c(6144, 6400), c(6400, 6528), c(6528 + MLA_NOPE, 6528 + MLA_QK), _unpad_heads(c(4096, 5120), RET_QK),
                             _unpad_heads(c(5120, 6144), RET_QK), c(2048, 3072), c(3072, 4096), c(0, 2048)], axis=1)
    un_kv = lambda t: t.reshape(MLA_KV_RANK, HEADS, LANES)[:, :, :MLA_NOPE]
    dw_kv = jnp.concatenate([un_kv(dw_kn_p), un_kv(dw_v_p)], axis=2).reshape(MLA_KV_RANK, HEADS * (MLA_NOPE + MLA_V))
    grads = {"w_in": dw_in, "w_q_b": _unpad_heads(dw_qb_p, MLA_QK), "w_kv_b": dw_kv}
    dx, dg_mix, first_got = _mm("d_h", dproj, w_in_p, "nt", None,
                                epilogue=(_epi_rms_bwd(1), [x, dx2], [small["g_mix"]], [F32], [(1, D_MODEL)]),
                                scatter=[_split_for_reducers(n, grads[n], BF16) for n in FIRST])
    sgrads = {"g_mix": dg_mix, "g_q_a": dg_q_a, "g_kv_a": dg_kv_a, "g_qn": dg_qn_p[:, :MLA_QK], "g_kn": dg_kn_p[:, :MLA_QK],
              "ret_decay_fwd": ddec_f[:, 0].reshape(1, HEADS), "ret_decay_bwd": ddec_b[:, 0].reshape(1, HEADS), "g_ffn": dg_ffn}
    return loss_row, dx, first_got + late_got, sgrads


def _coords():
    return lax.axis_index("x"), lax.axis_index("y"), lax.axis_index("c")


def _other_chips(x, y):
    return [(1 - x, y), (x, 1 - y), (1 - x, 1 - y)]


ANY = pl.BlockSpec(memory_space=pl.ANY)


def _gather_copies(ins, outs, send_sems, recv_sems):
    x, y, c = _coords()
    mine = 2 * x + y
    sends, arrivals = [], []
    for w in range(len(ins)):
        for j, (cx, cy) in enumerate(_other_chips(x, y)):
            sems = dict(send_sem=send_sems.at[3 * w + j], recv_sem=recv_sems.at[3 * w + j],
                        device_id=(cx, cy, c), device_id_type=MESH)
            sends.append(pltpu.make_async_remote_copy(src_ref=ins[w], dst_ref=outs[w].at[mine], **sems))
            arrivals.append(functools.partial(pltpu.make_async_remote_copy, src_ref=ins[w],
                                              dst_ref=outs[w].at[2 * cx + cy], **sems))
    return sends, arrivals


def _gather_start(copies):
    for cp in list(copies[0]) + list(copies[2] if len(copies) > 2 else []):
        cp.start()


def _gather_wait(copies):
    sends, arrivals = copies[:2]
    for make in arrivals:
        make().wait_recv()
    for cp in sends:
        cp.wait_send()
    for cp in (copies[2] if len(copies) > 2 else []):
        cp.wait()


def _fill_slot(buf, piece, slot):
    idx = lax.broadcasted_iota(jnp.int32, (buf.shape[0],) + (1,) * piece.ndim, 0)
    return jnp.where(idx == slot, piece[None], buf)


def _weight_gather_first(shards):
    n = len(shards)

    def body(*refs):
        ins, outs = refs[:n], refs[n:2 * n]
        send_sems, recv_sems = refs[2 * n:]
        x, y, c = _coords()
        chips = _other_chips(x, y)
        mine = 2 * x + y

        def half(ref, slot, core):
            rows = ref.shape[1] // 2
            return ref.at[slot, pl.ds(pl.multiple_of(core * rows, 8), rows)]

        def copy(w, k, slot, core, to, src=None):
            return pltpu.make_async_remote_copy(
                src_ref=half(outs[w], slot, core) if src is None else src, dst_ref=half(outs[w], slot, core),
                send_sem=send_sems.at[6 * w + k], recv_sem=recv_sems.at[6 * w + k], device_id=to, device_id_type=MESH)

        first, passed = [], []
        for w in range(n):
            rows = ins[w].shape[0] // 2
            my_half = ins[w].at[pl.ds(pl.multiple_of(c * rows, 8), rows)]
            for j, (cx, cy) in enumerate(chips):
                cp = copy(w, j, mine, c, (cx, cy, c), src=my_half)
                cp.start()
                first.append(cp)
        for w in range(n):
            for j, (cx, cy) in enumerate(chips):
                copy(w, j, 2 * cx + cy, c, (x, y, c)).wait_recv()
                cp = copy(w, 3 + j, 2 * cx + cy, c, (x, y, 1 - c))
                cp.start()
                passed.append(cp)
        for w in range(n):
            for j, (cx, cy) in enumerate(chips):
                copy(w, 3 + j, 2 * cx + cy, 1 - c, (x, y, c)).wait_recv()
        for cp in first + passed:
            cp.wait_send()

    return pl.pallas_call(
        body, name="weight_gather_first", in_specs=[ANY] * n, out_specs=[ANY] * n,
        out_shape=[jax.ShapeDtypeStruct((4,) + s.shape, s.dtype) for s in shards],
        scratch_shapes=[pltpu.SemaphoreType.DMA((6 * n,)), pltpu.SemaphoreType.DMA((6 * n,))],
    )(*shards)


def _scatter_copies(ins, outs, send_sems, recv_sems):
    x, y, c = _coords()
    me = 4 * x + 2 * y + c
    n = len(ins)
    sends, arrivals = [], []
    local = [pltpu.make_async_copy(ins[w].at[2 * x + y, c], outs[w].at[me], send_sems.at[7 * n + w]) for w in range(n)]
    for w in range(n):
        for k in range(1, 8):
            px, py, pc = x ^ (k >> 2), y ^ ((k >> 1) & 1), c ^ (k & 1)
            sems = dict(send_sem=send_sems.at[7 * w + k - 1], recv_sem=recv_sems.at[7 * w + k - 1],
                        device_id=(px, py, pc), device_id_type=MESH)
            sends.append(pltpu.make_async_remote_copy(src_ref=ins[w].at[2 * px + py, pc], dst_ref=outs[w].at[me], **sems))
            arrivals.append(functools.partial(
                pltpu.make_async_remote_copy, src_ref=ins[w].at[2 * px + py, pc],
                dst_ref=outs[w].at[4 * px + 2 * py + pc], **sems))
    return sends, arrivals, local


_scatter_start, _scatter_wait = _gather_start, _gather_wait


def _grad_scatter_late(gs):
    n = len(gs)

    def body(*refs):
        copies = _scatter_copies(refs[:n], refs[n:2 * n], *refs[2 * n:])
        _scatter_start(copies)
        _scatter_wait(copies)

    return pl.pallas_call(
        body, name="grad_scatter_late", in_specs=[ANY] * n, out_specs=[ANY] * n,
        out_shape=[jax.ShapeDtypeStruct((8,) + g.shape[2:], g.dtype) for g in gs],
        scratch_shapes=[pltpu.SemaphoreType.DMA((7 * n,)), pltpu.SemaphoreType.DMA((7 * n,))],
    )(*gs)


def _grad_sum8(name, got):
    _, R, W = got.shape
    tr = _pick(R, 256, 16)

    def body(g_ref, o_ref):
        total = g_ref[0].astype(F32)
        for d in range(1, 8):
            total = total + g_ref[d].astype(F32)
        o_ref[...] = total

    return pl.pallas_call(
        body, name=name, grid=(R // tr,), in_specs=[pl.BlockSpec((8, tr, W), lambda i: (0, i, 0))],
        out_specs=pl.BlockSpec((tr, W), lambda i: (i, 0)), out_shape=jax.ShapeDtypeStruct((R, W), F32),
        compiler_params=_params(),
    )(got)


def _sibling_exchange(gs):
    n = len(gs)

    def body(*refs):
        ins, outs, send_sems, recv_sems = refs[:n], refs[n:2 * n], refs[2 * n], refs[2 * n + 1]
        x, y, c = _coords()
        cps = []
        for w in range(n):
            cp = pltpu.make_async_remote_copy(
                src_ref=ins[w].at[:, 1 - c], dst_ref=outs[w], send_sem=send_sems.at[w], recv_sem=recv_sems.at[w],
                device_id=(x, y, 1 - c), device_id_type=MESH)
            cp.start()
            cps.append(cp)
        for cp in cps:
            cp.wait()

    return pl.pallas_call(
        body, name="grad_sibling_exchange", in_specs=[ANY] * n, out_specs=[ANY] * n,
        out_shape=[jax.ShapeDtypeStruct((4,) + g.shape[2:], F32) for g in gs],
        scratch_shapes=[pltpu.SemaphoreType.DMA((n,)), pltpu.SemaphoreType.DMA((n,))],
    )(*gs)


def _pair_sum(name, g, got, c_arr):
    _, _, R, W = g.shape
    tr = _pick(R, 256, 8)

    def body(c_ref, a_ref, b_ref, o_ref):
        o_ref[...] = a_ref[0] + b_ref[...]

    return pl.pallas_call(
        body, name=name,
        grid_spec=pltpu.PrefetchScalarGridSpec(
            num_scalar_prefetch=1, grid=(4, R // tr),
            in_specs=[pl.BlockSpec((1, 1, tr, W), lambda j, i, c_ref: (j, c_ref[0], i, 0)),
                      pl.BlockSpec((1, tr, W), lambda j, i, c_ref: (j, i, 0))],
            out_specs=pl.BlockSpec((1, tr, W), lambda j, i, c_ref: (j, i, 0))),
        out_shape=jax.ShapeDtypeStruct((4, R, W), F32), compiler_params=_params(),
    )(c_arr, g, got)


def _chip_exchange(parts):
    n = len(parts)

    def body(*refs):
        ins, outs, send_sems, recv_sems = refs[:n], refs[n:2 * n], refs[2 * n], refs[2 * n + 1]
        x, y, c = _coords()
        sends = []
        for w in range(n):
            for j, (cx, cy) in enumerate(_other_chips(x, y)):
                cp = pltpu.make_async_remote_copy(
                    src_ref=ins[w].at[2 * cx + cy], dst_ref=outs[w].at[j], send_sem=send_sems.at[3 * w + j],
                    recv_sem=recv_sems.at[3 * w + j], device_id=(cx, cy, c), device_id_type=MESH)
                cp.start()
                sends.append(cp)
        for cp in sends:
            cp.wait_recv()
        for cp in sends:
            cp.wait_send()

    return pl.pallas_call(
        body, name="grad_chip_exchange", in_specs=[ANY] * n, out_specs=[ANY] * n,
        out_shape=[jax.ShapeDtypeStruct((3,) + p.shape[1:], F32) for p in parts],
        scratch_shapes=[pltpu.SemaphoreType.DMA((3 * n,)), pltpu.SemaphoreType.DMA((3 * n,))],
    )(*parts)


def _chip_sum(name, part, got, slot_arr):
    _, R, W = part.shape
    tr = _pick(R, 256, 8)

    def body(s_ref, a_ref, b_ref, o_ref):
        o_ref[...] = ((a_ref[0] + b_ref[0]) + b_ref[1]) + b_ref[2]

    return pl.pallas_call(
        body, name=name,
        grid_spec=pltpu.PrefetchScalarGridSpec(
            num_scalar_prefetch=1, grid=(R // tr,),
            in_specs=[pl.BlockSpec((1, tr, W), lambda i, s_ref: (s_ref[0], i, 0)),
                      pl.BlockSpec((3, tr, W), lambda i, s_ref: (0, i, 0))],
            out_specs=pl.BlockSpec((tr, W), lambda i, s_ref: (i, 0))),
        out_shape=jax.ShapeDtypeStruct((R, W), F32), compiler_params=_params(),
    )(slot_arr, part, got)


def _half_exchange(halves):
    n = len(halves)

    def body(*refs):
        ins, outs, send_sems, recv_sems = refs[:n], refs[n:2 * n], refs[2 * n], refs[2 * n + 1]
        x, y, c = _coords()
        sends = []
        for w in range(n):
            cp = pltpu.make_async_remote_copy(
                src_ref=ins[w], dst_ref=outs[w], send_sem=send_sems.at[w], recv_sem=recv_sems.at[w],
                device_id=(x, y, 1 - c), device_id_type=MESH)
            cp.start()
            sends.append(cp)
        for cp in sends:
            cp.wait()

    got = pl.pallas_call(
        body, name="grad_half_exchange", in_specs=[ANY] * n, out_specs=[ANY] * n,
        out_shape=[jax.ShapeDtypeStruct(h.shape, F32) for h in halves],
        scratch_shapes=[pltpu.SemaphoreType.DMA((n,)), pltpu.SemaphoreType.DMA((n,))],
    )(*halves)
    c = lax.axis_index("c")
    return [jnp.where(c == 0, jnp.stack([mine, theirs]), jnp.stack([theirs, mine])) for mine, theirs in zip(halves, got)]


def _adamw_math(w, g, m, v):
    m2 = ADAM_B1 * m + (1.0 - ADAM_B1) * g
    v2 = ADAM_B2 * v + (1.0 - ADAM_B2) * (g * g)
    m_hat = m2 / (1.0 - ADAM_B1 ** ADAM_STEP)
    v_hat = v2 / (1.0 - ADAM_B2 ** ADAM_STEP)
    return -ADAM_LR * (m_hat / (jnp.sqrt(v_hat) + ADAM_EPS) + ADAM_WD * w), m2, v2


def _small_allreduce_adamw(pack_g, pack_w, pack_m, pack_v):
    def body(g_ref, w_ref, m_ref, v_ref, sum_ref, d_ref, m_out, v_out, land, send_sems, recv_sems):
        x, y, c = _coords()
        me = 4 * x + 2 * y + c
        land[me] = g_ref[...]
        sends = []
        for k in range(1, 8):
            peer = (x ^ (k >> 2), y ^ ((k >> 1) & 1), c ^ (k & 1))
            cp = pltpu.make_async_remote_copy(
                src_ref=g_ref, dst_ref=land.at[me], send_sem=send_sems.at[k - 1], recv_sem=recv_sems.at[k - 1],
                device_id=peer, device_id_type=MESH)
            cp.start()
            sends.append((cp, peer))
        for k, (cp, peer) in enumerate(sends):
            pltpu.make_async_remote_copy(
                src_ref=g_ref, dst_ref=land.at[4 * peer[0] + 2 * peer[1] + peer[2]], send_sem=send_sems.at[k],
                recv_sem=recv_sems.at[k], device_id=peer, device_id_type=MESH).wait_recv()
        for cp, _ in sends:
            cp.wait_send()
        total = land[0]
        for d in range(1, 8):
            total = total + land[d]
        sum_ref[...] = total
        d_ref[...], m_out[...], v_out[...] = _adamw_math(w_ref[...], total, m_ref[...], v_ref[...])

    vm = pl.BlockSpec(memory_space=pltpu.VMEM)
    shp = jax.ShapeDtypeStruct(pack_g.shape, F32)
    return pl.pallas_call(
        body, name="small_allreduce_adamw", in_specs=[vm] * 4, out_specs=[vm] * 4, out_shape=[shp] * 4,
        scratch_shapes=[pltpu.VMEM((8,) + pack_g.shape, F32), pltpu.SemaphoreType.DMA((7,)), pltpu.SemaphoreType.DMA((7,))],
    )(pack_g, pack_w, pack_m, pack_v)


def _adamw(name, w, g, m, v):
    R, C = w.shape
    tr = _pick(R, 256, 8)

    def body(w_ref, g_ref, m_ref, v_ref, d_out, m_out, v_out):
        d_out[...], m_out[...], v_out[...] = _adamw_math(w_ref[...], g_ref[...], m_ref[...], v_ref[...])

    spec = pl.BlockSpec((tr, C), lambda i: (i, 0))
    return pl.pallas_call(
        body, name=name, grid=(R // tr,), in_specs=[spec] * 4, out_specs=[spec] * 3,
        out_shape=[jax.ShapeDtypeStruct((R, C), F32)] * 3, compiler_params=_params(),
    )(w, g, m, v)


def _pack_small(vals, last):
    flat = jnp.concatenate([v.reshape(-1) for v in vals] + [last.reshape(-1)])
    return jnp.pad(flat, (0, SMALL_ROWS * LANES - flat.shape[0])).reshape(SMALL_ROWS, LANES)


def kernel(x, positions, g_mix, w_in, g_q_a, w_q_b, g_kv_a, w_kv_b, g_qn, g_kn, w_mla_out, ret_decay_fwd, ret_decay_bwd, w_ret_out, w_out, g_ffn, w_gate_up, w_down, loss_target, m_g_mix, m_w_in, m_g_q_a, m_w_q_b, m_g_kv_a, m_w_kv_b, m_g_qn, m_g_kn, m_w_mla_out, m_ret_decay_fwd, m_ret_decay_bwd, m_w_ret_out, m_w_out, m_g_ffn, m_w_gate_up, m_w_down, v_g_mix, v_w_in, v_g_q_a, v_w_q_b, v_g_kv_a, v_w_kv_b, v_g_qn, v_g_kn, v_w_mla_out, v_ret_decay_fwd, v_ret_decay_bwd, v_w_ret_out, v_w_out, v_g_ffn, v_w_gate_up, v_w_down):
    given = dict(locals())
    S = x.shape[1]
    xs, tgt = x.reshape(S, D_MODEL), loss_target.reshape(S, D_MODEL)
    pos = positions.reshape(S, 1).astype(F32)

    first_shards = [given[n].astype(BF16) for n in FIRST]
    my_chip = 2 * lax.axis_index("x") + lax.axis_index("y")
    wts = {n: _assemble(n, _fill_slot(g, s, my_chip))
           for n, g, s in zip(FIRST, _weight_gather_first(first_shards), first_shards)}
    late_shards = {n: given[n].astype(BF16) for n in LATE}
    small = {n: given[n].reshape(1, -1) for n in SMALL}

    loss_row, dx, pieces, sgrads = _local_step(xs, pos, tgt, wts, late_shards, small)

    halves = [_grad_sum8("grad_sum_" + n, got) for n, got in zip(FIRST + LATE, pieces)]
    reduced = _half_exchange(halves)

    out = {}
    for n, r in zip(FIRST + LATE, reduced):
        g = r.reshape(given[n].shape)
        out["grad_" + n] = g
        out["delta_" + n], out["new_m_" + n], out["new_v_" + n] = _adamw("adamw_" + n, given[n], g, given["m_" + n], given["v_" + n])

    one = jnp.ones((1,), F32)
    pk = _small_allreduce_adamw(
        _pack_small([sgrads[n] for n in SMALL], loss_row[0, :1]),
        _pack_small([given[n] for n in SMALL], 0 * one),
        _pack_small([given["m_" + n] for n in SMALL], 0 * one),
        _pack_small([given["v_" + n] for n in SMALL], one))
    off = 0
    for n in SMALL:
        sz = given[n].shape[0]
        for pre, arr in zip(["grad_", "delta_", "new_m_", "new_v_"], pk):
            out[pre + n] = arr.reshape(-1)[off:off + sz]
        off += sz
    loss = pk[0].reshape(-1)[off]

    return (loss, dx.reshape(x.shape), *[out["grad_" + n] for n in WEIGHTS], *[out["delta_" + n] for n in WEIGHTS],
            *[out["new_m_" + n] for n in WEIGHTS], *[out["new_v_" + n] for n in WEIGHTS])
```

```python
import functools
import math

import numpy as np
import jax
import jax.numpy as jnp
from jax import lax
from jax.experimental import pallas as pl
from jax.experimental.pallas import tpu as pltpu

F32 = jnp.float32
BF16 = jnp.bfloat16
MESH = pl.DeviceIdType.MESH

D_MODEL = 1024
HEADS = 8
LANES = 128
MLA_Q_RANK, MLA_KV_RANK = 256, 128
MLA_NOPE, MLA_ROPE, MLA_V = 64, 32, 64
MLA_QK = MLA_NOPE + MLA_ROPE
LN2 = math.log(2.0)
MLA_Q_SCALE = MLA_QK ** -0.5 / LN2
RET_QK, RET_V, RET_CHUNK = 64, 128, 128
RET_QK_DTYPE = BF16
RET_CHUNKS_PER_STEP = 2
FFN_HIDDEN = 2816
ROPE_THETA = 10000.0
EPS = 1e-6
IN_SPLITS = [256, 128, 32, 512, 512, 1024, 1024, 2048]
IN_OFFS = [0] + list(np.cumsum(IN_SPLITS))
ADAM_LR, ADAM_B1, ADAM_B2, ADAM_EPS, ADAM_WD, ADAM_STEP = 0.001, 0.9, 0.999, 1e-08, 0.01, 10

VMEM_LIMIT = 56 * 1024 * 1024
ROW_TILE = 256
HEAD_ROW_TILE = 2048
MM_TM, MM_TN, MM_TK, MM_KFULL = 1408, 2048, 2048, 2816
ATT_TQ = 256
ATT_BQ, ATT_BK = 1024, 1024
ATT_HEADS_PER_STEP = 8
ATT_BWD_HEADS_PER_STEP = 4

SHARDED = ["w_in", "w_q_b", "w_kv_b", "w_mla_out", "w_ret_out", "w_out", "w_gate_up", "w_down"]
COL_SHARDED = {"w_in", "w_q_b", "w_kv_b", "w_mla_out", "w_gate_up"}
FIRST = ["w_in", "w_q_b", "w_kv_b"]
LATE = ["w_mla_out", "w_ret_out", "w_out", "w_gate_up", "w_down"]
SMALL = ["g_mix", "g_q_a", "g_kv_a", "g_qn", "g_kn", "ret_decay_fwd", "ret_decay_bwd", "g_ffn"]
WEIGHTS = ["g_mix", "w_in", "g_q_a", "w_q_b", "g_kv_a", "w_kv_b", "g_qn", "g_kn", "w_mla_out",
           "ret_decay_fwd", "ret_decay_bwd", "w_ret_out", "w_out", "g_ffn", "w_gate_up", "w_down"]
SMALL_ROWS = 24


def _params(**kw):
    return pltpu.CompilerParams(vmem_limit_bytes=VMEM_LIMIT, **kw)


def _pick(dim, target, unit=128):
    if dim <= target:
        return dim
    best = None
    for d in range(unit, target + 1, unit):
        if dim % d == 0:
            best = d
    assert best is not None, (dim, target)
    return best


_DOT = {"nn": (((1,), (0,)), ((), ())), "nt": (((1,), (1,)), ((), ())), "tn": (((0,), (0,)), ((), ()))}


def _dot(a, b, mode="nn"):
    return lax.dot_general(a, b, _DOT[mode], preferred_element_type=F32)


def _rms_rows(x, g):
    x = x.astype(F32)
    return x * lax.rsqrt(jnp.mean(x * x, axis=-1, keepdims=True) + EPS) * g


def _epi_loss(acc, extras, params):
    e = acc + extras[0] - extras[1]
    dy = e * (1.0 / D_MODEL)
    loss = 0.5 * jnp.sum(jnp.mean(e * e, axis=-1, keepdims=True), axis=0, keepdims=True)
    return [dy, dy], [jnp.broadcast_to(loss, (1, LANES))]


def _epi_rms_bwd(n_out):
    def fn(acc, extras, params):
        _, vjp = jax.vjp(_rms_rows, extras[0], params[0])
        dx, dg = vjp(acc)
        return [dx + extras[1]] * n_out, [dg]
    return fn


def _mm(name, a, b, mode, out_dtype, res=None, a_gain=None, a_scale=None, epilogue=None, shard_out=False,
        scatter=None):
    if mode == "nn":
        (M, K), (K2, N) = a.shape, b.shape
    elif mode == "nt":
        (M, K), (N, K2) = a.shape, b.shape
    else:
        (K, M), (K2, N) = a.shape, b.shape
    assert K == K2, (name, a.shape, b.shape)
    tm, tn = _pick(M, MM_TM), _pick(N, MM_TN)
    tk = K if K <= MM_KFULL else _pick(K, MM_TK)
    if shard_out:
        tm, tn = M // 2, N // 4
    if a_scale is not None:
        assert mode == "tn", name
        tk = _pick(K, MM_TK // 2)
    if epilogue is not None:
        tm = _pick(M, MM_TM // 2)
    nk = K // tk
    cache_a = a_gain is not None
    if a_gain is not None:
        assert mode == "nn" and tk == K and epilogue is None and not shard_out, name
    n_in = 2 + (res is not None) + (a_gain is not None) + 2 * (a_scale is not None)
    extras, eparams, e_outs, e_sums = ([], [], [], [])
    if epilogue is not None:
        assert tn == N and res is None and not shard_out, name
        epi_fn, extras, eparams, e_outs, e_sums = epilogue
    n_out = len(e_outs) + len(e_sums) if epilogue is not None else 1 + cache_a
    scatter = list(scatter or [])
    n_sc = len(scatter)
    assert not n_sc or epilogue is not None, name
    ni, nj = M // tm, N // tn

    def body(*refs):
        a_ref, b_ref = refs[0], refs[1]
        base = n_in + len(extras) + len(eparams)
        ex_refs = refs[n_in:n_in + len(extras)]
        ep_refs = refs[n_in + len(extras):base]
        sc_in, out_refs = refs[base:base + n_sc], refs[base + n_sc:base + n_sc + n_out]
        sc_out = refs[base + n_sc + n_out:base + 2 * n_sc + n_out]
        scratch = refs[base + 2 * n_sc + n_out:]
        acc = scratch[0]
        i, j, k = pl.program_id(0), pl.program_id(1), pl.program_id(2)

        if n_sc:
            @pl.when(jnp.logical_and(i == 0, jnp.logical_and(j == 0, k == 0)))
            def _():
                _scatter_start(_scatter_copies(sc_in, sc_out, scratch[-2], scratch[-1]))

        @pl.when(k == 0)
        def _():
            acc[...] = jnp.zeros_like(acc)

        if cache_a:
            @pl.when(j == 0)
            def _():
                x = a_ref[...].astype(F32)
                rstd = lax.rsqrt(jnp.mean(x * x, axis=-1, keepdims=True) + EPS)
                scratch[1][...] = (x * rstd * refs[n_in - 1][...]).astype(BF16)
                out_refs[1][...] = rstd
            av = scratch[1][...]
        elif a_scale is not None:
            av = (a_ref[...].astype(F32) * refs[n_in - 2][...] * refs[n_in - 1][...]).astype(BF16)
        else:
            av = a_ref[...].astype(BF16)
        acc[...] += _dot(av, b_ref[...].astype(BF16), mode)

        @pl.when(k == nk - 1)
        def _():
            if epilogue is None:
                r = acc[...]
                if res is not None:
                    r = r + refs[2][...].astype(F32)
                out_refs[0][...] = r.astype(out_refs[0].dtype).reshape(out_refs[0].shape)
            else:
                vals, sums = epi_fn(acc[...], [r[...] for r in ex_refs], [p[...] for p in ep_refs])
                for o_ref, v in zip(out_refs, vals):
                    o_ref[...] = v.astype(o_ref.dtype)
                for s_ref, v in zip(out_refs[len(vals):], sums):
                    @pl.when(i == 0)
                    def _(s_ref=s_ref):
                        s_ref[...] = jnp.zeros_like(s_ref)
                    s_ref[...] += v

        if n_sc:
            @pl.when(jnp.logical_and(i == ni - 1, jnp.logical_and(j == nj - 1, k == nk - 1)))
            def _():
                _scatter_wait(_scatter_copies(sc_in, sc_out, scratch[-2], scratch[-1]))

    a_spec = pl.BlockSpec((tk, tm), lambda i, j, k: (k, i)) if mode == "tn" else pl.BlockSpec((tm, tk), lambda i, j, k: (i, k))
    b_spec = pl.BlockSpec((tn, tk), lambda i, j, k: (j, k)) if mode == "nt" else pl.BlockSpec((tk, tn), lambda i, j, k: (k, j))
    o_spec = pl.BlockSpec((tm, tn), lambda i, j, k: (i, j))
    const = lambda p: pl.BlockSpec(p.shape, lambda i, j, k: (0,) * p.ndim)
    ins, specs = [a, b], [a_spec, b_spec]
    if res is not None:
        ins.append(res)
        specs.append(o_spec)
    if a_gain is not None:
        ins.append(a_gain)
        specs.append(const(a_gain))
    if a_scale is not None:
        ins += list(a_scale)
        specs += [pl.BlockSpec((tk, 1), lambda i, j, k: (k, 0)), pl.BlockSpec((1, tm), lambda i, j, k: (0, i))]
    ins += list(extras) + list(eparams)
    specs += [o_spec] * len(extras) + [const(p) for p in eparams]
    if epilogue is not None:
        out_specs = [o_spec] * len(e_outs) + [pl.BlockSpec(s, lambda i, j, k: (0, 0)) for s in e_sums]
        out_shape = [jax.ShapeDtypeStruct((M, N), dt) for dt in e_outs] + [jax.ShapeDtypeStruct(s, F32) for s in e_sums]
    elif shard_out:
        out_specs = pl.BlockSpec((1, 1, tm, tn), lambda i, j, k: (j, i, 0, 0))
        out_shape = jax.ShapeDtypeStruct((4, 2, tm, tn), out_dtype)
    elif cache_a:
        out_specs = [o_spec, pl.BlockSpec((tm, 1), lambda i, j, k: (i, 0))]
        out_shape = [jax.ShapeDtypeStruct((M, N), out_dtype), jax.ShapeDtypeStruct((M, 1), F32)]
    else:
        out_specs, out_shape = o_spec, jax.ShapeDtypeStruct((M, N), out_dtype)
    scratch_shapes = [pltpu.VMEM((tm, tn), F32)] + ([pltpu.VMEM((tm, tk), BF16)] if cache_a else [])
    if n_sc:
        ins += scatter
        specs += [ANY] * n_sc
        out_specs = list(out_specs) + [ANY] * n_sc
        out_shape = list(out_shape) + [jax.ShapeDtypeStruct((8,) + g.shape[2:], g.dtype) for g in scatter]
        scratch_shapes += [pltpu.SemaphoreType.DMA((8 * n_sc,)), pltpu.SemaphoreType.DMA((7 * n_sc,))]
    res_ = pl.pallas_call(
        body, name=name, grid=(ni, nj, nk), in_specs=specs, out_specs=out_specs, out_shape=out_shape,
        scratch_shapes=scratch_shapes, compiler_params=_params(),
    )(*ins)
    if n_sc:
        return list(res_[:n_out]) + [list(res_[n_out:])]
    return res_


def _piece_spec(tm, piece):
    _, w, c0, per_group = piece
    if per_group:
        return pl.BlockSpec((tm, w), lambda i, g: (i, c0 + g))
    return pl.BlockSpec((tm, w), lambda i, g: (i, c0))


def _const_spec(p):
    return pl.BlockSpec(p.shape, lambda i, g: (0, 0))


def _rowwise(name, fn, params, rows, auxs, outs, tm, groups=1):
    S = rows[0][0].shape[0]
    tm = min(tm, S)
    n_p, n_r, n_a = len(params), len(rows), len(auxs)

    def body(*refs):
        p = [r[...] for r in refs[:n_p]]
        r_ = [r[...] for r in refs[n_p:n_p + n_r]]
        a_ = [r[...] for r in refs[n_p + n_r:n_p + n_r + n_a]]
        for o_ref, o in zip(refs[n_p + n_r + n_a:], fn(p, r_, a_)):
            o_ref[...] = o.astype(o_ref.dtype)

    out_specs, out_shape = [], []
    for w, dt, per_group in outs:
        out_specs.append(_piece_spec(tm, (None, w, 0, per_group)))
        out_shape.append(jax.ShapeDtypeStruct((S, w * (groups if per_group else 1)), dt))
    return pl.pallas_call(
        body, name=name, grid=(S // tm, groups),
        in_specs=[_const_spec(p) for p in params] + [_piece_spec(tm, q) for q in list(rows) + list(auxs)],
        out_specs=out_specs, out_shape=out_shape, compiler_params=_params(),
    )(*params, *[q[0] for q in list(rows) + list(auxs)])


def _rowwise_vjp(name, fn, params, rows, auxs, cots, d_outs, tm, groups=1, adds=None):
    S = rows[0][0].shape[0]
    tm = min(tm, S)
    n_p, n_r, n_a = len(params), len(rows), len(auxs)
    cot_flat = [q for c in cots for q in c]
    adds = adds or [None] * len(d_outs)
    add_flat = [q for q in adds if q is not None]
    n_c, n_add = len(cot_flat), len(add_flat)
    shared = [not all(rows[k][3] for k in idx) and groups > 1 for idx, _ in d_outs]

    def body(*refs):
        pos = 0
        p = [r[...] for r in refs[pos:pos + n_p]]; pos += n_p
        r_ = [r[...] for r in refs[pos:pos + n_r]]; pos += n_r
        a_ = [r[...] for r in refs[pos:pos + n_a]]; pos += n_a
        c_refs = refs[pos:pos + n_c]; pos += n_c
        add_refs = list(refs[pos:pos + n_add]); pos += n_add
        d_refs = refs[pos:pos + len(d_outs)]; pos += len(d_outs)
        dp_refs = refs[pos:]
        i, g = pl.program_id(0), pl.program_id(1)
        outs, vjp_fn = jax.vjp(lambda pp, rr: fn(pp, rr, a_), p, r_)
        cts, ci = [], 0
        for c, o in zip(cots, outs):
            t = c_refs[ci][...].astype(F32)
            for extra in c_refs[ci + 1:ci + len(c)]:
                t = t + extra[...].astype(F32)
            ci += len(c)
            cts.append(t.astype(o.dtype))
        dp, dr = vjp_fn(cts)
        for (idx, _), d_ref, add, sh in zip(d_outs, d_refs, adds, shared):
            val = dr[idx[0]].astype(F32) if len(idx) == 1 else jnp.concatenate([dr[k].astype(F32) for k in idx], axis=1)
            if add is not None:
                val = val + add_refs.pop(0)[...].astype(F32)
            if sh:
                @pl.when(g == 0)
                def _(d_ref=d_ref):
                    d_ref[...] = jnp.zeros_like(d_ref)
                d_ref[...] += val.astype(d_ref.dtype)
            else:
                d_ref[...] = val.astype(d_ref.dtype)
        first = jnp.logical_and(i == 0, g == 0)
        for dp_ref, d in zip(dp_refs, dp):
            @pl.when(first)
            def _(dp_ref=dp_ref):
                dp_ref[...] = jnp.zeros_like(dp_ref)
            dp_ref[...] += d.astype(F32)

    out_specs, out_shape = [], []
    for (idx, dt), sh in zip(d_outs, shared):
        w = sum(rows[k][1] for k in idx)
        per_group = (not sh) and groups > 1
        out_specs.append(_piece_spec(tm, (None, w, 0, per_group)))
        out_shape.append(jax.ShapeDtypeStruct((S, w * (groups if per_group else 1)), dt))
    for p in params:
        out_specs.append(_const_spec(p))
        out_shape.append(jax.ShapeDtypeStruct(p.shape, F32))
    pieces = list(rows) + list(auxs) + cot_flat + add_flat
    res = pl.pallas_call(
        body, name=name, grid=(S // tm, groups),
        in_specs=[_const_spec(p) for p in params] + [_piece_spec(tm, q) for q in pieces],
        out_specs=out_specs, out_shape=out_shape, compiler_params=_params(),
    )(*params, *[q[0] for q in pieces])
    return list(res[:len(d_outs)]), list(res[len(d_outs):])


def _lane_roll(x, shift):
    @jax.custom_vjp
    def roll(v):
        return pltpu.roll(v, shift, 1)

    roll.defvjp(lambda v: (roll(v), None), lambda _, ct: (pltpu.roll(ct, LANES - shift, 1),))
    return roll(x)


@jax.custom_vjp
def _sigmoid(x):
    return 1.0 / (1.0 + jnp.exp(-x))


def _sigmoid_fwd(x):
    s = _sigmoid(x)
    return s, s


_sigmoid.defvjp(_sigmoid_fwd, lambda s, ct: (ct * s * (1.0 - s),))


def _rope(x, cos, sin_lo, sin_hi, half):
    return x * cos + _lane_roll(x, LANES - half) * sin_lo + _lane_roll(x, half) * sin_hi


def _f_rope_table(p, r, a):
    inv, first, second, fixed = p
    ang = a[0] * inv
    cs, sn = jnp.cos(ang), jnp.sin(ang)
    return [cs * (first + second) + fixed, -sn * first, sn * second]


def _f_rms(p, r, a):
    x = r[0].astype(F32)
    return [x * lax.rsqrt(jnp.mean(x * x, axis=-1, keepdims=True) + EPS) * p[0]]


def _f_mla_a(p, r, a):
    return _f_rms([p[0]], [r[0]], a) + _f_rms([p[1]], [r[1]], a)


def _f_mla_b(p, r, a):
    def norm_rope(v, g):
        ms = jnp.sum(v * v, axis=-1, keepdims=True) * (1.0 / MLA_QK)
        return _rope(v * lax.rsqrt(ms + EPS) * g, a[0], a[1], a[2], MLA_ROPE // 2)

    return [norm_rope(r[0].astype(F32), p[0]) * MLA_Q_SCALE, norm_rope(r[1].astype(F32) + r[2].astype(F32), p[1])]


def _f_ret_rope(p, r, a):
    q = _rope(r[0].astype(F32), a[0], a[1], a[2], RET_QK // 2)
    k = _rope(r[1].astype(F32), a[0], a[1], a[2], RET_QK // 2)
    return [q, k * (RET_QK ** -0.5)]


def _f_ret_post(p, r, a):
    ret = r[0].astype(F32) + r[1].astype(F32)
    g = r[2].astype(F32)
    normed = ret * lax.rsqrt(jnp.mean(ret * ret, axis=-1, keepdims=True) + EPS)
    return [g * _sigmoid(g) * normed]


def _f_merge(p, r, a):
    return [_sigmoid(r[0].astype(F32)) * r[2].astype(F32) + _sigmoid(r[1].astype(F32)) * r[3].astype(F32)]


def _f_swiglu(p, r, a):
    g = r[0].astype(F32)
    return [g * _sigmoid(g) * r[1].astype(F32)]


def _f_add(p, r, a):
    return [r[0].astype(F32) + r[1].astype(F32)]


def _flash_fwd(q, k, kv, shards):
    S = q.shape[0]
    tq = min(ATT_TQ, S)
    nq = S // tq
    n = len(shards)

    def body(q_ref, k_ref, v_ref, *rest):
        shard_refs, (o_ref, lse_ref), gathered = rest[:n], rest[n:n + 2], rest[n + 2:2 * n + 2]
        send_sems, recv_sems = rest[2 * n + 2:]
        h, qi = pl.program_id(0), pl.program_id(1)

        @pl.when(jnp.logical_and(h == 0, qi == 0))
        def _():
            _gather_start(_gather_copies(shard_refs, gathered, send_sems, recv_sems))

        for hh in range(hps):
            lanes = slice(hh * LANES, (hh + 1) * LANES)
            s = _dot(q_ref[:, lanes], k_ref[:, lanes], "nt")
            m = jnp.max(s, axis=-1, keepdims=True)
            p = jnp.exp2(s - m)
            l = jnp.sum(p, axis=-1, keepdims=True)
            o_ref[:, lanes] = (_dot(p.astype(BF16), v_ref[:, lanes]) / l).astype(o_ref.dtype)
            lse_ref[:, lanes] = jnp.broadcast_to(m + jnp.log2(l), (tq, LANES))

        @pl.when(jnp.logical_and(h == HEADS // hps - 1, qi == nq - 1))
        def _():
            _gather_wait(_gather_copies(shard_refs, gathered, send_sems, recv_sems))

    hps = ATT_HEADS_PER_STEP
    qs = pl.BlockSpec((tq, hps * LANES), lambda h, i: (i, h))
    res = pl.pallas_call(
        body, name="mla_fwd", grid=(HEADS // hps, nq),
        in_specs=[qs, pl.BlockSpec((S, hps * LANES), lambda h, i: (0, h), pipeline_mode=pl.Buffered(1)),
                  pl.BlockSpec((S, hps * LANES), lambda h, i: (0, HEADS // hps + h), pipeline_mode=pl.Buffered(1))]
        + [ANY] * n,
        out_specs=[qs, qs] + [ANY] * n,
        out_shape=[jax.ShapeDtypeStruct((S, HEADS * LANES), BF16), jax.ShapeDtypeStruct((S, HEADS * LANES), F32)]
        + [jax.ShapeDtypeStruct((4,) + s.shape, s.dtype) for s in shards],
        scratch_shapes=[pltpu.SemaphoreType.DMA((3 * n,)), pltpu.SemaphoreType.DMA((3 * n,))],
        compiler_params=_params(),
    )(q, k, kv, *shards)
    mine = 2 * lax.axis_index("x") + lax.axis_index("y")
    return res[0], res[1], [_fill_slot(g, s, mine) for g, s in zip(res[2:], shards)]


def _flash_bwd(q, k, kv, do, lse, o, gs):
    S = q.shape[0]
    tq, tk = min(ATT_BQ, S), min(ATT_BK, S)
    nq, nkt = S // tq, S // tk
    n = len(gs)

    def body(q_ref, k_ref, v_ref, do_ref, lse_ref, o_ref, *rest):
        g_refs, (dq_ref, dk_ref, dv_ref), got_refs = rest[:n], rest[n:n + 3], rest[n + 3:2 * n + 3]
        dk_sc, dv_sc, send_sems, recv_sems = rest[2 * n + 3:]
        h, ki, qi = pl.program_id(0), pl.program_id(1), pl.program_id(2)

        @pl.when(jnp.logical_and(h == 0, jnp.logical_and(ki == 0, qi == 0)))
        def _():
            _scatter_start(_scatter_copies(g_refs, got_refs, send_sems, recv_sems))

        @pl.when(jnp.logical_and(ki == 0, qi == 0))
        def _():
            dq_ref[...] = jnp.zeros_like(dq_ref)

        @pl.when(qi == 0)
        def _():
            dk_sc[...] = jnp.zeros_like(dk_sc)
            dv_sc[...] = jnp.zeros_like(dv_sc)

        rows = pl.ds(pl.multiple_of(qi * tq, tq), tq)
        for hh in range(hps):
            lanes = slice(hh * LANES, (hh + 1) * LANES)
            qv, kv_, dov = q_ref[:, lanes], k_ref[:, lanes], do_ref[:, lanes]
            p = jnp.exp2(_dot(qv, kv_, "nt") - lse_ref[:, lanes][:, :1])
            dp = _dot(dov, v_ref[:, lanes], "nt")
            delta = jnp.sum(dov.astype(F32) * o_ref[:, lanes].astype(F32), axis=-1, keepdims=True)
            ds = (p * (dp - delta) * LN2).astype(BF16)
            dv_sc[:, lanes] += _dot(p.astype(BF16), dov, "tn")
            dk_sc[:, lanes] += _dot(ds, qv, "tn")
            dq_ref[rows, lanes] += _dot(ds, kv_)

        @pl.when(qi == nq - 1)
        def _():
            dk_ref[...] = dk_sc[...].astype(dk_ref.dtype)
            dv_ref[...] = dv_sc[...].astype(dv_ref.dtype)

        @pl.when(jnp.logical_and(h == HEADS // hps - 1, jnp.logical_and(ki == nkt - 1, qi == nq - 1)))
        def _():
            _scatter_wait(_scatter_copies(g_refs, got_refs, send_sems, recv_sems))

    hps = ATT_BWD_HEADS_PER_STEP
    qs = pl.BlockSpec((tq, hps * LANES), lambda h, j, i: (i, h))
    ks = pl.BlockSpec((tk, hps * LANES), lambda h, j, i: (j, h))
    res = pl.pallas_call(
        body, name="mla_bwd", grid=(HEADS // hps, nkt, nq),
        in_specs=[qs, ks, pl.BlockSpec((tk, hps * LANES), lambda h, j, i: (j, HEADS // hps + h)), qs, qs, qs] + [ANY] * n,
        out_specs=[pl.BlockSpec((S, hps * LANES), lambda h, j, i: (0, h), pipeline_mode=pl.Buffered(1)), ks, ks] + [ANY] * n,
        out_shape=[jax.ShapeDtypeStruct((S, HEADS * LANES), F32), jax.ShapeDtypeStruct((S, HEADS * LANES), BF16),
                   jax.ShapeDtypeStruct((S, HEADS * LANES), BF16)]
        + [jax.ShapeDtypeStruct((8,) + g.shape[2:], g.dtype) for g in gs],
        scratch_shapes=[pltpu.VMEM((tk, hps * LANES), F32)] * 2
        + [pltpu.SemaphoreType.DMA((8 * n,)), pltpu.SemaphoreType.DMA((7 * n,))],
        compiler_params=_params(),
    )(q, k, kv, do, lse, o, *gs)
    return res[0], res[1], res[2], list(res[3:])


def _ret_tables(decay_row, backward):
    C = RET_CHUNK
    lg = -jnp.exp(decay_row)
    t = lax.broadcasted_iota(jnp.int32, (C, C), 0).astype(F32)
    s = lax.broadcasted_iota(jnp.int32, (C, C), 1).astype(F32)
    ridx = lax.broadcasted_iota(jnp.int32, (C, LANES), 0).astype(F32)
    if backward:
        dist, mask, aw, bw = s - t, s > t, C - ridx, ridx
    else:
        dist, mask, aw, bw = t - s, t >= s, ridx + 1.0, C - 1.0 - ridx
    dist = jnp.maximum(dist, 0.0)
    din = jnp.where(mask, jnp.exp(lg[:, :1] * dist), 0.0)
    return dict(din=din, dist=dist, a=jnp.exp(lg * aw), b=jnp.exp(lg * bw), c=jnp.exp(lg * C), aw=aw, bw=bw)


def _ret_fill_tables(decs, din_sc, a_sc, b_sc):
    for d, dec in enumerate(decs):
        for h in range(HEADS):
            tb = _ret_tables(dec[h:h + 1, :], d == 1)
            din_sc[d, h], a_sc[d, h], b_sc[d, h] = tb["din"], tb["a"], tb["b"]


def _ret_fwd(qr, kr, proj, v_block, dec_f, dec_b):
    S = qr.shape[0]
    C = RET_CHUNK
    n = S // C
    nc = min(RET_CHUNKS_PER_STEP, n)
    nb = n // nc
    W = HEADS * LANES

    def body(qf, kf, vf, qb, kb, vb, df, db, of, ob, sf_out, sb_out, st, din_sc, a_sc, b_sc):
        @pl.when(pl.program_id(0) == 0)
        def _():
            st[...] = jnp.zeros_like(st)
            _ret_fill_tables((df, db), din_sc, a_sc, b_sc)

        for d, (q_ref, k_ref, v_ref, dec, o_ref, s_out) in enumerate(
                [(qf, kf, vf, df, of, sf_out), (qb, kb, vb, db, ob, sb_out)]):
            for h in range(HEADS):
                lanes = slice(h * LANES, (h + 1) * LANES)
                din, a, b = din_sc[d, h], a_sc[d, h], b_sc[d, h]
                c = jnp.exp(-jnp.exp(dec[h:h + 1, :]) * C)
                for ci in (range(nc) if d == 0 else reversed(range(nc))):
                    rows = slice(ci * C, (ci + 1) * C)
                    qf32, kf32, v = q_ref[rows, lanes].astype(F32), k_ref[rows, lanes].astype(F32), v_ref[rows, lanes]
                    state = st[d, h]
                    s_out[ci, h] = state
                    inner = _dot((_dot(qf32.astype(BF16), kf32.astype(BF16), "nt") * din).astype(BF16), v)
                    cross = _dot((qf32 * a).astype(BF16), state.astype(BF16))
                    o_ref[rows, lanes] = inner + cross
                    st[d, h] = state * c + _dot((kf32 * b).astype(BF16), v, "tn")

    fw = lambda c0: pl.BlockSpec((nc * C, W), lambda j: (j, c0))
    bw = lambda c0: pl.BlockSpec((nc * C, W), lambda j: (nb - 1 - j, c0))
    dec_spec = pl.BlockSpec((HEADS, LANES), lambda j: (0, 0))
    st_shape = jax.ShapeDtypeStruct((n, HEADS, LANES, LANES), F32)
    return pl.pallas_call(
        body, name="ret_fwd", grid=(nb,),
        in_specs=[fw(0), fw(0), fw(v_block), bw(0), bw(0), bw(v_block), dec_spec, dec_spec],
        out_specs=[fw(0), bw(0), pl.BlockSpec((nc, HEADS, LANES, LANES), lambda j: (j, 0, 0, 0)),
                   pl.BlockSpec((nc, HEADS, LANES, LANES), lambda j: (nb - 1 - j, 0, 0, 0))],
        out_shape=[jax.ShapeDtypeStruct((S, W), F32)] * 2 + [st_shape] * 2,
        scratch_shapes=[pltpu.VMEM((2, HEADS, LANES, LANES), F32), pltpu.VMEM((2, HEADS, C, C), F32),
                        pltpu.VMEM((2, HEADS, C, LANES), F32), pltpu.VMEM((2, HEADS, C, LANES), F32)],
        compiler_params=_params(),
    )(qr, kr, proj, qr, kr, proj, dec_f, dec_b)


def _ret_bwd(qr, kr, proj, v_block, dret, sf, sb, dec_f, dec_b):
    S = qr.shape[0]
    C = RET_CHUNK
    n = S // C
    nc = min(RET_CHUNKS_PER_STEP, n)
    nb = n // nc
    W = HEADS * LANES

    def body(qf, kf, vf, gf, sf_ref, qb, kb, vb, gb, sb_ref, df, db,
             dqf, dkf, dvf, dqb, dkb, dvb, ddf, ddb, ds_sc, din_sc, a_sc, b_sc):
        j = pl.program_id(0)

        @pl.when(j == 0)
        def _():
            ds_sc[...] = jnp.zeros_like(ds_sc)
            ddf[...] = jnp.zeros_like(ddf)
            ddb[...] = jnp.zeros_like(ddb)
            _ret_fill_tables((df, db), din_sc, a_sc, b_sc)

        for d, (q_ref, k_ref, v_ref, g_ref, s_ref, dec, dq_ref, dk_ref, dv_ref, dd_ref) in enumerate(
                [(qf, kf, vf, gf, sf_ref, df, dqf, dkf, dvf, ddf), (qb, kb, vb, gb, sb_ref, db, dqb, dkb, dvb, ddb)]):
            static = _ret_tables(dec[0:1, :], d == 1)
            dist, aw, bw_ = static["dist"], static["aw"], static["bw"]
            for h in range(HEADS):
                lanes = slice(h * LANES, (h + 1) * LANES)
                din, a, b = din_sc[d, h], a_sc[d, h], b_sc[d, h]
                c = jnp.exp(-jnp.exp(dec[h:h + 1, :]) * C)
                dlg = jnp.zeros((1, 1), F32)
                for ci in (reversed(range(nc)) if d == 0 else range(nc)):
                    rows = slice(ci * C, (ci + 1) * C)
                    v, g = v_ref[rows, lanes], g_ref[rows, lanes]
                    qf32, kf32 = q_ref[rows, lanes].astype(F32), k_ref[rows, lanes].astype(F32)
                    q, k = qf32.astype(BF16), kf32.astype(BF16)
                    state, dstate = s_ref[ci, h], ds_sc[d, h]
                    dstate_b = dstate.astype(BF16)
                    dp = _dot(g, v, "nt")
                    a_ = _dot(q, k, "nt")
                    da = (dp * din).astype(BF16)
                    g1 = _dot(g, state.astype(BF16), "nt")
                    g2 = _dot(v, dstate_b, "nt")
                    dq_ref[rows, lanes] = (_dot(da, k) + g1 * a).astype(dq_ref.dtype)
                    dk_ref[rows, lanes] = (_dot(da, q, "tn") + g2 * b).astype(dk_ref.dtype)
                    dv_ref[rows, lanes] = (_dot((a_ * din).astype(BF16), g, "tn")
                                           + _dot((kf32 * b).astype(BF16), dstate_b)).astype(dv_ref.dtype)
                    dlg = dlg + (jnp.sum(dp * a_ * din * dist, keepdims=True)
                                 + jnp.sum(g1 * qf32 * a * aw, keepdims=True)
                                 + jnp.sum(g2 * kf32 * b * bw_, keepdims=True)
                                 + C * jnp.sum(c * dstate * state, keepdims=True))
                    ds_sc[d, h] = dstate * c + _dot((qf32 * a).astype(BF16), g, "tn")
                dd_ref[h:h + 1, :] += jnp.broadcast_to(dlg, (1, LANES))

        @pl.when(j == nb - 1)
        def _():
            ddf[...] = ddf[...] * -jnp.exp(df[...])
            ddb[...] = ddb[...] * -jnp.exp(db[...])

    fw = lambda c0: pl.BlockSpec((nc * C, W), lambda j: (nb - 1 - j, c0))
    bw = lambda c0: pl.BlockSpec((nc * C, W), lambda j: (j, c0))
    dec_spec = pl.BlockSpec((HEADS, LANES), lambda j: (0, 0))
    act = jax.ShapeDtypeStruct((S, W), BF16)
    return pl.pallas_call(
        body, name="ret_bwd", grid=(nb,),
        in_specs=[fw(0), fw(0), fw(v_block), fw(0), pl.BlockSpec((nc, HEADS, LANES, LANES), lambda j: (nb - 1 - j, 0, 0, 0)),
                  bw(0), bw(0), bw(v_block), bw(0), pl.BlockSpec((nc, HEADS, LANES, LANES), lambda j: (j, 0, 0, 0)),
                  dec_spec, dec_spec],
        out_specs=[fw(0)] * 3 + [bw(0)] * 3 + [dec_spec] * 2,
        out_shape=[act] * 6 + [jax.ShapeDtypeStruct((HEADS, LANES), F32)] * 2,
        scratch_shapes=[pltpu.VMEM((2, HEADS, LANES, LANES), F32), pltpu.VMEM((2, HEADS, C, C), F32),
                        pltpu.VMEM((2, HEADS, C, LANES), F32), pltpu.VMEM((2, HEADS, C, LANES), F32)],
        compiler_params=_params(),
    )(qr, kr, proj, dret, sf, qr, kr, proj, dret, sb, dec_f, dec_b)


def _pad_heads(w, hd):
    K = w.shape[0]
    return jnp.pad(w.reshape(K, HEADS, hd), ((0, 0), (0, 0), (0, LANES - hd))).reshape(K, HEADS * LANES)


def _unpad_heads(w, hd):
    K = w.shape[0]
    return w.reshape(K, HEADS, LANES)[:, :, :hd].reshape(K, HEADS * hd)


def _rope_consts(first_lane, half):
    lane = np.arange(LANES)
    first = ((lane >= first_lane) & (lane < first_lane + half)).astype(np.float32)
    second = ((lane >= first_lane + half) & (lane < first_lane + 2 * half)).astype(np.float32)
    fixed = (lane < first_lane).astype(np.float32)
    j = np.where(first > 0, lane - first_lane, lane - first_lane - half) * (first + second)
    inv = (ROPE_THETA ** (-j.astype(np.float64) / half)).astype(np.float32)
    return [jnp.asarray(v.reshape(1, LANES), F32) for v in (inv, first, second, fixed)]


def _assemble(name, gathered):
    if name in COL_SHARDED:
        return jnp.transpose(gathered, (1, 0, 2)).reshape(gathered.shape[1], 4 * gathered.shape[2])
    return gathered.reshape(4 * gathered.shape[1], gathered.shape[2])


def _split_for_reducers(name, g, dtype):
    if name in COL_SHARDED:
        K, N4 = g.shape
        return jnp.transpose(g.reshape(2, K // 2, 4, N4 // 4), (2, 0, 1, 3)).astype(dtype)
    return g.reshape(4, 2, g.shape[0] // 8, g.shape[1]).astype(dtype)


def _local_step(x, tab_m, tab_r, tgt, wts, late_shards, small):
    w_in = wts["w_in"]
    seg = [w_in[:, IN_OFFS[i]:IN_OFFS[i + 1]] for i in range(8)]
    kr_w = jnp.pad(seg[2], ((0, 0), (MLA_NOPE, LANES - MLA_QK)))
    w_in_p = jnp.concatenate([seg[7], seg[5], seg[6], _pad_heads(seg[3], RET_QK), _pad_heads(seg[4], RET_QK),
                              seg[0], seg[1], kr_w], axis=1)
    w_qb_p = _pad_heads(wts["w_q_b"], MLA_QK)
    kvw = wts["w_kv_b"].reshape(MLA_KV_RANK, HEADS, MLA_NOPE + MLA_V)
    pad_kv = lambda t: jnp.pad(t, ((0, 0), (0, 0), (0, LANES - t.shape[2]))).reshape(MLA_KV_RANK, HEADS * LANES)
    w_kn_p, w_v_p = pad_kv(kvw[:, :, :MLA_NOPE]), pad_kv(kvw[:, :, MLA_NOPE:])
    w_kv_p = jnp.concatenate([w_kn_p, w_v_p], axis=1)
    g_qn_p = jnp.pad(small["g_qn"], ((0, 0), (0, LANES - MLA_QK)))
    g_kn_p = jnp.pad(small["g_kn"], ((0, 0), (0, LANES - MLA_QK)))
    dec_f = jnp.broadcast_to(small["ret_decay_fwd"].reshape(HEADS, 1), (HEADS, LANES))
    dec_b = jnp.broadcast_to(small["ret_decay_bwd"].reshape(HEADS, 1), (HEADS, LANES))
    T, N = True, False
    RT, HT = ROW_TILE, HEAD_ROW_TILE

    aux_m = [(t, LANES, 0, N) for t in tab_m]
    aux_r = [(t, LANES, 0, N) for t in tab_r]

    proj, rstd1 = _mm("proj", x, w_in_p, "nn", BF16, a_gain=small["g_mix"])
    rows_a = [(proj, MLA_Q_RANK, 24, N), (proj, MLA_KV_RANK, 50, N)]
    cqn, ckvn = _rowwise("mla_lat_norm", _f_mla_a, [small["g_q_a"], small["g_kv_a"]], rows_a, [],
                         [(MLA_Q_RANK, BF16, N), (MLA_KV_RANK, BF16, N)], RT)
    qraw = _mm("mla_q_up", cqn, w_qb_p, "nn", BF16)
    kv = _mm("mla_kv_up", ckvn, w_kv_p, "nn", BF16)
    rows_b = [(qraw, LANES, 0, T), (kv, LANES, 0, T), (proj, LANES, 51, N)]
    q, k = _rowwise("mla_qk_norm_rope", _f_mla_b, [g_qn_p, g_kn_p], rows_b, aux_m, [(LANES, BF16, T)] * 2, HT, HEADS)
    o, lse, late = _flash_fwd(q, k, kv, [late_shards[n] for n in LATE])
    wl = {n: _assemble(n, g) for n, g in zip(LATE, late)}
    w_mla_p = jnp.pad(wl["w_mla_out"].reshape(HEADS, MLA_V, D_MODEL), ((0, 0), (0, LANES - MLA_V), (0, 0))).reshape(HEADS * LANES, D_MODEL)
    w_ret_out, w_out, w_gu, w_down = wl["w_ret_out"], wl["w_out"], wl["w_gate_up"], wl["w_down"]
    y_a = _mm("mla_out", o, w_mla_p, "nn", BF16)
    rows_rr = [(proj, LANES, 32, T), (proj, LANES, 40, T)]
    qr, kr = _rowwise("ret_rope", _f_ret_rope, [], rows_rr, aux_r, [(LANES, RET_QK_DTYPE, T)] * 2, HT, HEADS)
    ret_f, ret_b, st_f, st_b = _ret_fwd(qr, kr, proj, 2, dec_f, dec_b)
    rows_rp = [(ret_f, LANES, 0, T), (ret_b, LANES, 0, T), (proj, LANES, 24, T)]
    (o_b,) = _rowwise("ret_post", _f_ret_post, [], rows_rp, [], [(LANES, BF16, T)], HT, HEADS)
    y_b = _mm("ret_out", o_b, w_ret_out, "nn", BF16)
    rows_m = [(proj, D_MODEL, 0, N), (proj, D_MODEL, 1, N), (y_a, D_MODEL, 0, N), (y_b, D_MODEL, 0, N)]
    (merged,) = _rowwise("merge", _f_merge, [], rows_m, [], [(D_MODEL, BF16, N)], RT)
    x2 = _mm("mix_out", merged, w_out, "nn", F32, res=x)
    gu, rstd2 = _mm("ffn_gate_up", x2, w_gu, "nn", BF16, a_gain=small["g_ffn"])
    rows_sw = [(gu, FFN_HIDDEN, 0, N), (gu, FFN_HIDDEN, 1, N)]
    (act,) = _rowwise("swiglu", _f_swiglu, [], rows_sw, [], [(FFN_HIDDEN, BF16, N)], RT)
    dy, dy_b16, loss_row = _mm("ffn_down", act, w_down, "nn", None,
                               epilogue=(_epi_loss, [x2, tgt], [], [F32, BF16], [(1, LANES)]))

    dact = _mm("d_act", dy_b16, w_down, "nt", BF16)
    dw_down = _mm("dw_down", act, dy_b16, "tn", BF16)
    (dgu,), _ = _rowwise_vjp("swiglu_bwd", _f_swiglu, [], rows_sw, [], [[(dact, FFN_HIDDEN, 0, N)]], [([0, 1], BF16)], RT)
    dx2, dx2_b16, dg_ffn = _mm("d_h2", dgu, w_gu, "nt", None,
                               epilogue=(_epi_rms_bwd(2), [x2, dy], [small["g_ffn"]], [F32, BF16], [(1, D_MODEL)]))
    dw_gu = _mm("dw_gate_up", x2, dgu, "tn", BF16, a_scale=(rstd2, small["g_ffn"]), shard_out=True)
    dmerged = _mm("d_merged", dx2_b16, w_out, "nt", BF16)
    dw_out = _mm("dw_out", merged, dx2_b16, "tn", BF16)
    (dgl, dy_a, dy_b), _ = _rowwise_vjp("merge_bwd", _f_merge, [], rows_m, [], [[(dmerged, D_MODEL, 0, N)]],
                                        [([0, 1], BF16), ([2], BF16), ([3], BF16)], RT)
    do_b = _mm("d_ret_o", dy_b, w_ret_out, "nt", BF16)
    dw_ret_out = _mm("dw_ret_out", o_b, dy_b, "tn", BF16)
    (dret, dg_r), _ = _rowwise_vjp("ret_post_bwd", _f_ret_post, [], rows_rp, [], [[(do_b, LANES, 0, T)]],
                                   [([0], BF16), ([2], BF16)], HT, HEADS)
    dqf, dkf, dvf, dqb, dkb, dvb, ddec_f, ddec_b = _ret_bwd(qr, kr, proj, 2, dret, st_f, st_b, dec_f, dec_b)
    (dq_r, dk_r), _ = _rowwise_vjp("ret_rope_bwd", _f_ret_rope, [], rows_rr, aux_r,
                                   [[(dqf, LANES, 0, T), (dqb, LANES, 0, T)], [(dkf, LANES, 0, T), (dkb, LANES, 0, T)]],
                                   [([0], BF16), ([1], BF16)], HT, HEADS)
    (dv_r,) = _rowwise("ret_dv_sum", _f_add, [], [(dvf, D_MODEL, 0, N), (dvb, D_MODEL, 0, N)], [], [(D_MODEL, BF16, N)], RT)
    do = _mm("d_mla_o", dy_a, w_mla_p, "nt", BF16)
    dw_mla_p = _mm("dw_mla_out", o, dy_a, "tn", BF16)
    dw_mla = dw_mla_p.reshape(HEADS, LANES, D_MODEL)[:, :MLA_V].reshape(HEADS * MLA_V, D_MODEL)
    late_grads = {"w_mla_out": dw_mla, "w_ret_out": dw_ret_out, "w_out": dw_out, "w_down": dw_down}
    late_gs = [dw_gu if n == "w_gate_up" else _split_for_reducers(n, late_grads[n], BF16) for n in LATE]
    dq, dk, dv, late_got = _flash_bwd(q, k, kv, do, lse, o, late_gs)
    (dqraw, dkn, dkr), (dg_qn_p, dg_kn_p) = _rowwise_vjp(
        "mla_qk_norm_rope_bwd", _f_mla_b, [g_qn_p, g_kn_p], rows_b, aux_m, [[(dq, LANES, 0, T)], [(dk, LANES, 0, T)]],
        [([0], BF16), ([1], BF16), ([2], F32)], HT, HEADS)
    dckvn = _mm("d_ckvn_v", dv, w_v_p, "nt", BF16, res=_mm("d_ckvn_k", dkn, w_kn_p, "nt", F32))
    dw_kn_p = _mm("dw_kv_k", ckvn, dkn, "tn", BF16)
    dw_v_p = _mm("dw_kv_v", ckvn, dv, "tn", BF16)
    dcqn = _mm("d_cqn", dqraw, w_qb_p, "nt", BF16)
    dw_qb_p = _mm("dw_q_b", cqn, dqraw, "tn", BF16)
    (dcq, dckv), (dg_q_a, dg_kv_a) = _rowwise_vjp(
        "mla_lat_norm_bwd", _f_mla_a, [small["g_q_a"], small["g_kv_a"]], rows_a, [],
        [[(dcqn, MLA_Q_RANK, 0, N)], [(dckvn, MLA_KV_RANK, 0, N)]], [([0], BF16), ([1], BF16)], RT)
    dproj = jnp.concatenate([dgl, dv_r, dg_r, dq_r, dk_r, dcq, dckv, dkr.astype(BF16)], axis=1)
    dw_in_p = _mm("dw_in", x, dproj, "tn", BF16, a_scale=(rstd1, small["g_mix"]))

    c = lambda a, b_: dw_in_p[:, a:b_]
    dw_in = jnp.concatenate([c(6144, 6400), c(6400, 6528), c(6528 + MLA_NOPE, 6528 + MLA_QK), _unpad_heads(c(4096, 5120), RET_QK),
                             _unpad_heads(c(5120, 6144), RET_QK), c(2048, 3072), c(3072, 4096), c(0, 2048)], axis=1)
    un_kv = lambda t: t.reshape(MLA_KV_RANK, HEADS, LANES)[:, :, :MLA_NOPE]
    dw_kv = jnp.concatenate([un_kv(dw_kn_p), un_kv(dw_v_p)], axis=2).reshape(MLA_KV_RANK, HEADS * (MLA_NOPE + MLA_V))
    grads = {"w_in": dw_in, "w_q_b": _unpad_heads(dw_qb_p, MLA_QK), "w_kv_b": dw_kv}
    dx, dg_mix, first_got = _mm("d_h", dproj, w_in_p, "nt", None,
                                epilogue=(_epi_rms_bwd(1), [x, dx2], [small["g_mix"]], [F32], [(1, D_MODEL)]),
                                scatter=[_split_for_reducers(n, grads[n], BF16) for n in FIRST])
    sgrads = {"g_mix": dg_mix, "g_q_a": dg_q_a, "g_kv_a": dg_kv_a, "g_qn": dg_qn_p[:, :MLA_QK], "g_kn": dg_kn_p[:, :MLA_QK],
              "ret_decay_fwd": ddec_f[:, 0].reshape(1, HEADS), "ret_decay_bwd": ddec_b[:, 0].reshape(1, HEADS), "g_ffn": dg_ffn}
    return loss_row, dx, first_got + late_got, sgrads


def _coords():
    return lax.axis_index("x"), lax.axis_index("y"), lax.axis_index("c")


def _other_chips(x, y):
    return [(1 - x, y), (x, 1 - y), (1 - x, 1 - y)]


ANY = pl.BlockSpec(memory_space=pl.ANY)


def _gather_copies(ins, outs, send_sems, recv_sems):
    x, y, c = _coords()
    mine = 2 * x + y
    sends, arrivals = [], []
    for w in range(len(ins)):
        for j, (cx, cy) in enumerate(_other_chips(x, y)):
            sems = dict(send_sem=send_sems.at[3 * w + j], recv_sem=recv_sems.at[3 * w + j],
                        device_id=(cx, cy, c), device_id_type=MESH)
            sends.append(pltpu.make_async_remote_copy(src_ref=ins[w], dst_ref=outs[w].at[mine], **sems))
            arrivals.append(functools.partial(pltpu.make_async_remote_copy, src_ref=ins[w],
                                              dst_ref=outs[w].at[2 * cx + cy], **sems))
    return sends, arrivals


def _gather_start(copies):
    for cp in list(copies[0]) + list(copies[2] if len(copies) > 2 else []):
        cp.start()


def _gather_wait(copies):
    sends, arrivals = copies[:2]
    for make in arrivals:
        make().wait_recv()
    for cp in sends:
        cp.wait_send()
    for cp in (copies[2] if len(copies) > 2 else []):
        cp.wait()


def _fill_slot(buf, piece, slot):
    idx = lax.broadcasted_iota(jnp.int32, (buf.shape[0],) + (1,) * piece.ndim, 0)
    return jnp.where(idx == slot, piece[None], buf)


def _rope_tables_and_first_gather(pos, consts_mla, consts_ret, shards):
    S = pos.shape[0]
    tm = min(HEAD_ROW_TILE, S)
    nt = S // tm
    n = len(shards)

    def body(pos_ref, *refs):
        consts, ins = (refs[:4], refs[4:8]), refs[8:8 + n]
        tabs, outs = refs[8 + n:14 + n], refs[14 + n:14 + 2 * n]
        send_sems, recv_sems = refs[14 + 2 * n:]
        i = pl.program_id(0)
        x, y, c = _coords()
        chips = _other_chips(x, y)
        mine = 2 * x + y

        def half(ref, slot, core):
            rows = ref.shape[1] // 2
            return ref.at[slot, pl.ds(pl.multiple_of(core * rows, 8), rows)]

        def copy(w, k, slot, core, to, src=None):
            return pltpu.make_async_remote_copy(
                src_ref=half(outs[w], slot, core) if src is None else src, dst_ref=half(outs[w], slot, core),
                send_sem=send_sems.at[6 * w + k], recv_sem=recv_sems.at[6 * w + k], device_id=to, device_id_type=MESH)

        def first(w, j):
            rows = ins[w].shape[0] // 2
            return copy(w, j, mine, c, (*chips[j], c), src=ins[w].at[pl.ds(pl.multiple_of(c * rows, 8), rows)])

        @pl.when(i == 0)
        def _():
            for w in range(n):
                for j in range(3):
                    first(w, j).start()

        for k in range(2):
            vals = _f_rope_table([r[...] for r in consts[k]], None, [pos_ref[...]])
            for t_ref, v in zip(tabs[3 * k:3 * k + 3], vals):
                t_ref[...] = v

        @pl.when(i == nt - 1)
        def _():
            passed = []
            for w in range(n):
                for j, (cx, cy) in enumerate(chips):
                    copy(w, j, 2 * cx + cy, c, (x, y, c)).wait_recv()
                    cp = copy(w, 3 + j, 2 * cx + cy, c, (x, y, 1 - c))
                    cp.start()
                    passed.append(cp)
            for w in range(n):
                for j, (cx, cy) in enumerate(chips):
                    copy(w, 3 + j, 2 * cx + cy, 1 - c, (x, y, c)).wait_recv()
            for w in range(n):
                for j in range(3):
                    first(w, j).wait_send()
            for cp in passed:
                cp.wait_send()

    const = lambda p: pl.BlockSpec(p.shape, lambda i: (0, 0))
    tab = pl.BlockSpec((tm, LANES), lambda i: (i, 0))
    res = pl.pallas_call(
        body, name="rope_tables_first_gather", grid=(nt,),
        in_specs=[pl.BlockSpec((tm, 1), lambda i: (i, 0))] + [const(p) for p in list(consts_mla) + list(consts_ret)] + [ANY] * n,
        out_specs=[tab] * 6 + [ANY] * n,
        out_shape=[jax.ShapeDtypeStruct((S, LANES), F32)] * 6 + [jax.ShapeDtypeStruct((4,) + s.shape, s.dtype) for s in shards],
        scratch_shapes=[pltpu.SemaphoreType.DMA((6 * n,)), pltpu.SemaphoreType.DMA((6 * n,))],
        compiler_params=_params(),
    )(pos, *consts_mla, *consts_ret, *shards)
    return list(res[:3]), list(res[3:6]), list(res[6:])


def _scatter_copies(ins, outs, send_sems, recv_sems):
    x, y, c = _coords()
    me = 4 * x + 2 * y + c
    n = len(ins)
    sends, arrivals = [], []
    local = [pltpu.make_async_copy(ins[w].at[2 * x + y, c], outs[w].at[me], send_sems.at[7 * n + w]) for w in range(n)]
    for w in range(n):
        for k in range(1, 8):
            px, py, pc = x ^ (k >> 2), y ^ ((k >> 1) & 1), c ^ (k & 1)
            sems = dict(send_sem=send_sems.at[7 * w + k - 1], recv_sem=recv_sems.at[7 * w + k - 1],
                        device_id=(px, py, pc), device_id_type=MESH)
            sends.append(pltpu.make_async_remote_copy(src_ref=ins[w].at[2 * px + py, pc], dst_ref=outs[w].at[me], **sems))
            arrivals.append(functools.partial(
                pltpu.make_async_remote_copy, src_ref=ins[w].at[2 * px + py, pc],
                dst_ref=outs[w].at[4 * px + 2 * py + pc], **sems))
    return sends, arrivals, local


_scatter_start, _scatter_wait = _gather_start, _gather_wait


def _grad_sum8(name, got):
    _, R, W = got.shape
    tr = _pick(R, 256, 16)

    def body(g_ref, o_ref):
        total = g_ref[0].astype(F32)
        for d in range(1, 8):
            total = total + g_ref[d].astype(F32)
        o_ref[...] = total

    return pl.pallas_call(
        body, name=name, grid=(R // tr,), in_specs=[pl.BlockSpec((8, tr, W), lambda i: (0, i, 0))],
        out_specs=pl.BlockSpec((tr, W), lambda i: (i, 0)), out_shape=jax.ShapeDtypeStruct((R, W), F32),
        compiler_params=_params(),
    )(got)


def _half_exchange(halves):
    n = len(halves)

    def body(*refs):
        ins, outs, send_sems, recv_sems = refs[:n], refs[n:2 * n], refs[2 * n], refs[2 * n + 1]
        x, y, c = _coords()
        sends = []
        for w in range(n):
            cp = pltpu.make_async_remote_copy(
                src_ref=ins[w], dst_ref=outs[w], send_sem=send_sems.at[w], recv_sem=recv_sems.at[w],
                device_id=(x, y, 1 - c), device_id_type=MESH)
            cp.start()
            sends.append(cp)
        for cp in sends:
            cp.wait()

    got = pl.pallas_call(
        body, name="grad_half_exchange", in_specs=[ANY] * n, out_specs=[ANY] * n,
        out_shape=[jax.ShapeDtypeStruct(h.shape, F32) for h in halves],
        scratch_shapes=[pltpu.SemaphoreType.DMA((n,)), pltpu.SemaphoreType.DMA((n,))],
    )(*halves)
    c = lax.axis_index("c")
    return [jnp.where(c == 0, jnp.stack([mine, theirs]), jnp.stack([theirs, mine])) for mine, theirs in zip(halves, got)]


def _adamw_math(w, g, m, v):
    m2 = ADAM_B1 * m + (1.0 - ADAM_B1) * g
    v2 = ADAM_B2 * v + (1.0 - ADAM_B2) * (g * g)
    m_hat = m2 / (1.0 - ADAM_B1 ** ADAM_STEP)
    v_hat = v2 / (1.0 - ADAM_B2 ** ADAM_STEP)
    return -ADAM_LR * (m_hat / (jnp.sqrt(v_hat) + ADAM_EPS) + ADAM_WD * w), m2, v2


def _small_allreduce_adamw(pack_g, pack_w, pack_m, pack_v):
    def body(g_ref, w_ref, m_ref, v_ref, sum_ref, d_ref, m_out, v_out, land, send_sems, recv_sems):
        x, y, c = _coords()
        me = 4 * x + 2 * y + c
        land[me] = g_ref[...]
        sends = []
        for k in range(1, 8):
            peer = (x ^ (k >> 2), y ^ ((k >> 1) & 1), c ^ (k & 1))
            cp = pltpu.make_async_remote_copy(
                src_ref=g_ref, dst_ref=land.at[me], send_sem=send_sems.at[k - 1], recv_sem=recv_sems.at[k - 1],
                device_id=peer, device_id_type=MESH)
            cp.start()
            sends.append((cp, peer))
        for k, (cp, peer) in enumerate(sends):
            pltpu.make_async_remote_copy(
                src_ref=g_ref, dst_ref=land.at[4 * peer[0] + 2 * peer[1] + peer[2]], send_sem=send_sems.at[k],
                recv_sem=recv_sems.at[k], device_id=peer, device_id_type=MESH).wait_recv()
        for cp, _ in sends:
            cp.wait_send()
        total = land[0]
        for d in range(1, 8):
            total = total + land[d]
        sum_ref[...] = total
        d_ref[...], m_out[...], v_out[...] = _adamw_math(w_ref[...], total, m_ref[...], v_ref[...])

    vm = pl.BlockSpec(memory_space=pltpu.VMEM)
    shp = jax.ShapeDtypeStruct(pack_g.shape, F32)
    return pl.pallas_call(
        body, name="small_allreduce_adamw", in_specs=[vm] * 4, out_specs=[vm] * 4, out_shape=[shp] * 4,
        scratch_shapes=[pltpu.VMEM((8,) + pack_g.shape, F32), pltpu.SemaphoreType.DMA((7,)), pltpu.SemaphoreType.DMA((7,))],
    )(pack_g, pack_w, pack_m, pack_v)


def _adamw(name, w, g, m, v):
    R, C = w.shape
    tr = _pick(R, 256, 8)

    def body(w_ref, g_ref, m_ref, v_ref, d_out, m_out, v_out):
        d_out[...], m_out[...], v_out[...] = _adamw_math(w_ref[...], g_ref[...], m_ref[...], v_ref[...])

    spec = pl.BlockSpec((tr, C), lambda i: (i, 0))
    return pl.pallas_call(
        body, name=name, grid=(R // tr,), in_specs=[spec] * 4, out_specs=[spec] * 3,
        out_shape=[jax.ShapeDtypeStruct((R, C), F32)] * 3, compiler_params=_params(),
    )(w, g, m, v)


def _pack_small(vals, last):
    flat = jnp.concatenate([v.reshape(-1) for v in vals] + [last.reshape(-1)])
    return jnp.pad(flat, (0, SMALL_ROWS * LANES - flat.shape[0])).reshape(SMALL_ROWS, LANES)


def kernel(x, positions, g_mix, w_in, g_q_a, w_q_b, g_kv_a, w_kv_b, g_qn, g_kn, w_mla_out, ret_decay_fwd, ret_decay_bwd, w_ret_out, w_out, g_ffn, w_gate_up, w_down, loss_target, m_g_mix, m_w_in, m_g_q_a, m_w_q_b, m_g_kv_a, m_w_kv_b, m_g_qn, m_g_kn, m_w_mla_out, m_ret_decay_fwd, m_ret_decay_bwd, m_w_ret_out, m_w_out, m_g_ffn, m_w_gate_up, m_w_down, v_g_mix, v_w_in, v_g_q_a, v_w_q_b, v_g_kv_a, v_w_kv_b, v_g_qn, v_g_kn, v_w_mla_out, v_ret_decay_fwd, v_ret_decay_bwd, v_w_ret_out, v_w_out, v_g_ffn, v_w_gate_up, v_w_down):
    given = dict(locals())
    S = x.shape[1]
    xs, tgt = x.reshape(S, D_MODEL), loss_target.reshape(S, D_MODEL)
    pos = positions.reshape(S, 1).astype(F32)

    first_shards = [given[n].astype(BF16) for n in FIRST]
    my_chip = 2 * lax.axis_index("x") + lax.axis_index("y")
    tab_m, tab_r, gathered = _rope_tables_and_first_gather(
        pos, _rope_consts(MLA_NOPE, MLA_ROPE // 2), _rope_consts(0, RET_QK // 2), first_shards)
    wts = {n: _assemble(n, _fill_slot(g, s, my_chip)) for n, g, s in zip(FIRST, gathered, first_shards)}
    late_shards = {n: given[n].astype(BF16) for n in LATE}
    small = {n: given[n].reshape(1, -1) for n in SMALL}

    loss_row, dx, pieces, sgrads = _local_step(xs, tab_m, tab_r, tgt, wts, late_shards, small)

    halves = [_grad_sum8("grad_sum_" + n, got) for n, got in zip(FIRST + LATE, pieces)]
    reduced = _half_exchange(halves)

    out = {}
    for n, r in zip(FIRST + LATE, reduced):
        g = r.reshape(given[n].shape)
        out["grad_" + n] = g
        out["delta_" + n], out["new_m_" + n], out["new_v_" + n] = _adamw("adamw_" + n, given[n], g, given["m_" + n], given["v_" + n])

    one = jnp.ones((1,), F32)
    pk = _small_allreduce_adamw(
        _pack_small([sgrads[n] for n in SMALL], loss_row[0, :1]),
        _pack_small([given[n] for n in SMALL], 0 * one),
        _pack_small([given["m_" + n] for n in SMALL], 0 * one),
        _pack_small([given["v_" + n] for n in SMALL], one))
    off = 0
    for n in SMALL:
        sz = given[n].shape[0]
        for pre, arr in zip(["grad_", "delta_", "new_m_", "new_v_"], pk):
            out[pre + n] = arr.reshape(-1)[off:off + sz]
        off += sz
    loss = pk[0].reshape(-1)[off]

    return (loss, dx.reshape(x.shape), *[out["grad_" + n] for n in WEIGHTS], *[out["delta_" + n] for n in WEIGHTS],
            *[out["new_m_" + n] for n in WEIGHTS], *[out["new_v_" + n] for n in WEIGHTS])
```

```python
import functools
import math

import numpy as np
import jax
import jax.numpy as jnp
from jax import lax
from jax.experimental import pallas as pl
from jax.experimental.pallas import tpu as pltpu

F32 = jnp.float32
BF16 = jnp.bfloat16
MESH = pl.DeviceIdType.MESH

D_MODEL = 1024
HEADS = 8
LANES = 128
MLA_Q_RANK, MLA_KV_RANK = 256, 128
MLA_NOPE, MLA_ROPE, MLA_V = 64, 32, 64
MLA_QK = MLA_NOPE + MLA_ROPE
LN2 = math.log(2.0)
MLA_Q_SCALE = MLA_QK ** -0.5 / LN2
RET_QK, RET_V, RET_CHUNK = 64, 128, 128
RET_QK_DTYPE = BF16
RET_CHUNKS_PER_STEP = 2
FFN_HIDDEN = 2816
ROPE_THETA = 10000.0
EPS = 1e-6
IN_SPLITS = [256, 128, 32, 512, 512, 1024, 1024, 2048]
IN_OFFS = [0] + list(np.cumsum(IN_SPLITS))
ADAM_LR, ADAM_B1, ADAM_B2, ADAM_EPS, ADAM_WD, ADAM_STEP = 0.001, 0.9, 0.999, 1e-08, 0.01, 10

VMEM_LIMIT = 56 * 1024 * 1024
ROW_TILE = 256
HEAD_ROW_TILE = 2048
MM_TM, MM_TN, MM_TK, MM_KFULL = 1408, 2048, 2048, 2816
ATT_TQ = 256
ATT_BQ, ATT_BK = 1024, 1024
ATT_HEADS_PER_STEP = 8
ATT_BWD_HEADS_PER_STEP = 4

SHARDED = ["w_in", "w_q_b", "w_kv_b", "w_mla_out", "w_ret_out", "w_out", "w_gate_up", "w_down"]
COL_SHARDED = {"w_in", "w_q_b", "w_kv_b", "w_mla_out", "w_gate_up"}
FIRST = ["w_in", "w_q_b", "w_kv_b"]
LATE = ["w_mla_out", "w_ret_out", "w_out", "w_gate_up", "w_down"]
SMALL = ["g_mix", "g_q_a", "g_kv_a", "g_qn", "g_kn", "ret_decay_fwd", "ret_decay_bwd", "g_ffn"]
WEIGHTS = ["g_mix", "w_in", "g_q_a", "w_q_b", "g_kv_a", "w_kv_b", "g_qn", "g_kn", "w_mla_out",
           "ret_decay_fwd", "ret_decay_bwd", "w_ret_out", "w_out", "g_ffn", "w_gate_up", "w_down"]
SMALL_ROWS = 24


def _params(**kw):
    return pltpu.CompilerParams(vmem_limit_bytes=VMEM_LIMIT, **kw)


def _pick(dim, target, unit=128):
    if dim <= target:
        return dim
    best = None
    for d in range(unit, target + 1, unit):
        if dim % d == 0:
            best = d
    assert best is not None, (dim, target)
    return best


_DOT = {"nn": (((1,), (0,)), ((), ())), "nt": (((1,), (1,)), ((), ())), "tn": (((0,), (0,)), ((), ()))}


def _dot(a, b, mode="nn"):
    return lax.dot_general(a, b, _DOT[mode], preferred_element_type=F32)


def _rms_rows(x, g):
    x = x.astype(F32)
    return x * lax.rsqrt(jnp.mean(x * x, axis=-1, keepdims=True) + EPS) * g


def _epi_loss(acc, extras, params):
    e = acc + extras[0] - extras[1]
    dy = e * (1.0 / D_MODEL)
    loss = 0.5 * jnp.sum(jnp.mean(e * e, axis=-1, keepdims=True), axis=0, keepdims=True)
    return [dy, dy], [jnp.broadcast_to(loss, (1, LANES))]


def _epi_rms_bwd(n_out):
    def fn(acc, extras, params):
        _, vjp = jax.vjp(_rms_rows, extras[0], params[0])
        dx, dg = vjp(acc)
        return [dx + extras[1]] * n_out, [dg]
    return fn


def _mm(name, a, b, mode, out_dtype, res=None, a_gain=None, epilogue=None, shard_out=False, scatter=None):
    if mode == "nn":
        (M, K), (K2, N) = a.shape, b.shape
    elif mode == "nt":
        (M, K), (N, K2) = a.shape, b.shape
    else:
        (K, M), (K2, N) = a.shape, b.shape
    assert K == K2, (name, a.shape, b.shape)
    tm, tn = _pick(M, MM_TM), _pick(N, MM_TN)
    tk = K if K <= MM_KFULL else _pick(K, MM_TK)
    if shard_out:
        tm, tn = M // 2, N // 4
    if epilogue is not None:
        tm = _pick(M, MM_TM // 2)
    nk = K // tk
    cache_a = a_gain is not None
    if a_gain is not None:
        assert mode == "nn" and tk == K and epilogue is None and not shard_out, name
    n_in = 2 + (res is not None) + (a_gain is not None)
    extras, eparams, e_outs, e_sums = ([], [], [], [])
    if epilogue is not None:
        assert tn == N and res is None and not shard_out, name
        epi_fn, extras, eparams, e_outs, e_sums = epilogue
    n_out = len(e_outs) + len(e_sums) if epilogue is not None else 1 + cache_a
    scatter = list(scatter or [])
    n_sc = len(scatter)
    assert not n_sc or epilogue is not None, name
    ni, nj = M // tm, N // tn

    def body(*refs):
        a_ref, b_ref = refs[0], refs[1]
        base = n_in + len(extras) + len(eparams)
        ex_refs = refs[n_in:n_in + len(extras)]
        ep_refs = refs[n_in + len(extras):base]
        sc_in, out_refs = refs[base:base + n_sc], refs[base + n_sc:base + n_sc + n_out]
        sc_out = refs[base + n_sc + n_out:base + 2 * n_sc + n_out]
        scratch = refs[base + 2 * n_sc + n_out:]
        acc = scratch[0]
        i, j, k = pl.program_id(0), pl.program_id(1), pl.program_id(2)

        if n_sc:
            @pl.when(jnp.logical_and(i == 0, jnp.logical_and(j == 0, k == 0)))
            def _():
                _scatter_start(_scatter_copies(sc_in, sc_out, scratch[-2], scratch[-1]))

        @pl.when(k == 0)
        def _():
            acc[...] = jnp.zeros_like(acc)

        if cache_a:
            @pl.when(j == 0)
            def _():
                out_refs[1][...] = _rms_rows(a_ref[...], refs[n_in - 1][...]).astype(BF16)
            av = out_refs[1][...]
        else:
            av = a_ref[...].astype(BF16)
        acc[...] += _dot(av, b_ref[...].astype(BF16), mode)

        @pl.when(k == nk - 1)
        def _():
            if epilogue is None:
                r = acc[...]
                if res is not None:
                    r = r + refs[2][...].astype(F32)
                out_refs[0][...] = r.astype(out_refs[0].dtype).reshape(out_refs[0].shape)
            else:
                vals, sums = epi_fn(acc[...], [r[...] for r in ex_refs], [p[...] for p in ep_refs])
                for o_ref, v in zip(out_refs, vals):
                    o_ref[...] = v.astype(o_ref.dtype)
                for s_ref, v in zip(out_refs[len(vals):], sums):
                    @pl.when(i == 0)
                    def _(s_ref=s_ref):
                        s_ref[...] = jnp.zeros_like(s_ref)
                    s_ref[...] += v

        if n_sc:
            @pl.when(jnp.logical_and(i == ni - 1, jnp.logical_and(j == nj - 1, k == nk - 1)))
            def _():
                _scatter_wait(_scatter_copies(sc_in, sc_out, scratch[-2], scratch[-1]))

    a_spec = pl.BlockSpec((tk, tm), lambda i, j, k: (k, i)) if mode == "tn" else pl.BlockSpec((tm, tk), lambda i, j, k: (i, k))
    b_spec = pl.BlockSpec((tn, tk), lambda i, j, k: (j, k)) if mode == "nt" else pl.BlockSpec((tk, tn), lambda i, j, k: (k, j))
    o_spec = pl.BlockSpec((tm, tn), lambda i, j, k: (i, j))
    const = lambda p: pl.BlockSpec(p.shape, lambda i, j, k: (0,) * p.ndim)
    ins, specs = [a, b], [a_spec, b_spec]
    if res is not None:
        ins.append(res)
        specs.append(o_spec)
    if a_gain is not None:
        ins.append(a_gain)
        specs.append(const(a_gain))
    ins += list(extras) + list(eparams)
    specs += [o_spec] * len(extras) + [const(p) for p in eparams]
    if epilogue is not None:
        out_specs = [o_spec] * len(e_outs) + [pl.BlockSpec(s, lambda i, j, k: (0, 0)) for s in e_sums]
        out_shape = [jax.ShapeDtypeStruct((M, N), dt) for dt in e_outs] + [jax.ShapeDtypeStruct(s, F32) for s in e_sums]
    elif shard_out:
        out_specs = pl.BlockSpec((1, 1, tm, tn), lambda i, j, k: (j, i, 0, 0))
        out_shape = jax.ShapeDtypeStruct((4, 2, tm, tn), out_dtype)
    elif cache_a:
        out_specs = [o_spec, pl.BlockSpec((tm, K), lambda i, j, k: (i, 0))]
        out_shape = [jax.ShapeDtypeStruct((M, N), out_dtype), jax.ShapeDtypeStruct((M, K), BF16)]
    else:
        out_specs, out_shape = o_spec, jax.ShapeDtypeStruct((M, N), out_dtype)
    scratch_shapes = [pltpu.VMEM((tm, tn), F32)]
    if n_sc:
        ins += scatter
        specs += [ANY] * n_sc
        out_specs = list(out_specs) + [ANY] * n_sc
        out_shape = list(out_shape) + [jax.ShapeDtypeStruct((8,) + g.shape[2:], g.dtype) for g in scatter]
        scratch_shapes += [pltpu.SemaphoreType.DMA((8 * n_sc,)), pltpu.SemaphoreType.DMA((7 * n_sc,))]
    res_ = pl.pallas_call(
        body, name=name, grid=(ni, nj, nk), in_specs=specs, out_specs=out_specs, out_shape=out_shape,
        scratch_shapes=scratch_shapes, compiler_params=_params(),
    )(*ins)
    if n_sc:
        return list(res_[:n_out]) + [list(res_[n_out:])]
    return res_


def _piece_spec(tm, piece):
    _, w, c0, per_group = piece
    if per_group:
        return pl.BlockSpec((tm, w), lambda i, g: (i, c0 + g))
    return pl.BlockSpec((tm, w), lambda i, g: (i, c0))


def _const_spec(p):
    return pl.BlockSpec(p.shape, lambda i, g: (0, 0))


def _rowwise(name, fn, params, rows, auxs, outs, tm, groups=1):
    S = rows[0][0].shape[0]
    tm = min(tm, S)
    n_p, n_r, n_a = len(params), len(rows), len(auxs)

    def body(*refs):
        p = [r[...] for r in refs[:n_p]]
        r_ = [r[...] for r in refs[n_p:n_p + n_r]]
        a_ = [r[...] for r in refs[n_p + n_r:n_p + n_r + n_a]]
        for o_ref, o in zip(refs[n_p + n_r + n_a:], fn(p, r_, a_)):
            o_ref[...] = o.astype(o_ref.dtype)

    out_specs, out_shape = [], []
    for w, dt, per_group in outs:
        out_specs.append(_piece_spec(tm, (None, w, 0, per_group)))
        out_shape.append(jax.ShapeDtypeStruct((S, w * (groups if per_group else 1)), dt))
    return pl.pallas_call(
        body, name=name, grid=(S // tm, groups),
        in_specs=[_const_spec(p) for p in params] + [_piece_spec(tm, q) for q in list(rows) + list(auxs)],
        out_specs=out_specs, out_shape=out_shape, compiler_params=_params(),
    )(*params, *[q[0] for q in list(rows) + list(auxs)])


def _rowwise_vjp(name, fn, params, rows, auxs, cots, d_outs, tm, groups=1, adds=None):
    S = rows[0][0].shape[0]
    tm = min(tm, S)
    n_p, n_r, n_a = len(params), len(rows), len(auxs)
    cot_flat = [q for c in cots for q in c]
    adds = adds or [None] * len(d_outs)
    add_flat = [q for q in adds if q is not None]
    n_c, n_add = len(cot_flat), len(add_flat)
    shared = [not all(rows[k][3] for k in idx) and groups > 1 for idx, _ in d_outs]

    def body(*refs):
        pos = 0
        p = [r[...] for r in refs[pos:pos + n_p]]; pos += n_p
        r_ = [r[...] for r in refs[pos:pos + n_r]]; pos += n_r
        a_ = [r[...] for r in refs[pos:pos + n_a]]; pos += n_a
        c_refs = refs[pos:pos + n_c]; pos += n_c
        add_refs = list(refs[pos:pos + n_add]); pos += n_add
        d_refs = refs[pos:pos + len(d_outs)]; pos += len(d_outs)
        dp_refs = refs[pos:]
        i, g = pl.program_id(0), pl.program_id(1)
        outs, vjp_fn = jax.vjp(lambda pp, rr: fn(pp, rr, a_), p, r_)
        cts, ci = [], 0
        for c, o in zip(cots, outs):
            t = c_refs[ci][...].astype(F32)
            for extra in c_refs[ci + 1:ci + len(c)]:
                t = t + extra[...].astype(F32)
            ci += len(c)
            cts.append(t.astype(o.dtype))
        dp, dr = vjp_fn(cts)
        for (idx, _), d_ref, add, sh in zip(d_outs, d_refs, adds, shared):
            val = dr[idx[0]].astype(F32) if len(idx) == 1 else jnp.concatenate([dr[k].astype(F32) for k in idx], axis=1)
            if add is not None:
                val = val + add_refs.pop(0)[...].astype(F32)
            if sh:
                @pl.when(g == 0)
                def _(d_ref=d_ref):
                    d_ref[...] = jnp.zeros_like(d_ref)
                d_ref[...] += val.astype(d_ref.dtype)
            else:
                d_ref[...] = val.astype(d_ref.dtype)
        first = jnp.logical_and(i == 0, g == 0)
        for dp_ref, d in zip(dp_refs, dp):
            @pl.when(first)
            def _(dp_ref=dp_ref):
                dp_ref[...] = jnp.zeros_like(dp_ref)
            dp_ref[...] += d.astype(F32)

    out_specs, out_shape = [], []
    for (idx, dt), sh in zip(d_outs, shared):
        w = sum(rows[k][1] for k in idx)
        per_group = (not sh) and groups > 1
        out_specs.append(_piece_spec(tm, (None, w, 0, per_group)))
        out_shape.append(jax.ShapeDtypeStruct((S, w * (groups if per_group else 1)), dt))
    for p in params:
        out_specs.append(_const_spec(p))
        out_shape.append(jax.ShapeDtypeStruct(p.shape, F32))
    pieces = list(rows) + list(auxs) + cot_flat + add_flat
    res = pl.pallas_call(
        body, name=name, grid=(S // tm, groups),
        in_specs=[_const_spec(p) for p in params] + [_piece_spec(tm, q) for q in pieces],
        out_specs=out_specs, out_shape=out_shape, compiler_params=_params(),
    )(*params, *[q[0] for q in pieces])
    return list(res[:len(d_outs)]), list(res[len(d_outs):])


def _lane_roll(x, shift):
    @jax.custom_vjp
    def roll(v):
        return pltpu.roll(v, shift, 1)

    roll.defvjp(lambda v: (roll(v), None), lambda _, ct: (pltpu.roll(ct, LANES - shift, 1),))
    return roll(x)


@jax.custom_vjp
def _sigmoid(x):
    return 1.0 / (1.0 + jnp.exp(-x))


def _sigmoid_fwd(x):
    s = _sigmoid(x)
    return s, s


_sigmoid.defvjp(_sigmoid_fwd, lambda s, ct: (ct * s * (1.0 - s),))


def _rope(x, cos, sin_lo, sin_hi, half):
    return x * cos + _lane_roll(x, LANES - half) * sin_lo + _lane_roll(x, half) * sin_hi


def _f_rope_table(p, r, a):
    inv, first, second, fixed = p
    ang = a[0] * inv
    cs, sn = jnp.cos(ang), jnp.sin(ang)
    return [cs * (first + second) + fixed, -sn * first, sn * second]


def _f_rms(p, r, a):
    x = r[0].astype(F32)
    return [x * lax.rsqrt(jnp.mean(x * x, axis=-1, keepdims=True) + EPS) * p[0]]


def _f_mla_a(p, r, a):
    return _f_rms([p[0]], [r[0]], a) + _f_rms([p[1]], [r[1]], a)


def _f_mla_b(p, r, a):
    def norm_rope(v, g):
        ms = jnp.sum(v * v, axis=-1, keepdims=True) * (1.0 / MLA_QK)
        return _rope(v * lax.rsqrt(ms + EPS) * g, a[0], a[1], a[2], MLA_ROPE // 2)

    return [norm_rope(r[0].astype(F32), p[0]) * MLA_Q_SCALE, norm_rope(r[1].astype(F32) + r[2].astype(F32), p[1])]


def _f_ret_rope(p, r, a):
    q = _rope(r[0].astype(F32), a[0], a[1], a[2], RET_QK // 2)
    k = _rope(r[1].astype(F32), a[0], a[1], a[2], RET_QK // 2)
    return [q, k * (RET_QK ** -0.5)]


def _f_ret_post(p, r, a):
    ret = r[0].astype(F32) + r[1].astype(F32)
    g = r[2].astype(F32)
    normed = ret * lax.rsqrt(jnp.mean(ret * ret, axis=-1, keepdims=True) + EPS)
    return [g * _sigmoid(g) * normed]


def _f_merge(p, r, a):
    return [_sigmoid(r[0].astype(F32)) * r[2].astype(F32) + _sigmoid(r[1].astype(F32)) * r[3].astype(F32)]


def _f_swiglu(p, r, a):
    g = r[0].astype(F32)
    return [g * _sigmoid(g) * r[1].astype(F32)]


def _f_add(p, r, a):
    return [r[0].astype(F32) + r[1].astype(F32)]


def _flash_fwd(q, k, kv, shards):
    S = q.shape[0]
    tq = min(ATT_TQ, S)
    nq = S // tq
    n = len(shards)

    def body(q_ref, k_ref, v_ref, *rest):
        shard_refs, (o_ref, lse_ref), gathered = rest[:n], rest[n:n + 2], rest[n + 2:2 * n + 2]
        send_sems, recv_sems = rest[2 * n + 2:]
        h, qi = pl.program_id(0), pl.program_id(1)

        @pl.when(jnp.logical_and(h == 0, qi == 0))
        def _():
            _gather_start(_gather_copies(shard_refs, gathered, send_sems, recv_sems))

        for hh in range(hps):
            lanes = slice(hh * LANES, (hh + 1) * LANES)
            s = _dot(q_ref[:, lanes], k_ref[:, lanes], "nt")
            m = jnp.max(s, axis=-1, keepdims=True)
            p = jnp.exp2(s - m)
            l = jnp.sum(p, axis=-1, keepdims=True)
            o_ref[:, lanes] = (_dot(p.astype(BF16), v_ref[:, lanes]) / l).astype(o_ref.dtype)
            lse_ref[:, lanes] = jnp.broadcast_to(m + jnp.log2(l), (tq, LANES))

        @pl.when(jnp.logical_and(h == HEADS // hps - 1, qi == nq - 1))
        def _():
            _gather_wait(_gather_copies(shard_refs, gathered, send_sems, recv_sems))

    hps = ATT_HEADS_PER_STEP
    qs = pl.BlockSpec((tq, hps * LANES), lambda h, i: (i, h))
    res = pl.pallas_call(
        body, name="mla_fwd", grid=(HEADS // hps, nq),
        in_specs=[qs, pl.BlockSpec((S, hps * LANES), lambda h, i: (0, h), pipeline_mode=pl.Buffered(1)),
                  pl.BlockSpec((S, hps * LANES), lambda h, i: (0, HEADS // hps + h), pipeline_mode=pl.Buffered(1))]
        + [ANY] * n,
        out_specs=[qs, qs] + [ANY] * n,
        out_shape=[jax.ShapeDtypeStruct((S, HEADS * LANES), BF16), jax.ShapeDtypeStruct((S, HEADS * LANES), F32)]
        + [jax.ShapeDtypeStruct((4,) + s.shape, s.dtype) for s in shards],
        scratch_shapes=[pltpu.SemaphoreType.DMA((3 * n,)), pltpu.SemaphoreType.DMA((3 * n,))],
        compiler_params=_params(),
    )(q, k, kv, *shards)
    mine = 2 * lax.axis_index("x") + lax.axis_index("y")
    return res[0], res[1], [_fill_slot(g, s, mine) for g, s in zip(res[2:], shards)]


def _flash_bwd(q, k, kv, do, lse, o, gs):
    S = q.shape[0]
    tq, tk = min(ATT_BQ, S), min(ATT_BK, S)
    nq, nkt = S // tq, S // tk
    n = len(gs)

    def body(q_ref, k_ref, v_ref, do_ref, lse_ref, o_ref, *rest):
        g_refs, (dq_ref, dk_ref, dv_ref), got_refs = rest[:n], rest[n:n + 3], rest[n + 3:2 * n + 3]
        dk_sc, dv_sc, send_sems, recv_sems = rest[2 * n + 3:]
        h, ki, qi = pl.program_id(0), pl.program_id(1), pl.program_id(2)

        @pl.when(jnp.logical_and(h == 0, jnp.logical_and(ki == 0, qi == 0)))
        def _():
            _scatter_start(_scatter_copies(g_refs, got_refs, send_sems, recv_sems))

        @pl.when(jnp.logical_and(ki == 0, qi == 0))
        def _():
            dq_ref[...] = jnp.zeros_like(dq_ref)

        @pl.when(qi == 0)
        def _():
            dk_sc[...] = jnp.zeros_like(dk_sc)
            dv_sc[...] = jnp.zeros_like(dv_sc)

        rows = pl.ds(pl.multiple_of(qi * tq, tq), tq)
        for hh in range(hps):
            lanes = slice(hh * LANES, (hh + 1) * LANES)
            qv, kv_, dov = q_ref[:, lanes], k_ref[:, lanes], do_ref[:, lanes]
            p = jnp.exp2(_dot(qv, kv_, "nt") - lse_ref[:, lanes][:, :1])
            dp = _dot(dov, v_ref[:, lanes], "nt")
            delta = jnp.sum(dov.astype(F32) * o_ref[:, lanes].astype(F32), axis=-1, keepdims=True)
            ds = (p * (dp - delta) * LN2).astype(BF16)
            dv_sc[:, lanes] += _dot(p.astype(BF16), dov, "tn")
            dk_sc[:, lanes] += _dot(ds, qv, "tn")
            dq_ref[rows, lanes] += _dot(ds, kv_)

        @pl.when(qi == nq - 1)
        def _():
            dk_ref[...] = dk_sc[...].astype(dk_ref.dtype)
            dv_ref[...] = dv_sc[...].astype(dv_ref.dtype)

        @pl.when(jnp.logical_and(h == HEADS // hps - 1, jnp.logical_and(ki == nkt - 1, qi == nq - 1)))
        def _():
            _scatter_wait(_scatter_copies(g_refs, got_refs, send_sems, recv_sems))

    hps = ATT_BWD_HEADS_PER_STEP
    qs = pl.BlockSpec((tq, hps * LANES), lambda h, j, i: (i, h))
    ks = pl.BlockSpec((tk, hps * LANES), lambda h, j, i: (j, h))
    res = pl.pallas_call(
        body, name="mla_bwd", grid=(HEADS // hps, nkt, nq),
        in_specs=[qs, ks, pl.BlockSpec((tk, hps * LANES), lambda h, j, i: (j, HEADS // hps + h)), qs, qs, qs] + [ANY] * n,
        out_specs=[pl.BlockSpec((S, hps * LANES), lambda h, j, i: (0, h), pipeline_mode=pl.Buffered(1)), ks, ks] + [ANY] * n,
        out_shape=[jax.ShapeDtypeStruct((S, HEADS * LANES), F32), jax.ShapeDtypeStruct((S, HEADS * LANES), BF16),
                   jax.ShapeDtypeStruct((S, HEADS * LANES), BF16)]
        + [jax.ShapeDtypeStruct((8,) + g.shape[2:], g.dtype) for g in gs],
        scratch_shapes=[pltpu.VMEM((tk, hps * LANES), F32)] * 2
        + [pltpu.SemaphoreType.DMA((8 * n,)), pltpu.SemaphoreType.DMA((7 * n,))],
        compiler_params=_params(),
    )(q, k, kv, do, lse, o, *gs)
    return res[0], res[1], res[2], list(res[3:])


def _ret_tables(decay_row, backward):
    C = RET_CHUNK
    lg = -jnp.exp(decay_row)
    t = lax.broadcasted_iota(jnp.int32, (C, C), 0).astype(F32)
    s = lax.broadcasted_iota(jnp.int32, (C, C), 1).astype(F32)
    ridx = lax.broadcasted_iota(jnp.int32, (C, LANES), 0).astype(F32)
    if backward:
        dist, mask, aw, bw = s - t, s > t, C - ridx, ridx
    else:
        dist, mask, aw, bw = t - s, t >= s, ridx + 1.0, C - 1.0 - ridx
    dist = jnp.maximum(dist, 0.0)
    din = jnp.where(mask, jnp.exp(lg[:, :1] * dist), 0.0)
    return dict(din=din, dist=dist, a=jnp.exp(lg * aw), b=jnp.exp(lg * bw), c=jnp.exp(lg * C), aw=aw, bw=bw)


def _ret_fill_tables(decs, din_sc, a_sc, b_sc):
    for d, dec in enumerate(decs):
        for h in range(HEADS):
            tb = _ret_tables(dec[h:h + 1, :], d == 1)
            din_sc[d, h], a_sc[d, h], b_sc[d, h] = tb["din"], tb["a"], tb["b"]


def _ret_fwd(qr, kr, proj, v_block, dec_f, dec_b):
    S = qr.shape[0]
    C = RET_CHUNK
    n = S // C
    nc = min(RET_CHUNKS_PER_STEP, n)
    nb = n // nc
    W = HEADS * LANES

    def body(qf, kf, vf, qb, kb, vb, df, db, of, ob, sf_out, sb_out, st, din_sc, a_sc, b_sc):
        @pl.when(pl.program_id(0) == 0)
        def _():
            st[...] = jnp.zeros_like(st)
            _ret_fill_tables((df, db), din_sc, a_sc, b_sc)

        for d, (q_ref, k_ref, v_ref, dec, o_ref, s_out) in enumerate(
                [(qf, kf, vf, df, of, sf_out), (qb, kb, vb, db, ob, sb_out)]):
            for h in range(HEADS):
                lanes = slice(h * LANES, (h + 1) * LANES)
                din, a, b = din_sc[d, h], a_sc[d, h], b_sc[d, h]
                c = jnp.exp(-jnp.exp(dec[h:h + 1, :]) * C)
                for ci in (range(nc) if d == 0 else reversed(range(nc))):
                    rows = slice(ci * C, (ci + 1) * C)
                    qf32, kf32, v = q_ref[rows, lanes].astype(F32), k_ref[rows, lanes].astype(F32), v_ref[rows, lanes]
                    state = st[d, h]
                    s_out[ci, h] = state
                    inner = _dot((_dot(qf32.astype(BF16), kf32.astype(BF16), "nt") * din).astype(BF16), v)
                    cross = _dot((qf32 * a).astype(BF16), state.astype(BF16))
                    o_ref[rows, lanes] = inner + cross
                    st[d, h] = state * c + _dot((kf32 * b).astype(BF16), v, "tn")

    fw = lambda c0: pl.BlockSpec((nc * C, W), lambda j: (j, c0))
    bw = lambda c0: pl.BlockSpec((nc * C, W), lambda j: (nb - 1 - j, c0))
    dec_spec = pl.BlockSpec((HEADS, LANES), lambda j: (0, 0))
    st_shape = jax.ShapeDtypeStruct((n, HEADS, LANES, LANES), F32)
    return pl.pallas_call(
        body, name="ret_fwd", grid=(nb,),
        in_specs=[fw(0), fw(0), fw(v_block), bw(0), bw(0), bw(v_block), dec_spec, dec_spec],
        out_specs=[fw(0), bw(0), pl.BlockSpec((nc, HEADS, LANES, LANES), lambda j: (j, 0, 0, 0)),
                   pl.BlockSpec((nc, HEADS, LANES, LANES), lambda j: (nb - 1 - j, 0, 0, 0))],
        out_shape=[jax.ShapeDtypeStruct((S, W), F32)] * 2 + [st_shape] * 2,
        scratch_shapes=[pltpu.VMEM((2, HEADS, LANES, LANES), F32), pltpu.VMEM((2, HEADS, C, C), F32),
                        pltpu.VMEM((2, HEADS, C, LANES), F32), pltpu.VMEM((2, HEADS, C, LANES), F32)],
        compiler_params=_params(),
    )(qr, kr, proj, qr, kr, proj, dec_f, dec_b)


def _ret_bwd(qr, kr, proj, v_block, dret, sf, sb, dec_f, dec_b):
    S = qr.shape[0]
    C = RET_CHUNK
    n = S // C
    nc = min(RET_CHUNKS_PER_STEP, n)
    nb = n // nc
    W = HEADS * LANES

    def body(qf, kf, vf, gf, sf_ref, qb, kb, vb, gb, sb_ref, df, db,
             dqf, dkf, dvf, dqb, dkb, dvb, ddf, ddb, ds_sc, din_sc, a_sc, b_sc):
        j = pl.program_id(0)

        @pl.when(j == 0)
        def _():
            ds_sc[...] = jnp.zeros_like(ds_sc)
            ddf[...] = jnp.zeros_like(ddf)
            ddb[...] = jnp.zeros_like(ddb)
            _ret_fill_tables((df, db), din_sc, a_sc, b_sc)

        for d, (q_ref, k_ref, v_ref, g_ref, s_ref, dec, dq_ref, dk_ref, dv_ref, dd_ref) in enumerate(
                [(qf, kf, vf, gf, sf_ref, df, dqf, dkf, dvf, ddf), (qb, kb, vb, gb, sb_ref, db, dqb, dkb, dvb, ddb)]):
            static = _ret_tables(dec[0:1, :], d == 1)
            dist, aw, bw_ = static["dist"], static["aw"], static["bw"]
            for h in range(HEADS):
                lanes = slice(h * LANES, (h + 1) * LANES)
                din, a, b = din_sc[d, h], a_sc[d, h], b_sc[d, h]
                c = jnp.exp(-jnp.exp(dec[h:h + 1, :]) * C)
                dlg = jnp.zeros((1, 1), F32)
                for ci in (reversed(range(nc)) if d == 0 else range(nc)):
                    rows = slice(ci * C, (ci + 1) * C)
                    v, g = v_ref[rows, lanes], g_ref[rows, lanes]
                    qf32, kf32 = q_ref[rows, lanes].astype(F32), k_ref[rows, lanes].astype(F32)
                    q, k = qf32.astype(BF16), kf32.astype(BF16)
                    state, dstate = s_ref[ci, h], ds_sc[d, h]
                    dstate_b = dstate.astype(BF16)
                    dp = _dot(g, v, "nt")
                    a_ = _dot(q, k, "nt")
                    da = (dp * din).astype(BF16)
                    g1 = _dot(g, state.astype(BF16), "nt")
                    g2 = _dot(v, dstate_b, "nt")
                    dq_ref[rows, lanes] = (_dot(da, k) + g1 * a).astype(dq_ref.dtype)
                    dk_ref[rows, lanes] = (_dot(da, q, "tn") + g2 * b).astype(dk_ref.dtype)
                    dv_ref[rows, lanes] = (_dot((a_ * din).astype(BF16), g, "tn")
                                           + _dot((kf32 * b).astype(BF16), dstate_b)).astype(dv_ref.dtype)
                    dlg = dlg + (jnp.sum(dp * a_ * din * dist, keepdims=True)
                                 + jnp.sum(g1 * qf32 * a * aw, keepdims=True)
                                 + jnp.sum(g2 * kf32 * b * bw_, keepdims=True)
                                 + C * jnp.sum(c * dstate * state, keepdims=True))
                    ds_sc[d, h] = dstate * c + _dot((qf32 * a).astype(BF16), g, "tn")
                dd_ref[h:h + 1, :] += jnp.broadcast_to(dlg, (1, LANES))

        @pl.when(j == nb - 1)
        def _():
            ddf[...] = ddf[...] * -jnp.exp(df[...])
            ddb[...] = ddb[...] * -jnp.exp(db[...])

    fw = lambda c0: pl.BlockSpec((nc * C, W), lambda j: (nb - 1 - j, c0))
    bw = lambda c0: pl.BlockSpec((nc * C, W), lambda j: (j, c0))
    dec_spec = pl.BlockSpec((HEADS, LANES), lambda j: (0, 0))
    act = jax.ShapeDtypeStruct((S, W), BF16)
    return pl.pallas_call(
        body, name="ret_bwd", grid=(nb,),
        in_specs=[fw(0), fw(0), fw(v_block), fw(0), pl.BlockSpec((nc, HEADS, LANES, LANES), lambda j: (nb - 1 - j, 0, 0, 0)),
                  bw(0), bw(0), bw(v_block), bw(0), pl.BlockSpec((nc, HEADS, LANES, LANES), lambda j: (j, 0, 0, 0)),
                  dec_spec, dec_spec],
        out_specs=[fw(0)] * 3 + [bw(0)] * 3 + [dec_spec] * 2,
        out_shape=[act] * 6 + [jax.ShapeDtypeStruct((HEADS, LANES), F32)] * 2,
        scratch_shapes=[pltpu.VMEM((2, HEADS, LANES, LANES), F32), pltpu.VMEM((2, HEADS, C, C), F32),
                        pltpu.VMEM((2, HEADS, C, LANES), F32), pltpu.VMEM((2, HEADS, C, LANES), F32)],
        compiler_params=_params(),
    )(qr, kr, proj, dret, sf, qr, kr, proj, dret, sb, dec_f, dec_b)


def _pad_heads(w, hd):
    K = w.shape[0]
    return jnp.pad(w.reshape(K, HEADS, hd), ((0, 0), (0, 0), (0, LANES - hd))).reshape(K, HEADS * LANES)


def _unpad_heads(w, hd):
    K = w.shape[0]
    return w.reshape(K, HEADS, LANES)[:, :, :hd].reshape(K, HEADS * hd)


def _rope_consts(first_lane, half):
    lane = np.arange(LANES)
    first = ((lane >= first_lane) & (lane < first_lane + half)).astype(np.float32)
    second = ((lane >= first_lane + half) & (lane < first_lane + 2 * half)).astype(np.float32)
    fixed = (lane < first_lane).astype(np.float32)
    j = np.where(first > 0, lane - first_lane, lane - first_lane - half) * (first + second)
    inv = (ROPE_THETA ** (-j.astype(np.float64) / half)).astype(np.float32)
    return [jnp.asarray(v.reshape(1, LANES), F32) for v in (inv, first, second, fixed)]


def _assemble(name, gathered):
    if name in COL_SHARDED:
        return jnp.transpose(gathered, (1, 0, 2)).reshape(gathered.shape[1], 4 * gathered.shape[2])
    return gathered.reshape(4 * gathered.shape[1], gathered.shape[2])


def _split_for_reducers(name, g, dtype):
    if name in COL_SHARDED:
        K, N4 = g.shape
        return jnp.transpose(g.reshape(2, K // 2, 4, N4 // 4), (2, 0, 1, 3)).astype(dtype)
    return g.reshape(4, 2, g.shape[0] // 8, g.shape[1]).astype(dtype)


def _local_step(x, tab_m, tab_r, tgt, wts, late_shards, small):
    w_in = wts["w_in"]
    seg = [w_in[:, IN_OFFS[i]:IN_OFFS[i + 1]] for i in range(8)]
    kr_w = jnp.pad(seg[2], ((0, 0), (MLA_NOPE, LANES - MLA_QK)))
    w_in_p = jnp.concatenate([seg[7], seg[5], seg[6], _pad_heads(seg[3], RET_QK), _pad_heads(seg[4], RET_QK),
                              seg[0], seg[1], kr_w], axis=1)
    w_qb_p = _pad_heads(wts["w_q_b"], MLA_QK)
    kvw = wts["w_kv_b"].reshape(MLA_KV_RANK, HEADS, MLA_NOPE + MLA_V)
    pad_kv = lambda t: jnp.pad(t, ((0, 0), (0, 0), (0, LANES - t.shape[2]))).reshape(MLA_KV_RANK, HEADS * LANES)
    w_kn_p, w_v_p = pad_kv(kvw[:, :, :MLA_NOPE]), pad_kv(kvw[:, :, MLA_NOPE:])
    w_kv_p = jnp.concatenate([w_kn_p, w_v_p], axis=1)
    g_qn_p = jnp.pad(small["g_qn"], ((0, 0), (0, LANES - MLA_QK)))
    g_kn_p = jnp.pad(small["g_kn"], ((0, 0), (0, LANES - MLA_QK)))
    dec_f = jnp.broadcast_to(small["ret_decay_fwd"].reshape(HEADS, 1), (HEADS, LANES))
    dec_b = jnp.broadcast_to(small["ret_decay_bwd"].reshape(HEADS, 1), (HEADS, LANES))
    T, N = True, False
    RT, HT = ROW_TILE, HEAD_ROW_TILE
    RW = 2 * ROW_TILE

    aux_m = [(t, LANES, 0, N) for t in tab_m]
    aux_r = [(t, LANES, 0, N) for t in tab_r]

    proj, h = _mm("proj", x, w_in_p, "nn", BF16, a_gain=small["g_mix"])
    rows_a = [(proj, MLA_Q_RANK, 24, N), (proj, MLA_KV_RANK, 50, N)]
    cqn, ckvn = _rowwise("mla_lat_norm", _f_mla_a, [small["g_q_a"], small["g_kv_a"]], rows_a, [],
                         [(MLA_Q_RANK, BF16, N), (MLA_KV_RANK, BF16, N)], RW)
    qraw = _mm("mla_q_up", cqn, w_qb_p, "nn", BF16)
    kv = _mm("mla_kv_up", ckvn, w_kv_p, "nn", BF16)
    rows_b = [(qraw, LANES, 0, T), (kv, LANES, 0, T), (proj, LANES, 51, N)]
    q, k = _rowwise("mla_qk_norm_rope", _f_mla_b, [g_qn_p, g_kn_p], rows_b, aux_m, [(LANES, BF16, T)] * 2, HT, HEADS)
    o, lse, late = _flash_fwd(q, k, kv, [late_shards[n] for n in LATE])
    wl = {n: _assemble(n, g) for n, g in zip(LATE, late)}
    w_mla_p = jnp.pad(wl["w_mla_out"].reshape(HEADS, MLA_V, D_MODEL), ((0, 0), (0, LANES - MLA_V), (0, 0))).reshape(HEADS * LANES, D_MODEL)
    w_ret_out, w_out, w_gu, w_down = wl["w_ret_out"], wl["w_out"], wl["w_gate_up"], wl["w_down"]
    y_a = _mm("mla_out", o, w_mla_p, "nn", BF16)
    rows_rr = [(proj, LANES, 32, T), (proj, LANES, 40, T)]
    qr, kr = _rowwise("ret_rope", _f_ret_rope, [], rows_rr, aux_r, [(LANES, RET_QK_DTYPE, T)] * 2, HT, HEADS)
    ret_f, ret_b, st_f, st_b = _ret_fwd(qr, kr, proj, 2, dec_f, dec_b)
    rows_rp = [(ret_f, LANES, 0, T), (ret_b, LANES, 0, T), (proj, LANES, 24, T)]
    (o_b,) = _rowwise("ret_post", _f_ret_post, [], rows_rp, [], [(LANES, BF16, T)], HT, HEADS)
    y_b = _mm("ret_out", o_b, w_ret_out, "nn", BF16)
    rows_m = [(proj, D_MODEL, 0, N), (proj, D_MODEL, 1, N), (y_a, D_MODEL, 0, N), (y_b, D_MODEL, 0, N)]
    (merged,) = _rowwise("merge", _f_merge, [], rows_m, [], [(D_MODEL, BF16, N)], RW)
    x2 = _mm("mix_out", merged, w_out, "nn", F32, res=x)
    gu, h2 = _mm("ffn_gate_up", x2, w_gu, "nn", BF16, a_gain=small["g_ffn"])
    rows_sw = [(gu, FFN_HIDDEN, 0, N), (gu, FFN_HIDDEN, 1, N)]
    (act,) = _rowwise("swiglu", _f_swiglu, [], rows_sw, [], [(FFN_HIDDEN, BF16, N)], RT)
    dy, dy_b16, loss_row = _mm("ffn_down", act, w_down, "nn", None,
                               epilogue=(_epi_loss, [x2, tgt], [], [F32, BF16], [(1, LANES)]))

    dact = _mm("d_act", dy_b16, w_down, "nt", BF16)
    dw_down = _mm("dw_down", act, dy_b16, "tn", BF16)
    (dgu,), _ = _rowwise_vjp("swiglu_bwd", _f_swiglu, [], rows_sw, [], [[(dact, FFN_HIDDEN, 0, N)]], [([0, 1], BF16)], RT)
    dx2, dx2_b16, dg_ffn = _mm("d_h2", dgu, w_gu, "nt", None,
                               epilogue=(_epi_rms_bwd(2), [x2, dy], [small["g_ffn"]], [F32, BF16], [(1, D_MODEL)]))
    dw_gu = _mm("dw_gate_up", h2, dgu, "tn", BF16, shard_out=True)
    dmerged = _mm("d_merged", dx2_b16, w_out, "nt", BF16)
    dw_out = _mm("dw_out", merged, dx2_b16, "tn", BF16)
    (dgl, dy_a, dy_b), _ = _rowwise_vjp("merge_bwd", _f_merge, [], rows_m, [], [[(dmerged, D_MODEL, 0, N)]],
                                        [([0, 1], BF16), ([2], BF16), ([3], BF16)], RW)
    do_b = _mm("d_ret_o", dy_b, w_ret_out, "nt", BF16)
    dw_ret_out = _mm("dw_ret_out", o_b, dy_b, "tn", BF16)
    (dret, dg_r), _ = _rowwise_vjp("ret_post_bwd", _f_ret_post, [], rows_rp, [], [[(do_b, LANES, 0, T)]],
                                   [([0], BF16), ([2], BF16)], HT, HEADS)
    dqf, dkf, dvf, dqb, dkb, dvb, ddec_f, ddec_b = _ret_bwd(qr, kr, proj, 2, dret, st_f, st_b, dec_f, dec_b)
    (dq_r, dk_r), _ = _rowwise_vjp("ret_rope_bwd", _f_ret_rope, [], rows_rr, aux_r,
                                   [[(dqf, LANES, 0, T), (dqb, LANES, 0, T)], [(dkf, LANES, 0, T), (dkb, LANES, 0, T)]],
                                   [([0], BF16), ([1], BF16)], HT, HEADS)
    (dv_r,) = _rowwise("ret_dv_sum", _f_add, [], [(dvf, D_MODEL, 0, N), (dvb, D_MODEL, 0, N)], [], [(D_MODEL, BF16, N)], RW)
    do = _mm("d_mla_o", dy_a, w_mla_p, "nt", BF16)
    dw_mla_p = _mm("dw_mla_out", o, dy_a, "tn", BF16)
    dw_mla = dw_mla_p.reshape(HEADS, LANES, D_MODEL)[:, :MLA_V].reshape(HEADS * MLA_V, D_MODEL)
    late_grads = {"w_mla_out": dw_mla, "w_ret_out": dw_ret_out, "w_out": dw_out, "w_down": dw_down}
    late_gs = [dw_gu if n == "w_gate_up" else _split_for_reducers(n, late_grads[n], BF16) for n in LATE]
    dq, dk, dv, late_got = _flash_bwd(q, k, kv, do, lse, o, late_gs)
    (dqraw, dkn, dkr), (dg_qn_p, dg_kn_p) = _rowwise_vjp(
        "mla_qk_norm_rope_bwd", _f_mla_b, [g_qn_p, g_kn_p], rows_b, aux_m, [[(dq, LANES, 0, T)], [(dk, LANES, 0, T)]],
        [([0], BF16), ([1], BF16), ([2], F32)], HT, HEADS)
    dckvn = _mm("d_ckvn_v", dv, w_v_p, "nt", BF16, res=_mm("d_ckvn_k", dkn, w_kn_p, "nt", F32))
    dw_kn_p = _mm("dw_kv_k", ckvn, dkn, "tn", BF16)
    dw_v_p = _mm("dw_kv_v", ckvn, dv, "tn", BF16)
    dcqn = _mm("d_cqn", dqraw, w_qb_p, "nt", BF16)
    dw_qb_p = _mm("dw_q_b", cqn, dqraw, "tn", BF16)
    (dcq, dckv), (dg_q_a, dg_kv_a) = _rowwise_vjp(
        "mla_lat_norm_bwd", _f_mla_a, [small["g_q_a"], small["g_kv_a"]], rows_a, [],
        [[(dcqn, MLA_Q_RANK, 0, N)], [(dckvn, MLA_KV_RANK, 0, N)]], [([0], BF16), ([1], BF16)], RW)
    dproj = jnp.concatenate([dgl, dv_r, dg_r, dq_r, dk_r, dcq, dckv, dkr.astype(BF16)], axis=1)
    dw_in_p = _mm("dw_in", h, dproj, "tn", BF16)

    c = lambda a, b_: dw_in_p[:, a:b_]
    dw_in = jnp.concatenate([c(6144, 6400), c(6400, 6528), c(6528 + MLA_NOPE, 6528 + MLA_QK), _unpad_heads(c(4096, 5120), RET_QK),
                             _unpad_heads(c(5120, 6144), RET_QK), c(2048, 3072), c(3072, 4096), c(0, 2048)], axis=1)
    un_kv = lambda t: t.reshape(MLA_KV_RANK, HEADS, LANES)[:, :, :MLA_NOPE]
    dw_kv = jnp.concatenate([un_kv(dw_kn_p), un_kv(dw_v_p)], axis=2).reshape(MLA_KV_RANK, HEADS * (MLA_NOPE + MLA_V))
    grads = {"w_in": dw_in, "w_q_b": _unpad_heads(dw_qb_p, MLA_QK), "w_kv_b": dw_kv}
    dx, dg_mix, first_got = _mm("d_h", dproj, w_in_p, "nt", None,
                                epilogue=(_epi_rms_bwd(1), [x, dx2], [small["g_mix"]], [F32], [(1, D_MODEL)]),
                                scatter=[_split_for_reducers(n, grads[n], BF16) for n in FIRST])
    sgrads = {"g_mix": dg_mix, "g_q_a": dg_q_a, "g_kv_a": dg_kv_a, "g_qn": dg_qn_p[:, :MLA_QK], "g_kn": dg_kn_p[:, :MLA_QK],
              "ret_decay_fwd": ddec_f[:, 0].reshape(1, HEADS), "ret_decay_bwd": ddec_b[:, 0].reshape(1, HEADS), "g_ffn": dg_ffn}
    return loss_row, dx, first_got + late_got, sgrads


def _coords():
    return lax.axis_index("x"), lax.axis_index("y"), lax.axis_index("c")


def _other_chips(x, y):
    return [(1 - x, y), (x, 1 - y), (1 - x, 1 - y)]


ANY = pl.BlockSpec(memory_space=pl.ANY)


def _gather_copies(ins, outs, send_sems, recv_sems):
    x, y, c = _coords()
    mine = 2 * x + y
    sends, arrivals = [], []
    for w in range(len(ins)):
        for j, (cx, cy) in enumerate(_other_chips(x, y)):
            sems = dict(send_sem=send_sems.at[3 * w + j], recv_sem=recv_sems.at[3 * w + j],
                        device_id=(cx, cy, c), device_id_type=MESH)
            sends.append(pltpu.make_async_remote_copy(src_ref=ins[w], dst_ref=outs[w].at[mine], **sems))
            arrivals.append(functools.partial(pltpu.make_async_remote_copy, src_ref=ins[w],
                                              dst_ref=outs[w].at[2 * cx + cy], **sems))
    return sends, arrivals


def _gather_start(copies):
    for cp in list(copies[0]) + list(copies[2] if len(copies) > 2 else []):
        cp.start()


def _gather_wait(copies):
    sends, arrivals = copies[:2]
    for make in arrivals:
        make().wait_recv()
    for cp in sends:
        cp.wait_send()
    for cp in (copies[2] if len(copies) > 2 else []):
        cp.wait()


def _fill_slot(buf, piece, slot):
    idx = lax.broadcasted_iota(jnp.int32, (buf.shape[0],) + (1,) * piece.ndim, 0)
    return jnp.where(idx == slot, piece[None], buf)


def _rope_tables_and_first_gather(pos, consts_mla, consts_ret, shards):
    S = pos.shape[0]
    tm = min(HEAD_ROW_TILE, S)
    nt = S // tm
    n = len(shards)

    def body(pos_ref, *refs):
        consts, ins = (refs[:4], refs[4:8]), refs[8:8 + n]
        tabs, outs = refs[8 + n:14 + n], refs[14 + n:14 + 2 * n]
        send_sems, recv_sems = refs[14 + 2 * n:]
        i = pl.program_id(0)
        x, y, c = _coords()
        chips = _other_chips(x, y)
        mine = 2 * x + y

        def half(ref, slot, core):
            rows = ref.shape[1] // 2
            return ref.at[slot, pl.ds(pl.multiple_of(core * rows, 8), rows)]

        def copy(w, k, slot, core, to, src=None):
            return pltpu.make_async_remote_copy(
                src_ref=half(outs[w], slot, core) if src is None else src, dst_ref=half(outs[w], slot, core),
                send_sem=send_sems.at[6 * w + k], recv_sem=recv_sems.at[6 * w + k], device_id=to, device_id_type=MESH)

        def first(w, j):
            rows = ins[w].shape[0] // 2
            return copy(w, j, mine, c, (*chips[j], c), src=ins[w].at[pl.ds(pl.multiple_of(c * rows, 8), rows)])

        @pl.when(i == 0)
        def _():
            for w in range(n):
                for j in range(3):
                    first(w, j).start()

        for k in range(2):
            vals = _f_rope_table([r[...] for r in consts[k]], None, [pos_ref[...]])
            for t_ref, v in zip(tabs[3 * k:3 * k + 3], vals):
                t_ref[...] = v

        @pl.when(i == nt - 1)
        def _():
            passed = []
            for w in range(n):
                for j, (cx, cy) in enumerate(chips):
                    copy(w, j, 2 * cx + cy, c, (x, y, c)).wait_recv()
                    cp = copy(w, 3 + j, 2 * cx + cy, c, (x, y, 1 - c))
                    cp.start()
                    passed.append(cp)
            for w in range(n):
                for j, (cx, cy) in enumerate(chips):
                    copy(w, 3 + j, 2 * cx + cy, 1 - c, (x, y, c)).wait_recv()
            for w in range(n):
                for j in range(3):
                    first(w, j).wait_send()
            for cp in passed:
                cp.wait_send()

    const = lambda p: pl.BlockSpec(p.shape, lambda i: (0, 0))
    tab = pl.BlockSpec((tm, LANES), lambda i: (i, 0))
    res = pl.pallas_call(
        body, name="rope_tables_first_gather", grid=(nt,),
        in_specs=[pl.BlockSpec((tm, 1), lambda i: (i, 0))] + [const(p) for p in list(consts_mla) + list(consts_ret)] + [ANY] * n,
        out_specs=[tab] * 6 + [ANY] * n,
        out_shape=[jax.ShapeDtypeStruct((S, LANES), F32)] * 6 + [jax.ShapeDtypeStruct((4,) + s.shape, s.dtype) for s in shards],
        scratch_shapes=[pltpu.SemaphoreType.DMA((6 * n,)), pltpu.SemaphoreType.DMA((6 * n,))],
        compiler_params=_params(),
    )(pos, *consts_mla, *consts_ret, *shards)
    return list(res[:3]), list(res[3:6]), list(res[6:])


def _scatter_copies(ins, outs, send_sems, recv_sems):
    x, y, c = _coords()
    me = 4 * x + 2 * y + c
    n = len(ins)
    sends, arrivals = [], []
    local = [pltpu.make_async_copy(ins[w].at[2 * x + y, c], outs[w].at[me], send_sems.at[7 * n + w]) for w in range(n)]
    for w in range(n):
        for k in range(1, 8):
            px, py, pc = x ^ (k >> 2), y ^ ((k >> 1) & 1), c ^ (k & 1)
            sems = dict(send_sem=send_sems.at[7 * w + k - 1], recv_sem=recv_sems.at[7 * w + k - 1],
                        device_id=(px, py, pc), device_id_type=MESH)
            sends.append(pltpu.make_async_remote_copy(src_ref=ins[w].at[2 * px + py, pc], dst_ref=outs[w].at[me], **sems))
            arrivals.append(functools.partial(
                pltpu.make_async_remote_copy, src_ref=ins[w].at[2 * px + py, pc],
                dst_ref=outs[w].at[4 * px + 2 * py + pc], **sems))
    return sends, arrivals, local


_scatter_start, _scatter_wait = _gather_start, _gather_wait


def _grad_sum8(name, got):
    _, R, W = got.shape
    tr = _pick(R, 256, 16)

    def body(g_ref, o_ref):
        total = g_ref[0].astype(F32)
        for d in range(1, 8):
            total = total + g_ref[d].astype(F32)
        o_ref[...] = total

    return pl.pallas_call(
        body, name=name, grid=(R // tr,), in_specs=[pl.BlockSpec((8, tr, W), lambda i: (0, i, 0))],
        out_specs=pl.BlockSpec((tr, W), lambda i: (i, 0)), out_shape=jax.ShapeDtypeStruct((R, W), F32),
        compiler_params=_params(),
    )(got)


def _half_exchange(halves):
    n = len(halves)

    def body(*refs):
        ins, outs, send_sems, recv_sems = refs[:n], refs[n:2 * n], refs[2 * n], refs[2 * n + 1]
        x, y, c = _coords()
        sends = []
        for w in range(n):
            cp = pltpu.make_async_remote_copy(
                src_ref=ins[w], dst_ref=outs[w], send_sem=send_sems.at[w], recv_sem=recv_sems.at[w],
                device_id=(x, y, 1 - c), device_id_type=MESH)
            cp.start()
            sends.append(cp)
        for cp in sends:
            cp.wait()

    got = pl.pallas_call(
        body, name="grad_half_exchange", in_specs=[ANY] * n, out_specs=[ANY] * n,
        out_shape=[jax.ShapeDtypeStruct(h.shape, F32) for h in halves],
        scratch_shapes=[pltpu.SemaphoreType.DMA((n,)), pltpu.SemaphoreType.DMA((n,))],
    )(*halves)
    c = lax.axis_index("c")
    return [jnp.where(c == 0, jnp.stack([mine, theirs]), jnp.stack([theirs, mine])) for mine, theirs in zip(halves, got)]


def _adamw_math(w, g, m, v):
    m2 = ADAM_B1 * m + (1.0 - ADAM_B1) * g
    v2 = ADAM_B2 * v + (1.0 - ADAM_B2) * (g * g)
    m_hat = m2 / (1.0 - ADAM_B1 ** ADAM_STEP)
    v_hat = v2 / (1.0 - ADAM_B2 ** ADAM_STEP)
    return -ADAM_LR * (m_hat / (jnp.sqrt(v_hat) + ADAM_EPS) + ADAM_WD * w), m2, v2


def _small_allreduce_adamw(pack_g, pack_w, pack_m, pack_v):
    def body(g_ref, w_ref, m_ref, v_ref, sum_ref, d_ref, m_out, v_out, land, send_sems, recv_sems):
        x, y, c = _coords()
        me = 4 * x + 2 * y + c
        land[me] = g_ref[...]
        sends = []
        for k in range(1, 8):
            peer = (x ^ (k >> 2), y ^ ((k >> 1) & 1), c ^ (k & 1))
            cp = pltpu.make_async_remote_copy(
                src_ref=g_ref, dst_ref=land.at[me], send_sem=send_sems.at[k - 1], recv_sem=recv_sems.at[k - 1],
                device_id=peer, device_id_type=MESH)
            cp.start()
            sends.append((cp, peer))
        for k, (cp, peer) in enumerate(sends):
            pltpu.make_async_remote_copy(
                src_ref=g_ref, dst_ref=land.at[4 * peer[0] + 2 * peer[1] + peer[2]], send_sem=send_sems.at[k],
                recv_sem=recv_sems.at[k], device_id=peer, device_id_type=MESH).wait_recv()
        for cp, _ in sends:
            cp.wait_send()
        total = land[0]
        for d in range(1, 8):
            total = total + land[d]
        sum_ref[...] = total
        d_ref[...], m_out[...], v_out[...] = _adamw_math(w_ref[...], total, m_ref[...], v_ref[...])

    vm = pl.BlockSpec(memory_space=pltpu.VMEM)
    shp = jax.ShapeDtypeStruct(pack_g.shape, F32)
    return pl.pallas_call(
        body, name="small_allreduce_adamw", in_specs=[vm] * 4, out_specs=[vm] * 4, out_shape=[shp] * 4,
        scratch_shapes=[pltpu.VMEM((8,) + pack_g.shape, F32), pltpu.SemaphoreType.DMA((7,)), pltpu.SemaphoreType.DMA((7,))],
    )(pack_g, pack_w, pack_m, pack_v)


def _adamw(name, w, g, m, v):
    R, C = w.shape
    tr = _pick(R, 256, 8)

    def body(w_ref, g_ref, m_ref, v_ref, d_out, m_out, v_out):
        d_out[...], m_out[...], v_out[...] = _adamw_math(w_ref[...], g_ref[...], m_ref[...], v_ref[...])

    spec = pl.BlockSpec((tr, C), lambda i: (i, 0))
    return pl.pallas_call(
        body, name=name, grid=(R // tr,), in_specs=[spec] * 4, out_specs=[spec] * 3,
        out_shape=[jax.ShapeDtypeStruct((R, C), F32)] * 3, compiler_params=_params(),
    )(w, g, m, v)


def _pack_small(vals, last):
    flat = jnp.concatenate([v.reshape(-1) for v in vals] + [last.reshape(-1)])
    return jnp.pad(flat, (0, SMALL_ROWS * LANES - flat.shape[0])).reshape(SMALL_ROWS, LANES)


def kernel(x, positions, g_mix, w_in, g_q_a, w_q_b, g_kv_a, w_kv_b, g_qn, g_kn, w_mla_out, ret_decay_fwd, ret_decay_bwd, w_ret_out, w_out, g_ffn, w_gate_up, w_down, loss_target, m_g_mix, m_w_in, m_g_q_a, m_w_q_b, m_g_kv_a, m_w_kv_b, m_g_qn, m_g_kn, m_w_mla_out, m_ret_decay_fwd, m_ret_decay_bwd, m_w_ret_out, m_w_out, m_g_ffn, m_w_gate_up, m_w_down, v_g_mix, v_w_in, v_g_q_a, v_w_q_b, v_g_kv_a, v_w_kv_b, v_g_qn, v_g_kn, v_w_mla_out, v_ret_decay_fwd, v_ret_decay_bwd, v_w_ret_out, v_w_out, v_g_ffn, v_w_gate_up, v_w_down):
    given = dict(locals())
    S = x.shape[1]
    xs, tgt = x.reshape(S, D_MODEL), loss_target.reshape(S, D_MODEL)
    pos = positions.reshape(S, 1).astype(F32)

    first_shards = [given[n].astype(BF16) for n in FIRST]
    my_chip = 2 * lax.axis_index("x") + lax.axis_index("y")
    tab_m, tab_r, gathered = _rope_tables_and_first_gather(
        pos, _rope_consts(MLA_NOPE, MLA_ROPE // 2), _rope_consts(0, RET_QK // 2), first_shards)
    wts = {n: _assemble(n, _fill_slot(g, s, my_chip)) for n, g, s in zip(FIRST, gathered, first_shards)}
    late_shards = {n: given[n].astype(BF16) for n in LATE}
    small = {n: given[n].reshape(1, -1) for n in SMALL}

    loss_row, dx, pieces, sgrads = _local_step(xs, tab_m, tab_r, tgt, wts, late_shards, small)

    halves = [_grad_sum8("grad_sum_" + n, got) for n, got in zip(FIRST + LATE, pieces)]
    reduced = _half_exchange(halves)

    out = {}
    for n, r in zip(FIRST + LATE, reduced):
        g = r.reshape(given[n].shape)
        out["grad_" + n] = g
        out["delta_" + n], out["new_m_" + n], out["new_v_" + n] = _adamw("adamw_" + n, given[n], g, given["m_" + n], given["v_" + n])

    one = jnp.ones((1,), F32)
    pk = _small_allreduce_adamw(
        _pack_small([sgrads[n] for n in SMALL], loss_row[0, :1]),
        _pack_small([given[n] for n in SMALL], 0 * one),
        _pack_small([given["m_" + n] for n in SMALL], 0 * one),
        _pack_small([given["v_" + n] for n in SMALL], one))
    off = 0
    for n in SMALL:
        sz = given[n].shape[0]
        for pre, arr in zip(["grad_", "delta_", "new_m_", "new_v_"], pk):
            out[pre + n] = arr.reshape(-1)[off:off + sz]
        off += sz
    loss = pk[0].reshape(-1)[off]

    return (loss, dx.reshape(x.shape), *[out["grad_" + n] for n in WEIGHTS], *[out["delta_" + n] for n in WEIGHTS],
            *[out["new_m_" + n] for n in WEIGHTS], *[out["new_v_" + n] for n in WEIGHTS])
```

```python
import functools
import math

import numpy as np
import jax
import jax.numpy as jnp
from jax import lax
from jax.experimental import pallas as pl
from jax.experimental.pallas import tpu as pltpu

F32 = jnp.float32
BF16 = jnp.bfloat16
MESH = pl.DeviceIdType.MESH

D_MODEL = 1024
HEADS = 8
LANES = 128
MLA_Q_RANK, MLA_KV_RANK = 256, 128
MLA_NOPE, MLA_ROPE, MLA_V = 64, 32, 64
MLA_QK = MLA_NOPE + MLA_ROPE
LN2 = math.log(2.0)
MLA_Q_SCALE = MLA_QK ** -0.5 / LN2
RET_QK, RET_V, RET_CHUNK = 64, 128, 128
RET_QK_DTYPE = BF16
RET_CHUNKS_PER_STEP = 2
FFN_HIDDEN = 2816
ROPE_THETA = 10000.0
EPS = 1e-6
IN_SPLITS = [256, 128, 32, 512, 512, 1024, 1024, 2048]
IN_OFFS = [0] + list(np.cumsum(IN_SPLITS))
ADAM_LR, ADAM_B1, ADAM_B2, ADAM_EPS, ADAM_WD, ADAM_STEP = 0.001, 0.9, 0.999, 1e-08, 0.01, 10

VMEM_LIMIT = 56 * 1024 * 1024
ROW_TILE = 256
HEAD_ROW_TILE = 2048
MM_TM, MM_TN, MM_TK, MM_KFULL = 1408, 2048, 2048, 2816
ATT_TQ = 256
ATT_BQ, ATT_BK = 1024, 1024
ATT_HEADS_PER_STEP = 8
ATT_BWD_HEADS_PER_STEP = 4

SHARDED = ["w_in", "w_q_b", "w_kv_b", "w_mla_out", "w_ret_out", "w_out", "w_gate_up", "w_down"]
COL_SHARDED = {"w_in", "w_q_b", "w_kv_b", "w_mla_out", "w_gate_up"}
FIRST = ["w_in", "w_q_b", "w_kv_b"]
LATE = ["w_mla_out", "w_ret_out", "w_out", "w_gate_up", "w_down"]
SMALL = ["g_mix", "g_q_a", "g_kv_a", "g_qn", "g_kn", "ret_decay_fwd", "ret_decay_bwd", "g_ffn"]
WEIGHTS = ["g_mix", "w_in", "g_q_a", "w_q_b", "g_kv_a", "w_kv_b", "g_qn", "g_kn", "w_mla_out",
           "ret_decay_fwd", "ret_decay_bwd", "w_ret_out", "w_out", "g_ffn", "w_gate_up", "w_down"]
SMALL_ROWS = 24


def _params(**kw):
    return pltpu.CompilerParams(vmem_limit_bytes=VMEM_LIMIT, **kw)


def _pick(dim, target, unit=128):
    if dim <= target:
        return dim
    best = None
    for d in range(unit, target + 1, unit):
        if dim % d == 0:
            best = d
    assert best is not None, (dim, target)
    return best


_DOT = {"nn": (((1,), (0,)), ((), ())), "nt": (((1,), (1,)), ((), ())), "tn": (((0,), (0,)), ((), ()))}


def _dot(a, b, mode="nn"):
    return lax.dot_general(a, b, _DOT[mode], preferred_element_type=F32)


def _rms_rows(x, g):
    x = x.astype(F32)
    return x * lax.rsqrt(jnp.mean(x * x, axis=-1, keepdims=True) + EPS) * g


def _epi_loss(acc, extras, params):
    e = acc + extras[0] - extras[1]
    dy = e * (1.0 / D_MODEL)
    loss = 0.5 * jnp.sum(jnp.mean(e * e, axis=-1, keepdims=True), axis=0, keepdims=True)
    return [dy, dy], [jnp.broadcast_to(loss, (1, LANES))]


def _epi_rms_bwd(n_out):
    def fn(acc, extras, params):
        _, vjp = jax.vjp(_rms_rows, extras[0], params[0])
        dx, dg = vjp(acc)
        return [dx + extras[1]] * n_out, [dg]
    return fn


def _mm(name, a, b, mode, out_dtype, res=None, a_gain=None, epilogue=None, shard_out=False, scatter=None):
    if mode == "nn":
        (M, K), (K2, N) = a.shape, b.shape
    elif mode == "nt":
        (M, K), (N, K2) = a.shape, b.shape
    else:
        (K, M), (K2, N) = a.shape, b.shape
    assert K == K2, (name, a.shape, b.shape)
    tm, tn = _pick(M, MM_TM), _pick(N, MM_TN)
    tk = K if K <= MM_KFULL else _pick(K, MM_TK)
    if shard_out:
        tm, tn = M // 2, N // 4
    if epilogue is not None:
        tm = _pick(M, MM_TM // 2)
    nk = K // tk
    cache_a = a_gain is not None
    if a_gain is not None:
        assert mode == "nn" and tk == K and epilogue is None and not shard_out, name
    n_in = 2 + (res is not None) + (a_gain is not None)
    extras, eparams, e_outs, e_sums = ([], [], [], [])
    if epilogue is not None:
        assert tn == N and res is None and not shard_out, name
        epi_fn, extras, eparams, e_outs, e_sums = epilogue
    n_out = len(e_outs) + len(e_sums) if epilogue is not None else 1 + cache_a
    scatter = list(scatter or [])
    n_sc = len(scatter)
    assert not n_sc or epilogue is not None, name
    ni, nj = M // tm, N // tn

    def body(*refs):
        a_ref, b_ref = refs[0], refs[1]
        base = n_in + len(extras) + len(eparams)
        ex_refs = refs[n_in:n_in + len(extras)]
        ep_refs = refs[n_in + len(extras):base]
        sc_in, out_refs = refs[base:base + n_sc], refs[base + n_sc:base + n_sc + n_out]
        sc_out = refs[base + n_sc + n_out:base + 2 * n_sc + n_out]
        scratch = refs[base + 2 * n_sc + n_out:]
        acc = scratch[0]
        i, j, k = pl.program_id(0), pl.program_id(1), pl.program_id(2)

        if n_sc:
            @pl.when(jnp.logical_and(i == 0, jnp.logical_and(j == 0, k == 0)))
            def _():
                _scatter_start(_scatter_copies(sc_in, sc_out, scratch[-2], scratch[-1]))

        @pl.when(k == 0)
        def _():
            acc[...] = jnp.zeros_like(acc)

        if cache_a:
            @pl.when(j == 0)
            def _():
                out_refs[1][...] = _rms_rows(a_ref[...], refs[n_in - 1][...]).astype(BF16)
            av = out_refs[1][...]
        else:
            av = a_ref[...].astype(BF16)
        acc[...] += _dot(av, b_ref[...].astype(BF16), mode)

        @pl.when(k == nk - 1)
        def _():
            if epilogue is None:
                r = acc[...]
                if res is not None:
                    r = r + refs[2][...].astype(F32)
                out_refs[0][...] = r.astype(out_refs[0].dtype).reshape(out_refs[0].shape)
            else:
                vals, sums = epi_fn(acc[...], [r[...] for r in ex_refs], [p[...] for p in ep_refs])
                for o_ref, v in zip(out_refs, vals):
                    o_ref[...] = v.astype(o_ref.dtype)
                for s_ref, v in zip(out_refs[len(vals):], sums):
                    @pl.when(i == 0)
                    def _(s_ref=s_ref):
                        s_ref[...] = jnp.zeros_like(s_ref)
                    s_ref[...] += v

        if n_sc:
            @pl.when(jnp.logical_and(i == ni - 1, jnp.logical_and(j == nj - 1, k == nk - 1)))
            def _():
                _scatter_wait(_scatter_copies(sc_in, sc_out, scratch[-2], scratch[-1]))

    a_spec = pl.BlockSpec((tk, tm), lambda i, j, k: (k, i)) if mode == "tn" else pl.BlockSpec((tm, tk), lambda i, j, k: (i, k))
    b_spec = pl.BlockSpec((tn, tk), lambda i, j, k: (j, k)) if mode == "nt" else pl.BlockSpec((tk, tn), lambda i, j, k: (k, j))
    o_spec = pl.BlockSpec((tm, tn), lambda i, j, k: (i, j))
    const = lambda p: pl.BlockSpec(p.shape, lambda i, j, k: (0,) * p.ndim)
    ins, specs = [a, b], [a_spec, b_spec]
    if res is not None:
        ins.append(res)
        specs.append(o_spec)
    if a_gain is not None:
        ins.append(a_gain)
        specs.append(const(a_gain))
    ins += list(extras) + list(eparams)
    specs += [o_spec] * len(extras) + [const(p) for p in eparams]
    if epilogue is not None:
        out_specs = [o_spec] * len(e_outs) + [pl.BlockSpec(s, lambda i, j, k: (0, 0)) for s in e_sums]
        out_shape = [jax.ShapeDtypeStruct((M, N), dt) for dt in e_outs] + [jax.ShapeDtypeStruct(s, F32) for s in e_sums]
    elif shard_out:
        out_specs = pl.BlockSpec((1, 1, tm, tn), lambda i, j, k: (j, i, 0, 0))
        out_shape = jax.ShapeDtypeStruct((4, 2, tm, tn), out_dtype)
    elif cache_a:
        out_specs = [o_spec, pl.BlockSpec((tm, K), lambda i, j, k: (i, 0))]
        out_shape = [jax.ShapeDtypeStruct((M, N), out_dtype), jax.ShapeDtypeStruct((M, K), BF16)]
    else:
        out_specs, out_shape = o_spec, jax.ShapeDtypeStruct((M, N), out_dtype)
    scratch_shapes = [pltpu.VMEM((tm, tn), F32)]
    if n_sc:
        ins += scatter
        specs += [ANY] * n_sc
        out_specs = list(out_specs) + [ANY] * n_sc
        out_shape = list(out_shape) + [jax.ShapeDtypeStruct((8,) + g.shape[2:], g.dtype) for g in scatter]
        scratch_shapes += [pltpu.SemaphoreType.DMA((8 * n_sc,)), pltpu.SemaphoreType.DMA((7 * n_sc,))]
    res_ = pl.pallas_call(
        body, name=name, grid=(ni, nj, nk), in_specs=specs, out_specs=out_specs, out_shape=out_shape,
        scratch_shapes=scratch_shapes, compiler_params=_params(),
    )(*ins)
    if n_sc:
        return list(res_[:n_out]) + [list(res_[n_out:])]
    return res_


def _piece_spec(tm, piece):
    _, w, c0, per_group = piece
    if per_group:
        return pl.BlockSpec((tm, w), lambda i, g: (i, c0 + g))
    return pl.BlockSpec((tm, w), lambda i, g: (i, c0))


def _const_spec(p):
    return pl.BlockSpec(p.shape, lambda i, g: (0, 0))


def _rowwise(name, fn, params, rows, auxs, outs, tm, groups=1):
    S = rows[0][0].shape[0]
    tm = min(tm, S)
    n_p, n_r, n_a = len(params), len(rows), len(auxs)

    def body(*refs):
        p = [r[...] for r in refs[:n_p]]
        r_ = [r[...] for r in refs[n_p:n_p + n_r]]
        a_ = [r[...] for r in refs[n_p + n_r:n_p + n_r + n_a]]
        for o_ref, o in zip(refs[n_p + n_r + n_a:], fn(p, r_, a_)):
            o_ref[...] = o.astype(o_ref.dtype)

    out_specs, out_shape = [], []
    for w, dt, per_group in outs:
        out_specs.append(_piece_spec(tm, (None, w, 0, per_group)))
        out_shape.append(jax.ShapeDtypeStruct((S, w * (groups if per_group else 1)), dt))
    return pl.pallas_call(
        body, name=name, grid=(S // tm, groups),
        in_specs=[_const_spec(p) for p in params] + [_piece_spec(tm, q) for q in list(rows) + list(auxs)],
        out_specs=out_specs, out_shape=out_shape, compiler_params=_params(),
    )(*params, *[q[0] for q in list(rows) + list(auxs)])


def _rowwise_vjp(name, fn, params, rows, auxs, cots, d_outs, tm, groups=1, adds=None):
    S = rows[0][0].shape[0]
    tm = min(tm, S)
    n_p, n_r, n_a = len(params), len(rows), len(auxs)
    cot_flat = [q for c in cots for q in c]
    adds = adds or [None] * len(d_outs)
    add_flat = [q for q in adds if q is not None]
    n_c, n_add = len(cot_flat), len(add_flat)
    shared = [not all(rows[k][3] for k in idx) and groups > 1 for idx, _ in d_outs]

    def body(*refs):
        pos = 0
        p = [r[...] for r in refs[pos:pos + n_p]]; pos += n_p
        r_ = [r[...] for r in refs[pos:pos + n_r]]; pos += n_r
        a_ = [r[...] for r in refs[pos:pos + n_a]]; pos += n_a
        c_refs = refs[pos:pos + n_c]; pos += n_c
        add_refs = list(refs[pos:pos + n_add]); pos += n_add
        d_refs = refs[pos:pos + len(d_outs)]; pos += len(d_outs)
        dp_refs = refs[pos:]
        i, g = pl.program_id(0), pl.program_id(1)
        outs, vjp_fn = jax.vjp(lambda pp, rr: fn(pp, rr, a_), p, r_)
        cts, ci = [], 0
        for c, o in zip(cots, outs):
            t = c_refs[ci][...].astype(F32)
            for extra in c_refs[ci + 1:ci + len(c)]:
                t = t + extra[...].astype(F32)
            ci += len(c)
            cts.append(t.astype(o.dtype))
        dp, dr = vjp_fn(cts)
        for (idx, _), d_ref, add, sh in zip(d_outs, d_refs, adds, shared):
            val = dr[idx[0]].astype(F32) if len(idx) == 1 else jnp.concatenate([dr[k].astype(F32) for k in idx], axis=1)
            if add is not None:
                val = val + add_refs.pop(0)[...].astype(F32)
            if sh:
                @pl.when(g == 0)
                def _(d_ref=d_ref):
                    d_ref[...] = jnp.zeros_like(d_ref)
                d_ref[...] += val.astype(d_ref.dtype)
            else:
                d_ref[...] = val.astype(d_ref.dtype)
        first = jnp.logical_and(i == 0, g == 0)
        for dp_ref, d in zip(dp_refs, dp):
            @pl.when(first)
            def _(dp_ref=dp_ref):
                dp_ref[...] = jnp.zeros_like(dp_ref)
            dp_ref[...] += d.astype(F32)

    out_specs, out_shape = [], []
    for (idx, dt), sh in zip(d_outs, shared):
        w = sum(rows[k][1] for k in idx)
        per_group = (not sh) and groups > 1
        out_specs.append(_piece_spec(tm, (None, w, 0, per_group)))
        out_shape.append(jax.ShapeDtypeStruct((S, w * (groups if per_group else 1)), dt))
    for p in params:
        out_specs.append(_const_spec(p))
        out_shape.append(jax.ShapeDtypeStruct(p.shape, F32))
    pieces = list(rows) + list(auxs) + cot_flat + add_flat
    res = pl.pallas_call(
        body, name=name, grid=(S // tm, groups),
        in_specs=[_const_spec(p) for p in params] + [_piece_spec(tm, q) for q in pieces],
        out_specs=out_specs, out_shape=out_shape, compiler_params=_params(),
    )(*params, *[q[0] for q in pieces])
    return list(res[:len(d_outs)]), list(res[len(d_outs):])


def _lane_roll(x, shift):
    @jax.custom_vjp
    def roll(v):
        return pltpu.roll(v, shift, 1)

    roll.defvjp(lambda v: (roll(v), None), lambda _, ct: (pltpu.roll(ct, LANES - shift, 1),))
    return roll(x)


@jax.custom_vjp
def _sigmoid(x):
    return 1.0 / (1.0 + jnp.exp(-x))


def _sigmoid_fwd(x):
    s = _sigmoid(x)
    return s, s


_sigmoid.defvjp(_sigmoid_fwd, lambda s, ct: (ct * s * (1.0 - s),))


def _rope(x, cos, sin_lo, sin_hi, half):
    return x * cos + _lane_roll(x, LANES - half) * sin_lo + _lane_roll(x, half) * sin_hi


def _f_rope_table(p, r, a):
    inv, first, second, fixed = p
    ang = a[0] * inv
    cs, sn = jnp.cos(ang), jnp.sin(ang)
    return [cs * (first + second) + fixed, -sn * first, sn * second]


def _f_rms(p, r, a):
    x = r[0].astype(F32)
    return [x * lax.rsqrt(jnp.mean(x * x, axis=-1, keepdims=True) + EPS) * p[0]]


def _f_mla_a(p, r, a):
    return _f_rms([p[0]], [r[0]], a) + _f_rms([p[1]], [r[1]], a)


def _f_mla_b(p, r, a):
    def norm_rope(v, g):
        ms = jnp.sum(v * v, axis=-1, keepdims=True) * (1.0 / MLA_QK)
        return _rope(v * lax.rsqrt(ms + EPS) * g, a[0], a[1], a[2], MLA_ROPE // 2)

    return [norm_rope(r[0].astype(F32), p[0]) * MLA_Q_SCALE, norm_rope(r[1].astype(F32) + r[2].astype(F32), p[1])]


def _f_ret_rope(p, r, a):
    q = _rope(r[0].astype(F32), a[0], a[1], a[2], RET_QK // 2)
    k = _rope(r[1].astype(F32), a[0], a[1], a[2], RET_QK // 2)
    return [q, k * (RET_QK ** -0.5)]


def _f_ret_post(p, r, a):
    ret = r[0].astype(F32) + r[1].astype(F32)
    g = r[2].astype(F32)
    normed = ret * lax.rsqrt(jnp.mean(ret * ret, axis=-1, keepdims=True) + EPS)
    return [g * _sigmoid(g) * normed]


def _f_merge(p, r, a):
    return [_sigmoid(r[0].astype(F32)) * r[2].astype(F32) + _sigmoid(r[1].astype(F32)) * r[3].astype(F32)]


def _f_swiglu(p, r, a):
    g = r[0].astype(F32)
    return [g * _sigmoid(g) * r[1].astype(F32)]


def _f_add(p, r, a):
    return [r[0].astype(F32) + r[1].astype(F32)]


def _flash_fwd(q, k, kv, shards):
    S = q.shape[0]
    tq = min(ATT_TQ, S)
    nq = S // tq
    n = len(shards)

    def body(q_ref, k_ref, v_ref, *rest):
        shard_refs, (o_ref, lse_ref), gathered = rest[:n], rest[n:n + 2], rest[n + 2:2 * n + 2]
        send_sems, recv_sems = rest[2 * n + 2:]
        h, qi = pl.program_id(0), pl.program_id(1)

        @pl.when(jnp.logical_and(h == 0, qi == 0))
        def _():
            _gather_start(_gather_copies(shard_refs, gathered, send_sems, recv_sems))

        for hh in range(hps):
            lanes = slice(hh * LANES, (hh + 1) * LANES)
            s = _dot(q_ref[:, lanes], k_ref[:, lanes], "nt")
            m = jnp.max(s, axis=-1, keepdims=True)
            p = jnp.exp2(s - m)
            l = jnp.sum(p, axis=-1, keepdims=True)
            o_ref[:, lanes] = (_dot(p.astype(BF16), v_ref[:, lanes]) / l).astype(o_ref.dtype)
            lse_ref[:, lanes] = jnp.broadcast_to(m + jnp.log2(l), (tq, LANES))

        @pl.when(jnp.logical_and(h == HEADS // hps - 1, qi == nq - 1))
        def _():
            _gather_wait(_gather_copies(shard_refs, gathered, send_sems, recv_sems))

    hps = ATT_HEADS_PER_STEP
    qs = pl.BlockSpec((tq, hps * LANES), lambda h, i: (i, h))
    res = pl.pallas_call(
        body, name="mla_fwd", grid=(HEADS // hps, nq),
        in_specs=[qs, pl.BlockSpec((S, hps * LANES), lambda h, i: (0, h), pipeline_mode=pl.Buffered(1)),
                  pl.BlockSpec((S, hps * LANES), lambda h, i: (0, HEADS // hps + h), pipeline_mode=pl.Buffered(1))]
        + [ANY] * n,
        out_specs=[qs, qs] + [ANY] * n,
        out_shape=[jax.ShapeDtypeStruct((S, HEADS * LANES), BF16), jax.ShapeDtypeStruct((S, HEADS * LANES), F32)]
        + [jax.ShapeDtypeStruct((4,) + s.shape, s.dtype) for s in shards],
        scratch_shapes=[pltpu.SemaphoreType.DMA((3 * n,)), pltpu.SemaphoreType.DMA((3 * n,))],
        compiler_params=_params(),
    )(q, k, kv, *shards)
    mine = 2 * lax.axis_index("x") + lax.axis_index("y")
    return res[0], res[1], [_fill_slot(g, s, mine) for g, s in zip(res[2:], shards)]


def _flash_bwd(q, k, kv, do, lse, o, gs):
    S = q.shape[0]
    tq, tk = min(ATT_BQ, S), min(ATT_BK, S)
    nq, nkt = S // tq, S // tk
    n = len(gs)

    def body(q_ref, k_ref, v_ref, do_ref, lse_ref, o_ref, *rest):
        g_refs, (dq_ref, dk_ref, dv_ref), got_refs = rest[:n], rest[n:n + 3], rest[n + 3:2 * n + 3]
        dk_sc, dv_sc, send_sems, recv_sems = rest[2 * n + 3:]
        h, ki, qi = pl.program_id(0), pl.program_id(1), pl.program_id(2)

        @pl.when(jnp.logical_and(h == 0, jnp.logical_and(ki == 0, qi == 0)))
        def _():
            _scatter_start(_scatter_copies(g_refs, got_refs, send_sems, recv_sems))

        @pl.when(jnp.logical_and(ki == 0, qi == 0))
        def _():
            dq_ref[...] = jnp.zeros_like(dq_ref)

        @pl.when(qi == 0)
        def _():
            dk_sc[...] = jnp.zeros_like(dk_sc)
            dv_sc[...] = jnp.zeros_like(dv_sc)

        rows = pl.ds(pl.multiple_of(qi * tq, tq), tq)
        for hh in range(hps):
            lanes = slice(hh * LANES, (hh + 1) * LANES)
            qv, kv_, dov = q_ref[:, lanes], k_ref[:, lanes], do_ref[:, lanes]
            p = jnp.exp2(_dot(qv, kv_, "nt") - lse_ref[:, lanes][:, :1])
            dp = _dot(dov, v_ref[:, lanes], "nt")
            delta = jnp.sum(dov.astype(F32) * o_ref[:, lanes].astype(F32), axis=-1, keepdims=True)
            ds = (p * (dp - delta) * LN2).astype(BF16)
            dv_sc[:, lanes] += _dot(p.astype(BF16), dov, "tn")
            dk_sc[:, lanes] += _dot(ds, qv, "tn")
            dq_ref[rows, lanes] += _dot(ds, kv_)

        @pl.when(qi == nq - 1)
        def _():
            dk_ref[...] = dk_sc[...].astype(dk_ref.dtype)
            dv_ref[...] = dv_sc[...].astype(dv_ref.dtype)

        @pl.when(jnp.logical_and(h == HEADS // hps - 1, jnp.logical_and(ki == nkt - 1, qi == nq - 1)))
        def _():
            _scatter_wait(_scatter_copies(g_refs, got_refs, send_sems, recv_sems))

    hps = ATT_BWD_HEADS_PER_STEP
    qs = pl.BlockSpec((tq, hps * LANES), lambda h, j, i: (i, h))
    ks = pl.BlockSpec((tk, hps * LANES), lambda h, j, i: (j, h))
    res = pl.pallas_call(
        body, name="mla_bwd", grid=(HEADS // hps, nkt, nq),
        in_specs=[qs, ks, pl.BlockSpec((tk, hps * LANES), lambda h, j, i: (j, HEADS // hps + h)), qs, qs, qs] + [ANY] * n,
        out_specs=[pl.BlockSpec((S, hps * LANES), lambda h, j, i: (0, h), pipeline_mode=pl.Buffered(1)), ks, ks] + [ANY] * n,
        out_shape=[jax.ShapeDtypeStruct((S, HEADS * LANES), F32), jax.ShapeDtypeStruct((S, HEADS * LANES), BF16),
                   jax.ShapeDtypeStruct((S, HEADS * LANES), BF16)]
        + [jax.ShapeDtypeStruct((8,) + g.shape[2:], g.dtype) for g in gs],
        scratch_shapes=[pltpu.VMEM((tk, hps * LANES), F32)] * 2
        + [pltpu.SemaphoreType.DMA((8 * n,)), pltpu.SemaphoreType.DMA((7 * n,))],
        compiler_params=_params(),
    )(q, k, kv, do, lse, o, *gs)
    return res[0], res[1], res[2], list(res[3:])


def _ret_tables(decay_row, backward):
    C = RET_CHUNK
    lg = -jnp.exp(decay_row)
    t = lax.broadcasted_iota(jnp.int32, (C, C), 0).astype(F32)
    s = lax.broadcasted_iota(jnp.int32, (C, C), 1).astype(F32)
    ridx = lax.broadcasted_iota(jnp.int32, (C, LANES), 0).astype(F32)
    if backward:
        dist, mask, aw, bw = s - t, s > t, C - ridx, ridx
    else:
        dist, mask, aw, bw = t - s, t >= s, ridx + 1.0, C - 1.0 - ridx
    dist = jnp.maximum(dist, 0.0)
    din = jnp.where(mask, jnp.exp(lg[:, :1] * dist), 0.0)
    return dict(din=din, dist=dist, a=jnp.exp(lg * aw), b=jnp.exp(lg * bw), c=jnp.exp(lg * C), aw=aw, bw=bw)


def _ret_fill_tables(decs, din_sc, a_sc, b_sc):
    for d, dec in enumerate(decs):
        for h in range(HEADS):
            tb = _ret_tables(dec[h:h + 1, :], d == 1)
            din_sc[d, h], a_sc[d, h], b_sc[d, h] = tb["din"], tb["a"], tb["b"]


def _ret_head_mask(h):
    lane = lax.broadcasted_iota(jnp.int32, (1, LANES), 1)
    return jnp.where((lane >= RET_QK) == bool(h % 2), 1.0, 0.0).astype(F32)


def _ret_fwd(qr, kr, proj, v_block, dec_f, dec_b):
    S = qr.shape[0]
    C = RET_CHUNK
    n = S // C
    nc = min(RET_CHUNKS_PER_STEP, n)
    nb = n // nc
    W = HEADS * LANES

    def body(qf, kf, vf, qb, kb, vb, df, db, of, ob, sf_out, sb_out, st, din_sc, a_sc, b_sc):
        @pl.when(pl.program_id(0) == 0)
        def _():
            st[...] = jnp.zeros_like(st)
            _ret_fill_tables((df, db), din_sc, a_sc, b_sc)

        for d, (q_ref, k_ref, v_ref, dec, o_ref, s_out) in enumerate(
                [(qf, kf, vf, df, of, sf_out), (qb, kb, vb, db, ob, sb_out)]):
            for h in range(HEADS):
                lanes, pair = slice(h * LANES, (h + 1) * LANES), slice(h // 2 * LANES, (h // 2 + 1) * LANES)
                mine = _ret_head_mask(h)
                din, a, b = din_sc[d, h], a_sc[d, h], b_sc[d, h]
                c = jnp.exp(-jnp.exp(dec[h:h + 1, :]) * C)
                for ci in (range(nc) if d == 0 else reversed(range(nc))):
                    rows = slice(ci * C, (ci + 1) * C)
                    qf32, kf32 = q_ref[rows, pair].astype(F32) * mine, k_ref[rows, pair].astype(F32) * mine
                    v = v_ref[rows, lanes]
                    state = st[d, h]
                    s_out[ci, h] = state
                    inner = _dot((_dot(qf32.astype(BF16), kf32.astype(BF16), "nt") * din).astype(BF16), v)
                    cross = _dot((qf32 * a).astype(BF16), state.astype(BF16))
                    o_ref[rows, lanes] = inner + cross
                    st[d, h] = state * c + _dot((kf32 * b).astype(BF16), v, "tn")

    fw = lambda c0, w=W: pl.BlockSpec((nc * C, w), lambda j: (j, c0))
    bw = lambda c0, w=W: pl.BlockSpec((nc * C, w), lambda j: (nb - 1 - j, c0))
    dec_spec = pl.BlockSpec((HEADS, LANES), lambda j: (0, 0))
    st_shape = jax.ShapeDtypeStruct((n, HEADS, LANES, LANES), F32)
    QW = W // 2
    return pl.pallas_call(
        body, name="ret_fwd", grid=(nb,),
        in_specs=[fw(0, QW), fw(0, QW), fw(v_block), bw(0, QW), bw(0, QW), bw(v_block), dec_spec, dec_spec],
        out_specs=[fw(0), bw(0), pl.BlockSpec((nc, HEADS, LANES, LANES), lambda j: (j, 0, 0, 0)),
                   pl.BlockSpec((nc, HEADS, LANES, LANES), lambda j: (nb - 1 - j, 0, 0, 0))],
        out_shape=[jax.ShapeDtypeStruct((S, W), F32)] * 2 + [st_shape] * 2,
        scratch_shapes=[pltpu.VMEM((2, HEADS, LANES, LANES), F32), pltpu.VMEM((2, HEADS, C, C), F32),
                        pltpu.VMEM((2, HEADS, C, LANES), F32), pltpu.VMEM((2, HEADS, C, LANES), F32)],
        compiler_params=_params(),
    )(qr, kr, proj, qr, kr, proj, dec_f, dec_b)


def _ret_bwd(qr, kr, proj, v_block, dret, sf, sb, dec_f, dec_b):
    S = qr.shape[0]
    C = RET_CHUNK
    n = S // C
    nc = min(RET_CHUNKS_PER_STEP, n)
    nb = n // nc
    W = HEADS * LANES

    def body(qf, kf, vf, gf, sf_ref, qb, kb, vb, gb, sb_ref, df, db,
             dqf, dkf, dvf, dqb, dkb, dvb, ddf, ddb, ds_sc, din_sc, a_sc, b_sc):
        j = pl.program_id(0)

        @pl.when(j == 0)
        def _():
            ds_sc[...] = jnp.zeros_like(ds_sc)
            ddf[...] = jnp.zeros_like(ddf)
            ddb[...] = jnp.zeros_like(ddb)
            _ret_fill_tables((df, db), din_sc, a_sc, b_sc)

        for d, (q_ref, k_ref, v_ref, g_ref, s_ref, dec, dq_ref, dk_ref, dv_ref, dd_ref) in enumerate(
                [(qf, kf, vf, gf, sf_ref, df, dqf, dkf, dvf, ddf), (qb, kb, vb, gb, sb_ref, db, dqb, dkb, dvb, ddb)]):
            static = _ret_tables(dec[0:1, :], d == 1)
            dist, aw, bw_ = static["dist"], static["aw"], static["bw"]
            for h in range(HEADS):
                lanes, pair = slice(h * LANES, (h + 1) * LANES), slice(h // 2 * LANES, (h // 2 + 1) * LANES)
                mine = _ret_head_mask(h)
                din, a, b = din_sc[d, h], a_sc[d, h], b_sc[d, h]
                c = jnp.exp(-jnp.exp(dec[h:h + 1, :]) * C)
                dlg = jnp.zeros((1, 1), F32)
                for ci in (reversed(range(nc)) if d == 0 else range(nc)):
                    rows = slice(ci * C, (ci + 1) * C)
                    v, g = v_ref[rows, lanes], g_ref[rows, lanes]
                    qf32, kf32 = q_ref[rows, pair].astype(F32) * mine, k_ref[rows, pair].astype(F32) * mine
                    q, k = qf32.astype(BF16), kf32.astype(BF16)
                    state, dstate = s_ref[ci, h], ds_sc[d, h]
                    dstate_b = dstate.astype(BF16)
                    dp = _dot(g, v, "nt")
                    a_ = _dot(q, k, "nt")
                    da = (dp * din).astype(BF16)
                    g1 = _dot(g, state.astype(BF16), "nt")
                    g2 = _dot(v, dstate_b, "nt")
                    dq_h = (_dot(da, k) + g1 * a).astype(dq_ref.dtype)
                    dk_h = (_dot(da, q, "tn") + g2 * b).astype(dk_ref.dtype)
                    if h % 2 == 0:
                        dq_ref[rows, pair], dk_ref[rows, pair] = dq_h, dk_h
                    else:
                        dq_ref[rows, pair] += dq_h
                        dk_ref[rows, pair] += dk_h
                    dv_ref[rows, lanes] = (_dot((a_ * din).astype(BF16), g, "tn")
                                           + _dot((kf32 * b).astype(BF16), dstate_b)).astype(dv_ref.dtype)
                    dlg = dlg + (jnp.sum(dp * a_ * din * dist, keepdims=True)
                                 + jnp.sum(g1 * qf32 * a * aw, keepdims=True)
                                 + jnp.sum(g2 * kf32 * b * bw_, keepdims=True)
                                 + C * jnp.sum(c * dstate * state, keepdims=True))
                    ds_sc[d, h] = dstate * c + _dot((qf32 * a).astype(BF16), g, "tn")
                dd_ref[h:h + 1, :] += jnp.broadcast_to(dlg, (1, LANES))

        @pl.when(j == nb - 1)
        def _():
            ddf[...] = ddf[...] * -jnp.exp(df[...])
            ddb[...] = ddb[...] * -jnp.exp(db[...])

    fw = lambda c0, w=W: pl.BlockSpec((nc * C, w), lambda j: (nb - 1 - j, c0))
    bw = lambda c0, w=W: pl.BlockSpec((nc * C, w), lambda j: (j, c0))
    dec_spec = pl.BlockSpec((HEADS, LANES), lambda j: (0, 0))
    QW = W // 2
    act, act_qk = jax.ShapeDtypeStruct((S, W), BF16), jax.ShapeDtypeStruct((S, QW), BF16)
    return pl.pallas_call(
        body, name="ret_bwd", grid=(nb,),
        in_specs=[fw(0, QW), fw(0, QW), fw(v_block), fw(0),
                  pl.BlockSpec((nc, HEADS, LANES, LANES), lambda j: (nb - 1 - j, 0, 0, 0)),
                  bw(0, QW), bw(0, QW), bw(v_block), bw(0), pl.BlockSpec((nc, HEADS, LANES, LANES), lambda j: (j, 0, 0, 0)),
                  dec_spec, dec_spec],
        out_specs=[fw(0, QW), fw(0, QW), fw(0), bw(0, QW), bw(0, QW), bw(0)] + [dec_spec] * 2,
        out_shape=[act_qk, act_qk, act, act_qk, act_qk, act] + [jax.ShapeDtypeStruct((HEADS, LANES), F32)] * 2,
        scratch_shapes=[pltpu.VMEM((2, HEADS, LANES, LANES), F32), pltpu.VMEM((2, HEADS, C, C), F32),
                        pltpu.VMEM((2, HEADS, C, LANES), F32), pltpu.VMEM((2, HEADS, C, LANES), F32)],
        compiler_params=_params(),
    )(qr, kr, proj, dret, sf, qr, kr, proj, dret, sb, dec_f, dec_b)


def _pad_heads(w, hd):
    K = w.shape[0]
    return jnp.pad(w.reshape(K, HEADS, hd), ((0, 0), (0, 0), (0, LANES - hd))).reshape(K, HEADS * LANES)


def _unpad_heads(w, hd):
    K = w.shape[0]
    return w.reshape(K, HEADS, LANES)[:, :, :hd].reshape(K, HEADS * hd)


def _rope_consts(first_lane, half, period=LANES):
    lane = np.arange(LANES) % period
    first = ((lane >= first_lane) & (lane < first_lane + half)).astype(np.float32)
    second = ((lane >= first_lane + half) & (lane < first_lane + 2 * half)).astype(np.float32)
    fixed = (lane < first_lane).astype(np.float32)
    j = np.where(first > 0, lane - first_lane, lane - first_lane - half) * (first + second)
    inv = (ROPE_THETA ** (-j.astype(np.float64) / half)).astype(np.float32)
    return [jnp.asarray(v.reshape(1, LANES), F32) for v in (inv, first, second, fixed)]


def _assemble(name, gathered):
    if name in COL_SHARDED:
        return jnp.transpose(gathered, (1, 0, 2)).reshape(gathered.shape[1], 4 * gathered.shape[2])
    return gathered.reshape(4 * gathered.shape[1], gathered.shape[2])


def _split_for_reducers(name, g, dtype):
    if name in COL_SHARDED:
        K, N4 = g.shape
        return jnp.transpose(g.reshape(2, K // 2, 4, N4 // 4), (2, 0, 1, 3)).astype(dtype)
    return g.reshape(4, 2, g.shape[0] // 8, g.shape[1]).astype(dtype)


def _local_step(x, tab_m, tab_r, tgt, wts, late_shards, small):
    w_in = wts["w_in"]
    seg = [w_in[:, IN_OFFS[i]:IN_OFFS[i + 1]] for i in range(8)]
    kr_w = jnp.pad(seg[2], ((0, 0), (MLA_NOPE, LANES - MLA_QK)))
    w_in_p = jnp.concatenate([seg[7], seg[5], seg[6], seg[3], seg[4], seg[0], seg[1], kr_w], axis=1)
    QR0, KR0, CQ0 = 4096, 4608, 5120
    w_qb_p = _pad_heads(wts["w_q_b"], MLA_QK)
    kvw = wts["w_kv_b"].reshape(MLA_KV_RANK, HEADS, MLA_NOPE + MLA_V)
    pad_kv = lambda t: jnp.pad(t, ((0, 0), (0, 0), (0, LANES - t.shape[2]))).reshape(MLA_KV_RANK, HEADS * LANES)
    w_kn_p, w_v_p = pad_kv(kvw[:, :, :MLA_NOPE]), pad_kv(kvw[:, :, MLA_NOPE:])
    w_kv_p = jnp.concatenate([w_kn_p, w_v_p], axis=1)
    g_qn_p = jnp.pad(small["g_qn"], ((0, 0), (0, LANES - MLA_QK)))
    g_kn_p = jnp.pad(small["g_kn"], ((0, 0), (0, LANES - MLA_QK)))
    dec_f = jnp.broadcast_to(small["ret_decay_fwd"].reshape(HEADS, 1), (HEADS, LANES))
    dec_b = jnp.broadcast_to(small["ret_decay_bwd"].reshape(HEADS, 1), (HEADS, LANES))
    T, N = True, False
    RT, HT = ROW_TILE, HEAD_ROW_TILE
    RW = 2 * ROW_TILE

    aux_m = [(t, LANES, 0, N) for t in tab_m]
    aux_r = [(t, LANES, 0, N) for t in tab_r]

    proj, h = _mm("proj", x, w_in_p, "nn", BF16, a_gain=small["g_mix"])
    rows_a = [(proj, MLA_Q_RANK, CQ0 // MLA_Q_RANK, N), (proj, MLA_KV_RANK, (CQ0 + MLA_Q_RANK) // MLA_KV_RANK, N)]
    cqn, ckvn = _rowwise("mla_lat_norm", _f_mla_a, [small["g_q_a"], small["g_kv_a"]], rows_a, [],
                         [(MLA_Q_RANK, BF16, N), (MLA_KV_RANK, BF16, N)], RW)
    qraw = _mm("mla_q_up", cqn, w_qb_p, "nn", BF16)
    kv = _mm("mla_kv_up", ckvn, w_kv_p, "nn", BF16)
    rows_b = [(qraw, LANES, 0, T), (kv, LANES, 0, T), (proj, LANES, (CQ0 + MLA_Q_RANK + MLA_KV_RANK) // LANES, N)]
    q, k = _rowwise("mla_qk_norm_rope", _f_mla_b, [g_qn_p, g_kn_p], rows_b, aux_m, [(LANES, BF16, T)] * 2, HT, HEADS)
    o, lse, late = _flash_fwd(q, k, kv, [late_shards[n] for n in LATE])
    wl = {n: _assemble(n, g) for n, g in zip(LATE, late)}
    w_mla_p = jnp.pad(wl["w_mla_out"].reshape(HEADS, MLA_V, D_MODEL), ((0, 0), (0, LANES - MLA_V), (0, 0))).reshape(HEADS * LANES, D_MODEL)
    w_ret_out, w_out, w_gu, w_down = wl["w_ret_out"], wl["w_out"], wl["w_gate_up"], wl["w_down"]
    y_a = _mm("mla_out", o, w_mla_p, "nn", BF16)
    rows_rr = [(proj, LANES, QR0 // LANES, T), (proj, LANES, KR0 // LANES, T)]
    qr, kr = _rowwise("ret_rope", _f_ret_rope, [], rows_rr, aux_r, [(LANES, RET_QK_DTYPE, T)] * 2, HT, HEADS // 2)
    ret_f, ret_b, st_f, st_b = _ret_fwd(qr, kr, proj, 2, dec_f, dec_b)
    rows_rp = [(ret_f, LANES, 0, T), (ret_b, LANES, 0, T), (proj, LANES, 24, T)]
    (o_b,) = _rowwise("ret_post", _f_ret_post, [], rows_rp, [], [(LANES, BF16, T)], HT, HEADS)
    y_b = _mm("ret_out", o_b, w_ret_out, "nn", BF16)
    rows_m = [(proj, D_MODEL, 0, N), (proj, D_MODEL, 1, N), (y_a, D_MODEL, 0, N), (y_b, D_MODEL, 0, N)]
    (merged,) = _rowwise("merge", _f_merge, [], rows_m, [], [(D_MODEL, BF16, N)], RW)
    x2 = _mm("mix_out", merged, w_out, "nn", F32, res=x)
    gu, h2 = _mm("ffn_gate_up", x2, w_gu, "nn", BF16, a_gain=small["g_ffn"])
    rows_sw = [(gu, FFN_HIDDEN, 0, N), (gu, FFN_HIDDEN, 1, N)]
    (act,) = _rowwise("swiglu", _f_swiglu, [], rows_sw, [], [(FFN_HIDDEN, BF16, N)], RT)
    dy, dy_b16, loss_row = _mm("ffn_down", act, w_down, "nn", None,
                               epilogue=(_epi_loss, [x2, tgt], [], [F32, BF16], [(1, LANES)]))

    dact = _mm("d_act", dy_b16, w_down, "nt", BF16)
    dw_down = _mm("dw_down", act, dy_b16, "tn", BF16)
    (dgu,), _ = _rowwise_vjp("swiglu_bwd", _f_swiglu, [], rows_sw, [], [[(dact, FFN_HIDDEN, 0, N)]], [([0, 1], BF16)], RT)
    dx2, dx2_b16, dg_ffn = _mm("d_h2", dgu, w_gu, "nt", None,
                               epilogue=(_epi_rms_bwd(2), [x2, dy], [small["g_ffn"]], [F32, BF16], [(1, D_MODEL)]))
    dw_gu = _mm("dw_gate_up", h2, dgu, "tn", BF16, shard_out=True)
    dmerged = _mm("d_merged", dx2_b16, w_out, "nt", BF16)
    dw_out = _mm("dw_out", merged, dx2_b16, "tn", BF16)
    (dgl, dy_a, dy_b), _ = _rowwise_vjp("merge_bwd", _f_merge, [], rows_m, [], [[(dmerged, D_MODEL, 0, N)]],
                                        [([0, 1], BF16), ([2], BF16), ([3], BF16)], RW)
    do_b = _mm("d_ret_o", dy_b, w_ret_out, "nt", BF16)
    dw_ret_out = _mm("dw_ret_out", o_b, dy_b, "tn", BF16)
    (dret, dg_r), _ = _rowwise_vjp("ret_post_bwd", _f_ret_post, [], rows_rp, [], [[(do_b, LANES, 0, T)]],
                                   [([0], BF16), ([2], BF16)], HT, HEADS)
    dqf, dkf, dvf, dqb, dkb, dvb, ddec_f, ddec_b = _ret_bwd(qr, kr, proj, 2, dret, st_f, st_b, dec_f, dec_b)
    (dq_r, dk_r), _ = _rowwise_vjp("ret_rope_bwd", _f_ret_rope, [], rows_rr, aux_r,
                                   [[(dqf, LANES, 0, T), (dqb, LANES, 0, T)], [(dkf, LANES, 0, T), (dkb, LANES, 0, T)]],
                                   [([0], BF16), ([1], BF16)], HT, HEADS // 2)
    (dv_r,) =_rowwise("ret_dv_sum", _f_add, [], [(dvf, D_MODEL, 0, N), (dvb, D_MODEL, 0, N)], [], [(D_MODEL, BF16, N)], RW)
    do = _mm("d_mla_o", dy_a, w_mla_p, "nt", BF16)
    dw_mla_p = _mm("dw_mla_out", o, dy_a, "tn", BF16)
    dw_mla = dw_mla_p.reshape(HEADS, LANES, D_MODEL)[:, :MLA_V].reshape(HEADS * MLA_V, D_MODEL)
    late_grads = {"w_mla_out": dw_mla, "w_ret_out": dw_ret_out, "w_out": dw_out, "w_down": dw_down}
    late_gs = [dw_gu if n == "w_gate_up" else _split_for_reducers(n, late_grads[n], BF16) for n in LATE]
    dq, dk, dv, late_got = _flash_bwd(q, k, kv, do, lse, o, late_gs)
    (dqraw, dkn, dkr), (dg_qn_p, dg_kn_p) = _rowwise_vjp(
        "mla_qk_norm_rope_bwd", _f_mla_b, [g_qn_p, g_kn_p], rows_b, aux_m, [[(dq, LANES, 0, T)], [(dk, LANES, 0, T)]],
        [([0], BF16), ([1], BF16), ([2], F32)], HT, HEADS)
    dckvn = _mm("d_ckvn_v", dv, w_v_p, "nt", BF16, res=_mm("d_ckvn_k", dkn, w_kn_p, "nt", F32))
    dw_kn_p = _mm("dw_kv_k", ckvn, dkn, "tn", BF16)
    dw_v_p = _mm("dw_kv_v", ckvn, dv, "tn", BF16)
    dcqn = _mm("d_cqn", dqraw, w_qb_p, "nt", BF16)
    dw_qb_p = _mm("dw_q_b", cqn, dqraw, "tn", BF16)
    (dcq, dckv), (dg_q_a, dg_kv_a) = _rowwise_vjp(
        "mla_lat_norm_bwd", _f_mla_a, [small["g_q_a"], small["g_kv_a"]], rows_a, [],
        [[(dcqn, MLA_Q_RANK, 0, N)], [(dckvn, MLA_KV_RANK, 0, N)]], [([0], BF16), ([1], BF16)], RW)
    dproj = jnp.concatenate([dgl, dv_r, dg_r, dq_r, dk_r, dcq, dckv, dkr.astype(BF16)], axis=1)
    dw_in_p = _mm("dw_in", h, dproj, "tn", BF16)

    c = lambda a, b_: dw_in_p[:, a:b_]
    kr0 = CQ0 + MLA_Q_RANK + MLA_KV_RANK
    dw_in = jnp.concatenate([c(CQ0, CQ0 + MLA_Q_RANK), c(CQ0 + MLA_Q_RANK, kr0), c(kr0 + MLA_NOPE, kr0 + MLA_QK), c(QR0, KR0),
                             c(KR0, CQ0), c(2048, 3072), c(3072, 4096), c(0, 2048)], axis=1)
    un_kv = lambda t: t.reshape(MLA_KV_RANK, HEADS, LANES)[:, :, :MLA_NOPE]
    dw_kv = jnp.concatenate([un_kv(dw_kn_p), un_kv(dw_v_p)], axis=2).reshape(MLA_KV_RANK, HEADS * (MLA_NOPE + MLA_V))
    grads = {"w_in": dw_in, "w_q_b": _unpad_heads(dw_qb_p, MLA_QK), "w_kv_b": dw_kv}
    dx, dg_mix, first_got = _mm("d_h", dproj, w_in_p, "nt", None,
                                epilogue=(_epi_rms_bwd(1), [x, dx2], [small["g_mix"]], [F32], [(1, D_MODEL)]),
                                scatter=[_split_for_reducers(n, grads[n], BF16) for n in FIRST])
    sgrads = {"g_mix": dg_mix, "g_q_a": dg_q_a, "g_kv_a": dg_kv_a, "g_qn": dg_qn_p[:, :MLA_QK], "g_kn": dg_kn_p[:, :MLA_QK],
              "ret_decay_fwd": ddec_f[:, 0].reshape(1, HEADS), "ret_decay_bwd": ddec_b[:, 0].reshape(1, HEADS), "g_ffn": dg_ffn}
    return loss_row, dx, first_got + late_got, sgrads


def _coords():
    return lax.axis_index("x"), lax.axis_index("y"), lax.axis_index("c")


def _other_chips(x, y):
    return [(1 - x, y), (x, 1 - y), (1 - x, 1 - y)]


ANY = pl.BlockSpec(memory_space=pl.ANY)


def _gather_copies(ins, outs, send_sems, recv_sems):
    x, y, c = _coords()
    mine = 2 * x + y
    sends, arrivals = [], []
    for w in range(len(ins)):
        for j, (cx, cy) in enumerate(_other_chips(x, y)):
            sems = dict(send_sem=send_sems.at[3 * w + j], recv_sem=recv_sems.at[3 * w + j],
                        device_id=(cx, cy, c), device_id_type=MESH)
            sends.append(pltpu.make_async_remote_copy(src_ref=ins[w], dst_ref=outs[w].at[mine], **sems))
            arrivals.append(functools.partial(pltpu.make_async_remote_copy, src_ref=ins[w],
                                              dst_ref=outs[w].at[2 * cx + cy], **sems))
    return sends, arrivals


def _gather_start(copies):
    for cp in list(copies[0]) + list(copies[2] if len(copies) > 2 else []):
        cp.start()


def _gather_wait(copies):
    sends, arrivals = copies[:2]
    for make in arrivals:
        make().wait_recv()
    for cp in sends:
        cp.wait_send()
    for cp in (copies[2] if len(copies) > 2 else []):
        cp.wait()


def _fill_slot(buf, piece, slot):
    idx = lax.broadcasted_iota(jnp.int32, (buf.shape[0],) + (1,) * piece.ndim, 0)
    return jnp.where(idx == slot, piece[None], buf)


def _rope_tables_and_first_gather(pos, consts_mla, consts_ret, shards):
    S = pos.shape[0]
    tm = min(HEAD_ROW_TILE, S)
    nt = S // tm
    n = len(shards)

    def body(pos_ref, *refs):
        consts, ins = (refs[:4], refs[4:8]), refs[8:8 + n]
        tabs, outs = refs[8 + n:14 + n], refs[14 + n:14 + 2 * n]
        send_sems, recv_sems = refs[14 + 2 * n:]
        i = pl.program_id(0)
        x, y, c = _coords()
        chips = _other_chips(x, y)
        mine = 2 * x + y

        def half(ref, slot, core):
            rows = ref.shape[1] // 2
            return ref.at[slot, pl.ds(pl.multiple_of(core * rows, 8), rows)]

        def copy(w, k, slot, core, to, src=None):
            return pltpu.make_async_remote_copy(
                src_ref=half(outs[w], slot, core) if src is None else src, dst_ref=half(outs[w], slot, core),
                send_sem=send_sems.at[6 * w + k], recv_sem=recv_sems.at[6 * w + k], device_id=to, device_id_type=MESH)

        def first(w, j):
            rows = ins[w].shape[0] // 2
            return copy(w, j, mine, c, (*chips[j], c), src=ins[w].at[pl.ds(pl.multiple_of(c * rows, 8), rows)])

        @pl.when(i == 0)
        def _():
            for w in range(n):
                for j in range(3):
                    first(w, j).start()

        for k in range(2):
            vals = _f_rope_table([r[...] for r in consts[k]], None, [pos_ref[...]])
            for t_ref, v in zip(tabs[3 * k:3 * k + 3], vals):
                t_ref[...] = v

        @pl.when(i == nt - 1)
        def _():
            passed = []
            for w in range(n):
                for j, (cx, cy) in enumerate(chips):
                    copy(w, j, 2 * cx + cy, c, (x, y, c)).wait_recv()
                    cp = copy(w, 3 + j, 2 * cx + cy, c, (x, y, 1 - c))
                    cp.start()
                    passed.append(cp)
            for w in range(n):
                for j, (cx, cy) in enumerate(chips):
                    copy(w, 3 + j, 2 * cx + cy, 1 - c, (x, y, c)).wait_recv()
            for w in range(n):
                for j in range(3):
                    first(w, j).wait_send()
            for cp in passed:
                cp.wait_send()

    const = lambda p: pl.BlockSpec(p.shape, lambda i: (0, 0))
    tab = pl.BlockSpec((tm, LANES), lambda i: (i, 0))
    res = pl.pallas_call(
        body, name="rope_tables_first_gather", grid=(nt,),
        in_specs=[pl.BlockSpec((tm, 1), lambda i: (i, 0))] + [const(p) for p in list(consts_mla) + list(consts_ret)] + [ANY] * n,
        out_specs=[tab] * 6 + [ANY] * n,
        out_shape=[jax.ShapeDtypeStruct((S, LANES), F32)] * 6 + [jax.ShapeDtypeStruct((4,) + s.shape, s.dtype) for s in shards],
        scratch_shapes=[pltpu.SemaphoreType.DMA((6 * n,)), pltpu.SemaphoreType.DMA((6 * n,))],
        compiler_params=_params(),
    )(pos, *consts_mla, *consts_ret, *shards)
    return list(res[:3]), list(res[3:6]), list(res[6:])


def _scatter_copies(ins, outs, send_sems, recv_sems):
    x, y, c = _coords()
    me = 4 * x + 2 * y + c
    n = len(ins)
    sends, arrivals = [], []
    local = [pltpu.make_async_copy(ins[w].at[2 * x + y, c], outs[w].at[me], send_sems.at[7 * n + w]) for w in range(n)]
    for w in range(n):
        for k in range(1, 8):
            px, py, pc = x ^ (k >> 2), y ^ ((k >> 1) & 1), c ^ (k & 1)
            sems = dict(send_sem=send_sems.at[7 * w + k - 1], recv_sem=recv_sems.at[7 * w + k - 1],
                        device_id=(px, py, pc), device_id_type=MESH)
            sends.append(pltpu.make_async_remote_copy(src_ref=ins[w].at[2 * px + py, pc], dst_ref=outs[w].at[me], **sems))
            arrivals.append(functools.partial(
                pltpu.make_async_remote_copy, src_ref=ins[w].at[2 * px + py, pc],
                dst_ref=outs[w].at[4 * px + 2 * py + pc], **sems))
    return sends, arrivals, local


_scatter_start, _scatter_wait = _gather_start, _gather_wait


def _grad_sum8(name, got):
    _, R, W = got.shape
    tr = _pick(R, 256, 16)

    def body(g_ref, o_ref):
        total = g_ref[0].astype(F32)
        for d in range(1, 8):
            total = total + g_ref[d].astype(F32)
        o_ref[...] = total

    return pl.pallas_call(
        body, name=name, grid=(R // tr,), in_specs=[pl.BlockSpec((8, tr, W), lambda i: (0, i, 0))],
        out_specs=pl.BlockSpec((tr, W), lambda i: (i, 0)), out_shape=jax.ShapeDtypeStruct((R, W), F32),
        compiler_params=_params(),
    )(got)


def _half_exchange(halves):
    n = len(halves)

    def body(*refs):
        ins, outs, send_sems, recv_sems = refs[:n], refs[n:2 * n], refs[2 * n], refs[2 * n + 1]
        x, y, c = _coords()
        sends = []
        for w in range(n):
            cp = pltpu.make_async_remote_copy(
                src_ref=ins[w], dst_ref=outs[w], send_sem=send_sems.at[w], recv_sem=recv_sems.at[w],
                device_id=(x, y, 1 - c), device_id_type=MESH)
            cp.start()
            sends.append(cp)
        for cp in sends:
            cp.wait()

    got = pl.pallas_call(
        body, name="grad_half_exchange", in_specs=[ANY] * n, out_specs=[ANY] * n,
        out_shape=[jax.ShapeDtypeStruct(h.shape, F32) for h in halves],
        scratch_shapes=[pltpu.SemaphoreType.DMA((n,)), pltpu.SemaphoreType.DMA((n,))],
    )(*halves)
    c = lax.axis_index("c")
    return [jnp.where(c == 0, jnp.stack([mine, theirs]), jnp.stack([theirs, mine])) for mine, theirs in zip(halves, got)]


def _adamw_math(w, g, m, v):
    m2 = ADAM_B1 * m + (1.0 - ADAM_B1) * g
    v2 = ADAM_B2 * v + (1.0 - ADAM_B2) * (g * g)
    m_hat = m2 / (1.0 - ADAM_B1 ** ADAM_STEP)
    v_hat = v2 / (1.0 - ADAM_B2 ** ADAM_STEP)
    return -ADAM_LR * (m_hat / (jnp.sqrt(v_hat) + ADAM_EPS) + ADAM_WD * w), m2, v2


def _small_allreduce_adamw(pack_g, pack_w, pack_m, pack_v):
    def body(g_ref, w_ref, m_ref, v_ref, sum_ref, d_ref, m_out, v_out, land, send_sems, recv_sems):
        x, y, c = _coords()
        me = 4 * x + 2 * y + c
        land[me] = g_ref[...]
        sends = []
        for k in range(1, 8):
            peer = (x ^ (k >> 2), y ^ ((k >> 1) & 1), c ^ (k & 1))
            cp = pltpu.make_async_remote_copy(
                src_ref=g_ref, dst_ref=land.at[me], send_sem=send_sems.at[k - 1], recv_sem=recv_sems.at[k - 1],
                device_id=peer, device_id_type=MESH)
            cp.start()
            sends.append((cp, peer))
        for k, (cp, peer) in enumerate(sends):
            pltpu.make_async_remote_copy(
                src_ref=g_ref, dst_ref=land.at[4 * peer[0] + 2 * peer[1] + peer[2]], send_sem=send_sems.at[k],
                recv_sem=recv_sems.at[k], device_id=peer, device_id_type=MESH).wait_recv()
        for cp, _ in sends:
            cp.wait_send()
        total = land[0]
        for d in range(1, 8):
            total = total + land[d]
        sum_ref[...] = total
        d_ref[...], m_out[...], v_out[...] = _adamw_math(w_ref[...], total, m_ref[...], v_ref[...])

    vm = pl.BlockSpec(memory_space=pltpu.VMEM)
    shp = jax.ShapeDtypeStruct(pack_g.shape, F32)
    return pl.pallas_call(
        body, name="small_allreduce_adamw", in_specs=[vm] * 4, out_specs=[vm] * 4, out_shape=[shp] * 4,
        scratch_shapes=[pltpu.VMEM((8,) + pack_g.shape, F32), pltpu.SemaphoreType.DMA((7,)), pltpu.SemaphoreType.DMA((7,))],
    )(pack_g, pack_w, pack_m, pack_v)


def _adamw(name, w, g, m, v):
    R, C = w.shape
    tr = _pick(R, 256, 8)

    def body(w_ref, g_ref, m_ref, v_ref, d_out, m_out, v_out):
        d_out[...], m_out[...], v_out[...] = _adamw_math(w_ref[...], g_ref[...], m_ref[...], v_ref[...])

    spec = pl.BlockSpec((tr, C), lambda i: (i, 0))
    return pl.pallas_call(
        body, name=name, grid=(R // tr,), in_specs=[spec] * 4, out_specs=[spec] * 3,
        out_shape=[jax.ShapeDtypeStruct((R, C), F32)] * 3, compiler_params=_params(),
    )(w, g, m, v)


def _pack_small(vals, last):
    flat = jnp.concatenate([v.reshape(-1) for v in vals] + [last.reshape(-1)])
    return jnp.pad(flat, (0, SMALL_ROWS * LANES - flat.shape[0])).reshape(SMALL_ROWS, LANES)


def kernel(x, positions, g_mix, w_in, g_q_a, w_q_b, g_kv_a, w_kv_b, g_qn, g_kn, w_mla_out, ret_decay_fwd, ret_decay_bwd, w_ret_out, w_out, g_ffn, w_gate_up, w_down, loss_target, m_g_mix, m_w_in, m_g_q_a, m_w_q_b, m_g_kv_a, m_w_kv_b, m_g_qn, m_g_kn, m_w_mla_out, m_ret_decay_fwd, m_ret_decay_bwd, m_w_ret_out, m_w_out, m_g_ffn, m_w_gate_up, m_w_down, v_g_mix, v_w_in, v_g_q_a, v_w_q_b, v_g_kv_a, v_w_kv_b, v_g_qn, v_g_kn, v_w_mla_out, v_ret_decay_fwd, v_ret_decay_bwd, v_w_ret_out, v_w_out, v_g_ffn, v_w_gate_up, v_w_down):
    given = dict(locals())
    S = x.shape[1]
    xs, tgt = x.reshape(S, D_MODEL), loss_target.reshape(S, D_MODEL)
    pos = positions.reshape(S, 1).astype(F32)

    first_shards = [given[n].astype(BF16) for n in FIRST]
    my_chip = 2 * lax.axis_index("x") + lax.axis_index("y")
    tab_m, tab_r, gathered = _rope_tables_and_first_gather(
        pos, _rope_consts(MLA_NOPE, MLA_ROPE // 2), _rope_consts(0, RET_QK // 2, RET_QK), first_shards)
    wts = {n: _assemble(n, _fill_slot(g, s, my_chip)) for n, g, s in zip(FIRST, gathered, first_shards)}
    late_shards = {n: given[n].astype(BF16) for n in LATE}
    small = {n: given[n].reshape(1, -1) for n in SMALL}

    loss_row, dx, pieces, sgrads = _local_step(xs, tab_m, tab_r, tgt, wts, late_shards, small)

    halves = [_grad_sum8("grad_sum_" + n, got) for n, got in zip(FIRST + LATE, pieces)]
    reduced = _half_exchange(halves)

    out = {}
    for n, r in zip(FIRST + LATE, reduced):
        g = r.reshape(given[n].shape)
        out["grad_" + n] = g
        out["delta_" + n], out["new_m_" + n], out["new_v_" + n] = _adamw("adamw_" + n, given[n], g, given["m_" + n], given["v_" + n])

    one = jnp.ones((1,), F32)
    pk = _small_allreduce_adamw(
        _pack_small([sgrads[n] for n in SMALL], loss_row[0, :1]),
        _pack_small([given[n] for n in SMALL], 0 * one),
        _pack_small([given["m_" + n] for n in SMALL], 0 * one),
        _pack_small([given["v_" + n] for n in SMALL], one))
    off = 0
    for n in SMALL:
        sz = given[n].shape[0]
        for pre, arr in zip(["grad_", "delta_", "new_m_", "new_v_"], pk):
            out[pre + n] = arr.reshape(-1)[off:off + sz]
        off += sz
    loss = pk[0].reshape(-1)[off]

    return (loss, dx.reshape(x.shape), *[out["grad_" + n] for n in WEIGHTS], *[out["delta_" + n] for n in WEIGHTS],
            *[out["new_m_" + n] for n in WEIGHTS], *[out["new_v_" + n] for n in WEIGHTS])
```

```python
import functools
import math

import numpy as np
import jax
import jax.numpy as jnp
from jax import lax
from jax.experimental import pallas as pl
from jax.experimental.pallas import tpu as pltpu

F32 = jnp.float32
BF16 = jnp.bfloat16
MESH = pl.DeviceIdType.MESH

D_MODEL = 1024
HEADS = 8
LANES = 128
MLA_Q_RANK, MLA_KV_RANK = 256, 128
MLA_NOPE, MLA_ROPE, MLA_V = 64, 32, 64
MLA_QK = MLA_NOPE + MLA_ROPE
LN2 = math.log(2.0)
MLA_Q_SCALE = MLA_QK ** -0.5 / LN2
RET_QK, RET_V, RET_CHUNK = 64, 128, 128
RET_QK_DTYPE = BF16
RET_CHUNKS_PER_STEP = 2
FFN_HIDDEN = 2816
ROPE_THETA = 10000.0
EPS = 1e-6
IN_SPLITS = [256, 128, 32, 512, 512, 1024, 1024, 2048]
IN_OFFS = [0] + list(np.cumsum(IN_SPLITS))
ADAM_LR, ADAM_B1, ADAM_B2, ADAM_EPS, ADAM_WD, ADAM_STEP = 0.001, 0.9, 0.999, 1e-08, 0.01, 10

VMEM_LIMIT = 56 * 1024 * 1024
ROW_TILE = 256
HEAD_ROW_TILE = 2048
MM_TM, MM_TN, MM_TK, MM_KFULL = 1408, 2048, 2048, 2816
ATT_TQ = 256
ATT_BQ, ATT_BK = 1024, 1024
ATT_HEADS_PER_STEP = 8
ATT_BWD_HEADS_PER_STEP = 4

SHARDED = ["w_in", "w_q_b", "w_kv_b", "w_mla_out", "w_ret_out", "w_out", "w_gate_up", "w_down"]
COL_SHARDED = {"w_in", "w_q_b", "w_kv_b", "w_mla_out", "w_gate_up"}
FIRST = ["w_in", "w_q_b", "w_kv_b"]
LATE = ["w_mla_out", "w_ret_out", "w_out", "w_gate_up", "w_down"]
SMALL = ["g_mix", "g_q_a", "g_kv_a", "g_qn", "g_kn", "ret_decay_fwd", "ret_decay_bwd", "g_ffn"]
WEIGHTS = ["g_mix", "w_in", "g_q_a", "w_q_b", "g_kv_a", "w_kv_b", "g_qn", "g_kn", "w_mla_out",
           "ret_decay_fwd", "ret_decay_bwd", "w_ret_out", "w_out", "g_ffn", "w_gate_up", "w_down"]
SMALL_ROWS = 24


def _params(**kw):
    return pltpu.CompilerParams(vmem_limit_bytes=VMEM_LIMIT, **kw)


def _pick(dim, target, unit=128):
    if dim <= target:
        return dim
    best = None
    for d in range(unit, target + 1, unit):
        if dim % d == 0:
            best = d
    assert best is not None, (dim, target)
    return best


_DOT = {"nn": (((1,), (0,)), ((), ())), "nt": (((1,), (1,)), ((), ())), "tn": (((0,), (0,)), ((), ()))}


def _dot(a, b, mode="nn"):
    return lax.dot_general(a, b, _DOT[mode], preferred_element_type=F32)


def _rms_rows(x, g):
    x = x.astype(F32)
    return x * lax.rsqrt(jnp.mean(x * x, axis=-1, keepdims=True) + EPS) * g


def _epi_loss(acc, extras, params):
    e = acc + extras[0] - extras[1]
    dy = e * (1.0 / D_MODEL)
    loss = 0.5 * jnp.sum(jnp.mean(e * e, axis=-1, keepdims=True), axis=0, keepdims=True)
    return [dy, dy], [jnp.broadcast_to(loss, (1, LANES))]


def _epi_rms_bwd(n_out):
    def fn(acc, extras, params):
        _, vjp = jax.vjp(_rms_rows, extras[0], params[0])
        dx, dg = vjp(acc)
        return [dx + extras[1]] * n_out, [dg]
    return fn


def _mm(name, a, b, mode, out_dtype, res=None, a_gain=None, epilogue=None, shard_out=False, scatter=None):
    if mode == "nn":
        (M, K), (K2, N) = a.shape, b.shape
    elif mode == "nt":
        (M, K), (N, K2) = a.shape, b.shape
    else:
        (K, M), (K2, N) = a.shape, b.shape
    assert K == K2, (name, a.shape, b.shape)
    tm, tn = _pick(M, MM_TM), _pick(N, MM_TN)
    tk = K if K <= MM_KFULL else _pick(K, MM_TK)
    if shard_out:
        tm, tn = M // 2, N // 4
    if epilogue is not None:
        tm = _pick(M, MM_TM // 2)
    nk = K // tk
    cache_a = a_gain is not None
    if a_gain is not None:
        assert mode == "nn" and tk == K and epilogue is None and not shard_out, name
    n_in = 2 + (res is not None) + (a_gain is not None)
    extras, eparams, e_outs, e_sums = ([], [], [], [])
    if epilogue is not None:
        assert tn == N and res is None and not shard_out, name
        epi_fn, extras, eparams, e_outs, e_sums = epilogue
    n_out = len(e_outs) + len(e_sums) if epilogue is not None else 1 + cache_a
    scatter = list(scatter or [])
    n_sc = len(scatter)
    assert not n_sc or epilogue is not None, name
    ni, nj = M // tm, N // tn

    def body(*refs):
        a_ref, b_ref = refs[0], refs[1]
        base = n_in + len(extras) + len(eparams)
        ex_refs = refs[n_in:n_in + len(extras)]
        ep_refs = refs[n_in + len(extras):base]
        sc_in, out_refs = refs[base:base + n_sc], refs[base + n_sc:base + n_sc + n_out]
        sc_out = refs[base + n_sc + n_out:base + 2 * n_sc + n_out]
        scratch = refs[base + 2 * n_sc + n_out:]
        acc = scratch[0]
        i, j, k = pl.program_id(0), pl.program_id(1), pl.program_id(2)

        if n_sc:
            @pl.when(jnp.logical_and(i == 0, jnp.logical_and(j == 0, k == 0)))
            def _():
                _scatter_start(_scatter_copies(sc_in, sc_out, scratch[-2], scratch[-1]))

        @pl.when(k == 0)
        def _():
            acc[...] = jnp.zeros_like(acc)

        if cache_a:
            @pl.when(j == 0)
            def _():
                out_refs[1][...] = _rms_rows(a_ref[...], refs[n_in - 1][...]).astype(BF16)
            av = out_refs[1][...]
        else:
            av = a_ref[...].astype(BF16)
        acc[...] += _dot(av, b_ref[...].astype(BF16), mode)

        @pl.when(k == nk - 1)
        def _():
            if epilogue is None:
                r = acc[...]
                if res is not None:
                    r = r + refs[2][...].astype(F32)
                out_refs[0][...] = r.astype(out_refs[0].dtype).reshape(out_refs[0].shape)
            else:
                vals, sums = epi_fn(acc[...], [r[...] for r in ex_refs], [p[...] for p in ep_refs])
                for o_ref, v in zip(out_refs, vals):
                    o_ref[...] = v.astype(o_ref.dtype)
                for s_ref, v in zip(out_refs[len(vals):], sums):
                    @pl.when(i == 0)
                    def _(s_ref=s_ref):
                        s_ref[...] = jnp.zeros_like(s_ref)
                    s_ref[...] += v

        if n_sc:
            @pl.when(jnp.logical_and(i == ni - 1, jnp.logical_and(j == nj - 1, k == nk - 1)))
            def _():
                _scatter_wait(_scatter_copies(sc_in, sc_out, scratch[-2], scratch[-1]))

    a_spec = pl.BlockSpec((tk, tm), lambda i, j, k: (k, i)) if mode == "tn" else pl.BlockSpec((tm, tk), lambda i, j, k: (i, k))
    b_spec = pl.BlockSpec((tn, tk), lambda i, j, k: (j, k)) if mode == "nt" else pl.BlockSpec((tk, tn), lambda i, j, k: (k, j))
    o_spec = pl.BlockSpec((tm, tn), lambda i, j, k: (i, j))
    const = lambda p: pl.BlockSpec(p.shape, lambda i, j, k: (0,) * p.ndim)
    ins, specs = [a, b], [a_spec, b_spec]
    if res is not None:
        ins.append(res)
        specs.append(o_spec)
    if a_gain is not None:
        ins.append(a_gain)
        specs.append(const(a_gain))
    ins += list(extras) + list(eparams)
    specs += [o_spec] * len(extras) + [const(p) for p in eparams]
    if epilogue is not None:
        out_specs = [o_spec] * len(e_outs) + [pl.BlockSpec(s, lambda i, j, k: (0, 0)) for s in e_sums]
        out_shape = [jax.ShapeDtypeStruct((M, N), dt) for dt in e_outs] + [jax.ShapeDtypeStruct(s, F32) for s in e_sums]
    elif shard_out:
        out_specs = pl.BlockSpec((1, 1, tm, tn), lambda i, j, k: (j, i, 0, 0))
        out_shape = jax.ShapeDtypeStruct((4, 2, tm, tn), out_dtype)
    elif cache_a:
        out_specs = [o_spec, pl.BlockSpec((tm, K), lambda i, j, k: (i, 0))]
        out_shape = [jax.ShapeDtypeStruct((M, N), out_dtype), jax.ShapeDtypeStruct((M, K), BF16)]
    else:
        out_specs, out_shape = o_spec, jax.ShapeDtypeStruct((M, N), out_dtype)
    scratch_shapes = [pltpu.VMEM((tm, tn), F32)]
    if n_sc:
        ins += scatter
        specs += [ANY] * n_sc
        out_specs = list(out_specs) + [ANY] * n_sc
        out_shape = list(out_shape) + [jax.ShapeDtypeStruct((8,) + g.shape[2:], g.dtype) for g in scatter]
        scratch_shapes += [pltpu.SemaphoreType.DMA((8 * n_sc,)), pltpu.SemaphoreType.DMA((7 * n_sc,))]
    res_ = pl.pallas_call(
        body, name=name, grid=(ni, nj, nk), in_specs=specs, out_specs=out_specs, out_shape=out_shape,
        scratch_shapes=scratch_shapes, compiler_params=_params(),
    )(*ins)
    if n_sc:
        return list(res_[:n_out]) + [list(res_[n_out:])]
    return res_


def _piece_spec(tm, piece):
    _, w, c0, per_group = piece
    if per_group:
        return pl.BlockSpec((tm, w), lambda i, g: (i, c0 + g))
    return pl.BlockSpec((tm, w), lambda i, g: (i, c0))


def _const_spec(p):
    return pl.BlockSpec(p.shape, lambda i, g: (0, 0))


def _rowwise(name, fn, params, rows, auxs, outs, tm, groups=1):
    S = rows[0][0].shape[0]
    tm = min(tm, S)
    n_p, n_r, n_a = len(params), len(rows), len(auxs)

    def body(*refs):
        p = [r[...] for r in refs[:n_p]]
        r_ = [r[...] for r in refs[n_p:n_p + n_r]]
        a_ = [r[...] for r in refs[n_p + n_r:n_p + n_r + n_a]]
        for o_ref, o in zip(refs[n_p + n_r + n_a:], fn(p, r_, a_)):
            o_ref[...] = o.astype(o_ref.dtype)

    out_specs, out_shape = [], []
    for w, dt, per_group in outs:
        out_specs.append(_piece_spec(tm, (None, w, 0, per_group)))
        out_shape.append(jax.ShapeDtypeStruct((S, w * (groups if per_group else 1)), dt))
    return pl.pallas_call(
        body, name=name, grid=(S // tm, groups),
        in_specs=[_const_spec(p) for p in params] + [_piece_spec(tm, q) for q in list(rows) + list(auxs)],
        out_specs=out_specs, out_shape=out_shape, compiler_params=_params(),
    )(*params, *[q[0] for q in list(rows) + list(auxs)])


def _rowwise_vjp(name, fn, params, rows, auxs, cots, d_outs, tm, groups=1, adds=None):
    S = rows[0][0].shape[0]
    tm = min(tm, S)
    n_p, n_r, n_a = len(params), len(rows), len(auxs)
    cot_flat = [q for c in cots for q in c]
    adds = adds or [None] * len(d_outs)
    add_flat = [q for q in adds if q is not None]
    n_c, n_add = len(cot_flat), len(add_flat)
    shared = [not all(rows[k][3] for k in idx) and groups > 1 for idx, _ in d_outs]

    def body(*refs):
        pos = 0
        p = [r[...] for r in refs[pos:pos + n_p]]; pos += n_p
        r_ = [r[...] for r in refs[pos:pos + n_r]]; pos += n_r
        a_ = [r[...] for r in refs[pos:pos + n_a]]; pos += n_a
        c_refs = refs[pos:pos + n_c]; pos += n_c
        add_refs = list(refs[pos:pos + n_add]); pos += n_add
        d_refs = refs[pos:pos + len(d_outs)]; pos += len(d_outs)
        dp_refs = refs[pos:]
        i, g = pl.program_id(0), pl.program_id(1)
        outs, vjp_fn = jax.vjp(lambda pp, rr: fn(pp, rr, a_), p, r_)
        cts, ci = [], 0
        for c, o in zip(cots, outs):
            t = c_refs[ci][...].astype(F32)
            for extra in c_refs[ci + 1:ci + len(c)]:
                t = t + extra[...].astype(F32)
            ci += len(c)
            cts.append(t.astype(o.dtype))
        dp, dr = vjp_fn(cts)
        for (idx, _), d_ref, add, sh in zip(d_outs, d_refs, adds, shared):
            val = dr[idx[0]].astype(F32) if len(idx) == 1 else jnp.concatenate([dr[k].astype(F32) for k in idx], axis=1)
            if add is not None:
                val = val + add_refs.pop(0)[...].astype(F32)
            if sh:
                @pl.when(g == 0)
                def _(d_ref=d_ref):
                    d_ref[...] = jnp.zeros_like(d_ref)
                d_ref[...] += val.astype(d_ref.dtype)
            else:
                d_ref[...] = val.astype(d_ref.dtype)
        first = jnp.logical_and(i == 0, g == 0)
        for dp_ref, d in zip(dp_refs, dp):
            @pl.when(first)
            def _(dp_ref=dp_ref):
                dp_ref[...] = jnp.zeros_like(dp_ref)
            dp_ref[...] += d.astype(F32)

    out_specs, out_shape = [], []
    for (idx, dt), sh in zip(d_outs, shared):
        w = sum(rows[k][1] for k in idx)
        per_group = (not sh) and groups > 1
        out_specs.append(_piece_spec(tm, (None, w, 0, per_group)))
        out_shape.append(jax.ShapeDtypeStruct((S, w * (groups if per_group else 1)), dt))
    for p in params:
        out_specs.append(_const_spec(p))
        out_shape.append(jax.ShapeDtypeStruct(p.shape, F32))
    pieces = list(rows) + list(auxs) + cot_flat + add_flat
    res = pl.pallas_call(
        body, name=name, grid=(S // tm, groups),
        in_specs=[_const_spec(p) for p in params] + [_piece_spec(tm, q) for q in pieces],
        out_specs=out_specs, out_shape=out_shape, compiler_params=_params(),
    )(*params, *[q[0] for q in pieces])
    return list(res[:len(d_outs)]), list(res[len(d_outs):])


def _lane_roll(x, shift):
    @jax.custom_vjp
    def roll(v):
        return pltpu.roll(v, shift, 1)

    roll.defvjp(lambda v: (roll(v), None), lambda _, ct: (pltpu.roll(ct, LANES - shift, 1),))
    return roll(x)


@jax.custom_vjp
def _sigmoid(x):
    return 1.0 / (1.0 + jnp.exp(-x))


def _sigmoid_fwd(x):
    s = _sigmoid(x)
    return s, s


_sigmoid.defvjp(_sigmoid_fwd, lambda s, ct: (ct * s * (1.0 - s),))


def _rope(x, cos, sin_lo, sin_hi, half):
    return x * cos + _lane_roll(x, LANES - half) * sin_lo + _lane_roll(x, half) * sin_hi


def _f_rope_table(p, r, a):
    inv, first, second, fixed = p
    ang = a[0] * inv
    cs, sn = jnp.cos(ang), jnp.sin(ang)
    return [cs * (first + second) + fixed, -sn * first, sn * second]


def _f_rms(p, r, a):
    x = r[0].astype(F32)
    return [x * lax.rsqrt(jnp.mean(x * x, axis=-1, keepdims=True) + EPS) * p[0]]


def _f_mla_a(p, r, a):
    return _f_rms([p[0]], [r[0]], a) + _f_rms([p[1]], [r[1]], a)


def _f_mla_b(p, r, a):
    def norm_rope(v, g):
        ms = jnp.sum(v * v, axis=-1, keepdims=True) * (1.0 / MLA_QK)
        return _rope(v * lax.rsqrt(ms + EPS) * g, a[0], a[1], a[2], MLA_ROPE // 2)

    return [norm_rope(r[0].astype(F32), p[0]) * MLA_Q_SCALE, norm_rope(r[1].astype(F32) + r[2].astype(F32), p[1])]


def _f_ret_rope(p, r, a):
    q = _rope(r[0].astype(F32), a[0], a[1], a[2], RET_QK // 2)
    k = _rope(r[1].astype(F32), a[0], a[1], a[2], RET_QK // 2)
    return [q, k * (RET_QK ** -0.5)]


def _f_ret_post(p, r, a):
    ret = r[0].astype(F32) + r[1].astype(F32)
    g = r[2].astype(F32)
    normed = ret * lax.rsqrt(jnp.mean(ret * ret, axis=-1, keepdims=True) + EPS)
    return [g * _sigmoid(g) * normed]


def _f_merge(p, r, a):
    return [_sigmoid(r[0].astype(F32)) * r[2].astype(F32) + _sigmoid(r[1].astype(F32)) * r[3].astype(F32)]


def _f_swiglu(p, r, a):
    g = r[0].astype(F32)
    return [g * _sigmoid(g) * r[1].astype(F32)]


def _f_add(p, r, a):
    return [r[0].astype(F32) + r[1].astype(F32)]


def _flash_fwd(q, k, kv, shards):
    S = q.shape[0]
    tq = min(ATT_TQ, S)
    nq = S // tq
    n = len(shards)

    def body(q_ref, k_ref, v_ref, *rest):
        shard_refs, (o_ref, lse_ref), gathered = rest[:n], rest[n:n + 2], rest[n + 2:2 * n + 2]
        send_sems, recv_sems = rest[2 * n + 2:]
        h, qi = pl.program_id(0), pl.program_id(1)

        @pl.when(jnp.logical_and(h == 0, qi == 0))
        def _():
            _gather_start(_gather_copies(shard_refs, gathered, send_sems, recv_sems))

        for hh in range(hps):
            lanes = slice(hh * LANES, (hh + 1) * LANES)
            s = _dot(q_ref[:, lanes], k_ref[:, lanes], "nt")
            m = jnp.max(s, axis=-1, keepdims=True)
            p = jnp.exp2(s - m)
            l = jnp.sum(p, axis=-1, keepdims=True)
            pair = slice(hh // 2 * LANES, (hh // 2 + 1) * LANES)
            o_h = (_dot(p.astype(BF16), v_ref[:, pair]) / l * _ret_head_mask(hh)).astype(o_ref.dtype)
            if hh % 2 == 0:
                o_ref[:, pair] = o_h
            else:
                o_ref[:, pair] += o_h
            lse_ref[:, lanes] = jnp.broadcast_to(m + jnp.log2(l), (tq, LANES))

        @pl.when(jnp.logical_and(h == HEADS // hps - 1, qi == nq - 1))
        def _():
            _gather_wait(_gather_copies(shard_refs, gathered, send_sems, recv_sems))

    hps = ATT_HEADS_PER_STEP
    qs = pl.BlockSpec((tq, hps * LANES), lambda h, i: (i, h))
    vw = hps * MLA_V
    v0 = HEADS * LANES // vw
    res = pl.pallas_call(
        body, name="mla_fwd", grid=(HEADS // hps, nq),
        in_specs=[qs, pl.BlockSpec((S, hps * LANES), lambda h, i: (0, h), pipeline_mode=pl.Buffered(1)),
                  pl.BlockSpec((S, vw), lambda h, i: (0, v0 + h), pipeline_mode=pl.Buffered(1))]
        + [ANY] * n,
        out_specs=[pl.BlockSpec((tq, vw), lambda h, i: (i, h)), qs] + [ANY] * n,
        out_shape=[jax.ShapeDtypeStruct((S, HEADS * MLA_V), BF16), jax.ShapeDtypeStruct((S, HEADS * LANES), F32)]
        + [jax.ShapeDtypeStruct((4,) + s.shape, s.dtype) for s in shards],
        scratch_shapes=[pltpu.SemaphoreType.DMA((3 * n,)), pltpu.SemaphoreType.DMA((3 * n,))],
        compiler_params=_params(),
    )(q, k, kv, *shards)
    mine = 2 * lax.axis_index("x") + lax.axis_index("y")
    return res[0], res[1], [_fill_slot(g, s, mine) for g, s in zip(res[2:], shards)]


def _flash_bwd(q, k, kv, do, lse, o, gs):
    S = q.shape[0]
    tq, tk = min(ATT_BQ, S), min(ATT_BK, S)
    nq, nkt = S // tq, S // tk
    n = len(gs)

    def body(q_ref, k_ref, v_ref, do_ref, lse_ref, o_ref, *rest):
        g_refs, (dq_ref, dk_ref, dv_ref), got_refs = rest[:n], rest[n:n + 3], rest[n + 3:2 * n + 3]
        dk_sc, dv_sc, send_sems, recv_sems = rest[2 * n + 3:]
        h, ki, qi = pl.program_id(0), pl.program_id(1), pl.program_id(2)

        @pl.when(jnp.logical_and(h == 0, jnp.logical_and(ki == 0, qi == 0)))
        def _():
            _scatter_start(_scatter_copies(g_refs, got_refs, send_sems, recv_sems))

        @pl.when(jnp.logical_and(ki == 0, qi == 0))
        def _():
            dq_ref[...] = jnp.zeros_like(dq_ref)

        @pl.when(qi == 0)
        def _():
            dk_sc[...] = jnp.zeros_like(dk_sc)
            dv_sc[...] = jnp.zeros_like(dv_sc)

        rows = pl.ds(pl.multiple_of(qi * tq, tq), tq)
        for hh in range(hps):
            lanes = slice(hh * LANES, (hh + 1) * LANES)
            pair = slice(hh // 2 * LANES, (hh // 2 + 1) * LANES)
            qv, kv_ = q_ref[:, lanes], k_ref[:, lanes]
            do32 = do_ref[:, pair].astype(F32) * _ret_head_mask(hh)
            dov = do32.astype(BF16)
            p = jnp.exp2(_dot(qv, kv_, "nt") - lse_ref[:, lanes][:, :1])
            dp = _dot(dov, v_ref[:, pair], "nt")
            delta = jnp.sum(do32 * o_ref[:, pair].astype(F32), axis=-1, keepdims=True)
            ds = (p * (dp - delta) * LN2).astype(BF16)
            dv_sc[:, pair] += _dot(p.astype(BF16), dov, "tn")
            dk_sc[:, lanes] += _dot(ds, qv, "tn")
            dq_ref[rows, lanes] += _dot(ds, kv_)

        @pl.when(qi == nq - 1)
        def _():
            dk_ref[...] = dk_sc[...].astype(dk_ref.dtype)
            dv_ref[...] = dv_sc[...].astype(dv_ref.dtype)

        @pl.when(jnp.logical_and(h == HEADS // hps - 1, jnp.logical_and(ki == nkt - 1, qi == nq - 1)))
        def _():
            _scatter_wait(_scatter_copies(g_refs, got_refs, send_sems, recv_sems))

    hps = ATT_BWD_HEADS_PER_STEP
    qs = pl.BlockSpec((tq, hps * LANES), lambda h, j, i: (i, h))
    ks = pl.BlockSpec((tk, hps * LANES), lambda h, j, i: (j, h))
    vw = hps * MLA_V
    v0 = HEADS * LANES // vw
    qv_s = pl.BlockSpec((tq, vw), lambda h, j, i: (i, h))
    kv_s = pl.BlockSpec((tk, vw), lambda h, j, i: (j, h))
    res = pl.pallas_call(
        body, name="mla_bwd", grid=(HEADS // hps, nkt, nq),
        in_specs=[qs, ks, pl.BlockSpec((tk, vw), lambda h, j, i: (j, v0 + h)), qv_s, qs, qv_s] + [ANY] * n,
        out_specs=[pl.BlockSpec((S, hps * LANES), lambda h, j, i: (0, h), pipeline_mode=pl.Buffered(1)), ks, kv_s] + [ANY] * n,
        out_shape=[jax.ShapeDtypeStruct((S, HEADS * LANES), F32), jax.ShapeDtypeStruct((S, HEADS * LANES), BF16),
                   jax.ShapeDtypeStruct((S, HEADS * MLA_V), BF16)]
        + [jax.ShapeDtypeStruct((8,) + g.shape[2:], g.dtype) for g in gs],
        scratch_shapes=[pltpu.VMEM((tk, hps * LANES), F32), pltpu.VMEM((tk, vw), F32)]
        + [pltpu.SemaphoreType.DMA((8 * n,)), pltpu.SemaphoreType.DMA((7 * n,))],
        compiler_params=_params(),
    )(q, k, kv, do, lse, o, *gs)
    return res[0], res[1], res[2], list(res[3:])


def _ret_tables(decay_row, backward):
    C = RET_CHUNK
    lg = -jnp.exp(decay_row)
    t = lax.broadcasted_iota(jnp.int32, (C, C), 0).astype(F32)
    s = lax.broadcasted_iota(jnp.int32, (C, C), 1).astype(F32)
    ridx = lax.broadcasted_iota(jnp.int32, (C, LANES), 0).astype(F32)
    if backward:
        dist, mask, aw, bw = s - t, s > t, C - ridx, ridx
    else:
        dist, mask, aw, bw = t - s, t >= s, ridx + 1.0, C - 1.0 - ridx
    dist = jnp.maximum(dist, 0.0)
    din = jnp.where(mask, jnp.exp(lg[:, :1] * dist), 0.0)
    return dict(din=din, dist=dist, a=jnp.exp(lg * aw), b=jnp.exp(lg * bw), c=jnp.exp(lg * C), aw=aw, bw=bw)


def _ret_fill_tables(decs, din_sc, a_sc, b_sc):
    for d, dec in enumerate(decs):
        for h in range(HEADS):
            tb = _ret_tables(dec[h:h + 1, :], d == 1)
            din_sc[d, h], a_sc[d, h], b_sc[d, h] = tb["din"], tb["a"], tb["b"]


def _ret_head_mask(h):
    lane = lax.broadcasted_iota(jnp.int32, (1, LANES), 1)
    return jnp.where((lane >= RET_QK) == bool(h % 2), 1.0, 0.0).astype(F32)


def _ret_fwd(qr, kr, proj, v_block, dec_f, dec_b):
    S = qr.shape[0]
    C = RET_CHUNK
    n = S // C
    nc = min(RET_CHUNKS_PER_STEP, n)
    nb = n // nc
    W = HEADS * LANES

    def body(qf, kf, vf, qb, kb, vb, df, db, of, ob, sf_out, sb_out, st, din_sc, a_sc, b_sc):
        @pl.when(pl.program_id(0) == 0)
        def _():
            st[...] = jnp.zeros_like(st)
            _ret_fill_tables((df, db), din_sc, a_sc, b_sc)

        for d, (q_ref, k_ref, v_ref, dec, o_ref, s_out) in enumerate(
                [(qf, kf, vf, df, of, sf_out), (qb, kb, vb, db, ob, sb_out)]):
            for h in range(HEADS):
                lanes, pair = slice(h * LANES, (h + 1) * LANES), slice(h // 2 * LANES, (h // 2 + 1) * LANES)
                mine = _ret_head_mask(h)
                din, a, b = din_sc[d, h], a_sc[d, h], b_sc[d, h]
                c = jnp.exp(-jnp.exp(dec[h:h + 1, :]) * C)
                for ci in (range(nc) if d == 0 else reversed(range(nc))):
                    rows = slice(ci * C, (ci + 1) * C)
                    qf32, kf32 = q_ref[rows, pair].astype(F32) * mine, k_ref[rows, pair].astype(F32) * mine
                    v = v_ref[rows, lanes]
                    state = st[d, h]
                    s_out[ci, h] = state
                    inner = _dot((_dot(qf32.astype(BF16), kf32.astype(BF16), "nt") * din).astype(BF16), v)
                    cross = _dot((qf32 * a).astype(BF16), state.astype(BF16))
                    o_ref[rows, lanes] = inner + cross
                    st[d, h] = state * c + _dot((kf32 * b).astype(BF16), v, "tn")

    fw = lambda c0, w=W: pl.BlockSpec((nc * C, w), lambda j: (j, c0))
    bw = lambda c0, w=W: pl.BlockSpec((nc * C, w), lambda j: (nb - 1 - j, c0))
    dec_spec = pl.BlockSpec((HEADS, LANES), lambda j: (0, 0))
    st_shape = jax.ShapeDtypeStruct((n, HEADS, LANES, LANES), F32)
    QW = W // 2
    return pl.pallas_call(
        body, name="ret_fwd", grid=(nb,),
        in_specs=[fw(0, QW), fw(0, QW), fw(v_block), bw(0, QW), bw(0, QW), bw(v_block), dec_spec, dec_spec],
        out_specs=[fw(0), bw(0), pl.BlockSpec((nc, HEADS, LANES, LANES), lambda j: (j, 0, 0, 0)),
                   pl.BlockSpec((nc, HEADS, LANES, LANES), lambda j: (nb - 1 - j, 0, 0, 0))],
        out_shape=[jax.ShapeDtypeStruct((S, W), F32)] * 2 + [st_shape] * 2,
        scratch_shapes=[pltpu.VMEM((2, HEADS, LANES, LANES), F32), pltpu.VMEM((2, HEADS, C, C), F32),
                        pltpu.VMEM((2, HEADS, C, LANES), F32), pltpu.VMEM((2, HEADS, C, LANES), F32)],
        compiler_params=_params(),
    )(qr, kr, proj, qr, kr, proj, dec_f, dec_b)


def _ret_bwd(qr, kr, proj, v_block, dret, sf, sb, dec_f, dec_b):
    S = qr.shape[0]
    C = RET_CHUNK
    n = S // C
    nc = min(RET_CHUNKS_PER_STEP, n)
    nb = n // nc
    W = HEADS * LANES

    def body(qf, kf, vf, gf, sf_ref, qb, kb, vb, gb, sb_ref, df, db,
             dqf, dkf, dvf, dqb, dkb, dvb, ddf, ddb, ds_sc, din_sc, a_sc, b_sc):
        j = pl.program_id(0)

        @pl.when(j == 0)
        def _():
            ds_sc[...] = jnp.zeros_like(ds_sc)
            ddf[...] = jnp.zeros_like(ddf)
            ddb[...] = jnp.zeros_like(ddb)
            _ret_fill_tables((df, db), din_sc, a_sc, b_sc)

        for d, (q_ref, k_ref, v_ref, g_ref, s_ref, dec, dq_ref, dk_ref, dv_ref, dd_ref) in enumerate(
                [(qf, kf, vf, gf, sf_ref, df, dqf, dkf, dvf, ddf), (qb, kb, vb, gb, sb_ref, db, dqb, dkb, dvb, ddb)]):
            static = _ret_tables(dec[0:1, :], d == 1)
            dist, aw, bw_ = static["dist"], static["aw"], static["bw"]
            for h in range(HEADS):
                lanes, pair = slice(h * LANES, (h + 1) * LANES), slice(h // 2 * LANES, (h // 2 + 1) * LANES)
                mine = _ret_head_mask(h)
                din, a, b = din_sc[d, h], a_sc[d, h], b_sc[d, h]
                c = jnp.exp(-jnp.exp(dec[h:h + 1, :]) * C)
                dlg = jnp.zeros((1, 1), F32)
                for ci in (reversed(range(nc)) if d == 0 else range(nc)):
                    rows = slice(ci * C, (ci + 1) * C)
                    v, g = v_ref[rows, lanes], g_ref[rows, lanes]
                    qf32, kf32 = q_ref[rows, pair].astype(F32) * mine, k_ref[rows, pair].astype(F32) * mine
                    q, k = qf32.astype(BF16), kf32.astype(BF16)
                    state, dstate = s_ref[ci, h], ds_sc[d, h]
                    dstate_b = dstate.astype(BF16)
                    dp = _dot(g, v, "nt")
                    a_ = _dot(q, k, "nt")
                    da = (dp * din).astype(BF16)
                    g1 = _dot(g, state.astype(BF16), "nt")
                    g2 = _dot(v, dstate_b, "nt")
                    dq_h = (_dot(da, k) + g1 * a).astype(dq_ref.dtype)
                    dk_h = (_dot(da, q, "tn") + g2 * b).astype(dk_ref.dtype)
                    if h % 2 == 0:
                        dq_ref[rows, pair], dk_ref[rows, pair] = dq_h, dk_h
                    else:
                        dq_ref[rows, pair] += dq_h
                        dk_ref[rows, pair] += dk_h
                    dv_ref[rows, lanes] = (_dot((a_ * din).astype(BF16), g, "tn")
                                           + _dot((kf32 * b).astype(BF16), dstate_b)).astype(dv_ref.dtype)
                    dlg = dlg + (jnp.sum(dp * a_ * din * dist, keepdims=True)
                                 + jnp.sum(g1 * qf32 * a * aw, keepdims=True)
                                 + jnp.sum(g2 * kf32 * b * bw_, keepdims=True)
                                 + C * jnp.sum(c * dstate * state, keepdims=True))
                    ds_sc[d, h] = dstate * c + _dot((qf32 * a).astype(BF16), g, "tn")
                dd_ref[h:h + 1, :] += jnp.broadcast_to(dlg, (1, LANES))

        @pl.when(j == nb - 1)
        def _():
            ddf[...] = ddf[...] * -jnp.exp(df[...])
            ddb[...] = ddb[...] * -jnp.exp(db[...])

    fw = lambda c0, w=W: pl.BlockSpec((nc * C, w), lambda j: (nb - 1 - j, c0))
    bw = lambda c0, w=W: pl.BlockSpec((nc * C, w), lambda j: (j, c0))
    dec_spec = pl.BlockSpec((HEADS, LANES), lambda j: (0, 0))
    QW = W // 2
    act, act_qk = jax.ShapeDtypeStruct((S, W), BF16), jax.ShapeDtypeStruct((S, QW), BF16)
    return pl.pallas_call(
        body, name="ret_bwd", grid=(nb,),
        in_specs=[fw(0, QW), fw(0, QW), fw(v_block), fw(0),
                  pl.BlockSpec((nc, HEADS, LANES, LANES), lambda j: (nb - 1 - j, 0, 0, 0)),
                  bw(0, QW), bw(0, QW), bw(v_block), bw(0), pl.BlockSpec((nc, HEADS, LANES, LANES), lambda j: (j, 0, 0, 0)),
                  dec_spec, dec_spec],
        out_specs=[fw(0, QW), fw(0, QW), fw(0), bw(0, QW), bw(0, QW), bw(0)] + [dec_spec] * 2,
        out_shape=[act_qk, act_qk, act, act_qk, act_qk, act] + [jax.ShapeDtypeStruct((HEADS, LANES), F32)] * 2,
        scratch_shapes=[pltpu.VMEM((2, HEADS, LANES, LANES), F32), pltpu.VMEM((2, HEADS, C, C), F32),
                        pltpu.VMEM((2, HEADS, C, LANES), F32), pltpu.VMEM((2, HEADS, C, LANES), F32)],
        compiler_params=_params(),
    )(qr, kr, proj, dret, sf, qr, kr, proj, dret, sb, dec_f, dec_b)


def _pad_heads(w, hd):
    K = w.shape[0]
    return jnp.pad(w.reshape(K, HEADS, hd), ((0, 0), (0, 0), (0, LANES - hd))).reshape(K, HEADS * LANES)


def _unpad_heads(w, hd):
    K = w.shape[0]
    return w.reshape(K, HEADS, LANES)[:, :, :hd].reshape(K, HEADS * hd)


def _rope_consts(first_lane, half, period=LANES):
    lane = np.arange(LANES) % period
    first = ((lane >= first_lane) & (lane < first_lane + half)).astype(np.float32)
    second = ((lane >= first_lane + half) & (lane < first_lane + 2 * half)).astype(np.float32)
    fixed = (lane < first_lane).astype(np.float32)
    j = np.where(first > 0, lane - first_lane, lane - first_lane - half) * (first + second)
    inv = (ROPE_THETA ** (-j.astype(np.float64) / half)).astype(np.float32)
    return [jnp.asarray(v.reshape(1, LANES), F32) for v in (inv, first, second, fixed)]


def _assemble(name, gathered):
    if name in COL_SHARDED:
        return jnp.transpose(gathered, (1, 0, 2)).reshape(gathered.shape[1], 4 * gathered.shape[2])
    return gathered.reshape(4 * gathered.shape[1], gathered.shape[2])


def _split_for_reducers(name, g, dtype):
    if name in COL_SHARDED:
        K, N4 = g.shape
        return jnp.transpose(g.reshape(2, K // 2, 4, N4 // 4), (2, 0, 1, 3)).astype(dtype)
    return g.reshape(4, 2, g.shape[0] // 8, g.shape[1]).astype(dtype)


def _local_step(x, tab_m, tab_r, tgt, wts, late_shards, small):
    w_in = wts["w_in"]
    seg = [w_in[:, IN_OFFS[i]:IN_OFFS[i + 1]] for i in range(8)]
    kr_w = jnp.pad(seg[2], ((0, 0), (MLA_NOPE, LANES - MLA_QK)))
    w_in_p = jnp.concatenate([seg[7], seg[5], seg[6], seg[3], seg[4], seg[0], seg[1], kr_w], axis=1)
    QR0, KR0, CQ0 = 4096, 4608, 5120
    w_qb_p = _pad_heads(wts["w_q_b"], MLA_QK)
    kvw = wts["w_kv_b"].reshape(MLA_KV_RANK, HEADS, MLA_NOPE + MLA_V)
    pad_kv = lambda t: jnp.pad(t, ((0, 0), (0, 0), (0, LANES - t.shape[2]))).reshape(MLA_KV_RANK, HEADS * LANES)
    w_kn_p, w_v = pad_kv(kvw[:, :, :MLA_NOPE]), kvw[:, :, MLA_NOPE:].reshape(MLA_KV_RANK, HEADS * MLA_V)
    w_kv_p = jnp.concatenate([w_kn_p, w_v], axis=1)
    g_qn_p = jnp.pad(small["g_qn"], ((0, 0), (0, LANES - MLA_QK)))
    g_kn_p = jnp.pad(small["g_kn"], ((0, 0), (0, LANES - MLA_QK)))
    dec_f = jnp.broadcast_to(small["ret_decay_fwd"].reshape(HEADS, 1), (HEADS, LANES))
    dec_b = jnp.broadcast_to(small["ret_decay_bwd"].reshape(HEADS, 1), (HEADS, LANES))
    T, N = True, False
    RT, HT = ROW_TILE, HEAD_ROW_TILE
    RW = 2 * ROW_TILE

    aux_m = [(t, LANES, 0, N) for t in tab_m]
    aux_r = [(t, LANES, 0, N) for t in tab_r]

    proj, h = _mm("proj", x, w_in_p, "nn", BF16, a_gain=small["g_mix"])
    rows_a = [(proj, MLA_Q_RANK, CQ0 // MLA_Q_RANK, N), (proj, MLA_KV_RANK, (CQ0 + MLA_Q_RANK) // MLA_KV_RANK, N)]
    cqn, ckvn = _rowwise("mla_lat_norm", _f_mla_a, [small["g_q_a"], small["g_kv_a"]], rows_a, [],
                         [(MLA_Q_RANK, BF16, N), (MLA_KV_RANK, BF16, N)], RW)
    qraw = _mm("mla_q_up", cqn, w_qb_p, "nn", BF16)
    kv = _mm("mla_kv_up", ckvn, w_kv_p, "nn", BF16)
    rows_b = [(qraw, LANES, 0, T), (kv, LANES, 0, T), (proj, LANES, (CQ0 + MLA_Q_RANK + MLA_KV_RANK) // LANES, N)]
    q, k = _rowwise("mla_qk_norm_rope", _f_mla_b, [g_qn_p, g_kn_p], rows_b, aux_m, [(LANES, BF16, T)] * 2, HT, HEADS)
    o, lse, late = _flash_fwd(q, k, kv, [late_shards[n] for n in LATE])
    wl = {n: _assemble(n, g) for n, g in zip(LATE, late)}
    w_mla_p = wl["w_mla_out"]
    w_ret_out, w_out, w_gu, w_down = wl["w_ret_out"], wl["w_out"], wl["w_gate_up"], wl["w_down"]
    y_a = _mm("mla_out", o, w_mla_p, "nn", BF16)
    rows_rr = [(proj, LANES, QR0 // LANES, T), (proj, LANES, KR0 // LANES, T)]
    qr, kr = _rowwise("ret_rope", _f_ret_rope, [], rows_rr, aux_r, [(LANES, RET_QK_DTYPE, T)] * 2, HT, HEADS // 2)
    ret_f, ret_b, st_f, st_b = _ret_fwd(qr, kr, proj, 2, dec_f, dec_b)
    rows_rp = [(ret_f, LANES, 0, T), (ret_b, LANES, 0, T), (proj, LANES, 24, T)]
    (o_b,) = _rowwise("ret_post", _f_ret_post, [], rows_rp, [], [(LANES, BF16, T)], HT, HEADS)
    y_b = _mm("ret_out", o_b, w_ret_out, "nn", BF16)
    rows_m = [(proj, D_MODEL, 0, N), (proj, D_MODEL, 1, N), (y_a, D_MODEL, 0, N), (y_b, D_MODEL, 0, N)]
    (merged,) = _rowwise("merge", _f_merge, [], rows_m, [], [(D_MODEL, BF16, N)], RW)
    x2 = _mm("mix_out", merged, w_out, "nn", F32, res=x)
    gu, h2 = _mm("ffn_gate_up", x2, w_gu, "nn", BF16, a_gain=small["g_ffn"])
    rows_sw = [(gu, FFN_HIDDEN, 0, N), (gu, FFN_HIDDEN, 1, N)]
    (act,) = _rowwise("swiglu", _f_swiglu, [], rows_sw, [], [(FFN_HIDDEN, BF16, N)], RT)
    dy, dy_b16, loss_row = _mm("ffn_down", act, w_down, "nn", None,
                               epilogue=(_epi_loss, [x2, tgt], [], [F32, BF16], [(1, LANES)]))

    dact = _mm("d_act", dy_b16, w_down, "nt", BF16)
    dw_down = _mm("dw_down", act, dy_b16, "tn", BF16)
    (dgu,), _ = _rowwise_vjp("swiglu_bwd", _f_swiglu, [], rows_sw, [], [[(dact, FFN_HIDDEN, 0, N)]], [([0, 1], BF16)], RT)
    dx2, dx2_b16, dg_ffn = _mm("d_h2", dgu, w_gu, "nt", None,
                               epilogue=(_epi_rms_bwd(2), [x2, dy], [small["g_ffn"]], [F32, BF16], [(1, D_MODEL)]))
    dw_gu = _mm("dw_gate_up", h2, dgu, "tn", BF16, shard_out=True)
    dmerged = _mm("d_merged", dx2_b16, w_out, "nt", BF16)
    dw_out = _mm("dw_out", merged, dx2_b16, "tn", BF16)
    (dgl, dy_a, dy_b), _ = _rowwise_vjp("merge_bwd", _f_merge, [], rows_m, [], [[(dmerged, D_MODEL, 0, N)]],
                                        [([0, 1], BF16), ([2], BF16), ([3], BF16)], RW)
    do_b = _mm("d_ret_o", dy_b, w_ret_out, "nt", BF16)
    dw_ret_out = _mm("dw_ret_out", o_b, dy_b, "tn", BF16)
    (dret, dg_r), _ = _rowwise_vjp("ret_post_bwd", _f_ret_post, [], rows_rp, [], [[(do_b, LANES, 0, T)]],
                                   [([0], BF16), ([2], BF16)], HT, HEADS)
    dqf, dkf, dvf, dqb, dkb, dvb, ddec_f, ddec_b = _ret_bwd(qr, kr, proj, 2, dret, st_f, st_b, dec_f, dec_b)
    (dq_r, dk_r), _ = _rowwise_vjp("ret_rope_bwd", _f_ret_rope, [], rows_rr, aux_r,
                                   [[(dqf, LANES, 0, T), (dqb, LANES, 0, T)], [(dkf, LANES, 0, T), (dkb, LANES, 0, T)]],
                                   [([0], BF16), ([1], BF16)], HT, HEADS // 2)
    (dv_r,) =_rowwise("ret_dv_sum", _f_add, [], [(dvf, D_MODEL, 0, N), (dvb, D_MODEL, 0, N)], [], [(D_MODEL, BF16, N)], RW)
    do = _mm("d_mla_o", dy_a, w_mla_p, "nt", BF16)
    dw_mla = _mm("dw_mla_out", o, dy_a, "tn", BF16)
    late_grads = {"w_mla_out": dw_mla, "w_ret_out": dw_ret_out, "w_out": dw_out, "w_down": dw_down}
    late_gs = [dw_gu if n == "w_gate_up" else _split_for_reducers(n, late_grads[n], BF16) for n in LATE]
    dq, dk, dv, late_got = _flash_bwd(q, k, kv, do, lse, o, late_gs)
    (dqraw, dkn, dkr), (dg_qn_p, dg_kn_p) = _rowwise_vjp(
        "mla_qk_norm_rope_bwd", _f_mla_b, [g_qn_p, g_kn_p], rows_b, aux_m, [[(dq, LANES, 0, T)], [(dk, LANES, 0, T)]],
        [([0], BF16), ([1], BF16), ([2], F32)], HT, HEADS)
    dckvn = _mm("d_ckvn_v", dv, w_v, "nt", BF16, res=_mm("d_ckvn_k", dkn, w_kn_p, "nt", F32))
    dw_kn_p = _mm("dw_kv_k", ckvn, dkn, "tn", BF16)
    dw_v = _mm("dw_kv_v", ckvn, dv, "tn", BF16)
    dcqn = _mm("d_cqn", dqraw, w_qb_p, "nt", BF16)
    dw_qb_p = _mm("dw_q_b", cqn, dqraw, "tn", BF16)
    (dcq, dckv), (dg_q_a, dg_kv_a) = _rowwise_vjp(
        "mla_lat_norm_bwd", _f_mla_a, [small["g_q_a"], small["g_kv_a"]], rows_a, [],
        [[(dcqn, MLA_Q_RANK, 0, N)], [(dckvn, MLA_KV_RANK, 0, N)]], [([0], BF16), ([1], BF16)], RW)
    dproj = jnp.concatenate([dgl, dv_r, dg_r, dq_r, dk_r, dcq, dckv, dkr.astype(BF16)], axis=1)
    dw_in_p = _mm("dw_in", h, dproj, "tn", BF16)

    c = lambda a, b_: dw_in_p[:, a:b_]
    kr0 = CQ0 + MLA_Q_RANK + MLA_KV_RANK
    dw_in = jnp.concatenate([c(CQ0, CQ0 + MLA_Q_RANK), c(CQ0 + MLA_Q_RANK, kr0), c(kr0 + MLA_NOPE, kr0 + MLA_QK), c(QR0, KR0),
                             c(KR0, CQ0), c(2048, 3072), c(3072, 4096), c(0, 2048)], axis=1)
    dw_kn = dw_kn_p.reshape(MLA_KV_RANK, HEADS, LANES)[:, :, :MLA_NOPE]
    dw_kv = jnp.concatenate([dw_kn, dw_v.reshape(MLA_KV_RANK, HEADS, MLA_V)], axis=2).reshape(MLA_KV_RANK, HEADS * (MLA_NOPE + MLA_V))
    grads = {"w_in": dw_in, "w_q_b": _unpad_heads(dw_qb_p, MLA_QK), "w_kv_b": dw_kv}
    dx, dg_mix, first_got = _mm("d_h", dproj, w_in_p, "nt", None,
                                epilogue=(_epi_rms_bwd(1), [x, dx2], [small["g_mix"]], [F32], [(1, D_MODEL)]),
                                scatter=[_split_for_reducers(n, grads[n], BF16) for n in FIRST])
    sgrads = {"g_mix": dg_mix, "g_q_a": dg_q_a, "g_kv_a": dg_kv_a, "g_qn": dg_qn_p[:, :MLA_QK], "g_kn": dg_kn_p[:, :MLA_QK],
              "ret_decay_fwd": ddec_f[:, 0].reshape(1, HEADS), "ret_decay_bwd": ddec_b[:, 0].reshape(1, HEADS), "g_ffn": dg_ffn}
    return loss_row, dx, first_got + late_got, sgrads


def _coords():
    return lax.axis_index("x"), lax.axis_index("y"), lax.axis_index("c")


def _other_chips(x, y):
    return [(1 - x, y), (x, 1 - y), (1 - x, 1 - y)]


ANY = pl.BlockSpec(memory_space=pl.ANY)


def _gather_copies(ins, outs, send_sems, recv_sems):
    x, y, c = _coords()
    mine = 2 * x + y
    sends, arrivals = [], []
    for w in range(len(ins)):
        for j, (cx, cy) in enumerate(_other_chips(x, y)):
            sems = dict(send_sem=send_sems.at[3 * w + j], recv_sem=recv_sems.at[3 * w + j],
                        device_id=(cx, cy, c), device_id_type=MESH)
            sends.append(pltpu.make_async_remote_copy(src_ref=ins[w], dst_ref=outs[w].at[mine], **sems))
            arrivals.append(functools.partial(pltpu.make_async_remote_copy, src_ref=ins[w],
                                              dst_ref=outs[w].at[2 * cx + cy], **sems))
    return sends, arrivals


def _gather_start(copies):
    for cp in list(copies[0]) + list(copies[2] if len(copies) > 2 else []):
        cp.start()


def _gather_wait(copies):
    sends, arrivals = copies[:2]
    for make in arrivals:
        make().wait_recv()
    for cp in sends:
        cp.wait_send()
    for cp in (copies[2] if len(copies) > 2 else []):
        cp.wait()


def _fill_slot(buf, piece, slot):
    idx = lax.broadcasted_iota(jnp.int32, (buf.shape[0],) + (1,) * piece.ndim, 0)
    return jnp.where(idx == slot, piece[None], buf)


def _rope_tables_and_first_gather(pos, consts_mla, consts_ret, shards):
    S = pos.shape[0]
    tm = min(HEAD_ROW_TILE, S)
    nt = S // tm
    n = len(shards)

    def body(pos_ref, *refs):
        consts, ins = (refs[:4], refs[4:8]), refs[8:8 + n]
        tabs, outs = refs[8 + n:14 + n], refs[14 + n:14 + 2 * n]
        send_sems, recv_sems = refs[14 + 2 * n:]
        i = pl.program_id(0)
        x, y, c = _coords()
        chips = _other_chips(x, y)
        mine = 2 * x + y

        def half(ref, slot, core):
            rows = ref.shape[1] // 2
            return ref.at[slot, pl.ds(pl.multiple_of(core * rows, 8), rows)]

        def copy(w, k, slot, core, to, src=None):
            return pltpu.make_async_remote_copy(
                src_ref=half(outs[w], slot, core) if src is None else src, dst_ref=half(outs[w], slot, core),
                send_sem=send_sems.at[6 * w + k], recv_sem=recv_sems.at[6 * w + k], device_id=to, device_id_type=MESH)

        def first(w, j):
            rows = ins[w].shape[0] // 2
            return copy(w, j, mine, c, (*chips[j], c), src=ins[w].at[pl.ds(pl.multiple_of(c * rows, 8), rows)])

        @pl.when(i == 0)
        def _():
            for w in range(n):
                for j in range(3):
                    first(w, j).start()

        for k in range(2):
            vals = _f_rope_table([r[...] for r in consts[k]], None, [pos_ref[...]])
            for t_ref, v in zip(tabs[3 * k:3 * k + 3], vals):
                t_ref[...] = v

        @pl.when(i == nt - 1)
        def _():
            passed = []
            for w in range(n):
                for j, (cx, cy) in enumerate(chips):
                    copy(w, j, 2 * cx + cy, c, (x, y, c)).wait_recv()
                    cp = copy(w, 3 + j, 2 * cx + cy, c, (x, y, 1 - c))
                    cp.start()
                    passed.append(cp)
            for w in range(n):
                for j, (cx, cy) in enumerate(chips):
                    copy(w, 3 + j, 2 * cx + cy, 1 - c, (x, y, c)).wait_recv()
            for w in range(n):
                for j in range(3):
                    first(w, j).wait_send()
            for cp in passed:
                cp.wait_send()

    const = lambda p: pl.BlockSpec(p.shape, lambda i: (0, 0))
    tab = pl.BlockSpec((tm, LANES), lambda i: (i, 0))
    res = pl.pallas_call(
        body, name="rope_tables_first_gather", grid=(nt,),
        in_specs=[pl.BlockSpec((tm, 1), lambda i: (i, 0))] + [const(p) for p in list(consts_mla) + list(consts_ret)] + [ANY] * n,
        out_specs=[tab] * 6 + [ANY] * n,
        out_shape=[jax.ShapeDtypeStruct((S, LANES), F32)] * 6 + [jax.ShapeDtypeStruct((4,) + s.shape, s.dtype) for s in shards],
        scratch_shapes=[pltpu.SemaphoreType.DMA((6 * n,)), pltpu.SemaphoreType.DMA((6 * n,))],
        compiler_params=_params(),
    )(pos, *consts_mla, *consts_ret, *shards)
    return list(res[:3]), list(res[3:6]), list(res[6:])


def _scatter_copies(ins, outs, send_sems, recv_sems):
    x, y, c = _coords()
    me = 4 * x + 2 * y + c
    n = len(ins)
    sends, arrivals = [], []
    local = [pltpu.make_async_copy(ins[w].at[2 * x + y, c], outs[w].at[me], send_sems.at[7 * n + w]) for w in range(n)]
    for w in range(n):
        for k in range(1, 8):
            px, py, pc = x ^ (k >> 2), y ^ ((k >> 1) & 1), c ^ (k & 1)
            sems = dict(send_sem=send_sems.at[7 * w + k - 1], recv_sem=recv_sems.at[7 * w + k - 1],
                        device_id=(px, py, pc), device_id_type=MESH)
            sends.append(pltpu.make_async_remote_copy(src_ref=ins[w].at[2 * px + py, pc], dst_ref=outs[w].at[me], **sems))
            arrivals.append(functools.partial(
                pltpu.make_async_remote_copy, src_ref=ins[w].at[2 * px + py, pc],
                dst_ref=outs[w].at[4 * px + 2 * py + pc], **sems))
    return sends, arrivals, local


_scatter_start, _scatter_wait = _gather_start, _gather_wait


def _grad_sum8(name, got):
    _, R, W = got.shape
    tr = _pick(R, 256, 16)

    def body(g_ref, o_ref):
        total = g_ref[0].astype(F32)
        for d in range(1, 8):
            total = total + g_ref[d].astype(F32)
        o_ref[...] = total

    return pl.pallas_call(
        body, name=name, grid=(R // tr,), in_specs=[pl.BlockSpec((8, tr, W), lambda i: (0, i, 0))],
        out_specs=pl.BlockSpec((tr, W), lambda i: (i, 0)), out_shape=jax.ShapeDtypeStruct((R, W), F32),
        compiler_params=_params(),
    )(got)


def _half_exchange(halves):
    n = len(halves)

    def body(*refs):
        ins, outs, send_sems, recv_sems = refs[:n], refs[n:2 * n], refs[2 * n], refs[2 * n + 1]
        x, y, c = _coords()
        sends = []
        for w in range(n):
            cp = pltpu.make_async_remote_copy(
                src_ref=ins[w], dst_ref=outs[w], send_sem=send_sems.at[w], recv_sem=recv_sems.at[w],
                device_id=(x, y, 1 - c), device_id_type=MESH)
            cp.start()
            sends.append(cp)
        for cp in sends:
            cp.wait()

    got = pl.pallas_call(
        body, name="grad_half_exchange", in_specs=[ANY] * n, out_specs=[ANY] * n,
        out_shape=[jax.ShapeDtypeStruct(h.shape, F32) for h in halves],
        scratch_shapes=[pltpu.SemaphoreType.DMA((n,)), pltpu.SemaphoreType.DMA((n,))],
    )(*halves)
    c = lax.axis_index("c")
    return [jnp.where(c == 0, jnp.stack([mine, theirs]), jnp.stack([theirs, mine])) for mine, theirs in zip(halves, got)]


def _adamw_math(w, g, m, v):
    m2 = ADAM_B1 * m + (1.0 - ADAM_B1) * g
    v2 = ADAM_B2 * v + (1.0 - ADAM_B2) * (g * g)
    m_hat = m2 / (1.0 - ADAM_B1 ** ADAM_STEP)
    v_hat = v2 / (1.0 - ADAM_B2 ** ADAM_STEP)
    return -ADAM_LR * (m_hat / (jnp.sqrt(v_hat) + ADAM_EPS) + ADAM_WD * w), m2, v2


def _small_allreduce_adamw(pack_g, pack_w, pack_m, pack_v):
    def body(g_ref, w_ref, m_ref, v_ref, sum_ref, d_ref, m_out, v_out, land, send_sems, recv_sems):
        x, y, c = _coords()
        me = 4 * x + 2 * y + c
        land[me] = g_ref[...]
        sends = []
        for k in range(1, 8):
            peer = (x ^ (k >> 2), y ^ ((k >> 1) & 1), c ^ (k & 1))
            cp = pltpu.make_async_remote_copy(
                src_ref=g_ref, dst_ref=land.at[me], send_sem=send_sems.at[k - 1], recv_sem=recv_sems.at[k - 1],
                device_id=peer, device_id_type=MESH)
            cp.start()
            sends.append((cp, peer))
        for k, (cp, peer) in enumerate(sends):
            pltpu.make_async_remote_copy(
                src_ref=g_ref, dst_ref=land.at[4 * peer[0] + 2 * peer[1] + peer[2]], send_sem=send_sems.at[k],
                recv_sem=recv_sems.at[k], device_id=peer, device_id_type=MESH).wait_recv()
        for cp, _ in sends:
            cp.wait_send()
        total = land[0]
        for d in range(1, 8):
            total = total + land[d]
        sum_ref[...] = total
        d_ref[...], m_out[...], v_out[...] = _adamw_math(w_ref[...], total, m_ref[...], v_ref[...])

    vm = pl.BlockSpec(memory_space=pltpu.VMEM)
    shp = jax.ShapeDtypeStruct(pack_g.shape, F32)
    return pl.pallas_call(
        body, name="small_allreduce_adamw", in_specs=[vm] * 4, out_specs=[vm] * 4, out_shape=[shp] * 4,
        scratch_shapes=[pltpu.VMEM((8,) + pack_g.shape, F32), pltpu.SemaphoreType.DMA((7,)), pltpu.SemaphoreType.DMA((7,))],
    )(pack_g, pack_w, pack_m, pack_v)


def _adamw(name, w, g, m, v):
    R, C = w.shape
    tr = _pick(R, 256, 8)

    def body(w_ref, g_ref, m_ref, v_ref, d_out, m_out, v_out):
        d_out[...], m_out[...], v_out[...] = _adamw_math(w_ref[...], g_ref[...], m_ref[...], v_ref[...])

    spec = pl.BlockSpec((tr, C), lambda i: (i, 0))
    return pl.pallas_call(
        body, name=name, grid=(R // tr,), in_specs=[spec] * 4, out_specs=[spec] * 3,
        out_shape=[jax.ShapeDtypeStruct((R, C), F32)] * 3, compiler_params=_params(),
    )(w, g, m, v)


def _pack_small(vals, last):
    flat = jnp.concatenate([v.reshape(-1) for v in vals] + [last.reshape(-1)])
    return jnp.pad(flat, (0, SMALL_ROWS * LANES - flat.shape[0])).reshape(SMALL_ROWS, LANES)


def kernel(x, positions, g_mix, w_in, g_q_a, w_q_b, g_kv_a, w_kv_b, g_qn, g_kn, w_mla_out, ret_decay_fwd, ret_decay_bwd, w_ret_out, w_out, g_ffn, w_gate_up, w_down, loss_target, m_g_mix, m_w_in, m_g_q_a, m_w_q_b, m_g_kv_a, m_w_kv_b, m_g_qn, m_g_kn, m_w_mla_out, m_ret_decay_fwd, m_ret_decay_bwd, m_w_ret_out, m_w_out, m_g_ffn, m_w_gate_up, m_w_down, v_g_mix, v_w_in, v_g_q_a, v_w_q_b, v_g_kv_a, v_w_kv_b, v_g_qn, v_g_kn, v_w_mla_out, v_ret_decay_fwd, v_ret_decay_bwd, v_w_ret_out, v_w_out, v_g_ffn, v_w_gate_up, v_w_down):
    given = dict(locals())
    S = x.shape[1]
    xs, tgt = x.reshape(S, D_MODEL), loss_target.reshape(S, D_MODEL)
    pos = positions.reshape(S, 1).astype(F32)

    first_shards = [given[n].astype(BF16) for n in FIRST]
    my_chip = 2 * lax.axis_index("x") + lax.axis_index("y")
    tab_m, tab_r, gathered = _rope_tables_and_first_gather(
        pos, _rope_consts(MLA_NOPE, MLA_ROPE // 2), _rope_consts(0, RET_QK // 2, RET_QK), first_shards)
    wts = {n: _assemble(n, _fill_slot(g, s, my_chip)) for n, g, s in zip(FIRST, gathered, first_shards)}
    late_shards = {n: given[n].astype(BF16) for n in LATE}
    small = {n: given[n].reshape(1, -1) for n in SMALL}

    loss_row, dx, pieces, sgrads = _local_step(xs, tab_m, tab_r, tgt, wts, late_shards, small)

    halves = [_grad_sum8("grad_sum_" + n, got) for n, got in zip(FIRST + LATE, pieces)]
    reduced = _half_exchange(halves)

    out = {}
    for n, r in zip(FIRST + LATE, reduced):
        g = r.reshape(given[n].shape)
        out["grad_" + n] = g
        out["delta_" + n], out["new_m_" + n], out["new_v_" + n] = _adamw("adamw_" + n, given[n], g, given["m_" + n], given["v_" + n])

    one = jnp.ones((1,), F32)
    pk = _small_allreduce_adamw(
        _pack_small([sgrads[n] for n in SMALL], loss_row[0, :1]),
        _pack_small([given[n] for n in SMALL], 0 * one),
        _pack_small([given["m_" + n] for n in SMALL], 0 * one),
        _pack_small([given["v_" + n] for n in SMALL], one))
    off = 0
    for n in SMALL:
        sz = given[n].shape[0]
        for pre, arr in zip(["grad_", "delta_", "new_m_", "new_v_"], pk):
            out[pre + n] = arr.reshape(-1)[off:off + sz]
        off += sz
    loss = pk[0].reshape(-1)[off]

    return (loss, dx.reshape(x.shape), *[out["grad_" + n] for n in WEIGHTS], *[out["delta_" + n] for n in WEIGHTS],
            *[out["new_m_" + n] for n in WEIGHTS], *[out["new_v_" + n] for n in WEIGHTS])
```

```python
import functools
import math

import numpy as np
import jax
import jax.numpy as jnp
from jax import lax
from jax.experimental import pallas as pl
from jax.experimental.pallas import tpu as pltpu

F32 = jnp.float32
BF16 = jnp.bfloat16
MESH = pl.DeviceIdType.MESH

D_MODEL = 1024
HEADS = 8
LANES = 128
MLA_Q_RANK, MLA_KV_RANK = 256, 128
MLA_NOPE, MLA_ROPE, MLA_V = 64, 32, 64
MLA_QK = MLA_NOPE + MLA_ROPE
LN2 = math.log(2.0)
MLA_Q_SCALE = MLA_QK ** -0.5 / LN2
RET_QK, RET_V, RET_CHUNK = 64, 128, 128
RET_QK_DTYPE = BF16
RET_CHUNKS_PER_STEP = 2
FFN_HIDDEN = 2816
ROPE_THETA = 10000.0
EPS = 1e-6
IN_SPLITS = [256, 128, 32, 512, 512, 1024, 1024, 2048]
IN_OFFS = [0] + list(np.cumsum(IN_SPLITS))
ADAM_LR, ADAM_B1, ADAM_B2, ADAM_EPS, ADAM_WD, ADAM_STEP = 0.001, 0.9, 0.999, 1e-08, 0.01, 10

VMEM_LIMIT = 56 * 1024 * 1024
ROW_TILE = 256
HEAD_ROW_TILE = 2048
MM_TM, MM_TN, MM_TK, MM_KFULL = 1408, 2048, 2048, 2816
ATT_TQ = 256
ATT_BQ, ATT_BK = 1024, 1024
ATT_HEADS_PER_STEP = 8
ATT_BWD_HEADS_PER_STEP = 4

SHARDED = ["w_in", "w_q_b", "w_kv_b", "w_mla_out", "w_ret_out", "w_out", "w_gate_up", "w_down"]
COL_SHARDED = {"w_in", "w_q_b", "w_kv_b", "w_mla_out", "w_gate_up"}
FIRST = ["w_in", "w_q_b", "w_kv_b"]
LATE = ["w_mla_out", "w_ret_out", "w_out", "w_gate_up", "w_down"]
SMALL = ["g_mix", "g_q_a", "g_kv_a", "g_qn", "g_kn", "ret_decay_fwd", "ret_decay_bwd", "g_ffn"]
WEIGHTS = ["g_mix", "w_in", "g_q_a", "w_q_b", "g_kv_a", "w_kv_b", "g_qn", "g_kn", "w_mla_out",
           "ret_decay_fwd", "ret_decay_bwd", "w_ret_out", "w_out", "g_ffn", "w_gate_up", "w_down"]
SMALL_ROWS = 24


def _params(**kw):
    return pltpu.CompilerParams(vmem_limit_bytes=VMEM_LIMIT, **kw)


def _pick(dim, target, unit=128):
    if dim <= target:
        return dim
    best = None
    for d in range(unit, target + 1, unit):
        if dim % d == 0:
            best = d
    assert best is not None, (dim, target)
    return best


_DOT = {"nn": (((1,), (0,)), ((), ())), "nt": (((1,), (1,)), ((), ())), "tn": (((0,), (0,)), ((), ()))}


def _dot(a, b, mode="nn"):
    return lax.dot_general(a, b, _DOT[mode], preferred_element_type=F32)


def _rms_rows(x, g):
    x = x.astype(F32)
    return x * lax.rsqrt(jnp.mean(x * x, axis=-1, keepdims=True) + EPS) * g


def _epi_loss(acc, extras, params):
    e = acc + extras[0] - extras[1]
    dy = e * (1.0 / D_MODEL)
    loss = 0.5 * jnp.sum(jnp.mean(e * e, axis=-1, keepdims=True), axis=0, keepdims=True)
    return [dy, dy], [jnp.broadcast_to(loss, (1, LANES))]


def _epi_rms_bwd(n_out):
    def fn(acc, extras, params):
        _, vjp = jax.vjp(_rms_rows, extras[0], params[0])
        dx, dg = vjp(acc)
        return [dx + extras[1]] * n_out, [dg]
    return fn


def _mm_epilogue(name, a, b, mode, epilogue, scatter=None):
    epi_fn, extras, eparams, e_outs, e_sums = epilogue
    if mode == "nn":
        (M, K), (K2, N) = a.shape, b.shape
    else:
        (M, K), (N, K2) = a.shape, b.shape
    assert K == K2 and mode in ("nn", "nt"), (name, a.shape, b.shape)
    tm = _pick(M, MM_TM // 2)
    tk = K if K <= MM_KFULL else _pick(K, MM_TK)
    ni, nk = M // tm, K // tk
    scatter = list(scatter or [])
    n_sc, n_ex, n_ep, n_out = len(scatter), len(extras), len(eparams), len(e_outs) + len(e_sums)

    def body(a_ref, b_ref, *refs):
        ex_refs, ep_refs = refs[:n_ex], refs[n_ex:n_ex + n_ep]
        sc_in = refs[n_ex + n_ep:n_ex + n_ep + n_sc]
        out_refs = refs[n_ex + n_ep + n_sc:n_ex + n_ep + n_sc + n_out]
        sc_out = refs[n_ex + n_ep + n_sc + n_out:n_ex + n_ep + 2 * n_sc + n_out]
        scratch = refs[n_ex + n_ep + 2 * n_sc + n_out:]
        acc, done = scratch[0], scratch[1]
        i, k = pl.program_id(0), pl.program_id(1)

        if n_sc:
            @pl.when(jnp.logical_and(i == 0, k == 0))
            def _():
                _scatter_start(_scatter_copies(sc_in, sc_out, scratch[-2], scratch[-1]))

        def product():
            return _dot(a_ref[...].astype(BF16), b_ref[...].astype(BF16), mode)

        @pl.when(jnp.logical_and(i == 0, k == 0))
        def _():
            done[...] = jnp.zeros_like(done)

        @pl.when(k == 0)
        def _():
            vals, sums = epi_fn(done[...], [r[...] for r in ex_refs], [p[...] for p in ep_refs])
            for o_ref, v in zip(out_refs, vals):
                o_ref[...] = v.astype(o_ref.dtype)
            for s_ref, v in zip(out_refs[len(vals):], sums):
                s_ref[...] = jnp.where(i <= 1, 0.0, s_ref[...]) + jnp.where(i >= 1, v, 0.0)
            acc[...] = product()

        @pl.when(k > 0)
        def _():
            acc[...] += product()

        @pl.when(k == nk - 1)
        def _():
            done[...] = acc[...]

        if n_sc:
            @pl.when(jnp.logical_and(i == ni, k == nk - 1))
            def _():
                _scatter_wait(_scatter_copies(sc_in, sc_out, scratch[-2], scratch[-1]))

    cur = lambda i: jnp.minimum(i, ni - 1)
    prev = lambda i: jnp.maximum(i - 1, 0)
    a_spec = pl.BlockSpec((tm, tk), lambda i, k: (cur(i), k))
    b_spec = pl.BlockSpec((N, tk), lambda i, k: (0, k)) if mode == "nt" else pl.BlockSpec((tk, N), lambda i, k: (k, 0))
    row = pl.BlockSpec((tm, N), lambda i, k: (prev(i), 0))
    const = lambda p: pl.BlockSpec(p.shape, lambda i, k: (0,) * p.ndim)
    out_specs = [row] * len(e_outs) + [pl.BlockSpec(s, lambda i, k: (0, 0)) for s in e_sums] + [ANY] * n_sc
    out_shape = ([jax.ShapeDtypeStruct((M, N), dt) for dt in e_outs] + [jax.ShapeDtypeStruct(s, F32) for s in e_sums]
                 + [jax.ShapeDtypeStruct((8,) + g.shape[2:], g.dtype) for g in scatter])
    scratch_shapes = [pltpu.VMEM((tm, N), F32)] * 2
    if n_sc:
        scratch_shapes += [pltpu.SemaphoreType.DMA((8 * n_sc,)), pltpu.SemaphoreType.DMA((7 * n_sc,))]
    res_ = pl.pallas_call(
        body, name=name, grid=(ni + 1, nk),
        in_specs=[a_spec, b_spec] + [row] * n_ex + [const(p) for p in eparams] + [ANY] * n_sc,
        out_specs=out_specs, out_shape=out_shape, scratch_shapes=scratch_shapes, compiler_params=_params(),
    )(a, b, *extras, *eparams, *scatter)
    if n_sc:
        return list(res_[:n_out]) + [list(res_[n_out:])]
    return res_


def _mm(name, a, b, mode, out_dtype, res=None, a_gain=None, epilogue=None, shard_out=False, scatter=None):
    if epilogue is not None:
        assert res is None and a_gain is None and not shard_out, name
        return _mm_epilogue(name, a, b, mode, epilogue, scatter)
    if mode == "nn":
        (M, K), (K2, N) = a.shape, b.shape
    elif mode == "nt":
        (M, K), (N, K2) = a.shape, b.shape
    else:
        (K, M), (K2, N) = a.shape, b.shape
    assert K == K2, (name, a.shape, b.shape)
    tm, tn = _pick(M, MM_TM), _pick(N, MM_TN)
    tk = K if K <= MM_KFULL else _pick(K, MM_TK)
    if shard_out:
        tm, tn = M // 2, N // 4
    if epilogue is not None:
        tm = _pick(M, MM_TM // 2)
    nk = K // tk
    cache_a = a_gain is not None
    if a_gain is not None:
        assert mode == "nn" and tk == K and epilogue is None and not shard_out, name
    n_in = 2 + (res is not None) + (a_gain is not None)
    extras, eparams, e_outs, e_sums = ([], [], [], [])
    if epilogue is not None:
        assert tn == N and res is None and not shard_out, name
        epi_fn, extras, eparams, e_outs, e_sums = epilogue
    n_out = len(e_outs) + len(e_sums) if epilogue is not None else 1 + cache_a
    scatter = list(scatter or [])
    n_sc = len(scatter)
    assert not n_sc or epilogue is not None, name
    ni, nj = M // tm, N // tn

    def body(*refs):
        a_ref, b_ref = refs[0], refs[1]
        base = n_in + len(extras) + len(eparams)
        ex_refs = refs[n_in:n_in + len(extras)]
        ep_refs = refs[n_in + len(extras):base]
        sc_in, out_refs = refs[base:base + n_sc], refs[base + n_sc:base + n_sc + n_out]
        sc_out = refs[base + n_sc + n_out:base + 2 * n_sc + n_out]
        scratch = refs[base + 2 * n_sc + n_out:]
        acc = scratch[0]
        i, j, k = pl.program_id(0), pl.program_id(1), pl.program_id(2)

        if n_sc:
            @pl.when(jnp.logical_and(i == 0, jnp.logical_and(j == 0, k == 0)))
            def _():
                _scatter_start(_scatter_copies(sc_in, sc_out, scratch[-2], scratch[-1]))

        @pl.when(k == 0)
        def _():
            acc[...] = jnp.zeros_like(acc)

        if cache_a:
            @pl.when(j == 0)
            def _():
                out_refs[1][...] = _rms_rows(a_ref[...], refs[n_in - 1][...]).astype(BF16)
            av = out_refs[1][...]
        else:
            av = a_ref[...].astype(BF16)
        acc[...] += _dot(av, b_ref[...].astype(BF16), mode)

        @pl.when(k == nk - 1)
        def _():
            if epilogue is None:
                r = acc[...]
                if res is not None:
                    r = r + refs[2][...].astype(F32)
                out_refs[0][...] = r.astype(out_refs[0].dtype).reshape(out_refs[0].shape)
            else:
                vals, sums = epi_fn(acc[...], [r[...] for r in ex_refs], [p[...] for p in ep_refs])
                for o_ref, v in zip(out_refs, vals):
                    o_ref[...] = v.astype(o_ref.dtype)
                for s_ref, v in zip(out_refs[len(vals):], sums):
                    @pl.when(i == 0)
                    def _(s_ref=s_ref):
                        s_ref[...] = jnp.zeros_like(s_ref)
                    s_ref[...] += v

        if n_sc:
            @pl.when(jnp.logical_and(i == ni - 1, jnp.logical_and(j == nj - 1, k == nk - 1)))
            def _():
                _scatter_wait(_scatter_copies(sc_in, sc_out, scratch[-2], scratch[-1]))

    a_spec = pl.BlockSpec((tk, tm), lambda i, j, k: (k, i)) if mode == "tn" else pl.BlockSpec((tm, tk), lambda i, j, k: (i, k))
    b_spec = pl.BlockSpec((tn, tk), lambda i, j, k: (j, k)) if mode == "nt" else pl.BlockSpec((tk, tn), lambda i, j, k: (k, j))
    o_spec = pl.BlockSpec((tm, tn), lambda i, j, k: (i, j))
    const = lambda p: pl.BlockSpec(p.shape, lambda i, j, k: (0,) * p.ndim)
    ins, specs = [a, b], [a_spec, b_spec]
    if res is not None:
        ins.append(res)
        specs.append(o_spec)
    if a_gain is not None:
        ins.append(a_gain)
        specs.append(const(a_gain))
    ins += list(extras) + list(eparams)
    specs += [o_spec] * len(extras) + [const(p) for p in eparams]
    if epilogue is not None:
        out_specs = [o_spec] * len(e_outs) + [pl.BlockSpec(s, lambda i, j, k: (0, 0)) for s in e_sums]
        out_shape = [jax.ShapeDtypeStruct((M, N), dt) for dt in e_outs] + [jax.ShapeDtypeStruct(s, F32) for s in e_sums]
    elif shard_out:
        out_specs = pl.BlockSpec((1, 1, tm, tn), lambda i, j, k: (j, i, 0, 0))
        out_shape = jax.ShapeDtypeStruct((4, 2, tm, tn), out_dtype)
    elif cache_a:
        out_specs = [o_spec, pl.BlockSpec((tm, K), lambda i, j, k: (i, 0))]
        out_shape = [jax.ShapeDtypeStruct((M, N), out_dtype), jax.ShapeDtypeStruct((M, K), BF16)]
    else:
        out_specs, out_shape = o_spec, jax.ShapeDtypeStruct((M, N), out_dtype)
    scratch_shapes = [pltpu.VMEM((tm, tn), F32)]
    if n_sc:
        ins += scatter
        specs += [ANY] * n_sc
        out_specs = list(out_specs) + [ANY] * n_sc
        out_shape = list(out_shape) + [jax.ShapeDtypeStruct((8,) + g.shape[2:], g.dtype) for g in scatter]
        scratch_shapes += [pltpu.SemaphoreType.DMA((8 * n_sc,)), pltpu.SemaphoreType.DMA((7 * n_sc,))]
    res_ = pl.pallas_call(
        body, name=name, grid=(ni, nj, nk), in_specs=specs, out_specs=out_specs, out_shape=out_shape,
        scratch_shapes=scratch_shapes, compiler_params=_params(),
    )(*ins)
    if n_sc:
        return list(res_[:n_out]) + [list(res_[n_out:])]
    return res_


def _piece_spec(tm, piece):
    _, w, c0, per_group = piece
    if per_group:
        return pl.BlockSpec((tm, w), lambda i, g: (i, c0 + g))
    return pl.BlockSpec((tm, w), lambda i, g: (i, c0))


def _const_spec(p):
    return pl.BlockSpec(p.shape, lambda i, g: (0, 0))


def _rowwise(name, fn, params, rows, auxs, outs, tm, groups=1):
    S = rows[0][0].shape[0]
    tm = min(tm, S)
    n_p, n_r, n_a = len(params), len(rows), len(auxs)

    def body(*refs):
        p = [r[...] for r in refs[:n_p]]
        r_ = [r[...] for r in refs[n_p:n_p + n_r]]
        a_ = [r[...] for r in refs[n_p + n_r:n_p + n_r + n_a]]
        for o_ref, o in zip(refs[n_p + n_r + n_a:], fn(p, r_, a_)):
            o_ref[...] = o.astype(o_ref.dtype)

    out_specs, out_shape = [], []
    for w, dt, per_group in outs:
        out_specs.append(_piece_spec(tm, (None, w, 0, per_group)))
        out_shape.append(jax.ShapeDtypeStruct((S, w * (groups if per_group else 1)), dt))
    return pl.pallas_call(
        body, name=name, grid=(S // tm, groups),
        in_specs=[_const_spec(p) for p in params] + [_piece_spec(tm, q) for q in list(rows) + list(auxs)],
        out_specs=out_specs, out_shape=out_shape, compiler_params=_params(),
    )(*params, *[q[0] for q in list(rows) + list(auxs)])


def _rowwise_vjp(name, fn, params, rows, auxs, cots, d_outs, tm, groups=1, adds=None):
    S = rows[0][0].shape[0]
    tm = min(tm, S)
    n_p, n_r, n_a = len(params), len(rows), len(auxs)
    cot_flat = [q for c in cots for q in c]
    adds = adds or [None] * len(d_outs)
    add_flat = [q for q in adds if q is not None]
    n_c, n_add = len(cot_flat), len(add_flat)
    shared = [not all(rows[k][3] for k in idx) and groups > 1 for idx, _ in d_outs]

    def body(*refs):
        pos = 0
        p = [r[...] for r in refs[pos:pos + n_p]]; pos += n_p
        r_ = [r[...] for r in refs[pos:pos + n_r]]; pos += n_r
        a_ = [r[...] for r in refs[pos:pos + n_a]]; pos += n_a
        c_refs = refs[pos:pos + n_c]; pos += n_c
        add_refs = list(refs[pos:pos + n_add]); pos += n_add
        d_refs = refs[pos:pos + len(d_outs)]; pos += len(d_outs)
        dp_refs = refs[pos:]
        i, g = pl.program_id(0), pl.program_id(1)
        outs, vjp_fn = jax.vjp(lambda pp, rr: fn(pp, rr, a_), p, r_)
        cts, ci = [], 0
        for c, o in zip(cots, outs):
            t = c_refs[ci][...].astype(F32)
            for extra in c_refs[ci + 1:ci + len(c)]:
                t = t + extra[...].astype(F32)
            ci += len(c)
            cts.append(t.astype(o.dtype))
        dp, dr = vjp_fn(cts)
        for (idx, _), d_ref, add, sh in zip(d_outs, d_refs, adds, shared):
            val = dr[idx[0]].astype(F32) if len(idx) == 1 else jnp.concatenate([dr[k].astype(F32) for k in idx], axis=1)
            if add is not None:
                val = val + add_refs.pop(0)[...].astype(F32)
            if sh:
                @pl.when(g == 0)
                def _(d_ref=d_ref):
                    d_ref[...] = jnp.zeros_like(d_ref)
                d_ref[...] += val.astype(d_ref.dtype)
            else:
                d_ref[...] = val.astype(d_ref.dtype)
        first = jnp.logical_and(i == 0, g == 0)
        for dp_ref, d in zip(dp_refs, dp):
            @pl.when(first)
            def _(dp_ref=dp_ref):
                dp_ref[...] = jnp.zeros_like(dp_ref)
            dp_ref[...] += d.astype(F32)

    out_specs, out_shape = [], []
    for (idx, dt), sh in zip(d_outs, shared):
        w = sum(rows[k][1] for k in idx)
        per_group = (not sh) and groups > 1
        out_specs.append(_piece_spec(tm, (None, w, 0, per_group)))
        out_shape.append(jax.ShapeDtypeStruct((S, w * (groups if per_group else 1)), dt))
    for p in params:
        out_specs.append(_const_spec(p))
        out_shape.append(jax.ShapeDtypeStruct(p.shape, F32))
    pieces = list(rows) + list(auxs) + cot_flat + add_flat
    res = pl.pallas_call(
        body, name=name, grid=(S // tm, groups),
        in_specs=[_const_spec(p) for p in params] + [_piece_spec(tm, q) for q in pieces],
        out_specs=out_specs, out_shape=out_shape, compiler_params=_params(),
    )(*params, *[q[0] for q in pieces])
    return list(res[:len(d_outs)]), list(res[len(d_outs):])


def _lane_roll(x, shift):
    @jax.custom_vjp
    def roll(v):
        return pltpu.roll(v, shift, 1)

    roll.defvjp(lambda v: (roll(v), None), lambda _, ct: (pltpu.roll(ct, LANES - shift, 1),))
    return roll(x)


@jax.custom_vjp
def _sigmoid(x):
    return 1.0 / (1.0 + jnp.exp(-x))


def _sigmoid_fwd(x):
    s = _sigmoid(x)
    return s, s


_sigmoid.defvjp(_sigmoid_fwd, lambda s, ct: (ct * s * (1.0 - s),))


def _rope(x, cos, sin_lo, sin_hi, half):
    return x * cos + _lane_roll(x, LANES - half) * sin_lo + _lane_roll(x, half) * sin_hi


def _f_rope_table(p, r, a):
    inv, first, second, fixed = p
    ang = a[0] * inv
    cs, sn = jnp.cos(ang), jnp.sin(ang)
    return [cs * (first + second) + fixed, -sn * first, sn * second]


def _f_rms(p, r, a):
    x = r[0].astype(F32)
    return [x * lax.rsqrt(jnp.mean(x * x, axis=-1, keepdims=True) + EPS) * p[0]]


def _f_mla_a(p, r, a):
    return _f_rms([p[0]], [r[0]], a) + _f_rms([p[1]], [r[1]], a)


def _f_mla_b(p, r, a):
    def norm_rope(v, g):
        ms = jnp.sum(v * v, axis=-1, keepdims=True) * (1.0 / MLA_QK)
        return _rope(v * lax.rsqrt(ms + EPS) * g, a[0], a[1], a[2], MLA_ROPE // 2)

    return [norm_rope(r[0].astype(F32), p[0]) * MLA_Q_SCALE, norm_rope(r[1].astype(F32) + r[2].astype(F32), p[1])]


def _f_ret_rope(p, r, a):
    q = _rope(r[0].astype(F32), a[0], a[1], a[2], RET_QK // 2)
    k = _rope(r[1].astype(F32), a[0], a[1], a[2], RET_QK // 2)
    return [q, k * (RET_QK ** -0.5)]


def _f_ret_post(p, r, a):
    ret = r[0].astype(F32) + r[1].astype(F32)
    g = r[2].astype(F32)
    normed = ret * lax.rsqrt(jnp.mean(ret * ret, axis=-1, keepdims=True) + EPS)
    return [g * _sigmoid(g) * normed]


def _f_merge(p, r, a):
    return [_sigmoid(r[0].astype(F32)) * r[2].astype(F32) + _sigmoid(r[1].astype(F32)) * r[3].astype(F32)]


def _f_swiglu(p, r, a):
    g = r[0].astype(F32)
    return [g * _sigmoid(g) * r[1].astype(F32)]


def _f_add(p, r, a):
    return [r[0].astype(F32) + r[1].astype(F32)]


def _flash_fwd(q, k, kv, shards):
    S = q.shape[0]
    tq = min(ATT_TQ, S)
    nq = S // tq
    n = len(shards)

    def body(q_ref, k_ref, v_ref, *rest):
        shard_refs, (o_ref, lse_ref), gathered = rest[:n], rest[n:n + 2], rest[n + 2:2 * n + 2]
        send_sems, recv_sems = rest[2 * n + 2:]
        h, qi = pl.program_id(0), pl.program_id(1)

        @pl.when(jnp.logical_and(h == 0, qi == 0))
        def _():
            _gather_start(_gather_copies(shard_refs, gathered, send_sems, recv_sems))

        for hh in range(hps):
            lanes = slice(hh * LANES, (hh + 1) * LANES)
            s = _dot(q_ref[:, lanes], k_ref[:, lanes], "nt")
            m = jnp.max(s, axis=-1, keepdims=True)
            p = jnp.exp2(s - m)
            l = jnp.sum(p, axis=-1, keepdims=True)
            pair = slice(hh // 2 * LANES, (hh // 2 + 1) * LANES)
            o_h = (_dot(p.astype(BF16), v_ref[:, pair]) / l * _ret_head_mask(hh)).astype(o_ref.dtype)
            if hh % 2 == 0:
                o_ref[:, pair] = o_h
            else:
                o_ref[:, pair] += o_h
            lse_ref[:, lanes] = jnp.broadcast_to(m + jnp.log2(l), (tq, LANES))

        @pl.when(jnp.logical_and(h == HEADS // hps - 1, qi == nq - 1))
        def _():
            _gather_wait(_gather_copies(shard_refs, gathered, send_sems, recv_sems))

    hps = ATT_HEADS_PER_STEP
    qs = pl.BlockSpec((tq, hps * LANES), lambda h, i: (i, h))
    vw = hps * MLA_V
    v0 = HEADS * LANES // vw
    res = pl.pallas_call(
        body, name="mla_fwd", grid=(HEADS // hps, nq),
        in_specs=[qs, pl.BlockSpec((S, hps * LANES), lambda h, i: (0, h), pipeline_mode=pl.Buffered(1)),
                  pl.BlockSpec((S, vw), lambda h, i: (0, v0 + h), pipeline_mode=pl.Buffered(1))]
        + [ANY] * n,
        out_specs=[pl.BlockSpec((tq, vw), lambda h, i: (i, h)), qs] + [ANY] * n,
        out_shape=[jax.ShapeDtypeStruct((S, HEADS * MLA_V), BF16), jax.ShapeDtypeStruct((S, HEADS * LANES), F32)]
        + [jax.ShapeDtypeStruct((4,) + s.shape, s.dtype) for s in shards],
        scratch_shapes=[pltpu.SemaphoreType.DMA((3 * n,)), pltpu.SemaphoreType.DMA((3 * n,))],
        compiler_params=_params(),
    )(q, k, kv, *shards)
    mine = 2 * lax.axis_index("x") + lax.axis_index("y")
    return res[0], res[1], [_fill_slot(g, s, mine) for g, s in zip(res[2:], shards)]


def _flash_bwd(q, k, kv, do, lse, o, gs):
    S = q.shape[0]
    tq, tk = min(ATT_BQ, S), min(ATT_BK, S)
    nq, nkt = S // tq, S // tk
    n = len(gs)

    def body(q_ref, k_ref, v_ref, do_ref, lse_ref, o_ref, *rest):
        g_refs, (dq_ref, dk_ref, dv_ref), got_refs = rest[:n], rest[n:n + 3], rest[n + 3:2 * n + 3]
        dk_sc, dv_sc, send_sems, recv_sems = rest[2 * n + 3:]
        h, ki, qi = pl.program_id(0), pl.program_id(1), pl.program_id(2)

        @pl.when(jnp.logical_and(h == 0, jnp.logical_and(ki == 0, qi == 0)))
        def _():
            _scatter_start(_scatter_copies(g_refs, got_refs, send_sems, recv_sems))

        @pl.when(jnp.logical_and(ki == 0, qi == 0))
        def _():
            dq_ref[...] = jnp.zeros_like(dq_ref)

        @pl.when(qi == 0)
        def _():
            dk_sc[...] = jnp.zeros_like(dk_sc)
            dv_sc[...] = jnp.zeros_like(dv_sc)

        rows = pl.ds(pl.multiple_of(qi * tq, tq), tq)
        for hh in range(hps):
            lanes = slice(hh * LANES, (hh + 1) * LANES)
            pair = slice(hh // 2 * LANES, (hh // 2 + 1) * LANES)
            qv, kv_ = q_ref[:, lanes], k_ref[:, lanes]
            do32 = do_ref[:, pair].astype(F32) * _ret_head_mask(hh)
            dov = do32.astype(BF16)
            p = jnp.exp2(_dot(qv, kv_, "nt") - lse_ref[:, lanes][:, :1])
            dp = _dot(dov, v_ref[:, pair], "nt")
            delta = jnp.sum(do32 * o_ref[:, pair].astype(F32), axis=-1, keepdims=True)
            ds = (p * (dp - delta) * LN2).astype(BF16)
            dv_sc[:, pair] += _dot(p.astype(BF16), dov, "tn")
            dk_sc[:, lanes] += _dot(ds, qv, "tn")
            dq_ref[rows, lanes] += _dot(ds, kv_)

        @pl.when(qi == nq - 1)
        def _():
            dk_ref[...] = dk_sc[...].astype(dk_ref.dtype)
            dv_ref[...] = dv_sc[...].astype(dv_ref.dtype)

        @pl.when(jnp.logical_and(h == HEADS // hps - 1, jnp.logical_and(ki == nkt - 1, qi == nq - 1)))
        def _():
            _scatter_wait(_scatter_copies(g_refs, got_refs, send_sems, recv_sems))

    hps = ATT_BWD_HEADS_PER_STEP
    qs = pl.BlockSpec((tq, hps * LANES), lambda h, j, i: (i, h))
    ks = pl.BlockSpec((tk, hps * LANES), lambda h, j, i: (j, h))
    vw = hps * MLA_V
    v0 = HEADS * LANES // vw
    qv_s = pl.BlockSpec((tq, vw), lambda h, j, i: (i, h))
    kv_s = pl.BlockSpec((tk, vw), lambda h, j, i: (j, h))
    res = pl.pallas_call(
        body, name="mla_bwd", grid=(HEADS // hps, nkt, nq),
        in_specs=[qs, ks, pl.BlockSpec((tk, vw), lambda h, j, i: (j, v0 + h)), qv_s, qs, qv_s] + [ANY] * n,
        out_specs=[pl.BlockSpec((S, hps * LANES), lambda h, j, i: (0, h), pipeline_mode=pl.Buffered(1)), ks, kv_s] + [ANY] * n,
        out_shape=[jax.ShapeDtypeStruct((S, HEADS * LANES), F32), jax.ShapeDtypeStruct((S, HEADS * LANES), BF16),
                   jax.ShapeDtypeStruct((S, HEADS * MLA_V), BF16)]
        + [jax.ShapeDtypeStruct((8,) + g.shape[2:], g.dtype) for g in gs],
        scratch_shapes=[pltpu.VMEM((tk, hps * LANES), F32), pltpu.VMEM((tk, vw), F32)]
        + [pltpu.SemaphoreType.DMA((8 * n,)), pltpu.SemaphoreType.DMA((7 * n,))],
        compiler_params=_params(),
    )(q, k, kv, do, lse, o, *gs)
    return res[0], res[1], res[2], list(res[3:])


def _ret_tables(decay_row, backward):
    C = RET_CHUNK
    lg = -jnp.exp(decay_row)
    t = lax.broadcasted_iota(jnp.int32, (C, C), 0).astype(F32)
    s = lax.broadcasted_iota(jnp.int32, (C, C), 1).astype(F32)
    ridx = lax.broadcasted_iota(jnp.int32, (C, LANES), 0).astype(F32)
    if backward:
        dist, mask, aw, bw = s - t, s > t, C - ridx, ridx
    else:
        dist, mask, aw, bw = t - s, t >= s, ridx + 1.0, C - 1.0 - ridx
    dist = jnp.maximum(dist, 0.0)
    din = jnp.where(mask, jnp.exp(lg[:, :1] * dist), 0.0)
    return dict(din=din, dist=dist, a=jnp.exp(lg * aw), b=jnp.exp(lg * bw), c=jnp.exp(lg * C), aw=aw, bw=bw)


def _ret_fill_tables(decs, din_sc, a_sc, b_sc):
    for d, dec in enumerate(decs):
        for h in range(HEADS):
            tb = _ret_tables(dec[h:h + 1, :], d == 1)
            din_sc[d, h], a_sc[d, h], b_sc[d, h] = tb["din"], tb["a"], tb["b"]


def _ret_head_mask(h):
    lane = lax.broadcasted_iota(jnp.int32, (1, LANES), 1)
    return jnp.where((lane >= RET_QK) == bool(h % 2), 1.0, 0.0).astype(F32)


def _ret_fwd(qr, kr, proj, v_block, dec_f, dec_b):
    S = qr.shape[0]
    C = RET_CHUNK
    n = S // C
    nc = min(RET_CHUNKS_PER_STEP, n)
    nb = n // nc
    W = HEADS * LANES

    def body(qf, kf, vf, qb, kb, vb, df, db, of, ob, sf_out, sb_out, st, din_sc, a_sc, b_sc):
        @pl.when(pl.program_id(0) == 0)
        def _():
            st[...] = jnp.zeros_like(st)
            _ret_fill_tables((df, db), din_sc, a_sc, b_sc)

        for d, (q_ref, k_ref, v_ref, dec, o_ref, s_out) in enumerate(
                [(qf, kf, vf, df, of, sf_out), (qb, kb, vb, db, ob, sb_out)]):
            for h in range(HEADS):
                lanes, pair = slice(h * LANES, (h + 1) * LANES), slice(h // 2 * LANES, (h // 2 + 1) * LANES)
                mine = _ret_head_mask(h)
                din, a, b = din_sc[d, h], a_sc[d, h], b_sc[d, h]
                c = jnp.exp(-jnp.exp(dec[h:h + 1, :]) * C)
                for ci in (range(nc) if d == 0 else reversed(range(nc))):
                    rows = slice(ci * C, (ci + 1) * C)
                    qf32, kf32 = q_ref[rows, pair].astype(F32) * mine, k_ref[rows, pair].astype(F32) * mine
                    v = v_ref[rows, lanes]
                    state = st[d, h]
                    s_out[ci, h] = state
                    inner = _dot((_dot(qf32.astype(BF16), kf32.astype(BF16), "nt") * din).astype(BF16), v)
                    cross = _dot((qf32 * a).astype(BF16), state.astype(BF16))
                    o_ref[rows, lanes] = inner + cross
                    st[d, h] = state * c + _dot((kf32 * b).astype(BF16), v, "tn")

    fw = lambda c0, w=W: pl.BlockSpec((nc * C, w), lambda j: (j, c0))
    bw = lambda c0, w=W: pl.BlockSpec((nc * C, w), lambda j: (nb - 1 - j, c0))
    dec_spec = pl.BlockSpec((HEADS, LANES), lambda j: (0, 0))
    st_shape = jax.ShapeDtypeStruct((n, HEADS, LANES, LANES), F32)
    QW = W // 2
    return pl.pallas_call(
        body, name="ret_fwd", grid=(nb,),
        in_specs=[fw(0, QW), fw(0, QW), fw(v_block), bw(0, QW), bw(0, QW), bw(v_block), dec_spec, dec_spec],
        out_specs=[fw(0), bw(0), pl.BlockSpec((nc, HEADS, LANES, LANES), lambda j: (j, 0, 0, 0)),
                   pl.BlockSpec((nc, HEADS, LANES, LANES), lambda j: (nb - 1 - j, 0, 0, 0))],
        out_shape=[jax.ShapeDtypeStruct((S, W), F32)] * 2 + [st_shape] * 2,
        scratch_shapes=[pltpu.VMEM((2, HEADS, LANES, LANES), F32), pltpu.VMEM((2, HEADS, C, C), F32),
                        pltpu.VMEM((2, HEADS, C, LANES), F32), pltpu.VMEM((2, HEADS, C, LANES), F32)],
        compiler_params=_params(),
    )(qr, kr, proj, qr, kr, proj, dec_f, dec_b)


def _ret_bwd(qr, kr, proj, v_block, dret, sf, sb, dec_f, dec_b):
    S = qr.shape[0]
    C = RET_CHUNK
    n = S // C
    nc = min(RET_CHUNKS_PER_STEP, n)
    nb = n // nc
    W = HEADS * LANES

    def body(qf, kf, vf, gf, sf_ref, qb, kb, vb, gb, sb_ref, df, db,
             dqf, dkf, dvf, dqb, dkb, dvb, ddf, ddb, ds_sc, din_sc, a_sc, b_sc):
        j = pl.program_id(0)

        @pl.when(j == 0)
        def _():
            ds_sc[...] = jnp.zeros_like(ds_sc)
            ddf[...] = jnp.zeros_like(ddf)
            ddb[...] = jnp.zeros_like(ddb)
            _ret_fill_tables((df, db), din_sc, a_sc, b_sc)

        for d, (q_ref, k_ref, v_ref, g_ref, s_ref, dec, dq_ref, dk_ref, dv_ref, dd_ref) in enumerate(
                [(qf, kf, vf, gf, sf_ref, df, dqf, dkf, dvf, ddf), (qb, kb, vb, gb, sb_ref, db, dqb, dkb, dvb, ddb)]):
            static = _ret_tables(dec[0:1, :], d == 1)
            dist, aw, bw_ = static["dist"], static["aw"], static["bw"]
            for h in range(HEADS):
                lanes, pair = slice(h * LANES, (h + 1) * LANES), slice(h // 2 * LANES, (h // 2 + 1) * LANES)
                mine = _ret_head_mask(h)
                din, a, b = din_sc[d, h], a_sc[d, h], b_sc[d, h]
                c = jnp.exp(-jnp.exp(dec[h:h + 1, :]) * C)
                dlg = jnp.zeros((1, 1), F32)
                for ci in (reversed(range(nc)) if d == 0 else range(nc)):
                    rows = slice(ci * C, (ci + 1) * C)
                    v, g = v_ref[rows, lanes], g_ref[rows, lanes]
                    qf32, kf32 = q_ref[rows, pair].astype(F32) * mine, k_ref[rows, pair].astype(F32) * mine
                    q, k = qf32.astype(BF16), kf32.astype(BF16)
                    state, dstate = s_ref[ci, h], ds_sc[d, h]
                    dstate_b = dstate.astype(BF16)
                    dp = _dot(g, v, "nt")
                    a_ = _dot(q, k, "nt")
                    da = (dp * din).astype(BF16)
                    g1 = _dot(g, state.astype(BF16), "nt")
                    g2 = _dot(v, dstate_b, "nt")
                    dq_h = (_dot(da, k) + g1 * a).astype(dq_ref.dtype)
                    dk_h = (_dot(da, q, "tn") + g2 * b).astype(dk_ref.dtype)
                    if h % 2 == 0:
                        dq_ref[rows, pair], dk_ref[rows, pair] = dq_h, dk_h
                    else:
                        dq_ref[rows, pair] += dq_h
                        dk_ref[rows, pair] += dk_h
                    dv_ref[rows, lanes] = (_dot((a_ * din).astype(BF16), g, "tn")
                                           + _dot((kf32 * b).astype(BF16), dstate_b)).astype(dv_ref.dtype)
                    dlg = dlg + (jnp.sum(dp * a_ * din * dist, keepdims=True)
                                 + jnp.sum(g1 * qf32 * a * aw, keepdims=True)
                                 + jnp.sum(g2 * kf32 * b * bw_, keepdims=True)
                                 + C * jnp.sum(c * dstate * state, keepdims=True))
                    ds_sc[d, h] = dstate * c + _dot((qf32 * a).astype(BF16), g, "tn")
                dd_ref[h:h + 1, :] += jnp.broadcast_to(dlg, (1, LANES))

        @pl.when(j == nb - 1)
        def _():
            ddf[...] = ddf[...] * -jnp.exp(df[...])
            ddb[...] = ddb[...] * -jnp.exp(db[...])

    fw = lambda c0, w=W: pl.BlockSpec((nc * C, w), lambda j: (nb - 1 - j, c0))
    bw = lambda c0, w=W: pl.BlockSpec((nc * C, w), lambda j: (j, c0))
    dec_spec = pl.BlockSpec((HEADS, LANES), lambda j: (0, 0))
    QW = W // 2
    act, act_qk = jax.ShapeDtypeStruct((S, W), BF16), jax.ShapeDtypeStruct((S, QW), BF16)
    return pl.pallas_call(
        body, name="ret_bwd", grid=(nb,),
        in_specs=[fw(0, QW), fw(0, QW), fw(v_block), fw(0),
                  pl.BlockSpec((nc, HEADS, LANES, LANES), lambda j: (nb - 1 - j, 0, 0, 0)),
                  bw(0, QW), bw(0, QW), bw(v_block), bw(0), pl.BlockSpec((nc, HEADS, LANES, LANES), lambda j: (j, 0, 0, 0)),
                  dec_spec, dec_spec],
        out_specs=[fw(0, QW), fw(0, QW), fw(0), bw(0, QW), bw(0, QW), bw(0)] + [dec_spec] * 2,
        out_shape=[act_qk, act_qk, act, act_qk, act_qk, act] + [jax.ShapeDtypeStruct((HEADS, LANES), F32)] * 2,
        scratch_shapes=[pltpu.VMEM((2, HEADS, LANES, LANES), F32), pltpu.VMEM((2, HEADS, C, C), F32),
                        pltpu.VMEM((2, HEADS, C, LANES), F32), pltpu.VMEM((2, HEADS, C, LANES), F32)],
        compiler_params=_params(),
    )(qr, kr, proj, dret, sf, qr, kr, proj, dret, sb, dec_f, dec_b)


def _pad_heads(w, hd):
    K = w.shape[0]
    return jnp.pad(w.reshape(K, HEADS, hd), ((0, 0), (0, 0), (0, LANES - hd))).reshape(K, HEADS * LANES)


def _unpad_heads(w, hd):
    K = w.shape[0]
    return w.reshape(K, HEADS, LANES)[:, :, :hd].reshape(K, HEADS * hd)


def _rope_consts(first_lane, half, period=LANES):
    lane = np.arange(LANES) % period
    first = ((lane >= first_lane) & (lane < first_lane + half)).astype(np.float32)
    second = ((lane >= first_lane + half) & (lane < first_lane + 2 * half)).astype(np.float32)
    fixed = (lane < first_lane).astype(np.float32)
    j = np.where(first > 0, lane - first_lane, lane - first_lane - half) * (first + second)
    inv = (ROPE_THETA ** (-j.astype(np.float64) / half)).astype(np.float32)
    return [jnp.asarray(v.reshape(1, LANES), F32) for v in (inv, first, second, fixed)]


def _assemble(name, gathered):
    if name in COL_SHARDED:
        return jnp.transpose(gathered, (1, 0, 2)).reshape(gathered.shape[1], 4 * gathered.shape[2])
    return gathered.reshape(4 * gathered.shape[1], gathered.shape[2])


def _split_for_reducers(name, g, dtype):
    if name in COL_SHARDED:
        K, N4 = g.shape
        return jnp.transpose(g.reshape(2, K // 2, 4, N4 // 4), (2, 0, 1, 3)).astype(dtype)
    return g.reshape(4, 2, g.shape[0] // 8, g.shape[1]).astype(dtype)


def _local_step(x, tab_m, tab_r, tgt, wts, late_shards, small):
    w_in = wts["w_in"]
    seg = [w_in[:, IN_OFFS[i]:IN_OFFS[i + 1]] for i in range(8)]
    kr_w = jnp.pad(seg[2], ((0, 0), (MLA_NOPE, LANES - MLA_QK)))
    w_in_p = jnp.concatenate([seg[7], seg[5], seg[6], seg[3], seg[4], seg[0], seg[1], kr_w], axis=1)
    QR0, KR0, CQ0 = 4096, 4608, 5120
    w_qb_p = _pad_heads(wts["w_q_b"], MLA_QK)
    kvw = wts["w_kv_b"].reshape(MLA_KV_RANK, HEADS, MLA_NOPE + MLA_V)
    pad_kv = lambda t: jnp.pad(t, ((0, 0), (0, 0), (0, LANES - t.shape[2]))).reshape(MLA_KV_RANK, HEADS * LANES)
    w_kn_p, w_v = pad_kv(kvw[:, :, :MLA_NOPE]), kvw[:, :, MLA_NOPE:].reshape(MLA_KV_RANK, HEADS * MLA_V)
    w_kv_p = jnp.concatenate([w_kn_p, w_v], axis=1)
    g_qn_p = jnp.pad(small["g_qn"], ((0, 0), (0, LANES - MLA_QK)))
    g_kn_p = jnp.pad(small["g_kn"], ((0, 0), (0, LANES - MLA_QK)))
    dec_f = jnp.broadcast_to(small["ret_decay_fwd"].reshape(HEADS, 1), (HEADS, LANES))
    dec_b = jnp.broadcast_to(small["ret_decay_bwd"].reshape(HEADS, 1), (HEADS, LANES))
    T, N = True, False
    RT, HT = ROW_TILE, HEAD_ROW_TILE
    RW = 2 * ROW_TILE

    aux_m = [(t, LANES, 0, N) for t in tab_m]
    aux_r = [(t, LANES, 0, N) for t in tab_r]

    proj, h = _mm("proj", x, w_in_p, "nn", BF16, a_gain=small["g_mix"])
    rows_a = [(proj, MLA_Q_RANK, CQ0 // MLA_Q_RANK, N), (proj, MLA_KV_RANK, (CQ0 + MLA_Q_RANK) // MLA_KV_RANK, N)]
    cqn, ckvn = _rowwise("mla_lat_norm", _f_mla_a, [small["g_q_a"], small["g_kv_a"]], rows_a, [],
                         [(MLA_Q_RANK, BF16, N), (MLA_KV_RANK, BF16, N)], RW)
    qraw = _mm("mla_q_up", cqn, w_qb_p, "nn", BF16)
    kv = _mm("mla_kv_up", ckvn, w_kv_p, "nn", BF16)
    rows_b = [(qraw, LANES, 0, T), (kv, LANES, 0, T), (proj, LANES, (CQ0 + MLA_Q_RANK + MLA_KV_RANK) // LANES, N)]
    q, k = _rowwise("mla_qk_norm_rope", _f_mla_b, [g_qn_p, g_kn_p], rows_b, aux_m, [(LANES, BF16, T)] * 2, HT, HEADS)
    o, lse, late = _flash_fwd(q, k, kv, [late_shards[n] for n in LATE])
    wl = {n: _assemble(n, g) for n, g in zip(LATE, late)}
    w_mla_p = wl["w_mla_out"]
    w_ret_out, w_out, w_gu, w_down = wl["w_ret_out"], wl["w_out"], wl["w_gate_up"], wl["w_down"]
    y_a = _mm("mla_out", o, w_mla_p, "nn", BF16)
    rows_rr = [(proj, LANES, QR0 // LANES, T), (proj, LANES, KR0 // LANES, T)]
    qr, kr = _rowwise("ret_rope", _f_ret_rope, [], rows_rr, aux_r, [(LANES, RET_QK_DTYPE, T)] * 2, HT, HEADS // 2)
    ret_f, ret_b, st_f, st_b = _ret_fwd(qr, kr, proj, 2, dec_f, dec_b)
    rows_rp = [(ret_f, LANES, 0, T), (ret_b, LANES, 0, T), (proj, LANES, 24, T)]
    (o_b,) = _rowwise("ret_post", _f_ret_post, [], rows_rp, [], [(LANES, BF16, T)], HT, HEADS)
    y_b = _mm("ret_out", o_b, w_ret_out, "nn", BF16)
    rows_m = [(proj, D_MODEL, 0, N), (proj, D_MODEL, 1, N), (y_a, D_MODEL, 0, N), (y_b, D_MODEL, 0, N)]
    (merged,) = _rowwise("merge", _f_merge, [], rows_m, [], [(D_MODEL, BF16, N)], RW)
    x2 = _mm("mix_out", merged, w_out, "nn", F32, res=x)
    gu, h2 = _mm("ffn_gate_up", x2, w_gu, "nn", BF16, a_gain=small["g_ffn"])
    rows_sw = [(gu, FFN_HIDDEN, 0, N), (gu, FFN_HIDDEN, 1, N)]
    (act,) = _rowwise("swiglu", _f_swiglu, [], rows_sw, [], [(FFN_HIDDEN, BF16, N)], RT)
    dy, dy_b16, loss_row = _mm("ffn_down", act, w_down, "nn", None,
                               epilogue=(_epi_loss, [x2, tgt], [], [F32, BF16], [(1, LANES)]))

    dact = _mm("d_act", dy_b16, w_down, "nt", BF16)
    dw_down = _mm("dw_down", act, dy_b16, "tn", BF16)
    (dgu,), _ = _rowwise_vjp("swiglu_bwd", _f_swiglu, [], rows_sw, [], [[(dact, FFN_HIDDEN, 0, N)]], [([0, 1], BF16)], RT)
    dx2, dx2_b16, dg_ffn = _mm("d_h2", dgu, w_gu, "nt", None,
                               epilogue=(_epi_rms_bwd(2), [x2, dy], [small["g_ffn"]], [F32, BF16], [(1, D_MODEL)]))
    dw_gu = _mm("dw_gate_up", h2, dgu, "tn", BF16, shard_out=True)
    dmerged = _mm("d_merged", dx2_b16, w_out, "nt", BF16)
    dw_out = _mm("dw_out", merged, dx2_b16, "tn", BF16)
    (dgl, dy_a, dy_b), _ = _rowwise_vjp("merge_bwd", _f_merge, [], rows_m, [], [[(dmerged, D_MODEL, 0, N)]],
                                        [([0, 1], BF16), ([2], BF16), ([3], BF16)], RW)
    do_b = _mm("d_ret_o", dy_b, w_ret_out, "nt", BF16)
    dw_ret_out = _mm("dw_ret_out", o_b, dy_b, "tn", BF16)
    (dret, dg_r), _ = _rowwise_vjp("ret_post_bwd", _f_ret_post, [], rows_rp, [], [[(do_b, LANES, 0, T)]],
                                   [([0], BF16), ([2], BF16)], HT, HEADS)
    dqf, dkf, dvf, dqb, dkb, dvb, ddec_f, ddec_b = _ret_bwd(qr, kr, proj, 2, dret, st_f, st_b, dec_f, dec_b)
    (dq_r, dk_r), _ = _rowwise_vjp("ret_rope_bwd", _f_ret_rope, [], rows_rr, aux_r,
                                   [[(dqf, LANES, 0, T), (dqb, LANES, 0, T)], [(dkf, LANES, 0, T), (dkb, LANES, 0, T)]],
                                   [([0], BF16), ([1], BF16)], HT, HEADS // 2)
    (dv_r,) =_rowwise("ret_dv_sum", _f_add, [], [(dvf, D_MODEL, 0, N), (dvb, D_MODEL, 0, N)], [], [(D_MODEL, BF16, N)], RW)
    do = _mm("d_mla_o", dy_a, w_mla_p, "nt", BF16)
    dw_mla = _mm("dw_mla_out", o, dy_a, "tn", BF16)
    late_grads = {"w_mla_out": dw_mla, "w_ret_out": dw_ret_out, "w_out": dw_out, "w_down": dw_down}
    late_gs = [dw_gu if n == "w_gate_up" else _split_for_reducers(n, late_grads[n], BF16) for n in LATE]
    dq, dk, dv, late_got = _flash_bwd(q, k, kv, do, lse, o, late_gs)
    (dqraw, dkn, dkr), (dg_qn_p, dg_kn_p) = _rowwise_vjp(
        "mla_qk_norm_rope_bwd", _f_mla_b, [g_qn_p, g_kn_p], rows_b, aux_m, [[(dq, LANES, 0, T)], [(dk, LANES, 0, T)]],
        [([0], BF16), ([1], BF16), ([2], F32)], HT, HEADS)
    dckvn = _mm("d_ckvn_v", dv, w_v, "nt", BF16, res=_mm("d_ckvn_k", dkn, w_kn_p, "nt", F32))
    dw_kn_p = _mm("dw_kv_k", ckvn, dkn, "tn", BF16)
    dw_v = _mm("dw_kv_v", ckvn, dv, "tn", BF16)
    dcqn = _mm("d_cqn", dqraw, w_qb_p, "nt", BF16)
    dw_qb_p = _mm("dw_q_b", cqn, dqraw, "tn", BF16)
    (dcq, dckv), (dg_q_a, dg_kv_a) = _rowwise_vjp(
        "mla_lat_norm_bwd", _f_mla_a, [small["g_q_a"], small["g_kv_a"]], rows_a, [],
        [[(dcqn, MLA_Q_RANK, 0, N)], [(dckvn, MLA_KV_RANK, 0, N)]], [([0], BF16), ([1], BF16)], RW)
    dproj = jnp.concatenate([dgl, dv_r, dg_r, dq_r, dk_r, dcq, dckv, dkr.astype(BF16)], axis=1)
    dw_in_p = _mm("dw_in", h, dproj, "tn", BF16)

    c = lambda a, b_: dw_in_p[:, a:b_]
    kr0 = CQ0 + MLA_Q_RANK + MLA_KV_RANK
    dw_in = jnp.concatenate([c(CQ0, CQ0 + MLA_Q_RANK), c(CQ0 + MLA_Q_RANK, kr0), c(kr0 + MLA_NOPE, kr0 + MLA_QK), c(QR0, KR0),
                             c(KR0, CQ0), c(2048, 3072), c(3072, 4096), c(0, 2048)], axis=1)
    dw_kn = dw_kn_p.reshape(MLA_KV_RANK, HEADS, LANES)[:, :, :MLA_NOPE]
    dw_kv = jnp.concatenate([dw_kn, dw_v.reshape(MLA_KV_RANK, HEADS, MLA_V)], axis=2).reshape(MLA_KV_RANK, HEADS * (MLA_NOPE + MLA_V))
    grads = {"w_in": dw_in, "w_q_b": _unpad_heads(dw_qb_p, MLA_QK), "w_kv_b": dw_kv}
    dx, dg_mix, first_got = _mm("d_h", dproj, w_in_p, "nt", None,
                                epilogue=(_epi_rms_bwd(1), [x, dx2], [small["g_mix"]], [F32], [(1, D_MODEL)]),
                                scatter=[_split_for_reducers(n, grads[n], BF16) for n in FIRST])
    sgrads = {"g_mix": dg_mix, "g_q_a": dg_q_a, "g_kv_a": dg_kv_a, "g_qn": dg_qn_p[:, :MLA_QK], "g_kn": dg_kn_p[:, :MLA_QK],
              "ret_decay_fwd": ddec_f[:, 0].reshape(1, HEADS), "ret_decay_bwd": ddec_b[:, 0].reshape(1, HEADS), "g_ffn": dg_ffn}
    return loss_row, dx, first_got + late_got, sgrads


def _coords():
    return lax.axis_index("x"), lax.axis_index("y"), lax.axis_index("c")


def _other_chips(x, y):
    return [(1 - x, y), (x, 1 - y), (1 - x, 1 - y)]


ANY = pl.BlockSpec(memory_space=pl.ANY)


def _gather_copies(ins, outs, send_sems, recv_sems):
    x, y, c = _coords()
    mine = 2 * x + y
    sends, arrivals = [], []
    for w in range(len(ins)):
        for j, (cx, cy) in enumerate(_other_chips(x, y)):
            sems = dict(send_sem=send_sems.at[3 * w + j], recv_sem=recv_sems.at[3 * w + j],
                        device_id=(cx, cy, c), device_id_type=MESH)
            sends.append(pltpu.make_async_remote_copy(src_ref=ins[w], dst_ref=outs[w].at[mine], **sems))
            arrivals.append(functools.partial(pltpu.make_async_remote_copy, src_ref=ins[w],
                                              dst_ref=outs[w].at[2 * cx + cy], **sems))
    return sends, arrivals


def _gather_start(copies):
    for cp in list(copies[0]) + list(copies[2] if len(copies) > 2 else []):
        cp.start()


def _gather_wait(copies):
    sends, arrivals = copies[:2]
    for make in arrivals:
        make().wait_recv()
    for cp in sends:
        cp.wait_send()
    for cp in (copies[2] if len(copies) > 2 else []):
        cp.wait()


def _fill_slot(buf, piece, slot):
    idx = lax.broadcasted_iota(jnp.int32, (buf.shape[0],) + (1,) * piece.ndim, 0)
    return jnp.where(idx == slot, piece[None], buf)


def _rope_tables_and_first_gather(pos, consts_mla, consts_ret, shards):
    S = pos.shape[0]
    tm = min(HEAD_ROW_TILE, S)
    nt = S // tm
    n = len(shards)

    def body(pos_ref, *refs):
        consts, ins = (refs[:4], refs[4:8]), refs[8:8 + n]
        tabs, outs = refs[8 + n:14 + n], refs[14 + n:14 + 2 * n]
        send_sems, recv_sems = refs[14 + 2 * n:]
        i = pl.program_id(0)
        x, y, c = _coords()
        chips = _other_chips(x, y)
        mine = 2 * x + y

        def half(ref, slot, core):
            rows = ref.shape[1] // 2
            return ref.at[slot, pl.ds(pl.multiple_of(core * rows, 8), rows)]

        def copy(w, k, slot, core, to, src=None):
            return pltpu.make_async_remote_copy(
                src_ref=half(outs[w], slot, core) if src is None else src, dst_ref=half(outs[w], slot, core),
                send_sem=send_sems.at[6 * w + k], recv_sem=recv_sems.at[6 * w + k], device_id=to, device_id_type=MESH)

        def first(w, j):
            rows = ins[w].shape[0] // 2
            return copy(w, j, mine, c, (*chips[j], c), src=ins[w].at[pl.ds(pl.multiple_of(c * rows, 8), rows)])

        @pl.when(i == 0)
        def _():
            for w in range(n):
                for j in range(3):
                    first(w, j).start()

        for k in range(2):
            vals = _f_rope_table([r[...] for r in consts[k]], None, [pos_ref[...]])
            for t_ref, v in zip(tabs[3 * k:3 * k + 3], vals):
                t_ref[...] = v

        @pl.when(i == nt - 1)
        def _():
            passed = []
            for w in range(n):
                for j, (cx, cy) in enumerate(chips):
                    copy(w, j, 2 * cx + cy, c, (x, y, c)).wait_recv()
                    cp = copy(w, 3 + j, 2 * cx + cy, c, (x, y, 1 - c))
                    cp.start()
                    passed.append(cp)
            for w in range(n):
                for j, (cx, cy) in enumerate(chips):
                    copy(w, 3 + j, 2 * cx + cy, 1 - c, (x, y, c)).wait_recv()
            for w in range(n):
                for j in range(3):
                    first(w, j).wait_send()
            for cp in passed:
                cp.wait_send()

    const = lambda p: pl.BlockSpec(p.shape, lambda i: (0, 0))
    tab = pl.BlockSpec((tm, LANES), lambda i: (i, 0))
    res = pl.pallas_call(
        body, name="rope_tables_first_gather", grid=(nt,),
        in_specs=[pl.BlockSpec((tm, 1), lambda i: (i, 0))] + [const(p) for p in list(consts_mla) + list(consts_ret)] + [ANY] * n,
        out_specs=[tab] * 6 + [ANY] * n,
        out_shape=[jax.ShapeDtypeStruct((S, LANES), F32)] * 6 + [jax.ShapeDtypeStruct((4,) + s.shape, s.dtype) for s in shards],
        scratch_shapes=[pltpu.SemaphoreType.DMA((6 * n,)), pltpu.SemaphoreType.DMA((6 * n,))],
        compiler_params=_params(),
    )(pos, *consts_mla, *consts_ret, *shards)
    return list(res[:3]), list(res[3:6]), list(res[6:])


def _scatter_copies(ins, outs, send_sems, recv_sems):
    x, y, c = _coords()
    me = 4 * x + 2 * y + c
    n = len(ins)
    sends, arrivals = [], []
    local = [pltpu.make_async_copy(ins[w].at[2 * x + y, c], outs[w].at[me], send_sems.at[7 * n + w]) for w in range(n)]
    for w in range(n):
        for k in range(1, 8):
            px, py, pc = x ^ (k >> 2), y ^ ((k >> 1) & 1), c ^ (k & 1)
            sems = dict(send_sem=send_sems.at[7 * w + k - 1], recv_sem=recv_sems.at[7 * w + k - 1],
                        device_id=(px, py, pc), device_id_type=MESH)
            sends.append(pltpu.make_async_remote_copy(src_ref=ins[w].at[2 * px + py, pc], dst_ref=outs[w].at[me], **sems))
            arrivals.append(functools.partial(
                pltpu.make_async_remote_copy, src_ref=ins[w].at[2 * px + py, pc],
                dst_ref=outs[w].at[4 * px + 2 * py + pc], **sems))
    return sends, arrivals, local


_scatter_start, _scatter_wait = _gather_start, _gather_wait


def _grad_sum8(name, got):
    _, R, W = got.shape
    tr = _pick(R, 256, 16)

    def body(g_ref, o_ref):
        total = g_ref[0].astype(F32)
        for d in range(1, 8):
            total = total + g_ref[d].astype(F32)
        o_ref[...] = total

    return pl.pallas_call(
        body, name=name, grid=(R // tr,), in_specs=[pl.BlockSpec((8, tr, W), lambda i: (0, i, 0))],
        out_specs=pl.BlockSpec((tr, W), lambda i: (i, 0)), out_shape=jax.ShapeDtypeStruct((R, W), F32),
        compiler_params=_params(),
    )(got)


def _half_exchange(halves):
    n = len(halves)

    def body(*refs):
        ins, outs, send_sems, recv_sems = refs[:n], refs[n:2 * n], refs[2 * n], refs[2 * n + 1]
        x, y, c = _coords()
        sends = []
        for w in range(n):
            cp = pltpu.make_async_remote_copy(
                src_ref=ins[w], dst_ref=outs[w], send_sem=send_sems.at[w], recv_sem=recv_sems.at[w],
                device_id=(x, y, 1 - c), device_id_type=MESH)
            cp.start()
            sends.append(cp)
        for cp in sends:
            cp.wait()

    got = pl.pallas_call(
        body, name="grad_half_exchange", in_specs=[ANY] * n, out_specs=[ANY] * n,
        out_shape=[jax.ShapeDtypeStruct(h.shape, F32) for h in halves],
        scratch_shapes=[pltpu.SemaphoreType.DMA((n,)), pltpu.SemaphoreType.DMA((n,))],
    )(*halves)
    c = lax.axis_index("c")
    return [jnp.where(c == 0, jnp.stack([mine, theirs]), jnp.stack([theirs, mine])) for mine, theirs in zip(halves, got)]


def _adamw_math(w, g, m, v):
    m2 = ADAM_B1 * m + (1.0 - ADAM_B1) * g
    v2 = ADAM_B2 * v + (1.0 - ADAM_B2) * (g * g)
    m_hat = m2 / (1.0 - ADAM_B1 ** ADAM_STEP)
    v_hat = v2 / (1.0 - ADAM_B2 ** ADAM_STEP)
    return -ADAM_LR * (m_hat / (jnp.sqrt(v_hat) + ADAM_EPS) + ADAM_WD * w), m2, v2


def _small_allreduce_adamw(pack_g, pack_w, pack_m, pack_v):
    def body(g_ref, w_ref, m_ref, v_ref, sum_ref, d_ref, m_out, v_out, land, send_sems, recv_sems):
        x, y, c = _coords()
        me = 4 * x + 2 * y + c
        land[me] = g_ref[...]
        sends = []
        for k in range(1, 8):
            peer = (x ^ (k >> 2), y ^ ((k >> 1) & 1), c ^ (k & 1))
            cp = pltpu.make_async_remote_copy(
                src_ref=g_ref, dst_ref=land.at[me], send_sem=send_sems.at[k - 1], recv_sem=recv_sems.at[k - 1],
                device_id=peer, device_id_type=MESH)
            cp.start()
            sends.append((cp, peer))
        for k, (cp, peer) in enumerate(sends):
            pltpu.make_async_remote_copy(
                src_ref=g_ref, dst_ref=land.at[4 * peer[0] + 2 * peer[1] + peer[2]], send_sem=send_sems.at[k],
                recv_sem=recv_sems.at[k], device_id=peer, device_id_type=MESH).wait_recv()
        for cp, _ in sends:
            cp.wait_send()
        total = land[0]
        for d in range(1, 8):
            total = total + land[d]
        sum_ref[...] = total
        d_ref[...], m_out[...], v_out[...] = _adamw_math(w_ref[...], total, m_ref[...], v_ref[...])

    vm = pl.BlockSpec(memory_space=pltpu.VMEM)
    shp = jax.ShapeDtypeStruct(pack_g.shape, F32)
    return pl.pallas_call(
        body, name="small_allreduce_adamw", in_specs=[vm] * 4, out_specs=[vm] * 4, out_shape=[shp] * 4,
        scratch_shapes=[pltpu.VMEM((8,) + pack_g.shape, F32), pltpu.SemaphoreType.DMA((7,)), pltpu.SemaphoreType.DMA((7,))],
    )(pack_g, pack_w, pack_m, pack_v)


def _adamw(name, w, g, m, v):
    R, C = w.shape
    tr = _pick(R, 256, 8)

    def body(w_ref, g_ref, m_ref, v_ref, d_out, m_out, v_out):
        d_out[...], m_out[...], v_out[...] = _adamw_math(w_ref[...], g_ref[...], m_ref[...], v_ref[...])

    spec = pl.BlockSpec((tr, C), lambda i: (i, 0))
    return pl.pallas_call(
        body, name=name, grid=(R // tr,), in_specs=[spec] * 4, out_specs=[spec] * 3,
        out_shape=[jax.ShapeDtypeStruct((R, C), F32)] * 3, compiler_params=_params(),
    )(w, g, m, v)


def _pack_small(vals, last):
    flat = jnp.concatenate([v.reshape(-1) for v in vals] + [last.reshape(-1)])
    return jnp.pad(flat, (0, SMALL_ROWS * LANES - flat.shape[0])).reshape(SMALL_ROWS, LANES)


def kernel(x, positions, g_mix, w_in, g_q_a, w_q_b, g_kv_a, w_kv_b, g_qn, g_kn, w_mla_out, ret_decay_fwd, ret_decay_bwd, w_ret_out, w_out, g_ffn, w_gate_up, w_down, loss_target, m_g_mix, m_w_in, m_g_q_a, m_w_q_b, m_g_kv_a, m_w_kv_b, m_g_qn, m_g_kn, m_w_mla_out, m_ret_decay_fwd, m_ret_decay_bwd, m_w_ret_out, m_w_out, m_g_ffn, m_w_gate_up, m_w_down, v_g_mix, v_w_in, v_g_q_a, v_w_q_b, v_g_kv_a, v_w_kv_b, v_g_qn, v_g_kn, v_w_mla_out, v_ret_decay_fwd, v_ret_decay_bwd, v_w_ret_out, v_w_out, v_g_ffn, v_w_gate_up, v_w_down):
    given = dict(locals())
    S = x.shape[1]
    xs, tgt = x.reshape(S, D_MODEL), loss_target.reshape(S, D_MODEL)
    pos = positions.reshape(S, 1).astype(F32)

    first_shards = [given[n].astype(BF16) for n in FIRST]
    my_chip = 2 * lax.axis_index("x") + lax.axis_index("y")
    tab_m, tab_r, gathered = _rope_tables_and_first_gather(
        pos, _rope_consts(MLA_NOPE, MLA_ROPE // 2), _rope_consts(0, RET_QK // 2, RET_QK), first_shards)
    wts = {n: _assemble(n, _fill_slot(g, s, my_chip)) for n, g, s in zip(FIRST, gathered, first_shards)}
    late_shards = {n: given[n].astype(BF16) for n in LATE}
    small = {n: given[n].reshape(1, -1) for n in SMALL}

    loss_row, dx, pieces, sgrads = _local_step(xs, tab_m, tab_r, tgt, wts, late_shards, small)

    halves = [_grad_sum8("grad_sum_" + n, got) for n, got in zip(FIRST + LATE, pieces)]
    reduced = _half_exchange(halves)

    out = {}
    for n, r in zip(FIRST + LATE, reduced):
        g = r.reshape(given[n].shape)
        out["grad_" + n] = g
        out["delta_" + n], out["new_m_" + n], out["new_v_" + n] = _adamw("adamw_" + n, given[n], g, given["m_" + n], given["v_" + n])

    one = jnp.ones((1,), F32)
    pk = _small_allreduce_adamw(
        _pack_small([sgrads[n] for n in SMALL], loss_row[0, :1]),
        _pack_small([given[n] for n in SMALL], 0 * one),
        _pack_small([given["m_" + n] for n in SMALL], 0 * one),
        _pack_small([given["v_" + n] for n in SMALL], one))
    off = 0
    for n in SMALL:
        sz = given[n].shape[0]
        for pre, arr in zip(["grad_", "delta_", "new_m_", "new_v_"], pk):
            out[pre + n] = arr.reshape(-1)[off:off + sz]
        off += sz
    loss = pk[0].reshape(-1)[off]

    return (loss, dx.reshape(x.shape), *[out["grad_" + n] for n in WEIGHTS], *[out["delta_" + n] for n in WEIGHTS],
            *[out["new_m_" + n] for n in WEIGHTS], *[out["new_v_" + n] for n in WEIGHTS])
```

```python
import functools
import math

import numpy as np
import jax
import jax.numpy as jnp
from jax import lax
from jax.experimental import pallas as pl
from jax.experimental.pallas import tpu as pltpu

F32 = jnp.float32
BF16 = jnp.bfloat16
MESH = pl.DeviceIdType.MESH

D_MODEL = 1024
HEADS = 8
LANES = 128
MLA_Q_RANK, MLA_KV_RANK = 256, 128
MLA_NOPE, MLA_ROPE, MLA_V = 64, 32, 64
MLA_QK = MLA_NOPE + MLA_ROPE
LN2 = math.log(2.0)
MLA_Q_SCALE = MLA_QK ** -0.5 / LN2
RET_QK, RET_V, RET_CHUNK = 64, 128, 128
RET_QK_DTYPE = BF16
RET_CHUNKS_PER_STEP = 2
FFN_HIDDEN = 2816
ROPE_THETA = 10000.0
EPS = 1e-6
IN_SPLITS = [256, 128, 32, 512, 512, 1024, 1024, 2048]
IN_OFFS = [0] + list(np.cumsum(IN_SPLITS))
ADAM_LR, ADAM_B1, ADAM_B2, ADAM_EPS, ADAM_WD, ADAM_STEP = 0.001, 0.9, 0.999, 1e-08, 0.01, 10

VMEM_LIMIT = 56 * 1024 * 1024
ROW_TILE = 256
HEAD_ROW_TILE = 2048
MM_TM, MM_TN, MM_TK, MM_KFULL = 1408, 2048, 2048, 2816
ATT_TQ = 256
ATT_BQ, ATT_BK = 1024, 1024
ATT_HEADS_PER_STEP = 8
ATT_BWD_HEADS_PER_STEP = 4

SHARDED = ["w_in", "w_q_b", "w_kv_b", "w_mla_out", "w_ret_out", "w_out", "w_gate_up", "w_down"]
COL_SHARDED = {"w_in", "w_q_b", "w_kv_b", "w_mla_out", "w_gate_up"}
FIRST = ["w_in", "w_q_b", "w_kv_b"]
LATE = ["w_mla_out", "w_ret_out", "w_out", "w_gate_up", "w_down"]
SMALL = ["g_mix", "g_q_a", "g_kv_a", "g_qn", "g_kn", "ret_decay_fwd", "ret_decay_bwd", "g_ffn"]
WEIGHTS = ["g_mix", "w_in", "g_q_a", "w_q_b", "g_kv_a", "w_kv_b", "g_qn", "g_kn", "w_mla_out",
           "ret_decay_fwd", "ret_decay_bwd", "w_ret_out", "w_out", "g_ffn", "w_gate_up", "w_down"]
SMALL_ROWS = 24


def _params(**kw):
    return pltpu.CompilerParams(vmem_limit_bytes=VMEM_LIMIT, **kw)


def _pick(dim, target, unit=128):
    if dim <= target:
        return dim
    best = None
    for d in range(unit, target + 1, unit):
        if dim % d == 0:
            best = d
    assert best is not None, (dim, target)
    return best


_DOT = {"nn": (((1,), (0,)), ((), ())), "nt": (((1,), (1,)), ((), ())), "tn": (((0,), (0,)), ((), ()))}


def _dot(a, b, mode="nn"):
    return lax.dot_general(a, b, _DOT[mode], preferred_element_type=F32)


def _rms_rows(x, g):
    x = x.astype(F32)
    return x * lax.rsqrt(jnp.mean(x * x, axis=-1, keepdims=True) + EPS) * g


def _epi_loss(acc, extras, params):
    e = acc + extras[0] - extras[1]
    dy = e * (1.0 / D_MODEL)
    loss = 0.5 * jnp.sum(jnp.mean(e * e, axis=-1, keepdims=True), axis=0, keepdims=True)
    return [dy, dy], [jnp.broadcast_to(loss, (1, LANES))]


def _epi_rms_bwd(n_out):
    def fn(acc, extras, params):
        _, vjp = jax.vjp(_rms_rows, extras[0], params[0])
        dx, dg = vjp(acc)
        return [dx + extras[1]] * n_out, [dg]
    return fn


def _mm(name, a, b, mode, out_dtype, res=None, a_gain=None, epilogue=None, shard_out=False, scatter=None):
    if mode == "nn":
        (M, K), (K2, N) = a.shape, b.shape
    elif mode == "nt":
        (M, K), (N, K2) = a.shape, b.shape
    else:
        (K, M), (K2, N) = a.shape, b.shape
    assert K == K2, (name, a.shape, b.shape)
    tm, tn = _pick(M, MM_TM), _pick(N, MM_TN)
    tk = K if K <= MM_KFULL else _pick(K, MM_TK)
    if shard_out:
        tm, tn = M // 2, N // 4
    if epilogue is not None:
        tm = _pick(M, MM_TM // 2)
    nk = K // tk
    cache_a = a_gain is not None
    if a_gain is not None:
        assert mode == "nn" and tk == K and epilogue is None and not shard_out, name
    n_in = 2 + (res is not None) + (a_gain is not None)
    extras, eparams, e_outs, e_sums = ([], [], [], [])
    if epilogue is not None:
        assert tn == N and res is None and not shard_out, name
        epi_fn, extras, eparams, e_outs, e_sums = epilogue
    n_out = len(e_outs) + len(e_sums) if epilogue is not None else 1 + cache_a
    scatter = list(scatter or [])
    n_sc = len(scatter)
    assert not n_sc or epilogue is not None, name
    ni, nj = M // tm, N // tn

    def body(*refs):
        a_ref, b_ref = refs[0], refs[1]
        base = n_in + len(extras) + len(eparams)
        ex_refs = refs[n_in:n_in + len(extras)]
        ep_refs = refs[n_in + len(extras):base]
        sc_in, out_refs = refs[base:base + n_sc], refs[base + n_sc:base + n_sc + n_out]
        sc_out = refs[base + n_sc + n_out:base + 2 * n_sc + n_out]
        scratch = refs[base + 2 * n_sc + n_out:]
        acc = scratch[0]
        i, j, k = pl.program_id(0), pl.program_id(1), pl.program_id(2)

        if n_sc:
            @pl.when(jnp.logical_and(i == 0, jnp.logical_and(j == 0, k == 0)))
            def _():
                _scatter_start(_scatter_copies(sc_in, sc_out, scratch[-2], scratch[-1]))

        @pl.when(k == 0)
        def _():
            acc[...] = jnp.zeros_like(acc)

        if cache_a:
            @pl.when(j == 0)
            def _():
                out_refs[1][...] = _rms_rows(a_ref[...], refs[n_in - 1][...]).astype(BF16)
            av = out_refs[1][...]
        else:
            av = a_ref[...].astype(BF16)
        acc[...] += _dot(av, b_ref[...].astype(BF16), mode)

        @pl.when(k == nk - 1)
        def _():
            if epilogue is None:
                r = acc[...]
                if res is not None:
                    r = r + refs[2][...].astype(F32)
                out_refs[0][...] = r.astype(out_refs[0].dtype).reshape(out_refs[0].shape)
            else:
                vals, sums = epi_fn(acc[...], [r[...] for r in ex_refs], [p[...] for p in ep_refs])
                for o_ref, v in zip(out_refs, vals):
                    o_ref[...] = v.astype(o_ref.dtype)
                for s_ref, v in zip(out_refs[len(vals):], sums):
                    @pl.when(i == 0)
                    def _(s_ref=s_ref):
                        s_ref[...] = jnp.zeros_like(s_ref)
                    s_ref[...] += v

        if n_sc:
            @pl.when(jnp.logical_and(i == ni - 1, jnp.logical_and(j == nj - 1, k == nk - 1)))
            def _():
                _scatter_wait(_scatter_copies(sc_in, sc_out, scratch[-2], scratch[-1]))

    a_spec = pl.BlockSpec((tk, tm), lambda i, j, k: (k, i)) if mode == "tn" else pl.BlockSpec((tm, tk), lambda i, j, k: (i, k))
    b_spec = pl.BlockSpec((tn, tk), lambda i, j, k: (j, k)) if mode == "nt" else pl.BlockSpec((tk, tn), lambda i, j, k: (k, j))
    o_spec = pl.BlockSpec((tm, tn), lambda i, j, k: (i, j))
    const = lambda p: pl.BlockSpec(p.shape, lambda i, j, k: (0,) * p.ndim)
    ins, specs = [a, b], [a_spec, b_spec]
    if res is not None:
        ins.append(res)
        specs.append(o_spec)
    if a_gain is not None:
        ins.append(a_gain)
        specs.append(const(a_gain))
    ins += list(extras) + list(eparams)
    specs += [o_spec] * len(extras) + [const(p) for p in eparams]
    if epilogue is not None:
        out_specs = [o_spec] * len(e_outs) + [pl.BlockSpec(s, lambda i, j, k: (0, 0)) for s in e_sums]
        out_shape = [jax.ShapeDtypeStruct((M, N), dt) for dt in e_outs] + [jax.ShapeDtypeStruct(s, F32) for s in e_sums]
    elif shard_out:
        out_specs = pl.BlockSpec((1, 1, tm, tn), lambda i, j, k: (j, i, 0, 0))
        out_shape = jax.ShapeDtypeStruct((4, 2, tm, tn), out_dtype)
    elif cache_a:
        out_specs = [o_spec, pl.BlockSpec((tm, K), lambda i, j, k: (i, 0))]
        out_shape = [jax.ShapeDtypeStruct((M, N), out_dtype), jax.ShapeDtypeStruct((M, K), BF16)]
    else:
        out_specs, out_shape = o_spec, jax.ShapeDtypeStruct((M, N), out_dtype)
    scratch_shapes = [pltpu.VMEM((tm, tn), F32)]
    if n_sc:
        ins += scatter
        specs += [ANY] * n_sc
        out_specs = list(out_specs) + [ANY] * n_sc
        out_shape = list(out_shape) + [jax.ShapeDtypeStruct((8,) + g.shape[2:], g.dtype) for g in scatter]
        scratch_shapes += [pltpu.SemaphoreType.DMA((8 * n_sc,)), pltpu.SemaphoreType.DMA((7 * n_sc,))]
    res_ = pl.pallas_call(
        body, name=name, grid=(ni, nj, nk), in_specs=specs, out_specs=out_specs, out_shape=out_shape,
        scratch_shapes=scratch_shapes, compiler_params=_params(),
    )(*ins)
    if n_sc:
        return list(res_[:n_out]) + [list(res_[n_out:])]
    return res_


def _piece_spec(tm, piece):
    _, w, c0, per_group = piece
    if per_group:
        return pl.BlockSpec((tm, w), lambda i, g: (i, c0 + g))
    return pl.BlockSpec((tm, w), lambda i, g: (i, c0))


def _const_spec(p):
    return pl.BlockSpec(p.shape, lambda i, g: (0, 0))


def _place_into(into, S, tm, out_specs, out_shape, n_inputs):
    if into is None:
        return [], [], {}
    k, buf, total, c0 = into
    (_, w), per_group = out_specs[k].block_shape, out_shape[k].shape[1] != out_specs[k].block_shape[1]
    out_specs[k] = _piece_spec(tm, (None, w, c0, per_group))
    out_shape[k] = jax.ShapeDtypeStruct((S, total), out_shape[k].dtype)
    if buf is None:
        return [], [], {}
    return [buf], [pl.BlockSpec(memory_space=pl.ANY)], {n_inputs: k}


def _rowwise(name, fn, params, rows, auxs, outs, tm, groups=1, into=None):
    S = rows[0][0].shape[0]
    tm = min(tm, S)
    n_p, n_r, n_a = len(params), len(rows), len(auxs)
    n_buf = int(into is not None and into[1] is not None)

    def body(*refs):
        p = [r[...] for r in refs[:n_p]]
        r_ = [r[...] for r in refs[n_p:n_p + n_r]]
        a_ = [r[...] for r in refs[n_p + n_r:n_p + n_r + n_a]]
        for o_ref, o in zip(refs[n_p + n_r + n_a + n_buf:], fn(p, r_, a_)):
            o_ref[...] = o.astype(o_ref.dtype)

    out_specs, out_shape = [], []
    for w, dt, per_group in outs:
        out_specs.append(_piece_spec(tm, (None, w, 0, per_group)))
        out_shape.append(jax.ShapeDtypeStruct((S, w * (groups if per_group else 1)), dt))
    bufs, buf_specs, aliases = _place_into(into, S, tm, out_specs, out_shape, n_p + n_r + n_a)
    return pl.pallas_call(
        body, name=name, grid=(S // tm, groups),
        in_specs=[_const_spec(p) for p in params] + [_piece_spec(tm, q) for q in list(rows) + list(auxs)] + buf_specs,
        out_specs=out_specs, out_shape=out_shape, input_output_aliases=aliases, compiler_params=_params(),
    )(*params, *[q[0] for q in list(rows) + list(auxs)], *bufs)


def _rowwise_vjp(name, fn, params, rows, auxs, cots, d_outs, tm, groups=1, adds=None, into=None):
    S = rows[0][0].shape[0]
    tm = min(tm, S)
    n_p, n_r, n_a = len(params), len(rows), len(auxs)
    cot_flat = [q for c in cots for q in c]
    adds = adds or [None] * len(d_outs)
    add_flat = [q for q in adds if q is not None]
    n_c, n_add = len(cot_flat), len(add_flat)
    n_buf = int(into is not None and into[1] is not None)
    shared = [not all(rows[k][3] for k in idx) and groups > 1 for idx, _ in d_outs]

    def body(*refs):
        pos = 0
        p = [r[...] for r in refs[pos:pos + n_p]]; pos += n_p
        r_ = [r[...] for r in refs[pos:pos + n_r]]; pos += n_r
        a_ = [r[...] for r in refs[pos:pos + n_a]]; pos += n_a
        c_refs = refs[pos:pos + n_c]; pos += n_c
        add_refs = list(refs[pos:pos + n_add]); pos += n_add + n_buf
        d_refs = refs[pos:pos + len(d_outs)]; pos += len(d_outs)
        dp_refs = refs[pos:]
        i, g = pl.program_id(0), pl.program_id(1)
        outs, vjp_fn = jax.vjp(lambda pp, rr: fn(pp, rr, a_), p, r_)
        cts, ci = [], 0
        for c, o in zip(cots, outs):
            t = c_refs[ci][...].astype(F32)
            for extra in c_refs[ci + 1:ci + len(c)]:
                t = t + extra[...].astype(F32)
            ci += len(c)
            cts.append(t.astype(o.dtype))
        dp, dr = vjp_fn(cts)
        for (idx, _), d_ref, add, sh in zip(d_outs, d_refs, adds, shared):
            val = dr[idx[0]].astype(F32) if len(idx) == 1 else jnp.concatenate([dr[k].astype(F32) for k in idx], axis=1)
            if add is not None:
                val = val + add_refs.pop(0)[...].astype(F32)
            if sh:
                @pl.when(g == 0)
                def _(d_ref=d_ref):
                    d_ref[...] = jnp.zeros_like(d_ref)
                d_ref[...] += val.astype(d_ref.dtype)
            else:
                d_ref[...] = val.astype(d_ref.dtype)
        first = jnp.logical_and(i == 0, g == 0)
        for dp_ref, d in zip(dp_refs, dp):
            @pl.when(first)
            def _(dp_ref=dp_ref):
                dp_ref[...] = jnp.zeros_like(dp_ref)
            dp_ref[...] += d.astype(F32)

    out_specs, out_shape = [], []
    for (idx, dt), sh in zip(d_outs, shared):
        w = sum(rows[k][1] for k in idx)
        per_group = (not sh) and groups > 1
        out_specs.append(_piece_spec(tm, (None, w, 0, per_group)))
        out_shape.append(jax.ShapeDtypeStruct((S, w * (groups if per_group else 1)), dt))
    for p in params:
        out_specs.append(_const_spec(p))
        out_shape.append(jax.ShapeDtypeStruct(p.shape, F32))
    pieces = list(rows) + list(auxs) + cot_flat + add_flat
    bufs, buf_specs, aliases = _place_into(into, S, tm, out_specs, out_shape, n_p + len(pieces))
    res = pl.pallas_call(
        body, name=name, grid=(S // tm, groups),
        in_specs=[_const_spec(p) for p in params] + [_piece_spec(tm, q) for q in pieces] + buf_specs,
        out_specs=out_specs, out_shape=out_shape, input_output_aliases=aliases, compiler_params=_params(),
    )(*params, *[q[0] for q in pieces], *bufs)
    return list(res[:len(d_outs)]), list(res[len(d_outs):])


def _lane_roll(x, shift):
    @jax.custom_vjp
    def roll(v):
        return pltpu.roll(v, shift, 1)

    roll.defvjp(lambda v: (roll(v), None), lambda _, ct: (pltpu.roll(ct, LANES - shift, 1),))
    return roll(x)


@jax.custom_vjp
def _sigmoid(x):
    return 1.0 / (1.0 + jnp.exp(-x))


def _sigmoid_fwd(x):
    s = _sigmoid(x)
    return s, s


_sigmoid.defvjp(_sigmoid_fwd, lambda s, ct: (ct * s * (1.0 - s),))


def _rope(x, cos, sin_lo, sin_hi, half):
    return x * cos + _lane_roll(x, LANES - half) * sin_lo + _lane_roll(x, half) * sin_hi


def _f_rope_table(p, r, a):
    inv, first, second, fixed = p
    ang = a[0] * inv
    cs, sn = jnp.cos(ang), jnp.sin(ang)
    return [cs * (first + second) + fixed, -sn * first, sn * second]


def _f_rms(p, r, a):
    x = r[0].astype(F32)
    return [x * lax.rsqrt(jnp.mean(x * x, axis=-1, keepdims=True) + EPS) * p[0]]


def _f_mla_a(p, r, a):
    return _f_rms([p[0]], [r[0]], a) + _f_rms([p[1]], [r[1]], a)


def _f_mla_b(p, r, a):
    def norm_rope(v, g):
        ms = jnp.sum(v * v, axis=-1, keepdims=True) * (1.0 / MLA_QK)
        return _rope(v * lax.rsqrt(ms + EPS) * g, a[0], a[1], a[2], MLA_ROPE // 2)

    return [norm_rope(r[0].astype(F32), p[0]) * MLA_Q_SCALE, norm_rope(r[1].astype(F32) + r[2].astype(F32), p[1])]


def _f_ret_rope(p, r, a):
    q = _rope(r[0].astype(F32), a[0], a[1], a[2], RET_QK // 2)
    k = _rope(r[1].astype(F32), a[0], a[1], a[2], RET_QK // 2)
    return [q, k * (RET_QK ** -0.5)]


def _f_ret_post(p, r, a):
    ret = r[0].astype(F32) + r[1].astype(F32)
    g = r[2].astype(F32)
    normed = ret * lax.rsqrt(jnp.mean(ret * ret, axis=-1, keepdims=True) + EPS)
    return [g * _sigmoid(g) * normed]


def _f_merge(p, r, a):
    return [_sigmoid(r[0].astype(F32)) * r[2].astype(F32) + _sigmoid(r[1].astype(F32)) * r[3].astype(F32)]


def _f_swiglu(p, r, a):
    g = r[0].astype(F32)
    return [g * _sigmoid(g) * r[1].astype(F32)]


def _f_add(p, r, a):
    return [r[0].astype(F32) + r[1].astype(F32)]


def _flash_fwd(q, k, kv, shards):
    S = q.shape[0]
    tq = min(ATT_TQ, S)
    nq = S // tq
    n = len(shards)

    def body(q_ref, k_ref, v_ref, *rest):
        shard_refs, (o_ref, lse_ref), gathered = rest[:n], rest[n:n + 2], rest[n + 2:2 * n + 2]
        send_sems, recv_sems = rest[2 * n + 2:]
        h, qi = pl.program_id(0), pl.program_id(1)

        @pl.when(jnp.logical_and(h == 0, qi == 0))
        def _():
            _gather_start(_gather_copies(shard_refs, gathered, send_sems, recv_sems))

        for hh in range(hps):
            lanes = slice(hh * LANES, (hh + 1) * LANES)
            s = _dot(q_ref[:, lanes], k_ref[:, lanes], "nt")
            m = jnp.max(s, axis=-1, keepdims=True)
            p = jnp.exp2(s - m)
            l = jnp.sum(p, axis=-1, keepdims=True)
            pair = slice(hh // 2 * LANES, (hh // 2 + 1) * LANES)
            o_h = (_dot(p.astype(BF16), v_ref[:, pair]) / l * _ret_head_mask(hh)).astype(o_ref.dtype)
            if hh % 2 == 0:
                o_ref[:, pair] = o_h
            else:
                o_ref[:, pair] += o_h
            lse_ref[:, lanes] = jnp.broadcast_to(m + jnp.log2(l), (tq, LANES))

        @pl.when(jnp.logical_and(h == HEADS // hps - 1, qi == nq - 1))
        def _():
            _gather_wait(_gather_copies(shard_refs, gathered, send_sems, recv_sems))

    hps = ATT_HEADS_PER_STEP
    qs = pl.BlockSpec((tq, hps * LANES), lambda h, i: (i, h))
    vw = hps * MLA_V
    v0 = HEADS * LANES // vw
    res = pl.pallas_call(
        body, name="mla_fwd", grid=(HEADS // hps, nq),
        in_specs=[qs, pl.BlockSpec((S, hps * LANES), lambda h, i: (0, h), pipeline_mode=pl.Buffered(1)),
                  pl.BlockSpec((S, vw), lambda h, i: (0, v0 + h), pipeline_mode=pl.Buffered(1))]
        + [ANY] * n,
        out_specs=[pl.BlockSpec((tq, vw), lambda h, i: (i, h)), qs] + [ANY] * n,
        out_shape=[jax.ShapeDtypeStruct((S, HEADS * MLA_V), BF16), jax.ShapeDtypeStruct((S, HEADS * LANES), F32)]
        + [jax.ShapeDtypeStruct((4,) + s.shape, s.dtype) for s in shards],
        scratch_shapes=[pltpu.SemaphoreType.DMA((3 * n,)), pltpu.SemaphoreType.DMA((3 * n,))],
        compiler_params=_params(),
    )(q, k, kv, *shards)
    mine = 2 * lax.axis_index("x") + lax.axis_index("y")
    return res[0], res[1], [_fill_slot(g, s, mine) for g, s in zip(res[2:], shards)]


def _flash_bwd(q, k, kv, do, lse, o, gs):
    S = q.shape[0]
    tq, tk = min(ATT_BQ, S), min(ATT_BK, S)
    nq, nkt = S // tq, S // tk
    n = len(gs)

    def body(q_ref, k_ref, v_ref, do_ref, lse_ref, o_ref, *rest):
        g_refs, (dq_ref, dk_ref, dv_ref), got_refs = rest[:n], rest[n:n + 3], rest[n + 3:2 * n + 3]
        dk_sc, dv_sc, send_sems, recv_sems = rest[2 * n + 3:]
        h, ki, qi = pl.program_id(0), pl.program_id(1), pl.program_id(2)

        @pl.when(jnp.logical_and(h == 0, jnp.logical_and(ki == 0, qi == 0)))
        def _():
            _scatter_start(_scatter_copies(g_refs, got_refs, send_sems, recv_sems))

        @pl.when(jnp.logical_and(ki == 0, qi == 0))
        def _():
            dq_ref[...] = jnp.zeros_like(dq_ref)

        @pl.when(qi == 0)
        def _():
            dk_sc[...] = jnp.zeros_like(dk_sc)
            dv_sc[...] = jnp.zeros_like(dv_sc)

        rows = pl.ds(pl.multiple_of(qi * tq, tq), tq)
        for hh in range(hps):
            lanes = slice(hh * LANES, (hh + 1) * LANES)
            pair = slice(hh // 2 * LANES, (hh // 2 + 1) * LANES)
            qv, kv_ = q_ref[:, lanes], k_ref[:, lanes]
            do32 = do_ref[:, pair].astype(F32) * _ret_head_mask(hh)
            dov = do32.astype(BF16)
            p = jnp.exp2(_dot(qv, kv_, "nt") - lse_ref[:, lanes][:, :1])
            dp = _dot(dov, v_ref[:, pair], "nt")
            delta = jnp.sum(do32 * o_ref[:, pair].astype(F32), axis=-1, keepdims=True)
            ds = (p * (dp - delta) * LN2).astype(BF16)
            dv_sc[:, pair] += _dot(p.astype(BF16), dov, "tn")
            dk_sc[:, lanes] += _dot(ds, qv, "tn")
            dq_ref[rows, lanes] += _dot(ds, kv_)

        @pl.when(qi == nq - 1)
        def _():
            dk_ref[...] = dk_sc[...].astype(dk_ref.dtype)
            dv_ref[...] = dv_sc[...].astype(dv_ref.dtype)

        @pl.when(jnp.logical_and(h == HEADS // hps - 1, jnp.logical_and(ki == nkt - 1, qi == nq - 1)))
        def _():
            _scatter_wait(_scatter_copies(g_refs, got_refs, send_sems, recv_sems))

    hps = ATT_BWD_HEADS_PER_STEP
    qs = pl.BlockSpec((tq, hps * LANES), lambda h, j, i: (i, h))
    ks = pl.BlockSpec((tk, hps * LANES), lambda h, j, i: (j, h))
    vw = hps * MLA_V
    v0 = HEADS * LANES // vw
    qv_s = pl.BlockSpec((tq, vw), lambda h, j, i: (i, h))
    kv_s = pl.BlockSpec((tk, vw), lambda h, j, i: (j, h))
    res = pl.pallas_call(
        body, name="mla_bwd", grid=(HEADS // hps, nkt, nq),
        in_specs=[qs, ks, pl.BlockSpec((tk, vw), lambda h, j, i: (j, v0 + h)), qv_s, qs, qv_s] + [ANY] * n,
        out_specs=[pl.BlockSpec((S, hps * LANES), lambda h, j, i: (0, h), pipeline_mode=pl.Buffered(1)), ks, kv_s] + [ANY] * n,
        out_shape=[jax.ShapeDtypeStruct((S, HEADS * LANES), F32), jax.ShapeDtypeStruct((S, HEADS * LANES), BF16),
                   jax.ShapeDtypeStruct((S, HEADS * MLA_V), BF16)]
        + [jax.ShapeDtypeStruct((8,) + g.shape[2:], g.dtype) for g in gs],
        scratch_shapes=[pltpu.VMEM((tk, hps * LANES), F32), pltpu.VMEM((tk, vw), F32)]
        + [pltpu.SemaphoreType.DMA((8 * n,)), pltpu.SemaphoreType.DMA((7 * n,))],
        compiler_params=_params(),
    )(q, k, kv, do, lse, o, *gs)
    return res[0], res[1], res[2], list(res[3:])


def _ret_tables(decay_row, backward):
    C = RET_CHUNK
    lg = -jnp.exp(decay_row)
    t = lax.broadcasted_iota(jnp.int32, (C, C), 0).astype(F32)
    s = lax.broadcasted_iota(jnp.int32, (C, C), 1).astype(F32)
    ridx = lax.broadcasted_iota(jnp.int32, (C, LANES), 0).astype(F32)
    if backward:
        dist, mask, aw, bw = s - t, s > t, C - ridx, ridx
    else:
        dist, mask, aw, bw = t - s, t >= s, ridx + 1.0, C - 1.0 - ridx
    dist = jnp.maximum(dist, 0.0)
    din = jnp.where(mask, jnp.exp(lg[:, :1] * dist), 0.0)
    return dict(din=din, dist=dist, a=jnp.exp(lg * aw), b=jnp.exp(lg * bw), c=jnp.exp(lg * C), aw=aw, bw=bw)


def _ret_fill_tables(decs, din_sc, a_sc, b_sc):
    for d, dec in enumerate(decs):
        for h in range(HEADS):
            tb = _ret_tables(dec[h:h + 1, :], d == 1)
            din_sc[d, h], a_sc[d, h], b_sc[d, h] = tb["din"], tb["a"], tb["b"]


def _ret_head_mask(h):
    lane = lax.broadcasted_iota(jnp.int32, (1, LANES), 1)
    return jnp.where((lane >= RET_QK) == bool(h % 2), 1.0, 0.0).astype(F32)


def _ret_fwd(qr, kr, proj, v_block, dec_f, dec_b):
    S = qr.shape[0]
    C = RET_CHUNK
    n = S // C
    nc = min(RET_CHUNKS_PER_STEP, n)
    nb = n // nc
    W = HEADS * LANES

    def body(qf, kf, vf, qb, kb, vb, df, db, of, ob, sf_out, sb_out, st, din_sc, a_sc, b_sc):
        @pl.when(pl.program_id(0) == 0)
        def _():
            st[...] = jnp.zeros_like(st)
            _ret_fill_tables((df, db), din_sc, a_sc, b_sc)

        for d, (q_ref, k_ref, v_ref, dec, o_ref, s_out) in enumerate(
                [(qf, kf, vf, df, of, sf_out), (qb, kb, vb, db, ob, sb_out)]):
            for h in range(HEADS):
                lanes, pair = slice(h * LANES, (h + 1) * LANES), slice(h // 2 * LANES, (h // 2 + 1) * LANES)
                mine = _ret_head_mask(h)
                din, a, b = din_sc[d, h], a_sc[d, h], b_sc[d, h]
                c = jnp.exp(-jnp.exp(dec[h:h + 1, :]) * C)
                for ci in (range(nc) if d == 0 else reversed(range(nc))):
                    rows = slice(ci * C, (ci + 1) * C)
                    qf32, kf32 = q_ref[rows, pair].astype(F32) * mine, k_ref[rows, pair].astype(F32) * mine
                    v = v_ref[rows, lanes]
                    state = st[d, h]
                    s_out[ci, h] = state
                    inner = _dot((_dot(qf32.astype(BF16), kf32.astype(BF16), "nt") * din).astype(BF16), v)
                    cross = _dot((qf32 * a).astype(BF16), state.astype(BF16))
                    o_ref[rows, lanes] = inner + cross
                    st[d, h] = state * c + _dot((kf32 * b).astype(BF16), v, "tn")

    fw = lambda c0, w=W: pl.BlockSpec((nc * C, w), lambda j: (j, c0))
    bw = lambda c0, w=W: pl.BlockSpec((nc * C, w), lambda j: (nb - 1 - j, c0))
    dec_spec = pl.BlockSpec((HEADS, LANES), lambda j: (0, 0))
    st_shape = jax.ShapeDtypeStruct((n, HEADS, LANES, LANES), F32)
    QW = W // 2
    return pl.pallas_call(
        body, name="ret_fwd", grid=(nb,),
        in_specs=[fw(0, QW), fw(0, QW), fw(v_block), bw(0, QW), bw(0, QW), bw(v_block), dec_spec, dec_spec],
        out_specs=[fw(0), bw(0), pl.BlockSpec((nc, HEADS, LANES, LANES), lambda j: (j, 0, 0, 0)),
                   pl.BlockSpec((nc, HEADS, LANES, LANES), lambda j: (nb - 1 - j, 0, 0, 0))],
        out_shape=[jax.ShapeDtypeStruct((S, W), F32)] * 2 + [st_shape] * 2,
        scratch_shapes=[pltpu.VMEM((2, HEADS, LANES, LANES), F32), pltpu.VMEM((2, HEADS, C, C), F32),
                        pltpu.VMEM((2, HEADS, C, LANES), F32), pltpu.VMEM((2, HEADS, C, LANES), F32)],
        compiler_params=_params(),
    )(qr, kr, proj, qr, kr, proj, dec_f, dec_b)


def _ret_bwd(qr, kr, proj, v_block, dret, sf, sb, dec_f, dec_b):
    S = qr.shape[0]
    C = RET_CHUNK
    n = S // C
    nc = min(RET_CHUNKS_PER_STEP, n)
    nb = n // nc
    W = HEADS * LANES

    def body(qf, kf, vf, gf, sf_ref, qb, kb, vb, gb, sb_ref, df, db,
             dqf, dkf, dvf, dqb, dkb, dvb, ddf, ddb, ds_sc, din_sc, a_sc, b_sc):
        j = pl.program_id(0)

        @pl.when(j == 0)
        def _():
            ds_sc[...] = jnp.zeros_like(ds_sc)
            ddf[...] = jnp.zeros_like(ddf)
            ddb[...] = jnp.zeros_like(ddb)
            _ret_fill_tables((df, db), din_sc, a_sc, b_sc)

        for d, (q_ref, k_ref, v_ref, g_ref, s_ref, dec, dq_ref, dk_ref, dv_ref, dd_ref) in enumerate(
                [(qf, kf, vf, gf, sf_ref, df, dqf, dkf, dvf, ddf), (qb, kb, vb, gb, sb_ref, db, dqb, dkb, dvb, ddb)]):
            static = _ret_tables(dec[0:1, :], d == 1)
            dist, aw, bw_ = static["dist"], static["aw"], static["bw"]
            for h in range(HEADS):
                lanes, pair = slice(h * LANES, (h + 1) * LANES), slice(h // 2 * LANES, (h // 2 + 1) * LANES)
                mine = _ret_head_mask(h)
                din, a, b = din_sc[d, h], a_sc[d, h], b_sc[d, h]
                c = jnp.exp(-jnp.exp(dec[h:h + 1, :]) * C)
                dlg = jnp.zeros((1, 1), F32)
                for ci in (reversed(range(nc)) if d == 0 else range(nc)):
                    rows = slice(ci * C, (ci + 1) * C)
                    v, g = v_ref[rows, lanes], g_ref[rows, lanes]
                    qf32, kf32 = q_ref[rows, pair].astype(F32) * mine, k_ref[rows, pair].astype(F32) * mine
                    q, k = qf32.astype(BF16), kf32.astype(BF16)
                    state, dstate = s_ref[ci, h], ds_sc[d, h]
                    dstate_b = dstate.astype(BF16)
                    dp = _dot(g, v, "nt")
                    a_ = _dot(q, k, "nt")
                    da = (dp * din).astype(BF16)
                    g1 = _dot(g, state.astype(BF16), "nt")
                    g2 = _dot(v, dstate_b, "nt")
                    dq_h = (_dot(da, k) + g1 * a).astype(dq_ref.dtype)
                    dk_h = (_dot(da, q, "tn") + g2 * b).astype(dk_ref.dtype)
                    if h % 2 == 0:
                        dq_ref[rows, pair], dk_ref[rows, pair] = dq_h, dk_h
                    else:
                        dq_ref[rows, pair] += dq_h
                        dk_ref[rows, pair] += dk_h
                    dv_ref[rows, lanes] = (_dot((a_ * din).astype(BF16), g, "tn")
                                           + _dot((kf32 * b).astype(BF16), dstate_b)).astype(dv_ref.dtype)
                    dlg = dlg + (jnp.sum(dp * a_ * din * dist, keepdims=True)
                                 + jnp.sum(g1 * qf32 * a * aw, keepdims=True)
                                 + jnp.sum(g2 * kf32 * b * bw_, keepdims=True)
                                 + C * jnp.sum(c * dstate * state, keepdims=True))
                    ds_sc[d, h] = dstate * c + _dot((qf32 * a).astype(BF16), g, "tn")
                dd_ref[h:h + 1, :] += jnp.broadcast_to(dlg, (1, LANES))

        @pl.when(j == nb - 1)
        def _():
            ddf[...] = ddf[...] * -jnp.exp(df[...])
            ddb[...] = ddb[...] * -jnp.exp(db[...])

    fw = lambda c0, w=W: pl.BlockSpec((nc * C, w), lambda j: (nb - 1 - j, c0))
    bw = lambda c0, w=W: pl.BlockSpec((nc * C, w), lambda j: (j, c0))
    dec_spec = pl.BlockSpec((HEADS, LANES), lambda j: (0, 0))
    QW = W // 2
    act, act_qk = jax.ShapeDtypeStruct((S, W), BF16), jax.ShapeDtypeStruct((S, QW), BF16)
    return pl.pallas_call(
        body, name="ret_bwd", grid=(nb,),
        in_specs=[fw(0, QW), fw(0, QW), fw(v_block), fw(0),
                  pl.BlockSpec((nc, HEADS, LANES, LANES), lambda j: (nb - 1 - j, 0, 0, 0)),
                  bw(0, QW), bw(0, QW), bw(v_block), bw(0), pl.BlockSpec((nc, HEADS, LANES, LANES), lambda j: (j, 0, 0, 0)),
                  dec_spec, dec_spec],
        out_specs=[fw(0, QW), fw(0, QW), fw(0), bw(0, QW), bw(0, QW), bw(0)] + [dec_spec] * 2,
        out_shape=[act_qk, act_qk, act, act_qk, act_qk, act] + [jax.ShapeDtypeStruct((HEADS, LANES), F32)] * 2,
        scratch_shapes=[pltpu.VMEM((2, HEADS, LANES, LANES), F32), pltpu.VMEM((2, HEADS, C, C), F32),
                        pltpu.VMEM((2, HEADS, C, LANES), F32), pltpu.VMEM((2, HEADS, C, LANES), F32)],
        compiler_params=_params(),
    )(qr, kr, proj, dret, sf, qr, kr, proj, dret, sb, dec_f, dec_b)


def _pad_heads(w, hd):
    K = w.shape[0]
    return jnp.pad(w.reshape(K, HEADS, hd), ((0, 0), (0, 0), (0, LANES - hd))).reshape(K, HEADS * LANES)


def _unpad_heads(w, hd):
    K = w.shape[0]
    return w.reshape(K, HEADS, LANES)[:, :, :hd].reshape(K, HEADS * hd)


def _rope_consts(first_lane, half, period=LANES):
    lane = np.arange(LANES) % period
    first = ((lane >= first_lane) & (lane < first_lane + half)).astype(np.float32)
    second = ((lane >= first_lane + half) & (lane < first_lane + 2 * half)).astype(np.float32)
    fixed = (lane < first_lane).astype(np.float32)
    j = np.where(first > 0, lane - first_lane, lane - first_lane - half) * (first + second)
    inv = (ROPE_THETA ** (-j.astype(np.float64) / half)).astype(np.float32)
    return [jnp.asarray(v.reshape(1, LANES), F32) for v in (inv, first, second, fixed)]


def _assemble(name, gathered):
    if name in COL_SHARDED:
        return jnp.transpose(gathered, (1, 0, 2)).reshape(gathered.shape[1], 4 * gathered.shape[2])
    return gathered.reshape(4 * gathered.shape[1], gathered.shape[2])


def _split_for_reducers(name, g, dtype):
    if name in COL_SHARDED:
        K, N4 = g.shape
        return jnp.transpose(g.reshape(2, K // 2, 4, N4 // 4), (2, 0, 1, 3)).astype(dtype)
    return g.reshape(4, 2, g.shape[0] // 8, g.shape[1]).astype(dtype)


def _local_step(x, tab_m, tab_r, tgt, wts, late_shards, small):
    w_in = wts["w_in"]
    seg = [w_in[:, IN_OFFS[i]:IN_OFFS[i + 1]] for i in range(8)]
    kr_w = jnp.pad(seg[2], ((0, 0), (MLA_NOPE, LANES - MLA_QK)))
    w_in_p = jnp.concatenate([seg[7], seg[5], seg[6], seg[3], seg[4], seg[0], seg[1], kr_w], axis=1)
    QR0, KR0, CQ0 = 4096, 4608, 5120
    w_qb_p = _pad_heads(wts["w_q_b"], MLA_QK)
    kvw = wts["w_kv_b"].reshape(MLA_KV_RANK, HEADS, MLA_NOPE + MLA_V)
    pad_kv = lambda t: jnp.pad(t, ((0, 0), (0, 0), (0, LANES - t.shape[2]))).reshape(MLA_KV_RANK, HEADS * LANES)
    w_kn_p, w_v = pad_kv(kvw[:, :, :MLA_NOPE]), kvw[:, :, MLA_NOPE:].reshape(MLA_KV_RANK, HEADS * MLA_V)
    w_kv_p = jnp.concatenate([w_kn_p, w_v], axis=1)
    g_qn_p = jnp.pad(small["g_qn"], ((0, 0), (0, LANES - MLA_QK)))
    g_kn_p = jnp.pad(small["g_kn"], ((0, 0), (0, LANES - MLA_QK)))
    dec_f = jnp.broadcast_to(small["ret_decay_fwd"].reshape(HEADS, 1), (HEADS, LANES))
    dec_b = jnp.broadcast_to(small["ret_decay_bwd"].reshape(HEADS, 1), (HEADS, LANES))
    T, N = True, False
    RT, HT = ROW_TILE, HEAD_ROW_TILE
    RW = 2 * ROW_TILE

    aux_m = [(t, LANES, 0, N) for t in tab_m]
    aux_r = [(t, LANES, 0, N) for t in tab_r]

    proj, h = _mm("proj", x, w_in_p, "nn", BF16, a_gain=small["g_mix"])
    rows_a = [(proj, MLA_Q_RANK, CQ0 // MLA_Q_RANK, N), (proj, MLA_KV_RANK, (CQ0 + MLA_Q_RANK) // MLA_KV_RANK, N)]
    cqn, ckvn = _rowwise("mla_lat_norm", _f_mla_a, [small["g_q_a"], small["g_kv_a"]], rows_a, [],
                         [(MLA_Q_RANK, BF16, N), (MLA_KV_RANK, BF16, N)], RW)
    qraw = _mm("mla_q_up", cqn, w_qb_p, "nn", BF16)
    kv = _mm("mla_kv_up", ckvn, w_kv_p, "nn", BF16)
    rows_b = [(qraw, LANES, 0, T), (kv, LANES, 0, T), (proj, LANES, (CQ0 + MLA_Q_RANK + MLA_KV_RANK) // LANES, N)]
    q, k = _rowwise("mla_qk_norm_rope", _f_mla_b, [g_qn_p, g_kn_p], rows_b, aux_m, [(LANES, BF16, T)] * 2, HT, HEADS)
    o, lse, late = _flash_fwd(q, k, kv, [late_shards[n] for n in LATE])
    wl = {n: _assemble(n, g) for n, g in zip(LATE, late)}
    w_mla_p = wl["w_mla_out"]
    w_ret_out, w_out, w_gu, w_down = wl["w_ret_out"], wl["w_out"], wl["w_gate_up"], wl["w_down"]
    y_a = _mm("mla_out", o, w_mla_p, "nn", BF16)
    rows_rr = [(proj, LANES, QR0 // LANES, T), (proj, LANES, KR0 // LANES, T)]
    qr, kr = _rowwise("ret_rope", _f_ret_rope, [], rows_rr, aux_r, [(LANES, RET_QK_DTYPE, T)] * 2, HT, HEADS // 2)
    ret_f, ret_b, st_f, st_b = _ret_fwd(qr, kr, proj, 2, dec_f, dec_b)
    rows_rp = [(ret_f, LANES, 0, T), (ret_b, LANES, 0, T), (proj, LANES, 24, T)]
    (o_b,) = _rowwise("ret_post", _f_ret_post, [], rows_rp, [], [(LANES, BF16, T)], HT, HEADS)
    y_b = _mm("ret_out", o_b, w_ret_out, "nn", BF16)
    rows_m = [(proj, D_MODEL, 0, N), (proj, D_MODEL, 1, N), (y_a, D_MODEL, 0, N), (y_b, D_MODEL, 0, N)]
    (merged,) = _rowwise("merge", _f_merge, [], rows_m, [], [(D_MODEL, BF16, N)], RW)
    x2 = _mm("mix_out", merged, w_out, "nn", F32, res=x)
    gu, h2 = _mm("ffn_gate_up", x2, w_gu, "nn", BF16, a_gain=small["g_ffn"])
    rows_sw = [(gu, FFN_HIDDEN, 0, N), (gu, FFN_HIDDEN, 1, N)]
    (act,) = _rowwise("swiglu", _f_swiglu, [], rows_sw, [], [(FFN_HIDDEN, BF16, N)], RT)
    dy, dy_b16, loss_row = _mm("ffn_down", act, w_down, "nn", None,
                               epilogue=(_epi_loss, [x2, tgt], [], [F32, BF16], [(1, LANES)]))

    dact = _mm("d_act", dy_b16, w_down, "nt", BF16)
    dw_down = _mm("dw_down", act, dy_b16, "tn", BF16)
    (dgu,), _ = _rowwise_vjp("swiglu_bwd", _f_swiglu, [], rows_sw, [], [[(dact, FFN_HIDDEN, 0, N)]], [([0, 1], BF16)], RT)
    dx2, dx2_b16, dg_ffn = _mm("d_h2", dgu, w_gu, "nt", None,
                               epilogue=(_epi_rms_bwd(2), [x2, dy], [small["g_ffn"]], [F32, BF16], [(1, D_MODEL)]))
    dw_gu = _mm("dw_gate_up", h2, dgu, "tn", BF16, shard_out=True)
    dmerged = _mm("d_merged", dx2_b16, w_out, "nt", BF16)
    dw_out = _mm("dw_out", merged, dx2_b16, "tn", BF16)
    PW = w_in_p.shape[1]
    (dproj, dy_a, dy_b), _ = _rowwise_vjp("merge_bwd", _f_merge, [], rows_m, [], [[(dmerged, D_MODEL, 0, N)]],
                                          [([0, 1], BF16), ([2], BF16), ([3], BF16)], RW, into=(0, None, PW, 0))
    do_b = _mm("d_ret_o", dy_b, w_ret_out, "nt", BF16)
    dw_ret_out = _mm("dw_ret_out", o_b, dy_b, "tn", BF16)
    (dret, dproj), _ = _rowwise_vjp("ret_post_bwd", _f_ret_post, [], rows_rp, [], [[(do_b, LANES, 0, T)]],
                                    [([0], BF16), ([2], BF16)], HT, HEADS, into=(1, dproj, PW, 24))
    dqf, dkf, dvf, dqb, dkb, dvb, ddec_f, ddec_b = _ret_bwd(qr, kr, proj, 2, dret, st_f, st_b, dec_f, dec_b)
    (dq_r, dk_r), _ = _rowwise_vjp("ret_rope_bwd", _f_ret_rope, [], rows_rr, aux_r,
                                   [[(dqf, LANES, 0, T), (dqb, LANES, 0, T)], [(dkf, LANES, 0, T), (dkb, LANES, 0, T)]],
                                   [([0], BF16), ([1], BF16)], HT, HEADS // 2)
    (dproj,) = _rowwise("ret_dv_sum", _f_add, [], [(dvf, D_MODEL, 0, N), (dvb, D_MODEL, 0, N)], [], [(D_MODEL, BF16, N)], RW,
                        into=(0, dproj, PW, 2))
    do = _mm("d_mla_o", dy_a, w_mla_p, "nt", BF16)
    dw_mla = _mm("dw_mla_out", o, dy_a, "tn", BF16)
    late_grads = {"w_mla_out": dw_mla, "w_ret_out": dw_ret_out, "w_out": dw_out, "w_down": dw_down}
    late_gs = [dw_gu if n == "w_gate_up" else _split_for_reducers(n, late_grads[n], BF16) for n in LATE]
    dq, dk, dv, late_got = _flash_bwd(q, k, kv, do, lse, o, late_gs)
    (dqraw, dkn, dkr), (dg_qn_p, dg_kn_p) = _rowwise_vjp(
        "mla_qk_norm_rope_bwd", _f_mla_b, [g_qn_p, g_kn_p], rows_b, aux_m, [[(dq, LANES, 0, T)], [(dk, LANES, 0, T)]],
        [([0], BF16), ([1], BF16), ([2], F32)], HT, HEADS)
    dckvn = _mm("d_ckvn_v", dv, w_v, "nt", BF16, res=_mm("d_ckvn_k", dkn, w_kn_p, "nt", F32))
    dw_kn_p = _mm("dw_kv_k", ckvn, dkn, "tn", BF16)
    dw_v = _mm("dw_kv_v", ckvn, dv, "tn", BF16)
    dcqn = _mm("d_cqn", dqraw, w_qb_p, "nt", BF16)
    dw_qb_p = _mm("dw_q_b", cqn, dqraw, "tn", BF16)
    (dcq, dckv), (dg_q_a, dg_kv_a) = _rowwise_vjp(
        "mla_lat_norm_bwd", _f_mla_a, [small["g_q_a"], small["g_kv_a"]], rows_a, [],
        [[(dcqn, MLA_Q_RANK, 0, N)], [(dckvn, MLA_KV_RANK, 0, N)]], [([0], BF16), ([1], BF16)], RW)
    dproj = lax.dynamic_update_slice(dproj, jnp.concatenate([dq_r, dk_r, dcq, dckv, dkr.astype(BF16)], axis=1), (0, QR0))
    dw_in_p = _mm("dw_in", h, dproj, "tn", BF16)

    c = lambda a, b_: dw_in_p[:, a:b_]
    kr0 = CQ0 + MLA_Q_RANK + MLA_KV_RANK
    dw_in = jnp.concatenate([c(CQ0, CQ0 + MLA_Q_RANK), c(CQ0 + MLA_Q_RANK, kr0), c(kr0 + MLA_NOPE, kr0 + MLA_QK), c(QR0, KR0),
                             c(KR0, CQ0), c(2048, 3072), c(3072, 4096), c(0, 2048)], axis=1)
    dw_kn = dw_kn_p.reshape(MLA_KV_RANK, HEADS, LANES)[:, :, :MLA_NOPE]
    dw_kv = jnp.concatenate([dw_kn, dw_v.reshape(MLA_KV_RANK, HEADS, MLA_V)], axis=2).reshape(MLA_KV_RANK, HEADS * (MLA_NOPE + MLA_V))
    grads = {"w_in": dw_in, "w_q_b": _unpad_heads(dw_qb_p, MLA_QK), "w_kv_b": dw_kv}
    dx, dg_mix, first_got = _mm("d_h", dproj, w_in_p, "nt", None,
                                epilogue=(_epi_rms_bwd(1), [x, dx2], [small["g_mix"]], [F32], [(1, D_MODEL)]),
                                scatter=[_split_for_reducers(n, grads[n], BF16) for n in FIRST])
    sgrads = {"g_mix": dg_mix, "g_q_a": dg_q_a, "g_kv_a": dg_kv_a, "g_qn": dg_qn_p[:, :MLA_QK], "g_kn": dg_kn_p[:, :MLA_QK],
              "ret_decay_fwd": ddec_f[:, 0].reshape(1, HEADS), "ret_decay_bwd": ddec_b[:, 0].reshape(1, HEADS), "g_ffn": dg_ffn}
    return loss_row, dx, first_got + late_got, sgrads


def _coords():
    return lax.axis_index("x"), lax.axis_index("y"), lax.axis_index("c")


def _other_chips(x, y):
    return [(1 - x, y), (x, 1 - y), (1 - x, 1 - y)]


ANY = pl.BlockSpec(memory_space=pl.ANY)


def _gather_copies(ins, outs, send_sems, recv_sems):
    x, y, c = _coords()
    mine = 2 * x + y
    sends, arrivals = [], []
    for w in range(len(ins)):
        for j, (cx, cy) in enumerate(_other_chips(x, y)):
            sems = dict(send_sem=send_sems.at[3 * w + j], recv_sem=recv_sems.at[3 * w + j],
                        device_id=(cx, cy, c), device_id_type=MESH)
            sends.append(pltpu.make_async_remote_copy(src_ref=ins[w], dst_ref=outs[w].at[mine], **sems))
            arrivals.append(functools.partial(pltpu.make_async_remote_copy, src_ref=ins[w],
                                              dst_ref=outs[w].at[2 * cx + cy], **sems))
    return sends, arrivals


def _gather_start(copies):
    for cp in list(copies[0]) + list(copies[2] if len(copies) > 2 else []):
        cp.start()


def _gather_wait(copies):
    sends, arrivals = copies[:2]
    for make in arrivals:
        make().wait_recv()
    for cp in sends:
        cp.wait_send()
    for cp in (copies[2] if len(copies) > 2 else []):
        cp.wait()


def _fill_slot(buf, piece, slot):
    idx = lax.broadcasted_iota(jnp.int32, (buf.shape[0],) + (1,) * piece.ndim, 0)
    return jnp.where(idx == slot, piece[None], buf)


def _rope_tables_and_first_gather(pos, consts_mla, consts_ret, shards):
    S = pos.shape[0]
    tm = min(HEAD_ROW_TILE, S)
    nt = S // tm
    n = len(shards)

    def body(pos_ref, *refs):
        consts, ins = (refs[:4], refs[4:8]), refs[8:8 + n]
        tabs, outs = refs[8 + n:14 + n], refs[14 + n:14 + 2 * n]
        send_sems, recv_sems = refs[14 + 2 * n:]
        i = pl.program_id(0)
        x, y, c = _coords()
        chips = _other_chips(x, y)
        mine = 2 * x + y

        def half(ref, slot, core):
            rows = ref.shape[1] // 2
            return ref.at[slot, pl.ds(pl.multiple_of(core * rows, 8), rows)]

        def copy(w, k, slot, core, to, src=None):
            return pltpu.make_async_remote_copy(
                src_ref=half(outs[w], slot, core) if src is None else src, dst_ref=half(outs[w], slot, core),
                send_sem=send_sems.at[6 * w + k], recv_sem=recv_sems.at[6 * w + k], device_id=to, device_id_type=MESH)

        def first(w, j):
            rows = ins[w].shape[0] // 2
            return copy(w, j, mine, c, (*chips[j], c), src=ins[w].at[pl.ds(pl.multiple_of(c * rows, 8), rows)])

        @pl.when(i == 0)
        def _():
            for w in range(n):
                for j in range(3):
                    first(w, j).start()

        for k in range(2):
            vals = _f_rope_table([r[...] for r in consts[k]], None, [pos_ref[...]])
            for t_ref, v in zip(tabs[3 * k:3 * k + 3], vals):
                t_ref[...] = v

        @pl.when(i == nt - 1)
        def _():
            passed = []
            for w in range(n):
                for j, (cx, cy) in enumerate(chips):
                    copy(w, j, 2 * cx + cy, c, (x, y, c)).wait_recv()
                    cp = copy(w, 3 + j, 2 * cx + cy, c, (x, y, 1 - c))
                    cp.start()
                    passed.append(cp)
            for w in range(n):
                for j, (cx, cy) in enumerate(chips):
                    copy(w, 3 + j, 2 * cx + cy, 1 - c, (x, y, c)).wait_recv()
            for w in range(n):
                for j in range(3):
                    first(w, j).wait_send()
            for cp in passed:
                cp.wait_send()

    const = lambda p: pl.BlockSpec(p.shape, lambda i: (0, 0))
    tab = pl.BlockSpec((tm, LANES), lambda i: (i, 0))
    res = pl.pallas_call(
        body, name="rope_tables_first_gather", grid=(nt,),
        in_specs=[pl.BlockSpec((tm, 1), lambda i: (i, 0))] + [const(p) for p in list(consts_mla) + list(consts_ret)] + [ANY] * n,
        out_specs=[tab] * 6 + [ANY] * n,
        out_shape=[jax.ShapeDtypeStruct((S, LANES), F32)] * 6 + [jax.ShapeDtypeStruct((4,) + s.shape, s.dtype) for s in shards],
        scratch_shapes=[pltpu.SemaphoreType.DMA((6 * n,)), pltpu.SemaphoreType.DMA((6 * n,))],
        compiler_params=_params(),
    )(pos, *consts_mla, *consts_ret, *shards)
    return list(res[:3]), list(res[3:6]), list(res[6:])


def _scatter_copies(ins, outs, send_sems, recv_sems):
    x, y, c = _coords()
    me = 4 * x + 2 * y + c
    n = len(ins)
    sends, arrivals = [], []
    local = [pltpu.make_async_copy(ins[w].at[2 * x + y, c], outs[w].at[me], send_sems.at[7 * n + w]) for w in range(n)]
    for w in range(n):
        for k in range(1, 8):
            px, py, pc = x ^ (k >> 2), y ^ ((k >> 1) & 1), c ^ (k & 1)
            sems = dict(send_sem=send_sems.at[7 * w + k - 1], recv_sem=recv_sems.at[7 * w + k - 1],
                        device_id=(px, py, pc), device_id_type=MESH)
            sends.append(pltpu.make_async_remote_copy(src_ref=ins[w].at[2 * px + py, pc], dst_ref=outs[w].at[me], **sems))
            arrivals.append(functools.partial(
                pltpu.make_async_remote_copy, src_ref=ins[w].at[2 * px + py, pc],
                dst_ref=outs[w].at[4 * px + 2 * py + pc], **sems))
    return sends, arrivals, local


_scatter_start, _scatter_wait = _gather_start, _gather_wait


def _grad_sum8(name, got):
    _, R, W = got.shape
    tr = _pick(R, 256, 16)

    def body(g_ref, o_ref):
        total = g_ref[0].astype(F32)
        for d in range(1, 8):
            total = total + g_ref[d].astype(F32)
        o_ref[...] = total

    return pl.pallas_call(
        body, name=name, grid=(R // tr,), in_specs=[pl.BlockSpec((8, tr, W), lambda i: (0, i, 0))],
        out_specs=pl.BlockSpec((tr, W), lambda i: (i, 0)), out_shape=jax.ShapeDtypeStruct((R, W), F32),
        compiler_params=_params(),
    )(got)


def _half_exchange(halves):
    n = len(halves)

    def body(*refs):
        ins, outs, send_sems, recv_sems = refs[:n], refs[n:2 * n], refs[2 * n], refs[2 * n + 1]
        x, y, c = _coords()
        sends = []
        for w in range(n):
            cp = pltpu.make_async_remote_copy(
                src_ref=ins[w], dst_ref=outs[w], send_sem=send_sems.at[w], recv_sem=recv_sems.at[w],
                device_id=(x, y, 1 - c), device_id_type=MESH)
            cp.start()
            sends.append(cp)
        for cp in sends:
            cp.wait()

    got = pl.pallas_call(
        body, name="grad_half_exchange", in_specs=[ANY] * n, out_specs=[ANY] * n,
        out_shape=[jax.ShapeDtypeStruct(h.shape, F32) for h in halves],
        scratch_shapes=[pltpu.SemaphoreType.DMA((n,)), pltpu.SemaphoreType.DMA((n,))],
    )(*halves)
    c = lax.axis_index("c")
    return [jnp.where(c == 0, jnp.stack([mine, theirs]), jnp.stack([theirs, mine])) for mine, theirs in zip(halves, got)]


def _adamw_math(w, g, m, v):
    m2 = ADAM_B1 * m + (1.0 - ADAM_B1) * g
    v2 = ADAM_B2 * v + (1.0 - ADAM_B2) * (g * g)
    m_hat = m2 / (1.0 - ADAM_B1 ** ADAM_STEP)
    v_hat = v2 / (1.0 - ADAM_B2 ** ADAM_STEP)
    return -ADAM_LR * (m_hat / (jnp.sqrt(v_hat) + ADAM_EPS) + ADAM_WD * w), m2, v2


def _small_allreduce_adamw(pack_g, pack_w, pack_m, pack_v):
    def body(g_ref, w_ref, m_ref, v_ref, sum_ref, d_ref, m_out, v_out, land, send_sems, recv_sems):
        x, y, c = _coords()
        me = 4 * x + 2 * y + c
        land[me] = g_ref[...]
        sends = []
        for k in range(1, 8):
            peer = (x ^ (k >> 2), y ^ ((k >> 1) & 1), c ^ (k & 1))
            cp = pltpu.make_async_remote_copy(
                src_ref=g_ref, dst_ref=land.at[me], send_sem=send_sems.at[k - 1], recv_sem=recv_sems.at[k - 1],
                device_id=peer, device_id_type=MESH)
            cp.start()
            sends.append((cp, peer))
        for k, (cp, peer) in enumerate(sends):
            pltpu.make_async_remote_copy(
                src_ref=g_ref, dst_ref=land.at[4 * peer[0] + 2 * peer[1] + peer[2]], send_sem=send_sems.at[k],
                recv_sem=recv_sems.at[k], device_id=peer, device_id_type=MESH).wait_recv()
        for cp, _ in sends:
            cp.wait_send()
        total = land[0]
        for d in range(1, 8):
            total = total + land[d]
        sum_ref[...] = total
        d_ref[...], m_out[...], v_out[...] = _adamw_math(w_ref[...], total, m_ref[...], v_ref[...])

    vm = pl.BlockSpec(memory_space=pltpu.VMEM)
    shp = jax.ShapeDtypeStruct(pack_g.shape, F32)
    return pl.pallas_call(
        body, name="small_allreduce_adamw", in_specs=[vm] * 4, out_specs=[vm] * 4, out_shape=[shp] * 4,
        scratch_shapes=[pltpu.VMEM((8,) + pack_g.shape, F32), pltpu.SemaphoreType.DMA((7,)), pltpu.SemaphoreType.DMA((7,))],
    )(pack_g, pack_w, pack_m, pack_v)


def _adamw(name, w, g, m, v):
    R, C = w.shape
    tr = _pick(R, 256, 8)

    def body(w_ref, g_ref, m_ref, v_ref, d_out, m_out, v_out):
        d_out[...], m_out[...], v_out[...] = _adamw_math(w_ref[...], g_ref[...], m_ref[...], v_ref[...])

    spec = pl.BlockSpec((tr, C), lambda i: (i, 0))
    return pl.pallas_call(
        body, name=name, grid=(R // tr,), in_specs=[spec] * 4, out_specs=[spec] * 3,
        out_shape=[jax.ShapeDtypeStruct((R, C), F32)] * 3, compiler_params=_params(),
    )(w, g, m, v)


def _pack_small(vals, last):
    flat = jnp.concatenate([v.reshape(-1) for v in vals] + [last.reshape(-1)])
    return jnp.pad(flat, (0, SMALL_ROWS * LANES - flat.shape[0])).reshape(SMALL_ROWS, LANES)


def kernel(x, positions, g_mix, w_in, g_q_a, w_q_b, g_kv_a, w_kv_b, g_qn, g_kn, w_mla_out, ret_decay_fwd, ret_decay_bwd, w_ret_out, w_out, g_ffn, w_gate_up, w_down, loss_target, m_g_mix, m_w_in, m_g_q_a, m_w_q_b, m_g_kv_a, m_w_kv_b, m_g_qn, m_g_kn, m_w_mla_out, m_ret_decay_fwd, m_ret_decay_bwd, m_w_ret_out, m_w_out, m_g_ffn, m_w_gate_up, m_w_down, v_g_mix, v_w_in, v_g_q_a, v_w_q_b, v_g_kv_a, v_w_kv_b, v_g_qn, v_g_kn, v_w_mla_out, v_ret_decay_fwd, v_ret_decay_bwd, v_w_ret_out, v_w_out, v_g_ffn, v_w_gate_up, v_w_down):
    given = dict(locals())
    S = x.shape[1]
    xs, tgt = x.reshape(S, D_MODEL), loss_target.reshape(S, D_MODEL)
    pos = positions.reshape(S, 1).astype(F32)

    first_shards = [given[n].astype(BF16) for n in FIRST]
    my_chip = 2 * lax.axis_index("x") + lax.axis_index("y")
    tab_m, tab_r, gathered = _rope_tables_and_first_gather(
        pos, _rope_consts(MLA_NOPE, MLA_ROPE // 2), _rope_consts(0, RET_QK // 2, RET_QK), first_shards)
    wts = {n: _assemble(n, _fill_slot(g, s, my_chip)) for n, g, s in zip(FIRST, gathered, first_shards)}
    late_shards = {n: given[n].astype(BF16) for n in LATE}
    small = {n: given[n].reshape(1, -1) for n in SMALL}

    loss_row, dx, pieces, sgrads = _local_step(xs, tab_m, tab_r, tgt, wts, late_shards, small)

    halves = [_grad_sum8("grad_sum_" + n, got) for n, got in zip(FIRST + LATE, pieces)]
    reduced = _half_exchange(halves)

    out = {}
    for n, r in zip(FIRST + LATE, reduced):
        g = r.reshape(given[n].shape)
        out["grad_" + n] = g
        out["delta_" + n], out["new_m_" + n], out["new_v_" + n] = _adamw("adamw_" + n, given[n], g, given["m_" + n], given["v_" + n])

    one = jnp.ones((1,), F32)
    pk = _small_allreduce_adamw(
        _pack_small([sgrads[n] for n in SMALL], loss_row[0, :1]),
        _pack_small([given[n] for n in SMALL], 0 * one),
        _pack_small([given["m_" + n] for n in SMALL], 0 * one),
        _pack_small([given["v_" + n] for n in SMALL], one))
    off = 0
    for n in SMALL:
        sz = given[n].shape[0]
        for pre, arr in zip(["grad_", "delta_", "new_m_", "new_v_"], pk):
            out[pre + n] = arr.reshape(-1)[off:off + sz]
        off += sz
    loss = pk[0].reshape(-1)[off]

    return (loss, dx.reshape(x.shape), *[out["grad_" + n] for n in WEIGHTS], *[out["delta_" + n] for n in WEIGHTS],
            *[out["new_m_" + n] for n in WEIGHTS], *[out["new_v_" + n] for n in WEIGHTS])
```

```python
import functools
import math

import numpy as np
import jax
import jax.numpy as jnp
from jax import lax
from jax.experimental import pallas as pl
from jax.experimental.pallas import tpu as pltpu

F32 = jnp.float32
BF16 = jnp.bfloat16
MESH = pl.DeviceIdType.MESH

D_MODEL = 1024
HEADS = 8
LANES = 128
MLA_Q_RANK, MLA_KV_RANK = 256, 128
MLA_NOPE, MLA_ROPE, MLA_V = 64, 32, 64
MLA_QK = MLA_NOPE + MLA_ROPE
LN2 = math.log(2.0)
MLA_Q_SCALE = MLA_QK ** -0.5 / LN2
RET_QK, RET_V, RET_CHUNK = 64, 128, 128
RET_QK_DTYPE = BF16
RET_CHUNKS_PER_STEP = 2
FFN_HIDDEN = 2816
ROPE_THETA = 10000.0
EPS = 1e-6
IN_SPLITS = [256, 128, 32, 512, 512, 1024, 1024, 2048]
IN_OFFS = [0] + list(np.cumsum(IN_SPLITS))
ADAM_LR, ADAM_B1, ADAM_B2, ADAM_EPS, ADAM_WD, ADAM_STEP = 0.001, 0.9, 0.999, 1e-08, 0.01, 10

VMEM_LIMIT = 56 * 1024 * 1024
ROW_TILE = 256
HEAD_ROW_TILE = 2048
MM_TM, MM_TN, MM_TK, MM_KFULL = 1408, 2048, 2048, 2816
ATT_TQ = 256
ATT_BQ, ATT_BK = 1024, 1024
ATT_HEADS_PER_STEP = 8
ATT_BWD_HEADS_PER_STEP = 4

SHARDED = ["w_in", "w_q_b", "w_kv_b", "w_mla_out", "w_ret_out", "w_out", "w_gate_up", "w_down"]
COL_SHARDED = {"w_in", "w_q_b", "w_kv_b", "w_mla_out", "w_gate_up"}
FIRST = ["w_in", "w_q_b", "w_kv_b"]
LATE = ["w_mla_out", "w_ret_out", "w_out", "w_gate_up", "w_down"]
SMALL = ["g_mix", "g_q_a", "g_kv_a", "g_qn", "g_kn", "ret_decay_fwd", "ret_decay_bwd", "g_ffn"]
WEIGHTS = ["g_mix", "w_in", "g_q_a", "w_q_b", "g_kv_a", "w_kv_b", "g_qn", "g_kn", "w_mla_out",
           "ret_decay_fwd", "ret_decay_bwd", "w_ret_out", "w_out", "g_ffn", "w_gate_up", "w_down"]
SMALL_ROWS = 24


def _params(**kw):
    return pltpu.CompilerParams(vmem_limit_bytes=VMEM_LIMIT, **kw)


def _pick(dim, target, unit=128):
    if dim <= target:
        return dim
    best = None
    for d in range(unit, target + 1, unit):
        if dim % d == 0:
            best = d
    assert best is not None, (dim, target)
    return best


_DOT = {"nn": (((1,), (0,)), ((), ())), "nt": (((1,), (1,)), ((), ())), "tn": (((0,), (0,)), ((), ()))}


def _dot(a, b, mode="nn"):
    return lax.dot_general(a, b, _DOT[mode], preferred_element_type=F32)


def _rms_rows(x, g):
    x = x.astype(F32)
    return x * lax.rsqrt(jnp.mean(x * x, axis=-1, keepdims=True) + EPS) * g


def _epi_loss(acc, extras, params):
    e = acc + extras[0] - extras[1]
    dy = e * (1.0 / D_MODEL)
    loss = 0.5 * jnp.sum(jnp.mean(e * e, axis=-1, keepdims=True), axis=0, keepdims=True)
    return [dy, dy], [jnp.broadcast_to(loss, (1, LANES))]


def _epi_rms_bwd(n_out):
    def fn(acc, extras, params):
        _, vjp = jax.vjp(_rms_rows, extras[0], params[0])
        dx, dg = vjp(acc)
        return [dx + extras[1]] * n_out, [dg]
    return fn


def _mm(name, a, b, mode, out_dtype, res=None, a_gain=None, epilogue=None, shard_out=False, scatter=None):
    if mode == "nn":
        (M, K), (K2, N) = a.shape, b.shape
    elif mode == "nt":
        (M, K), (N, K2) = a.shape, b.shape
    else:
        (K, M), (K2, N) = a.shape, b.shape
    assert K == K2, (name, a.shape, b.shape)
    tm, tn = _pick(M, MM_TM), _pick(N, MM_TN)
    tk = K if K <= MM_KFULL else _pick(K, MM_TK)
    if shard_out:
        tm, tn = M // 2, N // 4
    if epilogue is not None:
        tm = _pick(M, MM_TM // 2)
    nk = K // tk
    cache_a = a_gain is not None
    if a_gain is not None:
        assert mode == "nn" and tk == K and epilogue is None and not shard_out, name
    n_in = 2 + (res is not None) + (a_gain is not None)
    extras, eparams, e_outs, e_sums = ([], [], [], [])
    if epilogue is not None:
        assert tn == N and res is None and not shard_out, name
        epi_fn, extras, eparams, e_outs, e_sums = epilogue
    n_out = len(e_outs) + len(e_sums) if epilogue is not None else 1 + cache_a
    scatter = list(scatter or [])
    n_sc = len(scatter)
    assert not n_sc or epilogue is not None, name
    ni, nj = M // tm, N // tn

    def body(*refs):
        a_ref, b_ref = refs[0], refs[1]
        base = n_in + len(extras) + len(eparams)
        ex_refs = refs[n_in:n_in + len(extras)]
        ep_refs = refs[n_in + len(extras):base]
        sc_in, out_refs = refs[base:base + n_sc], refs[base + n_sc:base + n_sc + n_out]
        sc_out = refs[base + n_sc + n_out:base + 2 * n_sc + n_out]
        scratch = refs[base + 2 * n_sc + n_out:]
        acc = scratch[0]
        i, j, k = pl.program_id(0), pl.program_id(1), pl.program_id(2)

        if n_sc:
            @pl.when(jnp.logical_and(i == 0, jnp.logical_and(j == 0, k == 0)))
            def _():
                _scatter_start(_scatter_copies(sc_in, sc_out, scratch[-2], scratch[-1]))

        @pl.when(k == 0)
        def _():
            acc[...] = jnp.zeros_like(acc)

        if cache_a:
            @pl.when(j == 0)
            def _():
                out_refs[1][...] = _rms_rows(a_ref[...], refs[n_in - 1][...]).astype(BF16)
            av = out_refs[1][...]
        else:
            av = a_ref[...].astype(BF16)
        acc[...] += _dot(av, b_ref[...].astype(BF16), mode)

        @pl.when(k == nk - 1)
        def _():
            if epilogue is None:
                r = acc[...]
                if res is not None:
                    r = r + refs[2][...].astype(F32)
                out_refs[0][...] = r.astype(out_refs[0].dtype).reshape(out_refs[0].shape)
            else:
                vals, sums = epi_fn(acc[...], [r[...] for r in ex_refs], [p[...] for p in ep_refs])
                for o_ref, v in zip(out_refs, vals):
                    o_ref[...] = v.astype(o_ref.dtype)
                for s_ref, v in zip(out_refs[len(vals):], sums):
                    @pl.when(i == 0)
                    def _(s_ref=s_ref):
                        s_ref[...] = jnp.zeros_like(s_ref)
                    s_ref[...] += v

        if n_sc:
            @pl.when(jnp.logical_and(i == ni - 1, jnp.logical_and(j == nj - 1, k == nk - 1)))
            def _():
                _scatter_wait(_scatter_copies(sc_in, sc_out, scratch[-2], scratch[-1]))

    a_spec = pl.BlockSpec((tk, tm), lambda i, j, k: (k, i)) if mode == "tn" else pl.BlockSpec((tm, tk), lambda i, j, k: (i, k))
    b_spec = pl.BlockSpec((tn, tk), lambda i, j, k: (j, k)) if mode == "nt" else pl.BlockSpec((tk, tn), lambda i, j, k: (k, j))
    o_spec = pl.BlockSpec((tm, tn), lambda i, j, k: (i, j))
    const = lambda p: pl.BlockSpec(p.shape, lambda i, j, k: (0,) * p.ndim)
    ins, specs = [a, b], [a_spec, b_spec]
    if res is not None:
        ins.append(res)
        specs.append(o_spec)
    if a_gain is not None:
        ins.append(a_gain)
        specs.append(const(a_gain))
    ins += list(extras) + list(eparams)
    specs += [o_spec] * len(extras) + [const(p) for p in eparams]
    if epilogue is not None:
        out_specs = [o_spec] * len(e_outs) + [pl.BlockSpec(s, lambda i, j, k: (0, 0)) for s in e_sums]
        out_shape = [jax.ShapeDtypeStruct((M, N), dt) for dt in e_outs] + [jax.ShapeDtypeStruct(s, F32) for s in e_sums]
    elif shard_out:
        out_specs = pl.BlockSpec((1, 1, tm, tn), lambda i, j, k: (j, i, 0, 0))
        out_shape = jax.ShapeDtypeStruct((4, 2, tm, tn), out_dtype)
    elif cache_a:
        out_specs = [o_spec, pl.BlockSpec((tm, K), lambda i, j, k: (i, 0))]
        out_shape = [jax.ShapeDtypeStruct((M, N), out_dtype), jax.ShapeDtypeStruct((M, K), BF16)]
    else:
        out_specs, out_shape = o_spec, jax.ShapeDtypeStruct((M, N), out_dtype)
    scratch_shapes = [pltpu.VMEM((tm, tn), F32)]
    if n_sc:
        ins += scatter
        specs += [ANY] * n_sc
        out_specs = list(out_specs) + [ANY] * n_sc
        out_shape = list(out_shape) + [jax.ShapeDtypeStruct((8,) + g.shape[2:], g.dtype) for g in scatter]
        scratch_shapes += [pltpu.SemaphoreType.DMA((8 * n_sc,)), pltpu.SemaphoreType.DMA((7 * n_sc,))]
    res_ = pl.pallas_call(
        body, name=name, grid=(ni, nj, nk), in_specs=specs, out_specs=out_specs, out_shape=out_shape,
        scratch_shapes=scratch_shapes, compiler_params=_params(),
    )(*ins)
    if n_sc:
        return list(res_[:n_out]) + [list(res_[n_out:])]
    return res_


def _piece_spec(tm, piece):
    _, w, c0, per_group = piece
    if per_group:
        return pl.BlockSpec((tm, w), lambda i, g: (i, c0 + g))
    return pl.BlockSpec((tm, w), lambda i, g: (i, c0))


def _const_spec(p):
    return pl.BlockSpec(p.shape, lambda i, g: (0, 0))


def _place_into(into, S, tm, out_specs, out_shape, n_inputs):
    if into is None:
        return [], [], {}
    k, buf, total, c0 = into
    (_, w), per_group = out_specs[k].block_shape, out_shape[k].shape[1] != out_specs[k].block_shape[1]
    out_specs[k] = _piece_spec(tm, (None, w, c0, per_group))
    out_shape[k] = jax.ShapeDtypeStruct((S, total), out_shape[k].dtype)
    if buf is None:
        return [], [], {}
    return [buf], [pl.BlockSpec(memory_space=pl.ANY)], {n_inputs: k}


def _rowwise(name, fn, params, rows, auxs, outs, tm, groups=1, into=None):
    S = rows[0][0].shape[0]
    tm = min(tm, S)
    n_p, n_r, n_a = len(params), len(rows), len(auxs)
    n_buf = int(into is not None and into[1] is not None)

    def body(*refs):
        p = [r[...] for r in refs[:n_p]]
        r_ = [r[...] for r in refs[n_p:n_p + n_r]]
        a_ = [r[...] for r in refs[n_p + n_r:n_p + n_r + n_a]]
        for o_ref, o in zip(refs[n_p + n_r + n_a + n_buf:], fn(p, r_, a_)):
            o_ref[...] = o.astype(o_ref.dtype)

    out_specs, out_shape = [], []
    for w, dt, per_group in outs:
        out_specs.append(_piece_spec(tm, (None, w, 0, per_group)))
        out_shape.append(jax.ShapeDtypeStruct((S, w * (groups if per_group else 1)), dt))
    bufs, buf_specs, aliases = _place_into(into, S, tm, out_specs, out_shape, n_p + n_r + n_a)
    return pl.pallas_call(
        body, name=name, grid=(S // tm, groups),
        in_specs=[_const_spec(p) for p in params] + [_piece_spec(tm, q) for q in list(rows) + list(auxs)] + buf_specs,
        out_specs=out_specs, out_shape=out_shape, input_output_aliases=aliases, compiler_params=_params(),
    )(*params, *[q[0] for q in list(rows) + list(auxs)], *bufs)


def _rowwise_vjp(name, fn, params, rows, auxs, cots, d_outs, tm, groups=1, adds=None, into=None):
    S = rows[0][0].shape[0]
    tm = min(tm, S)
    n_p, n_r, n_a = len(params), len(rows), len(auxs)
    cot_flat = [q for c in cots for q in c]
    adds = adds or [None] * len(d_outs)
    add_flat = [q for q in adds if q is not None]
    n_c, n_add = len(cot_flat), len(add_flat)
    n_buf = int(into is not None and into[1] is not None)
    shared = [not all(rows[k][3] for k in idx) and groups > 1 for idx, _ in d_outs]

    def body(*refs):
        pos = 0
        p = [r[...] for r in refs[pos:pos + n_p]]; pos += n_p
        r_ = [r[...] for r in refs[pos:pos + n_r]]; pos += n_r
        a_ = [r[...] for r in refs[pos:pos + n_a]]; pos += n_a
        c_refs = refs[pos:pos + n_c]; pos += n_c
        add_refs = list(refs[pos:pos + n_add]); pos += n_add + n_buf
        d_refs = refs[pos:pos + len(d_outs)]; pos += len(d_outs)
        dp_refs = refs[pos:]
        i, g = pl.program_id(0), pl.program_id(1)
        outs, vjp_fn = jax.vjp(lambda pp, rr: fn(pp, rr, a_), p, r_)
        cts, ci = [], 0
        for c, o in zip(cots, outs):
            t = c_refs[ci][...].astype(F32)
            for extra in c_refs[ci + 1:ci + len(c)]:
                t = t + extra[...].astype(F32)
            ci += len(c)
            cts.append(t.astype(o.dtype))
        dp, dr = vjp_fn(cts)
        for (idx, _), d_ref, add, sh in zip(d_outs, d_refs, adds, shared):
            val = dr[idx[0]].astype(F32) if len(idx) == 1 else jnp.concatenate([dr[k].astype(F32) for k in idx], axis=1)
            if add is not None:
                val = val + add_refs.pop(0)[...].astype(F32)
            if sh:
                @pl.when(g == 0)
                def _(d_ref=d_ref):
                    d_ref[...] = jnp.zeros_like(d_ref)
                d_ref[...] += val.astype(d_ref.dtype)
            else:
                d_ref[...] = val.astype(d_ref.dtype)
        first = jnp.logical_and(i == 0, g == 0)
        for dp_ref, d in zip(dp_refs, dp):
            @pl.when(first)
            def _(dp_ref=dp_ref):
                dp_ref[...] = jnp.zeros_like(dp_ref)
            dp_ref[...] += d.astype(F32)

    out_specs, out_shape = [], []
    for (idx, dt), sh in zip(d_outs, shared):
        w = sum(rows[k][1] for k in idx)
        per_group = (not sh) and groups > 1
        out_specs.append(_piece_spec(tm, (None, w, 0, per_group)))
        out_shape.append(jax.ShapeDtypeStruct((S, w * (groups if per_group else 1)), dt))
    for p in params:
        out_specs.append(_const_spec(p))
        out_shape.append(jax.ShapeDtypeStruct(p.shape, F32))
    pieces = list(rows) + list(auxs) + cot_flat + add_flat
    bufs, buf_specs, aliases = _place_into(into, S, tm, out_specs, out_shape, n_p + len(pieces))
    res = pl.pallas_call(
        body, name=name, grid=(S // tm, groups),
        in_specs=[_const_spec(p) for p in params] + [_piece_spec(tm, q) for q in pieces] + buf_specs,
        out_specs=out_specs, out_shape=out_shape, input_output_aliases=aliases, compiler_params=_params(),
    )(*params, *[q[0] for q in pieces], *bufs)
    return list(res[:len(d_outs)]), list(res[len(d_outs):])


def _lane_roll(x, shift):
    @jax.custom_vjp
    def roll(v):
        return pltpu.roll(v, shift, 1)

    roll.defvjp(lambda v: (roll(v), None), lambda _, ct: (pltpu.roll(ct, LANES - shift, 1),))
    return roll(x)


@jax.custom_vjp
def _sigmoid(x):
    return 1.0 / (1.0 + jnp.exp(-x))


def _sigmoid_fwd(x):
    s = _sigmoid(x)
    return s, s


_sigmoid.defvjp(_sigmoid_fwd, lambda s, ct: (ct * s * (1.0 - s),))


def _rope(x, cos, sin_lo, sin_hi, half):
    return x * cos + _lane_roll(x, LANES - half) * sin_lo + _lane_roll(x, half) * sin_hi


def _f_rope_table(p, r, a):
    inv, first, second, fixed = p
    ang = a[0] * inv
    cs, sn = jnp.cos(ang), jnp.sin(ang)
    return [cs * (first + second) + fixed, -sn * first, sn * second]


def _f_rms(p, r, a):
    x = r[0].astype(F32)
    return [x * lax.rsqrt(jnp.mean(x * x, axis=-1, keepdims=True) + EPS) * p[0]]


def _f_mla_a(p, r, a):
    return _f_rms([p[0]], [r[0]], a) + _f_rms([p[1]], [r[1]], a)


def _f_mla_b(p, r, a):
    def norm_rope(v, g):
        ms = jnp.sum(v * v, axis=-1, keepdims=True) * (1.0 / MLA_QK)
        return _rope(v * lax.rsqrt(ms + EPS) * g, a[0], a[1], a[2], MLA_ROPE // 2)

    return [norm_rope(r[0].astype(F32), p[0]) * MLA_Q_SCALE, norm_rope(r[1].astype(F32) + r[2].astype(F32), p[1])]


def _f_ret_rope_q(p, r, a):
    return [_rope(r[0].astype(F32), a[0], a[1], a[2], RET_QK // 2)]


def _f_ret_rope_k(p, r, a):
    return [_rope(r[0].astype(F32), a[0], a[1], a[2], RET_QK // 2) * (RET_QK ** -0.5)]


def _f_ret_rope(p, r, a):
    return _f_ret_rope_q(p, r[:1], a) + _f_ret_rope_k(p, r[1:], a)


def _f_ret_post(p, r, a):
    ret = r[0].astype(F32) + r[1].astype(F32)
    g = r[2].astype(F32)
    normed = ret * lax.rsqrt(jnp.mean(ret * ret, axis=-1, keepdims=True) + EPS)
    return [g * _sigmoid(g) * normed]


def _f_merge(p, r, a):
    return [_sigmoid(r[0].astype(F32)) * r[2].astype(F32) + _sigmoid(r[1].astype(F32)) * r[3].astype(F32)]


def _f_swiglu(p, r, a):
    g = r[0].astype(F32)
    return [g * _sigmoid(g) * r[1].astype(F32)]


def _f_add(p, r, a):
    return [r[0].astype(F32) + r[1].astype(F32)]


def _flash_fwd(q, k, kv, shards):
    S = q.shape[0]
    tq = min(ATT_TQ, S)
    nq = S // tq
    n = len(shards)

    def body(q_ref, k_ref, v_ref, *rest):
        shard_refs, (o_ref, lse_ref), gathered = rest[:n], rest[n:n + 2], rest[n + 2:2 * n + 2]
        send_sems, recv_sems = rest[2 * n + 2:]
        h, qi = pl.program_id(0), pl.program_id(1)

        @pl.when(jnp.logical_and(h == 0, qi == 0))
        def _():
            _gather_start(_gather_copies(shard_refs, gathered, send_sems, recv_sems))

        for hh in range(hps):
            lanes = slice(hh * LANES, (hh + 1) * LANES)
            s = _dot(q_ref[:, lanes], k_ref[:, lanes], "nt")
            m = jnp.max(s, axis=-1, keepdims=True)
            p = jnp.exp2(s - m)
            l = jnp.sum(p, axis=-1, keepdims=True)
            pair = slice(hh // 2 * LANES, (hh // 2 + 1) * LANES)
            o_h = (_dot(p.astype(BF16), v_ref[:, pair]) / l * _ret_head_mask(hh)).astype(o_ref.dtype)
            if hh % 2 == 0:
                o_ref[:, pair] = o_h
            else:
                o_ref[:, pair] += o_h
            lse_ref[:, lanes] = jnp.broadcast_to(m + jnp.log2(l), (tq, LANES))

        @pl.when(jnp.logical_and(h == HEADS // hps - 1, qi == nq - 1))
        def _():
            _gather_wait(_gather_copies(shard_refs, gathered, send_sems, recv_sems))

    hps = ATT_HEADS_PER_STEP
    qs = pl.BlockSpec((tq, hps * LANES), lambda h, i: (i, h))
    vw = hps * MLA_V
    v0 = HEADS * LANES // vw
    res = pl.pallas_call(
        body, name="mla_fwd", grid=(HEADS // hps, nq),
        in_specs=[qs, pl.BlockSpec((S, hps * LANES), lambda h, i: (0, h), pipeline_mode=pl.Buffered(1)),
                  pl.BlockSpec((S, vw), lambda h, i: (0, v0 + h), pipeline_mode=pl.Buffered(1))]
        + [ANY] * n,
        out_specs=[pl.BlockSpec((tq, vw), lambda h, i: (i, h)), qs] + [ANY] * n,
        out_shape=[jax.ShapeDtypeStruct((S, HEADS * MLA_V), BF16), jax.ShapeDtypeStruct((S, HEADS * LANES), F32)]
        + [jax.ShapeDtypeStruct((4,) + s.shape, s.dtype) for s in shards],
        scratch_shapes=[pltpu.SemaphoreType.DMA((3 * n,)), pltpu.SemaphoreType.DMA((3 * n,))],
        compiler_params=_params(),
    )(q, k, kv, *shards)
    mine = 2 * lax.axis_index("x") + lax.axis_index("y")
    return res[0], res[1], [_fill_slot(g, s, mine) for g, s in zip(res[2:], shards)]


def _flash_bwd(q, k, kv, do, lse, o, gs):
    S = q.shape[0]
    tq, tk = min(ATT_BQ, S), min(ATT_BK, S)
    nq, nkt = S // tq, S // tk
    n = len(gs)

    def body(q_ref, k_ref, v_ref, do_ref, lse_ref, o_ref, *rest):
        g_refs, (dq_ref, dk_ref, dv_ref), got_refs = rest[:n], rest[n:n + 3], rest[n + 3:2 * n + 3]
        dk_sc, dv_sc, send_sems, recv_sems = rest[2 * n + 3:]
        h, ki, qi = pl.program_id(0), pl.program_id(1), pl.program_id(2)

        @pl.when(jnp.logical_and(h == 0, jnp.logical_and(ki == 0, qi == 0)))
        def _():
            _scatter_start(_scatter_copies(g_refs, got_refs, send_sems, recv_sems))

        @pl.when(jnp.logical_and(ki == 0, qi == 0))
        def _():
            dq_ref[...] = jnp.zeros_like(dq_ref)

        @pl.when(qi == 0)
        def _():
            dk_sc[...] = jnp.zeros_like(dk_sc)
            dv_sc[...] = jnp.zeros_like(dv_sc)

        rows = pl.ds(pl.multiple_of(qi * tq, tq), tq)
        for hh in range(hps):
            lanes = slice(hh * LANES, (hh + 1) * LANES)
            pair = slice(hh // 2 * LANES, (hh // 2 + 1) * LANES)
            qv, kv_ = q_ref[:, lanes], k_ref[:, lanes]
            do32 = do_ref[:, pair].astype(F32) * _ret_head_mask(hh)
            dov = do32.astype(BF16)
            p = jnp.exp2(_dot(qv, kv_, "nt") - lse_ref[:, lanes][:, :1])
            dp = _dot(dov, v_ref[:, pair], "nt")
            delta = jnp.sum(do32 * o_ref[:, pair].astype(F32), axis=-1, keepdims=True)
            ds = (p * (dp - delta) * LN2).astype(BF16)
            dv_sc[:, pair] += _dot(p.astype(BF16), dov, "tn")
            dk_sc[:, lanes] += _dot(ds, qv, "tn")
            dq_ref[rows, lanes] += _dot(ds, kv_)

        @pl.when(qi == nq - 1)
        def _():
            dk_ref[...] = dk_sc[...].astype(dk_ref.dtype)
            dv_ref[...] = dv_sc[...].astype(dv_ref.dtype)

        @pl.when(jnp.logical_and(h == HEADS // hps - 1, jnp.logical_and(ki == nkt - 1, qi == nq - 1)))
        def _():
            _scatter_wait(_scatter_copies(g_refs, got_refs, send_sems, recv_sems))

    hps = ATT_BWD_HEADS_PER_STEP
    qs = pl.BlockSpec((tq, hps * LANES), lambda h, j, i: (i, h))
    ks = pl.BlockSpec((tk, hps * LANES), lambda h, j, i: (j, h))
    vw = hps * MLA_V
    v0 = HEADS * LANES // vw
    qv_s = pl.BlockSpec((tq, vw), lambda h, j, i: (i, h))
    kv_s = pl.BlockSpec((tk, vw), lambda h, j, i: (j, h))
    res = pl.pallas_call(
        body, name="mla_bwd", grid=(HEADS // hps, nkt, nq),
        in_specs=[qs, ks, pl.BlockSpec((tk, vw), lambda h, j, i: (j, v0 + h)), qv_s, qs, qv_s] + [ANY] * n,
        out_specs=[pl.BlockSpec((S, hps * LANES), lambda h, j, i: (0, h), pipeline_mode=pl.Buffered(1)), ks, kv_s] + [ANY] * n,
        out_shape=[jax.ShapeDtypeStruct((S, HEADS * LANES), F32), jax.ShapeDtypeStruct((S, HEADS * LANES), BF16),
                   jax.ShapeDtypeStruct((S, HEADS * MLA_V), BF16)]
        + [jax.ShapeDtypeStruct((8,) + g.shape[2:], g.dtype) for g in gs],
        scratch_shapes=[pltpu.VMEM((tk, hps * LANES), F32), pltpu.VMEM((tk, vw), F32)]
        + [pltpu.SemaphoreType.DMA((8 * n,)), pltpu.SemaphoreType.DMA((7 * n,))],
        compiler_params=_params(),
    )(q, k, kv, do, lse, o, *gs)
    return res[0], res[1], res[2], list(res[3:])


def _ret_tables(decay_row, backward):
    C = RET_CHUNK
    lg = -jnp.exp(decay_row)
    t = lax.broadcasted_iota(jnp.int32, (C, C), 0).astype(F32)
    s = lax.broadcasted_iota(jnp.int32, (C, C), 1).astype(F32)
    ridx = lax.broadcasted_iota(jnp.int32, (C, LANES), 0).astype(F32)
    if backward:
        dist, mask, aw, bw = s - t, s > t, C - ridx, ridx
    else:
        dist, mask, aw, bw = t - s, t >= s, ridx + 1.0, C - 1.0 - ridx
    dist = jnp.maximum(dist, 0.0)
    din = jnp.where(mask, jnp.exp(lg[:, :1] * dist), 0.0)
    return dict(din=din, dist=dist, a=jnp.exp(lg * aw), b=jnp.exp(lg * bw), c=jnp.exp(lg * C), aw=aw, bw=bw)


def _ret_fill_tables(decs, din_sc, a_sc, b_sc):
    for d, dec in enumerate(decs):
        for h in range(HEADS):
            tb = _ret_tables(dec[h:h + 1, :], d == 1)
            din_sc[d, h], a_sc[d, h], b_sc[d, h] = tb["din"], tb["a"], tb["b"]


def _ret_head_mask(h):
    lane = lax.broadcasted_iota(jnp.int32, (1, LANES), 1)
    return jnp.where((lane >= RET_QK) == bool(h % 2), 1.0, 0.0).astype(F32)


def _ret_fwd(qr, kr, proj, v_block, dec_f, dec_b):
    S = qr.shape[0]
    C = RET_CHUNK
    n = S // C
    nc = min(RET_CHUNKS_PER_STEP, n)
    nb = n // nc
    W = HEADS * LANES

    def body(qf, kf, vf, qb, kb, vb, df, db, of, ob, sf_out, sb_out, st, din_sc, a_sc, b_sc):
        @pl.when(pl.program_id(0) == 0)
        def _():
            st[...] = jnp.zeros_like(st)
            _ret_fill_tables((df, db), din_sc, a_sc, b_sc)

        for d, (q_ref, k_ref, v_ref, dec, o_ref, s_out) in enumerate(
                [(qf, kf, vf, df, of, sf_out), (qb, kb, vb, db, ob, sb_out)]):
            for h in range(HEADS):
                lanes, pair = slice(h * LANES, (h + 1) * LANES), slice(h // 2 * LANES, (h // 2 + 1) * LANES)
                mine = _ret_head_mask(h)
                din, a, b = din_sc[d, h], a_sc[d, h], b_sc[d, h]
                c = jnp.exp(-jnp.exp(dec[h:h + 1, :]) * C)
                for ci in (range(nc) if d == 0 else reversed(range(nc))):
                    rows = slice(ci * C, (ci + 1) * C)
                    qf32, kf32 = q_ref[rows, pair].astype(F32) * mine, k_ref[rows, pair].astype(F32) * mine
                    v = v_ref[rows, lanes]
                    state = st[d, h]
                    s_out[ci, h] = state
                    inner = _dot((_dot(qf32.astype(BF16), kf32.astype(BF16), "nt") * din).astype(BF16), v)
                    cross = _dot((qf32 * a).astype(BF16), state.astype(BF16))
                    o_ref[rows, lanes] = inner + cross
                    st[d, h] = state * c + _dot((kf32 * b).astype(BF16), v, "tn")

    fw = lambda c0, w=W: pl.BlockSpec((nc * C, w), lambda j: (j, c0))
    bw = lambda c0, w=W: pl.BlockSpec((nc * C, w), lambda j: (nb - 1 - j, c0))
    dec_spec = pl.BlockSpec((HEADS, LANES), lambda j: (0, 0))
    st_shape = jax.ShapeDtypeStruct((n, HEADS, LANES, LANES), F32)
    QW = W // 2
    return pl.pallas_call(
        body, name="ret_fwd", grid=(nb,),
        in_specs=[fw(0, QW), fw(0, QW), fw(v_block), bw(0, QW), bw(0, QW), bw(v_block), dec_spec, dec_spec],
        out_specs=[fw(0), bw(0), pl.BlockSpec((nc, HEADS, LANES, LANES), lambda j: (j, 0, 0, 0)),
                   pl.BlockSpec((nc, HEADS, LANES, LANES), lambda j: (nb - 1 - j, 0, 0, 0))],
        out_shape=[jax.ShapeDtypeStruct((S, W), F32)] * 2 + [st_shape] * 2,
        scratch_shapes=[pltpu.VMEM((2, HEADS, LANES, LANES), F32), pltpu.VMEM((2, HEADS, C, C), F32),
                        pltpu.VMEM((2, HEADS, C, LANES), F32), pltpu.VMEM((2, HEADS, C, LANES), F32)],
        compiler_params=_params(),
    )(qr, kr, proj, qr, kr, proj, dec_f, dec_b)


def _ret_bwd(qr, kr, proj, v_block, dret, sf, sb, dec_f, dec_b):
    S = qr.shape[0]
    C = RET_CHUNK
    n = S // C
    nc = min(RET_CHUNKS_PER_STEP, n)
    nb = n // nc
    W = HEADS * LANES

    def body(qf, kf, vf, gf, sf_ref, qb, kb, vb, gb, sb_ref, df, db,
             dqf, dkf, dvf, dqb, dkb, dvb, ddf, ddb, ds_sc, din_sc, a_sc, b_sc):
        j = pl.program_id(0)

        @pl.when(j == 0)
        def _():
            ds_sc[...] = jnp.zeros_like(ds_sc)
            ddf[...] = jnp.zeros_like(ddf)
            ddb[...] = jnp.zeros_like(ddb)
            _ret_fill_tables((df, db), din_sc, a_sc, b_sc)

        for d, (q_ref, k_ref, v_ref, g_ref, s_ref, dec, dq_ref, dk_ref, dv_ref, dd_ref) in enumerate(
                [(qf, kf, vf, gf, sf_ref, df, dqf, dkf, dvf, ddf), (qb, kb, vb, gb, sb_ref, db, dqb, dkb, dvb, ddb)]):
            static = _ret_tables(dec[0:1, :], d == 1)
            dist, aw, bw_ = static["dist"], static["aw"], static["bw"]
            for h in range(HEADS):
                lanes, pair = slice(h * LANES, (h + 1) * LANES), slice(h // 2 * LANES, (h // 2 + 1) * LANES)
                mine = _ret_head_mask(h)
                din, a, b = din_sc[d, h], a_sc[d, h], b_sc[d, h]
                c = jnp.exp(-jnp.exp(dec[h:h + 1, :]) * C)
                dlg = jnp.zeros((1, 1), F32)
                for ci in (reversed(range(nc)) if d == 0 else range(nc)):
                    rows = slice(ci * C, (ci + 1) * C)
                    v, g = v_ref[rows, lanes], g_ref[rows, lanes]
                    qf32, kf32 = q_ref[rows, pair].astype(F32) * mine, k_ref[rows, pair].astype(F32) * mine
                    q, k = qf32.astype(BF16), kf32.astype(BF16)
                    state, dstate = s_ref[ci, h], ds_sc[d, h]
                    dstate_b = dstate.astype(BF16)
                    dp = _dot(g, v, "nt")
                    a_ = _dot(q, k, "nt")
                    da = (dp * din).astype(BF16)
                    g1 = _dot(g, state.astype(BF16), "nt")
                    g2 = _dot(v, dstate_b, "nt")
                    dq_h = (_dot(da, k) + g1 * a).astype(dq_ref.dtype)
                    dk_h = (_dot(da, q, "tn") + g2 * b).astype(dk_ref.dtype)
                    if h % 2 == 0:
                        dq_ref[rows, pair], dk_ref[rows, pair] = dq_h, dk_h
                    else:
                        dq_ref[rows, pair] += dq_h
                        dk_ref[rows, pair] += dk_h
                    dv_ref[rows, lanes] = (_dot((a_ * din).astype(BF16), g, "tn")
                                           + _dot((kf32 * b).astype(BF16), dstate_b)).astype(dv_ref.dtype)
                    dlg = dlg + (jnp.sum(dp * a_ * din * dist, keepdims=True)
                                 + jnp.sum(g1 * qf32 * a * aw, keepdims=True)
                                 + jnp.sum(g2 * kf32 * b * bw_, keepdims=True)
                                 + C * jnp.sum(c * dstate * state, keepdims=True))
                    ds_sc[d, h] = dstate * c + _dot((qf32 * a).astype(BF16), g, "tn")
                dd_ref[h:h + 1, :] += jnp.broadcast_to(dlg, (1, LANES))

        @pl.when(j == nb - 1)
        def _():
            ddf[...] = ddf[...] * -jnp.exp(df[...])
            ddb[...] = ddb[...] * -jnp.exp(db[...])

    fw = lambda c0, w=W: pl.BlockSpec((nc * C, w), lambda j: (nb - 1 - j, c0))
    bw = lambda c0, w=W: pl.BlockSpec((nc * C, w), lambda j: (j, c0))
    dec_spec = pl.BlockSpec((HEADS, LANES), lambda j: (0, 0))
    QW = W // 2
    act, act_qk = jax.ShapeDtypeStruct((S, W), BF16), jax.ShapeDtypeStruct((S, QW), BF16)
    return pl.pallas_call(
        body, name="ret_bwd", grid=(nb,),
        in_specs=[fw(0, QW), fw(0, QW), fw(v_block), fw(0),
                  pl.BlockSpec((nc, HEADS, LANES, LANES), lambda j: (nb - 1 - j, 0, 0, 0)),
                  bw(0, QW), bw(0, QW), bw(v_block), bw(0), pl.BlockSpec((nc, HEADS, LANES, LANES), lambda j: (j, 0, 0, 0)),
                  dec_spec, dec_spec],
        out_specs=[fw(0, QW), fw(0, QW), fw(0), bw(0, QW), bw(0, QW), bw(0)] + [dec_spec] * 2,
        out_shape=[act_qk, act_qk, act, act_qk, act_qk, act] + [jax.ShapeDtypeStruct((HEADS, LANES), F32)] * 2,
        scratch_shapes=[pltpu.VMEM((2, HEADS, LANES, LANES), F32), pltpu.VMEM((2, HEADS, C, C), F32),
                        pltpu.VMEM((2, HEADS, C, LANES), F32), pltpu.VMEM((2, HEADS, C, LANES), F32)],
        compiler_params=_params(),
    )(qr, kr, proj, dret, sf, qr, kr, proj, dret, sb, dec_f, dec_b)


def _pad_heads(w, hd):
    K = w.shape[0]
    return jnp.pad(w.reshape(K, HEADS, hd), ((0, 0), (0, 0), (0, LANES - hd))).reshape(K, HEADS * LANES)


def _unpad_heads(w, hd):
    K = w.shape[0]
    return w.reshape(K, HEADS, LANES)[:, :, :hd].reshape(K, HEADS * hd)


def _rope_consts(first_lane, half, period=LANES):
    lane = np.arange(LANES) % period
    first = ((lane >= first_lane) & (lane < first_lane + half)).astype(np.float32)
    second = ((lane >= first_lane + half) & (lane < first_lane + 2 * half)).astype(np.float32)
    fixed = (lane < first_lane).astype(np.float32)
    j = np.where(first > 0, lane - first_lane, lane - first_lane - half) * (first + second)
    inv = (ROPE_THETA ** (-j.astype(np.float64) / half)).astype(np.float32)
    return [jnp.asarray(v.reshape(1, LANES), F32) for v in (inv, first, second, fixed)]


def _assemble(name, gathered):
    if name in COL_SHARDED:
        return jnp.transpose(gathered, (1, 0, 2)).reshape(gathered.shape[1], 4 * gathered.shape[2])
    return gathered.reshape(4 * gathered.shape[1], gathered.shape[2])


def _split_for_reducers(name, g, dtype):
    if name in COL_SHARDED:
        K, N4 = g.shape
        return jnp.transpose(g.reshape(2, K // 2, 4, N4 // 4), (2, 0, 1, 3)).astype(dtype)
    return g.reshape(4, 2, g.shape[0] // 8, g.shape[1]).astype(dtype)


def _local_step(x, tab_m, tab_r, tgt, wts, late_shards, small):
    w_in = wts["w_in"]
    seg = [w_in[:, IN_OFFS[i]:IN_OFFS[i + 1]] for i in range(8)]
    kr_w = jnp.pad(seg[2], ((0, 0), (MLA_NOPE, LANES - MLA_QK)))
    w_in_p = jnp.concatenate([seg[7], seg[5], seg[6], seg[3], seg[4], seg[0], seg[1], kr_w], axis=1)
    QR0, KR0, CQ0 = 4096, 4608, 5120
    w_qb_p = _pad_heads(wts["w_q_b"], MLA_QK)
    kvw = wts["w_kv_b"].reshape(MLA_KV_RANK, HEADS, MLA_NOPE + MLA_V)
    pad_kv = lambda t: jnp.pad(t, ((0, 0), (0, 0), (0, LANES - t.shape[2]))).reshape(MLA_KV_RANK, HEADS * LANES)
    w_kn_p, w_v = pad_kv(kvw[:, :, :MLA_NOPE]), kvw[:, :, MLA_NOPE:].reshape(MLA_KV_RANK, HEADS * MLA_V)
    w_kv_p = jnp.concatenate([w_kn_p, w_v], axis=1)
    g_qn_p = jnp.pad(small["g_qn"], ((0, 0), (0, LANES - MLA_QK)))
    g_kn_p = jnp.pad(small["g_kn"], ((0, 0), (0, LANES - MLA_QK)))
    dec_f = jnp.broadcast_to(small["ret_decay_fwd"].reshape(HEADS, 1), (HEADS, LANES))
    dec_b = jnp.broadcast_to(small["ret_decay_bwd"].reshape(HEADS, 1), (HEADS, LANES))
    T, N = True, False
    RT, HT = ROW_TILE, HEAD_ROW_TILE
    RW = 2 * ROW_TILE

    aux_m = [(t, LANES, 0, N) for t in tab_m]
    aux_r = [(t, LANES, 0, N) for t in tab_r]

    proj, h = _mm("proj", x, w_in_p, "nn", BF16, a_gain=small["g_mix"])
    rows_a = [(proj, MLA_Q_RANK, CQ0 // MLA_Q_RANK, N), (proj, MLA_KV_RANK, (CQ0 + MLA_Q_RANK) // MLA_KV_RANK, N)]
    cqn, ckvn = _rowwise("mla_lat_norm", _f_mla_a, [small["g_q_a"], small["g_kv_a"]], rows_a, [],
                         [(MLA_Q_RANK, BF16, N), (MLA_KV_RANK, BF16, N)], RW)
    qraw = _mm("mla_q_up", cqn, w_qb_p, "nn", BF16)
    kv = _mm("mla_kv_up", ckvn, w_kv_p, "nn", BF16)
    rows_b = [(qraw, LANES, 0, T), (kv, LANES, 0, T), (proj, LANES, (CQ0 + MLA_Q_RANK + MLA_KV_RANK) // LANES, N)]
    q, k = _rowwise("mla_qk_norm_rope", _f_mla_b, [g_qn_p, g_kn_p], rows_b, aux_m, [(LANES, BF16, T)] * 2, HT, HEADS)
    o, lse, late = _flash_fwd(q, k, kv, [late_shards[n] for n in LATE])
    wl = {n: _assemble(n, g) for n, g in zip(LATE, late)}
    w_mla_p = wl["w_mla_out"]
    w_ret_out, w_out, w_gu, w_down = wl["w_ret_out"], wl["w_out"], wl["w_gate_up"], wl["w_down"]
    y_a = _mm("mla_out", o, w_mla_p, "nn", BF16)
    rows_rr = [(proj, LANES, QR0 // LANES, T), (proj, LANES, KR0 // LANES, T)]
    qr, kr = _rowwise("ret_rope", _f_ret_rope, [], rows_rr, aux_r, [(LANES, RET_QK_DTYPE, T)] * 2, HT, HEADS // 2)
    ret_f, ret_b, st_f, st_b = _ret_fwd(qr, kr, proj, 2, dec_f, dec_b)
    rows_rp = [(ret_f, LANES, 0, T), (ret_b, LANES, 0, T), (proj, LANES, 24, T)]
    (o_b,) = _rowwise("ret_post", _f_ret_post, [], rows_rp, [], [(LANES, BF16, T)], HT, HEADS)
    y_b = _mm("ret_out", o_b, w_ret_out, "nn", BF16)
    rows_m = [(proj, D_MODEL, 0, N), (proj, D_MODEL, 1, N), (y_a, D_MODEL, 0, N), (y_b, D_MODEL, 0, N)]
    (merged,) = _rowwise("merge", _f_merge, [], rows_m, [], [(D_MODEL, BF16, N)], RW)
    x2 = _mm("mix_out", merged, w_out, "nn", F32, res=x)
    gu, h2 = _mm("ffn_gate_up", x2, w_gu, "nn", BF16, a_gain=small["g_ffn"])
    rows_sw = [(gu, FFN_HIDDEN, 0, N), (gu, FFN_HIDDEN, 1, N)]
    (act,) = _rowwise("swiglu", _f_swiglu, [], rows_sw, [], [(FFN_HIDDEN, BF16, N)], RT)
    dy, dy_b16, loss_row = _mm("ffn_down", act, w_down, "nn", None,
                               epilogue=(_epi_loss, [x2, tgt], [], [F32, BF16], [(1, LANES)]))

    dact = _mm("d_act", dy_b16, w_down, "nt", BF16)
    dw_down = _mm("dw_down", act, dy_b16, "tn", BF16)
    (dgu,), _ = _rowwise_vjp("swiglu_bwd", _f_swiglu, [], rows_sw, [], [[(dact, FFN_HIDDEN, 0, N)]], [([0, 1], BF16)], RT)
    dx2, dx2_b16, dg_ffn = _mm("d_h2", dgu, w_gu, "nt", None,
                               epilogue=(_epi_rms_bwd(2), [x2, dy], [small["g_ffn"]], [F32, BF16], [(1, D_MODEL)]))
    dw_gu = _mm("dw_gate_up", h2, dgu, "tn", BF16, shard_out=True)
    dmerged = _mm("d_merged", dx2_b16, w_out, "nt", BF16)
    dw_out = _mm("dw_out", merged, dx2_b16, "tn", BF16)
    PW = w_in_p.shape[1]
    (dproj, dy_a, dy_b), _ = _rowwise_vjp("merge_bwd", _f_merge, [], rows_m, [], [[(dmerged, D_MODEL, 0, N)]],
                                          [([0, 1], BF16), ([2], BF16), ([3], BF16)], RW, into=(0, None, PW, 0))
    do_b = _mm("d_ret_o", dy_b, w_ret_out, "nt", BF16)
    dw_ret_out = _mm("dw_ret_out", o_b, dy_b, "tn", BF16)
    (dret, dproj), _ = _rowwise_vjp("ret_post_bwd", _f_ret_post, [], rows_rp, [], [[(do_b, LANES, 0, T)]],
                                    [([0], BF16), ([2], BF16)], HT, HEADS, into=(1, dproj, PW, 24))
    dqf, dkf, dvf, dqb, dkb, dvb, ddec_f, ddec_b = _ret_bwd(qr, kr, proj, 2, dret, st_f, st_b, dec_f, dec_b)
    (dproj,), _ = _rowwise_vjp("ret_rope_q_bwd", _f_ret_rope_q, [], rows_rr[:1], aux_r,
                               [[(dqf, LANES, 0, T), (dqb, LANES, 0, T)]], [([0], BF16)], HT, HEADS // 2,
                               into=(0, dproj, PW, QR0 // LANES))
    (dproj,), _ = _rowwise_vjp("ret_rope_k_bwd", _f_ret_rope_k, [], rows_rr[1:], aux_r,
                               [[(dkf, LANES, 0, T), (dkb, LANES, 0, T)]], [([0], BF16)], HT, HEADS // 2,
                               into=(0, dproj, PW, KR0 // LANES))
    (dproj,) = _rowwise("ret_dv_sum", _f_add, [], [(dvf, D_MODEL, 0, N), (dvb, D_MODEL, 0, N)], [], [(D_MODEL, BF16, N)], RW,
                        into=(0, dproj, PW, 2))
    do = _mm("d_mla_o", dy_a, w_mla_p, "nt", BF16)
    dw_mla = _mm("dw_mla_out", o, dy_a, "tn", BF16)
    late_grads = {"w_mla_out": dw_mla, "w_ret_out": dw_ret_out, "w_out": dw_out, "w_down": dw_down}
    late_gs = [dw_gu if n == "w_gate_up" else _split_for_reducers(n, late_grads[n], BF16) for n in LATE]
    dq, dk, dv, late_got = _flash_bwd(q, k, kv, do, lse, o, late_gs)
    (dqraw, dkn, dkr), (dg_qn_p, dg_kn_p) = _rowwise_vjp(
        "mla_qk_norm_rope_bwd", _f_mla_b, [g_qn_p, g_kn_p], rows_b, aux_m, [[(dq, LANES, 0, T)], [(dk, LANES, 0, T)]],
        [([0], BF16), ([1], BF16), ([2], F32)], HT, HEADS)
    dckvn = _mm("d_ckvn_v", dv, w_v, "nt", BF16, res=_mm("d_ckvn_k", dkn, w_kn_p, "nt", F32))
    dw_kn_p = _mm("dw_kv_k", ckvn, dkn, "tn", BF16)
    dw_v = _mm("dw_kv_v", ckvn, dv, "tn", BF16)
    dcqn = _mm("d_cqn", dqraw, w_qb_p, "nt", BF16)
    dw_qb_p = _mm("dw_q_b", cqn, dqraw, "tn", BF16)
    (dcq, dckv), (dg_q_a, dg_kv_a) = _rowwise_vjp(
        "mla_lat_norm_bwd", _f_mla_a, [small["g_q_a"], small["g_kv_a"]], rows_a, [],
        [[(dcqn, MLA_Q_RANK, 0, N)], [(dckvn, MLA_KV_RANK, 0, N)]], [([0], BF16), ([1], BF16)], RW)
    dproj = lax.dynamic_update_slice(dproj, jnp.concatenate([dcq, dckv, dkr.astype(BF16)], axis=1), (0, CQ0))
    dw_in_p = _mm("dw_in", h, dproj, "tn", BF16)

    c = lambda a, b_: dw_in_p[:, a:b_]
    kr0 = CQ0 + MLA_Q_RANK + MLA_KV_RANK
    dw_in = jnp.concatenate([c(CQ0, CQ0 + MLA_Q_RANK), c(CQ0 + MLA_Q_RANK, kr0), c(kr0 + MLA_NOPE, kr0 + MLA_QK), c(QR0, KR0),
                             c(KR0, CQ0), c(2048, 3072), c(3072, 4096), c(0, 2048)], axis=1)
    dw_kn = dw_kn_p.reshape(MLA_KV_RANK, HEADS, LANES)[:, :, :MLA_NOPE]
    dw_kv = jnp.concatenate([dw_kn, dw_v.reshape(MLA_KV_RANK, HEADS, MLA_V)], axis=2).reshape(MLA_KV_RANK, HEADS * (MLA_NOPE + MLA_V))
    grads = {"w_in": dw_in, "w_q_b": _unpad_heads(dw_qb_p, MLA_QK), "w_kv_b": dw_kv}
    dx, dg_mix, first_got = _mm("d_h", dproj, w_in_p, "nt", None,
                                epilogue=(_epi_rms_bwd(1), [x, dx2], [small["g_mix"]], [F32], [(1, D_MODEL)]),
                                scatter=[_split_for_reducers(n, grads[n], BF16) for n in FIRST])
    sgrads = {"g_mix": dg_mix, "g_q_a": dg_q_a, "g_kv_a": dg_kv_a, "g_qn": dg_qn_p[:, :MLA_QK], "g_kn": dg_kn_p[:, :MLA_QK],
              "ret_decay_fwd": ddec_f[:, 0].reshape(1, HEADS), "ret_decay_bwd": ddec_b[:, 0].reshape(1, HEADS), "g_ffn": dg_ffn}
    return loss_row, dx, first_got + late_got, sgrads


def _coords():
    return lax.axis_index("x"), lax.axis_index("y"), lax.axis_index("c")


def _other_chips(x, y):
    return [(1 - x, y), (x, 1 - y), (1 - x, 1 - y)]


ANY = pl.BlockSpec(memory_space=pl.ANY)


def _gather_copies(ins, outs, send_sems, recv_sems):
    x, y, c = _coords()
    mine = 2 * x + y
    sends, arrivals = [], []
    for w in range(len(ins)):
        for j, (cx, cy) in enumerate(_other_chips(x, y)):
            sems = dict(send_sem=send_sems.at[3 * w + j], recv_sem=recv_sems.at[3 * w + j],
                        device_id=(cx, cy, c), device_id_type=MESH)
            sends.append(pltpu.make_async_remote_copy(src_ref=ins[w], dst_ref=outs[w].at[mine], **sems))
            arrivals.append(functools.partial(pltpu.make_async_remote_copy, src_ref=ins[w],
                                              dst_ref=outs[w].at[2 * cx + cy], **sems))
    return sends, arrivals


def _gather_start(copies):
    for cp in list(copies[0]) + list(copies[2] if len(copies) > 2 else []):
        cp.start()


def _gather_wait(copies):
    sends, arrivals = copies[:2]
    for make in arrivals:
        make().wait_recv()
    for cp in sends:
        cp.wait_send()
    for cp in (copies[2] if len(copies) > 2 else []):
        cp.wait()


def _fill_slot(buf, piece, slot):
    idx = lax.broadcasted_iota(jnp.int32, (buf.shape[0],) + (1,) * piece.ndim, 0)
    return jnp.where(idx == slot, piece[None], buf)


def _rope_tables_and_first_gather(pos, consts_mla, consts_ret, shards):
    S = pos.shape[0]
    tm = min(HEAD_ROW_TILE, S)
    nt = S // tm
    n = len(shards)

    def body(pos_ref, *refs):
        consts, ins = (refs[:4], refs[4:8]), refs[8:8 + n]
        tabs, outs = refs[8 + n:14 + n], refs[14 + n:14 + 2 * n]
        send_sems, recv_sems = refs[14 + 2 * n:]
        i = pl.program_id(0)
        x, y, c = _coords()
        chips = _other_chips(x, y)
        mine = 2 * x + y

        def half(ref, slot, core):
            rows = ref.shape[1] // 2
            return ref.at[slot, pl.ds(pl.multiple_of(core * rows, 8), rows)]

        def copy(w, k, slot, core, to, src=None):
            return pltpu.make_async_remote_copy(
                src_ref=half(outs[w], slot, core) if src is None else src, dst_ref=half(outs[w], slot, core),
                send_sem=send_sems.at[6 * w + k], recv_sem=recv_sems.at[6 * w + k], device_id=to, device_id_type=MESH)

        def first(w, j):
            rows = ins[w].shape[0] // 2
            return copy(w, j, mine, c, (*chips[j], c), src=ins[w].at[pl.ds(pl.multiple_of(c * rows, 8), rows)])

        @pl.when(i == 0)
        def _():
            for w in range(n):
                for j in range(3):
                    first(w, j).start()

        for k in range(2):
            vals = _f_rope_table([r[...] for r in consts[k]], None, [pos_ref[...]])
            for t_ref, v in zip(tabs[3 * k:3 * k + 3], vals):
                t_ref[...] = v

        @pl.when(i == nt - 1)
        def _():
            passed = []
            for w in range(n):
                for j, (cx, cy) in enumerate(chips):
                    copy(w, j, 2 * cx + cy, c, (x, y, c)).wait_recv()
                    cp = copy(w, 3 + j, 2 * cx + cy, c, (x, y, 1 - c))
                    cp.start()
                    passed.append(cp)
            for w in range(n):
                for j, (cx, cy) in enumerate(chips):
                    copy(w, 3 + j, 2 * cx + cy, 1 - c, (x, y, c)).wait_recv()
            for w in range(n):
                for j in range(3):
                    first(w, j).wait_send()
            for cp in passed:
                cp.wait_send()

    const = lambda p: pl.BlockSpec(p.shape, lambda i: (0, 0))
    tab = pl.BlockSpec((tm, LANES), lambda i: (i, 0))
    res = pl.pallas_call(
        body, name="rope_tables_first_gather", grid=(nt,),
        in_specs=[pl.BlockSpec((tm, 1), lambda i: (i, 0))] + [const(p) for p in list(consts_mla) + list(consts_ret)] + [ANY] * n,
        out_specs=[tab] * 6 + [ANY] * n,
        out_shape=[jax.ShapeDtypeStruct((S, LANES), F32)] * 6 + [jax.ShapeDtypeStruct((4,) + s.shape, s.dtype) for s in shards],
        scratch_shapes=[pltpu.SemaphoreType.DMA((6 * n,)), pltpu.SemaphoreType.DMA((6 * n,))],
        compiler_params=_params(),
    )(pos, *consts_mla, *consts_ret, *shards)
    return list(res[:3]), list(res[3:6]), list(res[6:])


def _scatter_copies(ins, outs, send_sems, recv_sems):
    x, y, c = _coords()
    me = 4 * x + 2 * y + c
    n = len(ins)
    sends, arrivals = [], []
    local = [pltpu.make_async_copy(ins[w].at[2 * x + y, c], outs[w].at[me], send_sems.at[7 * n + w]) for w in range(n)]
    for w in range(n):
        for k in range(1, 8):
            px, py, pc = x ^ (k >> 2), y ^ ((k >> 1) & 1), c ^ (k & 1)
            sems = dict(send_sem=send_sems.at[7 * w + k - 1], recv_sem=recv_sems.at[7 * w + k - 1],
                        device_id=(px, py, pc), device_id_type=MESH)
            sends.append(pltpu.make_async_remote_copy(src_ref=ins[w].at[2 * px + py, pc], dst_ref=outs[w].at[me], **sems))
            arrivals.append(functools.partial(
                pltpu.make_async_remote_copy, src_ref=ins[w].at[2 * px + py, pc],
                dst_ref=outs[w].at[4 * px + 2 * py + pc], **sems))
    return sends, arrivals, local


_scatter_start, _scatter_wait = _gather_start, _gather_wait


def _grad_sum8(name, got):
    _, R, W = got.shape
    tr = _pick(R, 256, 16)

    def body(g_ref, o_ref):
        total = g_ref[0].astype(F32)
        for d in range(1, 8):
            total = total + g_ref[d].astype(F32)
        o_ref[...] = total

    return pl.pallas_call(
        body, name=name, grid=(R // tr,), in_specs=[pl.BlockSpec((8, tr, W), lambda i: (0, i, 0))],
        out_specs=pl.BlockSpec((tr, W), lambda i: (i, 0)), out_shape=jax.ShapeDtypeStruct((R, W), F32),
        compiler_params=_params(),
    )(got)


def _half_exchange(halves):
    n = len(halves)

    def body(*refs):
        ins, outs, send_sems, recv_sems = refs[:n], refs[n:2 * n], refs[2 * n], refs[2 * n + 1]
        x, y, c = _coords()
        sends = []
        for w in range(n):
            cp = pltpu.make_async_remote_copy(
                src_ref=ins[w], dst_ref=outs[w], send_sem=send_sems.at[w], recv_sem=recv_sems.at[w],
                device_id=(x, y, 1 - c), device_id_type=MESH)
            cp.start()
            sends.append(cp)
        for cp in sends:
            cp.wait()

    got = pl.pallas_call(
        body, name="grad_half_exchange", in_specs=[ANY] * n, out_specs=[ANY] * n,
        out_shape=[jax.ShapeDtypeStruct(h.shape, F32) for h in halves],
        scratch_shapes=[pltpu.SemaphoreType.DMA((n,)), pltpu.SemaphoreType.DMA((n,))],
    )(*halves)
    c = lax.axis_index("c")
    return [jnp.where(c == 0, jnp.stack([mine, theirs]), jnp.stack([theirs, mine])) for mine, theirs in zip(halves, got)]


def _adamw_math(w, g, m, v):
    m2 = ADAM_B1 * m + (1.0 - ADAM_B1) * g
    v2 = ADAM_B2 * v + (1.0 - ADAM_B2) * (g * g)
    m_hat = m2 / (1.0 - ADAM_B1 ** ADAM_STEP)
    v_hat = v2 / (1.0 - ADAM_B2 ** ADAM_STEP)
    return -ADAM_LR * (m_hat / (jnp.sqrt(v_hat) + ADAM_EPS) + ADAM_WD * w), m2, v2


def _small_allreduce_adamw(pack_g, pack_w, pack_m, pack_v):
    def body(g_ref, w_ref, m_ref, v_ref, sum_ref, d_ref, m_out, v_out, land, send_sems, recv_sems):
        x, y, c = _coords()
        me = 4 * x + 2 * y + c
        land[me] = g_ref[...]
        sends = []
        for k in range(1, 8):
            peer = (x ^ (k >> 2), y ^ ((k >> 1) & 1), c ^ (k & 1))
            cp = pltpu.make_async_remote_copy(
                src_ref=g_ref, dst_ref=land.at[me], send_sem=send_sems.at[k - 1], recv_sem=recv_sems.at[k - 1],
                device_id=peer, device_id_type=MESH)
            cp.start()
            sends.append((cp, peer))
        for k, (cp, peer) in enumerate(sends):
            pltpu.make_async_remote_copy(
                src_ref=g_ref, dst_ref=land.at[4 * peer[0] + 2 * peer[1] + peer[2]], send_sem=send_sems.at[k],
                recv_sem=recv_sems.at[k], device_id=peer, device_id_type=MESH).wait_recv()
        for cp, _ in sends:
            cp.wait_send()
        total = land[0]
        for d in range(1, 8):
            total = total + land[d]
        sum_ref[...] = total
        d_ref[...], m_out[...], v_out[...] = _adamw_math(w_ref[...], total, m_ref[...], v_ref[...])

    vm = pl.BlockSpec(memory_space=pltpu.VMEM)
    shp = jax.ShapeDtypeStruct(pack_g.shape, F32)
    return pl.pallas_call(
        body, name="small_allreduce_adamw", in_specs=[vm] * 4, out_specs=[vm] * 4, out_shape=[shp] * 4,
        scratch_shapes=[pltpu.VMEM((8,) + pack_g.shape, F32), pltpu.SemaphoreType.DMA((7,)), pltpu.SemaphoreType.DMA((7,))],
    )(pack_g, pack_w, pack_m, pack_v)


def _adamw(name, w, g, m, v):
    R, C = w.shape
    tr = _pick(R, 256, 8)

    def body(w_ref, g_ref, m_ref, v_ref, d_out, m_out, v_out):
        d_out[...], m_out[...], v_out[...] = _adamw_math(w_ref[...], g_ref[...], m_ref[...], v_ref[...])

    spec = pl.BlockSpec((tr, C), lambda i: (i, 0))
    return pl.pallas_call(
        body, name=name, grid=(R // tr,), in_specs=[spec] * 4, out_specs=[spec] * 3,
        out_shape=[jax.ShapeDtypeStruct((R, C), F32)] * 3, compiler_params=_params(),
    )(w, g, m, v)


def _pack_small(vals, last):
    flat = jnp.concatenate([v.reshape(-1) for v in vals] + [last.reshape(-1)])
    return jnp.pad(flat, (0, SMALL_ROWS * LANES - flat.shape[0])).reshape(SMALL_ROWS, LANES)


def kernel(x, positions, g_mix, w_in, g_q_a, w_q_b, g_kv_a, w_kv_b, g_qn, g_kn, w_mla_out, ret_decay_fwd, ret_decay_bwd, w_ret_out, w_out, g_ffn, w_gate_up, w_down, loss_target, m_g_mix, m_w_in, m_g_q_a, m_w_q_b, m_g_kv_a, m_w_kv_b, m_g_qn, m_g_kn, m_w_mla_out, m_ret_decay_fwd, m_ret_decay_bwd, m_w_ret_out, m_w_out, m_g_ffn, m_w_gate_up, m_w_down, v_g_mix, v_w_in, v_g_q_a, v_w_q_b, v_g_kv_a, v_w_kv_b, v_g_qn, v_g_kn, v_w_mla_out, v_ret_decay_fwd, v_ret_decay_bwd, v_w_ret_out, v_w_out, v_g_ffn, v_w_gate_up, v_w_down):
    given = dict(locals())
    S = x.shape[1]
    xs, tgt = x.reshape(S, D_MODEL), loss_target.reshape(S, D_MODEL)
    pos = positions.reshape(S, 1).astype(F32)

    first_shards = [given[n].astype(BF16) for n in FIRST]
    my_chip = 2 * lax.axis_index("x") + lax.axis_index("y")
    tab_m, tab_r, gathered = _rope_tables_and_first_gather(
        pos, _rope_consts(MLA_NOPE, MLA_ROPE // 2), _rope_consts(0, RET_QK // 2, RET_QK), first_shards)
    wts = {n: _assemble(n, _fill_slot(g, s, my_chip)) for n, g, s in zip(FIRST, gathered, first_shards)}
    late_shards = {n: given[n].astype(BF16) for n in LATE}
    small = {n: given[n].reshape(1, -1) for n in SMALL}

    loss_row, dx, pieces, sgrads = _local_step(xs, tab_m, tab_r, tgt, wts, late_shards, small)

    halves = [_grad_sum8("grad_sum_" + n, got) for n, got in zip(FIRST + LATE, pieces)]
    reduced = _half_exchange(halves)

    out = {}
    for n, r in zip(FIRST + LATE, reduced):
        g = r.reshape(given[n].shape)
        out["grad_" + n] = g
        out["delta_" + n], out["new_m_" + n], out["new_v_" + n] = _adamw("adamw_" + n, given[n], g, given["m_" + n], given["v_" + n])

    one = jnp.ones((1,), F32)
    pk = _small_allreduce_adamw(
        _pack_small([sgrads[n] for n in SMALL], loss_row[0, :1]),
        _pack_small([given[n] for n in SMALL], 0 * one),
        _pack_small([given["m_" + n] for n in SMALL], 0 * one),
        _pack_small([given["v_" + n] for n in SMALL], one))
    off = 0
    for n in SMALL:
        sz = given[n].shape[0]
        for pre, arr in zip(["grad_", "delta_", "new_m_", "new_v_"], pk):
            out[pre + n] = arr.reshape(-1)[off:off + sz]
        off += sz
    loss = pk[0].reshape(-1)[off]

    return (loss, dx.reshape(x.shape), *[out["grad_" + n] for n in WEIGHTS], *[out["delta_" + n] for n in WEIGHTS],
            *[out["new_m_" + n] for n in WEIGHTS], *[out["new_v_" + n] for n in WEIGHTS])
```

```python
import functools
import math

import numpy as np
import jax
import jax.numpy as jnp
from jax import lax
from jax.experimental import pallas as pl
from jax.experimental.pallas import tpu as pltpu

F32 = jnp.float32
BF16 = jnp.bfloat16
MESH = pl.DeviceIdType.MESH

D_MODEL = 1024
HEADS = 8
LANES = 128
MLA_Q_RANK, MLA_KV_RANK = 256, 128
MLA_NOPE, MLA_ROPE, MLA_V = 64, 32, 64
MLA_QK = MLA_NOPE + MLA_ROPE
LN2 = math.log(2.0)
MLA_Q_SCALE = MLA_QK ** -0.5 / LN2
RET_QK, RET_V, RET_CHUNK = 64, 128, 128
RET_QK_DTYPE = BF16
RET_CHUNKS_PER_STEP = 2
FFN_HIDDEN = 2816
ROPE_THETA = 10000.0
EPS = 1e-6
IN_SPLITS = [256, 128, 32, 512, 512, 1024, 1024, 2048]
IN_OFFS = [0] + list(np.cumsum(IN_SPLITS))
ADAM_LR, ADAM_B1, ADAM_B2, ADAM_EPS, ADAM_WD, ADAM_STEP = 0.001, 0.9, 0.999, 1e-08, 0.01, 10

VMEM_LIMIT = 56 * 1024 * 1024
ROW_TILE = 256
HEAD_ROW_TILE = 2048
MM_TM, MM_TN, MM_TK, MM_KFULL = 1408, 2048, 2048, 2816
ATT_TQ = 256
ATT_BQ, ATT_BK = 512, 2048
ATT_HEADS_PER_STEP = 8
ATT_BWD_HEADS_PER_STEP = 4

SHARDED = ["w_in", "w_q_b", "w_kv_b", "w_mla_out", "w_ret_out", "w_out", "w_gate_up", "w_down"]
COL_SHARDED = {"w_in", "w_q_b", "w_kv_b", "w_mla_out", "w_gate_up"}
FIRST = ["w_in", "w_q_b", "w_kv_b"]
LATE = ["w_mla_out", "w_ret_out", "w_out", "w_gate_up", "w_down"]
SMALL = ["g_mix", "g_q_a", "g_kv_a", "g_qn", "g_kn", "ret_decay_fwd", "ret_decay_bwd", "g_ffn"]
WEIGHTS = ["g_mix", "w_in", "g_q_a", "w_q_b", "g_kv_a", "w_kv_b", "g_qn", "g_kn", "w_mla_out",
           "ret_decay_fwd", "ret_decay_bwd", "w_ret_out", "w_out", "g_ffn", "w_gate_up", "w_down"]
SMALL_ROWS = 24


def _params(**kw):
    return pltpu.CompilerParams(vmem_limit_bytes=VMEM_LIMIT, **kw)


def _pick(dim, target, unit=128):
    if dim <= target:
        return dim
    best = None
    for d in range(unit, target + 1, unit):
        if dim % d == 0:
            best = d
    assert best is not None, (dim, target)
    return best


_DOT = {"nn": (((1,), (0,)), ((), ())), "nt": (((1,), (1,)), ((), ())), "tn": (((0,), (0,)), ((), ()))}


def _dot(a, b, mode="nn"):
    return lax.dot_general(a, b, _DOT[mode], preferred_element_type=F32)


def _rms_rows(x, g):
    x = x.astype(F32)
    return x * lax.rsqrt(jnp.mean(x * x, axis=-1, keepdims=True) + EPS) * g


def _epi_loss(acc, extras, params):
    e = acc + extras[0] - extras[1]
    dy = e * (1.0 / D_MODEL)
    loss = 0.5 * jnp.sum(jnp.mean(e * e, axis=-1, keepdims=True), axis=0, keepdims=True)
    return [dy, dy], [jnp.broadcast_to(loss, (1, LANES))]


def _epi_rms_bwd(n_out):
    def fn(acc, extras, params):
        _, vjp = jax.vjp(_rms_rows, extras[0], params[0])
        dx, dg = vjp(acc)
        return [dx + extras[1]] * n_out, [dg]
    return fn


def _mm(name, a, b, mode, out_dtype, res=None, a_gain=None, epilogue=None, shard_out=False, scatter=None):
    if mode == "nn":
        (M, K), (K2, N) = a.shape, b.shape
    elif mode == "nt":
        (M, K), (N, K2) = a.shape, b.shape
    else:
        (K, M), (K2, N) = a.shape, b.shape
    assert K == K2, (name, a.shape, b.shape)
    tm, tn = _pick(M, MM_TM), _pick(N, MM_TN)
    tk = K if K <= MM_KFULL else _pick(K, MM_TK)
    if shard_out:
        tm, tn = M // 2, N // 4
    if epilogue is not None:
        tm = _pick(M, MM_TM // 2)
    nk = K // tk
    cache_a = a_gain is not None
    if a_gain is not None:
        assert mode == "nn" and tk == K and epilogue is None and not shard_out, name
    n_in = 2 + (res is not None) + (a_gain is not None)
    extras, eparams, e_outs, e_sums = ([], [], [], [])
    if epilogue is not None:
        assert tn == N and res is None and not shard_out, name
        epi_fn, extras, eparams, e_outs, e_sums = epilogue
    n_out = len(e_outs) + len(e_sums) if epilogue is not None else 1 + cache_a
    scatter = list(scatter or [])
    n_sc = len(scatter)
    assert not n_sc or epilogue is not None, name
    ni, nj = M // tm, N // tn

    def body(*refs):
        a_ref, b_ref = refs[0], refs[1]
        base = n_in + len(extras) + len(eparams)
        ex_refs = refs[n_in:n_in + len(extras)]
        ep_refs = refs[n_in + len(extras):base]
        sc_in, out_refs = refs[base:base + n_sc], refs[base + n_sc:base + n_sc + n_out]
        sc_out = refs[base + n_sc + n_out:base + 2 * n_sc + n_out]
        scratch = refs[base + 2 * n_sc + n_out:]
        acc = scratch[0]
        i, j, k = pl.program_id(0), pl.program_id(1), pl.program_id(2)

        if n_sc:
            @pl.when(jnp.logical_and(i == 0, jnp.logical_and(j == 0, k == 0)))
            def _():
                _scatter_start(_scatter_copies(sc_in, sc_out, scratch[-2], scratch[-1]))

        @pl.when(k == 0)
        def _():
            acc[...] = jnp.zeros_like(acc)

        if cache_a:
            @pl.when(j == 0)
            def _():
                out_refs[1][...] = _rms_rows(a_ref[...], refs[n_in - 1][...]).astype(BF16)
            av = out_refs[1][...]
        else:
            av = a_ref[...].astype(BF16)
        acc[...] += _dot(av, b_ref[...].astype(BF16), mode)

        @pl.when(k == nk - 1)
        def _():
            if epilogue is None:
                r = acc[...]
                if res is not None:
                    r = r + refs[2][...].astype(F32)
                out_refs[0][...] = r.astype(out_refs[0].dtype).reshape(out_refs[0].shape)
            else:
                vals, sums = epi_fn(acc[...], [r[...] for r in ex_refs], [p[...] for p in ep_refs])
                for o_ref, v in zip(out_refs, vals):
                    o_ref[...] = v.astype(o_ref.dtype)
                for s_ref, v in zip(out_refs[len(vals):], sums):
                    @pl.when(i == 0)
                    def _(s_ref=s_ref):
                        s_ref[...] = jnp.zeros_like(s_ref)
                    s_ref[...] += v

        if n_sc:
            @pl.when(jnp.logical_and(i == ni - 1, jnp.logical_and(j == nj - 1, k == nk - 1)))
            def _():
                _scatter_wait(_scatter_copies(sc_in, sc_out, scratch[-2], scratch[-1]))

    a_spec = pl.BlockSpec((tk, tm), lambda i, j, k: (k, i)) if mode == "tn" else pl.BlockSpec((tm, tk), lambda i, j, k: (i, k))
    b_spec = pl.BlockSpec((tn, tk), lambda i, j, k: (j, k)) if mode == "nt" else pl.BlockSpec((tk, tn), lambda i, j, k: (k, j))
    o_spec = pl.BlockSpec((tm, tn), lambda i, j, k: (i, j))
    const = lambda p: pl.BlockSpec(p.shape, lambda i, j, k: (0,) * p.ndim)
    ins, specs = [a, b], [a_spec, b_spec]
    if res is not None:
        ins.append(res)
        specs.append(o_spec)
    if a_gain is not None:
        ins.append(a_gain)
        specs.append(const(a_gain))
    ins += list(extras) + list(eparams)
    specs += [o_spec] * len(extras) + [const(p) for p in eparams]
    if epilogue is not None:
        out_specs = [o_spec] * len(e_outs) + [pl.BlockSpec(s, lambda i, j, k: (0, 0)) for s in e_sums]
        out_shape = [jax.ShapeDtypeStruct((M, N), dt) for dt in e_outs] + [jax.ShapeDtypeStruct(s, F32) for s in e_sums]
    elif shard_out:
        out_specs = pl.BlockSpec((1, 1, tm, tn), lambda i, j, k: (j, i, 0, 0))
        out_shape = jax.ShapeDtypeStruct((4, 2, tm, tn), out_dtype)
    elif cache_a:
        out_specs = [o_spec, pl.BlockSpec((tm, K), lambda i, j, k: (i, 0))]
        out_shape = [jax.ShapeDtypeStruct((M, N), out_dtype), jax.ShapeDtypeStruct((M, K), BF16)]
    else:
        out_specs, out_shape = o_spec, jax.ShapeDtypeStruct((M, N), out_dtype)
    scratch_shapes = [pltpu.VMEM((tm, tn), F32)]
    if n_sc:
        ins += scatter
        specs += [ANY] * n_sc
        out_specs = list(out_specs) + [ANY] * n_sc
        out_shape = list(out_shape) + [jax.ShapeDtypeStruct((8,) + g.shape[2:], g.dtype) for g in scatter]
        scratch_shapes += [pltpu.SemaphoreType.DMA((8 * n_sc,)), pltpu.SemaphoreType.DMA((7 * n_sc,))]
    res_ = pl.pallas_call(
        body, name=name, grid=(ni, nj, nk), in_specs=specs, out_specs=out_specs, out_shape=out_shape,
        scratch_shapes=scratch_shapes, compiler_params=_params(),
    )(*ins)
    if n_sc:
        return list(res_[:n_out]) + [list(res_[n_out:])]
    return res_


def _piece_spec(tm, piece):
    _, w, c0, per_group = piece
    if per_group:
        return pl.BlockSpec((tm, w), lambda i, g: (i, c0 + g))
    return pl.BlockSpec((tm, w), lambda i, g: (i, c0))


def _const_spec(p):
    return pl.BlockSpec(p.shape, lambda i, g: (0, 0))


def _place_into(into, S, tm, out_specs, out_shape, n_inputs):
    if into is None:
        return [], [], {}
    k, buf, total, c0 = into
    (_, w), per_group = out_specs[k].block_shape, out_shape[k].shape[1] != out_specs[k].block_shape[1]
    out_specs[k] = _piece_spec(tm, (None, w, c0, per_group))
    out_shape[k] = jax.ShapeDtypeStruct((S, total), out_shape[k].dtype)
    if buf is None:
        return [], [], {}
    return [buf], [pl.BlockSpec(memory_space=pl.ANY)], {n_inputs: k}


def _rowwise(name, fn, params, rows, auxs, outs, tm, groups=1, into=None):
    S = rows[0][0].shape[0]
    tm = min(tm, S)
    n_p, n_r, n_a = len(params), len(rows), len(auxs)
    n_buf = int(into is not None and into[1] is not None)

    def body(*refs):
        p = [r[...] for r in refs[:n_p]]
        r_ = [r[...] for r in refs[n_p:n_p + n_r]]
        a_ = [r[...] for r in refs[n_p + n_r:n_p + n_r + n_a]]
        for o_ref, o in zip(refs[n_p + n_r + n_a + n_buf:], fn(p, r_, a_)):
            o_ref[...] = o.astype(o_ref.dtype)

    out_specs, out_shape = [], []
    for w, dt, per_group in outs:
        out_specs.append(_piece_spec(tm, (None, w, 0, per_group)))
        out_shape.append(jax.ShapeDtypeStruct((S, w * (groups if per_group else 1)), dt))
    bufs, buf_specs, aliases = _place_into(into, S, tm, out_specs, out_shape, n_p + n_r + n_a)
    return pl.pallas_call(
        body, name=name, grid=(S // tm, groups),
        in_specs=[_const_spec(p) for p in params] + [_piece_spec(tm, q) for q in list(rows) + list(auxs)] + buf_specs,
        out_specs=out_specs, out_shape=out_shape, input_output_aliases=aliases, compiler_params=_params(),
    )(*params, *[q[0] for q in list(rows) + list(auxs)], *bufs)


def _rowwise_vjp(name, fn, params, rows, auxs, cots, d_outs, tm, groups=1, adds=None, into=None):
    S = rows[0][0].shape[0]
    tm = min(tm, S)
    n_p, n_r, n_a = len(params), len(rows), len(auxs)
    cot_flat = [q for c in cots for q in c]
    adds = adds or [None] * len(d_outs)
    add_flat = [q for q in adds if q is not None]
    n_c, n_add = len(cot_flat), len(add_flat)
    n_buf = int(into is not None and into[1] is not None)
    shared = [not all(rows[k][3] for k in idx) and groups > 1 for idx, _ in d_outs]

    def body(*refs):
        pos = 0
        p = [r[...] for r in refs[pos:pos + n_p]]; pos += n_p
        r_ = [r[...] for r in refs[pos:pos + n_r]]; pos += n_r
        a_ = [r[...] for r in refs[pos:pos + n_a]]; pos += n_a
        c_refs = refs[pos:pos + n_c]; pos += n_c
        add_refs = list(refs[pos:pos + n_add]); pos += n_add + n_buf
        d_refs = refs[pos:pos + len(d_outs)]; pos += len(d_outs)
        dp_refs = refs[pos:]
        i, g = pl.program_id(0), pl.program_id(1)
        outs, vjp_fn = jax.vjp(lambda pp, rr: fn(pp, rr, a_), p, r_)
        cts, ci = [], 0
        for c, o in zip(cots, outs):
            t = c_refs[ci][...].astype(F32)
            for extra in c_refs[ci + 1:ci + len(c)]:
                t = t + extra[...].astype(F32)
            ci += len(c)
            cts.append(t.astype(o.dtype))
        dp, dr = vjp_fn(cts)
        for (idx, _), d_ref, add, sh in zip(d_outs, d_refs, adds, shared):
            val = dr[idx[0]].astype(F32) if len(idx) == 1 else jnp.concatenate([dr[k].astype(F32) for k in idx], axis=1)
            if add is not None:
                val = val + add_refs.pop(0)[...].astype(F32)
            if sh:
                @pl.when(g == 0)
                def _(d_ref=d_ref):
                    d_ref[...] = jnp.zeros_like(d_ref)
                d_ref[...] += val.astype(d_ref.dtype)
            else:
                d_ref[...] = val.astype(d_ref.dtype)
        first = jnp.logical_and(i == 0, g == 0)
        for dp_ref, d in zip(dp_refs, dp):
            @pl.when(first)
            def _(dp_ref=dp_ref):
                dp_ref[...] = jnp.zeros_like(dp_ref)
            dp_ref[...] += d.astype(F32)

    out_specs, out_shape = [], []
    for (idx, dt), sh in zip(d_outs, shared):
        w = sum(rows[k][1] for k in idx)
        per_group = (not sh) and groups > 1
        out_specs.append(_piece_spec(tm, (None, w, 0, per_group)))
        out_shape.append(jax.ShapeDtypeStruct((S, w * (groups if per_group else 1)), dt))
    for p in params:
        out_specs.append(_const_spec(p))
        out_shape.append(jax.ShapeDtypeStruct(p.shape, F32))
    pieces = list(rows) + list(auxs) + cot_flat + add_flat
    bufs, buf_specs, aliases = _place_into(into, S, tm, out_specs, out_shape, n_p + len(pieces))
    res = pl.pallas_call(
        body, name=name, grid=(S // tm, groups),
        in_specs=[_const_spec(p) for p in params] + [_piece_spec(tm, q) for q in pieces] + buf_specs,
        out_specs=out_specs, out_shape=out_shape, input_output_aliases=aliases, compiler_params=_params(),
    )(*params, *[q[0] for q in pieces], *bufs)
    return list(res[:len(d_outs)]), list(res[len(d_outs):])


def _lane_roll(x, shift):
    @jax.custom_vjp
    def roll(v):
        return pltpu.roll(v, shift, 1)

    roll.defvjp(lambda v: (roll(v), None), lambda _, ct: (pltpu.roll(ct, LANES - shift, 1),))
    return roll(x)


@jax.custom_vjp
def _sigmoid(x):
    return 1.0 / (1.0 + jnp.exp(-x))


def _sigmoid_fwd(x):
    s = _sigmoid(x)
    return s, s


_sigmoid.defvjp(_sigmoid_fwd, lambda s, ct: (ct * s * (1.0 - s),))


def _rope(x, cos, sin_lo, sin_hi, half):
    return x * cos + _lane_roll(x, LANES - half) * sin_lo + _lane_roll(x, half) * sin_hi


def _f_rope_table(p, r, a):
    inv, first, second, fixed = p
    ang = a[0] * inv
    cs, sn = jnp.cos(ang), jnp.sin(ang)
    return [cs * (first + second) + fixed, -sn * first, sn * second]


def _f_rms(p, r, a):
    x = r[0].astype(F32)
    return [x * lax.rsqrt(jnp.mean(x * x, axis=-1, keepdims=True) + EPS) * p[0]]


def _f_mla_a(p, r, a):
    return _f_rms([p[0]], [r[0]], a) + _f_rms([p[1]], [r[1]], a)


def _f_mla_b(p, r, a):
    def norm_rope(v, g):
        ms = jnp.sum(v * v, axis=-1, keepdims=True) * (1.0 / MLA_QK)
        return _rope(v * lax.rsqrt(ms + EPS) * g, a[0], a[1], a[2], MLA_ROPE // 2)

    return [norm_rope(r[0].astype(F32), p[0]) * MLA_Q_SCALE, norm_rope(r[1].astype(F32) + r[2].astype(F32), p[1])]


def _f_ret_rope_q(p, r, a):
    return [_rope(r[0].astype(F32), a[0], a[1], a[2], RET_QK // 2)]


def _f_ret_rope_k(p, r, a):
    return [_rope(r[0].astype(F32), a[0], a[1], a[2], RET_QK // 2) * (RET_QK ** -0.5)]


def _f_ret_rope(p, r, a):
    return _f_ret_rope_q(p, r[:1], a) + _f_ret_rope_k(p, r[1:], a)


def _f_ret_post(p, r, a):
    ret = r[0].astype(F32) + r[1].astype(F32)
    g = r[2].astype(F32)
    normed = ret * lax.rsqrt(jnp.mean(ret * ret, axis=-1, keepdims=True) + EPS)
    return [g * _sigmoid(g) * normed]


def _f_merge(p, r, a):
    return [_sigmoid(r[0].astype(F32)) * r[2].astype(F32) + _sigmoid(r[1].astype(F32)) * r[3].astype(F32)]


def _f_swiglu(p, r, a):
    g = r[0].astype(F32)
    return [g * _sigmoid(g) * r[1].astype(F32)]


def _f_add(p, r, a):
    return [r[0].astype(F32) + r[1].astype(F32)]


def _flash_fwd(q, k, kv, shards):
    S = q.shape[0]
    tq = min(ATT_TQ, S)
    nq = S // tq
    n = len(shards)

    def body(q_ref, k_ref, v_ref, *rest):
        shard_refs, (o_ref, lse_ref), gathered = rest[:n], rest[n:n + 2], rest[n + 2:2 * n + 2]
        send_sems, recv_sems = rest[2 * n + 2:]
        h, qi = pl.program_id(0), pl.program_id(1)

        @pl.when(jnp.logical_and(h == 0, qi == 0))
        def _():
            _gather_start(_gather_copies(shard_refs, gathered, send_sems, recv_sems))

        for hh in range(hps):
            lanes = slice(hh * LANES, (hh + 1) * LANES)
            s = _dot(q_ref[:, lanes], k_ref[:, lanes], "nt")
            m = jnp.max(s, axis=-1, keepdims=True)
            p = jnp.exp2(s - m)
            l = jnp.sum(p, axis=-1, keepdims=True)
            pair = slice(hh // 2 * LANES, (hh // 2 + 1) * LANES)
            o_h = (_dot(p.astype(BF16), v_ref[:, pair]) / l * _ret_head_mask(hh)).astype(o_ref.dtype)
            if hh % 2 == 0:
                o_ref[:, pair] = o_h
            else:
                o_ref[:, pair] += o_h
            lse_ref[:, lanes] = jnp.broadcast_to(m + jnp.log2(l), (tq, LANES))

        @pl.when(jnp.logical_and(h == HEADS // hps - 1, qi == nq - 1))
        def _():
            _gather_wait(_gather_copies(shard_refs, gathered, send_sems, recv_sems))

    hps = ATT_HEADS_PER_STEP
    qs = pl.BlockSpec((tq, hps * LANES), lambda h, i: (i, h))
    vw = hps * MLA_V
    v0 = HEADS * LANES // vw
    res = pl.pallas_call(
        body, name="mla_fwd", grid=(HEADS // hps, nq),
        in_specs=[qs, pl.BlockSpec((S, hps * LANES), lambda h, i: (0, h), pipeline_mode=pl.Buffered(1)),
                  pl.BlockSpec((S, vw), lambda h, i: (0, v0 + h), pipeline_mode=pl.Buffered(1))]
        + [ANY] * n,
        out_specs=[pl.BlockSpec((tq, vw), lambda h, i: (i, h)), qs] + [ANY] * n,
        out_shape=[jax.ShapeDtypeStruct((S, HEADS * MLA_V), BF16), jax.ShapeDtypeStruct((S, HEADS * LANES), F32)]
        + [jax.ShapeDtypeStruct((4,) + s.shape, s.dtype) for s in shards],
        scratch_shapes=[pltpu.SemaphoreType.DMA((3 * n,)), pltpu.SemaphoreType.DMA((3 * n,))],
        compiler_params=_params(),
    )(q, k, kv, *shards)
    mine = 2 * lax.axis_index("x") + lax.axis_index("y")
    return res[0], res[1], [_fill_slot(g, s, mine) for g, s in zip(res[2:], shards)]


def _flash_bwd(q, k, kv, do, lse, o, gs):
    S = q.shape[0]
    tq, tk = min(ATT_BQ, S), min(ATT_BK, S)
    nq, nkt = S // tq, S // tk
    n = len(gs)

    def body(q_ref, k_ref, v_ref, do_ref, lse_ref, o_ref, *rest):
        g_refs, (dq_ref, dk_ref, dv_ref), got_refs = rest[:n], rest[n:n + 3], rest[n + 3:2 * n + 3]
        dk_sc, dv_sc, send_sems, recv_sems = rest[2 * n + 3:]
        h, ki, qi = pl.program_id(0), pl.program_id(1), pl.program_id(2)

        @pl.when(jnp.logical_and(h == 0, jnp.logical_and(ki == 0, qi == 0)))
        def _():
            _scatter_start(_scatter_copies(g_refs, got_refs, send_sems, recv_sems))

        @pl.when(jnp.logical_and(ki == 0, qi == 0))
        def _():
            dq_ref[...] = jnp.zeros_like(dq_ref)

        @pl.when(qi == 0)
        def _():
            dk_sc[...] = jnp.zeros_like(dk_sc)
            dv_sc[...] = jnp.zeros_like(dv_sc)

        rows = pl.ds(pl.multiple_of(qi * tq, tq), tq)
        for hh in range(hps):
            lanes = slice(hh * LANES, (hh + 1) * LANES)
            pair = slice(hh // 2 * LANES, (hh // 2 + 1) * LANES)
            qv, kv_ = q_ref[:, lanes], k_ref[:, lanes]
            do32 = do_ref[:, pair].astype(F32) * _ret_head_mask(hh)
            dov = do32.astype(BF16)
            p = jnp.exp2(_dot(qv, kv_, "nt") - lse_ref[:, lanes][:, :1])
            dp = _dot(dov, v_ref[:, pair], "nt")
            delta = jnp.sum(do32 * o_ref[:, pair].astype(F32), axis=-1, keepdims=True)
            ds = (p * (dp - delta) * LN2).astype(BF16)
            dv_sc[:, pair] += _dot(p.astype(BF16), dov, "tn")
            dk_sc[:, lanes] += _dot(ds, qv, "tn")
            dq_ref[rows, lanes] += _dot(ds, kv_)

        @pl.when(qi == nq - 1)
        def _():
            dk_ref[...] = dk_sc[...].astype(dk_ref.dtype)
            dv_ref[...] = dv_sc[...].astype(dv_ref.dtype)

        @pl.when(jnp.logical_and(h == HEADS // hps - 1, jnp.logical_and(ki == nkt - 1, qi == nq - 1)))
        def _():
            _scatter_wait(_scatter_copies(g_refs, got_refs, send_sems, recv_sems))

    hps = ATT_BWD_HEADS_PER_STEP
    qs = pl.BlockSpec((tq, hps * LANES), lambda h, j, i: (i, h))
    ks = pl.BlockSpec((tk, hps * LANES), lambda h, j, i: (j, h))
    vw = hps * MLA_V
    v0 = HEADS * LANES // vw
    qv_s = pl.BlockSpec((tq, vw), lambda h, j, i: (i, h))
    kv_s = pl.BlockSpec((tk, vw), lambda h, j, i: (j, h))
    res = pl.pallas_call(
        body, name="mla_bwd", grid=(HEADS // hps, nkt, nq),
        in_specs=[qs, ks, pl.BlockSpec((tk, vw), lambda h, j, i: (j, v0 + h)), qv_s, qs, qv_s] + [ANY] * n,
        out_specs=[pl.BlockSpec((S, hps * LANES), lambda h, j, i: (0, h), pipeline_mode=pl.Buffered(1)), ks, kv_s] + [ANY] * n,
        out_shape=[jax.ShapeDtypeStruct((S, HEADS * LANES), F32), jax.ShapeDtypeStruct((S, HEADS * LANES), BF16),
                   jax.ShapeDtypeStruct((S, HEADS * MLA_V), BF16)]
        + [jax.ShapeDtypeStruct((8,) + g.shape[2:], g.dtype) for g in gs],
        scratch_shapes=[pltpu.VMEM((tk, hps * LANES), F32), pltpu.VMEM((tk, vw), F32)]
        + [pltpu.SemaphoreType.DMA((8 * n,)), pltpu.SemaphoreType.DMA((7 * n,))],
        compiler_params=_params(),
    )(q, k, kv, do, lse, o, *gs)
    return res[0], res[1], res[2], list(res[3:])


def _ret_tables(decay_row, backward):
    C = RET_CHUNK
    lg = -jnp.exp(decay_row)
    t = lax.broadcasted_iota(jnp.int32, (C, C), 0).astype(F32)
    s = lax.broadcasted_iota(jnp.int32, (C, C), 1).astype(F32)
    ridx = lax.broadcasted_iota(jnp.int32, (C, LANES), 0).astype(F32)
    if backward:
        dist, mask, aw, bw = s - t, s > t, C - ridx, ridx
    else:
        dist, mask, aw, bw = t - s, t >= s, ridx + 1.0, C - 1.0 - ridx
    dist = jnp.maximum(dist, 0.0)
    din = jnp.where(mask, jnp.exp(lg[:, :1] * dist), 0.0)
    return dict(din=din, dist=dist, a=jnp.exp(lg * aw), b=jnp.exp(lg * bw), c=jnp.exp(lg * C), aw=aw, bw=bw)


def _ret_fill_tables(decs, din_sc, a_sc, b_sc):
    for d, dec in enumerate(decs):
        for h in range(HEADS):
            tb = _ret_tables(dec[h:h + 1, :], d == 1)
            din_sc[d, h], a_sc[d, h], b_sc[d, h] = tb["din"], tb["a"], tb["b"]


def _ret_head_mask(h):
    lane = lax.broadcasted_iota(jnp.int32, (1, LANES), 1)
    return jnp.where((lane >= RET_QK) == bool(h % 2), 1.0, 0.0).astype(F32)


def _ret_fwd(qr, kr, proj, v_block, dec_f, dec_b):
    S = qr.shape[0]
    C = RET_CHUNK
    n = S // C
    nc = min(RET_CHUNKS_PER_STEP, n)
    nb = n // nc
    W = HEADS * LANES

    def body(qf, kf, vf, qb, kb, vb, df, db, of, ob, sf_out, sb_out, st, din_sc, a_sc, b_sc):
        @pl.when(pl.program_id(0) == 0)
        def _():
            st[...] = jnp.zeros_like(st)
            _ret_fill_tables((df, db), din_sc, a_sc, b_sc)

        for d, (q_ref, k_ref, v_ref, dec, o_ref, s_out) in enumerate(
                [(qf, kf, vf, df, of, sf_out), (qb, kb, vb, db, ob, sb_out)]):
            for h in range(HEADS):
                lanes, pair = slice(h * LANES, (h + 1) * LANES), slice(h // 2 * LANES, (h // 2 + 1) * LANES)
                mine = _ret_head_mask(h)
                din, a, b = din_sc[d, h], a_sc[d, h], b_sc[d, h]
                c = jnp.exp(-jnp.exp(dec[h:h + 1, :]) * C)
                for ci in (range(nc) if d == 0 else reversed(range(nc))):
                    rows = slice(ci * C, (ci + 1) * C)
                    qf32, kf32 = q_ref[rows, pair].astype(F32) * mine, k_ref[rows, pair].astype(F32) * mine
                    v = v_ref[rows, lanes]
                    state = st[d, h]
                    s_out[ci, h] = state
                    inner = _dot((_dot(qf32.astype(BF16), kf32.astype(BF16), "nt") * din).astype(BF16), v)
                    cross = _dot((qf32 * a).astype(BF16), state.astype(BF16))
                    o_ref[rows, lanes] = inner + cross
                    st[d, h] = state * c + _dot((kf32 * b).astype(BF16), v, "tn")

    fw = lambda c0, w=W: pl.BlockSpec((nc * C, w), lambda j: (j, c0))
    bw = lambda c0, w=W: pl.BlockSpec((nc * C, w), lambda j: (nb - 1 - j, c0))
    dec_spec = pl.BlockSpec((HEADS, LANES), lambda j: (0, 0))
    st_shape = jax.ShapeDtypeStruct((n, HEADS, LANES, LANES), F32)
    QW = W // 2
    return pl.pallas_call(
        body, name="ret_fwd", grid=(nb,),
        in_specs=[fw(0, QW), fw(0, QW), fw(v_block), bw(0, QW), bw(0, QW), bw(v_block), dec_spec, dec_spec],
        out_specs=[fw(0), bw(0), pl.BlockSpec((nc, HEADS, LANES, LANES), lambda j: (j, 0, 0, 0)),
                   pl.BlockSpec((nc, HEADS, LANES, LANES), lambda j: (nb - 1 - j, 0, 0, 0))],
        out_shape=[jax.ShapeDtypeStruct((S, W), F32)] * 2 + [st_shape] * 2,
        scratch_shapes=[pltpu.VMEM((2, HEADS, LANES, LANES), F32), pltpu.VMEM((2, HEADS, C, C), F32),
                        pltpu.VMEM((2, HEADS, C, LANES), F32), pltpu.VMEM((2, HEADS, C, LANES), F32)],
        compiler_params=_params(),
    )(qr, kr, proj, qr, kr, proj, dec_f, dec_b)


def _ret_bwd(qr, kr, proj, v_block, dret, sf, sb, dec_f, dec_b):
    S = qr.shape[0]
    C = RET_CHUNK
    n = S // C
    nc = min(RET_CHUNKS_PER_STEP, n)
    nb = n // nc
    W = HEADS * LANES

    def body(qf, kf, vf, gf, sf_ref, qb, kb, vb, gb, sb_ref, df, db,
             dqf, dkf, dvf, dqb, dkb, dvb, ddf, ddb, ds_sc, din_sc, a_sc, b_sc):
        j = pl.program_id(0)

        @pl.when(j == 0)
        def _():
            ds_sc[...] = jnp.zeros_like(ds_sc)
            ddf[...] = jnp.zeros_like(ddf)
            ddb[...] = jnp.zeros_like(ddb)
            _ret_fill_tables((df, db), din_sc, a_sc, b_sc)

        for d, (q_ref, k_ref, v_ref, g_ref, s_ref, dec, dq_ref, dk_ref, dv_ref, dd_ref) in enumerate(
                [(qf, kf, vf, gf, sf_ref, df, dqf, dkf, dvf, ddf), (qb, kb, vb, gb, sb_ref, db, dqb, dkb, dvb, ddb)]):
            static = _ret_tables(dec[0:1, :], d == 1)
            dist, aw, bw_ = static["dist"], static["aw"], static["bw"]
            for h in range(HEADS):
                lanes, pair = slice(h * LANES, (h + 1) * LANES), slice(h // 2 * LANES, (h // 2 + 1) * LANES)
                mine = _ret_head_mask(h)
                din, a, b = din_sc[d, h], a_sc[d, h], b_sc[d, h]
                c = jnp.exp(-jnp.exp(dec[h:h + 1, :]) * C)
                dlg = jnp.zeros((1, 1), F32)
                for ci in (reversed(range(nc)) if d == 0 else range(nc)):
                    rows = slice(ci * C, (ci + 1) * C)
                    v, g = v_ref[rows, lanes], g_ref[rows, lanes]
                    qf32, kf32 = q_ref[rows, pair].astype(F32) * mine, k_ref[rows, pair].astype(F32) * mine
                    q, k = qf32.astype(BF16), kf32.astype(BF16)
                    state, dstate = s_ref[ci, h], ds_sc[d, h]
                    dstate_b = dstate.astype(BF16)
                    dp = _dot(g, v, "nt")
                    a_ = _dot(q, k, "nt")
                    da = (dp * din).astype(BF16)
                    g1 = _dot(g, state.astype(BF16), "nt")
                    g2 = _dot(v, dstate_b, "nt")
                    dq_h = (_dot(da, k) + g1 * a).astype(dq_ref.dtype)
                    dk_h = (_dot(da, q, "tn") + g2 * b).astype(dk_ref.dtype)
                    if h % 2 == 0:
                        dq_ref[rows, pair], dk_ref[rows, pair] = dq_h, dk_h
                    else:
                        dq_ref[rows, pair] += dq_h
                        dk_ref[rows, pair] += dk_h
                    dv_ref[rows, lanes] = (_dot((a_ * din).astype(BF16), g, "tn")
                                           + _dot((kf32 * b).astype(BF16), dstate_b)).astype(dv_ref.dtype)
                    dlg = dlg + (jnp.sum(dp * a_ * din * dist, keepdims=True)
                                 + jnp.sum(g1 * qf32 * a * aw, keepdims=True)
                                 + jnp.sum(g2 * kf32 * b * bw_, keepdims=True)
                                 + C * jnp.sum(c * dstate * state, keepdims=True))
                    ds_sc[d, h] = dstate * c + _dot((qf32 * a).astype(BF16), g, "tn")
                dd_ref[h:h + 1, :] += jnp.broadcast_to(dlg, (1, LANES))

        @pl.when(j == nb - 1)
        def _():
            ddf[...] = ddf[...] * -jnp.exp(df[...])
            ddb[...] = ddb[...] * -jnp.exp(db[...])

    fw = lambda c0, w=W: pl.BlockSpec((nc * C, w), lambda j: (nb - 1 - j, c0))
    bw = lambda c0, w=W: pl.BlockSpec((nc * C, w), lambda j: (j, c0))
    dec_spec = pl.BlockSpec((HEADS, LANES), lambda j: (0, 0))
    QW = W // 2
    act, act_qk = jax.ShapeDtypeStruct((S, W), BF16), jax.ShapeDtypeStruct((S, QW), BF16)
    return pl.pallas_call(
        body, name="ret_bwd", grid=(nb,),
        in_specs=[fw(0, QW), fw(0, QW), fw(v_block), fw(0),
                  pl.BlockSpec((nc, HEADS, LANES, LANES), lambda j: (nb - 1 - j, 0, 0, 0)),
                  bw(0, QW), bw(0, QW), bw(v_block), bw(0), pl.BlockSpec((nc, HEADS, LANES, LANES), lambda j: (j, 0, 0, 0)),
                  dec_spec, dec_spec],
        out_specs=[fw(0, QW), fw(0, QW), fw(0), bw(0, QW), bw(0, QW), bw(0)] + [dec_spec] * 2,
        out_shape=[act_qk, act_qk, act, act_qk, act_qk, act] + [jax.ShapeDtypeStruct((HEADS, LANES), F32)] * 2,
        scratch_shapes=[pltpu.VMEM((2, HEADS, LANES, LANES), F32), pltpu.VMEM((2, HEADS, C, C), F32),
                        pltpu.VMEM((2, HEADS, C, LANES), F32), pltpu.VMEM((2, HEADS, C, LANES), F32)],
        compiler_params=_params(),
    )(qr, kr, proj, dret, sf, qr, kr, proj, dret, sb, dec_f, dec_b)


def _pad_heads(w, hd):
    K = w.shape[0]
    return jnp.pad(w.reshape(K, HEADS, hd), ((0, 0), (0, 0), (0, LANES - hd))).reshape(K, HEADS * LANES)


def _unpad_heads(w, hd):
    K = w.shape[0]
    return w.reshape(K, HEADS, LANES)[:, :, :hd].reshape(K, HEADS * hd)


def _rope_consts(first_lane, half, period=LANES):
    lane = np.arange(LANES) % period
    first = ((lane >= first_lane) & (lane < first_lane + half)).astype(np.float32)
    second = ((lane >= first_lane + half) & (lane < first_lane + 2 * half)).astype(np.float32)
    fixed = (lane < first_lane).astype(np.float32)
    j = np.where(first > 0, lane - first_lane, lane - first_lane - half) * (first + second)
    inv = (ROPE_THETA ** (-j.astype(np.float64) / half)).astype(np.float32)
    return [jnp.asarray(v.reshape(1, LANES), F32) for v in (inv, first, second, fixed)]


def _assemble(name, gathered):
    if name in COL_SHARDED:
        return jnp.transpose(gathered, (1, 0, 2)).reshape(gathered.shape[1], 4 * gathered.shape[2])
    return gathered.reshape(4 * gathered.shape[1], gathered.shape[2])


def _split_for_reducers(name, g, dtype):
    if name in COL_SHARDED:
        K, N4 = g.shape
        return jnp.transpose(g.reshape(2, K // 2, 4, N4 // 4), (2, 0, 1, 3)).astype(dtype)
    return g.reshape(4, 2, g.shape[0] // 8, g.shape[1]).astype(dtype)


def _local_step(x, tab_m, tab_r, tgt, wts, late_shards, small):
    w_in = wts["w_in"]
    seg = [w_in[:, IN_OFFS[i]:IN_OFFS[i + 1]] for i in range(8)]
    kr_w = jnp.pad(seg[2], ((0, 0), (MLA_NOPE, LANES - MLA_QK)))
    w_in_p = jnp.concatenate([seg[7], seg[5], seg[6], seg[3], seg[4], seg[0], seg[1], kr_w], axis=1)
    QR0, KR0, CQ0 = 4096, 4608, 5120
    w_qb_p = _pad_heads(wts["w_q_b"], MLA_QK)
    kvw = wts["w_kv_b"].reshape(MLA_KV_RANK, HEADS, MLA_NOPE + MLA_V)
    pad_kv = lambda t: jnp.pad(t, ((0, 0), (0, 0), (0, LANES - t.shape[2]))).reshape(MLA_KV_RANK, HEADS * LANES)
    w_kn_p, w_v = pad_kv(kvw[:, :, :MLA_NOPE]), kvw[:, :, MLA_NOPE:].reshape(MLA_KV_RANK, HEADS * MLA_V)
    w_kv_p = jnp.concatenate([w_kn_p, w_v], axis=1)
    g_qn_p = jnp.pad(small["g_qn"], ((0, 0), (0, LANES - MLA_QK)))
    g_kn_p = jnp.pad(small["g_kn"], ((0, 0), (0, LANES - MLA_QK)))
    dec_f = jnp.broadcast_to(small["ret_decay_fwd"].reshape(HEADS, 1), (HEADS, LANES))
    dec_b = jnp.broadcast_to(small["ret_decay_bwd"].reshape(HEADS, 1), (HEADS, LANES))
    T, N = True, False
    RT, HT = ROW_TILE, HEAD_ROW_TILE
    RW = 2 * ROW_TILE

    aux_m = [(t, LANES, 0, N) for t in tab_m]
    aux_r = [(t, LANES, 0, N) for t in tab_r]

    proj, h = _mm("proj", x, w_in_p, "nn", BF16, a_gain=small["g_mix"])
    rows_a = [(proj, MLA_Q_RANK, CQ0 // MLA_Q_RANK, N), (proj, MLA_KV_RANK, (CQ0 + MLA_Q_RANK) // MLA_KV_RANK, N)]
    cqn, ckvn = _rowwise("mla_lat_norm", _f_mla_a, [small["g_q_a"], small["g_kv_a"]], rows_a, [],
                         [(MLA_Q_RANK, BF16, N), (MLA_KV_RANK, BF16, N)], RW)
    qraw = _mm("mla_q_up", cqn, w_qb_p, "nn", BF16)
    kv = _mm("mla_kv_up", ckvn, w_kv_p, "nn", BF16)
    rows_b = [(qraw, LANES, 0, T), (kv, LANES, 0, T), (proj, LANES, (CQ0 + MLA_Q_RANK + MLA_KV_RANK) // LANES, N)]
    q, k = _rowwise("mla_qk_norm_rope", _f_mla_b, [g_qn_p, g_kn_p], rows_b, aux_m, [(LANES, BF16, T)] * 2, HT, HEADS)
    o, lse, late = _flash_fwd(q, k, kv, [late_shards[n] for n in LATE])
    wl = {n: _assemble(n, g) for n, g in zip(LATE, late)}
    w_mla_p = wl["w_mla_out"]
    w_ret_out, w_out, w_gu, w_down = wl["w_ret_out"], wl["w_out"], wl["w_gate_up"], wl["w_down"]
    y_a = _mm("mla_out", o, w_mla_p, "nn", BF16)
    rows_rr = [(proj, LANES, QR0 // LANES, T), (proj, LANES, KR0 // LANES, T)]
    qr, kr = _rowwise("ret_rope", _f_ret_rope, [], rows_rr, aux_r, [(LANES, RET_QK_DTYPE, T)] * 2, HT, HEADS // 2)
    ret_f, ret_b, st_f, st_b = _ret_fwd(qr, kr, proj, 2, dec_f, dec_b)
    rows_rp = [(ret_f, LANES, 0, T), (ret_b, LANES, 0, T), (proj, LANES, 24, T)]
    (o_b,) = _rowwise("ret_post", _f_ret_post, [], rows_rp, [], [(LANES, BF16, T)], HT, HEADS)
    y_b = _mm("ret_out", o_b, w_ret_out, "nn", BF16)
    rows_m = [(proj, D_MODEL, 0, N), (proj, D_MODEL, 1, N), (y_a, D_MODEL, 0, N), (y_b, D_MODEL, 0, N)]
    (merged,) = _rowwise("merge", _f_merge, [], rows_m, [], [(D_MODEL, BF16, N)], RW)
    x2 = _mm("mix_out", merged, w_out, "nn", F32, res=x)
    gu, h2 = _mm("ffn_gate_up", x2, w_gu, "nn", BF16, a_gain=small["g_ffn"])
    rows_sw = [(gu, FFN_HIDDEN, 0, N), (gu, FFN_HIDDEN, 1, N)]
    (act,) = _rowwise("swiglu", _f_swiglu, [], rows_sw, [], [(FFN_HIDDEN, BF16, N)], RW)
    dy, dy_b16, loss_row = _mm("ffn_down", act, w_down, "nn", None,
                               epilogue=(_epi_loss, [x2, tgt], [], [F32, BF16], [(1, LANES)]))

    dact = _mm("d_act", dy_b16, w_down, "nt", BF16)
    dw_down = _mm("dw_down", act, dy_b16, "tn", BF16)
    (dgu,), _ = _rowwise_vjp("swiglu_bwd", _f_swiglu, [], rows_sw, [], [[(dact, FFN_HIDDEN, 0, N)]], [([0, 1], BF16)], RT)
    dx2, dx2_b16, dg_ffn = _mm("d_h2", dgu, w_gu, "nt", None,
                               epilogue=(_epi_rms_bwd(2), [x2, dy], [small["g_ffn"]], [F32, BF16], [(1, D_MODEL)]))
    dw_gu = _mm("dw_gate_up", h2, dgu, "tn", BF16, shard_out=True)
    dmerged = _mm("d_merged", dx2_b16, w_out, "nt", BF16)
    dw_out = _mm("dw_out", merged, dx2_b16, "tn", BF16)
    PW = w_in_p.shape[1]
    (dproj, dy_a, dy_b), _ = _rowwise_vjp("merge_bwd", _f_merge, [], rows_m, [], [[(dmerged, D_MODEL, 0, N)]],
                                          [([0, 1], BF16), ([2], BF16), ([3], BF16)], RW, into=(0, None, PW, 0))
    do_b = _mm("d_ret_o", dy_b, w_ret_out, "nt", BF16)
    dw_ret_out = _mm("dw_ret_out", o_b, dy_b, "tn", BF16)
    (dret, dproj), _ = _rowwise_vjp("ret_post_bwd", _f_ret_post, [], rows_rp, [], [[(do_b, LANES, 0, T)]],
                                    [([0], BF16), ([2], BF16)], HT, HEADS, into=(1, dproj, PW, 24))
    dqf, dkf, dvf, dqb, dkb, dvb, ddec_f, ddec_b = _ret_bwd(qr, kr, proj, 2, dret, st_f, st_b, dec_f, dec_b)
    (dproj,), _ = _rowwise_vjp("ret_rope_q_bwd", _f_ret_rope_q, [], rows_rr[:1], aux_r,
                               [[(dqf, LANES, 0, T), (dqb, LANES, 0, T)]], [([0], BF16)], HT, HEADS // 2,
                               into=(0, dproj, PW, QR0 // LANES))
    (dproj,), _ = _rowwise_vjp("ret_rope_k_bwd", _f_ret_rope_k, [], rows_rr[1:], aux_r,
                               [[(dkf, LANES, 0, T), (dkb, LANES, 0, T)]], [([0], BF16)], HT, HEADS // 2,
                               into=(0, dproj, PW, KR0 // LANES))
    (dproj,) = _rowwise("ret_dv_sum", _f_add, [], [(dvf, D_MODEL, 0, N), (dvb, D_MODEL, 0, N)], [], [(D_MODEL, BF16, N)], RW,
                        into=(0, dproj, PW, 2))
    do = _mm("d_mla_o", dy_a, w_mla_p, "nt", BF16)
    dw_mla = _mm("dw_mla_out", o, dy_a, "tn", BF16)
    late_grads = {"w_mla_out": dw_mla, "w_ret_out": dw_ret_out, "w_out": dw_out, "w_down": dw_down}
    late_gs = [dw_gu if n == "w_gate_up" else _split_for_reducers(n, late_grads[n], BF16) for n in LATE]
    dq, dk, dv, late_got = _flash_bwd(q, k, kv, do, lse, o, late_gs)
    (dqraw, dkn, dkr), (dg_qn_p, dg_kn_p) = _rowwise_vjp(
        "mla_qk_norm_rope_bwd", _f_mla_b, [g_qn_p, g_kn_p], rows_b, aux_m, [[(dq, LANES, 0, T)], [(dk, LANES, 0, T)]],
        [([0], BF16), ([1], BF16), ([2], F32)], HT, HEADS)
    dckvn = _mm("d_ckvn_v", dv, w_v, "nt", BF16, res=_mm("d_ckvn_k", dkn, w_kn_p, "nt", F32))
    dw_kn_p = _mm("dw_kv_k", ckvn, dkn, "tn", BF16)
    dw_v = _mm("dw_kv_v", ckvn, dv, "tn", BF16)
    dcqn = _mm("d_cqn", dqraw, w_qb_p, "nt", BF16)
    dw_qb_p = _mm("dw_q_b", cqn, dqraw, "tn", BF16)
    (dcq, dckv), (dg_q_a, dg_kv_a) = _rowwise_vjp(
        "mla_lat_norm_bwd", _f_mla_a, [small["g_q_a"], small["g_kv_a"]], rows_a, [],
        [[(dcqn, MLA_Q_RANK, 0, N)], [(dckvn, MLA_KV_RANK, 0, N)]], [([0], BF16), ([1], BF16)], RW)
    dproj = lax.dynamic_update_slice(dproj, jnp.concatenate([dcq, dckv, dkr.astype(BF16)], axis=1), (0, CQ0))
    dw_in_p = _mm("dw_in", h, dproj, "tn", BF16)

    c = lambda a, b_: dw_in_p[:, a:b_]
    kr0 = CQ0 + MLA_Q_RANK + MLA_KV_RANK
    dw_in = jnp.concatenate([c(CQ0, CQ0 + MLA_Q_RANK), c(CQ0 + MLA_Q_RANK, kr0), c(kr0 + MLA_NOPE, kr0 + MLA_QK), c(QR0, KR0),
                             c(KR0, CQ0), c(2048, 3072), c(3072, 4096), c(0, 2048)], axis=1)
    dw_kn = dw_kn_p.reshape(MLA_KV_RANK, HEADS, LANES)[:, :, :MLA_NOPE]
    dw_kv = jnp.concatenate([dw_kn, dw_v.reshape(MLA_KV_RANK, HEADS, MLA_V)], axis=2).reshape(MLA_KV_RANK, HEADS * (MLA_NOPE + MLA_V))
    grads = {"w_in": dw_in, "w_q_b": _unpad_heads(dw_qb_p, MLA_QK), "w_kv_b": dw_kv}
    dx, dg_mix, first_got = _mm("d_h", dproj, w_in_p, "nt", None,
                                epilogue=(_epi_rms_bwd(1), [x, dx2], [small["g_mix"]], [F32], [(1, D_MODEL)]),
                                scatter=[_split_for_reducers(n, grads[n], BF16) for n in FIRST])
    sgrads = {"g_mix": dg_mix, "g_q_a": dg_q_a, "g_kv_a": dg_kv_a, "g_qn": dg_qn_p[:, :MLA_QK], "g_kn": dg_kn_p[:, :MLA_QK],
              "ret_decay_fwd": ddec_f[:, 0].reshape(1, HEADS), "ret_decay_bwd": ddec_b[:, 0].reshape(1, HEADS), "g_ffn": dg_ffn}
    return loss_row, dx, first_got + late_got, sgrads


def _coords():
    return lax.axis_index("x"), lax.axis_index("y"), lax.axis_index("c")


def _other_chips(x, y):
    return [(1 - x, y), (x, 1 - y), (1 - x, 1 - y)]


ANY = pl.BlockSpec(memory_space=pl.ANY)


def _gather_copies(ins, outs, send_sems, recv_sems):
    x, y, c = _coords()
    mine = 2 * x + y
    sends, arrivals = [], []
    for w in range(len(ins)):
        for j, (cx, cy) in enumerate(_other_chips(x, y)):
            sems = dict(send_sem=send_sems.at[3 * w + j], recv_sem=recv_sems.at[3 * w + j],
                        device_id=(cx, cy, c), device_id_type=MESH)
            sends.append(pltpu.make_async_remote_copy(src_ref=ins[w], dst_ref=outs[w].at[mine], **sems))
            arrivals.append(functools.partial(pltpu.make_async_remote_copy, src_ref=ins[w],
                                              dst_ref=outs[w].at[2 * cx + cy], **sems))
    return sends, arrivals


def _gather_start(copies):
    for cp in list(copies[0]) + list(copies[2] if len(copies) > 2 else []):
        cp.start()


def _gather_wait(copies):
    sends, arrivals = copies[:2]
    for make in arrivals:
        make().wait_recv()
    for cp in sends:
        cp.wait_send()
    for cp in (copies[2] if len(copies) > 2 else []):
        cp.wait()


def _fill_slot(buf, piece, slot):
    idx = lax.broadcasted_iota(jnp.int32, (buf.shape[0],) + (1,) * piece.ndim, 0)
    return jnp.where(idx == slot, piece[None], buf)


def _rope_tables_and_first_gather(pos, consts_mla, consts_ret, shards):
    S = pos.shape[0]
    tm = min(HEAD_ROW_TILE, S)
    nt = S // tm
    n = len(shards)

    def body(pos_ref, *refs):
        consts, ins = (refs[:4], refs[4:8]), refs[8:8 + n]
        tabs, outs = refs[8 + n:14 + n], refs[14 + n:14 + 2 * n]
        send_sems, recv_sems = refs[14 + 2 * n:]
        i = pl.program_id(0)
        x, y, c = _coords()
        chips = _other_chips(x, y)
        mine = 2 * x + y

        def half(ref, slot, core):
            rows = ref.shape[1] // 2
            return ref.at[slot, pl.ds(pl.multiple_of(core * rows, 8), rows)]

        def copy(w, k, slot, core, to, src=None):
            return pltpu.make_async_remote_copy(
                src_ref=half(outs[w], slot, core) if src is None else src, dst_ref=half(outs[w], slot, core),
                send_sem=send_sems.at[6 * w + k], recv_sem=recv_sems.at[6 * w + k], device_id=to, device_id_type=MESH)

        def first(w, j):
            rows = ins[w].shape[0] // 2
            return copy(w, j, mine, c, (*chips[j], c), src=ins[w].at[pl.ds(pl.multiple_of(c * rows, 8), rows)])

        @pl.when(i == 0)
        def _():
            for w in range(n):
                for j in range(3):
                    first(w, j).start()

        for k in range(2):
            vals = _f_rope_table([r[...] for r in consts[k]], None, [pos_ref[...]])
            for t_ref, v in zip(tabs[3 * k:3 * k + 3], vals):
                t_ref[...] = v

        @pl.when(i == nt - 1)
        def _():
            passed = []
            for w in range(n):
                for j, (cx, cy) in enumerate(chips):
                    copy(w, j, 2 * cx + cy, c, (x, y, c)).wait_recv()
                    cp = copy(w, 3 + j, 2 * cx + cy, c, (x, y, 1 - c))
                    cp.start()
                    passed.append(cp)
            for w in range(n):
                for j, (cx, cy) in enumerate(chips):
                    copy(w, 3 + j, 2 * cx + cy, 1 - c, (x, y, c)).wait_recv()
            for w in range(n):
                for j in range(3):
                    first(w, j).wait_send()
            for cp in passed:
                cp.wait_send()

    const = lambda p: pl.BlockSpec(p.shape, lambda i: (0, 0))
    tab = pl.BlockSpec((tm, LANES), lambda i: (i, 0))
    res = pl.pallas_call(
        body, name="rope_tables_first_gather", grid=(nt,),
        in_specs=[pl.BlockSpec((tm, 1), lambda i: (i, 0))] + [const(p) for p in list(consts_mla) + list(consts_ret)] + [ANY] * n,
        out_specs=[tab] * 6 + [ANY] * n,
        out_shape=[jax.ShapeDtypeStruct((S, LANES), F32)] * 6 + [jax.ShapeDtypeStruct((4,) + s.shape, s.dtype) for s in shards],
        scratch_shapes=[pltpu.SemaphoreType.DMA((6 * n,)), pltpu.SemaphoreType.DMA((6 * n,))],
        compiler_params=_params(),
    )(pos, *consts_mla, *consts_ret, *shards)
    return list(res[:3]), list(res[3:6]), list(res[6:])


def _scatter_copies(ins, outs, send_sems, recv_sems):
    x, y, c = _coords()
    me = 4 * x + 2 * y + c
    n = len(ins)
    sends, arrivals = [], []
    local = [pltpu.make_async_copy(ins[w].at[2 * x + y, c], outs[w].at[me], send_sems.at[7 * n + w]) for w in range(n)]
    for w in range(n):
        for k in range(1, 8):
            px, py, pc = x ^ (k >> 2), y ^ ((k >> 1) & 1), c ^ (k & 1)
            sems = dict(send_sem=send_sems.at[7 * w + k - 1], recv_sem=recv_sems.at[7 * w + k - 1],
                        device_id=(px, py, pc), device_id_type=MESH)
            sends.append(pltpu.make_async_remote_copy(src_ref=ins[w].at[2 * px + py, pc], dst_ref=outs[w].at[me], **sems))
            arrivals.append(functools.partial(
                pltpu.make_async_remote_copy, src_ref=ins[w].at[2 * px + py, pc],
                dst_ref=outs[w].at[4 * px + 2 * py + pc], **sems))
    return sends, arrivals, local


_scatter_start, _scatter_wait = _gather_start, _gather_wait


def _grad_sum8(name, got):
    _, R, W = got.shape
    tr = _pick(R, 256, 16)

    def body(g_ref, o_ref):
        total = g_ref[0].astype(F32)
        for d in range(1, 8):
            total = total + g_ref[d].astype(F32)
        o_ref[...] = total

    return pl.pallas_call(
        body, name=name, grid=(R // tr,), in_specs=[pl.BlockSpec((8, tr, W), lambda i: (0, i, 0))],
        out_specs=pl.BlockSpec((tr, W), lambda i: (i, 0)), out_shape=jax.ShapeDtypeStruct((R, W), F32),
        compiler_params=_params(),
    )(got)


def _half_exchange(halves):
    n = len(halves)

    def body(*refs):
        ins, outs, send_sems, recv_sems = refs[:n], refs[n:2 * n], refs[2 * n], refs[2 * n + 1]
        x, y, c = _coords()
        sends = []
        for w in range(n):
            cp = pltpu.make_async_remote_copy(
                src_ref=ins[w], dst_ref=outs[w], send_sem=send_sems.at[w], recv_sem=recv_sems.at[w],
                device_id=(x, y, 1 - c), device_id_type=MESH)
            cp.start()
            sends.append(cp)
        for cp in sends:
            cp.wait()

    got = pl.pallas_call(
        body, name="grad_half_exchange", in_specs=[ANY] * n, out_specs=[ANY] * n,
        out_shape=[jax.ShapeDtypeStruct(h.shape, F32) for h in halves],
        scratch_shapes=[pltpu.SemaphoreType.DMA((n,)), pltpu.SemaphoreType.DMA((n,))],
    )(*halves)
    c = lax.axis_index("c")
    return [jnp.where(c == 0, jnp.stack([mine, theirs]), jnp.stack([theirs, mine])) for mine, theirs in zip(halves, got)]


def _adamw_math(w, g, m, v):
    m2 = ADAM_B1 * m + (1.0 - ADAM_B1) * g
    v2 = ADAM_B2 * v + (1.0 - ADAM_B2) * (g * g)
    m_hat = m2 / (1.0 - ADAM_B1 ** ADAM_STEP)
    v_hat = v2 / (1.0 - ADAM_B2 ** ADAM_STEP)
    return -ADAM_LR * (m_hat / (jnp.sqrt(v_hat) + ADAM_EPS) + ADAM_WD * w), m2, v2


def _small_allreduce_adamw(pack_g, pack_w, pack_m, pack_v):
    def body(g_ref, w_ref, m_ref, v_ref, sum_ref, d_ref, m_out, v_out, land, send_sems, recv_sems):
        x, y, c = _coords()
        me = 4 * x + 2 * y + c
        land[me] = g_ref[...]
        sends = []
        for k in range(1, 8):
            peer = (x ^ (k >> 2), y ^ ((k >> 1) & 1), c ^ (k & 1))
            cp = pltpu.make_async_remote_copy(
                src_ref=g_ref, dst_ref=land.at[me], send_sem=send_sems.at[k - 1], recv_sem=recv_sems.at[k - 1],
                device_id=peer, device_id_type=MESH)
            cp.start()
            sends.append((cp, peer))
        for k, (cp, peer) in enumerate(sends):
            pltpu.make_async_remote_copy(
                src_ref=g_ref, dst_ref=land.at[4 * peer[0] + 2 * peer[1] + peer[2]], send_sem=send_sems.at[k],
                recv_sem=recv_sems.at[k], device_id=peer, device_id_type=MESH).wait_recv()
        for cp, _ in sends:
            cp.wait_send()
        total = land[0]
        for d in range(1, 8):
            total = total + land[d]
        sum_ref[...] = total
        d_ref[...], m_out[...], v_out[...] = _adamw_math(w_ref[...], total, m_ref[...], v_ref[...])

    vm = pl.BlockSpec(memory_space=pltpu.VMEM)
    shp = jax.ShapeDtypeStruct(pack_g.shape, F32)
    return pl.pallas_call(
        body, name="small_allreduce_adamw", in_specs=[vm] * 4, out_specs=[vm] * 4, out_shape=[shp] * 4,
        scratch_shapes=[pltpu.VMEM((8,) + pack_g.shape, F32), pltpu.SemaphoreType.DMA((7,)), pltpu.SemaphoreType.DMA((7,))],
    )(pack_g, pack_w, pack_m, pack_v)


def _adamw(name, w, g, m, v):
    R, C = w.shape
    tr = _pick(R, 256, 8)

    def body(w_ref, g_ref, m_ref, v_ref, d_out, m_out, v_out):
        d_out[...], m_out[...], v_out[...] = _adamw_math(w_ref[...], g_ref[...], m_ref[...], v_ref[...])

    spec = pl.BlockSpec((tr, C), lambda i: (i, 0))
    return pl.pallas_call(
        body, name=name, grid=(R // tr,), in_specs=[spec] * 4, out_specs=[spec] * 3,
        out_shape=[jax.ShapeDtypeStruct((R, C), F32)] * 3, compiler_params=_params(),
    )(w, g, m, v)


def _pack_small(vals, last):
    flat = jnp.concatenate([v.reshape(-1) for v in vals] + [last.reshape(-1)])
    return jnp.pad(flat, (0, SMALL_ROWS * LANES - flat.shape[0])).reshape(SMALL_ROWS, LANES)


def kernel(x, positions, g_mix, w_in, g_q_a, w_q_b, g_kv_a, w_kv_b, g_qn, g_kn, w_mla_out, ret_decay_fwd, ret_decay_bwd, w_ret_out, w_out, g_ffn, w_gate_up, w_down, loss_target, m_g_mix, m_w_in, m_g_q_a, m_w_q_b, m_g_kv_a, m_w_kv_b, m_g_qn, m_g_kn, m_w_mla_out, m_ret_decay_fwd, m_ret_decay_bwd, m_w_ret_out, m_w_out, m_g_ffn, m_w_gate_up, m_w_down, v_g_mix, v_w_in, v_g_q_a, v_w_q_b, v_g_kv_a, v_w_kv_b, v_g_qn, v_g_kn, v_w_mla_out, v_ret_decay_fwd, v_ret_decay_bwd, v_w_ret_out, v_w_out, v_g_ffn, v_w_gate_up, v_w_down):
    given = dict(locals())
    S = x.shape[1]
    xs, tgt = x.reshape(S, D_MODEL), loss_target.reshape(S, D_MODEL)
    pos = positions.reshape(S, 1).astype(F32)

    first_shards = [given[n].astype(BF16) for n in FIRST]
    my_chip = 2 * lax.axis_index("x") + lax.axis_index("y")
    tab_m, tab_r, gathered = _rope_tables_and_first_gather(
        pos, _rope_consts(MLA_NOPE, MLA_ROPE // 2), _rope_consts(0, RET_QK // 2, RET_QK), first_shards)
    wts = {n: _assemble(n, _fill_slot(g, s, my_chip)) for n, g, s in zip(FIRST, gathered, first_shards)}
    late_shards = {n: given[n].astype(BF16) for n in LATE}
    small = {n: given[n].reshape(1, -1) for n in SMALL}

    loss_row, dx, pieces, sgrads = _local_step(xs, tab_m, tab_r, tgt, wts, late_shards, small)

    halves = [_grad_sum8("grad_sum_" + n, got) for n, got in zip(FIRST + LATE, pieces)]
    reduced = _half_exchange(halves)

    out = {}
    for n, r in zip(FIRST + LATE, reduced):
        g = r.reshape(given[n].shape)
        out["grad_" + n] = g
        out["delta_" + n], out["new_m_" + n], out["new_v_" + n] = _adamw("adamw_" + n, given[n], g, given["m_" + n], given["v_" + n])

    one = jnp.ones((1,), F32)
    pk = _small_allreduce_adamw(
        _pack_small([sgrads[n] for n in SMALL], loss_row[0, :1]),
        _pack_small([given[n] for n in SMALL], 0 * one),
        _pack_small([given["m_" + n] for n in SMALL], 0 * one),
        _pack_small([given["v_" + n] for n in SMALL], one))
    off = 0
    for n in SMALL:
        sz = given[n].shape[0]
        for pre, arr in zip(["grad_", "delta_", "new_m_", "new_v_"], pk):
            out[pre + n] = arr.reshape(-1)[off:off + sz]
        off += sz
    loss = pk[0].reshape(-1)[off]

    return (loss, dx.reshape(x.shape), *[out["grad_" + n] for n in WEIGHTS], *[out["delta_" + n] for n in WEIGHTS],
            *[out["new_m_" + n] for n in WEIGHTS], *[out["new_v_" + n] for n in WEIGHTS])
```
